```python
import math
import jax, jax.numpy as jnp
from jax import lax
import numpy as np

D_MODEL = 1024
BATCH = 32
SEQ = 2048
DEPTH = 4

N_MEM = 256
HEAD_DIM = 64
N_DIL_HEADS = 12
DIL_WIDTH = N_DIL_HEADS * HEAD_DIM
DIL_PATTERNS = ((128, 1), (512, 4), (2048, 16))
BLOCK = 128
N_SGU_GROUPS = 12
SGU_GROUP_DIM = 64
SGU_WIDTH = N_SGU_GROUPS * SGU_GROUP_DIM
CHUNK = 128
N_MEM_HEADS = 4
MEM_WIDTH = N_MEM_HEADS * HEAD_DIM
MIX_WIDTH = DIL_WIDTH + MEM_WIDTH
D_FF = ((-(-8 * D_MODEL // 3) + 255) // 256) * 256
DN_ALPHA = (2 * DEPTH) ** 0.25
DN_BETA = (8 * DEPTH) ** -0.25
N_A = (DEPTH + 1) // 2
N_B = DEPTH // 2
IN_A = 3 * DIL_WIDTH + MEM_WIDTH
IN_B = 2 * SGU_WIDTH + MEM_WIDTH
LN_EPS = 1e-5

kernel_name = "hybrid_dilated_gmlp_memory_deepnorm"


def layer_norm(x, g, b):
    xf = x.astype(jnp.float32)
    mu = xf.mean(-1, keepdims=True)
    var = jnp.square(xf - mu).mean(-1, keepdims=True)
    y = (xf - mu) * lax.rsqrt(var + LN_EPS)
    return (y * g.astype(jnp.float32) + b.astype(jnp.float32)).astype(x.dtype)


def alibi_slopes(n):
    return jnp.exp2(-8.0 * (jnp.arange(n, dtype=jnp.float32) + 1.0) / n)


def band_pattern(q, k, v, slopes, window, dil):
    B, S, H, Dh = q.shape
    steps_max = window // dil
    L = S // dil
    n_blk = -(-L // BLOCK)
    Lp = n_blk * BLOCK
    def sub(t):
        return t.reshape(B, L, dil, H, Dh)
    qs = jnp.pad(sub(q), ((0, 0), (0, Lp - L), (0, 0), (0, 0), (0, 0)))
    qs = qs.reshape(B, n_blk, BLOCK, dil, H, Dh)
    def banded_keys(t):
        tp = jnp.pad(sub(t), ((0, 0), (BLOCK, Lp - L), (0, 0), (0, 0), (0, 0)))
        prev = tp[:, :Lp].reshape(B, n_blk, BLOCK, dil, H, Dh)
        cur = tp[:, BLOCK:].reshape(B, n_blk, BLOCK, dil, H, Dh)
        return jnp.concatenate([prev, cur], axis=2)
    kb = banded_keys(k)
    vb = banded_keys(v)
    s = jnp.einsum('bnqrhd,bnkrhd->bnrhqk', qs, kb).astype(jnp.float32) * (Dh ** -0.5)
    qi = jnp.arange(BLOCK)[:, None]
    ki = jnp.arange(2 * BLOCK)[None, :]
    steps = qi + BLOCK - ki
    key_idx = (jnp.arange(n_blk) * BLOCK)[:, None, None] - BLOCK + ki[None]
    valid = (steps >= 0) & (steps <= steps_max) & (key_idx >= 0)
    bias = -slopes[:, None, None] * (steps * dil).astype(jnp.float32)[None]
    s = s + bias[None, None, None]
    s = jnp.where(valid[None, :, None, None], s, -jnp.inf)
    m = s.max(-1, keepdims=True)
    p = jnp.exp(s - m)
    den = p.sum(-1)
    lse = m[..., 0] + jnp.log(den)
    o = jnp.einsum('bnrhqk,bnkrhd->bnqrhd', p.astype(v.dtype), vb)
    o = o / jnp.moveaxis(den, -1, 2)[..., None].astype(o.dtype)
    o = o.reshape(B, Lp, dil, H, Dh)[:, :L].reshape(B, S, H, Dh)
    lse = jnp.moveaxis(lse, -1, 2).reshape(B, Lp, dil, H)[:, :L].reshape(B, S, H)
    return o, lse


def dilated_attention(q, k, v):
    slopes = alibi_slopes(q.shape[2])
    outs, lses = [], []
    for window, dil in DIL_PATTERNS:
        o, lse = band_pattern(q, k, v, slopes, window, dil)
        outs.append(o)
        lses.append(lse)
    w = jax.nn.softmax(jnp.stack(lses, 0), axis=0).astype(q.dtype)
    return jnp.einsum('pbsh,pbshd->bshd', w, jnp.stack(outs, 0))


def causal_chunk_sgu(u, v, ln_g, ln_b, w_s, b_s):
    B, S, _ = v.shape
    v = layer_norm(v, ln_g, ln_b)
    vc = v.reshape(B, S // CHUNK, CHUNK, N_SGU_GROUPS, SGU_GROUP_DIM)
    ws = w_s * jnp.tril(jnp.ones((CHUNK, CHUNK), w_s.dtype))
    mixed = jnp.einsum('gts,bnsgc->bntgc', ws, vc) + b_s.T[None, None, :, :, None]
    return u * mixed.reshape(B, S, SGU_WIDTH)


def memory_attention(qm, mk, mv):
    s = jnp.einsum('bshd,bmhd->bhsm', qm, mk).astype(jnp.float32) * (HEAD_DIM ** -0.5)
    p = jax.nn.softmax(s, axis=-1).astype(mv.dtype)
    return jnp.einsum('bhsm,bmhd->bshd', p, mv)


def _fwd_setup_inputs(seed: int = 0) -> dict:
    key = jax.random.key(seed)
    ks = jax.random.split(key, 18)
    f32 = jnp.float32
    D = D_MODEL
    def nrm(k, shape, scale):
        return jax.random.normal(k, shape, f32) * scale
    return {
        "x": nrm(ks[0], (BATCH, SEQ, D), 1.0),
        "mem": nrm(ks[1], (BATCH, N_MEM, D), 1.0),
        "a_w_in": nrm(ks[2], (N_A, D, IN_A), D ** -0.5),
        "b_w_in": nrm(ks[3], (N_B, D, IN_B), D ** -0.5),
        "sgu_ln_g": 1.0 + nrm(ks[4], (N_B, SGU_WIDTH), 0.02),
        "sgu_ln_b": nrm(ks[5], (N_B, SGU_WIDTH), 0.02),
        "sgu_w_s": nrm(ks[6], (N_B, N_SGU_GROUPS, CHUNK, CHUNK), CHUNK ** -0.5),
        "sgu_b_s": 1.0 + nrm(ks[7], (N_B, N_SGU_GROUPS, CHUNK), 0.02),
        "w_mem_kv": nrm(ks[8], (DEPTH, D, 2 * MEM_WIDTH), D ** -0.5),
        "w_out": nrm(ks[9], (DEPTH, MIX_WIDTH, D), DN_BETA * MIX_WIDTH ** -0.5),
        "ln_mix_g": 1.0 + nrm(ks[10], (DEPTH, D), 0.02),
        "ln_mix_b": nrm(ks[11], (DEPTH, D), 0.02),
        "w_gate": nrm(ks[12], (DEPTH, D, D_FF), D ** -0.5),
        "w_up": nrm(ks[13], (DEPTH, D, D_FF), D ** -0.5),
        "w_down": nrm(ks[14], (DEPTH, D_FF, D), DN_BETA * D_FF ** -0.5),
        "ln_ffn_g": 1.0 + nrm(ks[15], (DEPTH, D), 0.02),
        "ln_ffn_b": nrm(ks[16], (DEPTH, D), 0.02),
    }


def _fwd_reference(x, mem, a_w_in, b_w_in, sgu_ln_g, sgu_ln_b, sgu_w_s, sgu_b_s, w_mem_kv, w_out,
              ln_mix_g, ln_mix_b, w_gate, w_up, w_down, ln_ffn_g, ln_ffn_b):
    B, S, _ = x.shape
    for i in range(DEPTH):
        j = i // 2
        mk, mv = jnp.split(mem @ w_mem_kv[i], 2, axis=-1)
        mk = mk.reshape(B, N_MEM, N_MEM_HEADS, HEAD_DIM)
        mv = mv.reshape(B, N_MEM, N_MEM_HEADS, HEAD_DIM)
        if i % 2 == 0:
            h = x @ a_w_in[j]
            q, k, v, qm = jnp.split(h, [DIL_WIDTH, 2 * DIL_WIDTH, 3 * DIL_WIDTH], axis=-1)
            hs = (B, S, N_DIL_HEADS, HEAD_DIM)
            mix = dilated_attention(q.reshape(hs), k.reshape(hs), v.reshape(hs)).reshape(B, S, DIL_WIDTH)
        else:
            h = x @ b_w_in[j]
            u, v, qm = jnp.split(h, [SGU_WIDTH, 2 * SGU_WIDTH], axis=-1)
            mix = causal_chunk_sgu(jax.nn.gelu(u), jax.nn.gelu(v), sgu_ln_g[j], sgu_ln_b[j],
                                   sgu_w_s[j], sgu_b_s[j])
        mo = memory_attention(qm.reshape(B, S, N_MEM_HEADS, HEAD_DIM), mk, mv).reshape(B, S, MEM_WIDTH)
        y = jnp.concatenate([mix, mo], axis=-1) @ w_out[i]
        x = layer_norm(DN_ALPHA * x + y, ln_mix_g[i], ln_mix_b[i])
        f = (jax.nn.silu(x @ w_gate[i]) * (x @ w_up[i])) @ w_down[i]
        x = layer_norm(DN_ALPHA * x + f, ln_ffn_g[i], ln_ffn_b[i])
    return x


import jax as _jax
import jax.numpy as _jnp

TWIN_FORMAT = 'train_step'
FWD_PARAMS = ['x', 'mem', 'a_w_in', 'b_w_in', 'sgu_ln_g', 'sgu_ln_b', 'sgu_w_s', 'sgu_b_s', 'w_mem_kv', 'w_out', 'ln_mix_g', 'ln_mix_b', 'w_gate', 'w_up', 'w_down', 'ln_ffn_g', 'ln_ffn_b']
TWIN_WEIGHTS = ['a_w_in', 'b_w_in', 'sgu_ln_g', 'sgu_ln_b', 'sgu_w_s', 'sgu_b_s', 'w_mem_kv', 'w_out', 'ln_mix_g', 'ln_mix_b', 'w_gate', 'w_up', 'w_down', 'ln_ffn_g', 'ln_ffn_b']
TWIN_DIFF_INPUT = 'x'
TWIN_INPUTS = ['x', 'mem', 'a_w_in', 'b_w_in', 'sgu_ln_g', 'sgu_ln_b', 'sgu_w_s', 'sgu_b_s', 'w_mem_kv', 'w_out', 'ln_mix_g', 'ln_mix_b', 'w_gate', 'w_up', 'w_down', 'ln_ffn_g', 'ln_ffn_b', 'loss_target', 'm_a_w_in', 'm_b_w_in', 'm_sgu_ln_g', 'm_sgu_ln_b', 'm_sgu_w_s', 'm_sgu_b_s', 'm_w_mem_kv', 'm_w_out', 'm_ln_mix_g', 'm_ln_mix_b', 'm_w_gate', 'm_w_up', 'm_w_down', 'm_ln_ffn_g', 'm_ln_ffn_b', 'v_a_w_in', 'v_b_w_in', 'v_sgu_ln_g', 'v_sgu_ln_b', 'v_sgu_w_s', 'v_sgu_b_s', 'v_w_mem_kv', 'v_w_out', 'v_ln_mix_g', 'v_ln_mix_b', 'v_w_gate', 'v_w_up', 'v_w_down', 'v_ln_ffn_g', 'v_ln_ffn_b']
TWIN_OUTPUTS = ['loss', 'grad_x', 'grad_a_w_in', 'grad_b_w_in', 'grad_sgu_ln_g', 'grad_sgu_ln_b', 'grad_sgu_w_s', 'grad_sgu_b_s', 'grad_w_mem_kv', 'grad_w_out', 'grad_ln_mix_g', 'grad_ln_mix_b', 'grad_w_gate', 'grad_w_up', 'grad_w_down', 'grad_ln_ffn_g', 'grad_ln_ffn_b', 'delta_a_w_in', 'delta_b_w_in', 'delta_sgu_ln_g', 'delta_sgu_ln_b', 'delta_sgu_w_s', 'delta_sgu_b_s', 'delta_w_mem_kv', 'delta_w_out', 'delta_ln_mix_g', 'delta_ln_mix_b', 'delta_w_gate', 'delta_w_up', 'delta_w_down', 'delta_ln_ffn_g', 'delta_ln_ffn_b', 'new_m_a_w_in', 'new_m_b_w_in', 'new_m_sgu_ln_g', 'new_m_sgu_ln_b', 'new_m_sgu_w_s', 'new_m_sgu_b_s', 'new_m_w_mem_kv', 'new_m_w_out', 'new_m_ln_mix_g', 'new_m_ln_mix_b', 'new_m_w_gate', 'new_m_w_up', 'new_m_w_down', 'new_m_ln_ffn_g', 'new_m_ln_ffn_b', 'new_v_a_w_in', 'new_v_b_w_in', 'new_v_sgu_ln_g', 'new_v_sgu_ln_b', 'new_v_sgu_w_s', 'new_v_sgu_b_s', 'new_v_w_mem_kv', 'new_v_w_out', 'new_v_ln_mix_g', 'new_v_ln_mix_b', 'new_v_w_gate', 'new_v_w_up', 'new_v_w_down', 'new_v_ln_ffn_g', 'new_v_ln_ffn_b']
TWIN_LEAF_KINDS = {'loss': 'loss', 'grad_x': 'grad_x', 'grad_a_w_in': 'grad_w', 'grad_b_w_in': 'grad_w', 'grad_sgu_ln_g': 'grad_w', 'grad_sgu_ln_b': 'grad_w', 'grad_sgu_w_s': 'grad_w', 'grad_sgu_b_s': 'grad_w', 'grad_w_mem_kv': 'grad_w', 'grad_w_out': 'grad_w', 'grad_ln_mix_g': 'grad_w', 'grad_ln_mix_b': 'grad_w', 'grad_w_gate': 'grad_w', 'grad_w_up': 'grad_w', 'grad_w_down': 'grad_w', 'grad_ln_ffn_g': 'grad_w', 'grad_ln_ffn_b': 'grad_w', 'delta_a_w_in': 'delta_w', 'delta_b_w_in': 'delta_w', 'delta_sgu_ln_g': 'delta_w', 'delta_sgu_ln_b': 'delta_w', 'delta_sgu_w_s': 'delta_w', 'delta_sgu_b_s': 'delta_w', 'delta_w_mem_kv': 'delta_w', 'delta_w_out': 'delta_w', 'delta_ln_mix_g': 'delta_w', 'delta_ln_mix_b': 'delta_w', 'delta_w_gate': 'delta_w', 'delta_w_up': 'delta_w', 'delta_w_down': 'delta_w', 'delta_ln_ffn_g': 'delta_w', 'delta_ln_ffn_b': 'delta_w', 'new_m_a_w_in': 'new_m', 'new_m_b_w_in': 'new_m', 'new_m_sgu_ln_g': 'new_m', 'new_m_sgu_ln_b': 'new_m', 'new_m_sgu_w_s': 'new_m', 'new_m_sgu_b_s': 'new_m', 'new_m_w_mem_kv': 'new_m', 'new_m_w_out': 'new_m', 'new_m_ln_mix_g': 'new_m', 'new_m_ln_mix_b': 'new_m', 'new_m_w_gate': 'new_m', 'new_m_w_up': 'new_m', 'new_m_w_down': 'new_m', 'new_m_ln_ffn_g': 'new_m', 'new_m_ln_ffn_b': 'new_m', 'new_v_a_w_in': 'new_v', 'new_v_b_w_in': 'new_v', 'new_v_sgu_ln_g': 'new_v', 'new_v_sgu_ln_b': 'new_v', 'new_v_sgu_w_s': 'new_v', 'new_v_sgu_b_s': 'new_v', 'new_v_w_mem_kv': 'new_v', 'new_v_w_out': 'new_v', 'new_v_ln_mix_g': 'new_v', 'new_v_ln_mix_b': 'new_v', 'new_v_w_gate': 'new_v', 'new_v_w_up': 'new_v', 'new_v_w_down': 'new_v', 'new_v_ln_ffn_g': 'new_v', 'new_v_ln_ffn_b': 'new_v'}


def _forward(args):
    return _fwd_reference(*[args[k] for k in FWD_PARAMS])


def _output_shape():
    out = _jax.eval_shape(lambda: _forward(_fwd_setup_inputs(0)))
    return out.shape, out.dtype

N_MICROBATCH = 1
ADAM_LR = 0.001
ADAM_B1 = 0.9
ADAM_B2 = 0.999
ADAM_EPS = 1e-08
ADAM_WD = 0.01
ADAM_STEP = 10
PER_EXAMPLE_BATCH_AXIS = {'x': 0, 'mem': 0, 'loss_target': 0}
SHARED_INPUTS = []
_WEIGHT_DTYPES = {'a_w_in': _jnp.float32, 'b_w_in': _jnp.float32, 'sgu_ln_g': _jnp.float32, 'sgu_ln_b': _jnp.float32, 'sgu_w_s': _jnp.float32, 'sgu_b_s': _jnp.float32, 'w_mem_kv': _jnp.float32, 'w_out': _jnp.float32, 'ln_mix_g': _jnp.float32, 'ln_mix_b': _jnp.float32, 'w_gate': _jnp.float32, 'w_up': _jnp.float32, 'w_down': _jnp.float32, 'ln_ffn_g': _jnp.float32, 'ln_ffn_b': _jnp.float32}
MOMENT_SCALE = {'a_w_in': 1.993175e-02, 'b_w_in': 4.097782e-02, 'sgu_ln_g': 2.930257e-02, 'sgu_ln_b': 2.916233e-02, 'sgu_w_s': 2.099987e-02, 'sgu_b_s': 2.914034e-02, 'w_mem_kv': 7.333232e-03, 'w_out': 9.905202e-02, 'ln_mix_g': 1.673984e+00, 'ln_mix_b': 7.679635e-01, 'w_gate': 2.360074e-02, 'w_up': 2.285556e-02, 'w_down': 9.038386e-02, 'ln_ffn_g': 3.212289e+01, 'ln_ffn_b': 2.910492e+00}


def _to_microbatches(a, axis):
    t = _jnp.moveaxis(a, axis, 0)
    t = t.reshape((N_MICROBATCH, t.shape[0] // N_MICROBATCH) + t.shape[1:])
    return _jnp.moveaxis(t, 1, axis + 1)


def setup_inputs(seed: int = 0) -> dict:
    inp = _fwd_setup_inputs(seed)
    key = _jax.random.fold_in(_jax.random.key(seed), 7919)
    shape, _ = _output_shape()
    out = dict(inp)
    out["loss_target"] = _jax.random.normal(_jax.random.fold_in(key, 0), shape, _jnp.float32)
    for i, name in enumerate(TWIN_WEIGHTS):
        w = inp[name].astype(_jnp.float32)
        if MOMENT_SCALE is None:
            s = _jnp.sqrt(_jnp.mean(_jnp.square(w)) + 1e-30)
        else:
            s = MOMENT_SCALE[name]
        km, kv = _jax.random.split(_jax.random.fold_in(key, i + 1))
        out[name] = w
        out["m_" + name] = s * _jax.random.normal(km, w.shape, _jnp.float32)
        out["v_" + name] = (s * s) * _jax.random.uniform(kv, w.shape, _jnp.float32, 0.5, 1.5)
    if N_MICROBATCH > 1:
        for name, axis in PER_EXAMPLE_BATCH_AXIS.items():
            out[name] = _to_microbatches(out[name], axis)
    return {'x': out['x'], 'mem': out['mem'], 'a_w_in': out['a_w_in'], 'b_w_in': out['b_w_in'], 'sgu_ln_g': out['sgu_ln_g'], 'sgu_ln_b': out['sgu_ln_b'], 'sgu_w_s': out['sgu_w_s'], 'sgu_b_s': out['sgu_b_s'], 'w_mem_kv': out['w_mem_kv'], 'w_out': out['w_out'], 'ln_mix_g': out['ln_mix_g'], 'ln_mix_b': out['ln_mix_b'], 'w_gate': out['w_gate'], 'w_up': out['w_up'], 'w_down': out['w_down'], 'ln_ffn_g': out['ln_ffn_g'], 'ln_ffn_b': out['ln_ffn_b'], 'loss_target': out['loss_target'], 'm_a_w_in': out['m_a_w_in'], 'm_b_w_in': out['m_b_w_in'], 'm_sgu_ln_g': out['m_sgu_ln_g'], 'm_sgu_ln_b': out['m_sgu_ln_b'], 'm_sgu_w_s': out['m_sgu_w_s'], 'm_sgu_b_s': out['m_sgu_b_s'], 'm_w_mem_kv': out['m_w_mem_kv'], 'm_w_out': out['m_w_out'], 'm_ln_mix_g': out['m_ln_mix_g'], 'm_ln_mix_b': out['m_ln_mix_b'], 'm_w_gate': out['m_w_gate'], 'm_w_up': out['m_w_up'], 'm_w_down': out['m_w_down'], 'm_ln_ffn_g': out['m_ln_ffn_g'], 'm_ln_ffn_b': out['m_ln_ffn_b'], 'v_a_w_in': out['v_a_w_in'], 'v_b_w_in': out['v_b_w_in'], 'v_sgu_ln_g': out['v_sgu_ln_g'], 'v_sgu_ln_b': out['v_sgu_ln_b'], 'v_sgu_w_s': out['v_sgu_w_s'], 'v_sgu_b_s': out['v_sgu_b_s'], 'v_w_mem_kv': out['v_w_mem_kv'], 'v_w_out': out['v_w_out'], 'v_ln_mix_g': out['v_ln_mix_g'], 'v_ln_mix_b': out['v_ln_mix_b'], 'v_w_gate': out['v_w_gate'], 'v_w_up': out['v_w_up'], 'v_w_down': out['v_w_down'], 'v_ln_ffn_g': out['v_ln_ffn_g'], 'v_ln_ffn_b': out['v_ln_ffn_b']}


def _loss(weights, diff, rest, loss_target):
    with _jax.named_scope("forward"):
        args = {**rest, TWIN_DIFF_INPUT: diff, **{k: w.astype(_WEIGHT_DTYPES[k]) for k, w in weights.items()}}
        y = _forward(args)
    with _jax.named_scope("loss_head"):
        err = _jnp.square(y.astype(_jnp.float32) - loss_target)
        return 0.5 * _jnp.sum(_jnp.mean(err, axis=-1)) if err.ndim else 0.5 * err


def _adamw(w, g, m, v):
    m = ADAM_B1 * m + (1.0 - ADAM_B1) * g
    v = ADAM_B2 * v + (1.0 - ADAM_B2) * _jnp.square(g)
    m_hat = m / (1.0 - ADAM_B1 ** ADAM_STEP)
    v_hat = v / (1.0 - ADAM_B2 ** ADAM_STEP)
    delta = -ADAM_LR * (m_hat / (_jnp.sqrt(v_hat) + ADAM_EPS) + ADAM_WD * w)
    return delta, m, v


def reference(x, mem, a_w_in, b_w_in, sgu_ln_g, sgu_ln_b, sgu_w_s, sgu_b_s, w_mem_kv, w_out, ln_mix_g, ln_mix_b, w_gate, w_up, w_down, ln_ffn_g, ln_ffn_b, loss_target, m_a_w_in, m_b_w_in, m_sgu_ln_g, m_sgu_ln_b, m_sgu_w_s, m_sgu_b_s, m_w_mem_kv, m_w_out, m_ln_mix_g, m_ln_mix_b, m_w_gate, m_w_up, m_w_down, m_ln_ffn_g, m_ln_ffn_b, v_a_w_in, v_b_w_in, v_sgu_ln_g, v_sgu_ln_b, v_sgu_w_s, v_sgu_b_s, v_w_mem_kv, v_w_out, v_ln_mix_g, v_ln_mix_b, v_w_gate, v_w_up, v_w_down, v_ln_ffn_g, v_ln_ffn_b):
    given = dict(x=x, mem=mem, a_w_in=a_w_in, b_w_in=b_w_in, sgu_ln_g=sgu_ln_g, sgu_ln_b=sgu_ln_b, sgu_w_s=sgu_w_s, sgu_b_s=sgu_b_s, w_mem_kv=w_mem_kv, w_out=w_out, ln_mix_g=ln_mix_g, ln_mix_b=ln_mix_b, w_gate=w_gate, w_up=w_up, w_down=w_down, ln_ffn_g=ln_ffn_g, ln_ffn_b=ln_ffn_b, loss_target=loss_target, m_a_w_in=m_a_w_in, m_b_w_in=m_b_w_in, m_sgu_ln_g=m_sgu_ln_g, m_sgu_ln_b=m_sgu_ln_b, m_sgu_w_s=m_sgu_w_s, m_sgu_b_s=m_sgu_b_s, m_w_mem_kv=m_w_mem_kv, m_w_out=m_w_out, m_ln_mix_g=m_ln_mix_g, m_ln_mix_b=m_ln_mix_b, m_w_gate=m_w_gate, m_w_up=m_w_up, m_w_down=m_w_down, m_ln_ffn_g=m_ln_ffn_g, m_ln_ffn_b=m_ln_ffn_b, v_a_w_in=v_a_w_in, v_b_w_in=v_b_w_in, v_sgu_ln_g=v_sgu_ln_g, v_sgu_ln_b=v_sgu_ln_b, v_sgu_w_s=v_sgu_w_s, v_sgu_b_s=v_sgu_b_s, v_w_mem_kv=v_w_mem_kv, v_w_out=v_w_out, v_ln_mix_g=v_ln_mix_g, v_ln_mix_b=v_ln_mix_b, v_w_gate=v_w_gate, v_w_up=v_w_up, v_w_down=v_w_down, v_ln_ffn_g=v_ln_ffn_g, v_ln_ffn_b=v_ln_ffn_b)
    weights = {n: given[n] for n in TWIN_WEIGHTS}
    shared = {n: given[n] for n in SHARED_INPUTS}
    per_example = {n: given[n] for n in ['x', 'mem']}
    grad_fn = _jax.value_and_grad(_loss, argnums=(0, 1))

    def one_microbatch(ex, loss_target):
        ex = dict(ex)
        diff = ex.pop(TWIN_DIFF_INPUT)
        return grad_fn(weights, diff, {**shared, **ex}, loss_target)

    if N_MICROBATCH == 1:
        loss, (grad_w, grad_x) = one_microbatch(per_example, given["loss_target"])
    else:
        def body(carry, xs):
            loss_sum, grad_sum = carry
            l_k, (gw_k, gx_k) = one_microbatch(xs[0], xs[1])
            with _jax.named_scope("update"):
                return (loss_sum + l_k, _jax.tree.map(_jnp.add, grad_sum, gw_k)), gx_k

        init = (_jnp.zeros((), _jnp.float32), _jax.tree.map(_jnp.zeros_like, weights))
        (loss, grad_w), grad_x = _jax.lax.scan(body, init, (per_example, given["loss_target"]))
    with _jax.named_scope("update"):
        delta_w, new_m, new_v = {}, {}, {}
        for n in TWIN_WEIGHTS:
            delta_w[n], new_m[n], new_v[n] = _adamw(weights[n], grad_w[n], given["m_" + n], given["v_" + n])
    return (loss, grad_x, *[grad_w[n] for n in TWIN_WEIGHTS], *[delta_w[n] for n in TWIN_WEIGHTS],
            *[new_m[n] for n in TWIN_WEIGHTS], *[new_v[n] for n in TWIN_WEIGHTS])
```

```python
import functools
import math

import jax
import jax.numpy as jnp
from jax import lax
from jax.experimental import pallas as pl
from jax.experimental.pallas import tpu as pltpu

F32 = jnp.float32
BF16 = jnp.bfloat16

DEPTH = 4
HEAD_DIM = 64
N_DIL_HEADS = 12
DIL_WIDTH = N_DIL_HEADS * HEAD_DIM
DIL_PATTERNS = ((128, 1), (512, 4), (2048, 16))
BLOCK = 128
N_SGU_GROUPS = 12
SGU_WIDTH = N_SGU_GROUPS * 64
CHUNK = 128
N_MEM_HEADS = 4
MEM_WIDTH = N_MEM_HEADS * HEAD_DIM
DN_ALPHA = (2 * DEPTH) ** 0.25
LN_EPS = 1e-5
ATT_SCALE = HEAD_DIM ** -0.5
ADAM_LR = 0.001
ADAM_B1 = 0.9
ADAM_B2 = 0.999
ADAM_EPS = 1e-08
ADAM_WD = 0.01
ADAM_STEP = 10
NEG_BIG = -1e30

LANES = 128
FLAT_COLS = 1024
VMEM_LIMIT = 56 * 1024 * 1024
MESH_AXES = ("x", "y", "c")
MESH_ID = pl.DeviceIdType.MESH


def _tile(n, pref, align=LANES):
  if n <= pref:
    return n
  t = (pref // align) * align
  while t >= align:
    if n % t == 0:
      return t
    t -= align
  return n


def _params(sem):
  return pltpu.CompilerParams(dimension_semantics=sem, vmem_limit_bytes=VMEM_LIMIT)


def _dot(a, b):
  return jnp.dot(a, b, preferred_element_type=F32)


def _dot_nt(a, b):
  return lax.dot_general(a, b, (((1,), (1,)), ((), ())), preferred_element_type=F32)


def _dot_tn(a, b):
  return lax.dot_general(a, b, (((0,), (0,)), ((), ())), preferred_element_type=F32)


def _bf(v):
  return v.astype(BF16)


def _ln_stats(z):
  mu = jnp.mean(z, axis=-1, keepdims=True)
  zc = z - mu
  var = jnp.mean(zc * zc, axis=-1, keepdims=True)
  rstd = lax.rsqrt(var + LN_EPS)
  return zc * rstd, rstd


def _ln_bwd(dy, xhat, rstd, g):
  gdy = dy * g
  m1 = jnp.mean(gdy, axis=-1, keepdims=True)
  m2 = jnp.mean(gdy * xhat, axis=-1, keepdims=True)
  return rstd * (gdy - m1 - xhat * m2)


_GELU_C = math.sqrt(2.0 / math.pi)


def _gelu_parts(v):
  v2 = v * v
  t = jnp.tanh(_GELU_C * (v + 0.044715 * v * v2))
  val = 0.5 * v * (1.0 + t)
  der = 0.5 * (1.0 + t) + 0.5 * v * (1.0 - t * t) * (_GELU_C * (1.0 + 3.0 * 0.044715 * v2))
  return val, der


def _gelu(v):
  t = jnp.tanh(_GELU_C * (v + 0.044715 * v * v * v))
  return 0.5 * v * (1.0 + t)


def _sigmoid(v):
  return 1.0 / (1.0 + jnp.exp(-v))


def _mm(a, b, mode, out_dtype, name, add=None, add_scale=1.0, tm=512, tn=512, tk=512):
  if mode == "nn":
    (m, k), (k2, n) = a.shape, b.shape
  elif mode == "nt":
    (m, k), (n, k2) = a.shape, b.shape
  else:
    (k, m), (k2, n) = a.shape, b.shape
  assert k == k2, (a.shape, b.shape, mode)
  tm, tn, tk = _tile(m, tm), _tile(n, tn), _tile(k, tk)
  nk = k // tk
  if mode == "nn":
    a_spec = pl.BlockSpec((tm, tk), lambda i, j, kk: (i, kk))
    b_spec = pl.BlockSpec((tk, tn), lambda i, j, kk: (kk, j))
    dot = _dot
  elif mode == "nt":
    a_spec = pl.BlockSpec((tm, tk), lambda i, j, kk: (i, kk))
    b_spec = pl.BlockSpec((tn, tk), lambda i, j, kk: (j, kk))
    dot = _dot_nt
  else:
    a_spec = pl.BlockSpec((tk, tm), lambda i, j, kk: (kk, i))
    b_spec = pl.BlockSpec((tk, tn), lambda i, j, kk: (kk, j))
    dot = _dot_tn
  o_spec = pl.BlockSpec((tm, tn), lambda i, j, kk: (i, j))
  has_add = add is not None

  def body(*refs):
    if has_add:
      a_ref, b_ref, add_ref, o_ref, acc_ref = refs
    else:
      a_ref, b_ref, o_ref, acc_ref = refs
    kk = pl.program_id(2)

    @pl.when(kk == 0)
    def _():
      acc_ref[...] = jnp.zeros_like(acc_ref)

    acc_ref[...] += dot(_bf(a_ref[...]), _bf(b_ref[...]))

    @pl.when(kk == nk - 1)
    def _():
      r = acc_ref[...]
      if has_add:
        r = r + add_scale * add_ref[...].astype(F32)
      o_ref[...] = r.astype(out_dtype)

  in_specs = [a_spec, b_spec] + ([o_spec] if has_add else [])
  args = (a, b) + ((add,) if has_add else ())
  return pl.pallas_call(
      body, name=name, grid=(m // tm, n // tn, nk), in_specs=in_specs, out_specs=o_spec,
      out_shape=jax.ShapeDtypeStruct((m, n), out_dtype),
      scratch_shapes=[pltpu.VMEM((tm, tn), F32)],
      compiler_params=_params(("parallel", "parallel", "arbitrary")),
  )(*args)


def _mm_res_ln(a, w, res, g, b, name, tm=512, tk=512):
  m, k = a.shape
  d = w.shape[1]
  tm, tk = _tile(m, tm), _tile(k, tk)
  nk = k // tk

  def body(a_ref, w_ref, r_ref, g_ref, b_ref, z_ref, x_ref, xb_ref, acc_ref):
    kk = pl.program_id(1)

    @pl.when(kk == 0)
    def _():
      acc_ref[...] = jnp.zeros_like(acc_ref)

    acc_ref[...] += _dot(_bf(a_ref[...]), _bf(w_ref[...]))

    @pl.when(kk == nk - 1)
    def _():
      z = DN_ALPHA * r_ref[...] + acc_ref[...]
      xhat, _ = _ln_stats(z)
      xn = xhat * g_ref[...] + b_ref[...]
      z_ref[...] = z
      x_ref[...] = xn
      xb_ref[...] = _bf(xn)

  row = pl.BlockSpec((tm, d), lambda i, kk: (i, 0))
  vec = pl.BlockSpec((1, d), lambda i, kk: (0, 0))
  return pl.pallas_call(
      body, name=name, grid=(m // tm, nk),
      in_specs=[pl.BlockSpec((tm, tk), lambda i, kk: (i, kk)), pl.BlockSpec((tk, d), lambda i, kk: (kk, 0)), row, vec, vec],
      out_specs=[row, row, row],
      out_shape=[jax.ShapeDtypeStruct((m, d), F32), jax.ShapeDtypeStruct((m, d), F32), jax.ShapeDtypeStruct((m, d), BF16)],
      scratch_shapes=[pltpu.VMEM((tm, d), F32)],
      compiler_params=_params(("parallel", "arbitrary")),
  )(a, w, res, g.reshape(1, d), b.reshape(1, d))


def _ln_bwd_call(dy, z, g, name, tm=512):
  m, d = z.shape
  tm = _tile(m, tm)
  n = m // tm

  def body(dy_ref, z_ref, g_ref, dz_ref, dzb_ref, dg_ref, db_ref):
    i = pl.program_id(0)

    @pl.when(i == 0)
    def _():
      dg_ref[...] = jnp.zeros_like(dg_ref)
      db_ref[...] = jnp.zeros_like(db_ref)

    dy_v = dy_ref[...]
    xhat, rstd = _ln_stats(z_ref[...])
    dz = _ln_bwd(dy_v, xhat, rstd, g_ref[...])
    dz_ref[...] = dz
    dzb_ref[...] = _bf(dz)
    dg_ref[...] += jnp.sum(dy_v * xhat, axis=0, keepdims=True)
    db_ref[...] += jnp.sum(dy_v, axis=0, keepdims=True)

  row = pl.BlockSpec((tm, d), lambda i: (i, 0))
  vec = pl.BlockSpec((1, d), lambda i: (0, 0))
  dz, dzb, dg, db = pl.pallas_call(
      body, name=name, grid=(n,), in_specs=[row, row, vec], out_specs=[row, row, vec, vec],
      out_shape=[jax.ShapeDtypeStruct((m, d), F32), jax.ShapeDtypeStruct((m, d), BF16),
                 jax.ShapeDtypeStruct((1, d), F32), jax.ShapeDtypeStruct((1, d), F32)],
      compiler_params=_params(("arbitrary",)),
  )(dy, z, g.reshape(1, d))
  return dz, dzb, dg[0], db[0]


def _ffn_up(xb, wg, wu, name, tm=1024, tn=256):
  m, d = xb.shape
  f = wg.shape[1]
  tm, tn = _tile(m, tm), _tile(f, tn)

  def body(x_ref, wg_ref, wu_ref, a_ref, b_ref, h_ref):
    xv = x_ref[...]
    a = _dot(xv, wg_ref[...])
    b = _dot(xv, wu_ref[...])
    a_ref[...] = _bf(a)
    b_ref[...] = _bf(b)
    h_ref[...] = _bf(a * _sigmoid(a) * b)

  wspec = pl.BlockSpec((d, tn), lambda i, j: (0, j))
  ospec = pl.BlockSpec((tm, tn), lambda i, j: (i, j))
  sds = jax.ShapeDtypeStruct((m, f), BF16)
  return pl.pallas_call(
      body, name=name, grid=(m // tm, f // tn),
      in_specs=[pl.BlockSpec((tm, d), lambda i, j: (i, 0)), wspec, wspec],
      out_specs=[ospec, ospec, ospec], out_shape=[sds, sds, sds],
      compiler_params=_params(("parallel", "parallel")),
  )(xb, wg, wu)


def _ffn_bwd_hidden(dzb, wd, a, b, name, tm=1024, tn=256):
  m, d = dzb.shape
  f = wd.shape[0]
  tm, tn = _tile(m, tm), _tile(f, tn)

  def body(dz_ref, wd_ref, a_ref, b_ref, da_ref, db_ref):
    dh = _dot_nt(dz_ref[...], wd_ref[...])
    av = a_ref[...].astype(F32)
    bv = b_ref[...].astype(F32)
    sg = _sigmoid(av)
    da_ref[...] = _bf(dh * bv * (sg * (1.0 + av * (1.0 - sg))))
    db_ref[...] = _bf(dh * (av * sg))

  hspec = pl.BlockSpec((tm, tn), lambda i, j: (i, j))
  sds = jax.ShapeDtypeStruct((m, f), BF16)
  return pl.pallas_call(
      body, name=name, grid=(m // tm, f // tn),
      in_specs=[pl.BlockSpec((tm, d), lambda i, j: (i, 0)), pl.BlockSpec((tn, d), lambda i, j: (j, 0)), hspec, hspec],
      out_specs=[hspec, hspec], out_shape=[sds, sds],
      compiler_params=_params(("parallel", "parallel")),
  )(dzb, wd, a, b)


def _ffn_bwd_input(da, db, wg, wu, dz, name, tm=512, tk=256):
  m, f = da.shape
  d = wg.shape[0]
  tm, tk = _tile(m, tm), _tile(f, tk)
  nk = f // tk

  def body(da_ref, db_ref, wg_ref, wu_ref, dz_ref, o_ref, acc_ref):
    kk = pl.program_id(1)

    @pl.when(kk == 0)
    def _():
      acc_ref[...] = jnp.zeros_like(acc_ref)

    acc_ref[...] += _dot_nt(da_ref[...], wg_ref[...]) + _dot_nt(db_ref[...], wu_ref[...])

    @pl.when(kk == nk - 1)
    def _():
      o_ref[...] = DN_ALPHA * dz_ref[...] + acc_ref[...]

  hspec = pl.BlockSpec((tm, tk), lambda i, kk: (i, kk))
  wspec = pl.BlockSpec((d, tk), lambda i, kk: (0, kk))
  row = pl.BlockSpec((tm, d), lambda i, kk: (i, 0))
  return pl.pallas_call(
      body, name=name, grid=(m // tm, nk), in_specs=[hspec, hspec, wspec, wspec, row], out_specs=row,
      out_shape=jax.ShapeDtypeStruct((m, d), F32), scratch_shapes=[pltpu.VMEM((tm, d), F32)],
      compiler_params=_params(("parallel", "arbitrary")),
  )(da, db, wg, wu, dz)


def _alibi_slopes():
  n = N_DIL_HEADS
  return jnp.exp2(-8.0 * (jnp.arange(n, dtype=F32) + 1.0) / n).reshape(1, n)


def _band_consts():
  qi = lax.broadcasted_iota(jnp.int32, (BLOCK, BLOCK), 0)
  ki = lax.broadcasted_iota(jnp.int32, (BLOCK, BLOCK), 1)
  steps_cur = (qi - ki).astype(F32)
  steps_prev = (qi + BLOCK - ki).astype(F32)
  return ki < 64, steps_cur, steps_prev, ki <= qi, ki >= qi


def _rows(start, d):
  if d == 1:
    return pl.ds(pl.multiple_of(start, BLOCK), BLOCK)
  return pl.ds(start, BLOCK, stride=d)


def _attn_fwd(h3, name):
  bl, s, _ = h3.shape
  npair = N_DIL_HEADS // 2

  def body(sl_ref, q_ref, k_ref, v_ref, o_ref, lse_ref, o_sc, l_sc):
    hp = pl.program_id(1)
    head0, steps_cur, steps_prev, mask_cur, mask_prev = _band_consts()
    slope = [sl_ref[0, 2 * hp], sl_ref[0, 2 * hp + 1]]

    for p, (_, d) in enumerate(DIL_PATTERNS):
      nblk = (s // d) // BLOCK
      has_prev_block = nblk > 1

      def blk(idx, carry, p=p, d=d, nblk=nblk, has_prev_block=has_prev_block):
        r = idx // nblk
        n = idx % nblk
        cur = _rows(r + n * (BLOCK * d), d)
        q2 = q_ref[cur, :]
        kc = _bf(k_ref[cur, :])
        vc = _bf(v_ref[cur, :])
        if has_prev_block:
          prev = _rows(r + jnp.maximum(n - 1, 0) * (BLOCK * d), d)
          kp = _bf(k_ref[prev, :])
          vp = _bf(v_ref[prev, :])
          first_block = jnp.where(n > 0, 0.0, NEG_BIG)
        outs, lses = [], []
        for j in range(2):
          hm = head0 if j == 0 else jnp.logical_not(head0)
          qj = _bf(jnp.where(hm, q2, 0.0) * ATT_SCALE)
          sc = _dot_nt(qj, kc) - (slope[j] * d) * steps_cur
          sc = jnp.where(mask_cur, sc, NEG_BIG)
          mx = jnp.max(sc, axis=1, keepdims=True)
          if has_prev_block:
            sp = _dot_nt(qj, kp) - (slope[j] * d) * steps_prev + first_block
            sp = jnp.where(mask_prev, sp, NEG_BIG)
            mx = jnp.maximum(mx, jnp.max(sp, axis=1, keepdims=True))
          pc = jnp.exp(sc - mx)
          den = jnp.sum(pc, axis=1, keepdims=True)
          acc = _dot(_bf(pc), vc)
          if has_prev_block:
            pp = jnp.exp(sp - mx)
            den = den + jnp.sum(pp, axis=1, keepdims=True)
            acc = acc + _dot(_bf(pp), vp)
          outs.append(acc / den)
          lses.append(mx + jnp.log(den))
        o_sc[p, cur, :] = jnp.where(head0, outs[0], outs[1])
        l_sc[p, cur, :] = jnp.where(head0, lses[0], lses[1])
        return carry

      lax.fori_loop(0, s // BLOCK, blk, 0)

    def merge(i, carry):
      rows = pl.ds(pl.multiple_of(i * BLOCK, BLOCK), BLOCK)
      l0, l1, l2 = l_sc[0, rows, :], l_sc[1, rows, :], l_sc[2, rows, :]
      mx = jnp.maximum(jnp.maximum(l0, l1), l2)
      e0, e1, e2 = jnp.exp(l0 - mx), jnp.exp(l1 - mx), jnp.exp(l2 - mx)
      tot = e0 + e1 + e2
      o_ref[rows, :] = (e0 * o_sc[0, rows, :] + e1 * o_sc[1, rows, :] + e2 * o_sc[2, rows, :]) / tot
      lse_ref[rows, :] = mx + jnp.log(tot)
      return carry

    lax.fori_loop(0, s // BLOCK, merge, 0)

  def col(off):
    return pl.BlockSpec((None, s, LANES), lambda b, p: (b, 0, off + p))

  sds = jax.ShapeDtypeStruct((bl, s, DIL_WIDTH), F32)
  return pl.pallas_call(
      body, name=name, grid=(bl, npair),
      in_specs=[pl.BlockSpec(memory_space=pltpu.SMEM), col(0), col(npair), col(2 * npair)],
      out_specs=[col(0), col(0)], out_shape=[sds, sds],
      scratch_shapes=[pltpu.VMEM((3, s, LANES), F32), pltpu.VMEM((3, s, LANES), F32)],
      compiler_params=_params(("parallel", "parallel")),
  )(_alibi_slopes(), h3, h3, h3)


def _attn_bwd(h3, out3, lse3, dcat3, name):
  bl, s, _ = h3.shape
  npair = N_DIL_HEADS // 2

  def body(sl_ref, q_ref, k_ref, v_ref, o_ref, l_ref, do_ref, dq_ref, dk_ref, dv_ref):
    hp = pl.program_id(1)
    head0, steps_cur, steps_prev, mask_cur, mask_prev = _band_consts()
    lane = lax.broadcasted_iota(jnp.int32, (BLOCK, LANES), 1)
    slope = [sl_ref[0, 2 * hp], sl_ref[0, 2 * hp + 1]]
    dq_ref[...] = jnp.zeros_like(dq_ref)
    dk_ref[...] = jnp.zeros_like(dk_ref)
    dv_ref[...] = jnp.zeros_like(dv_ref)

    for p, (_, d) in enumerate(DIL_PATTERNS):
      nblk = (s // d) // BLOCK
      has_prev_block = nblk > 1

      def blk(idx, carry, d=d, nblk=nblk, has_prev_block=has_prev_block):
        r = idx // nblk
        n = idx % nblk
        cur = _rows(r + n * (BLOCK * d), d)
        q2 = q_ref[cur, :]
        do2 = do_ref[cur, :]
        l2 = l_ref[cur, :]
        prod = do2 * o_ref[cur, :]
        kc = _bf(k_ref[cur, :])
        vc = _bf(v_ref[cur, :])
        if has_prev_block:
          prev = _rows(r + jnp.maximum(n - 1, 0) * (BLOCK * d), d)
          kp = _bf(k_ref[prev, :])
          vp = _bf(v_ref[prev, :])
          first_block = jnp.where(n > 0, 0.0, NEG_BIG)
          dkp = jnp.zeros((BLOCK, LANES), F32)
          dvp = jnp.zeros((BLOCK, LANES), F32)
        dq2 = jnp.zeros((BLOCK, LANES), F32)
        dkc = jnp.zeros((BLOCK, LANES), F32)
        dvc = jnp.zeros((BLOCK, LANES), F32)
        for j in range(2):
          hm = head0 if j == 0 else jnp.logical_not(head0)
          qj = _bf(jnp.where(hm, q2, 0.0) * ATT_SCALE)
          doj = _bf(jnp.where(hm, do2, 0.0))
          lj = jnp.sum(jnp.where(lane == 64 * j, l2, 0.0), axis=1, keepdims=True)
          dj = jnp.sum(jnp.where(hm, prod, 0.0), axis=1, keepdims=True)
          sc = _dot_nt(qj, kc) - (slope[j] * d) * steps_cur
          pc = jnp.exp(jnp.where(mask_cur, sc - lj, NEG_BIG))
          dsc = _bf(pc * (_dot_nt(doj, vc) - dj))
          dq_j = _dot(dsc, kc)
          dkc = dkc + _dot_tn(dsc, qj)
          dvc = dvc + _dot_tn(_bf(pc), doj)
          if has_prev_block:
            sp = _dot_nt(qj, kp) - (slope[j] * d) * steps_prev + first_block
            pp = jnp.exp(jnp.where(mask_prev, sp - lj, NEG_BIG))
            dsp = _bf(pp * (_dot_nt(doj, vp) - dj))
            dq_j = dq_j + _dot(dsp, kp)
            dkp = dkp + _dot_tn(dsp, qj)
            dvp = dvp + _dot_tn(_bf(pp), doj)
          dq2 = dq2 + jnp.where(hm, dq_j, 0.0) * ATT_SCALE
        dq_ref[cur, :] += dq2
        dk_ref[cur, :] += dkc
        dv_ref[cur, :] += dvc
        if has_prev_block:
          dk_ref[prev, :] += dkp
          dv_ref[prev, :] += dvp
        return carry

      lax.fori_loop(0, s // BLOCK, blk, 0)

  def col(off):
    return pl.BlockSpec((None, s, LANES), lambda b, p: (b, 0, off + p))

  sds = jax.ShapeDtypeStruct((bl, s, DIL_WIDTH), F32)
  return pl.pallas_call(
      body, name=name, grid=(bl, npair),
      in_specs=[pl.BlockSpec(memory_space=pltpu.SMEM), col(0), col(npair), col(2 * npair), col(0), col(0), col(0)],
      out_specs=[col(0), col(0), col(0)], out_shape=[sds, sds, sds],
      compiler_params=_params(("parallel", "parallel")),
  )(_alibi_slopes(), h3, h3, h3, out3, lse3, dcat3)


def _mem_heads(tq):
  lane = lax.broadcasted_iota(jnp.int32, (tq, LANES), 1)
  return lane < 64


def _mem_fwd(h3, qcol, mkv3, name, tq=512):
  bl, s, _ = h3.shape
  nm = mkv3.shape[1]
  tq = _tile(s, tq)

  def body(q_ref, kv_ref, o_ref):
    head0 = _mem_heads(tq)
    for lg in range(MEM_WIDTH // LANES):
      cs = slice(lg * LANES, (lg + 1) * LANES)
      q2 = q_ref[:, cs]
      mk = _bf(kv_ref[:, cs])
      mv = _bf(kv_ref[:, MEM_WIDTH + lg * LANES:MEM_WIDTH + (lg + 1) * LANES])
      outs = []
      for j in range(2):
        hm = head0 if j == 0 else jnp.logical_not(head0)
        qj = _bf(jnp.where(hm, q2, 0.0) * ATT_SCALE)
        sc = _dot_nt(qj, mk)
        mx = jnp.max(sc, axis=1, keepdims=True)
        pe = jnp.exp(sc - mx)
        den = jnp.sum(pe, axis=1, keepdims=True)
        outs.append(_dot(_bf(pe / den), mv))
      o_ref[:, cs] = jnp.where(head0, outs[0], outs[1])

  return pl.pallas_call(
      body, name=name, grid=(bl, s // tq),
      in_specs=[pl.BlockSpec((None, tq, MEM_WIDTH), lambda b, i: (b, i, qcol)),
                pl.BlockSpec((None, nm, 2 * MEM_WIDTH), lambda b, i: (b, 0, 0))],
      out_specs=pl.BlockSpec((None, tq, MEM_WIDTH), lambda b, i: (b, i, 0)),
      out_shape=jax.ShapeDtypeStruct((bl, s, MEM_WIDTH), F32),
      compiler_params=_params(("parallel", "parallel")),
  )(h3, mkv3)


def _mem_bwd(h3, qcol, mkv3, dcat3, name, tq=512):
  bl, s, _ = h3.shape
  nm = mkv3.shape[1]
  tq = _tile(s, tq)
  docol = dcat3.shape[2] // MEM_WIDTH - 1

  def body(q_ref, kv_ref, do_ref, dq_ref, dkv_ref):
    i = pl.program_id(1)

    @pl.when(i == 0)
    def _():
      dkv_ref[...] = jnp.zeros_like(dkv_ref)

    head0 = _mem_heads(tq)
    for lg in range(MEM_WIDTH // LANES):
      cs = slice(lg * LANES, (lg + 1) * LANES)
      vs = slice(MEM_WIDTH + lg * LANES, MEM_WIDTH + (lg + 1) * LANES)
      q2 = q_ref[:, cs]
      do2 = do_ref[:, cs]
      mk = _bf(kv_ref[:, cs])
      mv = _bf(kv_ref[:, vs])
      dq2 = jnp.zeros((tq, LANES), F32)
      dmk = jnp.zeros((nm, LANES), F32)
      dmv = jnp.zeros((nm, LANES), F32)
      for j in range(2):
        hm = head0 if j == 0 else jnp.logical_not(head0)
        qj = _bf(jnp.where(hm, q2, 0.0) * ATT_SCALE)
        doj = _bf(jnp.where(hm, do2, 0.0))
        sc = _dot_nt(qj, mk)
        mx = jnp.max(sc, axis=1, keepdims=True)
        pe = jnp.exp(sc - mx)
        pn = pe / jnp.sum(pe, axis=1, keepdims=True)
        pb = _bf(pn)
        dp = _dot_nt(doj, mv)
        dj = jnp.sum(pb.astype(F32) * dp, axis=1, keepdims=True)
        ds = _bf(pn * (dp - dj))
        dq2 = dq2 + jnp.where(hm, _dot(ds, mk), 0.0) * ATT_SCALE
        dmk = dmk + _dot_tn(ds, qj)
        dmv = dmv + _dot_tn(pb, doj)
      dq_ref[:, cs] = dq2
      dkv_ref[:, cs] += dmk
      dkv_ref[:, vs] += dmv

  return pl.pallas_call(
      body, name=name, grid=(bl, s // tq),
      in_specs=[pl.BlockSpec((None, tq, MEM_WIDTH), lambda b, i: (b, i, qcol)),
                pl.BlockSpec((None, nm, 2 * MEM_WIDTH), lambda b, i: (b, 0, 0)),
                pl.BlockSpec((None, tq, MEM_WIDTH), lambda b, i: (b, i, docol))],
      out_specs=[pl.BlockSpec((None, tq, MEM_WIDTH), lambda b, i: (b, i, 0)),
                 pl.BlockSpec((None, nm, 2 * MEM_WIDTH), lambda b, i: (b, 0, 0))],
      out_shape=[jax.ShapeDtypeStruct((bl, s, MEM_WIDTH), F32), jax.ShapeDtypeStruct((bl, nm, 2 * MEM_WIDTH), F32)],
      compiler_params=_params(("parallel", "arbitrary")),
  )(h3, mkv3, dcat3)


def _sgu_consts():
  ti = lax.broadcasted_iota(jnp.int32, (CHUNK, CHUNK), 0)
  si = lax.broadcasted_iota(jnp.int32, (CHUNK, CHUNK), 1)
  return si <= ti, si < 64


def _sgu_bias_lanes(b_s):
  return jnp.repeat(b_s.T, 64, axis=1)


def _sgu_fwd(h2, ln_g, ln_b, w_s, b_s, name, tr=512):
  t, _ = h2.shape
  tr = _tile(t, tr)
  nch = tr // CHUNK
  npair = N_SGU_GROUPS // 2

  def body(u_ref, v_ref, g_ref, b_ref, w_ref, bs_ref, o_ref, vn_sc):
    tril, head0 = _sgu_consts()
    xhat, _ = _ln_stats(_gelu(v_ref[...]))
    vn_sc[...] = _bf(xhat * g_ref[...] + b_ref[...])
    for jp in range(npair):
      cs = slice(jp * LANES, (jp + 1) * LANES)
      w0 = _bf(jnp.where(tril, w_ref[2 * jp], 0.0))
      w1 = _bf(jnp.where(tril, w_ref[2 * jp + 1], 0.0))
      bias = bs_ref[:, cs]
      for c in range(nch):
        rs = slice(c * CHUNK, (c + 1) * CHUNK)
        vb = vn_sc[rs, cs]
        mixed = jnp.where(head0, _dot(w0, vb), _dot(w1, vb)) + bias
        o_ref[rs, cs] = _gelu(u_ref[rs, cs]) * mixed

  blk = lambda j: pl.BlockSpec((tr, SGU_WIDTH), lambda i: (i, j))
  vec = pl.BlockSpec((1, SGU_WIDTH), lambda i: (0, 0))
  return pl.pallas_call(
      body, name=name, grid=(t // tr,),
      in_specs=[blk(0), blk(1), vec, vec,
                pl.BlockSpec((N_SGU_GROUPS, CHUNK, CHUNK), lambda i: (0, 0, 0)),
                pl.BlockSpec((CHUNK, SGU_WIDTH), lambda i: (0, 0))],
      out_specs=blk(0), out_shape=jax.ShapeDtypeStruct((t, SGU_WIDTH), F32),
      scratch_shapes=[pltpu.VMEM((tr, SGU_WIDTH), BF16)],
      compiler_params=_params(("parallel",)),
  )(h2, h2, ln_g.reshape(1, -1), ln_b.reshape(1, -1), w_s, _sgu_bias_lanes(b_s))


def _sgu_bwd(h2, dcat, ln_g, ln_b, w_s, b_s, name, tr=512):
  t, _ = h2.shape
  tr = _tile(t, tr)
  nch = tr // CHUNK
  npair = N_SGU_GROUPS // 2
  nsteps = t // tr

  def body(u_ref, v_ref, dm_ref, g_ref, b_ref, w_ref, bs_ref,
           du_ref, dv_ref, dw_ref, dbs_ref, dg_ref, db_ref, vn_sc, dmx_sc, dvn_sc, mix_sc, dbx_sc):
    i = pl.program_id(0)
    tril, head0 = _sgu_consts()

    @pl.when(i == 0)
    def _():
      dw_ref[...] = jnp.zeros_like(dw_ref)
      dg_ref[...] = jnp.zeros_like(dg_ref)
      db_ref[...] = jnp.zeros_like(db_ref)
      dbx_sc[...] = jnp.zeros_like(dbx_sc)

    gv, gv_der = _gelu_parts(v_ref[...])
    xhat, rstd = _ln_stats(gv)
    g = g_ref[...]
    vn_sc[...] = _bf(xhat * g + b_ref[...])
    gu, gu_der = _gelu_parts(u_ref[...])
    dmix = dm_ref[...]
    dmx_sc[...] = dmix * gu

    for jp in range(npair):
      cs = slice(jp * LANES, (jp + 1) * LANES)
      w0 = _bf(jnp.where(tril, w_ref[2 * jp], 0.0))
      w1 = _bf(jnp.where(tril, w_ref[2 * jp + 1], 0.0))
      bias = bs_ref[:, cs]
      dw0 = jnp.zeros((CHUNK, CHUNK), F32)
      dw1 = jnp.zeros((CHUNK, CHUNK), F32)
      dbx = jnp.zeros((CHUNK, LANES), F32)
      for c in range(nch):
        rs = slice(c * CHUNK, (c + 1) * CHUNK)
        vb = vn_sc[rs, cs]
        mix_sc[rs, cs] = jnp.where(head0, _dot(w0, vb), _dot(w1, vb)) + bias
        dmx = dmx_sc[rs, cs]
        d0 = _bf(jnp.where(head0, dmx, 0.0))
        d1 = _bf(jnp.where(head0, 0.0, dmx))
        dvn_sc[rs, cs] = _dot_tn(w0, d0) + _dot_tn(w1, d1)
        dw0 = dw0 + _dot_nt(d0, vb)
        dw1 = dw1 + _dot_nt(d1, vb)
        dbx = dbx + dmx
      dw_ref[2 * jp] += dw0
      dw_ref[2 * jp + 1] += dw1
      dbx_sc[:, cs] += dbx

    du_ref[...] = _bf(dmix * mix_sc[...] * gu_der)
    dvn = dvn_sc[...]
    dv_ref[...] = _bf(_ln_bwd(dvn, xhat, rstd, g) * gv_der)
    dg_ref[...] += jnp.sum(dvn * xhat, axis=0, keepdims=True)
    db_ref[...] += jnp.sum(dvn, axis=0, keepdims=True)

    @pl.when(i == nsteps - 1)
    def _():
      lane = lax.broadcasted_iota(jnp.int32, (CHUNK, LANES), 1)
      acc = jnp.zeros((CHUNK, LANES), F32)
      for gi in range(N_SGU_GROUPS):
        jp, j = gi // 2, gi % 2
        part = dbx_sc[:, jp * LANES:(jp + 1) * LANES]
        hm = (lane < 64) if j == 0 else (lane >= 64)
        colsum = jnp.sum(jnp.where(hm, part, 0.0), axis=1, keepdims=True)
        acc = jnp.where(lane == gi, colsum, acc)
        dw_ref[gi] = jnp.where(tril, dw_ref[gi], 0.0)
      dbs_ref[...] = acc

  blk = lambda j: pl.BlockSpec((tr, SGU_WIDTH), lambda i: (i, j))
  vec = pl.BlockSpec((1, SGU_WIDTH), lambda i: (0, 0))
  wspec = pl.BlockSpec((N_SGU_GROUPS, CHUNK, CHUNK), lambda i: (0, 0, 0))
  big = lambda dt: pltpu.VMEM((tr, SGU_WIDTH), dt)
  du, dv, dw, dbs, dg, db = pl.pallas_call(
      body, name=name, grid=(nsteps,),
      in_specs=[blk(0), blk(1), blk(0), vec, vec, wspec, pl.BlockSpec((CHUNK, SGU_WIDTH), lambda i: (0, 0))],
      out_specs=[blk(0), blk(0), wspec, pl.BlockSpec((CHUNK, LANES), lambda i: (0, 0)), vec, vec],
      out_shape=[jax.ShapeDtypeStruct((t, SGU_WIDTH), BF16), jax.ShapeDtypeStruct((t, SGU_WIDTH), BF16),
                 jax.ShapeDtypeStruct((N_SGU_GROUPS, CHUNK, CHUNK), F32), jax.ShapeDtypeStruct((CHUNK, LANES), F32),
                 jax.ShapeDtypeStruct((1, SGU_WIDTH), F32), jax.ShapeDtypeStruct((1, SGU_WIDTH), F32)],
      scratch_shapes=[big(BF16), big(F32), big(F32), big(F32), pltpu.VMEM((CHUNK, SGU_WIDTH), F32)],
      compiler_params=_params(("arbitrary",)),
  )(h2, h2, dcat, ln_g.reshape(1, -1), ln_b.reshape(1, -1), w_s, _sgu_bias_lanes(b_s))
  return du, dv, dw, dbs[:, :N_SGU_GROUPS].T, dg[0], db[0]


def _loss_head(xo, tgt, name, tm=512):
  m, d = xo.shape
  tm = _tile(m, tm)

  def body(x_ref, t_ref, dx_ref, l_ref):
    @pl.when(pl.program_id(0) == 0)
    def _():
      l_ref[...] = jnp.zeros_like(l_ref)

    diff = x_ref[...] - t_ref[...]
    dx_ref[...] = diff * (1.0 / d)
    rowsum = jnp.sum(diff * diff, axis=1, keepdims=True)
    tot = jnp.sum(rowsum, axis=0, keepdims=True) * (0.5 / d)
    l_ref[...] += jnp.broadcast_to(tot, l_ref.shape)

  row = pl.BlockSpec((tm, d), lambda i: (i, 0))
  dx, l = pl.pallas_call(
      body, name=name, grid=(m // tm,), in_specs=[row, row],
      out_specs=[row, pl.BlockSpec((8, LANES), lambda i: (0, 0))],
      out_shape=[jax.ShapeDtypeStruct((m, d), F32), jax.ShapeDtypeStruct((8, LANES), F32)],
      compiler_params=_params(("arbitrary",)),
  )(xo, tgt)
  return l[0, 0], dx


def _local_step(x3, mem3, tgt3, w):
  bl, s, d = x3.shape
  t = bl * s
  nm = mem3.shape[1]
  mem2 = mem3.reshape(bl * nm, d)
  x = x3.reshape(t, d)
  xb = x
  saved = []
  for i in range(DEPTH):
    j = i // 2
    attn = i % 2 == 0
    mkv = _mm(mem2, w["w_mem_kv"][i], "nn", F32, f"mkv_fwd_{i}", tm=1024, tn=512, tk=1024)
    mkv3 = mkv.reshape(bl, nm, 2 * MEM_WIDTH)
    w_in = w["a_w_in"][j] if attn else w["b_w_in"][j]
    h = _mm(xb, w_in, "nn", F32, f"in_proj_{i}", tm=1024, tn=512, tk=1024)
    h3 = h.reshape(bl, s, -1)
    if attn:
      mix3, lse3 = _attn_fwd(h3, f"dil_attn_fwd_{i}")
      mix = mix3.reshape(t, DIL_WIDTH)
      qcol = 3 * DIL_WIDTH // MEM_WIDTH
    else:
      mix = _sgu_fwd(h, w["sgu_ln_g"][j], w["sgu_ln_b"][j], w["sgu_w_s"][j], w["sgu_b_s"][j], f"sgu_fwd_{i}")
      lse3 = None
      qcol = 2 * SGU_WIDTH // MEM_WIDTH
    mo = _mem_fwd(h3, qcol, mkv3, f"mem_attn_fwd_{i}").reshape(t, MEM_WIDTH)
    cat = jnp.concatenate([mix, mo], axis=1).astype(BF16)
    z1, xm, xmb = _mm_res_ln(cat, w["w_out"][i], x, w["ln_mix_g"][i], w["ln_mix_b"][i], f"out_proj_ln_{i}", tk=1024)
    a, b, hm = _ffn_up(xmb, w["w_gate"][i], w["w_up"][i], f"ffn_up_{i}")
    z2, xo, xob = _mm_res_ln(hm, w["w_down"][i], xm, w["ln_ffn_g"][i], w["ln_ffn_b"][i], f"ffn_down_ln_{i}", tk=1408)
    saved.append(dict(xb=xb, h=h, h3=h3, mkv3=mkv3, mix3=(mix3 if attn else None), lse3=lse3, cat=cat, z1=z1,
                      xmb=xmb, a=a, b=b, hm=hm, z2=z2, qcol=qcol))
    x, xb = xo, xob

  loss, dx = _loss_head(x, tgt3.reshape(t, d), "loss_head")

  names = ("a_w_in", "b_w_in", "sgu_ln_g", "sgu_ln_b", "sgu_w_s", "sgu_b_s", "w_mem_kv", "w_out",
           "ln_mix_g", "ln_mix_b", "w_gate", "w_up", "w_down", "ln_ffn_g", "ln_ffn_b")
  grads = {n: [None] * w[n].shape[0] for n in names}
  for i in reversed(range(DEPTH)):
    j = i // 2
    attn = i % 2 == 0
    sv = saved[i]
    dz2, dz2b, grads["ln_ffn_g"][i], grads["ln_ffn_b"][i] = _ln_bwd_call(dx, sv["z2"], w["ln_ffn_g"][i], f"ln_ffn_bwd_{i}")
    da, db = _ffn_bwd_hidden(dz2b, w["w_down"][i], sv["a"], sv["b"], f"ffn_bwd_hidden_{i}")
    grads["w_down"][i] = _mm(sv["hm"], dz2b, "tn", F32, f"dw_down_{i}", tm=1408, tn=1024, tk=1024)
    grads["w_gate"][i] = _mm(sv["xmb"], da, "tn", F32, f"dw_gate_{i}", tm=1024, tn=1408, tk=1024)
    grads["w_up"][i] = _mm(sv["xmb"], db, "tn", F32, f"dw_up_{i}", tm=1024, tn=1408, tk=1024)
    dxm = _ffn_bwd_input(da, db, w["w_gate"][i], w["w_up"][i], dz2, f"ffn_bwd_input_{i}")
    dz1, dz1b, grads["ln_mix_g"][i], grads["ln_mix_b"][i] = _ln_bwd_call(dxm, sv["z1"], w["ln_mix_g"][i], f"ln_mix_bwd_{i}")
    grads["w_out"][i] = _mm(sv["cat"], dz1b, "tn", F32, f"dw_out_{i}", tm=1024, tn=1024, tk=1024)
    dcat = _mm(dz1b, w["w_out"][i], "nt", F32, f"out_proj_bwd_{i}", tm=1024, tn=1024, tk=1024)
    dcat3 = dcat.reshape(bl, s, -1)
    dqm3, dmkv3 = _mem_bwd(sv["h3"], sv["qcol"], sv["mkv3"], dcat3, f"mem_attn_bwd_{i}")
    grads["w_mem_kv"][i] = _mm(mem2, dmkv3.reshape(bl * nm, 2 * MEM_WIDTH), "tn", F32, f"dw_mem_kv_{i}", tm=1024, tn=512, tk=1024)
    dqm = dqm3.reshape(t, MEM_WIDTH).astype(BF16)
    if attn:
      dq3, dk3, dv3 = _attn_bwd(sv["h3"], sv["mix3"], sv["lse3"], dcat3, f"dil_attn_bwd_{i}")
      parts = [dq3.reshape(t, -1).astype(BF16), dk3.reshape(t, -1).astype(BF16), dv3.reshape(t, -1).astype(BF16), dqm]
    else:
      du, dv, dws, dbs, dlg, dlb = _sgu_bwd(sv["h"], dcat, w["sgu_ln_g"][j], w["sgu_ln_b"][j], w["sgu_w_s"][j],
                                             w["sgu_b_s"][j], f"sgu_bwd_{i}")
      grads["sgu_w_s"][j], grads["sgu_b_s"][j], grads["sgu_ln_g"][j], grads["sgu_ln_b"][j] = dws, dbs, dlg, dlb
      parts = [du, dv, dqm]
    dh = jnp.concatenate(parts, axis=1)
    w_in = w["a_w_in"][j] if attn else w["b_w_in"][j]
    grads["a_w_in" if attn else "b_w_in"][j] = _mm(sv["xb"], dh, "tn", F32, f"dw_in_{i}", tm=1024, tn=896 if not attn else 640, tk=1024)
    dx = _mm(dh, w_in, "nt", F32, f"in_proj_bwd_{i}", add=dz1, add_scale=DN_ALPHA, tm=1024, tn=1024, tk=896 if not attn else 640)
  return loss, dx.reshape(bl, s, d), {n: jnp.stack(v) for n, v in grads.items()}


def _my_place():
  return lax.axis_index("x"), lax.axis_index("y"), lax.axis_index("c")


def _other_chips(x, y):
  return [(1 - x, y), (x, 1 - y), (1 - x, 1 - y)]


ANY = pl.BlockSpec(memory_space=pl.ANY)


def _all_gather_halves(wl, name):
  _, r, c_ = wl.shape

  def body(w_ref, g_ref, send_sems, recv_sems, local_sem):
    x, y, c = _my_place()
    me = 2 * x + y
    sibling = (x, y, 1 - c)
    chips = _other_chips(x, y)
    mine = pltpu.make_async_copy(w_ref, g_ref.at[me], local_sem)
    mine.start()

    def copy(k, src, dst, to):
      return pltpu.make_async_remote_copy(src_ref=src, dst_ref=dst, send_sem=send_sems.at[k], recv_sem=recv_sems.at[k],
                                          device_id=to, device_id_type=MESH_ID)

    first = [copy(k, w_ref.at[c], g_ref.at[me, c], (px, py, c)) for k, (px, py) in enumerate(chips)]
    for cp in first:
      cp.start()
    passed = []
    for k, (px, py) in enumerate(chips):
      landed = g_ref.at[2 * px + py, c]
      copy(k, landed, landed, (px, py, c)).wait_recv()
      fwd = copy(3 + k, landed, landed, sibling)
      fwd.start()
      passed.append(fwd)
    for k, (px, py) in enumerate(chips):
      theirs = g_ref.at[2 * px + py, 1 - c]
      copy(3 + k, theirs, theirs, sibling).wait_recv()
    for cp in first + passed:
      cp.wait_send()
    mine.wait()

  return pl.pallas_call(
      body, name=name, in_specs=[ANY], out_specs=ANY,
      out_shape=jax.ShapeDtypeStruct((4, 2, r, c_), wl.dtype),
      scratch_shapes=[pltpu.SemaphoreType.DMA((6,)), pltpu.SemaphoreType.DMA((6,)), pltpu.SemaphoreType.DMA],
  )(wl)


def _sibling_swap(v, name):
  def body(v_ref, o_ref, send_sem, recv_sem):
    x, y, c = _my_place()
    cp = pltpu.make_async_remote_copy(src_ref=v_ref, dst_ref=o_ref, send_sem=send_sem, recv_sem=recv_sem,
                                      device_id=(x, y, 1 - c), device_id_type=MESH_ID)
    cp.start()
    cp.wait()

  return pl.pallas_call(
      body, name=name, in_specs=[ANY], out_specs=ANY, out_shape=jax.ShapeDtypeStruct(v.shape, v.dtype),
      scratch_shapes=[pltpu.SemaphoreType.DMA, pltpu.SemaphoreType.DMA],
  )(v)


def _chip_exchange(q, name):
  _, r, c_ = q.shape

  def body(q_ref, o_ref, send_sems, recv_sems):
    x, y, c = _my_place()
    cps = []
    for k, (px, py) in enumerate(_other_chips(x, y)):
      cp = pltpu.make_async_remote_copy(src_ref=q_ref.at[2 * px + py], dst_ref=o_ref.at[k], send_sem=send_sems.at[k],
                                        recv_sem=recv_sems.at[k], device_id=(px, py, c), device_id_type=MESH_ID)
      cp.start()
      cps.append(cp)
    for cp in cps:
      cp.wait()

  return pl.pallas_call(
      body, name=name, in_specs=[ANY], out_specs=ANY, out_shape=jax.ShapeDtypeStruct((3, r, c_), q.dtype),
      scratch_shapes=[pltpu.SemaphoreType.DMA((3,)), pltpu.SemaphoreType.DMA((3,))],
  )(q)


def _share_halves(v, name):
  r, c_ = v.shape

  def body(v_ref, o_ref, send_sem, recv_sem, local_sem):
    x, y, c = _my_place()
    mine = pltpu.make_async_copy(v_ref, o_ref.at[c], local_sem)
    mine.start()
    cp = pltpu.make_async_remote_copy(src_ref=v_ref, dst_ref=o_ref.at[c], send_sem=send_sem, recv_sem=recv_sem,
                                      device_id=(x, y, 1 - c), device_id_type=MESH_ID)
    cp.start()
    cp.wait()
    mine.wait()

  return pl.pallas_call(
      body, name=name, in_specs=[ANY], out_specs=ANY, out_shape=jax.ShapeDtypeStruct((2, r, c_), v.dtype),
      scratch_shapes=[pltpu.SemaphoreType.DMA, pltpu.SemaphoreType.DMA, pltpu.SemaphoreType.DMA],
  )(v)


def _half_spec(tr, c_, pick):
  return pl.BlockSpec((None, None, tr, c_), lambda s, r, place: (s, pick(place), r, 0))


def _cast_other_half(p, place, name, tr=512):
  _, _, r, c_ = p.shape
  tr = _tile(r, tr, 16)

  def body(place_ref, p_ref, o_ref):
    o_ref[...] = _bf(p_ref[...])

  out_spec = pl.BlockSpec((None, tr, c_), lambda s, rr, place: (s, rr, 0))
  return pl.pallas_call(
      body, name=name, out_shape=jax.ShapeDtypeStruct((4, r, c_), BF16),
      grid_spec=pltpu.PrefetchScalarGridSpec(num_scalar_prefetch=1, grid=(4, r // tr),
                                             in_specs=[_half_spec(tr, c_, lambda place: 1 - place[1])], out_specs=out_spec),
      compiler_params=_params(("parallel", "parallel")),
  )(place, p)


def _add_sibling(p, x1, place, name, tr=512):
  _, _, r, c_ = p.shape
  tr = _tile(r, tr, 16)

  def body(place_ref, p_ref, x_ref, o_ref):
    o_ref[...] = _bf(p_ref[...] + x_ref[...].astype(F32))

  row = pl.BlockSpec((None, tr, c_), lambda s, rr, place: (s, rr, 0))
  return pl.pallas_call(
      body, name=name, out_shape=jax.ShapeDtypeStruct((4, r, c_), BF16),
      grid_spec=pltpu.PrefetchScalarGridSpec(num_scalar_prefetch=1, grid=(4, r // tr),
                                             in_specs=[_half_spec(tr, c_, lambda place: place[1]), row], out_specs=row),
      compiler_params=_params(("parallel", "parallel")),
  )(place, p, x1)


def _sum_own(p, x1, x3, place, name, tr=512):
  _, _, r, c_ = p.shape
  tr = _tile(r, tr, 16)

  def body(place_ref, p_ref, x1_ref, x3_ref, o_ref):
    acc = p_ref[...] + x1_ref[...].astype(F32)
    for k in range(3):
      acc = acc + x3_ref[k].astype(F32)
    o_ref[...] = acc

  return pl.pallas_call(
      body, name=name, out_shape=jax.ShapeDtypeStruct((r, c_), F32),
      grid_spec=pltpu.PrefetchScalarGridSpec(
          num_scalar_prefetch=1, grid=(r // tr,),
          in_specs=[pl.BlockSpec((None, None, tr, c_), lambda rr, place: (place[0], place[1], rr, 0)),
                    pl.BlockSpec((None, tr, c_), lambda rr, place: (place[0], rr, 0)),
                    pl.BlockSpec((3, tr, c_), lambda rr, place: (0, rr, 0))],
          out_specs=pl.BlockSpec((tr, c_), lambda rr, place: (rr, 0))),
      compiler_params=_params(("parallel",)),
  )(place, p, x1, x3)


def _reduce_scatter(p):
  x, y, c = _my_place()
  place = jnp.stack([2 * x + y, c]).astype(jnp.int32)
  x1 = _sibling_swap(_cast_other_half(p, place, "rs_cast_other_half"), "rs_sibling_swap")
  q = _add_sibling(p, x1, place, "rs_add_sibling")
  x3 = _chip_exchange(q, "rs_chip_exchange")
  mine = _sum_own(p, x1, x3, place, "rs_sum_own")
  return _share_halves(mine, "rs_share_halves")


def _adamw(w, g, m, v, name):
  shape = w.shape
  cols = shape[-1]
  rows = w.size // cols
  tr = _tile(rows, max(8, (256 * 1024) // cols // 8 * 8), 8)

  def body(w_ref, g_ref, m_ref, v_ref, d_ref, nm_ref, nv_ref):
    gv = g_ref[...]
    nm = ADAM_B1 * m_ref[...] + (1.0 - ADAM_B1) * gv
    nv = ADAM_B2 * v_ref[...] + (1.0 - ADAM_B2) * (gv * gv)
    m_hat = nm / (1.0 - ADAM_B1 ** ADAM_STEP)
    v_hat = nv / (1.0 - ADAM_B2 ** ADAM_STEP)
    d_ref[...] = -ADAM_LR * (m_hat / (jnp.sqrt(v_hat) + ADAM_EPS) + ADAM_WD * w_ref[...])
    nm_ref[...] = nm
    nv_ref[...] = nv

  spec = pl.BlockSpec((tr, cols), lambda i: (i, 0))
  sds = jax.ShapeDtypeStruct((rows, cols), F32)
  outs = pl.pallas_call(
      body, name=name, grid=(rows // tr,), in_specs=[spec] * 4, out_specs=[spec] * 3, out_shape=[sds] * 3,
      compiler_params=_params(("parallel",)),
  )(*(t.reshape(rows, cols) for t in (w, g, m, v)))
  return tuple(o.reshape(shape) for o in outs)


SHARDED = (("a_w_in", 2), ("b_w_in", 2), ("w_mem_kv", 1), ("w_out", 1), ("w_gate", 2), ("w_up", 2), ("w_down", 1))
SMALL_SHARDED = (("sgu_ln_g", 1), ("sgu_ln_b", 1))
REPLICATED = ("sgu_w_s", "sgu_b_s", "ln_mix_g", "ln_mix_b", "ln_ffn_g", "ln_ffn_b")
SMALL_ORDER = ("sgu_w_s", "sgu_b_s", "ln_mix_g", "ln_mix_b", "ln_ffn_g", "ln_ffn_b", "sgu_ln_g", "sgu_ln_b")
HALF_ALIGN = 32 * FLAT_COLS


def _pad_to(v, n):
  return jnp.pad(v, (0, n - v.shape[0]))


def _round_up(n, a):
  return -(-n // a) * a


def _to_shard_major(full, axis):
  shp = full.shape
  cut = shp[:axis] + (4, shp[axis] // 4) + shp[axis + 1:]
  return jnp.moveaxis(full.reshape(cut), axis, 0).reshape(4, -1)


def _from_shard_major(rows, shard_shape, axis):
  full = jnp.moveaxis(rows.reshape((4,) + tuple(shard_shape)), 0, axis)
  shp = full.shape
  return full.reshape(shp[:axis] + (shp[axis] * shp[axis + 1],) + shp[axis + 2:])


def _gather_weights(shards):
  pieces = [shards[n].astype(BF16).reshape(-1) for n, _ in SHARDED]
  pieces += [lax.bitcast_convert_type(shards[n], BF16).reshape(-1) for n, _ in SMALL_SHARDED]
  flat = jnp.concatenate(pieces)
  n_pad = _round_up(flat.shape[0], 2 * HALF_ALIGN)
  flat = _pad_to(flat, n_pad).reshape(2, n_pad // (2 * FLAT_COLS), FLAT_COLS)
  g = _all_gather_halves(flat, "gather_weights").reshape(4, n_pad)
  out, off = {}, 0
  for n, axis in SHARDED:
    sz = shards[n].size
    out[n] = _from_shard_major(g[:, off:off + sz], shards[n].shape, axis)
    off += sz
  for n, axis in SMALL_SHARDED:
    sz = 2 * shards[n].size
    vals = lax.bitcast_convert_type(g[:, off:off + sz].reshape((4,) + shards[n].shape + (2,)), F32)
    out[n] = _from_shard_major(vals, shards[n].shape, axis)
    off += sz
  return out


def _reduce_grads(grads, shard_shapes):
  big = jnp.concatenate([_to_shard_major(grads[n], axis) for n, axis in SHARDED], axis=1)
  small = jnp.concatenate([grads[n].reshape(-1) for n in SMALL_ORDER])
  n_small = _round_up(small.shape[0], 4 * 2 * 8 * FLAT_COLS)
  small4 = _pad_to(small, n_small).reshape(4, n_small // 4)
  n_big = big.shape[1]
  n_row = _round_up(n_big + n_small // 4, 2 * HALF_ALIGN)
  p = jnp.concatenate([big, small4, jnp.zeros((4, n_row - n_big - n_small // 4), F32)], axis=1)
  p = p.reshape(4, 2, n_row // (2 * FLAT_COLS), FLAT_COLS)
  mine = _reduce_scatter(p).reshape(n_row)
  out, off = {}, 0
  for n, _ in SHARDED:
    sz = math.prod(shard_shapes[n])
    out[n] = mine[off:off + sz].reshape(shard_shapes[n])
    off += sz
  piece = mine[n_big:n_big + n_small // 4].reshape(2, n_small // (4 * 2 * FLAT_COLS), FLAT_COLS)
  small_sum = _all_gather_halves(piece, "gather_small_grads").reshape(n_small)
  off = 0
  for n in SMALL_ORDER:
    sz = grads[n].size
    out[n] = small_sum[off:off + sz].reshape(grads[n].shape)
    off += sz
  return out


WEIGHT_NAMES = ("a_w_in", "b_w_in", "sgu_ln_g", "sgu_ln_b", "sgu_w_s", "sgu_b_s", "w_mem_kv", "w_out",
                "ln_mix_g", "ln_mix_b", "w_gate", "w_up", "w_down", "ln_ffn_g", "ln_ffn_b")


def kernel(x, mem, a_w_in, b_w_in, sgu_ln_g, sgu_ln_b, sgu_w_s, sgu_b_s, w_mem_kv, w_out, ln_mix_g, ln_mix_b, w_gate, w_up, w_down, ln_ffn_g, ln_ffn_b, loss_target, m_a_w_in, m_b_w_in, m_sgu_ln_g, m_sgu_ln_b, m_sgu_w_s, m_sgu_b_s, m_w_mem_kv, m_w_out, m_ln_mix_g, m_ln_mix_b, m_w_gate, m_w_up, m_w_down, m_ln_ffn_g, m_ln_ffn_b, v_a_w_in, v_b_w_in, v_sgu_ln_g, v_sgu_ln_b, v_sgu_w_s, v_sgu_b_s, v_w_mem_kv, v_w_out, v_ln_mix_g, v_ln_mix_b, v_w_gate, v_w_up, v_w_down, v_ln_ffn_g, v_ln_ffn_b):
  weights = dict(a_w_in=a_w_in, b_w_in=b_w_in, sgu_ln_g=sgu_ln_g, sgu_ln_b=sgu_ln_b, sgu_w_s=sgu_w_s, sgu_b_s=sgu_b_s,
                 w_mem_kv=w_mem_kv, w_out=w_out, ln_mix_g=ln_mix_g, ln_mix_b=ln_mix_b, w_gate=w_gate, w_up=w_up,
                 w_down=w_down, ln_ffn_g=ln_ffn_g, ln_ffn_b=ln_ffn_b)
  mom1 = dict(a_w_in=m_a_w_in, b_w_in=m_b_w_in, sgu_ln_g=m_sgu_ln_g, sgu_ln_b=m_sgu_ln_b, sgu_w_s=m_sgu_w_s,
              sgu_b_s=m_sgu_b_s, w_mem_kv=m_w_mem_kv, w_out=m_w_out, ln_mix_g=m_ln_mix_g, ln_mix_b=m_ln_mix_b,
              w_gate=m_w_gate, w_up=m_w_up, w_down=m_w_down, ln_ffn_g=m_ln_ffn_g, ln_ffn_b=m_ln_ffn_b)
  mom2 = dict(a_w_in=v_a_w_in, b_w_in=v_b_w_in, sgu_ln_g=v_sgu_ln_g, sgu_ln_b=v_sgu_ln_b, sgu_w_s=v_sgu_w_s,
              sgu_b_s=v_sgu_b_s, w_mem_kv=v_w_mem_kv, w_out=v_w_out, ln_mix_g=v_ln_mix_g, ln_mix_b=v_ln_mix_b,
              w_gate=v_w_gate, w_up=v_w_up, w_down=v_w_down, ln_ffn_g=v_ln_ffn_g, ln_ffn_b=v_ln_ffn_b)

  full = _gather_weights(weights)
  for n in REPLICATED:
    full[n] = weights[n]
  loss_part, grad_x, grads = _local_step(x, mem, loss_target, full)
  loss = lax.psum(loss_part, MESH_AXES)

  shard_shapes = {n: weights[n].shape for n, _ in SHARDED}
  red = _reduce_grads(grads, shard_shapes)
  chip = 2 * lax.axis_index("x") + lax.axis_index("y")
  for n, axis in SMALL_SHARDED:
    width = weights[n].shape[axis]
    red[n] = lax.dynamic_slice_in_dim(red[n], chip * width, width, axis)

  small_names = SMALL_ORDER
  def pack(d):
    flat = jnp.concatenate([d[n].reshape(-1) for n in small_names])
    return _pad_to(flat, _round_up(flat.shape[0], 8 * FLAT_COLS)).reshape(-1, FLAT_COLS)
  small_out = _adamw(pack(weights), pack(red), pack(mom1), pack(mom2), "adamw_small")
  delta, new_m, new_v = {}, {}, {}
  off = 0
  for n in small_names:
    sz = weights[n].size
    for dst, src in zip((delta, new_m, new_v), small_out):
      dst[n] = src.reshape(-1)[off:off + sz].reshape(weights[n].shape)
    off += sz
  for n, _ in SHARDED:
    delta[n], new_m[n], new_v[n] = _adamw(weights[n], red[n], mom1[n], mom2[n], f"adamw_{n}")

  return (loss, grad_x, *[red[n] for n in WEIGHT_NAMES], *[delta[n] for n in WEIGHT_NAMES],
          *[new_m[n] for n in WEIGHT_NAMES], *[new_v[n] for n in WEIGHT_NAMES])
```

```python
import functools
import math

import jax
import jax.numpy as jnp
from jax import lax
from jax.experimental import pallas as pl
from jax.experimental.pallas import tpu as pltpu

F32 = jnp.float32
BF16 = jnp.bfloat16

DEPTH = 4
HEAD_DIM = 64
N_DIL_HEADS = 12
DIL_WIDTH = N_DIL_HEADS * HEAD_DIM
DIL_PATTERNS = ((128, 1), (512, 4), (2048, 16))
BLOCK = 128
N_SGU_GROUPS = 12
SGU_WIDTH = N_SGU_GROUPS * 64
CHUNK = 128
N_MEM_HEADS = 4
MEM_WIDTH = N_MEM_HEADS * HEAD_DIM
DN_ALPHA = (2 * DEPTH) ** 0.25
LN_EPS = 1e-5
ATT_SCALE = HEAD_DIM ** -0.5
ADAM_LR = 0.001
ADAM_B1 = 0.9
ADAM_B2 = 0.999
ADAM_EPS = 1e-08
ADAM_WD = 0.01
ADAM_STEP = 10
NEG_BIG = -1e30

LANES = 128
FLAT_COLS = 1024
VMEM_LIMIT = 56 * 1024 * 1024
MESH_AXES = ("x", "y", "c")
MESH_ID = pl.DeviceIdType.MESH


def _tile(n, pref, align=LANES):
  if n <= pref:
    return n
  t = (pref // align) * align
  while t >= align:
    if n % t == 0:
      return t
    t -= align
  return n


def _params(sem):
  return pltpu.CompilerParams(dimension_semantics=sem, vmem_limit_bytes=VMEM_LIMIT)


def _dot(a, b):
  return jnp.dot(a, b, preferred_element_type=F32)


def _dot_nt(a, b):
  return lax.dot_general(a, b, (((1,), (1,)), ((), ())), preferred_element_type=F32)


def _dot_tn(a, b):
  return lax.dot_general(a, b, (((0,), (0,)), ((), ())), preferred_element_type=F32)


def _bf(v):
  return v.astype(BF16)


def _ln_stats(z):
  mu = jnp.mean(z, axis=-1, keepdims=True)
  zc = z - mu
  var = jnp.mean(zc * zc, axis=-1, keepdims=True)
  rstd = lax.rsqrt(var + LN_EPS)
  return zc * rstd, rstd


def _ln_bwd(dy, xhat, rstd, g):
  gdy = dy * g
  m1 = jnp.mean(gdy, axis=-1, keepdims=True)
  m2 = jnp.mean(gdy * xhat, axis=-1, keepdims=True)
  return rstd * (gdy - m1 - xhat * m2)


_GELU_C = math.sqrt(2.0 / math.pi)


def _gelu_parts(v):
  v2 = v * v
  t = jnp.tanh(_GELU_C * (v + 0.044715 * v * v2))
  val = 0.5 * v * (1.0 + t)
  der = 0.5 * (1.0 + t) + 0.5 * v * (1.0 - t * t) * (_GELU_C * (1.0 + 3.0 * 0.044715 * v2))
  return val, der


def _gelu(v):
  t = jnp.tanh(_GELU_C * (v + 0.044715 * v * v * v))
  return 0.5 * v * (1.0 + t)


def _sigmoid(v):
  return 1.0 / (1.0 + jnp.exp(-v))


def _mm(a, b, mode, out_dtype, name, add=None, add_scale=1.0, tm=512, tn=512, tk=512):
  if mode == "nn":
    (m, k), (k2, n) = a.shape, b.shape
  elif mode == "nt":
    (m, k), (n, k2) = a.shape, b.shape
  else:
    (k, m), (k2, n) = a.shape, b.shape
  assert k == k2, (a.shape, b.shape, mode)
  tm, tn, tk = _tile(m, tm), _tile(n, tn), _tile(k, tk)
  nk = k // tk
  if mode == "nn":
    a_spec = pl.BlockSpec((tm, tk), lambda i, j, kk: (i, kk))
    b_spec = pl.BlockSpec((tk, tn), lambda i, j, kk: (kk, j))
    dot = _dot
  elif mode == "nt":
    a_spec = pl.BlockSpec((tm, tk), lambda i, j, kk: (i, kk))
    b_spec = pl.BlockSpec((tn, tk), lambda i, j, kk: (j, kk))
    dot = _dot_nt
  else:
    a_spec = pl.BlockSpec((tk, tm), lambda i, j, kk: (kk, i))
    b_spec = pl.BlockSpec((tk, tn), lambda i, j, kk: (kk, j))
    dot = _dot_tn
  o_spec = pl.BlockSpec((tm, tn), lambda i, j, kk: (i, j))
  has_add = add is not None

  def body(*refs):
    if has_add:
      a_ref, b_ref, add_ref, o_ref, acc_ref = refs
    else:
      a_ref, b_ref, o_ref, acc_ref = refs
    kk = pl.program_id(2)

    @pl.when(kk == 0)
    def _():
      acc_ref[...] = jnp.zeros_like(acc_ref)

    acc_ref[...] += dot(_bf(a_ref[...]), _bf(b_ref[...]))

    @pl.when(kk == nk - 1)
    def _():
      r = acc_ref[...]
      if has_add:
        r = r + add_scale * add_ref[...].astype(F32)
      o_ref[...] = r.astype(out_dtype)

  in_specs = [a_spec, b_spec] + ([o_spec] if has_add else [])
  args = (a, b) + ((add,) if has_add else ())
  return pl.pallas_call(
      body, name=name, grid=(m // tm, n // tn, nk), in_specs=in_specs, out_specs=o_spec,
      out_shape=jax.ShapeDtypeStruct((m, n), out_dtype),
      scratch_shapes=[pltpu.VMEM((tm, tn), F32)],
      compiler_params=_params(("parallel", "parallel", "arbitrary")),
  )(*args)


def _mm_res_ln(a, w, res, g, b, name, tm=512, tk=512):
  m, k = a.shape
  d = w.shape[1]
  tm, tk = _tile(m, tm), _tile(k, tk)
  nk = k // tk

  def body(a_ref, w_ref, r_ref, g_ref, b_ref, z_ref, x_ref, xb_ref, acc_ref):
    kk = pl.program_id(1)

    @pl.when(kk == 0)
    def _():
      acc_ref[...] = jnp.zeros_like(acc_ref)

    acc_ref[...] += _dot(_bf(a_ref[...]), _bf(w_ref[...]))

    @pl.when(kk == nk - 1)
    def _():
      z = DN_ALPHA * r_ref[...] + acc_ref[...]
      xhat, _ = _ln_stats(z)
      xn = xhat * g_ref[...] + b_ref[...]
      z_ref[...] = z
      x_ref[...] = xn
      xb_ref[...] = _bf(xn)

  row = pl.BlockSpec((tm, d), lambda i, kk: (i, 0))
  vec = pl.BlockSpec((1, d), lambda i, kk: (0, 0))
  return pl.pallas_call(
      body, name=name, grid=(m // tm, nk),
      in_specs=[pl.BlockSpec((tm, tk), lambda i, kk: (i, kk)), pl.BlockSpec((tk, d), lambda i, kk: (kk, 0)), row, vec, vec],
      out_specs=[row, row, row],
      out_shape=[jax.ShapeDtypeStruct((m, d), F32), jax.ShapeDtypeStruct((m, d), F32), jax.ShapeDtypeStruct((m, d), BF16)],
      scratch_shapes=[pltpu.VMEM((tm, d), F32)],
      compiler_params=_params(("parallel", "arbitrary")),
  )(a, w, res, g.reshape(1, d), b.reshape(1, d))


def _ln_bwd_call(dy, z, g, name, tm=512):
  m, d = z.shape
  tm = _tile(m, tm)
  n = m // tm

  def body(dy_ref, z_ref, g_ref, dz_ref, dzb_ref, dg_ref, db_ref):
    i = pl.program_id(0)

    @pl.when(i == 0)
    def _():
      dg_ref[...] = jnp.zeros_like(dg_ref)
      db_ref[...] = jnp.zeros_like(db_ref)

    dy_v = dy_ref[...]
    xhat, rstd = _ln_stats(z_ref[...])
    dz = _ln_bwd(dy_v, xhat, rstd, g_ref[...])
    dz_ref[...] = dz
    dzb_ref[...] = _bf(dz)
    dg_ref[...] += jnp.sum(dy_v * xhat, axis=0, keepdims=True)
    db_ref[...] += jnp.sum(dy_v, axis=0, keepdims=True)

  row = pl.BlockSpec((tm, d), lambda i: (i, 0))
  vec = pl.BlockSpec((1, d), lambda i: (0, 0))
  dz, dzb, dg, db = pl.pallas_call(
      body, name=name, grid=(n,), in_specs=[row, row, vec], out_specs=[row, row, vec, vec],
      out_shape=[jax.ShapeDtypeStruct((m, d), F32), jax.ShapeDtypeStruct((m, d), BF16),
                 jax.ShapeDtypeStruct((1, d), F32), jax.ShapeDtypeStruct((1, d), F32)],
      compiler_params=_params(("arbitrary",)),
  )(dy, z, g.reshape(1, d))
  return dz, dzb, dg[0], db[0]


def _ffn_up(xb, wg, wu, name, tm=1024, tn=256):
  m, d = xb.shape
  f = wg.shape[1]
  tm, tn = _tile(m, tm), _tile(f, tn)

  def body(x_ref, wg_ref, wu_ref, a_ref, b_ref, h_ref):
    xv = x_ref[...]
    a = _dot(xv, wg_ref[...])
    b = _dot(xv, wu_ref[...])
    a_ref[...] = _bf(a)
    b_ref[...] = _bf(b)
    h_ref[...] = _bf(a * _sigmoid(a) * b)

  wspec = pl.BlockSpec((d, tn), lambda i, j: (0, j))
  ospec = pl.BlockSpec((tm, tn), lambda i, j: (i, j))
  sds = jax.ShapeDtypeStruct((m, f), BF16)
  return pl.pallas_call(
      body, name=name, grid=(m // tm, f // tn),
      in_specs=[pl.BlockSpec((tm, d), lambda i, j: (i, 0)), wspec, wspec],
      out_specs=[ospec, ospec, ospec], out_shape=[sds, sds, sds],
      compiler_params=_params(("parallel", "parallel")),
  )(xb, wg, wu)


def _ffn_bwd_hidden(dzb, wd, a, b, name, tm=1024, tn=256):
  m, d = dzb.shape
  f = wd.shape[0]
  tm, tn = _tile(m, tm), _tile(f, tn)

  def body(dz_ref, wd_ref, a_ref, b_ref, da_ref, db_ref):
    dh = _dot_nt(dz_ref[...], wd_ref[...])
    av = a_ref[...].astype(F32)
    bv = b_ref[...].astype(F32)
    sg = _sigmoid(av)
    da_ref[...] = _bf(dh * bv * (sg * (1.0 + av * (1.0 - sg))))
    db_ref[...] = _bf(dh * (av * sg))

  hspec = pl.BlockSpec((tm, tn), lambda i, j: (i, j))
  sds = jax.ShapeDtypeStruct((m, f), BF16)
  return pl.pallas_call(
      body, name=name, grid=(m // tm, f // tn),
      in_specs=[pl.BlockSpec((tm, d), lambda i, j: (i, 0)), pl.BlockSpec((tn, d), lambda i, j: (j, 0)), hspec, hspec],
      out_specs=[hspec, hspec], out_shape=[sds, sds],
      compiler_params=_params(("parallel", "parallel")),
  )(dzb, wd, a, b)


def _ffn_bwd_input(da, db, wg, wu, dz, name, tm=512, tk=256):
  m, f = da.shape
  d = wg.shape[0]
  tm, tk = _tile(m, tm), _tile(f, tk)
  nk = f // tk

  def body(da_ref, db_ref, wg_ref, wu_ref, dz_ref, o_ref, acc_ref):
    kk = pl.program_id(1)

    @pl.when(kk == 0)
    def _():
      acc_ref[...] = jnp.zeros_like(acc_ref)

    acc_ref[...] += _dot_nt(da_ref[...], wg_ref[...]) + _dot_nt(db_ref[...], wu_ref[...])

    @pl.when(kk == nk - 1)
    def _():
      o_ref[...] = DN_ALPHA * dz_ref[...] + acc_ref[...]

  hspec = pl.BlockSpec((tm, tk), lambda i, kk: (i, kk))
  wspec = pl.BlockSpec((d, tk), lambda i, kk: (0, kk))
  row = pl.BlockSpec((tm, d), lambda i, kk: (i, 0))
  return pl.pallas_call(
      body, name=name, grid=(m // tm, nk), in_specs=[hspec, hspec, wspec, wspec, row], out_specs=row,
      out_shape=jax.ShapeDtypeStruct((m, d), F32), scratch_shapes=[pltpu.VMEM((tm, d), F32)],
      compiler_params=_params(("parallel", "arbitrary")),
  )(da, db, wg, wu, dz)


def _alibi_slopes():
  n = N_DIL_HEADS
  return jnp.exp2(-8.0 * (jnp.arange(n, dtype=F32) + 1.0) / n).reshape(1, n)


def _band_consts():
  qi = lax.broadcasted_iota(jnp.int32, (BLOCK, BLOCK), 0)
  ki = lax.broadcasted_iota(jnp.int32, (BLOCK, BLOCK), 1)
  steps_cur = (qi - ki).astype(F32)
  steps_prev = (qi + BLOCK - ki).astype(F32)
  return ki < 64, steps_cur, steps_prev, ki <= qi, ki >= qi


def _rows(start, d):
  if d == 1:
    return pl.ds(pl.multiple_of(start, BLOCK), BLOCK)
  return pl.ds(start, BLOCK, stride=d)


def _attn_fwd(h3, name):
  bl, s, _ = h3.shape
  npair = N_DIL_HEADS // 2

  def body(sl_ref, q_ref, k_ref, v_ref, o_ref, lse_ref, o_sc, l_sc):
    hp = pl.program_id(1)
    head0, steps_cur, steps_prev, mask_cur, mask_prev = _band_consts()
    slope = [sl_ref[0, 2 * hp], sl_ref[0, 2 * hp + 1]]

    for p, (_, d) in enumerate(DIL_PATTERNS):
      nblk = (s // d) // BLOCK
      has_prev_block = nblk > 1

      def blk(idx, carry, p=p, d=d, nblk=nblk, has_prev_block=has_prev_block):
        r = idx // nblk
        n = idx % nblk
        cur = _rows(r + n * (BLOCK * d), d)
        q2 = q_ref[cur, :]
        kc = _bf(k_ref[cur, :])
        vc = _bf(v_ref[cur, :])
        if has_prev_block:
          prev = _rows(r + jnp.maximum(n - 1, 0) * (BLOCK * d), d)
          kp = _bf(k_ref[prev, :])
          vp = _bf(v_ref[prev, :])
          first_block = jnp.where(n > 0, 0.0, NEG_BIG)
        outs, lses = [], []
        for j in range(2):
          hm = head0 if j == 0 else jnp.logical_not(head0)
          qj = _bf(jnp.where(hm, q2, 0.0) * ATT_SCALE)
          sc = _dot_nt(qj, kc) - (slope[j] * d) * steps_cur
          sc = jnp.where(mask_cur, sc, NEG_BIG)
          mx = jnp.max(sc, axis=1, keepdims=True)
          if has_prev_block:
            sp = _dot_nt(qj, kp) - (slope[j] * d) * steps_prev + first_block
            sp = jnp.where(mask_prev, sp, NEG_BIG)
            mx = jnp.maximum(mx, jnp.max(sp, axis=1, keepdims=True))
          pc = jnp.exp(sc - mx)
          den = jnp.sum(pc, axis=1, keepdims=True)
          acc = _dot(_bf(pc), vc)
          if has_prev_block:
            pp = jnp.exp(sp - mx)
            den = den + jnp.sum(pp, axis=1, keepdims=True)
            acc = acc + _dot(_bf(pp), vp)
          outs.append(acc / den)
          lses.append(mx + jnp.log(den))
        o_sc[p, cur, :] = jnp.where(head0, outs[0], outs[1])
        l_sc[p, cur, :] = jnp.where(head0, lses[0], lses[1])
        return carry

      lax.fori_loop(0, s // BLOCK, blk, 0)

    def merge(i, carry):
      rows = pl.ds(pl.multiple_of(i * BLOCK, BLOCK), BLOCK)
      l0, l1, l2 = l_sc[0, rows, :], l_sc[1, rows, :], l_sc[2, rows, :]
      mx = jnp.maximum(jnp.maximum(l0, l1), l2)
      e0, e1, e2 = jnp.exp(l0 - mx), jnp.exp(l1 - mx), jnp.exp(l2 - mx)
      tot = e0 + e1 + e2
      o_ref[rows, :] = (e0 * o_sc[0, rows, :] + e1 * o_sc[1, rows, :] + e2 * o_sc[2, rows, :]) / tot
      lse_ref[rows, :] = mx + jnp.log(tot)
      return carry

    lax.fori_loop(0, s // BLOCK, merge, 0)

  def col(off):
    return pl.BlockSpec((None, s, LANES), lambda b, p: (b, 0, off + p))

  sds = jax.ShapeDtypeStruct((bl, s, DIL_WIDTH), F32)
  return pl.pallas_call(
      body, name=name, grid=(bl, npair),
      in_specs=[pl.BlockSpec(memory_space=pltpu.SMEM), col(0), col(npair), col(2 * npair)],
      out_specs=[col(0), col(0)], out_shape=[sds, sds],
      scratch_shapes=[pltpu.VMEM((3, s, LANES), F32), pltpu.VMEM((3, s, LANES), F32)],
      compiler_params=_params(("parallel", "parallel")),
  )(_alibi_slopes(), h3, h3, h3)


def _attn_bwd(h3, out3, lse3, dcat3, name):
  bl, s, _ = h3.shape
  npair = N_DIL_HEADS // 2

  def body(sl_ref, q_ref, k_ref, v_ref, o_ref, l_ref, do_ref, dq_ref, dk_ref, dv_ref):
    hp = pl.program_id(1)
    head0, steps_cur, steps_prev, mask_cur, mask_prev = _band_consts()
    lane = lax.broadcasted_iota(jnp.int32, (BLOCK, LANES), 1)
    slope = [sl_ref[0, 2 * hp], sl_ref[0, 2 * hp + 1]]
    dq_ref[...] = jnp.zeros_like(dq_ref)
    dk_ref[...] = jnp.zeros_like(dk_ref)
    dv_ref[...] = jnp.zeros_like(dv_ref)

    for p, (_, d) in enumerate(DIL_PATTERNS):
      nblk = (s // d) // BLOCK
      has_prev_block = nblk > 1

      def blk(idx, carry, d=d, nblk=nblk, has_prev_block=has_prev_block):
        r = idx // nblk
        n = idx % nblk
        cur = _rows(r + n * (BLOCK * d), d)
        q2 = q_ref[cur, :]
        do2 = do_ref[cur, :]
        l2 = l_ref[cur, :]
        prod = do2 * o_ref[cur, :]
        kc = _bf(k_ref[cur, :])
        vc = _bf(v_ref[cur, :])
        if has_prev_block:
          prev = _rows(r + jnp.maximum(n - 1, 0) * (BLOCK * d), d)
          kp = _bf(k_ref[prev, :])
          vp = _bf(v_ref[prev, :])
          first_block = jnp.where(n > 0, 0.0, NEG_BIG)
          dkp = jnp.zeros((BLOCK, LANES), F32)
          dvp = jnp.zeros((BLOCK, LANES), F32)
        dq2 = jnp.zeros((BLOCK, LANES), F32)
        dkc = jnp.zeros((BLOCK, LANES), F32)
        dvc = jnp.zeros((BLOCK, LANES), F32)
        for j in range(2):
          hm = head0 if j == 0 else jnp.logical_not(head0)
          qj = _bf(jnp.where(hm, q2, 0.0) * ATT_SCALE)
          doj = _bf(jnp.where(hm, do2, 0.0))
          lj = jnp.sum(jnp.where(lane == 64 * j, l2, 0.0), axis=1, keepdims=True)
          dj = jnp.sum(jnp.where(hm, prod, 0.0), axis=1, keepdims=True)
          sc = _dot_nt(qj, kc) - (slope[j] * d) * steps_cur
          pc = jnp.exp(jnp.where(mask_cur, sc - lj, NEG_BIG))
          dsc = _bf(pc * (_dot_nt(doj, vc) - dj))
          dq_j = _dot(dsc, kc)
          dkc = dkc + _dot_tn(dsc, qj)
          dvc = dvc + _dot_tn(_bf(pc), doj)
          if has_prev_block:
            sp = _dot_nt(qj, kp) - (slope[j] * d) * steps_prev + first_block
            pp = jnp.exp(jnp.where(mask_prev, sp - lj, NEG_BIG))
            dsp = _bf(pp * (_dot_nt(doj, vp) - dj))
            dq_j = dq_j + _dot(dsp, kp)
            dkp = dkp + _dot_tn(dsp, qj)
            dvp = dvp + _dot_tn(_bf(pp), doj)
          dq2 = dq2 + jnp.where(hm, dq_j, 0.0) * ATT_SCALE
        dq_ref[cur, :] += dq2
        dk_ref[cur, :] += dkc
        dv_ref[cur, :] += dvc
        if has_prev_block:
          dk_ref[prev, :] += dkp
          dv_ref[prev, :] += dvp
        return carry

      lax.fori_loop(0, s // BLOCK, blk, 0)

  def col(off):
    return pl.BlockSpec((None, s, LANES), lambda b, p: (b, 0, off + p))

  sds = jax.ShapeDtypeStruct((bl, s, DIL_WIDTH), F32)
  return pl.pallas_call(
      body, name=name, grid=(bl, npair),
      in_specs=[pl.BlockSpec(memory_space=pltpu.SMEM), col(0), col(npair), col(2 * npair), col(0), col(0), col(0)],
      out_specs=[col(0), col(0), col(0)], out_shape=[sds, sds, sds],
      compiler_params=_params(("parallel", "parallel")),
  )(_alibi_slopes(), h3, h3, h3, out3, lse3, dcat3)


def _mem_heads(tq):
  lane = lax.broadcasted_iota(jnp.int32, (tq, LANES), 1)
  return lane < 64


def _mem_fwd(h3, qcol, mkv3, name, tq=512):
  bl, s, _ = h3.shape
  nm = mkv3.shape[1]
  tq = _tile(s, tq)

  def body(q_ref, kv_ref, o_ref):
    head0 = _mem_heads(tq)
    for lg in range(MEM_WIDTH // LANES):
      cs = slice(lg * LANES, (lg + 1) * LANES)
      q2 = q_ref[:, cs]
      mk = _bf(kv_ref[:, cs])
      mv = _bf(kv_ref[:, MEM_WIDTH + lg * LANES:MEM_WIDTH + (lg + 1) * LANES])
      outs = []
      for j in range(2):
        hm = head0 if j == 0 else jnp.logical_not(head0)
        qj = _bf(jnp.where(hm, q2, 0.0) * ATT_SCALE)
        sc = _dot_nt(qj, mk)
        mx = jnp.max(sc, axis=1, keepdims=True)
        pe = jnp.exp(sc - mx)
        den = jnp.sum(pe, axis=1, keepdims=True)
        outs.append(_dot(_bf(pe / den), mv))
      o_ref[:, cs] = jnp.where(head0, outs[0], outs[1])

  return pl.pallas_call(
      body, name=name, grid=(bl, s // tq),
      in_specs=[pl.BlockSpec((None, tq, MEM_WIDTH), lambda b, i: (b, i, qcol)),
                pl.BlockSpec((None, nm, 2 * MEM_WIDTH), lambda b, i: (b, 0, 0))],
      out_specs=pl.BlockSpec((None, tq, MEM_WIDTH), lambda b, i: (b, i, 0)),
      out_shape=jax.ShapeDtypeStruct((bl, s, MEM_WIDTH), F32),
      compiler_params=_params(("parallel", "parallel")),
  )(h3, mkv3)


def _mem_bwd(h3, qcol, mkv3, dcat3, name, tq=512):
  bl, s, _ = h3.shape
  nm = mkv3.shape[1]
  tq = _tile(s, tq)
  docol = dcat3.shape[2] // MEM_WIDTH - 1

  def body(q_ref, kv_ref, do_ref, dq_ref, dkv_ref):
    i = pl.program_id(1)

    @pl.when(i == 0)
    def _():
      dkv_ref[...] = jnp.zeros_like(dkv_ref)

    head0 = _mem_heads(tq)
    for lg in range(MEM_WIDTH // LANES):
      cs = slice(lg * LANES, (lg + 1) * LANES)
      vs = slice(MEM_WIDTH + lg * LANES, MEM_WIDTH + (lg + 1) * LANES)
      q2 = q_ref[:, cs]
      do2 = do_ref[:, cs]
      mk = _bf(kv_ref[:, cs])
      mv = _bf(kv_ref[:, vs])
      dq2 = jnp.zeros((tq, LANES), F32)
      dmk = jnp.zeros((nm, LANES), F32)
      dmv = jnp.zeros((nm, LANES), F32)
      for j in range(2):
        hm = head0 if j == 0 else jnp.logical_not(head0)
        qj = _bf(jnp.where(hm, q2, 0.0) * ATT_SCALE)
        doj = _bf(jnp.where(hm, do2, 0.0))
        sc = _dot_nt(qj, mk)
        mx = jnp.max(sc, axis=1, keepdims=True)
        pe = jnp.exp(sc - mx)
        pn = pe / jnp.sum(pe, axis=1, keepdims=True)
        pb = _bf(pn)
        dp = _dot_nt(doj, mv)
        dj = jnp.sum(pb.astype(F32) * dp, axis=1, keepdims=True)
        ds = _bf(pn * (dp - dj))
        dq2 = dq2 + jnp.where(hm, _dot(ds, mk), 0.0) * ATT_SCALE
        dmk = dmk + _dot_tn(ds, qj)
        dmv = dmv + _dot_tn(pb, doj)
      dq_ref[:, cs] = dq2
      dkv_ref[:, cs] += dmk
      dkv_ref[:, vs] += dmv

  return pl.pallas_call(
      body, name=name, grid=(bl, s // tq),
      in_specs=[pl.BlockSpec((None, tq, MEM_WIDTH), lambda b, i: (b, i, qcol)),
                pl.BlockSpec((None, nm, 2 * MEM_WIDTH), lambda b, i: (b, 0, 0)),
                pl.BlockSpec((None, tq, MEM_WIDTH), lambda b, i: (b, i, docol))],
      out_specs=[pl.BlockSpec((None, tq, MEM_WIDTH), lambda b, i: (b, i, 0)),
                 pl.BlockSpec((None, nm, 2 * MEM_WIDTH), lambda b, i: (b, 0, 0))],
      out_shape=[jax.ShapeDtypeStruct((bl, s, MEM_WIDTH), F32), jax.ShapeDtypeStruct((bl, nm, 2 * MEM_WIDTH), F32)],
      compiler_params=_params(("parallel", "arbitrary")),
  )(h3, mkv3, dcat3)


def _sgu_consts():
  ti = lax.broadcasted_iota(jnp.int32, (CHUNK, CHUNK), 0)
  si = lax.broadcasted_iota(jnp.int32, (CHUNK, CHUNK), 1)
  return si <= ti, si < 64


def _sgu_bias_lanes(b_s):
  return jnp.repeat(b_s.T, 64, axis=1)


def _sgu_fwd(h2, ln_g, ln_b, w_s, b_s, name, tr=512):
  t, _ = h2.shape
  tr = _tile(t, tr)
  nch = tr // CHUNK
  npair = N_SGU_GROUPS // 2

  def body(u_ref, v_ref, g_ref, b_ref, w_ref, bs_ref, o_ref, vn_sc):
    tril, head0 = _sgu_consts()
    xhat, _ = _ln_stats(_gelu(v_ref[...]))
    vn_sc[...] = _bf(xhat * g_ref[...] + b_ref[...])
    for jp in range(npair):
      cs = slice(jp * LANES, (jp + 1) * LANES)
      w0 = _bf(jnp.where(tril, w_ref[2 * jp], 0.0))
      w1 = _bf(jnp.where(tril, w_ref[2 * jp + 1], 0.0))
      bias = bs_ref[:, cs]
      for c in range(nch):
        rs = slice(c * CHUNK, (c + 1) * CHUNK)
        vb = vn_sc[rs, cs]
        mixed = jnp.where(head0, _dot(w0, vb), _dot(w1, vb)) + bias
        o_ref[rs, cs] = _gelu(u_ref[rs, cs]) * mixed

  blk = lambda j: pl.BlockSpec((tr, SGU_WIDTH), lambda i: (i, j))
  vec = pl.BlockSpec((1, SGU_WIDTH), lambda i: (0, 0))
  return pl.pallas_call(
      body, name=name, grid=(t // tr,),
      in_specs=[blk(0), blk(1), vec, vec,
                pl.BlockSpec((N_SGU_GROUPS, CHUNK, CHUNK), lambda i: (0, 0, 0)),
                pl.BlockSpec((CHUNK, SGU_WIDTH), lambda i: (0, 0))],
      out_specs=blk(0), out_shape=jax.ShapeDtypeStruct((t, SGU_WIDTH), F32),
      scratch_shapes=[pltpu.VMEM((tr, SGU_WIDTH), BF16)],
      compiler_params=_params(("parallel",)),
  )(h2, h2, ln_g.reshape(1, -1), ln_b.reshape(1, -1), w_s, _sgu_bias_lanes(b_s))


def _sgu_bwd(h2, dcat, ln_g, ln_b, w_s, b_s, name, tr=512):
  t, _ = h2.shape
  tr = _tile(t, tr)
  nch = tr // CHUNK
  npair = N_SGU_GROUPS // 2
  nsteps = t // tr

  def body(u_ref, v_ref, dm_ref, g_ref, b_ref, w_ref, bs_ref,
           du_ref, dv_ref, dw_ref, dbs_ref, dg_ref, db_ref, vn_sc, dmx_sc, dvn_sc, mix_sc, dbx_sc):
    i = pl.program_id(0)
    tril, head0 = _sgu_consts()

    @pl.when(i == 0)
    def _():
      dw_ref[...] = jnp.zeros_like(dw_ref)
      dg_ref[...] = jnp.zeros_like(dg_ref)
      db_ref[...] = jnp.zeros_like(db_ref)
      dbx_sc[...] = jnp.zeros_like(dbx_sc)

    gv, gv_der = _gelu_parts(v_ref[...])
    xhat, rstd = _ln_stats(gv)
    g = g_ref[...]
    vn_sc[...] = _bf(xhat * g + b_ref[...])
    gu, gu_der = _gelu_parts(u_ref[...])
    dmix = dm_ref[...]
    dmx_sc[...] = dmix * gu

    for jp in range(npair):
      cs = slice(jp * LANES, (jp + 1) * LANES)
      w0 = _bf(jnp.where(tril, w_ref[2 * jp], 0.0))
      w1 = _bf(jnp.where(tril, w_ref[2 * jp + 1], 0.0))
      bias = bs_ref[:, cs]
      dw0 = jnp.zeros((CHUNK, CHUNK), F32)
      dw1 = jnp.zeros((CHUNK, CHUNK), F32)
      dbx = jnp.zeros((CHUNK, LANES), F32)
      for c in range(nch):
        rs = slice(c * CHUNK, (c + 1) * CHUNK)
        vb = vn_sc[rs, cs]
        mix_sc[rs, cs] = jnp.where(head0, _dot(w0, vb), _dot(w1, vb)) + bias
        dmx = dmx_sc[rs, cs]
        d0 = _bf(jnp.where(head0, dmx, 0.0))
        d1 = _bf(jnp.where(head0, 0.0, dmx))
        dvn_sc[rs, cs] = _dot_tn(w0, d0) + _dot_tn(w1, d1)
        dw0 = dw0 + _dot_nt(d0, vb)
        dw1 = dw1 + _dot_nt(d1, vb)
        dbx = dbx + dmx
      dw_ref[2 * jp] += dw0
      dw_ref[2 * jp + 1] += dw1
      dbx_sc[:, cs] += dbx

    du_ref[...] = _bf(dmix * mix_sc[...] * gu_der)
    dvn = dvn_sc[...]
    dv_ref[...] = _bf(_ln_bwd(dvn, xhat, rstd, g) * gv_der)
    dg_ref[...] += jnp.sum(dvn * xhat, axis=0, keepdims=True)
    db_ref[...] += jnp.sum(dvn, axis=0, keepdims=True)

    @pl.when(i == nsteps - 1)
    def _():
      lane = lax.broadcasted_iota(jnp.int32, (CHUNK, LANES), 1)
      acc = jnp.zeros((CHUNK, LANES), F32)
      for gi in range(N_SGU_GROUPS):
        jp, j = gi // 2, gi % 2
        part = dbx_sc[:, jp * LANES:(jp + 1) * LANES]
        hm = (lane < 64) if j == 0 else (lane >= 64)
        colsum = jnp.sum(jnp.where(hm, part, 0.0), axis=1, keepdims=True)
        acc = jnp.where(lane == gi, colsum, acc)
        dw_ref[gi] = jnp.where(tril, dw_ref[gi], 0.0)
      dbs_ref[...] = acc

  blk = lambda j: pl.BlockSpec((tr, SGU_WIDTH), lambda i: (i, j))
  vec = pl.BlockSpec((1, SGU_WIDTH), lambda i: (0, 0))
  wspec = pl.BlockSpec((N_SGU_GROUPS, CHUNK, CHUNK), lambda i: (0, 0, 0))
  big = lambda dt: pltpu.VMEM((tr, SGU_WIDTH), dt)
  du, dv, dw, dbs, dg, db = pl.pallas_call(
      body, name=name, grid=(nsteps,),
      in_specs=[blk(0), blk(1), blk(0), vec, vec, wspec, pl.BlockSpec((CHUNK, SGU_WIDTH), lambda i: (0, 0))],
      out_specs=[blk(0), blk(0), wspec, pl.BlockSpec((CHUNK, LANES), lambda i: (0, 0)), vec, vec],
      out_shape=[jax.ShapeDtypeStruct((t, SGU_WIDTH), BF16), jax.ShapeDtypeStruct((t, SGU_WIDTH), BF16),
                 jax.ShapeDtypeStruct((N_SGU_GROUPS, CHUNK, CHUNK), F32), jax.ShapeDtypeStruct((CHUNK, LANES), F32),
                 jax.ShapeDtypeStruct((1, SGU_WIDTH), F32), jax.ShapeDtypeStruct((1, SGU_WIDTH), F32)],
      scratch_shapes=[big(BF16), big(F32), big(F32), big(F32), pltpu.VMEM((CHUNK, SGU_WIDTH), F32)],
      compiler_params=_params(("arbitrary",)),
  )(h2, h2, dcat, ln_g.reshape(1, -1), ln_b.reshape(1, -1), w_s, _sgu_bias_lanes(b_s))
  return du, dv, dw, dbs[:, :N_SGU_GROUPS].T, dg[0], db[0]


def _loss_head(xo, tgt, name, tm=512):
  m, d = xo.shape
  tm = _tile(m, tm)

  def body(x_ref, t_ref, dx_ref, l_ref):
    @pl.when(pl.program_id(0) == 0)
    def _():
      l_ref[...] = jnp.zeros_like(l_ref)

    diff = x_ref[...] - t_ref[...]
    dx_ref[...] = diff * (1.0 / d)
    rowsum = jnp.sum(diff * diff, axis=1, keepdims=True)
    tot = jnp.sum(rowsum, axis=0, keepdims=True) * (0.5 / d)
    l_ref[...] += jnp.broadcast_to(tot, l_ref.shape)

  row = pl.BlockSpec((tm, d), lambda i: (i, 0))
  dx, l = pl.pallas_call(
      body, name=name, grid=(m // tm,), in_specs=[row, row],
      out_specs=[row, pl.BlockSpec((8, LANES), lambda i: (0, 0))],
      out_shape=[jax.ShapeDtypeStruct((m, d), F32), jax.ShapeDtypeStruct((8, LANES), F32)],
      compiler_params=_params(("arbitrary",)),
  )(xo, tgt)
  return l[0, 0], dx


def _local_step(x3, mem3, tgt3, w):
  bl, s, d = x3.shape
  t = bl * s
  nm = mem3.shape[1]
  mem2 = mem3.reshape(bl * nm, d)
  x = x3.reshape(t, d)
  xb = x
  saved = []
  for i in range(DEPTH):
    j = i // 2
    attn = i % 2 == 0
    mkv = _mm(mem2, w["w_mem_kv"][i], "nn", F32, f"mkv_fwd_{i}", tm=1024, tn=512, tk=1024)
    mkv3 = mkv.reshape(bl, nm, 2 * MEM_WIDTH)
    w_in = w["a_w_in"][j] if attn else w["b_w_in"][j]
    h = _mm(xb, w_in, "nn", F32, f"in_proj_{i}", tm=1024, tn=512, tk=1024)
    h3 = h.reshape(bl, s, -1)
    if attn:
      mix3, lse3 = _attn_fwd(h3, f"dil_attn_fwd_{i}")
      mix = mix3.reshape(t, DIL_WIDTH)
      qcol = 3 * DIL_WIDTH // MEM_WIDTH
    else:
      mix = _sgu_fwd(h, w["sgu_ln_g"][j], w["sgu_ln_b"][j], w["sgu_w_s"][j], w["sgu_b_s"][j], f"sgu_fwd_{i}")
      lse3 = None
      qcol = 2 * SGU_WIDTH // MEM_WIDTH
    mo = _mem_fwd(h3, qcol, mkv3, f"mem_attn_fwd_{i}").reshape(t, MEM_WIDTH)
    cat = jnp.concatenate([mix, mo], axis=1).astype(BF16)
    z1, xm, xmb = _mm_res_ln(cat, w["w_out"][i], x, w["ln_mix_g"][i], w["ln_mix_b"][i], f"out_proj_ln_{i}", tk=1024)
    a, b, hm = _ffn_up(xmb, w["w_gate"][i], w["w_up"][i], f"ffn_up_{i}")
    z2, xo, xob = _mm_res_ln(hm, w["w_down"][i], xm, w["ln_ffn_g"][i], w["ln_ffn_b"][i], f"ffn_down_ln_{i}", tk=1408)
    saved.append(dict(xb=xb, h=h, h3=h3, mkv3=mkv3, mix3=(mix3 if attn else None), lse3=lse3, cat=cat, z1=z1,
                      xmb=xmb, a=a, b=b, hm=hm, z2=z2, qcol=qcol))
    x, xb = xo, xob

  loss, dx = _loss_head(x, tgt3.reshape(t, d), "loss_head")

  names = ("a_w_in", "b_w_in", "sgu_ln_g", "sgu_ln_b", "sgu_w_s", "sgu_b_s", "w_mem_kv", "w_out",
           "ln_mix_g", "ln_mix_b", "w_gate", "w_up", "w_down", "ln_ffn_g", "ln_ffn_b")
  grads = {n: [None] * w[n].shape[0] for n in names}
  for i in reversed(range(DEPTH)):
    j = i // 2
    attn = i % 2 == 0
    sv = saved[i]
    dz2, dz2b, grads["ln_ffn_g"][i], grads["ln_ffn_b"][i] = _ln_bwd_call(dx, sv["z2"], w["ln_ffn_g"][i], f"ln_ffn_bwd_{i}")
    da, db = _ffn_bwd_hidden(dz2b, w["w_down"][i], sv["a"], sv["b"], f"ffn_bwd_hidden_{i}")
    grads["w_down"][i] = _mm(sv["hm"], dz2b, "tn", F32, f"dw_down_{i}", tm=1408, tn=1024, tk=1024)
    grads["w_gate"][i] = _mm(sv["xmb"], da, "tn", F32, f"dw_gate_{i}", tm=1024, tn=1408, tk=1024)
    grads["w_up"][i] = _mm(sv["xmb"], db, "tn", F32, f"dw_up_{i}", tm=1024, tn=1408, tk=1024)
    dxm = _ffn_bwd_input(da, db, w["w_gate"][i], w["w_up"][i], dz2, f"ffn_bwd_input_{i}")
    dz1, dz1b, grads["ln_mix_g"][i], grads["ln_mix_b"][i] = _ln_bwd_call(dxm, sv["z1"], w["ln_mix_g"][i], f"ln_mix_bwd_{i}")
    grads["w_out"][i] = _mm(sv["cat"], dz1b, "tn", F32, f"dw_out_{i}", tm=1024, tn=1024, tk=1024)
    dcat = _mm(dz1b, w["w_out"][i], "nt", F32, f"out_proj_bwd_{i}", tm=1024, tn=1024, tk=1024)
    dcat3 = dcat.reshape(bl, s, -1)
    dqm3, dmkv3 = _mem_bwd(sv["h3"], sv["qcol"], sv["mkv3"], dcat3, f"mem_attn_bwd_{i}")
    grads["w_mem_kv"][i] = _mm(mem2, dmkv3.reshape(bl * nm, 2 * MEM_WIDTH), "tn", F32, f"dw_mem_kv_{i}", tm=1024, tn=512, tk=1024)
    dqm = dqm3.reshape(t, MEM_WIDTH).astype(BF16)
    if attn:
      dq3, dk3, dv3 = _attn_bwd(sv["h3"], sv["mix3"], sv["lse3"], dcat3, f"dil_attn_bwd_{i}")
      parts = [dq3.reshape(t, -1).astype(BF16), dk3.reshape(t, -1).astype(BF16), dv3.reshape(t, -1).astype(BF16), dqm]
    else:
      du, dv, dws, dbs, dlg, dlb = _sgu_bwd(sv["h"], dcat, w["sgu_ln_g"][j], w["sgu_ln_b"][j], w["sgu_w_s"][j],
                                             w["sgu_b_s"][j], f"sgu_bwd_{i}")
      grads["sgu_w_s"][j], grads["sgu_b_s"][j], grads["sgu_ln_g"][j], grads["sgu_ln_b"][j] = dws, dbs, dlg, dlb
      parts = [du, dv, dqm]
    dh = jnp.concatenate(parts, axis=1)
    w_in = w["a_w_in"][j] if attn else w["b_w_in"][j]
    grads["a_w_in" if attn else "b_w_in"][j] = _mm(sv["xb"], dh, "tn", F32, f"dw_in_{i}", tm=1024, tn=896 if not attn else 640, tk=1024)
    dx = _mm(dh, w_in, "nt", F32, f"in_proj_bwd_{i}", add=dz1, add_scale=DN_ALPHA, tm=1024, tn=1024, tk=896 if not attn else 640)
  return loss, dx.reshape(bl, s, d), grads


def _my_place():
  return lax.axis_index("x"), lax.axis_index("y"), lax.axis_index("c")


def _other_chips(x, y):
  return [(1 - x, y), (x, 1 - y), (1 - x, 1 - y)]


ANY = pl.BlockSpec(memory_space=pl.ANY)


def _all_gather_halves(wl, name):
  _, r, c_ = wl.shape

  def body(w_ref, g_ref, send_sems, recv_sems):
    x, y, c = _my_place()
    me = 2 * x + y
    sibling = (x, y, 1 - c)
    chips = _other_chips(x, y)

    def copy(k, src, dst, to):
      return pltpu.make_async_remote_copy(src_ref=src, dst_ref=dst, send_sem=send_sems.at[k], recv_sem=recv_sems.at[k],
                                          device_id=to, device_id_type=MESH_ID)

    first = [copy(k, w_ref.at[c], g_ref.at[me, c], (px, py, c)) for k, (px, py) in enumerate(chips)]
    for cp in first:
      cp.start()
    passed = []
    for k, (px, py) in enumerate(chips):
      landed = g_ref.at[2 * px + py, c]
      copy(k, landed, landed, (px, py, c)).wait_recv()
      fwd = copy(3 + k, landed, landed, sibling)
      fwd.start()
      passed.append(fwd)
    for k, (px, py) in enumerate(chips):
      theirs = g_ref.at[2 * px + py, 1 - c]
      copy(3 + k, theirs, theirs, sibling).wait_recv()
    for cp in first + passed:
      cp.wait_send()

  got = pl.pallas_call(
      body, name=name, in_specs=[ANY], out_specs=ANY,
      out_shape=jax.ShapeDtypeStruct((4, 2, r, c_), wl.dtype),
      scratch_shapes=[pltpu.SemaphoreType.DMA((6,)), pltpu.SemaphoreType.DMA((6,))],
  )(wl)
  chip = 2 * lax.axis_index("x") + lax.axis_index("y")
  return lax.dynamic_update_slice(got, wl[None], (chip, 0, 0, 0))


def _sibling_swap(v, name):
  def body(v_ref, o_ref, send_sem, recv_sem):
    x, y, c = _my_place()
    cp = pltpu.make_async_remote_copy(src_ref=v_ref, dst_ref=o_ref, send_sem=send_sem, recv_sem=recv_sem,
                                      device_id=(x, y, 1 - c), device_id_type=MESH_ID)
    cp.start()
    cp.wait()

  return pl.pallas_call(
      body, name=name, in_specs=[ANY], out_specs=ANY, out_shape=jax.ShapeDtypeStruct(v.shape, v.dtype),
      scratch_shapes=[pltpu.SemaphoreType.DMA, pltpu.SemaphoreType.DMA],
  )(v)


def _chip_exchange(q, name):
  _, r, c_ = q.shape

  def body(q_ref, o_ref, send_sems, recv_sems):
    x, y, c = _my_place()
    cps = []
    for k, (px, py) in enumerate(_other_chips(x, y)):
      cp = pltpu.make_async_remote_copy(src_ref=q_ref.at[2 * px + py], dst_ref=o_ref.at[k], send_sem=send_sems.at[k],
                                        recv_sem=recv_sems.at[k], device_id=(px, py, c), device_id_type=MESH_ID)
      cp.start()
      cps.append(cp)
    for cp in cps:
      cp.wait()

  return pl.pallas_call(
      body, name=name, in_specs=[ANY], out_specs=ANY, out_shape=jax.ShapeDtypeStruct((3, r, c_), q.dtype),
      scratch_shapes=[pltpu.SemaphoreType.DMA((3,)), pltpu.SemaphoreType.DMA((3,))],
  )(q)


def _share_halves(v, name):
  theirs = _sibling_swap(v, name)
  c = lax.axis_index("c")
  return jnp.where(c == 0, jnp.concatenate([v, theirs]), jnp.concatenate([theirs, v]))


def _half_spec(tr, c_, pick):
  return pl.BlockSpec((None, None, tr, c_), lambda s, r, place: (s, pick(place), r, 0))


def _cast_other_half(p, place, name, tr=512):
  _, _, r, c_ = p.shape
  tr = _tile(r, tr, 16)

  def body(place_ref, p_ref, o_ref):
    o_ref[...] = _bf(p_ref[...])

  out_spec = pl.BlockSpec((None, tr, c_), lambda s, rr, place: (s, rr, 0))
  return pl.pallas_call(
      body, name=name, out_shape=jax.ShapeDtypeStruct((4, r, c_), BF16),
      grid_spec=pltpu.PrefetchScalarGridSpec(num_scalar_prefetch=1, grid=(4, r // tr),
                                             in_specs=[_half_spec(tr, c_, lambda place: 1 - place[1])], out_specs=out_spec),
      compiler_params=_params(("parallel", "parallel")),
  )(place, p)


def _add_sibling(p, x1, place, name, tr=512):
  _, _, r, c_ = p.shape
  tr = _tile(r, tr, 16)

  def body(place_ref, p_ref, x_ref, o_ref):
    o_ref[...] = _bf(p_ref[...] + x_ref[...].astype(F32))

  row = pl.BlockSpec((None, tr, c_), lambda s, rr, place: (s, rr, 0))
  return pl.pallas_call(
      body, name=name, out_shape=jax.ShapeDtypeStruct((4, r, c_), BF16),
      grid_spec=pltpu.PrefetchScalarGridSpec(num_scalar_prefetch=1, grid=(4, r // tr),
                                             in_specs=[_half_spec(tr, c_, lambda place: place[1]), row], out_specs=row),
      compiler_params=_params(("parallel", "parallel")),
  )(place, p, x1)


def _sum_own(p, x1, x3, place, name, tr=512):
  _, _, r, c_ = p.shape
  tr = _tile(r, tr, 16)

  def body(place_ref, p_ref, x1_ref, x3_ref, o_ref):
    acc = p_ref[...] + x1_ref[...].astype(F32)
    for k in range(3):
      acc = acc + x3_ref[k].astype(F32)
    o_ref[...] = acc

  return pl.pallas_call(
      body, name=name, out_shape=jax.ShapeDtypeStruct((r, c_), F32),
      grid_spec=pltpu.PrefetchScalarGridSpec(
          num_scalar_prefetch=1, grid=(r // tr,),
          in_specs=[pl.BlockSpec((None, None, tr, c_), lambda rr, place: (place[0], place[1], rr, 0)),
                    pl.BlockSpec((None, tr, c_), lambda rr, place: (place[0], rr, 0)),
                    pl.BlockSpec((3, tr, c_), lambda rr, place: (0, rr, 0))],
          out_specs=pl.BlockSpec((tr, c_), lambda rr, place: (rr, 0))),
      compiler_params=_params(("parallel",)),
  )(place, p, x1, x3)


def _reduce_scatter(p):
  x, y, c = _my_place()
  place = jnp.stack([2 * x + y, c]).astype(jnp.int32)
  x1 = _sibling_swap(_cast_other_half(p, place, "rs_cast_other_half"), "rs_sibling_swap")
  q = _add_sibling(p, x1, place, "rs_add_sibling")
  x3 = _chip_exchange(q, "rs_chip_exchange")
  mine = _sum_own(p, x1, x3, place, "rs_sum_own")
  return _share_halves(mine, "rs_share_halves")


def _adamw(w, g, m, v, name):
  shape = w.shape
  cols = shape[-1]
  rows = w.size // cols
  tr = _tile(rows, max(8, (256 * 1024) // cols // 8 * 8), 8)

  def body(w_ref, g_ref, m_ref, v_ref, d_ref, nm_ref, nv_ref):
    gv = g_ref[...]
    nm = ADAM_B1 * m_ref[...] + (1.0 - ADAM_B1) * gv
    nv = ADAM_B2 * v_ref[...] + (1.0 - ADAM_B2) * (gv * gv)
    m_hat = nm / (1.0 - ADAM_B1 ** ADAM_STEP)
    v_hat = nv / (1.0 - ADAM_B2 ** ADAM_STEP)
    d_ref[...] = -ADAM_LR * (m_hat / (jnp.sqrt(v_hat) + ADAM_EPS) + ADAM_WD * w_ref[...])
    nm_ref[...] = nm
    nv_ref[...] = nv

  spec = pl.BlockSpec((tr, cols), lambda i: (i, 0))
  sds = jax.ShapeDtypeStruct((rows, cols), F32)
  outs = pl.pallas_call(
      body, name=name, grid=(rows // tr,), in_specs=[spec] * 4, out_specs=[spec] * 3, out_shape=[sds] * 3,
      compiler_params=_params(("parallel",)),
  )(*(t.reshape(rows, cols) for t in (w, g, m, v)))
  return tuple(o.reshape(shape) for o in outs)


SHARDED = (("a_w_in", 2), ("b_w_in", 2), ("w_mem_kv", 1), ("w_out", 1), ("w_gate", 2), ("w_up", 2), ("w_down", 1))
SMALL_SHARDED = (("sgu_ln_g", 1), ("sgu_ln_b", 1))
REPLICATED = ("sgu_w_s", "sgu_b_s", "ln_mix_g", "ln_mix_b", "ln_ffn_g", "ln_ffn_b")
SMALL_ORDER = ("sgu_w_s", "sgu_b_s", "ln_mix_g", "ln_mix_b", "ln_ffn_g", "ln_ffn_b", "sgu_ln_g", "sgu_ln_b")
ROW_ALIGN = 16


def _pad_to(v, n):
  return jnp.pad(v, (0, n - v.shape[0]))


def _round_up(n, a):
  return -(-n // a) * a


def _to_shard_major(full, axis):
  shp = full.shape
  cut = shp[:axis] + (4, shp[axis] // 4) + shp[axis + 1:]
  return jnp.moveaxis(full.reshape(cut), axis, 0).reshape(4, -1, FLAT_COLS)


def _from_shard_major(rows, shard_shape, axis):
  full = jnp.moveaxis(rows.reshape((4,) + tuple(shard_shape)), 0, axis)
  shp = full.shape
  return full.reshape(shp[:axis] + (shp[axis] * shp[axis + 1],) + shp[axis + 2:])


def _gather_weights(shards):
  segs = [shards[n].astype(BF16).reshape(-1, FLAT_COLS) for n, _ in SHARDED]
  small = jnp.concatenate([lax.bitcast_convert_type(shards[n], BF16).reshape(-1) for n, _ in SMALL_SHARDED])
  small_rows = _round_up(small.shape[0], ROW_ALIGN * FLAT_COLS) // FLAT_COLS
  segs.append(_pad_to(small, small_rows * FLAT_COLS).reshape(small_rows, FLAT_COLS))
  rows = sum(sg.shape[0] for sg in segs)
  rows_pad = _round_up(rows, 2 * ROW_ALIGN)
  if rows_pad > rows:
    segs.append(jnp.zeros((rows_pad - rows, FLAT_COLS), BF16))
  flat = jnp.concatenate(segs).reshape(2, rows_pad // 2, FLAT_COLS)
  g = _all_gather_halves(flat, "gather_weights").reshape(4, rows_pad, FLAT_COLS)
  out, off = {}, 0
  for n, axis in SHARDED:
    nr = shards[n].size // FLAT_COLS
    out[n] = _from_shard_major(g[:, off:off + nr], shards[n].shape, axis)
    off += nr
  small_g = g[:, off:off + small_rows].reshape(4, small_rows * FLAT_COLS)
  off = 0
  for n, axis in SMALL_SHARDED:
    sz = 2 * shards[n].size
    vals = lax.bitcast_convert_type(small_g[:, off:off + sz].reshape((4,) + shards[n].shape + (2,)), F32)
    out[n] = _from_shard_major(vals, shards[n].shape, axis)
    off += sz
  return out


def _reduce_grads(grads, shard_shapes):
  segs = []
  for n, axis in SHARDED:
    for g in grads[n]:
      segs.append(_to_shard_major(g, axis - 1))
  big_rows = sum(sg.shape[1] for sg in segs)
  small_full = {n: jnp.stack(grads[n]) for n in SMALL_ORDER}
  small = jnp.concatenate([small_full[n].reshape(-1) for n in SMALL_ORDER])
  n_small = _round_up(small.shape[0], 4 * 2 * 8 * FLAT_COLS)
  quarter_rows = n_small // (4 * FLAT_COLS)
  segs.append(_pad_to(small, n_small).reshape(4, quarter_rows, FLAT_COLS))
  rows_pad = _round_up(big_rows + quarter_rows, 2 * ROW_ALIGN)
  if rows_pad > big_rows + quarter_rows:
    segs.append(jnp.zeros((4, rows_pad - big_rows - quarter_rows, FLAT_COLS), F32))
  p = jnp.concatenate(segs, axis=1).reshape(4, 2, rows_pad // 2, FLAT_COLS)
  mine = _reduce_scatter(p)
  out, off = {}, 0
  for n, _ in SHARDED:
    nr = math.prod(shard_shapes[n]) // FLAT_COLS
    out[n] = mine[off:off + nr].reshape(shard_shapes[n])
    off += nr
  piece = mine[big_rows:big_rows + quarter_rows].reshape(2, quarter_rows // 2, FLAT_COLS)
  small_sum = _all_gather_halves(piece, "gather_small_grads").reshape(n_small)
  off = 0
  for n in SMALL_ORDER:
    sz = small_full[n].size
    out[n] = small_sum[off:off + sz].reshape(small_full[n].shape)
    off += sz
  return out


WEIGHT_NAMES = ("a_w_in", "b_w_in", "sgu_ln_g", "sgu_ln_b", "sgu_w_s", "sgu_b_s", "w_mem_kv", "w_out",
                "ln_mix_g", "ln_mix_b", "w_gate", "w_up", "w_down", "ln_ffn_g", "ln_ffn_b")


def kernel(x, mem, a_w_in, b_w_in, sgu_ln_g, sgu_ln_b, sgu_w_s, sgu_b_s, w_mem_kv, w_out, ln_mix_g, ln_mix_b, w_gate, w_up, w_down, ln_ffn_g, ln_ffn_b, loss_target, m_a_w_in, m_b_w_in, m_sgu_ln_g, m_sgu_ln_b, m_sgu_w_s, m_sgu_b_s, m_w_mem_kv, m_w_out, m_ln_mix_g, m_ln_mix_b, m_w_gate, m_w_up, m_w_down, m_ln_ffn_g, m_ln_ffn_b, v_a_w_in, v_b_w_in, v_sgu_ln_g, v_sgu_ln_b, v_sgu_w_s, v_sgu_b_s, v_w_mem_kv, v_w_out, v_ln_mix_g, v_ln_mix_b, v_w_gate, v_w_up, v_w_down, v_ln_ffn_g, v_ln_ffn_b):
  weights = dict(a_w_in=a_w_in, b_w_in=b_w_in, sgu_ln_g=sgu_ln_g, sgu_ln_b=sgu_ln_b, sgu_w_s=sgu_w_s, sgu_b_s=sgu_b_s,
                 w_mem_kv=w_mem_kv, w_out=w_out, ln_mix_g=ln_mix_g, ln_mix_b=ln_mix_b, w_gate=w_gate, w_up=w_up,
                 w_down=w_down, ln_ffn_g=ln_ffn_g, ln_ffn_b=ln_ffn_b)
  mom1 = dict(a_w_in=m_a_w_in, b_w_in=m_b_w_in, sgu_ln_g=m_sgu_ln_g, sgu_ln_b=m_sgu_ln_b, sgu_w_s=m_sgu_w_s,
              sgu_b_s=m_sgu_b_s, w_mem_kv=m_w_mem_kv, w_out=m_w_out, ln_mix_g=m_ln_mix_g, ln_mix_b=m_ln_mix_b,
              w_gate=m_w_gate, w_up=m_w_up, w_down=m_w_down, ln_ffn_g=m_ln_ffn_g, ln_ffn_b=m_ln_ffn_b)
  mom2 = dict(a_w_in=v_a_w_in, b_w_in=v_b_w_in, sgu_ln_g=v_sgu_ln_g, sgu_ln_b=v_sgu_ln_b, sgu_w_s=v_sgu_w_s,
              sgu_b_s=v_sgu_b_s, w_mem_kv=v_w_mem_kv, w_out=v_w_out, ln_mix_g=v_ln_mix_g, ln_mix_b=v_ln_mix_b,
              w_gate=v_w_gate, w_up=v_w_up, w_down=v_w_down, ln_ffn_g=v_ln_ffn_g, ln_ffn_b=v_ln_ffn_b)

  full = _gather_weights(weights)
  for n in REPLICATED:
    full[n] = weights[n]
  loss_part, grad_x, grads = _local_step(x, mem, loss_target, full)
  loss = lax.psum(loss_part, MESH_AXES)

  shard_shapes = {n: weights[n].shape for n, _ in SHARDED}
  red = _reduce_grads(grads, shard_shapes)
  chip = 2 * lax.axis_index("x") + lax.axis_index("y")
  for n, axis in SMALL_SHARDED:
    width = weights[n].shape[axis]
    red[n] = lax.dynamic_slice_in_dim(red[n], chip * width, width, axis)

  small_names = SMALL_ORDER
  def pack(d):
    flat = jnp.concatenate([d[n].reshape(-1) for n in small_names])
    return _pad_to(flat, _round_up(flat.shape[0], 8 * FLAT_COLS)).reshape(-1, FLAT_COLS)
  small_out = _adamw(pack(weights), pack(red), pack(mom1), pack(mom2), "adamw_small")
  delta, new_m, new_v = {}, {}, {}
  off = 0
  for n in small_names:
    sz = weights[n].size
    for dst, src in zip((delta, new_m, new_v), small_out):
      dst[n] = src.reshape(-1)[off:off + sz].reshape(weights[n].shape)
    off += sz
  for n, _ in SHARDED:
    delta[n], new_m[n], new_v[n] = _adamw(weights[n], red[n], mom1[n], mom2[n], f"adamw_{n}")

  return (loss, grad_x, *[red[n] for n in WEIGHT_NAMES], *[delta[n] for n in WEIGHT_NAMES],
          *[new_m[n] for n in WEIGHT_NAMES], *[new_v[n] for n in WEIGHT_NAMES])
```

```python
import functools
import math

import jax
import jax.numpy as jnp
from jax import lax
from jax.experimental import pallas as pl
from jax.experimental.pallas import tpu as pltpu

F32 = jnp.float32
BF16 = jnp.bfloat16

DEPTH = 4
HEAD_DIM = 64
N_DIL_HEADS = 12
DIL_WIDTH = N_DIL_HEADS * HEAD_DIM
DIL_PATTERNS = ((128, 1), (512, 4), (2048, 16))
BLOCK = 128
N_SGU_GROUPS = 12
SGU_WIDTH = N_SGU_GROUPS * 64
CHUNK = 128
N_MEM_HEADS = 4
MEM_WIDTH = N_MEM_HEADS * HEAD_DIM
DN_ALPHA = (2 * DEPTH) ** 0.25
LN_EPS = 1e-5
ATT_SCALE = HEAD_DIM ** -0.5
ADAM_LR = 0.001
ADAM_B1 = 0.9
ADAM_B2 = 0.999
ADAM_EPS = 1e-08
ADAM_WD = 0.01
ADAM_STEP = 10
NEG_BIG = -1e30
ATTN_UNROLL = 2

LANES = 128
FLAT_COLS = 1024
VMEM_LIMIT = 56 * 1024 * 1024
MESH_AXES = ("x", "y", "c")
MESH_ID = pl.DeviceIdType.MESH


def _tile(n, pref, align=LANES):
  if n <= pref:
    return n
  t = (pref // align) * align
  while t >= align:
    if n % t == 0:
      return t
    t -= align
  return n


def _params(sem):
  return pltpu.CompilerParams(dimension_semantics=sem, vmem_limit_bytes=VMEM_LIMIT)


def _dot(a, b):
  return jnp.dot(a, b, preferred_element_type=F32)


def _dot_nt(a, b):
  return lax.dot_general(a, b, (((1,), (1,)), ((), ())), preferred_element_type=F32)


def _dot_tn(a, b):
  return lax.dot_general(a, b, (((0,), (0,)), ((), ())), preferred_element_type=F32)


def _bf(v):
  return v.astype(BF16)


def _ln_stats(z):
  mu = jnp.mean(z, axis=-1, keepdims=True)
  zc = z - mu
  var = jnp.mean(zc * zc, axis=-1, keepdims=True)
  rstd = lax.rsqrt(var + LN_EPS)
  return zc * rstd, rstd


def _ln_bwd(dy, xhat, rstd, g):
  gdy = dy * g
  m1 = jnp.mean(gdy, axis=-1, keepdims=True)
  m2 = jnp.mean(gdy * xhat, axis=-1, keepdims=True)
  return rstd * (gdy - m1 - xhat * m2)


_GELU_C = math.sqrt(2.0 / math.pi)


def _gelu_parts(v):
  v2 = v * v
  t = jnp.tanh(_GELU_C * (v + 0.044715 * v * v2))
  val = 0.5 * v * (1.0 + t)
  der = 0.5 * (1.0 + t) + 0.5 * v * (1.0 - t * t) * (_GELU_C * (1.0 + 3.0 * 0.044715 * v2))
  return val, der


def _gelu(v):
  t = jnp.tanh(_GELU_C * (v + 0.044715 * v * v * v))
  return 0.5 * v * (1.0 + t)


def _sigmoid(v):
  return 1.0 / (1.0 + jnp.exp(-v))


def _mm(a, b, mode, out_dtype, name, add=None, add_scale=1.0, tm=512, tn=512, tk=512):
  if mode == "nn":
    (m, k), (k2, n) = a.shape, b.shape
  elif mode == "nt":
    (m, k), (n, k2) = a.shape, b.shape
  else:
    (k, m), (k2, n) = a.shape, b.shape
  assert k == k2, (a.shape, b.shape, mode)
  tm, tn, tk = _tile(m, tm), _tile(n, tn), _tile(k, tk)
  nk = k // tk
  if mode == "nn":
    a_spec = pl.BlockSpec((tm, tk), lambda i, j, kk: (i, kk))
    b_spec = pl.BlockSpec((tk, tn), lambda i, j, kk: (kk, j))
    dot = _dot
  elif mode == "nt":
    a_spec = pl.BlockSpec((tm, tk), lambda i, j, kk: (i, kk))
    b_spec = pl.BlockSpec((tn, tk), lambda i, j, kk: (j, kk))
    dot = _dot_nt
  else:
    a_spec = pl.BlockSpec((tk, tm), lambda i, j, kk: (kk, i))
    b_spec = pl.BlockSpec((tk, tn), lambda i, j, kk: (kk, j))
    dot = _dot_tn
  o_spec = pl.BlockSpec((tm, tn), lambda i, j, kk: (i, j))
  has_add = add is not None

  def body(*refs):
    if has_add:
      a_ref, b_ref, add_ref, o_ref, acc_ref = refs
    else:
      a_ref, b_ref, o_ref, acc_ref = refs
    kk = pl.program_id(2)

    @pl.when(kk == 0)
    def _():
      acc_ref[...] = jnp.zeros_like(acc_ref)

    acc_ref[...] += dot(_bf(a_ref[...]), _bf(b_ref[...]))

    @pl.when(kk == nk - 1)
    def _():
      r = acc_ref[...]
      if has_add:
        r = r + add_scale * add_ref[...].astype(F32)
      o_ref[...] = r.astype(out_dtype)

  in_specs = [a_spec, b_spec] + ([o_spec] if has_add else [])
  args = (a, b) + ((add,) if has_add else ())
  return pl.pallas_call(
      body, name=name, grid=(m // tm, n // tn, nk), in_specs=in_specs, out_specs=o_spec,
      out_shape=jax.ShapeDtypeStruct((m, n), out_dtype),
      scratch_shapes=[pltpu.VMEM((tm, tn), F32)],
      compiler_params=_params(("parallel", "parallel", "arbitrary")),
  )(*args)


def _mm_res_ln(a, w, res, g, b, name, tm=512, tk=512):
  m, k = a.shape
  d = w.shape[1]
  tm, tk = _tile(m, tm), _tile(k, tk)
  nk = k // tk

  def body(a_ref, w_ref, r_ref, g_ref, b_ref, z_ref, x_ref, xb_ref, acc_ref):
    kk = pl.program_id(1)

    @pl.when(kk == 0)
    def _():
      acc_ref[...] = jnp.zeros_like(acc_ref)

    acc_ref[...] += _dot(_bf(a_ref[...]), _bf(w_ref[...]))

    @pl.when(kk == nk - 1)
    def _():
      z = DN_ALPHA * r_ref[...] + acc_ref[...]
      xhat, _ = _ln_stats(z)
      xn = xhat * g_ref[...] + b_ref[...]
      z_ref[...] = z
      x_ref[...] = xn
      xb_ref[...] = _bf(xn)

  row = pl.BlockSpec((tm, d), lambda i, kk: (i, 0))
  vec = pl.BlockSpec((1, d), lambda i, kk: (0, 0))
  return pl.pallas_call(
      body, name=name, grid=(m // tm, nk),
      in_specs=[pl.BlockSpec((tm, tk), lambda i, kk: (i, kk)), pl.BlockSpec((tk, d), lambda i, kk: (kk, 0)), row, vec, vec],
      out_specs=[row, row, row],
      out_shape=[jax.ShapeDtypeStruct((m, d), F32), jax.ShapeDtypeStruct((m, d), F32), jax.ShapeDtypeStruct((m, d), BF16)],
      scratch_shapes=[pltpu.VMEM((tm, d), F32)],
      compiler_params=_params(("parallel", "arbitrary")),
  )(a, w, res, g.reshape(1, d), b.reshape(1, d))


def _ln_bwd_call(dy, z, g, name, tm=512):
  m, d = z.shape
  tm = _tile(m, tm)
  n = m // tm

  def body(dy_ref, z_ref, g_ref, dz_ref, dzb_ref, dg_ref, db_ref):
    i = pl.program_id(0)

    @pl.when(i == 0)
    def _():
      dg_ref[...] = jnp.zeros_like(dg_ref)
      db_ref[...] = jnp.zeros_like(db_ref)

    dy_v = dy_ref[...]
    xhat, rstd = _ln_stats(z_ref[...])
    dz = _ln_bwd(dy_v, xhat, rstd, g_ref[...])
    dz_ref[...] = dz
    dzb_ref[...] = _bf(dz)
    dg_ref[...] += jnp.sum(dy_v * xhat, axis=0, keepdims=True)
    db_ref[...] += jnp.sum(dy_v, axis=0, keepdims=True)

  row = pl.BlockSpec((tm, d), lambda i: (i, 0))
  vec = pl.BlockSpec((1, d), lambda i: (0, 0))
  dz, dzb, dg, db = pl.pallas_call(
      body, name=name, grid=(n,), in_specs=[row, row, vec], out_specs=[row, row, vec, vec],
      out_shape=[jax.ShapeDtypeStruct((m, d), F32), jax.ShapeDtypeStruct((m, d), BF16),
                 jax.ShapeDtypeStruct((1, d), F32), jax.ShapeDtypeStruct((1, d), F32)],
      compiler_params=_params(("arbitrary",)),
  )(dy, z, g.reshape(1, d))
  return dz, dzb, dg[0], db[0]


def _ffn_up(xb, wg, wu, name, tm=1024, tn=256):
  m, d = xb.shape
  f = wg.shape[1]
  tm, tn = _tile(m, tm), _tile(f, tn)

  def body(x_ref, wg_ref, wu_ref, a_ref, b_ref, h_ref):
    xv = x_ref[...]
    a = _dot(xv, wg_ref[...])
    b = _dot(xv, wu_ref[...])
    a_ref[...] = _bf(a)
    b_ref[...] = _bf(b)
    h_ref[...] = _bf(a * _sigmoid(a) * b)

  wspec = pl.BlockSpec((d, tn), lambda i, j: (0, j))
  ospec = pl.BlockSpec((tm, tn), lambda i, j: (i, j))
  sds = jax.ShapeDtypeStruct((m, f), BF16)
  return pl.pallas_call(
      body, name=name, grid=(m // tm, f // tn),
      in_specs=[pl.BlockSpec((tm, d), lambda i, j: (i, 0)), wspec, wspec],
      out_specs=[ospec, ospec, ospec], out_shape=[sds, sds, sds],
      compiler_params=_params(("parallel", "parallel")),
  )(xb, wg, wu)


def _ffn_bwd_hidden(dzb, wd, a, b, name, tm=1024, tn=256):
  m, d = dzb.shape
  f = wd.shape[0]
  tm, tn = _tile(m, tm), _tile(f, tn)

  def body(dz_ref, wd_ref, a_ref, b_ref, da_ref, db_ref):
    dh = _dot_nt(dz_ref[...], wd_ref[...])
    av = a_ref[...].astype(F32)
    bv = b_ref[...].astype(F32)
    sg = _sigmoid(av)
    da_ref[...] = _bf(dh * bv * (sg * (1.0 + av * (1.0 - sg))))
    db_ref[...] = _bf(dh * (av * sg))

  hspec = pl.BlockSpec((tm, tn), lambda i, j: (i, j))
  sds = jax.ShapeDtypeStruct((m, f), BF16)
  return pl.pallas_call(
      body, name=name, grid=(m // tm, f // tn),
      in_specs=[pl.BlockSpec((tm, d), lambda i, j: (i, 0)), pl.BlockSpec((tn, d), lambda i, j: (j, 0)), hspec, hspec],
      out_specs=[hspec, hspec], out_shape=[sds, sds],
      compiler_params=_params(("parallel", "parallel")),
  )(dzb, wd, a, b)


def _ffn_bwd_input(da, db, wg, wu, dz, name, tm=512, tk=256):
  m, f = da.shape
  d = wg.shape[0]
  tm, tk = _tile(m, tm), _tile(f, tk)
  nk = f // tk

  def body(da_ref, db_ref, wg_ref, wu_ref, dz_ref, o_ref, acc_ref):
    kk = pl.program_id(1)

    @pl.when(kk == 0)
    def _():
      acc_ref[...] = jnp.zeros_like(acc_ref)

    acc_ref[...] += _dot_nt(da_ref[...], wg_ref[...]) + _dot_nt(db_ref[...], wu_ref[...])

    @pl.when(kk == nk - 1)
    def _():
      o_ref[...] = DN_ALPHA * dz_ref[...] + acc_ref[...]

  hspec = pl.BlockSpec((tm, tk), lambda i, kk: (i, kk))
  wspec = pl.BlockSpec((d, tk), lambda i, kk: (0, kk))
  row = pl.BlockSpec((tm, d), lambda i, kk: (i, 0))
  return pl.pallas_call(
      body, name=name, grid=(m // tm, nk), in_specs=[hspec, hspec, wspec, wspec, row], out_specs=row,
      out_shape=jax.ShapeDtypeStruct((m, d), F32), scratch_shapes=[pltpu.VMEM((tm, d), F32)],
      compiler_params=_params(("parallel", "arbitrary")),
  )(da, db, wg, wu, dz)


def _alibi_slopes():
  n = N_DIL_HEADS
  return jnp.exp2(-8.0 * (jnp.arange(n, dtype=F32) + 1.0) / n).reshape(1, n)


def _band_consts():
  qi = lax.broadcasted_iota(jnp.int32, (BLOCK, BLOCK), 0)
  ki = lax.broadcasted_iota(jnp.int32, (BLOCK, BLOCK), 1)
  steps_cur = (qi - ki).astype(F32)
  steps_prev = (qi + BLOCK - ki).astype(F32)
  return ki < 64, steps_cur, steps_prev, ki <= qi, ki >= qi


def _rows(start, d):
  if d == 1:
    return pl.ds(pl.multiple_of(start, BLOCK), BLOCK)
  return pl.ds(start, BLOCK, stride=d)


def _fill_bias_tables(bias_sc, slope0, slope1):
  row = lax.broadcasted_iota(jnp.int32, (2 * BLOCK, 2 * BLOCK), 0)
  col = lax.broadcasted_iota(jnp.int32, (2 * BLOCK, 2 * BLOCK), 1)
  qi = jnp.bitwise_and(row, BLOCK - 1)
  ki = jnp.bitwise_and(col, BLOCK - 1)
  is_cur = col >= BLOCK
  steps = jnp.where(is_cur, qi - ki, qi + BLOCK - ki)
  valid = jnp.logical_and(steps >= 0, steps <= BLOCK)
  slope = jnp.where(row >= BLOCK, slope1, slope0)
  dist = slope * steps.astype(F32)
  for p, (_, d) in enumerate(DIL_PATTERNS):
    base = jnp.where(valid, -d * dist, NEG_BIG)
    bias_sc[2 * p] = base
    bias_sc[2 * p + 1] = jnp.where(is_cur, base, NEG_BIG)


def _stack_heads(v2, head0):
  return jnp.concatenate([jnp.where(head0, v2, 0.0), jnp.where(head0, 0.0, v2)], axis=0)


def _unstack_heads(v, head0):
  return jnp.where(head0, v[:BLOCK], v[BLOCK:])


def _block_rows(idx, d, nblk):
  r = idx // nblk
  n = idx % nblk
  cur = _rows(r + n * (BLOCK * d), d)
  prev = _rows(r + jnp.maximum(n - 1, 0) * (BLOCK * d), d)
  return cur, prev, n


def pair_tile(dt):
  return pltpu.VMEM((2 * BLOCK, 2 * BLOCK), dt)


def _two_stage_loop(nb, first_stage, second_stage, buf_a, buf_b):
  assert nb % 2 == 0

  def pair(t, carry):
    i = 2 * t + 1
    first_stage(i, buf_b)
    second_stage(i - 1, buf_a)
    first_stage(i + 1, buf_a)
    second_stage(i, buf_b)
    return carry

  first_stage(0, buf_a)
  lax.fori_loop(0, nb // 2 - 1, pair, 0)
  first_stage(nb - 1, buf_b)
  second_stage(nb - 2, buf_a)
  second_stage(nb - 1, buf_b)


def _attn_fwd(h3, name):
  bl, s, _ = h3.shape
  npair = N_DIL_HEADS // 2
  nb = s // BLOCK

  def body(sl_ref, q_ref, k_ref, v_ref, o_ref, lse_ref, o_sc, l_sc, bias_sc, s_a, s_b):
    hp = pl.program_id(1)
    head0 = lax.broadcasted_iota(jnp.int32, (BLOCK, LANES), 1) < 64
    _fill_bias_tables(bias_sc, sl_ref[0, 2 * hp], sl_ref[0, 2 * hp + 1])

    for p, (_, d) in enumerate(DIL_PATTERNS):
      nblk = (s // d) // BLOCK
      two = nblk > 1
      ks = slice(0, 2 * BLOCK) if two else slice(BLOCK, 2 * BLOCK)

      def scores(idx, buf, p=p, d=d, nblk=nblk, two=two, ks=ks):
        cur, prev, n = _block_rows(idx, d, nblk)
        qs = _bf(_stack_heads(q_ref[cur, :], head0) * ATT_SCALE)
        kb = _bf(jnp.concatenate([k_ref[prev, :], k_ref[cur, :]], axis=0)) if two else _bf(k_ref[cur, :])
        first = jnp.where(n == 0, 1, 0) if two else 0
        buf[:, ks] = _dot_nt(qs, kb) + bias_sc[2 * p + first, :, ks]

      def values(idx, buf, p=p, d=d, nblk=nblk, two=two, ks=ks):
        cur, prev, _ = _block_rows(idx, d, nblk)
        sc = buf[:, ks]
        mx = jnp.max(sc, axis=1, keepdims=True)
        pe = jnp.exp(sc - mx)
        den = jnp.sum(pe, axis=1, keepdims=True)
        vb = _bf(jnp.concatenate([v_ref[prev, :], v_ref[cur, :]], axis=0)) if two else _bf(v_ref[cur, :])
        acc = _dot(_bf(pe), vb) / den
        o_sc[p, cur, :] = _unstack_heads(acc, head0)
        l_sc[p, cur, :] = _unstack_heads(jnp.broadcast_to(mx + jnp.log(den), (2 * BLOCK, LANES)), head0)

      _two_stage_loop(nb, scores, values, s_a, s_b)

    def merge(i, carry):
      rows = pl.ds(pl.multiple_of(i * BLOCK, BLOCK), BLOCK)
      l0, l1, l2 = l_sc[0, rows, :], l_sc[1, rows, :], l_sc[2, rows, :]
      mx = jnp.maximum(jnp.maximum(l0, l1), l2)
      e0, e1, e2 = jnp.exp(l0 - mx), jnp.exp(l1 - mx), jnp.exp(l2 - mx)
      tot = e0 + e1 + e2
      o_ref[rows, :] = (e0 * o_sc[0, rows, :] + e1 * o_sc[1, rows, :] + e2 * o_sc[2, rows, :]) / tot
      lse_ref[rows, :] = mx + jnp.log(tot)
      return carry

    lax.fori_loop(0, nb, merge, 0)

  def col(off):
    return pl.BlockSpec((None, s, LANES), lambda b, p: (b, 0, off + p))

  sds = jax.ShapeDtypeStruct((bl, s, DIL_WIDTH), F32)
  return pl.pallas_call(
      body, name=name, grid=(bl, npair),
      in_specs=[pl.BlockSpec(memory_space=pltpu.SMEM), col(0), col(npair), col(2 * npair)],
      out_specs=[col(0), col(0)], out_shape=[sds, sds],
      scratch_shapes=[pltpu.VMEM((3, s, LANES), F32), pltpu.VMEM((3, s, LANES), F32),
                      pltpu.VMEM((6, 2 * BLOCK, 2 * BLOCK), F32), pair_tile(F32), pair_tile(F32)],
      compiler_params=_params(("parallel", "parallel")),
  )(_alibi_slopes(), h3, h3, h3)


def _attn_bwd(h3, out3, lse3, dcat3, name):
  bl, s, _ = h3.shape
  npair = N_DIL_HEADS // 2
  nb = s // BLOCK

  def body(sl_ref, q_ref, k_ref, v_ref, o_ref, l_ref, do_ref, dq_ref, dk_ref, dv_ref, bias_sc, p_a, ds_a, p_b, ds_b):
    hp = pl.program_id(1)
    lane = lax.broadcasted_iota(jnp.int32, (BLOCK, LANES), 1)
    head0 = lane < 64
    _fill_bias_tables(bias_sc, sl_ref[0, 2 * hp], sl_ref[0, 2 * hp + 1])
    dq_ref[...] = jnp.zeros_like(dq_ref)
    dk_ref[...] = jnp.zeros_like(dk_ref)
    dv_ref[...] = jnp.zeros_like(dv_ref)

    def per_row(v2, pick0, pick1):
      return jnp.concatenate([jnp.sum(jnp.where(pick0, v2, 0.0), axis=1, keepdims=True),
                              jnp.sum(jnp.where(pick1, v2, 0.0), axis=1, keepdims=True)], axis=0)

    for p, (_, d) in enumerate(DIL_PATTERNS):
      nblk = (s // d) // BLOCK
      two = nblk > 1
      ks = slice(0, 2 * BLOCK) if two else slice(BLOCK, 2 * BLOCK)

      def operands(idx, d=d, nblk=nblk, two=two):
        cur, prev, n = _block_rows(idx, d, nblk)
        qs = _bf(_stack_heads(q_ref[cur, :], head0) * ATT_SCALE)
        dos = _bf(_stack_heads(do_ref[cur, :], head0))
        kb = _bf(jnp.concatenate([k_ref[prev, :], k_ref[cur, :]], axis=0)) if two else _bf(k_ref[cur, :])
        return cur, prev, n, qs, dos, kb

      def probs(idx, bufs, p=p, two=two, ks=ks, operands=operands):
        cur, prev, n, qs, dos, kb = operands(idx)
        vb = _bf(jnp.concatenate([v_ref[prev, :], v_ref[cur, :]], axis=0)) if two else _bf(v_ref[cur, :])
        do2 = do_ref[cur, :]
        lse = per_row(l_ref[cur, :], lane == 0, lane == 64)
        delta = per_row(do2 * o_ref[cur, :], head0, jnp.logical_not(head0))
        first = jnp.where(n == 0, 1, 0) if two else 0
        pr = jnp.exp(_dot_nt(qs, kb) + bias_sc[2 * p + first, :, ks] - lse)
        bufs[0][:, ks] = _bf(pr)
        bufs[1][:, ks] = _bf(pr * (_dot_nt(dos, vb) - delta))

      def products(idx, bufs, two=two, ks=ks, operands=operands):
        cur, prev, _, qs, dos, kb = operands(idx)
        pr = bufs[0][:, ks]
        ds = bufs[1][:, ks]
        dq_ref[cur, :] += _unstack_heads(_dot(ds, kb), head0) * ATT_SCALE
        dkb = _dot_tn(ds, qs)
        dvb = _dot_tn(pr, dos)
        if two:
          dk_ref[prev, :] += dkb[:BLOCK]
          dv_ref[prev, :] += dvb[:BLOCK]
          dk_ref[cur, :] += dkb[BLOCK:]
          dv_ref[cur, :] += dvb[BLOCK:]
        else:
          dk_ref[cur, :] += dkb
          dv_ref[cur, :] += dvb

      _two_stage_loop(nb, probs, products, (p_a, ds_a), (p_b, ds_b))

  def col(off):
    return pl.BlockSpec((None, s, LANES), lambda b, p: (b, 0, off + p))

  sds = jax.ShapeDtypeStruct((bl, s, DIL_WIDTH), F32)
  return pl.pallas_call(
      body, name=name, grid=(bl, npair),
      in_specs=[pl.BlockSpec(memory_space=pltpu.SMEM), col(0), col(npair), col(2 * npair), col(0), col(0), col(0)],
      out_specs=[col(0), col(0), col(0)], out_shape=[sds, sds, sds],
      scratch_shapes=[pltpu.VMEM((6, 2 * BLOCK, 2 * BLOCK), F32)] + [pair_tile(BF16)] * 4,
      compiler_params=_params(("parallel", "parallel")),
  )(_alibi_slopes(), h3, h3, h3, out3, lse3, dcat3)


def _attn_fwd_old(h3, name):
  bl, s, _ = h3.shape
  npair = N_DIL_HEADS // 2

  def body(sl_ref, q_ref, k_ref, v_ref, o_ref, lse_ref, o_sc, l_sc):
    hp = pl.program_id(1)
    head0, steps_cur, steps_prev, mask_cur, mask_prev = _band_consts()
    slope = [sl_ref[0, 2 * hp], sl_ref[0, 2 * hp + 1]]

    for p, (_, d) in enumerate(DIL_PATTERNS):
      nblk = (s // d) // BLOCK
      has_prev_block = nblk > 1

      def blk(idx, carry, p=p, d=d, nblk=nblk, has_prev_block=has_prev_block):
        r = idx // nblk
        n = idx % nblk
        cur = _rows(r + n * (BLOCK * d), d)
        q2 = q_ref[cur, :]
        kc = _bf(k_ref[cur, :])
        vc = _bf(v_ref[cur, :])
        if has_prev_block:
          prev = _rows(r + jnp.maximum(n - 1, 0) * (BLOCK * d), d)
          kp = _bf(k_ref[prev, :])
          vp = _bf(v_ref[prev, :])
          first_block = jnp.where(n > 0, 0.0, NEG_BIG)
        outs, lses = [], []
        for j in range(2):
          hm = head0 if j == 0 else jnp.logical_not(head0)
          qj = _bf(jnp.where(hm, q2, 0.0) * ATT_SCALE)
          sc = _dot_nt(qj, kc) - (slope[j] * d) * steps_cur
          sc = jnp.where(mask_cur, sc, NEG_BIG)
          mx = jnp.max(sc, axis=1, keepdims=True)
          if has_prev_block:
            sp = _dot_nt(qj, kp) - (slope[j] * d) * steps_prev + first_block
            sp = jnp.where(mask_prev, sp, NEG_BIG)
            mx = jnp.maximum(mx, jnp.max(sp, axis=1, keepdims=True))
          pc = jnp.exp(sc - mx)
          den = jnp.sum(pc, axis=1, keepdims=True)
          acc = _dot(_bf(pc), vc)
          if has_prev_block:
            pp = jnp.exp(sp - mx)
            den = den + jnp.sum(pp, axis=1, keepdims=True)
            acc = acc + _dot(_bf(pp), vp)
          outs.append(acc / den)
          lses.append(mx + jnp.log(den))
        o_sc[p, cur, :] = jnp.where(head0, outs[0], outs[1])
        l_sc[p, cur, :] = jnp.where(head0, lses[0], lses[1])
        return carry

      lax.fori_loop(0, s // BLOCK, blk, 0, unroll=ATTN_UNROLL)

    def merge(i, carry):
      rows = pl.ds(pl.multiple_of(i * BLOCK, BLOCK), BLOCK)
      l0, l1, l2 = l_sc[0, rows, :], l_sc[1, rows, :], l_sc[2, rows, :]
      mx = jnp.maximum(jnp.maximum(l0, l1), l2)
      e0, e1, e2 = jnp.exp(l0 - mx), jnp.exp(l1 - mx), jnp.exp(l2 - mx)
      tot = e0 + e1 + e2
      o_ref[rows, :] = (e0 * o_sc[0, rows, :] + e1 * o_sc[1, rows, :] + e2 * o_sc[2, rows, :]) / tot
      lse_ref[rows, :] = mx + jnp.log(tot)
      return carry

    lax.fori_loop(0, s // BLOCK, merge, 0)

  def col(off):
    return pl.BlockSpec((None, s, LANES), lambda b, p: (b, 0, off + p))

  sds = jax.ShapeDtypeStruct((bl, s, DIL_WIDTH), F32)
  return pl.pallas_call(
      body, name=name, grid=(bl, npair),
      in_specs=[pl.BlockSpec(memory_space=pltpu.SMEM), col(0), col(npair), col(2 * npair)],
      out_specs=[col(0), col(0)], out_shape=[sds, sds],
      scratch_shapes=[pltpu.VMEM((3, s, LANES), F32), pltpu.VMEM((3, s, LANES), F32)],
      compiler_params=_params(("parallel", "parallel")),
  )(_alibi_slopes(), h3, h3, h3)


def _attn_bwd_old(h3, out3, lse3, dcat3, name):
  bl, s, _ = h3.shape
  npair = N_DIL_HEADS // 2

  def body(sl_ref, q_ref, k_ref, v_ref, o_ref, l_ref, do_ref, dq_ref, dk_ref, dv_ref):
    hp = pl.program_id(1)
    head0, steps_cur, steps_prev, mask_cur, mask_prev = _band_consts()
    lane = lax.broadcasted_iota(jnp.int32, (BLOCK, LANES), 1)
    slope = [sl_ref[0, 2 * hp], sl_ref[0, 2 * hp + 1]]
    dq_ref[...] = jnp.zeros_like(dq_ref)
    dk_ref[...] = jnp.zeros_like(dk_ref)
    dv_ref[...] = jnp.zeros_like(dv_ref)

    for p, (_, d) in enumerate(DIL_PATTERNS):
      nblk = (s // d) // BLOCK
      has_prev_block = nblk > 1

      def blk(idx, carry, d=d, nblk=nblk, has_prev_block=has_prev_block):
        r = idx // nblk
        n = idx % nblk
        cur = _rows(r + n * (BLOCK * d), d)
        q2 = q_ref[cur, :]
        do2 = do_ref[cur, :]
        l2 = l_ref[cur, :]
        prod = do2 * o_ref[cur, :]
        kc = _bf(k_ref[cur, :])
        vc = _bf(v_ref[cur, :])
        if has_prev_block:
          prev = _rows(r + jnp.maximum(n - 1, 0) * (BLOCK * d), d)
          kp = _bf(k_ref[prev, :])
          vp = _bf(v_ref[prev, :])
          first_block = jnp.where(n > 0, 0.0, NEG_BIG)
          dkp = jnp.zeros((BLOCK, LANES), F32)
          dvp = jnp.zeros((BLOCK, LANES), F32)
        dq2 = jnp.zeros((BLOCK, LANES), F32)
        dkc = jnp.zeros((BLOCK, LANES), F32)
        dvc = jnp.zeros((BLOCK, LANES), F32)
        for j in range(2):
          hm = head0 if j == 0 else jnp.logical_not(head0)
          qj = _bf(jnp.where(hm, q2, 0.0) * ATT_SCALE)
          doj = _bf(jnp.where(hm, do2, 0.0))
          lj = jnp.sum(jnp.where(lane == 64 * j, l2, 0.0), axis=1, keepdims=True)
          dj = jnp.sum(jnp.where(hm, prod, 0.0), axis=1, keepdims=True)
          sc = _dot_nt(qj, kc) - (slope[j] * d) * steps_cur
          pc = jnp.exp(jnp.where(mask_cur, sc - lj, NEG_BIG))
          dsc = _bf(pc * (_dot_nt(doj, vc) - dj))
          dq_j = _dot(dsc, kc)
          dkc = dkc + _dot_tn(dsc, qj)
          dvc = dvc + _dot_tn(_bf(pc), doj)
          if has_prev_block:
            sp = _dot_nt(qj, kp) - (slope[j] * d) * steps_prev + first_block
            pp = jnp.exp(jnp.where(mask_prev, sp - lj, NEG_BIG))
            dsp = _bf(pp * (_dot_nt(doj, vp) - dj))
            dq_j = dq_j + _dot(dsp, kp)
            dkp = dkp + _dot_tn(dsp, qj)
            dvp = dvp + _dot_tn(_bf(pp), doj)
          dq2 = dq2 + jnp.where(hm, dq_j, 0.0) * ATT_SCALE
        dq_ref[cur, :] += dq2
        dk_ref[cur, :] += dkc
        dv_ref[cur, :] += dvc
        if has_prev_block:
          dk_ref[prev, :] += dkp
          dv_ref[prev, :] += dvp
        return carry

      lax.fori_loop(0, s // BLOCK, blk, 0, unroll=ATTN_UNROLL)

  def col(off):
    return pl.BlockSpec((None, s, LANES), lambda b, p: (b, 0, off + p))

  sds = jax.ShapeDtypeStruct((bl, s, DIL_WIDTH), F32)
  return pl.pallas_call(
      body, name=name, grid=(bl, npair),
      in_specs=[pl.BlockSpec(memory_space=pltpu.SMEM), col(0), col(npair), col(2 * npair), col(0), col(0), col(0)],
      out_specs=[col(0), col(0), col(0)], out_shape=[sds, sds, sds],
      compiler_params=_params(("parallel", "parallel")),
  )(_alibi_slopes(), h3, h3, h3, out3, lse3, dcat3)


def _mem_heads(tq):
  lane = lax.broadcasted_iota(jnp.int32, (tq, LANES), 1)
  return lane < 64


def _mem_fwd(h3, qcol, mkv3, name, tq=512):
  bl, s, _ = h3.shape
  nm = mkv3.shape[1]
  tq = _tile(s, tq)

  def body(q_ref, kv_ref, o_ref):
    head0 = _mem_heads(tq)
    for lg in range(MEM_WIDTH // LANES):
      cs = slice(lg * LANES, (lg + 1) * LANES)
      q2 = q_ref[:, cs]
      mk = _bf(kv_ref[:, cs])
      mv = _bf(kv_ref[:, MEM_WIDTH + lg * LANES:MEM_WIDTH + (lg + 1) * LANES])
      outs = []
      for j in range(2):
        hm = head0 if j == 0 else jnp.logical_not(head0)
        qj = _bf(jnp.where(hm, q2, 0.0) * ATT_SCALE)
        sc = _dot_nt(qj, mk)
        mx = jnp.max(sc, axis=1, keepdims=True)
        pe = jnp.exp(sc - mx)
        den = jnp.sum(pe, axis=1, keepdims=True)
        outs.append(_dot(_bf(pe / den), mv))
      o_ref[:, cs] = jnp.where(head0, outs[0], outs[1])

  return pl.pallas_call(
      body, name=name, grid=(bl, s // tq),
      in_specs=[pl.BlockSpec((None, tq, MEM_WIDTH), lambda b, i: (b, i, qcol)),
                pl.BlockSpec((None, nm, 2 * MEM_WIDTH), lambda b, i: (b, 0, 0))],
      out_specs=pl.BlockSpec((None, tq, MEM_WIDTH), lambda b, i: (b, i, 0)),
      out_shape=jax.ShapeDtypeStruct((bl, s, MEM_WIDTH), F32),
      compiler_params=_params(("parallel", "parallel")),
  )(h3, mkv3)


def _mem_bwd(h3, qcol, mkv3, dcat3, name, tq=512):
  bl, s, _ = h3.shape
  nm = mkv3.shape[1]
  tq = _tile(s, tq)
  docol = dcat3.shape[2] // MEM_WIDTH - 1

  def body(q_ref, kv_ref, do_ref, dq_ref, dkv_ref):
    i = pl.program_id(1)

    @pl.when(i == 0)
    def _():
      dkv_ref[...] = jnp.zeros_like(dkv_ref)

    head0 = _mem_heads(tq)
    for lg in range(MEM_WIDTH // LANES):
      cs = slice(lg * LANES, (lg + 1) * LANES)
      vs = slice(MEM_WIDTH + lg * LANES, MEM_WIDTH + (lg + 1) * LANES)
      q2 = q_ref[:, cs]
      do2 = do_ref[:, cs]
      mk = _bf(kv_ref[:, cs])
      mv = _bf(kv_ref[:, vs])
      dq2 = jnp.zeros((tq, LANES), F32)
      dmk = jnp.zeros((nm, LANES), F32)
      dmv = jnp.zeros((nm, LANES), F32)
      for j in range(2):
        hm = head0 if j == 0 else jnp.logical_not(head0)
        qj = _bf(jnp.where(hm, q2, 0.0) * ATT_SCALE)
        doj = _bf(jnp.where(hm, do2, 0.0))
        sc = _dot_nt(qj, mk)
        mx = jnp.max(sc, axis=1, keepdims=True)
        pe = jnp.exp(sc - mx)
        pn = pe / jnp.sum(pe, axis=1, keepdims=True)
        pb = _bf(pn)
        dp = _dot_nt(doj, mv)
        dj = jnp.sum(pb.astype(F32) * dp, axis=1, keepdims=True)
        ds = _bf(pn * (dp - dj))
        dq2 = dq2 + jnp.where(hm, _dot(ds, mk), 0.0) * ATT_SCALE
        dmk = dmk + _dot_tn(ds, qj)
        dmv = dmv + _dot_tn(pb, doj)
      dq_ref[:, cs] = dq2
      dkv_ref[:, cs] += dmk
      dkv_ref[:, vs] += dmv

  return pl.pallas_call(
      body, name=name, grid=(bl, s // tq),
      in_specs=[pl.BlockSpec((None, tq, MEM_WIDTH), lambda b, i: (b, i, qcol)),
                pl.BlockSpec((None, nm, 2 * MEM_WIDTH), lambda b, i: (b, 0, 0)),
                pl.BlockSpec((None, tq, MEM_WIDTH), lambda b, i: (b, i, docol))],
      out_specs=[pl.BlockSpec((None, tq, MEM_WIDTH), lambda b, i: (b, i, 0)),
                 pl.BlockSpec((None, nm, 2 * MEM_WIDTH), lambda b, i: (b, 0, 0))],
      out_shape=[jax.ShapeDtypeStruct((bl, s, MEM_WIDTH), F32), jax.ShapeDtypeStruct((bl, nm, 2 * MEM_WIDTH), F32)],
      compiler_params=_params(("parallel", "arbitrary")),
  )(h3, mkv3, dcat3)


def _sgu_consts():
  ti = lax.broadcasted_iota(jnp.int32, (CHUNK, CHUNK), 0)
  si = lax.broadcasted_iota(jnp.int32, (CHUNK, CHUNK), 1)
  return si <= ti, si < 64


def _sgu_bias_lanes(b_s):
  return jnp.repeat(b_s.T, 64, axis=1)


def _sgu_fwd(h2, ln_g, ln_b, w_s, b_s, name, tr=512):
  t, _ = h2.shape
  tr = _tile(t, tr)
  nch = tr // CHUNK
  npair = N_SGU_GROUPS // 2

  def body(u_ref, v_ref, g_ref, b_ref, w_ref, bs_ref, o_ref, vn_sc):
    tril, head0 = _sgu_consts()
    xhat, _ = _ln_stats(_gelu(v_ref[...]))
    vn_sc[...] = _bf(xhat * g_ref[...] + b_ref[...])
    for jp in range(npair):
      cs = slice(jp * LANES, (jp + 1) * LANES)
      w0 = _bf(jnp.where(tril, w_ref[2 * jp], 0.0))
      w1 = _bf(jnp.where(tril, w_ref[2 * jp + 1], 0.0))
      bias = bs_ref[:, cs]
      for c in range(nch):
        rs = slice(c * CHUNK, (c + 1) * CHUNK)
        vb = vn_sc[rs, cs]
        mixed = jnp.where(head0, _dot(w0, vb), _dot(w1, vb)) + bias
        o_ref[rs, cs] = _gelu(u_ref[rs, cs]) * mixed

  blk = lambda j: pl.BlockSpec((tr, SGU_WIDTH), lambda i: (i, j))
  vec = pl.BlockSpec((1, SGU_WIDTH), lambda i: (0, 0))
  return pl.pallas_call(
      body, name=name, grid=(t // tr,),
      in_specs=[blk(0), blk(1), vec, vec,
                pl.BlockSpec((N_SGU_GROUPS, CHUNK, CHUNK), lambda i: (0, 0, 0)),
                pl.BlockSpec((CHUNK, SGU_WIDTH), lambda i: (0, 0))],
      out_specs=blk(0), out_shape=jax.ShapeDtypeStruct((t, SGU_WIDTH), F32),
      scratch_shapes=[pltpu.VMEM((tr, SGU_WIDTH), BF16)],
      compiler_params=_params(("parallel",)),
  )(h2, h2, ln_g.reshape(1, -1), ln_b.reshape(1, -1), w_s, _sgu_bias_lanes(b_s))


def _sgu_bwd(h2, dcat, ln_g, ln_b, w_s, b_s, name, tr=512):
  t, _ = h2.shape
  tr = _tile(t, tr)
  nch = tr // CHUNK
  npair = N_SGU_GROUPS // 2
  nsteps = t // tr

  def body(u_ref, v_ref, dm_ref, g_ref, b_ref, w_ref, bs_ref,
           du_ref, dv_ref, dw_ref, dbs_ref, dg_ref, db_ref, vn_sc, dmx_sc, dvn_sc, mix_sc, dbx_sc):
    i = pl.program_id(0)
    tril, head0 = _sgu_consts()

    @pl.when(i == 0)
    def _():
      dw_ref[...] = jnp.zeros_like(dw_ref)
      dg_ref[...] = jnp.zeros_like(dg_ref)
      db_ref[...] = jnp.zeros_like(db_ref)
      dbx_sc[...] = jnp.zeros_like(dbx_sc)

    gv, gv_der = _gelu_parts(v_ref[...])
    xhat, rstd = _ln_stats(gv)
    g = g_ref[...]
    vn_sc[...] = _bf(xhat * g + b_ref[...])
    gu, gu_der = _gelu_parts(u_ref[...])
    dmix = dm_ref[...]
    dmx_sc[...] = dmix * gu

    for jp in range(npair):
      cs = slice(jp * LANES, (jp + 1) * LANES)
      w0 = _bf(jnp.where(tril, w_ref[2 * jp], 0.0))
      w1 = _bf(jnp.where(tril, w_ref[2 * jp + 1], 0.0))
      bias = bs_ref[:, cs]
      dw0 = jnp.zeros((CHUNK, CHUNK), F32)
      dw1 = jnp.zeros((CHUNK, CHUNK), F32)
      dbx = jnp.zeros((CHUNK, LANES), F32)
      for c in range(nch):
        rs = slice(c * CHUNK, (c + 1) * CHUNK)
        vb = vn_sc[rs, cs]
        mix_sc[rs, cs] = jnp.where(head0, _dot(w0, vb), _dot(w1, vb)) + bias
        dmx = dmx_sc[rs, cs]
        d0 = _bf(jnp.where(head0, dmx, 0.0))
        d1 = _bf(jnp.where(head0, 0.0, dmx))
        dvn_sc[rs, cs] = _dot_tn(w0, d0) + _dot_tn(w1, d1)
        dw0 = dw0 + _dot_nt(d0, vb)
        dw1 = dw1 + _dot_nt(d1, vb)
        dbx = dbx + dmx
      dw_ref[2 * jp] += dw0
      dw_ref[2 * jp + 1] += dw1
      dbx_sc[:, cs] += dbx

    du_ref[...] = _bf(dmix * mix_sc[...] * gu_der)
    dvn = dvn_sc[...]
    dv_ref[...] = _bf(_ln_bwd(dvn, xhat, rstd, g) * gv_der)
    dg_ref[...] += jnp.sum(dvn * xhat, axis=0, keepdims=True)
    db_ref[...] += jnp.sum(dvn, axis=0, keepdims=True)

    @pl.when(i == nsteps - 1)
    def _():
      lane = lax.broadcasted_iota(jnp.int32, (CHUNK, LANES), 1)
      acc = jnp.zeros((CHUNK, LANES), F32)
      for gi in range(N_SGU_GROUPS):
        jp, j = gi // 2, gi % 2
        part = dbx_sc[:, jp * LANES:(jp + 1) * LANES]
        hm = (lane < 64) if j == 0 else (lane >= 64)
        colsum = jnp.sum(jnp.where(hm, part, 0.0), axis=1, keepdims=True)
        acc = jnp.where(lane == gi, colsum, acc)
        dw_ref[gi] = jnp.where(tril, dw_ref[gi], 0.0)
      dbs_ref[...] = acc

  blk = lambda j: pl.BlockSpec((tr, SGU_WIDTH), lambda i: (i, j))
  vec = pl.BlockSpec((1, SGU_WIDTH), lambda i: (0, 0))
  wspec = pl.BlockSpec((N_SGU_GROUPS, CHUNK, CHUNK), lambda i: (0, 0, 0))
  big = lambda dt: pltpu.VMEM((tr, SGU_WIDTH), dt)
  du, dv, dw, dbs, dg, db = pl.pallas_call(
      body, name=name, grid=(nsteps,),
      in_specs=[blk(0), blk(1), blk(0), vec, vec, wspec, pl.BlockSpec((CHUNK, SGU_WIDTH), lambda i: (0, 0))],
      out_specs=[blk(0), blk(0), wspec, pl.BlockSpec((CHUNK, LANES), lambda i: (0, 0)), vec, vec],
      out_shape=[jax.ShapeDtypeStruct((t, SGU_WIDTH), BF16), jax.ShapeDtypeStruct((t, SGU_WIDTH), BF16),
                 jax.ShapeDtypeStruct((N_SGU_GROUPS, CHUNK, CHUNK), F32), jax.ShapeDtypeStruct((CHUNK, LANES), F32),
                 jax.ShapeDtypeStruct((1, SGU_WIDTH), F32), jax.ShapeDtypeStruct((1, SGU_WIDTH), F32)],
      scratch_shapes=[big(BF16), big(F32), big(F32), big(F32), pltpu.VMEM((CHUNK, SGU_WIDTH), F32)],
      compiler_params=_params(("arbitrary",)),
  )(h2, h2, dcat, ln_g.reshape(1, -1), ln_b.reshape(1, -1), w_s, _sgu_bias_lanes(b_s))
  return du, dv, dw, dbs[:, :N_SGU_GROUPS].T, dg[0], db[0]


def _loss_head(xo, tgt, name, tm=512):
  m, d = xo.shape
  tm = _tile(m, tm)

  def body(x_ref, t_ref, dx_ref, l_ref):
    @pl.when(pl.program_id(0) == 0)
    def _():
      l_ref[...] = jnp.zeros_like(l_ref)

    diff = x_ref[...] - t_ref[...]
    dx_ref[...] = diff * (1.0 / d)
    rowsum = jnp.sum(diff * diff, axis=1, keepdims=True)
    tot = jnp.sum(rowsum, axis=0, keepdims=True) * (0.5 / d)
    l_ref[...] += jnp.broadcast_to(tot, l_ref.shape)

  row = pl.BlockSpec((tm, d), lambda i: (i, 0))
  dx, l = pl.pallas_call(
      body, name=name, grid=(m // tm,), in_specs=[row, row],
      out_specs=[row, pl.BlockSpec((8, LANES), lambda i: (0, 0))],
      out_shape=[jax.ShapeDtypeStruct((m, d), F32), jax.ShapeDtypeStruct((8, LANES), F32)],
      compiler_params=_params(("arbitrary",)),
  )(xo, tgt)
  return l[0, 0], dx


def _local_step(x3, mem3, tgt3, w):
  bl, s, d = x3.shape
  t = bl * s
  nm = mem3.shape[1]
  mem2 = mem3.reshape(bl * nm, d)
  x = x3.reshape(t, d)
  xb = x
  saved = []
  for i in range(DEPTH):
    j = i // 2
    attn = i % 2 == 0
    mkv = _mm(mem2, w["w_mem_kv"][i], "nn", F32, f"mkv_fwd_{i}", tm=1024, tn=512, tk=1024)
    mkv3 = mkv.reshape(bl, nm, 2 * MEM_WIDTH)
    w_in = w["a_w_in"][j] if attn else w["b_w_in"][j]
    h = _mm(xb, w_in, "nn", F32, f"in_proj_{i}", tm=1024, tn=512, tk=1024)
    h3 = h.reshape(bl, s, -1)
    if attn:
      mix3, lse3 = _attn_fwd(h3, f"dil_attn_fwd_{i}")
      mix = mix3.reshape(t, DIL_WIDTH)
      qcol = 3 * DIL_WIDTH // MEM_WIDTH
    else:
      mix = _sgu_fwd(h, w["sgu_ln_g"][j], w["sgu_ln_b"][j], w["sgu_w_s"][j], w["sgu_b_s"][j], f"sgu_fwd_{i}")
      lse3 = None
      qcol = 2 * SGU_WIDTH // MEM_WIDTH
    mo = _mem_fwd(h3, qcol, mkv3, f"mem_attn_fwd_{i}").reshape(t, MEM_WIDTH)
    cat = jnp.concatenate([mix, mo], axis=1).astype(BF16)
    z1, xm, xmb = _mm_res_ln(cat, w["w_out"][i], x, w["ln_mix_g"][i], w["ln_mix_b"][i], f"out_proj_ln_{i}", tk=1024)
    a, b, hm = _ffn_up(xmb, w["w_gate"][i], w["w_up"][i], f"ffn_up_{i}")
    z2, xo, xob = _mm_res_ln(hm, w["w_down"][i], xm, w["ln_ffn_g"][i], w["ln_ffn_b"][i], f"ffn_down_ln_{i}", tk=1408)
    saved.append(dict(xb=xb, h=h, h3=h3, mkv3=mkv3, mix3=(mix3 if attn else None), lse3=lse3, cat=cat, z1=z1,
                      xmb=xmb, a=a, b=b, hm=hm, z2=z2, qcol=qcol))
    x, xb = xo, xob

  loss, dx = _loss_head(x, tgt3.reshape(t, d), "loss_head")

  names = ("a_w_in", "b_w_in", "sgu_ln_g", "sgu_ln_b", "sgu_w_s", "sgu_b_s", "w_mem_kv", "w_out",
           "ln_mix_g", "ln_mix_b", "w_gate", "w_up", "w_down", "ln_ffn_g", "ln_ffn_b")
  grads = {n: [None] * w[n].shape[0] for n in names}
  for i in reversed(range(DEPTH)):
    j = i // 2
    attn = i % 2 == 0
    sv = saved[i]
    dz2, dz2b, grads["ln_ffn_g"][i], grads["ln_ffn_b"][i] = _ln_bwd_call(dx, sv["z2"], w["ln_ffn_g"][i], f"ln_ffn_bwd_{i}")
    da, db = _ffn_bwd_hidden(dz2b, w["w_down"][i], sv["a"], sv["b"], f"ffn_bwd_hidden_{i}")
    grads["w_down"][i] = _mm(sv["hm"], dz2b, "tn", F32, f"dw_down_{i}", tm=1408, tn=1024, tk=1024)
    grads["w_gate"][i] = _mm(sv["xmb"], da, "tn", F32, f"dw_gate_{i}", tm=1024, tn=1408, tk=1024)
    grads["w_up"][i] = _mm(sv["xmb"], db, "tn", F32, f"dw_up_{i}", tm=1024, tn=1408, tk=1024)
    dxm = _ffn_bwd_input(da, db, w["w_gate"][i], w["w_up"][i], dz2, f"ffn_bwd_input_{i}")
    dz1, dz1b, grads["ln_mix_g"][i], grads["ln_mix_b"][i] = _ln_bwd_call(dxm, sv["z1"], w["ln_mix_g"][i], f"ln_mix_bwd_{i}")
    grads["w_out"][i] = _mm(sv["cat"], dz1b, "tn", F32, f"dw_out_{i}", tm=1024, tn=1024, tk=1024)
    dcat = _mm(dz1b, w["w_out"][i], "nt", F32, f"out_proj_bwd_{i}", tm=1024, tn=1024, tk=1024)
    dcat3 = dcat.reshape(bl, s, -1)
    dqm3, dmkv3 = _mem_bwd(sv["h3"], sv["qcol"], sv["mkv3"], dcat3, f"mem_attn_bwd_{i}")
    grads["w_mem_kv"][i] = _mm(mem2, dmkv3.reshape(bl * nm, 2 * MEM_WIDTH), "tn", F32, f"dw_mem_kv_{i}", tm=1024, tn=512, tk=1024)
    dqm = dqm3.reshape(t, MEM_WIDTH).astype(BF16)
    if attn:
      dq3, dk3, dv3 = _attn_bwd(sv["h3"], sv["mix3"], sv["lse3"], dcat3, f"dil_attn_bwd_{i}")
      parts = [dq3.reshape(t, -1).astype(BF16), dk3.reshape(t, -1).astype(BF16), dv3.reshape(t, -1).astype(BF16), dqm]
    else:
      du, dv, dws, dbs, dlg, dlb = _sgu_bwd(sv["h"], dcat, w["sgu_ln_g"][j], w["sgu_ln_b"][j], w["sgu_w_s"][j],
                                             w["sgu_b_s"][j], f"sgu_bwd_{i}")
      grads["sgu_w_s"][j], grads["sgu_b_s"][j], grads["sgu_ln_g"][j], grads["sgu_ln_b"][j] = dws, dbs, dlg, dlb
      parts = [du, dv, dqm]
    dh = jnp.concatenate(parts, axis=1)
    w_in = w["a_w_in"][j] if attn else w["b_w_in"][j]
    grads["a_w_in" if attn else "b_w_in"][j] = _mm(sv["xb"], dh, "tn", F32, f"dw_in_{i}", tm=1024, tn=896 if not attn else 640, tk=1024)
    dx = _mm(dh, w_in, "nt", F32, f"in_proj_bwd_{i}", add=dz1, add_scale=DN_ALPHA, tm=1024, tn=1024, tk=896 if not attn else 640)
  return loss, dx.reshape(bl, s, d), grads


def _my_place():
  return lax.axis_index("x"), lax.axis_index("y"), lax.axis_index("c")


def _other_chips(x, y):
  return [(1 - x, y), (x, 1 - y), (1 - x, 1 - y)]


ANY = pl.BlockSpec(memory_space=pl.ANY)


def _all_gather_halves(wl, name):
  _, r, c_ = wl.shape

  def body(w_ref, g_ref, send_sems, recv_sems):
    x, y, c = _my_place()
    me = 2 * x + y
    sibling = (x, y, 1 - c)
    chips = _other_chips(x, y)

    def copy(k, src, dst, to):
      return pltpu.make_async_remote_copy(src_ref=src, dst_ref=dst, send_sem=send_sems.at[k], recv_sem=recv_sems.at[k],
                                          device_id=to, device_id_type=MESH_ID)

    first = [copy(k, w_ref.at[c], g_ref.at[me, c], (px, py, c)) for k, (px, py) in enumerate(chips)]
    for cp in first:
      cp.start()
    passed = []
    for k, (px, py) in enumerate(chips):
      landed = g_ref.at[2 * px + py, c]
      copy(k, landed, landed, (px, py, c)).wait_recv()
      fwd = copy(3 + k, landed, landed, sibling)
      fwd.start()
      passed.append(fwd)
    for k, (px, py) in enumerate(chips):
      theirs = g_ref.at[2 * px + py, 1 - c]
      copy(3 + k, theirs, theirs, sibling).wait_recv()
    for cp in first + passed:
      cp.wait_send()

  got = pl.pallas_call(
      body, name=name, in_specs=[ANY], out_specs=ANY,
      out_shape=jax.ShapeDtypeStruct((4, 2, r, c_), wl.dtype),
      scratch_shapes=[pltpu.SemaphoreType.DMA((6,)), pltpu.SemaphoreType.DMA((6,))],
  )(wl)
  chip = 2 * lax.axis_index("x") + lax.axis_index("y")
  return lax.dynamic_update_slice(got, wl[None], (chip, 0, 0, 0))


def _sibling_swap(v, name):
  def body(v_ref, o_ref, send_sem, recv_sem):
    x, y, c = _my_place()
    cp = pltpu.make_async_remote_copy(src_ref=v_ref, dst_ref=o_ref, send_sem=send_sem, recv_sem=recv_sem,
                                      device_id=(x, y, 1 - c), device_id_type=MESH_ID)
    cp.start()
    cp.wait()

  return pl.pallas_call(
      body, name=name, in_specs=[ANY], out_specs=ANY, out_shape=jax.ShapeDtypeStruct(v.shape, v.dtype),
      scratch_shapes=[pltpu.SemaphoreType.DMA, pltpu.SemaphoreType.DMA],
  )(v)


def _chip_exchange(q, name):
  _, r, c_ = q.shape

  def body(q_ref, o_ref, send_sems, recv_sems):
    x, y, c = _my_place()
    cps = []
    for k, (px, py) in enumerate(_other_chips(x, y)):
      cp = pltpu.make_async_remote_copy(src_ref=q_ref.at[2 * px + py], dst_ref=o_ref.at[k], send_sem=send_sems.at[k],
                                        recv_sem=recv_sems.at[k], device_id=(px, py, c), device_id_type=MESH_ID)
      cp.start()
      cps.append(cp)
    for cp in cps:
      cp.wait()

  return pl.pallas_call(
      body, name=name, in_specs=[ANY], out_specs=ANY, out_shape=jax.ShapeDtypeStruct((3, r, c_), q.dtype),
      scratch_shapes=[pltpu.SemaphoreType.DMA((3,)), pltpu.SemaphoreType.DMA((3,))],
  )(q)


def _share_halves(v, name):
  theirs = _sibling_swap(v, name)
  c = lax.axis_index("c")
  return jnp.where(c == 0, jnp.concatenate([v, theirs]), jnp.concatenate([theirs, v]))


def _half_spec(tr, c_, pick):
  return pl.BlockSpec((None, None, tr, c_), lambda s, r, place: (s, pick(place), r, 0))


def _cast_other_half(p, place, name, tr=512):
  _, _, r, c_ = p.shape
  tr = _tile(r, tr, 16)

  def body(place_ref, p_ref, o_ref):
    o_ref[...] = _bf(p_ref[...])

  out_spec = pl.BlockSpec((None, tr, c_), lambda s, rr, place: (s, rr, 0))
  return pl.pallas_call(
      body, name=name, out_shape=jax.ShapeDtypeStruct((4, r, c_), BF16),
      grid_spec=pltpu.PrefetchScalarGridSpec(num_scalar_prefetch=1, grid=(4, r // tr),
                                             in_specs=[_half_spec(tr, c_, lambda place: 1 - place[1])], out_specs=out_spec),
      compiler_params=_params(("parallel", "parallel")),
  )(place, p)


def _add_sibling(p, x1, place, name, tr=512):
  _, _, r, c_ = p.shape
  tr = _tile(r, tr, 16)

  def body(place_ref, p_ref, x_ref, o_ref):
    o_ref[...] = _bf(p_ref[...] + x_ref[...].astype(F32))

  row = pl.BlockSpec((None, tr, c_), lambda s, rr, place: (s, rr, 0))
  return pl.pallas_call(
      body, name=name, out_shape=jax.ShapeDtypeStruct((4, r, c_), BF16),
      grid_spec=pltpu.PrefetchScalarGridSpec(num_scalar_prefetch=1, grid=(4, r // tr),
                                             in_specs=[_half_spec(tr, c_, lambda place: place[1]), row], out_specs=row),
      compiler_params=_params(("parallel", "parallel")),
  )(place, p, x1)


def _sum_own(p, x1, x3, place, name, tr=512):
  _, _, r, c_ = p.shape
  tr = _tile(r, tr, 16)

  def body(place_ref, p_ref, x1_ref, x3_ref, o_ref):
    acc = p_ref[...] + x1_ref[...].astype(F32)
    for k in range(3):
      acc = acc + x3_ref[k].astype(F32)
    o_ref[...] = acc

  return pl.pallas_call(
      body, name=name, out_shape=jax.ShapeDtypeStruct((r, c_), F32),
      grid_spec=pltpu.PrefetchScalarGridSpec(
          num_scalar_prefetch=1, grid=(r // tr,),
          in_specs=[pl.BlockSpec((None, None, tr, c_), lambda rr, place: (place[0], place[1], rr, 0)),
                    pl.BlockSpec((None, tr, c_), lambda rr, place: (place[0], rr, 0)),
                    pl.BlockSpec((3, tr, c_), lambda rr, place: (0, rr, 0))],
          out_specs=pl.BlockSpec((tr, c_), lambda rr, place: (rr, 0))),
      compiler_params=_params(("parallel",)),
  )(place, p, x1, x3)


def _reduce_scatter(p):
  x, y, c = _my_place()
  place = jnp.stack([2 * x + y, c]).astype(jnp.int32)
  x1 = _sibling_swap(_cast_other_half(p, place, "rs_cast_other_half"), "rs_sibling_swap")
  q = _add_sibling(p, x1, place, "rs_add_sibling")
  x3 = _chip_exchange(q, "rs_chip_exchange")
  mine = _sum_own(p, x1, x3, place, "rs_sum_own")
  return _share_halves(mine, "rs_share_halves")


def _adamw(w, g, m, v, name):
  shape = w.shape
  cols = shape[-1]
  rows = w.size // cols
  tr = _tile(rows, max(8, (256 * 1024) // cols // 8 * 8), 8)

  def body(w_ref, g_ref, m_ref, v_ref, d_ref, nm_ref, nv_ref):
    gv = g_ref[...]
    nm = ADAM_B1 * m_ref[...] + (1.0 - ADAM_B1) * gv
    nv = ADAM_B2 * v_ref[...] + (1.0 - ADAM_B2) * (gv * gv)
    m_hat = nm / (1.0 - ADAM_B1 ** ADAM_STEP)
    v_hat = nv / (1.0 - ADAM_B2 ** ADAM_STEP)
    d_ref[...] = -ADAM_LR * (m_hat / (jnp.sqrt(v_hat) + ADAM_EPS) + ADAM_WD * w_ref[...])
    nm_ref[...] = nm
    nv_ref[...] = nv

  spec = pl.BlockSpec((tr, cols), lambda i: (i, 0))
  sds = jax.ShapeDtypeStruct((rows, cols), F32)
  outs = pl.pallas_call(
      body, name=name, grid=(rows // tr,), in_specs=[spec] * 4, out_specs=[spec] * 3, out_shape=[sds] * 3,
      compiler_params=_params(("parallel",)),
  )(*(t.reshape(rows, cols) for t in (w, g, m, v)))
  return tuple(o.reshape(shape) for o in outs)


SHARDED = (("a_w_in", 2), ("b_w_in", 2), ("w_mem_kv", 1), ("w_out", 1), ("w_gate", 2), ("w_up", 2), ("w_down", 1))
SMALL_SHARDED = (("sgu_ln_g", 1), ("sgu_ln_b", 1))
REPLICATED = ("sgu_w_s", "sgu_b_s", "ln_mix_g", "ln_mix_b", "ln_ffn_g", "ln_ffn_b")
SMALL_ORDER = ("sgu_w_s", "sgu_b_s", "ln_mix_g", "ln_mix_b", "ln_ffn_g", "ln_ffn_b", "sgu_ln_g", "sgu_ln_b")
ROW_ALIGN = 16


def _pad_to(v, n):
  return jnp.pad(v, (0, n - v.shape[0]))


def _round_up(n, a):
  return -(-n // a) * a


def _to_shard_major(full, axis):
  shp = full.shape
  cut = shp[:axis] + (4, shp[axis] // 4) + shp[axis + 1:]
  return jnp.moveaxis(full.reshape(cut), axis, 0).reshape(4, -1, FLAT_COLS)


def _from_shard_major(rows, shard_shape, axis):
  full = jnp.moveaxis(rows.reshape((4,) + tuple(shard_shape)), 0, axis)
  shp = full.shape
  return full.reshape(shp[:axis] + (shp[axis] * shp[axis + 1],) + shp[axis + 2:])


def _gather_weights(shards):
  segs = [shards[n].astype(BF16).reshape(-1, FLAT_COLS) for n, _ in SHARDED]
  small = jnp.concatenate([lax.bitcast_convert_type(shards[n], BF16).reshape(-1) for n, _ in SMALL_SHARDED])
  small_rows = _round_up(small.shape[0], ROW_ALIGN * FLAT_COLS) // FLAT_COLS
  segs.append(_pad_to(small, small_rows * FLAT_COLS).reshape(small_rows, FLAT_COLS))
  rows = sum(sg.shape[0] for sg in segs)
  rows_pad = _round_up(rows, 2 * ROW_ALIGN)
  if rows_pad > rows:
    segs.append(jnp.zeros((rows_pad - rows, FLAT_COLS), BF16))
  flat = jnp.concatenate(segs).reshape(2, rows_pad // 2, FLAT_COLS)
  g = _all_gather_halves(flat, "gather_weights").reshape(4, rows_pad, FLAT_COLS)
  out, off = {}, 0
  for n, axis in SHARDED:
    nr = shards[n].size // FLAT_COLS
    out[n] = _from_shard_major(g[:, off:off + nr], shards[n].shape, axis)
    off += nr
  small_g = g[:, off:off + small_rows].reshape(4, small_rows * FLAT_COLS)
  off = 0
  for n, axis in SMALL_SHARDED:
    sz = 2 * shards[n].size
    vals = lax.bitcast_convert_type(small_g[:, off:off + sz].reshape((4,) + shards[n].shape + (2,)), F32)
    out[n] = _from_shard_major(vals, shards[n].shape, axis)
    off += sz
  return out


def _reduce_grads(grads, shard_shapes):
  segs = []
  for n, axis in SHARDED:
    for g in grads[n]:
      segs.append(_to_shard_major(g, axis - 1))
  big_rows = sum(sg.shape[1] for sg in segs)
  small_full = {n: jnp.stack(grads[n]) for n in SMALL_ORDER}
  small = jnp.concatenate([small_full[n].reshape(-1) for n in SMALL_ORDER])
  n_small = _round_up(small.shape[0], 4 * 2 * 8 * FLAT_COLS)
  quarter_rows = n_small // (4 * FLAT_COLS)
  segs.append(_pad_to(small, n_small).reshape(4, quarter_rows, FLAT_COLS))
  rows_pad = _round_up(big_rows + quarter_rows, 2 * ROW_ALIGN)
  if rows_pad > big_rows + quarter_rows:
    segs.append(jnp.zeros((4, rows_pad - big_rows - quarter_rows, FLAT_COLS), F32))
  p = jnp.concatenate(segs, axis=1).reshape(4, 2, rows_pad // 2, FLAT_COLS)
  mine = _reduce_scatter(p)
  out, off = {}, 0
  for n, _ in SHARDED:
    nr = math.prod(shard_shapes[n]) // FLAT_COLS
    out[n] = mine[off:off + nr].reshape(shard_shapes[n])
    off += nr
  piece = mine[big_rows:big_rows + quarter_rows].reshape(2, quarter_rows // 2, FLAT_COLS)
  small_sum = _all_gather_halves(piece, "gather_small_grads").reshape(n_small)
  off = 0
  for n in SMALL_ORDER:
    sz = small_full[n].size
    out[n] = small_sum[off:off + sz].reshape(small_full[n].shape)
    off += sz
  return out


WEIGHT_NAMES = ("a_w_in", "b_w_in", "sgu_ln_g", "sgu_ln_b", "sgu_w_s", "sgu_b_s", "w_mem_kv", "w_out",
                "ln_mix_g", "ln_mix_b", "w_gate", "w_up", "w_down", "ln_ffn_g", "ln_ffn_b")


def kernel(x, mem, a_w_in, b_w_in, sgu_ln_g, sgu_ln_b, sgu_w_s, sgu_b_s, w_mem_kv, w_out, ln_mix_g, ln_mix_b, w_gate, w_up, w_down, ln_ffn_g, ln_ffn_b, loss_target, m_a_w_in, m_b_w_in, m_sgu_ln_g, m_sgu_ln_b, m_sgu_w_s, m_sgu_b_s, m_w_mem_kv, m_w_out, m_ln_mix_g, m_ln_mix_b, m_w_gate, m_w_up, m_w_down, m_ln_ffn_g, m_ln_ffn_b, v_a_w_in, v_b_w_in, v_sgu_ln_g, v_sgu_ln_b, v_sgu_w_s, v_sgu_b_s, v_w_mem_kv, v_w_out, v_ln_mix_g, v_ln_mix_b, v_w_gate, v_w_up, v_w_down, v_ln_ffn_g, v_ln_ffn_b):
  weights = dict(a_w_in=a_w_in, b_w_in=b_w_in, sgu_ln_g=sgu_ln_g, sgu_ln_b=sgu_ln_b, sgu_w_s=sgu_w_s, sgu_b_s=sgu_b_s,
                 w_mem_kv=w_mem_kv, w_out=w_out, ln_mix_g=ln_mix_g, ln_mix_b=ln_mix_b, w_gate=w_gate, w_up=w_up,
                 w_down=w_down, ln_ffn_g=ln_ffn_g, ln_ffn_b=ln_ffn_b)
  mom1 = dict(a_w_in=m_a_w_in, b_w_in=m_b_w_in, sgu_ln_g=m_sgu_ln_g, sgu_ln_b=m_sgu_ln_b, sgu_w_s=m_sgu_w_s,
              sgu_b_s=m_sgu_b_s, w_mem_kv=m_w_mem_kv, w_out=m_w_out, ln_mix_g=m_ln_mix_g, ln_mix_b=m_ln_mix_b,
              w_gate=m_w_gate, w_up=m_w_up, w_down=m_w_down, ln_ffn_g=m_ln_ffn_g, ln_ffn_b=m_ln_ffn_b)
  mom2 = dict(a_w_in=v_a_w_in, b_w_in=v_b_w_in, sgu_ln_g=v_sgu_ln_g, sgu_ln_b=v_sgu_ln_b, sgu_w_s=v_sgu_w_s,
              sgu_b_s=v_sgu_b_s, w_mem_kv=v_w_mem_kv, w_out=v_w_out, ln_mix_g=v_ln_mix_g, ln_mix_b=v_ln_mix_b,
              w_gate=v_w_gate, w_up=v_w_up, w_down=v_w_down, ln_ffn_g=v_ln_ffn_g, ln_ffn_b=v_ln_ffn_b)

  full = _gather_weights(weights)
  for n in REPLICATED:
    full[n] = weights[n]
  loss_part, grad_x, grads = _local_step(x, mem, loss_target, full)
  loss = lax.psum(loss_part, MESH_AXES)

  shard_shapes = {n: weights[n].shape for n, _ in SHARDED}
  red = _reduce_grads(grads, shard_shapes)
  chip = 2 * lax.axis_index("x") + lax.axis_index("y")
  for n, axis in SMALL_SHARDED:
    width = weights[n].shape[axis]
    red[n] = lax.dynamic_slice_in_dim(red[n], chip * width, width, axis)

  small_names = SMALL_ORDER
  def pack(d):
    flat = jnp.concatenate([d[n].reshape(-1) for n in small_names])
    return _pad_to(flat, _round_up(flat.shape[0], 8 * FLAT_COLS)).reshape(-1, FLAT_COLS)
  small_out = _adamw(pack(weights), pack(red), pack(mom1), pack(mom2), "adamw_small")
  delta, new_m, new_v = {}, {}, {}
  off = 0
  for n in small_names:
    sz = weights[n].size
    for dst, src in zip((delta, new_m, new_v), small_out):
      dst[n] = src.reshape(-1)[off:off + sz].reshape(weights[n].shape)
    off += sz
  for n, _ in SHARDED:
    delta[n], new_m[n], new_v[n] = _adamw(weights[n], red[n], mom1[n], mom2[n], f"adamw_{n}")

  return (loss, grad_x, *[red[n] for n in WEIGHT_NAMES], *[delta[n] for n in WEIGHT_NAMES],
          *[new_m[n] for n in WEIGHT_NAMES], *[new_v[n] for n in WEIGHT_NAMES])
```

```python
import functools
import math

import jax
import jax.numpy as jnp
from jax import lax
from jax.experimental import pallas as pl
from jax.experimental.pallas import tpu as pltpu

F32 = jnp.float32
BF16 = jnp.bfloat16

DEPTH = 4
HEAD_DIM = 64
N_DIL_HEADS = 12
DIL_WIDTH = N_DIL_HEADS * HEAD_DIM
DIL_PATTERNS = ((128, 1), (512, 4), (2048, 16))
BLOCK = 128
N_SGU_GROUPS = 12
SGU_WIDTH = N_SGU_GROUPS * 64
CHUNK = 128
N_MEM_HEADS = 4
MEM_WIDTH = N_MEM_HEADS * HEAD_DIM
DN_ALPHA = (2 * DEPTH) ** 0.25
LN_EPS = 1e-5
ATT_SCALE = HEAD_DIM ** -0.5
ADAM_LR = 0.001
ADAM_B1 = 0.9
ADAM_B2 = 0.999
ADAM_EPS = 1e-08
ADAM_WD = 0.01
ADAM_STEP = 10
NEG_BIG = -1e30
ATTN_UNROLL = 2

LANES = 128
FLAT_COLS = 1024
VMEM_LIMIT = 56 * 1024 * 1024
MESH_AXES = ("x", "y", "c")
MESH_ID = pl.DeviceIdType.MESH


def _tile(n, pref, align=LANES):
  if n <= pref:
    return n
  t = (pref // align) * align
  while t >= align:
    if n % t == 0:
      return t
    t -= align
  return n


def _params(sem):
  return pltpu.CompilerParams(dimension_semantics=sem, vmem_limit_bytes=VMEM_LIMIT)


def _dot(a, b):
  return jnp.dot(a, b, preferred_element_type=F32)


def _dot_nt(a, b):
  return lax.dot_general(a, b, (((1,), (1,)), ((), ())), preferred_element_type=F32)


def _dot_tn(a, b):
  return lax.dot_general(a, b, (((0,), (0,)), ((), ())), preferred_element_type=F32)


def _bf(v):
  return v.astype(BF16)


def _ln_stats(z):
  mu = jnp.mean(z, axis=-1, keepdims=True)
  zc = z - mu
  var = jnp.mean(zc * zc, axis=-1, keepdims=True)
  rstd = lax.rsqrt(var + LN_EPS)
  return zc * rstd, rstd


def _ln_bwd(dy, xhat, rstd, g):
  gdy = dy * g
  m1 = jnp.mean(gdy, axis=-1, keepdims=True)
  m2 = jnp.mean(gdy * xhat, axis=-1, keepdims=True)
  return rstd * (gdy - m1 - xhat * m2)


_GELU_C = math.sqrt(2.0 / math.pi)


def _gelu_parts(v):
  v2 = v * v
  t = jnp.tanh(_GELU_C * (v + 0.044715 * v * v2))
  val = 0.5 * v * (1.0 + t)
  der = 0.5 * (1.0 + t) + 0.5 * v * (1.0 - t * t) * (_GELU_C * (1.0 + 3.0 * 0.044715 * v2))
  return val, der


def _gelu(v):
  t = jnp.tanh(_GELU_C * (v + 0.044715 * v * v * v))
  return 0.5 * v * (1.0 + t)


def _sigmoid(v):
  return 1.0 / (1.0 + jnp.exp(-v))


def _mm(a, b, mode, out_dtype, name, add=None, add_scale=1.0, tm=512, tn=512, tk=512):
  if mode == "nn":
    (m, k), (k2, n) = a.shape, b.shape
  elif mode == "nt":
    (m, k), (n, k2) = a.shape, b.shape
  else:
    (k, m), (k2, n) = a.shape, b.shape
  assert k == k2, (a.shape, b.shape, mode)
  tm, tn, tk = _tile(m, tm), _tile(n, tn), _tile(k, tk)
  nk = k // tk
  if mode == "nn":
    a_spec = pl.BlockSpec((tm, tk), lambda i, j, kk: (i, kk))
    b_spec = pl.BlockSpec((tk, tn), lambda i, j, kk: (kk, j))
    dot = _dot
  elif mode == "nt":
    a_spec = pl.BlockSpec((tm, tk), lambda i, j, kk: (i, kk))
    b_spec = pl.BlockSpec((tn, tk), lambda i, j, kk: (j, kk))
    dot = _dot_nt
  else:
    a_spec = pl.BlockSpec((tk, tm), lambda i, j, kk: (kk, i))
    b_spec = pl.BlockSpec((tk, tn), lambda i, j, kk: (kk, j))
    dot = _dot_tn
  o_spec = pl.BlockSpec((tm, tn), lambda i, j, kk: (i, j))
  has_add = add is not None

  def body(*refs):
    if has_add:
      a_ref, b_ref, add_ref, o_ref, acc_ref = refs
    else:
      a_ref, b_ref, o_ref, acc_ref = refs
    kk = pl.program_id(2)

    @pl.when(kk == 0)
    def _():
      acc_ref[...] = jnp.zeros_like(acc_ref)

    acc_ref[...] += dot(_bf(a_ref[...]), _bf(b_ref[...]))

    @pl.when(kk == nk - 1)
    def _():
      r = acc_ref[...]
      if has_add:
        r = r + add_scale * add_ref[...].astype(F32)
      o_ref[...] = r.astype(out_dtype)

  in_specs = [a_spec, b_spec] + ([o_spec] if has_add else [])
  args = (a, b) + ((add,) if has_add else ())
  return pl.pallas_call(
      body, name=name, grid=(m // tm, n // tn, nk), in_specs=in_specs, out_specs=o_spec,
      out_shape=jax.ShapeDtypeStruct((m, n), out_dtype),
      scratch_shapes=[pltpu.VMEM((tm, tn), F32)],
      compiler_params=_params(("parallel", "parallel", "arbitrary")),
  )(*args)


def _mm_res_ln(a, w, res, g, b, name, tm=512, tk=512):
  m, k = a.shape
  d = w.shape[1]
  tm, tk = _tile(m, tm), _tile(k, tk)
  nk = k // tk

  def body(a_ref, w_ref, r_ref, g_ref, b_ref, z_ref, x_ref, xb_ref, acc_ref):
    kk = pl.program_id(1)

    @pl.when(kk == 0)
    def _():
      acc_ref[...] = jnp.zeros_like(acc_ref)

    acc_ref[...] += _dot(_bf(a_ref[...]), _bf(w_ref[...]))

    @pl.when(kk == nk - 1)
    def _():
      z = DN_ALPHA * r_ref[...] + acc_ref[...]
      xhat, _ = _ln_stats(z)
      xn = xhat * g_ref[...] + b_ref[...]
      z_ref[...] = z
      x_ref[...] = xn
      xb_ref[...] = _bf(xn)

  row = pl.BlockSpec((tm, d), lambda i, kk: (i, 0))
  vec = pl.BlockSpec((1, d), lambda i, kk: (0, 0))
  return pl.pallas_call(
      body, name=name, grid=(m // tm, nk),
      in_specs=[pl.BlockSpec((tm, tk), lambda i, kk: (i, kk)), pl.BlockSpec((tk, d), lambda i, kk: (kk, 0)), row, vec, vec],
      out_specs=[row, row, row],
      out_shape=[jax.ShapeDtypeStruct((m, d), F32), jax.ShapeDtypeStruct((m, d), F32), jax.ShapeDtypeStruct((m, d), BF16)],
      scratch_shapes=[pltpu.VMEM((tm, d), F32)],
      compiler_params=_params(("parallel", "arbitrary")),
  )(a, w, res, g.reshape(1, d), b.reshape(1, d))


def _ln_bwd_call(dy, z, g, name, tm=512):
  m, d = z.shape
  tm = _tile(m, tm)
  n = m // tm

  def body(dy_ref, z_ref, g_ref, dz_ref, dzb_ref, dg_ref, db_ref):
    i = pl.program_id(0)

    @pl.when(i == 0)
    def _():
      dg_ref[...] = jnp.zeros_like(dg_ref)
      db_ref[...] = jnp.zeros_like(db_ref)

    dy_v = dy_ref[...]
    xhat, rstd = _ln_stats(z_ref[...])
    dz = _ln_bwd(dy_v, xhat, rstd, g_ref[...])
    dz_ref[...] = dz
    dzb_ref[...] = _bf(dz)
    dg_ref[...] += jnp.sum(dy_v * xhat, axis=0, keepdims=True)
    db_ref[...] += jnp.sum(dy_v, axis=0, keepdims=True)

  row = pl.BlockSpec((tm, d), lambda i: (i, 0))
  vec = pl.BlockSpec((1, d), lambda i: (0, 0))
  dz, dzb, dg, db = pl.pallas_call(
      body, name=name, grid=(n,), in_specs=[row, row, vec], out_specs=[row, row, vec, vec],
      out_shape=[jax.ShapeDtypeStruct((m, d), F32), jax.ShapeDtypeStruct((m, d), BF16),
                 jax.ShapeDtypeStruct((1, d), F32), jax.ShapeDtypeStruct((1, d), F32)],
      compiler_params=_params(("arbitrary",)),
  )(dy, z, g.reshape(1, d))
  return dz, dzb, dg[0], db[0]


def _ffn_up(xb, wg, wu, name, tm=512, tn=1408):
  m, d = xb.shape
  f = wg.shape[1]
  tm, tn = _tile(m, tm), _tile(f, tn)

  def body(x_ref, wg_ref, wu_ref, a_ref, b_ref, h_ref):
    xv = x_ref[...]
    a = _dot(xv, wg_ref[...])
    b = _dot(xv, wu_ref[...])
    a_ref[...] = _bf(a)
    b_ref[...] = _bf(b)
    h_ref[...] = _bf(a * _sigmoid(a) * b)

  wspec = pl.BlockSpec((d, tn), lambda j, i: (0, j))
  ospec = pl.BlockSpec((tm, tn), lambda j, i: (i, j))
  sds = jax.ShapeDtypeStruct((m, f), BF16)
  return pl.pallas_call(
      body, name=name, grid=(f // tn, m // tm),
      in_specs=[pl.BlockSpec((tm, d), lambda j, i: (i, 0)), wspec, wspec],
      out_specs=[ospec, ospec, ospec], out_shape=[sds, sds, sds],
      compiler_params=_params(("parallel", "parallel")),
  )(xb, wg, wu)


def _ffn_bwd_hidden(dzb, wd, a, b, name, tm=512, tn=1408):
  m, d = dzb.shape
  f = wd.shape[0]
  tm, tn = _tile(m, tm), _tile(f, tn)

  def body(dz_ref, wd_ref, a_ref, b_ref, da_ref, db_ref):
    dh = _dot_nt(dz_ref[...], wd_ref[...])
    av = a_ref[...].astype(F32)
    bv = b_ref[...].astype(F32)
    sg = _sigmoid(av)
    da_ref[...] = _bf(dh * bv * (sg * (1.0 + av * (1.0 - sg))))
    db_ref[...] = _bf(dh * (av * sg))

  hspec = pl.BlockSpec((tm, tn), lambda j, i: (i, j))
  sds = jax.ShapeDtypeStruct((m, f), BF16)
  return pl.pallas_call(
      body, name=name, grid=(f // tn, m // tm),
      in_specs=[pl.BlockSpec((tm, d), lambda j, i: (i, 0)), pl.BlockSpec((tn, d), lambda j, i: (j, 0)), hspec, hspec],
      out_specs=[hspec, hspec], out_shape=[sds, sds],
      compiler_params=_params(("parallel", "parallel")),
  )(dzb, wd, a, b)


def _ffn_bwd_input(da, db, wg, wu, dz, name, tm=512, tk=1408):
  m, f = da.shape
  d = wg.shape[0]
  tm, tk = _tile(m, tm), _tile(f, tk)
  nk = f // tk

  def body(da_ref, db_ref, wg_ref, wu_ref, dz_ref, o_ref, acc_ref):
    kk = pl.program_id(1)

    @pl.when(kk == 0)
    def _():
      acc_ref[...] = jnp.zeros_like(acc_ref)

    acc_ref[...] += _dot_nt(da_ref[...], wg_ref[...]) + _dot_nt(db_ref[...], wu_ref[...])

    @pl.when(kk == nk - 1)
    def _():
      o_ref[...] = DN_ALPHA * dz_ref[...] + acc_ref[...]

  hspec = pl.BlockSpec((tm, tk), lambda i, kk: (i, kk))
  wspec = pl.BlockSpec((d, tk), lambda i, kk: (0, kk))
  row = pl.BlockSpec((tm, d), lambda i, kk: (i, 0))
  return pl.pallas_call(
      body, name=name, grid=(m // tm, nk), in_specs=[hspec, hspec, wspec, wspec, row], out_specs=row,
      out_shape=jax.ShapeDtypeStruct((m, d), F32), scratch_shapes=[pltpu.VMEM((tm, d), F32)],
      compiler_params=_params(("parallel", "arbitrary")),
  )(da, db, wg, wu, dz)


def _alibi_slopes():
  n = N_DIL_HEADS
  return jnp.exp2(-8.0 * (jnp.arange(n, dtype=F32) + 1.0) / n).reshape(1, n)


def _band_consts():
  qi = lax.broadcasted_iota(jnp.int32, (BLOCK, BLOCK), 0)
  ki = lax.broadcasted_iota(jnp.int32, (BLOCK, BLOCK), 1)
  steps_cur = (qi - ki).astype(F32)
  steps_prev = (qi + BLOCK - ki).astype(F32)
  return ki < 64, steps_cur, steps_prev, ki <= qi, ki >= qi


def _rows(start, d):
  if d == 1:
    return pl.ds(pl.multiple_of(start, BLOCK), BLOCK)
  return pl.ds(start, BLOCK, stride=d)


def _fill_bias_tables(bias_sc, slope0, slope1):
  row = lax.broadcasted_iota(jnp.int32, (2 * BLOCK, 2 * BLOCK), 0)
  col = lax.broadcasted_iota(jnp.int32, (2 * BLOCK, 2 * BLOCK), 1)
  qi = jnp.bitwise_and(row, BLOCK - 1)
  ki = jnp.bitwise_and(col, BLOCK - 1)
  is_cur = col >= BLOCK
  steps = jnp.where(is_cur, qi - ki, qi + BLOCK - ki)
  valid = jnp.logical_and(steps >= 0, steps <= BLOCK)
  slope = jnp.where(row >= BLOCK, slope1, slope0)
  dist = slope * steps.astype(F32)
  for p, (_, d) in enumerate(DIL_PATTERNS):
    base = jnp.where(valid, -d * dist, NEG_BIG)
    bias_sc[2 * p] = base
    bias_sc[2 * p + 1] = jnp.where(is_cur, base, NEG_BIG)


def _stack_heads(v2, head0):
  return jnp.concatenate([jnp.where(head0, v2, 0.0), jnp.where(head0, 0.0, v2)], axis=0)


def _unstack_heads(v, head0):
  return jnp.where(head0, v[:BLOCK], v[BLOCK:])


def _block_rows(idx, d, nblk):
  r = idx // nblk
  n = idx % nblk
  cur = _rows(r + n * (BLOCK * d), d)
  prev = _rows(r + jnp.maximum(n - 1, 0) * (BLOCK * d), d)
  return cur, prev, n


def pair_tile(dt):
  return pltpu.VMEM((2 * BLOCK, 2 * BLOCK), dt)


def _two_stage_loop(nb, first_stage, second_stage, buf_a, buf_b):
  assert nb % 2 == 0

  def pair(t, carry):
    i = 2 * t + 1
    first_stage(i, buf_b)
    second_stage(i - 1, buf_a)
    first_stage(i + 1, buf_a)
    second_stage(i, buf_b)
    return carry

  first_stage(0, buf_a)
  lax.fori_loop(0, nb // 2 - 1, pair, 0)
  first_stage(nb - 1, buf_b)
  second_stage(nb - 2, buf_a)
  second_stage(nb - 1, buf_b)


def _attn_fwd(h3, name):
  bl, s, _ = h3.shape
  npair = N_DIL_HEADS // 2
  nb = s // BLOCK

  def body(sl_ref, q_ref, k_ref, v_ref, o_ref, lse_ref, o_sc, l_sc, bias_sc, s_a, s_b):
    hp = pl.program_id(1)
    head0 = lax.broadcasted_iota(jnp.int32, (BLOCK, LANES), 1) < 64
    _fill_bias_tables(bias_sc, sl_ref[0, 2 * hp], sl_ref[0, 2 * hp + 1])

    for p, (_, d) in enumerate(DIL_PATTERNS):
      nblk = (s // d) // BLOCK
      two = nblk > 1
      ks = slice(0, 2 * BLOCK) if two else slice(BLOCK, 2 * BLOCK)

      def scores(idx, buf, p=p, d=d, nblk=nblk, two=two, ks=ks):
        cur, prev, n = _block_rows(idx, d, nblk)
        qs = _bf(_stack_heads(q_ref[cur, :], head0) * ATT_SCALE)
        kb = _bf(jnp.concatenate([k_ref[prev, :], k_ref[cur, :]], axis=0)) if two else _bf(k_ref[cur, :])
        first = jnp.where(n == 0, 1, 0) if two else 0
        buf[:, ks] = _dot_nt(qs, kb) + bias_sc[2 * p + first, :, ks]

      def values(idx, buf, p=p, d=d, nblk=nblk, two=two, ks=ks):
        cur, prev, _ = _block_rows(idx, d, nblk)
        sc = buf[:, ks]
        mx = jnp.max(sc, axis=1, keepdims=True)
        pe = jnp.exp(sc - mx)
        den = jnp.sum(pe, axis=1, keepdims=True)
        vb = _bf(jnp.concatenate([v_ref[prev, :], v_ref[cur, :]], axis=0)) if two else _bf(v_ref[cur, :])
        acc = _dot(_bf(pe), vb) / den
        o_sc[p, cur, :] = _unstack_heads(acc, head0)
        l_sc[p, cur, :] = _unstack_heads(jnp.broadcast_to(mx + jnp.log(den), (2 * BLOCK, LANES)), head0)

      _two_stage_loop(nb, scores, values, s_a, s_b)

    def merge(i, carry):
      rows = pl.ds(pl.multiple_of(i * BLOCK, BLOCK), BLOCK)
      l0, l1, l2 = l_sc[0, rows, :], l_sc[1, rows, :], l_sc[2, rows, :]
      mx = jnp.maximum(jnp.maximum(l0, l1), l2)
      e0, e1, e2 = jnp.exp(l0 - mx), jnp.exp(l1 - mx), jnp.exp(l2 - mx)
      tot = e0 + e1 + e2
      o_ref[rows, :] = _bf((e0 * o_sc[0, rows, :] + e1 * o_sc[1, rows, :] + e2 * o_sc[2, rows, :]) / tot)
      lse_ref[rows, :] = mx + jnp.log(tot)
      return carry

    lax.fori_loop(0, nb, merge, 0)

  def col(off):
    return pl.BlockSpec((None, s, LANES), lambda b, p: (b, 0, off + p))

  return pl.pallas_call(
      body, name=name, grid=(bl, npair),
      in_specs=[pl.BlockSpec(memory_space=pltpu.SMEM), col(0), col(npair), col(2 * npair)],
      out_specs=[col(0), col(0)],
      out_shape=[jax.ShapeDtypeStruct((bl, s, DIL_WIDTH), BF16), jax.ShapeDtypeStruct((bl, s, DIL_WIDTH), F32)],
      scratch_shapes=[pltpu.VMEM((3, s, LANES), F32), pltpu.VMEM((3, s, LANES), F32),
                      pltpu.VMEM((6, 2 * BLOCK, 2 * BLOCK), F32), pair_tile(F32), pair_tile(F32)],
      compiler_params=_params(("parallel", "parallel")),
  )(_alibi_slopes(), h3, h3, h3)


def _attn_bwd(h3, out3, lse3, dcat3, name):
  bl, s, _ = h3.shape
  npair = N_DIL_HEADS // 2
  nb = s // BLOCK

  def body(sl_ref, q_ref, k_ref, v_ref, o_ref, l_ref, do_ref, dq_out, dk_out, dv_out,
           bias_sc, p_a, ds_a, p_b, ds_b, prod_sc, dq_ref, dk_ref, dv_ref):
    hp = pl.program_id(1)
    lane = lax.broadcasted_iota(jnp.int32, (BLOCK, LANES), 1)
    head0 = lane < 64
    _fill_bias_tables(bias_sc, sl_ref[0, 2 * hp], sl_ref[0, 2 * hp + 1])
    dq_ref[...] = jnp.zeros_like(dq_ref)
    dk_ref[...] = jnp.zeros_like(dk_ref)
    dv_ref[...] = jnp.zeros_like(dv_ref)
    prod_sc[...] = do_ref[...] * o_ref[...].astype(F32)

    def per_row(v2, pick0, pick1):
      return jnp.concatenate([jnp.sum(jnp.where(pick0, v2, 0.0), axis=1, keepdims=True),
                              jnp.sum(jnp.where(pick1, v2, 0.0), axis=1, keepdims=True)], axis=0)

    for p, (_, d) in enumerate(DIL_PATTERNS):
      nblk = (s // d) // BLOCK
      two = nblk > 1
      ks = slice(0, 2 * BLOCK) if two else slice(BLOCK, 2 * BLOCK)

      def operands(idx, d=d, nblk=nblk, two=two):
        cur, prev, n = _block_rows(idx, d, nblk)
        qs = _bf(_stack_heads(q_ref[cur, :], head0) * ATT_SCALE)
        dos = _bf(_stack_heads(do_ref[cur, :], head0))
        kb = _bf(jnp.concatenate([k_ref[prev, :], k_ref[cur, :]], axis=0)) if two else _bf(k_ref[cur, :])
        return cur, prev, n, qs, dos, kb

      def probs(idx, bufs, p=p, two=two, ks=ks, operands=operands):
        cur, prev, n, qs, dos, kb = operands(idx)
        vb = _bf(jnp.concatenate([v_ref[prev, :], v_ref[cur, :]], axis=0)) if two else _bf(v_ref[cur, :])
        lse = per_row(l_ref[cur, :], lane == 0, lane == 64)
        delta = per_row(prod_sc[cur, :], head0, jnp.logical_not(head0))
        first = jnp.where(n == 0, 1, 0) if two else 0
        pr = jnp.exp(_dot_nt(qs, kb) + bias_sc[2 * p + first, :, ks] - lse)
        bufs[0][:, ks] = _bf(pr)
        bufs[1][:, ks] = _bf(pr * (_dot_nt(dos, vb) - delta))

      def products(idx, bufs, two=two, ks=ks, operands=operands):
        cur, prev, _, qs, dos, kb = operands(idx)
        pr = bufs[0][:, ks]
        ds = bufs[1][:, ks]
        dq_ref[cur, :] += _unstack_heads(_dot(ds, kb), head0) * ATT_SCALE
        dkb = _dot_tn(ds, qs)
        dvb = _dot_tn(pr, dos)
        if two:
          dk_ref[prev, :] += dkb[:BLOCK]
          dv_ref[prev, :] += dvb[:BLOCK]
          dk_ref[cur, :] += dkb[BLOCK:]
          dv_ref[cur, :] += dvb[BLOCK:]
        else:
          dk_ref[cur, :] += dkb
          dv_ref[cur, :] += dvb

      _two_stage_loop(nb, probs, products, (p_a, ds_a), (p_b, ds_b))

    dq_out[...] = _bf(dq_ref[...])
    dk_out[...] = _bf(dk_ref[...])
    dv_out[...] = _bf(dv_ref[...])

  def col(off):
    return pl.BlockSpec((None, s, LANES), lambda b, p: (b, 0, off + p))

  sds = jax.ShapeDtypeStruct((bl, s, DIL_WIDTH), BF16)
  return pl.pallas_call(
      body, name=name, grid=(bl, npair),
      in_specs=[pl.BlockSpec(memory_space=pltpu.SMEM), col(0), col(npair), col(2 * npair), col(0), col(0), col(0)],
      out_specs=[col(0), col(0), col(0)], out_shape=[sds, sds, sds],
      scratch_shapes=[pltpu.VMEM((6, 2 * BLOCK, 2 * BLOCK), F32)] + [pair_tile(BF16)] * 4
      + [pltpu.VMEM((s, LANES), F32)] * 4,
      compiler_params=_params(("parallel", "parallel")),
  )(_alibi_slopes(), h3, h3, h3, out3, lse3, dcat3)


def _attn_fwd_old(h3, name):
  bl, s, _ = h3.shape
  npair = N_DIL_HEADS // 2

  def body(sl_ref, q_ref, k_ref, v_ref, o_ref, lse_ref, o_sc, l_sc):
    hp = pl.program_id(1)
    head0, steps_cur, steps_prev, mask_cur, mask_prev = _band_consts()
    slope = [sl_ref[0, 2 * hp], sl_ref[0, 2 * hp + 1]]

    for p, (_, d) in enumerate(DIL_PATTERNS):
      nblk = (s // d) // BLOCK
      has_prev_block = nblk > 1

      def blk(idx, carry, p=p, d=d, nblk=nblk, has_prev_block=has_prev_block):
        r = idx // nblk
        n = idx % nblk
        cur = _rows(r + n * (BLOCK * d), d)
        q2 = q_ref[cur, :]
        kc = _bf(k_ref[cur, :])
        vc = _bf(v_ref[cur, :])
        if has_prev_block:
          prev = _rows(r + jnp.maximum(n - 1, 0) * (BLOCK * d), d)
          kp = _bf(k_ref[prev, :])
          vp = _bf(v_ref[prev, :])
          first_block = jnp.where(n > 0, 0.0, NEG_BIG)
        outs, lses = [], []
        for j in range(2):
          hm = head0 if j == 0 else jnp.logical_not(head0)
          qj = _bf(jnp.where(hm, q2, 0.0) * ATT_SCALE)
          sc = _dot_nt(qj, kc) - (slope[j] * d) * steps_cur
          sc = jnp.where(mask_cur, sc, NEG_BIG)
          mx = jnp.max(sc, axis=1, keepdims=True)
          if has_prev_block:
            sp = _dot_nt(qj, kp) - (slope[j] * d) * steps_prev + first_block
            sp = jnp.where(mask_prev, sp, NEG_BIG)
            mx = jnp.maximum(mx, jnp.max(sp, axis=1, keepdims=True))
          pc = jnp.exp(sc - mx)
          den = jnp.sum(pc, axis=1, keepdims=True)
          acc = _dot(_bf(pc), vc)
          if has_prev_block:
            pp = jnp.exp(sp - mx)
            den = den + jnp.sum(pp, axis=1, keepdims=True)
            acc = acc + _dot(_bf(pp), vp)
          outs.append(acc / den)
          lses.append(mx + jnp.log(den))
        o_sc[p, cur, :] = jnp.where(head0, outs[0], outs[1])
        l_sc[p, cur, :] = jnp.where(head0, lses[0], lses[1])
        return carry

      lax.fori_loop(0, s // BLOCK, blk, 0, unroll=ATTN_UNROLL)

    def merge(i, carry):
      rows = pl.ds(pl.multiple_of(i * BLOCK, BLOCK), BLOCK)
      l0, l1, l2 = l_sc[0, rows, :], l_sc[1, rows, :], l_sc[2, rows, :]
      mx = jnp.maximum(jnp.maximum(l0, l1), l2)
      e0, e1, e2 = jnp.exp(l0 - mx), jnp.exp(l1 - mx), jnp.exp(l2 - mx)
      tot = e0 + e1 + e2
      o_ref[rows, :] = (e0 * o_sc[0, rows, :] + e1 * o_sc[1, rows, :] + e2 * o_sc[2, rows, :]) / tot
      lse_ref[rows, :] = mx + jnp.log(tot)
      return carry

    lax.fori_loop(0, s // BLOCK, merge, 0)

  def col(off):
    return pl.BlockSpec((None, s, LANES), lambda b, p: (b, 0, off + p))

  sds = jax.ShapeDtypeStruct((bl, s, DIL_WIDTH), F32)
  return pl.pallas_call(
      body, name=name, grid=(bl, npair),
      in_specs=[pl.BlockSpec(memory_space=pltpu.SMEM), col(0), col(npair), col(2 * npair)],
      out_specs=[col(0), col(0)], out_shape=[sds, sds],
      scratch_shapes=[pltpu.VMEM((3, s, LANES), F32), pltpu.VMEM((3, s, LANES), F32)],
      compiler_params=_params(("parallel", "parallel")),
  )(_alibi_slopes(), h3, h3, h3)


def _attn_bwd_old(h3, out3, lse3, dcat3, name):
  bl, s, _ = h3.shape
  npair = N_DIL_HEADS // 2

  def body(sl_ref, q_ref, k_ref, v_ref, o_ref, l_ref, do_ref, dq_ref, dk_ref, dv_ref):
    hp = pl.program_id(1)
    head0, steps_cur, steps_prev, mask_cur, mask_prev = _band_consts()
    lane = lax.broadcasted_iota(jnp.int32, (BLOCK, LANES), 1)
    slope = [sl_ref[0, 2 * hp], sl_ref[0, 2 * hp + 1]]
    dq_ref[...] = jnp.zeros_like(dq_ref)
    dk_ref[...] = jnp.zeros_like(dk_ref)
    dv_ref[...] = jnp.zeros_like(dv_ref)

    for p, (_, d) in enumerate(DIL_PATTERNS):
      nblk = (s // d) // BLOCK
      has_prev_block = nblk > 1

      def blk(idx, carry, d=d, nblk=nblk, has_prev_block=has_prev_block):
        r = idx // nblk
        n = idx % nblk
        cur = _rows(r + n * (BLOCK * d), d)
        q2 = q_ref[cur, :]
        do2 = do_ref[cur, :]
        l2 = l_ref[cur, :]
        prod = do2 * o_ref[cur, :]
        kc = _bf(k_ref[cur, :])
        vc = _bf(v_ref[cur, :])
        if has_prev_block:
          prev = _rows(r + jnp.maximum(n - 1, 0) * (BLOCK * d), d)
          kp = _bf(k_ref[prev, :])
          vp = _bf(v_ref[prev, :])
          first_block = jnp.where(n > 0, 0.0, NEG_BIG)
          dkp = jnp.zeros((BLOCK, LANES), F32)
          dvp = jnp.zeros((BLOCK, LANES), F32)
        dq2 = jnp.zeros((BLOCK, LANES), F32)
        dkc = jnp.zeros((BLOCK, LANES), F32)
        dvc = jnp.zeros((BLOCK, LANES), F32)
        for j in range(2):
          hm = head0 if j == 0 else jnp.logical_not(head0)
          qj = _bf(jnp.where(hm, q2, 0.0) * ATT_SCALE)
          doj = _bf(jnp.where(hm, do2, 0.0))
          lj = jnp.sum(jnp.where(lane == 64 * j, l2, 0.0), axis=1, keepdims=True)
          dj = jnp.sum(jnp.where(hm, prod, 0.0), axis=1, keepdims=True)
          sc = _dot_nt(qj, kc) - (slope[j] * d) * steps_cur
          pc = jnp.exp(jnp.where(mask_cur, sc - lj, NEG_BIG))
          dsc = _bf(pc * (_dot_nt(doj, vc) - dj))
          dq_j = _dot(dsc, kc)
          dkc = dkc + _dot_tn(dsc, qj)
          dvc = dvc + _dot_tn(_bf(pc), doj)
          if has_prev_block:
            sp = _dot_nt(qj, kp) - (slope[j] * d) * steps_prev + first_block
            pp = jnp.exp(jnp.where(mask_prev, sp - lj, NEG_BIG))
            dsp = _bf(pp * (_dot_nt(doj, vp) - dj))
            dq_j = dq_j + _dot(dsp, kp)
            dkp = dkp + _dot_tn(dsp, qj)
            dvp = dvp + _dot_tn(_bf(pp), doj)
          dq2 = dq2 + jnp.where(hm, dq_j, 0.0) * ATT_SCALE
        dq_ref[cur, :] += dq2
        dk_ref[cur, :] += dkc
        dv_ref[cur, :] += dvc
        if has_prev_block:
          dk_ref[prev, :] += dkp
          dv_ref[prev, :] += dvp
        return carry

      lax.fori_loop(0, s // BLOCK, blk, 0, unroll=ATTN_UNROLL)

  def col(off):
    return pl.BlockSpec((None, s, LANES), lambda b, p: (b, 0, off + p))

  sds = jax.ShapeDtypeStruct((bl, s, DIL_WIDTH), F32)
  return pl.pallas_call(
      body, name=name, grid=(bl, npair),
      in_specs=[pl.BlockSpec(memory_space=pltpu.SMEM), col(0), col(npair), col(2 * npair), col(0), col(0), col(0)],
      out_specs=[col(0), col(0), col(0)], out_shape=[sds, sds, sds],
      compiler_params=_params(("parallel", "parallel")),
  )(_alibi_slopes(), h3, h3, h3, out3, lse3, dcat3)


def _mem_heads(tq):
  lane = lax.broadcasted_iota(jnp.int32, (tq, LANES), 1)
  return lane < 64


def _mem_fwd(h3, qcol, mkv3, name, tq=512):
  bl, s, _ = h3.shape
  nm = mkv3.shape[1]
  tq = _tile(s, tq)

  def body(q_ref, kv_ref, o_ref):
    head0 = _mem_heads(tq)
    for lg in range(MEM_WIDTH // LANES):
      cs = slice(lg * LANES, (lg + 1) * LANES)
      q2 = q_ref[:, cs]
      mk = _bf(kv_ref[:, cs])
      mv = _bf(kv_ref[:, MEM_WIDTH + lg * LANES:MEM_WIDTH + (lg + 1) * LANES])
      outs = []
      for j in range(2):
        hm = head0 if j == 0 else jnp.logical_not(head0)
        qj = _bf(jnp.where(hm, q2, 0.0) * ATT_SCALE)
        sc = _dot_nt(qj, mk)
        mx = jnp.max(sc, axis=1, keepdims=True)
        pe = jnp.exp(sc - mx)
        den = jnp.sum(pe, axis=1, keepdims=True)
        outs.append(_dot(_bf(pe / den), mv))
      o_ref[:, cs] = _bf(jnp.where(head0, outs[0], outs[1]))

  return pl.pallas_call(
      body, name=name, grid=(bl, s // tq),
      in_specs=[pl.BlockSpec((None, tq, MEM_WIDTH), lambda b, i: (b, i, qcol)),
                pl.BlockSpec((None, nm, 2 * MEM_WIDTH), lambda b, i: (b, 0, 0))],
      out_specs=pl.BlockSpec((None, tq, MEM_WIDTH), lambda b, i: (b, i, 0)),
      out_shape=jax.ShapeDtypeStruct((bl, s, MEM_WIDTH), BF16),
      compiler_params=_params(("parallel", "parallel")),
  )(h3, mkv3)


def _mem_bwd(h3, qcol, mkv3, dcat3, name, tq=512):
  bl, s, _ = h3.shape
  nm = mkv3.shape[1]
  tq = _tile(s, tq)
  docol = dcat3.shape[2] // MEM_WIDTH - 1

  def body(q_ref, kv_ref, do_ref, dq_ref, dkv_ref):
    i = pl.program_id(1)

    @pl.when(i == 0)
    def _():
      dkv_ref[...] = jnp.zeros_like(dkv_ref)

    head0 = _mem_heads(tq)
    for lg in range(MEM_WIDTH // LANES):
      cs = slice(lg * LANES, (lg + 1) * LANES)
      vs = slice(MEM_WIDTH + lg * LANES, MEM_WIDTH + (lg + 1) * LANES)
      q2 = q_ref[:, cs]
      do2 = do_ref[:, cs]
      mk = _bf(kv_ref[:, cs])
      mv = _bf(kv_ref[:, vs])
      dq2 = jnp.zeros((tq, LANES), F32)
      dmk = jnp.zeros((nm, LANES), F32)
      dmv = jnp.zeros((nm, LANES), F32)
      for j in range(2):
        hm = head0 if j == 0 else jnp.logical_not(head0)
        qj = _bf(jnp.where(hm, q2, 0.0) * ATT_SCALE)
        doj = _bf(jnp.where(hm, do2, 0.0))
        sc = _dot_nt(qj, mk)
        mx = jnp.max(sc, axis=1, keepdims=True)
        pe = jnp.exp(sc - mx)
        pn = pe / jnp.sum(pe, axis=1, keepdims=True)
        pb = _bf(pn)
        dp = _dot_nt(doj, mv)
        dj = jnp.sum(pb.astype(F32) * dp, axis=1, keepdims=True)
        ds = _bf(pn * (dp - dj))
        dq2 = dq2 + jnp.where(hm, _dot(ds, mk), 0.0) * ATT_SCALE
        dmk = dmk + _dot_tn(ds, qj)
        dmv = dmv + _dot_tn(pb, doj)
      dq_ref[:, cs] = _bf(dq2)
      dkv_ref[:, cs] += dmk
      dkv_ref[:, vs] += dmv

  return pl.pallas_call(
      body, name=name, grid=(bl, s // tq),
      in_specs=[pl.BlockSpec((None, tq, MEM_WIDTH), lambda b, i: (b, i, qcol)),
                pl.BlockSpec((None, nm, 2 * MEM_WIDTH), lambda b, i: (b, 0, 0)),
                pl.BlockSpec((None, tq, MEM_WIDTH), lambda b, i: (b, i, docol))],
      out_specs=[pl.BlockSpec((None, tq, MEM_WIDTH), lambda b, i: (b, i, 0)),
                 pl.BlockSpec((None, nm, 2 * MEM_WIDTH), lambda b, i: (b, 0, 0))],
      out_shape=[jax.ShapeDtypeStruct((bl, s, MEM_WIDTH), BF16), jax.ShapeDtypeStruct((bl, nm, 2 * MEM_WIDTH), F32)],
      compiler_params=_params(("parallel", "arbitrary")),
  )(h3, mkv3, dcat3)


def _sgu_consts():
  ti = lax.broadcasted_iota(jnp.int32, (CHUNK, CHUNK), 0)
  si = lax.broadcasted_iota(jnp.int32, (CHUNK, CHUNK), 1)
  return si <= ti, si < 64


def _sgu_bias_lanes(b_s):
  return jnp.repeat(b_s.T, 64, axis=1)


def _sgu_fwd(h2, ln_g, ln_b, w_s, b_s, name, tr=512):
  t, _ = h2.shape
  tr = _tile(t, tr)
  nch = tr // CHUNK
  npair = N_SGU_GROUPS // 2

  def body(u_ref, v_ref, g_ref, b_ref, w_ref, bs_ref, o_ref, vn_sc):
    tril, head0 = _sgu_consts()
    xhat, _ = _ln_stats(_gelu(v_ref[...]))
    vn_sc[...] = _bf(xhat * g_ref[...] + b_ref[...])
    for jp in range(npair):
      cs = slice(jp * LANES, (jp + 1) * LANES)
      w0 = _bf(jnp.where(tril, w_ref[2 * jp], 0.0))
      w1 = _bf(jnp.where(tril, w_ref[2 * jp + 1], 0.0))
      bias = bs_ref[:, cs]
      for c in range(nch):
        rs = slice(c * CHUNK, (c + 1) * CHUNK)
        vb = vn_sc[rs, cs]
        mixed = jnp.where(head0, _dot(w0, vb), _dot(w1, vb)) + bias
        o_ref[rs, cs] = _bf(_gelu(u_ref[rs, cs]) * mixed)

  blk = lambda j: pl.BlockSpec((tr, SGU_WIDTH), lambda i: (i, j))
  vec = pl.BlockSpec((1, SGU_WIDTH), lambda i: (0, 0))
  return pl.pallas_call(
      body, name=name, grid=(t // tr,),
      in_specs=[blk(0), blk(1), vec, vec,
                pl.BlockSpec((N_SGU_GROUPS, CHUNK, CHUNK), lambda i: (0, 0, 0)),
                pl.BlockSpec((CHUNK, SGU_WIDTH), lambda i: (0, 0))],
      out_specs=blk(0), out_shape=jax.ShapeDtypeStruct((t, SGU_WIDTH), BF16),
      scratch_shapes=[pltpu.VMEM((tr, SGU_WIDTH), BF16)],
      compiler_params=_params(("parallel",)),
  )(h2, h2, ln_g.reshape(1, -1), ln_b.reshape(1, -1), w_s, _sgu_bias_lanes(b_s))


def _sgu_bwd(h2, dcat, ln_g, ln_b, w_s, b_s, name, tr=512):
  t, _ = h2.shape
  tr = _tile(t, tr)
  nch = tr // CHUNK
  npair = N_SGU_GROUPS // 2
  nsteps = t // tr

  def body(u_ref, v_ref, dm_ref, g_ref, b_ref, w_ref, bs_ref,
           du_ref, dv_ref, dw_ref, dbs_ref, dg_ref, db_ref, vn_sc, dmx_sc, dvn_sc, mix_sc, dbx_sc):
    i = pl.program_id(0)
    tril, head0 = _sgu_consts()

    @pl.when(i == 0)
    def _():
      dw_ref[...] = jnp.zeros_like(dw_ref)
      dg_ref[...] = jnp.zeros_like(dg_ref)
      db_ref[...] = jnp.zeros_like(db_ref)
      dbx_sc[...] = jnp.zeros_like(dbx_sc)

    gv, gv_der = _gelu_parts(v_ref[...])
    xhat, rstd = _ln_stats(gv)
    g = g_ref[...]
    vn_sc[...] = _bf(xhat * g + b_ref[...])
    gu, gu_der = _gelu_parts(u_ref[...])
    dmix = dm_ref[...]
    dmx_sc[...] = dmix * gu

    for jp in range(npair):
      cs = slice(jp * LANES, (jp + 1) * LANES)
      w0 = _bf(jnp.where(tril, w_ref[2 * jp], 0.0))
      w1 = _bf(jnp.where(tril, w_ref[2 * jp + 1], 0.0))
      bias = bs_ref[:, cs]
      dw0 = jnp.zeros((CHUNK, CHUNK), F32)
      dw1 = jnp.zeros((CHUNK, CHUNK), F32)
      dbx = jnp.zeros((CHUNK, LANES), F32)
      for c in range(nch):
        rs = slice(c * CHUNK, (c + 1) * CHUNK)
        vb = vn_sc[rs, cs]
        mix_sc[rs, cs] = jnp.where(head0, _dot(w0, vb), _dot(w1, vb)) + bias
        dmx = dmx_sc[rs, cs]
        d0 = _bf(jnp.where(head0, dmx, 0.0))
        d1 = _bf(jnp.where(head0, 0.0, dmx))
        dvn_sc[rs, cs] = _dot_tn(w0, d0) + _dot_tn(w1, d1)
        dw0 = dw0 + _dot_nt(d0, vb)
        dw1 = dw1 + _dot_nt(d1, vb)
        dbx = dbx + dmx
      dw_ref[2 * jp] += dw0
      dw_ref[2 * jp + 1] += dw1
      dbx_sc[:, cs] += dbx

    du_ref[...] = _bf(dmix * mix_sc[...] * gu_der)
    dvn = dvn_sc[...]
    dv_ref[...] = _bf(_ln_bwd(dvn, xhat, rstd, g) * gv_der)
    dg_ref[...] += jnp.sum(dvn * xhat, axis=0, keepdims=True)
    db_ref[...] += jnp.sum(dvn, axis=0, keepdims=True)

    @pl.when(i == nsteps - 1)
    def _():
      lane = lax.broadcasted_iota(jnp.int32, (CHUNK, LANES), 1)
      acc = jnp.zeros((CHUNK, LANES), F32)
      for gi in range(N_SGU_GROUPS):
        jp, j = gi // 2, gi % 2
        part = dbx_sc[:, jp * LANES:(jp + 1) * LANES]
        hm = (lane < 64) if j == 0 else (lane >= 64)
        colsum = jnp.sum(jnp.where(hm, part, 0.0), axis=1, keepdims=True)
        acc = jnp.where(lane == gi, colsum, acc)
        dw_ref[gi] = jnp.where(tril, dw_ref[gi], 0.0)
      dbs_ref[...] = acc

  blk = lambda j: pl.BlockSpec((tr, SGU_WIDTH), lambda i: (i, j))
  vec = pl.BlockSpec((1, SGU_WIDTH), lambda i: (0, 0))
  wspec = pl.BlockSpec((N_SGU_GROUPS, CHUNK, CHUNK), lambda i: (0, 0, 0))
  big = lambda dt: pltpu.VMEM((tr, SGU_WIDTH), dt)
  du, dv, dw, dbs, dg, db = pl.pallas_call(
      body, name=name, grid=(nsteps,),
      in_specs=[blk(0), blk(1), blk(0), vec, vec, wspec, pl.BlockSpec((CHUNK, SGU_WIDTH), lambda i: (0, 0))],
      out_specs=[blk(0), blk(0), wspec, pl.BlockSpec((CHUNK, LANES), lambda i: (0, 0)), vec, vec],
      out_shape=[jax.ShapeDtypeStruct((t, SGU_WIDTH), BF16), jax.ShapeDtypeStruct((t, SGU_WIDTH), BF16),
                 jax.ShapeDtypeStruct((N_SGU_GROUPS, CHUNK, CHUNK), F32), jax.ShapeDtypeStruct((CHUNK, LANES), F32),
                 jax.ShapeDtypeStruct((1, SGU_WIDTH), F32), jax.ShapeDtypeStruct((1, SGU_WIDTH), F32)],
      scratch_shapes=[big(BF16), big(F32), big(F32), big(F32), pltpu.VMEM((CHUNK, SGU_WIDTH), F32)],
      compiler_params=_params(("arbitrary",)),
  )(h2, h2, dcat, ln_g.reshape(1, -1), ln_b.reshape(1, -1), w_s, _sgu_bias_lanes(b_s))
  return du, dv, dw, dbs[:, :N_SGU_GROUPS].T, dg[0], db[0]


def _loss_head(xo, tgt, name, tm=512):
  m, d = xo.shape
  tm = _tile(m, tm)

  def body(x_ref, t_ref, dx_ref, l_ref):
    @pl.when(pl.program_id(0) == 0)
    def _():
      l_ref[...] = jnp.zeros_like(l_ref)

    diff = x_ref[...] - t_ref[...]
    dx_ref[...] = diff * (1.0 / d)
    rowsum = jnp.sum(diff * diff, axis=1, keepdims=True)
    tot = jnp.sum(rowsum, axis=0, keepdims=True) * (0.5 / d)
    l_ref[...] += jnp.broadcast_to(tot, l_ref.shape)

  row = pl.BlockSpec((tm, d), lambda i: (i, 0))
  dx, l = pl.pallas_call(
      body, name=name, grid=(m // tm,), in_specs=[row, row],
      out_specs=[row, pl.BlockSpec((8, LANES), lambda i: (0, 0))],
      out_shape=[jax.ShapeDtypeStruct((m, d), F32), jax.ShapeDtypeStruct((8, LANES), F32)],
      compiler_params=_params(("arbitrary",)),
  )(xo, tgt)
  return l[0, 0], dx


def _local_step(x3, mem3, tgt3, w):
  bl, s, d = x3.shape
  t = bl * s
  nm = mem3.shape[1]
  mem2 = mem3.reshape(bl * nm, d)
  x = x3.reshape(t, d)
  xb = x
  saved = []
  for i in range(DEPTH):
    j = i // 2
    attn = i % 2 == 0
    mkv = _mm(mem2, w["w_mem_kv"][i], "nn", F32, f"mkv_fwd_{i}", tm=1024, tn=512, tk=1024)
    mkv3 = mkv.reshape(bl, nm, 2 * MEM_WIDTH)
    w_in = w["a_w_in"][j] if attn else w["b_w_in"][j]
    h = _mm(xb, w_in, "nn", F32, f"in_proj_{i}", tm=1024, tn=1280 if attn else 896, tk=1024)
    h3 = h.reshape(bl, s, -1)
    if attn:
      mix3, lse3 = _attn_fwd(h3, f"dil_attn_fwd_{i}")
      mix = mix3.reshape(t, DIL_WIDTH)
      qcol = 3 * DIL_WIDTH // MEM_WIDTH
    else:
      mix = _sgu_fwd(h, w["sgu_ln_g"][j], w["sgu_ln_b"][j], w["sgu_w_s"][j], w["sgu_b_s"][j], f"sgu_fwd_{i}")
      lse3 = None
      qcol = 2 * SGU_WIDTH // MEM_WIDTH
    mo = _mem_fwd(h3, qcol, mkv3, f"mem_attn_fwd_{i}").reshape(t, MEM_WIDTH)
    cat = jnp.concatenate([mix, mo], axis=1)
    z1, xm, xmb = _mm_res_ln(cat, w["w_out"][i], x, w["ln_mix_g"][i], w["ln_mix_b"][i], f"out_proj_ln_{i}", tk=1024)
    a, b, hm = _ffn_up(xmb, w["w_gate"][i], w["w_up"][i], f"ffn_up_{i}")
    z2, xo, xob = _mm_res_ln(hm, w["w_down"][i], xm, w["ln_ffn_g"][i], w["ln_ffn_b"][i], f"ffn_down_ln_{i}", tk=1408)
    saved.append(dict(xb=xb, h=h, h3=h3, mkv3=mkv3, mix3=(mix3 if attn else None), lse3=lse3, cat=cat, z1=z1,
                      xmb=xmb, a=a, b=b, hm=hm, z2=z2, qcol=qcol))
    x, xb = xo, xob

  loss, dx = _loss_head(x, tgt3.reshape(t, d), "loss_head")

  names = ("a_w_in", "b_w_in", "sgu_ln_g", "sgu_ln_b", "sgu_w_s", "sgu_b_s", "w_mem_kv", "w_out",
           "ln_mix_g", "ln_mix_b", "w_gate", "w_up", "w_down", "ln_ffn_g", "ln_ffn_b")
  grads = {n: [None] * w[n].shape[0] for n in names}
  for i in reversed(range(DEPTH)):
    j = i // 2
    attn = i % 2 == 0
    sv = saved[i]
    dz2, dz2b, grads["ln_ffn_g"][i], grads["ln_ffn_b"][i] = _ln_bwd_call(dx, sv["z2"], w["ln_ffn_g"][i], f"ln_ffn_bwd_{i}")
    da, db = _ffn_bwd_hidden(dz2b, w["w_down"][i], sv["a"], sv["b"], f"ffn_bwd_hidden_{i}")
    grads["w_down"][i] = _mm(sv["hm"], dz2b, "tn", F32, f"dw_down_{i}", tm=1408, tn=1024, tk=1024)
    grads["w_gate"][i] = _mm(sv["xmb"], da, "tn", F32, f"dw_gate_{i}", tm=1024, tn=1408, tk=1024)
    grads["w_up"][i] = _mm(sv["xmb"], db, "tn", F32, f"dw_up_{i}", tm=1024, tn=1408, tk=1024)
    dxm = _ffn_bwd_input(da, db, w["w_gate"][i], w["w_up"][i], dz2, f"ffn_bwd_input_{i}")
    dz1, dz1b, grads["ln_mix_g"][i], grads["ln_mix_b"][i] = _ln_bwd_call(dxm, sv["z1"], w["ln_mix_g"][i], f"ln_mix_bwd_{i}")
    grads["w_out"][i] = _mm(sv["cat"], dz1b, "tn", F32, f"dw_out_{i}", tm=1024, tn=1024, tk=1024)
    dcat = _mm(dz1b, w["w_out"][i], "nt", F32, f"out_proj_bwd_{i}", tm=1024, tn=1024, tk=1024)
    dcat3 = dcat.reshape(bl, s, -1)
    dqm3, dmkv3 = _mem_bwd(sv["h3"], sv["qcol"], sv["mkv3"], dcat3, f"mem_attn_bwd_{i}")
    grads["w_mem_kv"][i] = _mm(mem2, dmkv3.reshape(bl * nm, 2 * MEM_WIDTH), "tn", F32, f"dw_mem_kv_{i}", tm=1024, tn=512, tk=1024)
    dqm = dqm3.reshape(t, MEM_WIDTH)
    if attn:
      dq3, dk3, dv3 = _attn_bwd(sv["h3"], sv["mix3"], sv["lse3"], dcat3, f"dil_attn_bwd_{i}")
      parts = [dq3.reshape(t, -1), dk3.reshape(t, -1), dv3.reshape(t, -1), dqm]
    else:
      du, dv, dws, dbs, dlg, dlb = _sgu_bwd(sv["h"], dcat, w["sgu_ln_g"][j], w["sgu_ln_b"][j], w["sgu_w_s"][j],
                                             w["sgu_b_s"][j], f"sgu_bwd_{i}")
      grads["sgu_w_s"][j], grads["sgu_b_s"][j], grads["sgu_ln_g"][j], grads["sgu_ln_b"][j] = dws, dbs, dlg, dlb
      parts = [du, dv, dqm]
    dh = jnp.concatenate(parts, axis=1)
    w_in = w["a_w_in"][j] if attn else w["b_w_in"][j]
    grads["a_w_in" if attn else "b_w_in"][j] = _mm(sv["xb"], dh, "tn", F32, f"dw_in_{i}", tm=1024, tn=1280 if attn else 896, tk=1024)
    dx = _mm(dh, w_in, "nt", F32, f"in_proj_bwd_{i}", add=dz1, add_scale=DN_ALPHA, tm=1024, tn=1024, tk=1280 if attn else 896)
  return loss, dx.reshape(bl, s, d), grads


def _my_place():
  return lax.axis_index("x"), lax.axis_index("y"), lax.axis_index("c")


def _other_chips(x, y):
  return [(1 - x, y), (x, 1 - y), (1 - x, 1 - y)]


ANY = pl.BlockSpec(memory_space=pl.ANY)


def _all_gather_halves(wl, name):
  _, r, c_ = wl.shape

  def body(w_ref, g_ref, send_sems, recv_sems):
    x, y, c = _my_place()
    me = 2 * x + y
    sibling = (x, y, 1 - c)
    chips = _other_chips(x, y)

    def copy(k, src, dst, to):
      return pltpu.make_async_remote_copy(src_ref=src, dst_ref=dst, send_sem=send_sems.at[k], recv_sem=recv_sems.at[k],
                                          device_id=to, device_id_type=MESH_ID)

    first = [copy(k, w_ref.at[c], g_ref.at[me, c], (px, py, c)) for k, (px, py) in enumerate(chips)]
    for cp in first:
      cp.start()
    passed = []
    for k, (px, py) in enumerate(chips):
      landed = g_ref.at[2 * px + py, c]
      copy(k, landed, landed, (px, py, c)).wait_recv()
      fwd = copy(3 + k, landed, landed, sibling)
      fwd.start()
      passed.append(fwd)
    for k, (px, py) in enumerate(chips):
      theirs = g_ref.at[2 * px + py, 1 - c]
      copy(3 + k, theirs, theirs, sibling).wait_recv()
    for cp in first + passed:
      cp.wait_send()

  got = pl.pallas_call(
      body, name=name, in_specs=[ANY], out_specs=ANY,
      out_shape=jax.ShapeDtypeStruct((4, 2, r, c_), wl.dtype),
      scratch_shapes=[pltpu.SemaphoreType.DMA((6,)), pltpu.SemaphoreType.DMA((6,))],
  )(wl)
  chip = 2 * lax.axis_index("x") + lax.axis_index("y")
  return lax.dynamic_update_slice(got, wl[None], (chip, 0, 0, 0))


def _sibling_swap(v, name):
  def body(v_ref, o_ref, send_sem, recv_sem):
    x, y, c = _my_place()
    cp = pltpu.make_async_remote_copy(src_ref=v_ref, dst_ref=o_ref, send_sem=send_sem, recv_sem=recv_sem,
                                      device_id=(x, y, 1 - c), device_id_type=MESH_ID)
    cp.start()
    cp.wait()

  return pl.pallas_call(
      body, name=name, in_specs=[ANY], out_specs=ANY, out_shape=jax.ShapeDtypeStruct(v.shape, v.dtype),
      scratch_shapes=[pltpu.SemaphoreType.DMA, pltpu.SemaphoreType.DMA],
  )(v)


def _chip_exchange(q, name):
  _, r, c_ = q.shape

  def body(q_ref, o_ref, send_sems, recv_sems):
    x, y, c = _my_place()
    cps = []
    for k, (px, py) in enumerate(_other_chips(x, y)):
      cp = pltpu.make_async_remote_copy(src_ref=q_ref.at[2 * px + py], dst_ref=o_ref.at[k], send_sem=send_sems.at[k],
                                        recv_sem=recv_sems.at[k], device_id=(px, py, c), device_id_type=MESH_ID)
      cp.start()
      cps.append(cp)
    for cp in cps:
      cp.wait()

  return pl.pallas_call(
      body, name=name, in_specs=[ANY], out_specs=ANY, out_shape=jax.ShapeDtypeStruct((3, r, c_), q.dtype),
      scratch_shapes=[pltpu.SemaphoreType.DMA((3,)), pltpu.SemaphoreType.DMA((3,))],
  )(q)


def _share_halves(v, name):
  theirs = _sibling_swap(v, name)
  c = lax.axis_index("c")
  return jnp.where(c == 0, jnp.concatenate([v, theirs]), jnp.concatenate([theirs, v]))


def _half_spec(tr, c_, pick):
  return pl.BlockSpec((None, None, tr, c_), lambda s, r, place: (s, pick(place), r, 0))


def _cast_other_half(p, place, name, tr=512):
  _, _, r, c_ = p.shape
  tr = _tile(r, tr, 16)

  def body(place_ref, p_ref, o_ref):
    o_ref[...] = _bf(p_ref[...])

  out_spec = pl.BlockSpec((None, tr, c_), lambda s, rr, place: (s, rr, 0))
  return pl.pallas_call(
      body, name=name, out_shape=jax.ShapeDtypeStruct((4, r, c_), BF16),
      grid_spec=pltpu.PrefetchScalarGridSpec(num_scalar_prefetch=1, grid=(4, r // tr),
                                             in_specs=[_half_spec(tr, c_, lambda place: 1 - place[1])], out_specs=out_spec),
      compiler_params=_params(("parallel", "parallel")),
  )(place, p)


def _add_sibling(p, x1, place, name, tr=512):
  _, _, r, c_ = p.shape
  tr = _tile(r, tr, 16)

  def body(place_ref, p_ref, x_ref, o_ref):
    o_ref[...] = _bf(p_ref[...] + x_ref[...].astype(F32))

  row = pl.BlockSpec((None, tr, c_), lambda s, rr, place: (s, rr, 0))
  return pl.pallas_call(
      body, name=name, out_shape=jax.ShapeDtypeStruct((4, r, c_), BF16),
      grid_spec=pltpu.PrefetchScalarGridSpec(num_scalar_prefetch=1, grid=(4, r // tr),
                                             in_specs=[_half_spec(tr, c_, lambda place: place[1]), row], out_specs=row),
      compiler_params=_params(("parallel", "parallel")),
  )(place, p, x1)


def _sum_own(p, x1, x3, place, name, tr=512):
  _, _, r, c_ = p.shape
  tr = _tile(r, tr, 16)

  def body(place_ref, p_ref, x1_ref, x3_ref, o_ref):
    acc = p_ref[...] + x1_ref[...].astype(F32)
    for k in range(3):
      acc = acc + x3_ref[k].astype(F32)
    o_ref[...] = acc

  return pl.pallas_call(
      body, name=name, out_shape=jax.ShapeDtypeStruct((r, c_), F32),
      grid_spec=pltpu.PrefetchScalarGridSpec(
          num_scalar_prefetch=1, grid=(r // tr,),
          in_specs=[pl.BlockSpec((None, None, tr, c_), lambda rr, place: (place[0], place[1], rr, 0)),
                    pl.BlockSpec((None, tr, c_), lambda rr, place: (place[0], rr, 0)),
                    pl.BlockSpec((3, tr, c_), lambda rr, place: (0, rr, 0))],
          out_specs=pl.BlockSpec((tr, c_), lambda rr, place: (rr, 0))),
      compiler_params=_params(("parallel",)),
  )(place, p, x1, x3)


def _reduce_scatter(p):
  x, y, c = _my_place()
  place = jnp.stack([2 * x + y, c]).astype(jnp.int32)
  x1 = _sibling_swap(_cast_other_half(p, place, "rs_cast_other_half"), "rs_sibling_swap")
  q = _add_sibling(p, x1, place, "rs_add_sibling")
  x3 = _chip_exchange(q, "rs_chip_exchange")
  mine = _sum_own(p, x1, x3, place, "rs_sum_own")
  return _share_halves(mine, "rs_share_halves")


def _adamw(w, g, m, v, name):
  shape = w.shape
  cols = shape[-1]
  rows = w.size // cols
  tr = _tile(rows, max(8, (256 * 1024) // cols // 8 * 8), 8)

  def body(w_ref, g_ref, m_ref, v_ref, d_ref, nm_ref, nv_ref):
    gv = g_ref[...]
    nm = ADAM_B1 * m_ref[...] + (1.0 - ADAM_B1) * gv
    nv = ADAM_B2 * v_ref[...] + (1.0 - ADAM_B2) * (gv * gv)
    m_hat = nm / (1.0 - ADAM_B1 ** ADAM_STEP)
    v_hat = nv / (1.0 - ADAM_B2 ** ADAM_STEP)
    d_ref[...] = -ADAM_LR * (m_hat / (jnp.sqrt(v_hat) + ADAM_EPS) + ADAM_WD * w_ref[...])
    nm_ref[...] = nm
    nv_ref[...] = nv

  spec = pl.BlockSpec((tr, cols), lambda i: (i, 0))
  sds = jax.ShapeDtypeStruct((rows, cols), F32)
  outs = pl.pallas_call(
      body, name=name, grid=(rows // tr,), in_specs=[spec] * 4, out_specs=[spec] * 3, out_shape=[sds] * 3,
      compiler_params=_params(("parallel",)),
  )(*(t.reshape(rows, cols) for t in (w, g, m, v)))
  return tuple(o.reshape(shape) for o in outs)


SHARDED = (("a_w_in", 2), ("b_w_in", 2), ("w_mem_kv", 1), ("w_out", 1), ("w_gate", 2), ("w_up", 2), ("w_down", 1))
SMALL_SHARDED = (("sgu_ln_g", 1), ("sgu_ln_b", 1))
REPLICATED = ("sgu_w_s", "sgu_b_s", "ln_mix_g", "ln_mix_b", "ln_ffn_g", "ln_ffn_b")
SMALL_ORDER = ("sgu_w_s", "sgu_b_s", "ln_mix_g", "ln_mix_b", "ln_ffn_g", "ln_ffn_b", "sgu_ln_g", "sgu_ln_b")
ROW_ALIGN = 16


def _pad_to(v, n):
  return jnp.pad(v, (0, n - v.shape[0]))


def _round_up(n, a):
  return -(-n // a) * a


def _to_shard_major(full, axis):
  shp = full.shape
  cut = shp[:axis] + (4, shp[axis] // 4) + shp[axis + 1:]
  return jnp.moveaxis(full.reshape(cut), axis, 0).reshape(4, -1, FLAT_COLS)


def _from_shard_major(rows, shard_shape, axis):
  full = jnp.moveaxis(rows.reshape((4,) + tuple(shard_shape)), 0, axis)
  shp = full.shape
  return full.reshape(shp[:axis] + (shp[axis] * shp[axis + 1],) + shp[axis + 2:])


def _gather_weights(shards):
  segs = [shards[n].astype(BF16).reshape(-1, FLAT_COLS) for n, _ in SHARDED]
  small = jnp.concatenate([lax.bitcast_convert_type(shards[n], BF16).reshape(-1) for n, _ in SMALL_SHARDED])
  small_rows = _round_up(small.shape[0], ROW_ALIGN * FLAT_COLS) // FLAT_COLS
  segs.append(_pad_to(small, small_rows * FLAT_COLS).reshape(small_rows, FLAT_COLS))
  rows = sum(sg.shape[0] for sg in segs)
  rows_pad = _round_up(rows, 2 * ROW_ALIGN)
  if rows_pad > rows:
    segs.append(jnp.zeros((rows_pad - rows, FLAT_COLS), BF16))
  flat = jnp.concatenate(segs).reshape(2, rows_pad // 2, FLAT_COLS)
  g = _all_gather_halves(flat, "gather_weights").reshape(4, rows_pad, FLAT_COLS)
  out, off = {}, 0
  for n, axis in SHARDED:
    nr = shards[n].size // FLAT_COLS
    out[n] = _from_shard_major(g[:, off:off + nr], shards[n].shape, axis)
    off += nr
  small_g = g[:, off:off + small_rows].reshape(4, small_rows * FLAT_COLS)
  off = 0
  for n, axis in SMALL_SHARDED:
    sz = 2 * shards[n].size
    vals = lax.bitcast_convert_type(small_g[:, off:off + sz].reshape((4,) + shards[n].shape + (2,)), F32)
    out[n] = _from_shard_major(vals, shards[n].shape, axis)
    off += sz
  return out


def _reduce_grads(grads, shard_shapes):
  segs = []
  for n, axis in SHARDED:
    for g in grads[n]:
      segs.append(_to_shard_major(g, axis - 1))
  big_rows = sum(sg.shape[1] for sg in segs)
  small_full = {n: jnp.stack(grads[n]) for n in SMALL_ORDER}
  small = jnp.concatenate([small_full[n].reshape(-1) for n in SMALL_ORDER])
  n_small = _round_up(small.shape[0], 4 * 2 * 8 * FLAT_COLS)
  quarter_rows = n_small // (4 * FLAT_COLS)
  segs.append(_pad_to(small, n_small).reshape(4, quarter_rows, FLAT_COLS))
  rows_pad = _round_up(big_rows + quarter_rows, 2 * ROW_ALIGN)
  if rows_pad > big_rows + quarter_rows:
    segs.append(jnp.zeros((4, rows_pad - big_rows - quarter_rows, FLAT_COLS), F32))
  p = jnp.concatenate(segs, axis=1).reshape(4, 2, rows_pad // 2, FLAT_COLS)
  mine = _reduce_scatter(p)
  out, off = {}, 0
  for n, _ in SHARDED:
    nr = math.prod(shard_shapes[n]) // FLAT_COLS
    out[n] = mine[off:off + nr].reshape(shard_shapes[n])
    off += nr
  piece = mine[big_rows:big_rows + quarter_rows].reshape(2, quarter_rows // 2, FLAT_COLS)
  small_sum = _all_gather_halves(piece, "gather_small_grads").reshape(n_small)
  off = 0
  for n in SMALL_ORDER:
    sz = small_full[n].size
    out[n] = small_sum[off:off + sz].reshape(small_full[n].shape)
    off += sz
  return out


WEIGHT_NAMES = ("a_w_in", "b_w_in", "sgu_ln_g", "sgu_ln_b", "sgu_w_s", "sgu_b_s", "w_mem_kv", "w_out",
                "ln_mix_g", "ln_mix_b", "w_gate", "w_up", "w_down", "ln_ffn_g", "ln_ffn_b")


def kernel(x, mem, a_w_in, b_w_in, sgu_ln_g, sgu_ln_b, sgu_w_s, sgu_b_s, w_mem_kv, w_out, ln_mix_g, ln_mix_b, w_gate, w_up, w_down, ln_ffn_g, ln_ffn_b, loss_target, m_a_w_in, m_b_w_in, m_sgu_ln_g, m_sgu_ln_b, m_sgu_w_s, m_sgu_b_s, m_w_mem_kv, m_w_out, m_ln_mix_g, m_ln_mix_b, m_w_gate, m_w_up, m_w_down, m_ln_ffn_g, m_ln_ffn_b, v_a_w_in, v_b_w_in, v_sgu_ln_g, v_sgu_ln_b, v_sgu_w_s, v_sgu_b_s, v_w_mem_kv, v_w_out, v_ln_mix_g, v_ln_mix_b, v_w_gate, v_w_up, v_w_down, v_ln_ffn_g, v_ln_ffn_b):
  weights = dict(a_w_in=a_w_in, b_w_in=b_w_in, sgu_ln_g=sgu_ln_g, sgu_ln_b=sgu_ln_b, sgu_w_s=sgu_w_s, sgu_b_s=sgu_b_s,
                 w_mem_kv=w_mem_kv, w_out=w_out, ln_mix_g=ln_mix_g, ln_mix_b=ln_mix_b, w_gate=w_gate, w_up=w_up,
                 w_down=w_down, ln_ffn_g=ln_ffn_g, ln_ffn_b=ln_ffn_b)
  mom1 = dict(a_w_in=m_a_w_in, b_w_in=m_b_w_in, sgu_ln_g=m_sgu_ln_g, sgu_ln_b=m_sgu_ln_b, sgu_w_s=m_sgu_w_s,
              sgu_b_s=m_sgu_b_s, w_mem_kv=m_w_mem_kv, w_out=m_w_out, ln_mix_g=m_ln_mix_g, ln_mix_b=m_ln_mix_b,
              w_gate=m_w_gate, w_up=m_w_up, w_down=m_w_down, ln_ffn_g=m_ln_ffn_g, ln_ffn_b=m_ln_ffn_b)
  mom2 = dict(a_w_in=v_a_w_in, b_w_in=v_b_w_in, sgu_ln_g=v_sgu_ln_g, sgu_ln_b=v_sgu_ln_b, sgu_w_s=v_sgu_w_s,
              sgu_b_s=v_sgu_b_s, w_mem_kv=v_w_mem_kv, w_out=v_w_out, ln_mix_g=v_ln_mix_g, ln_mix_b=v_ln_mix_b,
              w_gate=v_w_gate, w_up=v_w_up, w_down=v_w_down, ln_ffn_g=v_ln_ffn_g, ln_ffn_b=v_ln_ffn_b)

  full = _gather_weights(weights)
  for n in REPLICATED:
    full[n] = weights[n]
  loss_part, grad_x, grads = _local_step(x, mem, loss_target, full)
  loss = lax.psum(loss_part, MESH_AXES)

  shard_shapes = {n: weights[n].shape for n, _ in SHARDED}
  red = _reduce_grads(grads, shard_shapes)
  chip = 2 * lax.axis_index("x") + lax.axis_index("y")
  for n, axis in SMALL_SHARDED:
    width = weights[n].shape[axis]
    red[n] = lax.dynamic_slice_in_dim(red[n], chip * width, width, axis)

  small_names = SMALL_ORDER
  def pack(d):
    flat = jnp.concatenate([d[n].reshape(-1) for n in small_names])
    return _pad_to(flat, _round_up(flat.shape[0], 8 * FLAT_COLS)).reshape(-1, FLAT_COLS)
  small_out = _adamw(pack(weights), pack(red), pack(mom1), pack(mom2), "adamw_small")
  delta, new_m, new_v = {}, {}, {}
  off = 0
  for n in small_names:
    sz = weights[n].size
    for dst, src in zip((delta, new_m, new_v), small_out):
      dst[n] = src.reshape(-1)[off:off + sz].reshape(weights[n].shape)
    off += sz
  for n, _ in SHARDED:
    delta[n], new_m[n], new_v[n] = _adamw(weights[n], red[n], mom1[n], mom2[n], f"adamw_{n}")

  return (loss, grad_x, *[red[n] for n in WEIGHT_NAMES], *[delta[n] for n in WEIGHT_NAMES],
          *[new_m[n] for n in WEIGHT_NAMES], *[new_v[n] for n in WEIGHT_NAMES])
```

```python
import functools
import math

import jax
import jax.numpy as jnp
from jax import lax
from jax.experimental import pallas as pl
from jax.experimental.pallas import tpu as pltpu

F32 = jnp.float32
BF16 = jnp.bfloat16

DEPTH = 4
HEAD_DIM = 64
N_DIL_HEADS = 12
DIL_WIDTH = N_DIL_HEADS * HEAD_DIM
DIL_PATTERNS = ((128, 1), (512, 4), (2048, 16))
BLOCK = 128
N_SGU_GROUPS = 12
SGU_WIDTH = N_SGU_GROUPS * 64
CHUNK = 128
N_MEM_HEADS = 4
MEM_WIDTH = N_MEM_HEADS * HEAD_DIM
DN_ALPHA = (2 * DEPTH) ** 0.25
LN_EPS = 1e-5
ATT_SCALE = HEAD_DIM ** -0.5
ADAM_LR = 0.001
ADAM_B1 = 0.9
ADAM_B2 = 0.999
ADAM_EPS = 1e-08
ADAM_WD = 0.01
ADAM_STEP = 10
NEG_BIG = -1e30
ATTN_UNROLL = 2

LANES = 128
FLAT_COLS = 1024
VMEM_LIMIT = 56 * 1024 * 1024
MESH_AXES = ("x", "y", "c")
MESH_ID = pl.DeviceIdType.MESH


def _tile(n, pref, align=LANES):
  if n <= pref:
    return n
  t = (pref // align) * align
  while t >= align:
    if n % t == 0:
      return t
    t -= align
  return n


def _params(sem):
  return pltpu.CompilerParams(dimension_semantics=sem, vmem_limit_bytes=VMEM_LIMIT)


def _dot(a, b):
  return jnp.dot(a, b, preferred_element_type=F32)


def _dot_nt(a, b):
  return lax.dot_general(a, b, (((1,), (1,)), ((), ())), preferred_element_type=F32)


def _dot_tn(a, b):
  return lax.dot_general(a, b, (((0,), (0,)), ((), ())), preferred_element_type=F32)


def _bf(v):
  return v.astype(BF16)


def _ln_stats(z):
  mu = jnp.mean(z, axis=-1, keepdims=True)
  zc = z - mu
  var = jnp.mean(zc * zc, axis=-1, keepdims=True)
  rstd = lax.rsqrt(var + LN_EPS)
  return zc * rstd, rstd


def _ln_bwd(dy, xhat, rstd, g):
  gdy = dy * g
  m1 = jnp.mean(gdy, axis=-1, keepdims=True)
  m2 = jnp.mean(gdy * xhat, axis=-1, keepdims=True)
  return rstd * (gdy - m1 - xhat * m2)


_GELU_C = math.sqrt(2.0 / math.pi)


def _gelu_parts(v):
  v2 = v * v
  t = jnp.tanh(_GELU_C * (v + 0.044715 * v * v2))
  val = 0.5 * v * (1.0 + t)
  der = 0.5 * (1.0 + t) + 0.5 * v * (1.0 - t * t) * (_GELU_C * (1.0 + 3.0 * 0.044715 * v2))
  return val, der


def _gelu(v):
  t = jnp.tanh(_GELU_C * (v + 0.044715 * v * v * v))
  return 0.5 * v * (1.0 + t)


def _sigmoid(v):
  return 1.0 / (1.0 + jnp.exp(-v))


def _mm(a, b, mode, out_dtype, name, add=None, add_scale=1.0, tm=512, tn=512, tk=512):
  if mode == "nn":
    (m, k), (k2, n) = a.shape, b.shape
  elif mode == "nt":
    (m, k), (n, k2) = a.shape, b.shape
  else:
    (k, m), (k2, n) = a.shape, b.shape
  assert k == k2, (a.shape, b.shape, mode)
  tm, tn, tk = _tile(m, tm), _tile(n, tn), _tile(k, tk)
  nk = k // tk
  if mode == "nn":
    a_spec = pl.BlockSpec((tm, tk), lambda i, j, kk: (i, kk))
    b_spec = pl.BlockSpec((tk, tn), lambda i, j, kk: (kk, j))
    dot = _dot
  elif mode == "nt":
    a_spec = pl.BlockSpec((tm, tk), lambda i, j, kk: (i, kk))
    b_spec = pl.BlockSpec((tn, tk), lambda i, j, kk: (j, kk))
    dot = _dot_nt
  else:
    a_spec = pl.BlockSpec((tk, tm), lambda i, j, kk: (kk, i))
    b_spec = pl.BlockSpec((tk, tn), lambda i, j, kk: (kk, j))
    dot = _dot_tn
  o_spec = pl.BlockSpec((tm, tn), lambda i, j, kk: (i, j))
  has_add = add is not None

  def body(*refs):
    if has_add:
      a_ref, b_ref, add_ref, o_ref, acc_ref = refs
    else:
      a_ref, b_ref, o_ref, acc_ref = refs
    kk = pl.program_id(2)

    @pl.when(kk == 0)
    def _():
      acc_ref[...] = jnp.zeros_like(acc_ref)

    acc_ref[...] += dot(_bf(a_ref[...]), _bf(b_ref[...]))

    @pl.when(kk == nk - 1)
    def _():
      r = acc_ref[...]
      if has_add:
        r = r + add_scale * add_ref[...].astype(F32)
      o_ref[...] = r.astype(out_dtype)

  in_specs = [a_spec, b_spec] + ([o_spec] if has_add else [])
  args = (a, b) + ((add,) if has_add else ())
  return pl.pallas_call(
      body, name=name, grid=(m // tm, n // tn, nk), in_specs=in_specs, out_specs=o_spec,
      out_shape=jax.ShapeDtypeStruct((m, n), out_dtype),
      scratch_shapes=[pltpu.VMEM((tm, tn), F32)],
      compiler_params=_params(("parallel", "parallel", "arbitrary")),
  )(*args)


def _mm_res_ln(a, w, res, g, b, name, tm=512, tk=512):
  m, k = a.shape
  d = w.shape[1]
  tm, tk = _tile(m, tm), _tile(k, tk)
  nk = k // tk

  def body(a_ref, w_ref, r_ref, g_ref, b_ref, z_ref, x_ref, xb_ref, acc_ref):
    kk = pl.program_id(1)

    @pl.when(kk == 0)
    def _():
      acc_ref[...] = jnp.zeros_like(acc_ref)

    acc_ref[...] += _dot(_bf(a_ref[...]), _bf(w_ref[...]))

    @pl.when(kk == nk - 1)
    def _():
      z = DN_ALPHA * r_ref[...] + acc_ref[...]
      xhat, _ = _ln_stats(z)
      xn = xhat * g_ref[...] + b_ref[...]
      z_ref[...] = z
      x_ref[...] = xn
      xb_ref[...] = _bf(xn)

  row = pl.BlockSpec((tm, d), lambda i, kk: (i, 0))
  vec = pl.BlockSpec((1, d), lambda i, kk: (0, 0))
  return pl.pallas_call(
      body, name=name, grid=(m // tm, nk),
      in_specs=[pl.BlockSpec((tm, tk), lambda i, kk: (i, kk)), pl.BlockSpec((tk, d), lambda i, kk: (kk, 0)), row, vec, vec],
      out_specs=[row, row, row],
      out_shape=[jax.ShapeDtypeStruct((m, d), F32), jax.ShapeDtypeStruct((m, d), F32), jax.ShapeDtypeStruct((m, d), BF16)],
      scratch_shapes=[pltpu.VMEM((tm, d), F32)],
      compiler_params=_params(("parallel", "arbitrary")),
  )(a, w, res, g.reshape(1, d), b.reshape(1, d))


def _ln_bwd_call(dy, z, g, name, tm=512):
  m, d = z.shape
  tm = _tile(m, tm)
  n = m // tm

  def body(dy_ref, z_ref, g_ref, dz_ref, dzb_ref, dg_ref, db_ref):
    i = pl.program_id(0)

    @pl.when(i == 0)
    def _():
      dg_ref[...] = jnp.zeros_like(dg_ref)
      db_ref[...] = jnp.zeros_like(db_ref)

    dy_v = dy_ref[...]
    xhat, rstd = _ln_stats(z_ref[...])
    dz = _ln_bwd(dy_v, xhat, rstd, g_ref[...])
    dz_ref[...] = dz
    dzb_ref[...] = _bf(dz)
    dg_ref[...] += jnp.sum(dy_v * xhat, axis=0, keepdims=True)
    db_ref[...] += jnp.sum(dy_v, axis=0, keepdims=True)

  row = pl.BlockSpec((tm, d), lambda i: (i, 0))
  vec = pl.BlockSpec((1, d), lambda i: (0, 0))
  dz, dzb, dg, db = pl.pallas_call(
      body, name=name, grid=(n,), in_specs=[row, row, vec], out_specs=[row, row, vec, vec],
      out_shape=[jax.ShapeDtypeStruct((m, d), F32), jax.ShapeDtypeStruct((m, d), BF16),
                 jax.ShapeDtypeStruct((1, d), F32), jax.ShapeDtypeStruct((1, d), F32)],
      compiler_params=_params(("arbitrary",)),
  )(dy, z, g.reshape(1, d))
  return dz, dzb, dg[0], db[0]


def _ffn_up(xb, wg, wu, name, tm=512, tn=1408):
  m, d = xb.shape
  f = wg.shape[1]
  tm, tn = _tile(m, tm), _tile(f, tn)

  def body(x_ref, wg_ref, wu_ref, a_ref, b_ref, h_ref):
    xv = x_ref[...]
    a = _dot(xv, wg_ref[...])
    b = _dot(xv, wu_ref[...])
    a_ref[...] = _bf(a)
    b_ref[...] = _bf(b)
    h_ref[...] = _bf(a * _sigmoid(a) * b)

  wspec = pl.BlockSpec((d, tn), lambda j, i: (0, j))
  ospec = pl.BlockSpec((tm, tn), lambda j, i: (i, j))
  sds = jax.ShapeDtypeStruct((m, f), BF16)
  return pl.pallas_call(
      body, name=name, grid=(f // tn, m // tm),
      in_specs=[pl.BlockSpec((tm, d), lambda j, i: (i, 0)), wspec, wspec],
      out_specs=[ospec, ospec, ospec], out_shape=[sds, sds, sds],
      compiler_params=_params(("parallel", "parallel")),
  )(xb, wg, wu)


def _ffn_bwd_hidden(dzb, wd, a, b, name, tm=512, tn=1408):
  m, d = dzb.shape
  f = wd.shape[0]
  tm, tn = _tile(m, tm), _tile(f, tn)

  def body(dz_ref, wd_ref, a_ref, b_ref, da_ref, db_ref):
    dh = _dot_nt(dz_ref[...], wd_ref[...])
    av = a_ref[...].astype(F32)
    bv = b_ref[...].astype(F32)
    sg = _sigmoid(av)
    da_ref[...] = _bf(dh * bv * (sg * (1.0 + av * (1.0 - sg))))
    db_ref[...] = _bf(dh * (av * sg))

  hspec = pl.BlockSpec((tm, tn), lambda j, i: (i, j))
  sds = jax.ShapeDtypeStruct((m, f), BF16)
  return pl.pallas_call(
      body, name=name, grid=(f // tn, m // tm),
      in_specs=[pl.BlockSpec((tm, d), lambda j, i: (i, 0)), pl.BlockSpec((tn, d), lambda j, i: (j, 0)), hspec, hspec],
      out_specs=[hspec, hspec], out_shape=[sds, sds],
      compiler_params=_params(("parallel", "parallel")),
  )(dzb, wd, a, b)


def _ffn_bwd_input(da, db, wg, wu, dz, name, tm=512, tk=2816):
  m, f = da.shape
  d = wg.shape[0]
  tm, tk = _tile(m, tm), _tile(f, tk)
  nk = f // tk

  def body(da_ref, db_ref, wg_ref, wu_ref, dz_ref, o_ref, acc_ref):
    kk = pl.program_id(1)

    @pl.when(kk == 0)
    def _():
      acc_ref[...] = jnp.zeros_like(acc_ref)

    acc_ref[...] += _dot_nt(da_ref[...], wg_ref[...]) + _dot_nt(db_ref[...], wu_ref[...])

    @pl.when(kk == nk - 1)
    def _():
      o_ref[...] = DN_ALPHA * dz_ref[...] + acc_ref[...]

  hspec = pl.BlockSpec((tm, tk), lambda i, kk: (i, kk))
  wspec = pl.BlockSpec((d, tk), lambda i, kk: (0, kk))
  row = pl.BlockSpec((tm, d), lambda i, kk: (i, 0))
  return pl.pallas_call(
      body, name=name, grid=(m // tm, nk), in_specs=[hspec, hspec, wspec, wspec, row], out_specs=row,
      out_shape=jax.ShapeDtypeStruct((m, d), F32), scratch_shapes=[pltpu.VMEM((tm, d), F32)],
      compiler_params=_params(("parallel", "arbitrary")),
  )(da, db, wg, wu, dz)


def _alibi_slopes():
  n = N_DIL_HEADS
  return jnp.exp2(-8.0 * (jnp.arange(n, dtype=F32) + 1.0) / n).reshape(1, n)


def _band_consts():
  qi = lax.broadcasted_iota(jnp.int32, (BLOCK, BLOCK), 0)
  ki = lax.broadcasted_iota(jnp.int32, (BLOCK, BLOCK), 1)
  steps_cur = (qi - ki).astype(F32)
  steps_prev = (qi + BLOCK - ki).astype(F32)
  return ki < 64, steps_cur, steps_prev, ki <= qi, ki >= qi


def _rows(start, d):
  if d == 1:
    return pl.ds(pl.multiple_of(start, BLOCK), BLOCK)
  return pl.ds(start, BLOCK, stride=d)


def _fill_bias_tables(bias_sc, slope0, slope1):
  row = lax.broadcasted_iota(jnp.int32, (2 * BLOCK, 2 * BLOCK), 0)
  col = lax.broadcasted_iota(jnp.int32, (2 * BLOCK, 2 * BLOCK), 1)
  qi = jnp.bitwise_and(row, BLOCK - 1)
  ki = jnp.bitwise_and(col, BLOCK - 1)
  is_cur = col >= BLOCK
  steps = jnp.where(is_cur, qi - ki, qi + BLOCK - ki)
  valid = jnp.logical_and(steps >= 0, steps <= BLOCK)
  slope = jnp.where(row >= BLOCK, slope1, slope0)
  dist = slope * steps.astype(F32)
  for p, (_, d) in enumerate(DIL_PATTERNS):
    base = jnp.where(valid, -d * dist, NEG_BIG)
    bias_sc[2 * p] = base
    bias_sc[2 * p + 1] = jnp.where(is_cur, base, NEG_BIG)


def _stack_heads(v2, head0):
  return jnp.concatenate([jnp.where(head0, v2, 0.0), jnp.where(head0, 0.0, v2)], axis=0)


def _unstack_heads(v, head0):
  return jnp.where(head0, v[:BLOCK], v[BLOCK:])


def _block_rows(idx, d, nblk):
  r = idx // nblk
  n = idx % nblk
  cur = _rows(r + n * (BLOCK * d), d)
  prev = _rows(r + jnp.maximum(n - 1, 0) * (BLOCK * d), d)
  return cur, prev, n


def pair_tile(dt):
  return pltpu.VMEM((2 * BLOCK, 2 * BLOCK), dt)


def _two_stage_loop(nb, first_stage, second_stage, buf_a, buf_b):
  assert nb % 2 == 0

  def pair(t, carry):
    i = 2 * t + 1
    first_stage(i, buf_b)
    second_stage(i - 1, buf_a)
    first_stage(i + 1, buf_a)
    second_stage(i, buf_b)
    return carry

  first_stage(0, buf_a)
  lax.fori_loop(0, nb // 2 - 1, pair, 0)
  first_stage(nb - 1, buf_b)
  second_stage(nb - 2, buf_a)
  second_stage(nb - 1, buf_b)


def _attn_fwd(h3, name):
  bl, s, _ = h3.shape
  npair = N_DIL_HEADS // 2
  nb = s // BLOCK

  def body(sl_ref, q_ref, k_ref, v_ref, o_ref, lse_ref, o_sc, l_sc, bias_sc, s_a, s_b):
    hp = pl.program_id(1)
    head0 = lax.broadcasted_iota(jnp.int32, (BLOCK, LANES), 1) < 64
    _fill_bias_tables(bias_sc, sl_ref[0, 2 * hp], sl_ref[0, 2 * hp + 1])

    for p, (_, d) in enumerate(DIL_PATTERNS):
      nblk = (s // d) // BLOCK
      two = nblk > 1
      ks = slice(0, 2 * BLOCK) if two else slice(BLOCK, 2 * BLOCK)

      def scores(idx, buf, p=p, d=d, nblk=nblk, two=two, ks=ks):
        cur, prev, n = _block_rows(idx, d, nblk)
        qs = _bf(_stack_heads(q_ref[cur, :], head0) * ATT_SCALE)
        kb = _bf(jnp.concatenate([k_ref[prev, :], k_ref[cur, :]], axis=0)) if two else _bf(k_ref[cur, :])
        first = jnp.where(n == 0, 1, 0) if two else 0
        buf[:, ks] = _dot_nt(qs, kb) + bias_sc[2 * p + first, :, ks]

      def values(idx, buf, p=p, d=d, nblk=nblk, two=two, ks=ks):
        cur, prev, _ = _block_rows(idx, d, nblk)
        sc = buf[:, ks]
        mx = jnp.max(sc, axis=1, keepdims=True)
        pe = jnp.exp(sc - mx)
        den = jnp.sum(pe, axis=1, keepdims=True)
        vb = _bf(jnp.concatenate([v_ref[prev, :], v_ref[cur, :]], axis=0)) if two else _bf(v_ref[cur, :])
        acc = _dot(_bf(pe), vb) / den
        o_sc[p, cur, :] = _unstack_heads(acc, head0)
        l_sc[p, cur, :] = _unstack_heads(jnp.broadcast_to(mx + jnp.log(den), (2 * BLOCK, LANES)), head0)

      _two_stage_loop(nb, scores, values, s_a, s_b)

    def merge(i, carry):
      rows = pl.ds(pl.multiple_of(i * BLOCK, BLOCK), BLOCK)
      l0, l1, l2 = l_sc[0, rows, :], l_sc[1, rows, :], l_sc[2, rows, :]
      mx = jnp.maximum(jnp.maximum(l0, l1), l2)
      e0, e1, e2 = jnp.exp(l0 - mx), jnp.exp(l1 - mx), jnp.exp(l2 - mx)
      tot = e0 + e1 + e2
      o_ref[rows, :] = _bf((e0 * o_sc[0, rows, :] + e1 * o_sc[1, rows, :] + e2 * o_sc[2, rows, :]) / tot)
      lse_ref[rows, :] = mx + jnp.log(tot)
      return carry

    lax.fori_loop(0, nb, merge, 0)

  def col(off):
    return pl.BlockSpec((None, s, LANES), lambda b, p: (b, 0, off + p))

  return pl.pallas_call(
      body, name=name, grid=(bl, npair),
      in_specs=[pl.BlockSpec(memory_space=pltpu.SMEM), col(0), col(npair), col(2 * npair)],
      out_specs=[col(0), col(0)],
      out_shape=[jax.ShapeDtypeStruct((bl, s, DIL_WIDTH), BF16), jax.ShapeDtypeStruct((bl, s, DIL_WIDTH), F32)],
      scratch_shapes=[pltpu.VMEM((3, s, LANES), F32), pltpu.VMEM((3, s, LANES), F32),
                      pltpu.VMEM((6, 2 * BLOCK, 2 * BLOCK), F32), pair_tile(F32), pair_tile(F32)],
      compiler_params=_params(("parallel", "parallel")),
  )(_alibi_slopes(), h3, h3, h3)


def _attn_bwd(h3, out3, lse3, dcat3, name):
  bl, s, _ = h3.shape
  npair = N_DIL_HEADS // 2
  nb = s // BLOCK

  def body(sl_ref, q_ref, k_ref, v_ref, o_ref, l_ref, do_ref, dq_out, dk_out, dv_out,
           bias_sc, p_a, ds_a, p_b, ds_b, prod_sc, dq_ref, dk_ref, dv_ref):
    hp = pl.program_id(1)
    lane = lax.broadcasted_iota(jnp.int32, (BLOCK, LANES), 1)
    head0 = lane < 64
    _fill_bias_tables(bias_sc, sl_ref[0, 2 * hp], sl_ref[0, 2 * hp + 1])
    dq_ref[...] = jnp.zeros_like(dq_ref)
    dk_ref[...] = jnp.zeros_like(dk_ref)
    dv_ref[...] = jnp.zeros_like(dv_ref)
    prod_sc[...] = do_ref[...] * o_ref[...].astype(F32)

    def per_row(v2, pick0, pick1):
      return jnp.concatenate([jnp.sum(jnp.where(pick0, v2, 0.0), axis=1, keepdims=True),
                              jnp.sum(jnp.where(pick1, v2, 0.0), axis=1, keepdims=True)], axis=0)

    for p, (_, d) in enumerate(DIL_PATTERNS):
      nblk = (s // d) // BLOCK
      two = nblk > 1
      ks = slice(0, 2 * BLOCK) if two else slice(BLOCK, 2 * BLOCK)

      def operands(idx, d=d, nblk=nblk, two=two):
        cur, prev, n = _block_rows(idx, d, nblk)
        qs = _bf(_stack_heads(q_ref[cur, :], head0) * ATT_SCALE)
        dos = _bf(_stack_heads(do_ref[cur, :], head0))
        kb = _bf(jnp.concatenate([k_ref[prev, :], k_ref[cur, :]], axis=0)) if two else _bf(k_ref[cur, :])
        return cur, prev, n, qs, dos, kb

      def probs(idx, bufs, p=p, two=two, ks=ks, operands=operands):
        cur, prev, n, qs, dos, kb = operands(idx)
        vb = _bf(jnp.concatenate([v_ref[prev, :], v_ref[cur, :]], axis=0)) if two else _bf(v_ref[cur, :])
        lse = per_row(l_ref[cur, :], lane == 0, lane == 64)
        delta = per_row(prod_sc[cur, :], head0, jnp.logical_not(head0))
        first = jnp.where(n == 0, 1, 0) if two else 0
        pr = jnp.exp(_dot_nt(qs, kb) + bias_sc[2 * p + first, :, ks] - lse)
        bufs[0][:, ks] = _bf(pr)
        bufs[1][:, ks] = _bf(pr * (_dot_nt(dos, vb) - delta))

      def products(idx, bufs, two=two, ks=ks, operands=operands):
        cur, prev, _, qs, dos, kb = operands(idx)
        pr = bufs[0][:, ks]
        ds = bufs[1][:, ks]
        dq_ref[cur, :] += _unstack_heads(_dot(ds, kb), head0) * ATT_SCALE
        dkb = _dot_tn(ds, qs)
        dvb = _dot_tn(pr, dos)
        if two:
          dk_ref[prev, :] += dkb[:BLOCK]
          dv_ref[prev, :] += dvb[:BLOCK]
          dk_ref[cur, :] += dkb[BLOCK:]
          dv_ref[cur, :] += dvb[BLOCK:]
        else:
          dk_ref[cur, :] += dkb
          dv_ref[cur, :] += dvb

      _two_stage_loop(nb, probs, products, (p_a, ds_a), (p_b, ds_b))

    dq_out[...] = _bf(dq_ref[...])
    dk_out[...] = _bf(dk_ref[...])
    dv_out[...] = _bf(dv_ref[...])

  def col(off):
    return pl.BlockSpec((None, s, LANES), lambda b, p: (b, 0, off + p))

  sds = jax.ShapeDtypeStruct((bl, s, DIL_WIDTH), BF16)
  return pl.pallas_call(
      body, name=name, grid=(bl, npair),
      in_specs=[pl.BlockSpec(memory_space=pltpu.SMEM), col(0), col(npair), col(2 * npair), col(0), col(0), col(0)],
      out_specs=[col(0), col(0), col(0)], out_shape=[sds, sds, sds],
      scratch_shapes=[pltpu.VMEM((6, 2 * BLOCK, 2 * BLOCK), F32)] + [pair_tile(BF16)] * 4
      + [pltpu.VMEM((s, LANES), F32)] * 4,
      compiler_params=_params(("parallel", "parallel")),
  )(_alibi_slopes(), h3, h3, h3, out3, lse3, dcat3)


def _attn_fwd_old(h3, name):
  bl, s, _ = h3.shape
  npair = N_DIL_HEADS // 2

  def body(sl_ref, q_ref, k_ref, v_ref, o_ref, lse_ref, o_sc, l_sc):
    hp = pl.program_id(1)
    head0, steps_cur, steps_prev, mask_cur, mask_prev = _band_consts()
    slope = [sl_ref[0, 2 * hp], sl_ref[0, 2 * hp + 1]]

    for p, (_, d) in enumerate(DIL_PATTERNS):
      nblk = (s // d) // BLOCK
      has_prev_block = nblk > 1

      def blk(idx, carry, p=p, d=d, nblk=nblk, has_prev_block=has_prev_block):
        r = idx // nblk
        n = idx % nblk
        cur = _rows(r + n * (BLOCK * d), d)
        q2 = q_ref[cur, :]
        kc = _bf(k_ref[cur, :])
        vc = _bf(v_ref[cur, :])
        if has_prev_block:
          prev = _rows(r + jnp.maximum(n - 1, 0) * (BLOCK * d), d)
          kp = _bf(k_ref[prev, :])
          vp = _bf(v_ref[prev, :])
          first_block = jnp.where(n > 0, 0.0, NEG_BIG)
        outs, lses = [], []
        for j in range(2):
          hm = head0 if j == 0 else jnp.logical_not(head0)
          qj = _bf(jnp.where(hm, q2, 0.0) * ATT_SCALE)
          sc = _dot_nt(qj, kc) - (slope[j] * d) * steps_cur
          sc = jnp.where(mask_cur, sc, NEG_BIG)
          mx = jnp.max(sc, axis=1, keepdims=True)
          if has_prev_block:
            sp = _dot_nt(qj, kp) - (slope[j] * d) * steps_prev + first_block
            sp = jnp.where(mask_prev, sp, NEG_BIG)
            mx = jnp.maximum(mx, jnp.max(sp, axis=1, keepdims=True))
          pc = jnp.exp(sc - mx)
          den = jnp.sum(pc, axis=1, keepdims=True)
          acc = _dot(_bf(pc), vc)
          if has_prev_block:
            pp = jnp.exp(sp - mx)
            den = den + jnp.sum(pp, axis=1, keepdims=True)
            acc = acc + _dot(_bf(pp), vp)
          outs.append(acc / den)
          lses.append(mx + jnp.log(den))
        o_sc[p, cur, :] = jnp.where(head0, outs[0], outs[1])
        l_sc[p, cur, :] = jnp.where(head0, lses[0], lses[1])
        return carry

      lax.fori_loop(0, s // BLOCK, blk, 0, unroll=ATTN_UNROLL)

    def merge(i, carry):
      rows = pl.ds(pl.multiple_of(i * BLOCK, BLOCK), BLOCK)
      l0, l1, l2 = l_sc[0, rows, :], l_sc[1, rows, :], l_sc[2, rows, :]
      mx = jnp.maximum(jnp.maximum(l0, l1), l2)
      e0, e1, e2 = jnp.exp(l0 - mx), jnp.exp(l1 - mx), jnp.exp(l2 - mx)
      tot = e0 + e1 + e2
      o_ref[rows, :] = (e0 * o_sc[0, rows, :] + e1 * o_sc[1, rows, :] + e2 * o_sc[2, rows, :]) / tot
      lse_ref[rows, :] = mx + jnp.log(tot)
      return carry

    lax.fori_loop(0, s // BLOCK, merge, 0)

  def col(off):
    return pl.BlockSpec((None, s, LANES), lambda b, p: (b, 0, off + p))

  sds = jax.ShapeDtypeStruct((bl, s, DIL_WIDTH), F32)
  return pl.pallas_call(
      body, name=name, grid=(bl, npair),
      in_specs=[pl.BlockSpec(memory_space=pltpu.SMEM), col(0), col(npair), col(2 * npair)],
      out_specs=[col(0), col(0)], out_shape=[sds, sds],
      scratch_shapes=[pltpu.VMEM((3, s, LANES), F32), pltpu.VMEM((3, s, LANES), F32)],
      compiler_params=_params(("parallel", "parallel")),
  )(_alibi_slopes(), h3, h3, h3)


def _attn_bwd_old(h3, out3, lse3, dcat3, name):
  bl, s, _ = h3.shape
  npair = N_DIL_HEADS // 2

  def body(sl_ref, q_ref, k_ref, v_ref, o_ref, l_ref, do_ref, dq_ref, dk_ref, dv_ref):
    hp = pl.program_id(1)
    head0, steps_cur, steps_prev, mask_cur, mask_prev = _band_consts()
    lane = lax.broadcasted_iota(jnp.int32, (BLOCK, LANES), 1)
    slope = [sl_ref[0, 2 * hp], sl_ref[0, 2 * hp + 1]]
    dq_ref[...] = jnp.zeros_like(dq_ref)
    dk_ref[...] = jnp.zeros_like(dk_ref)
    dv_ref[...] = jnp.zeros_like(dv_ref)

    for p, (_, d) in enumerate(DIL_PATTERNS):
      nblk = (s // d) // BLOCK
      has_prev_block = nblk > 1

      def blk(idx, carry, d=d, nblk=nblk, has_prev_block=has_prev_block):
        r = idx // nblk
        n = idx % nblk
        cur = _rows(r + n * (BLOCK * d), d)
        q2 = q_ref[cur, :]
        do2 = do_ref[cur, :]
        l2 = l_ref[cur, :]
        prod = do2 * o_ref[cur, :]
        kc = _bf(k_ref[cur, :])
        vc = _bf(v_ref[cur, :])
        if has_prev_block:
          prev = _rows(r + jnp.maximum(n - 1, 0) * (BLOCK * d), d)
          kp = _bf(k_ref[prev, :])
          vp = _bf(v_ref[prev, :])
          first_block = jnp.where(n > 0, 0.0, NEG_BIG)
          dkp = jnp.zeros((BLOCK, LANES), F32)
          dvp = jnp.zeros((BLOCK, LANES), F32)
        dq2 = jnp.zeros((BLOCK, LANES), F32)
        dkc = jnp.zeros((BLOCK, LANES), F32)
        dvc = jnp.zeros((BLOCK, LANES), F32)
        for j in range(2):
          hm = head0 if j == 0 else jnp.logical_not(head0)
          qj = _bf(jnp.where(hm, q2, 0.0) * ATT_SCALE)
          doj = _bf(jnp.where(hm, do2, 0.0))
          lj = jnp.sum(jnp.where(lane == 64 * j, l2, 0.0), axis=1, keepdims=True)
          dj = jnp.sum(jnp.where(hm, prod, 0.0), axis=1, keepdims=True)
          sc = _dot_nt(qj, kc) - (slope[j] * d) * steps_cur
          pc = jnp.exp(jnp.where(mask_cur, sc - lj, NEG_BIG))
          dsc = _bf(pc * (_dot_nt(doj, vc) - dj))
          dq_j = _dot(dsc, kc)
          dkc = dkc + _dot_tn(dsc, qj)
          dvc = dvc + _dot_tn(_bf(pc), doj)
          if has_prev_block:
            sp = _dot_nt(qj, kp) - (slope[j] * d) * steps_prev + first_block
            pp = jnp.exp(jnp.where(mask_prev, sp - lj, NEG_BIG))
            dsp = _bf(pp * (_dot_nt(doj, vp) - dj))
            dq_j = dq_j + _dot(dsp, kp)
            dkp = dkp + _dot_tn(dsp, qj)
            dvp = dvp + _dot_tn(_bf(pp), doj)
          dq2 = dq2 + jnp.where(hm, dq_j, 0.0) * ATT_SCALE
        dq_ref[cur, :] += dq2
        dk_ref[cur, :] += dkc
        dv_ref[cur, :] += dvc
        if has_prev_block:
          dk_ref[prev, :] += dkp
          dv_ref[prev, :] += dvp
        return carry

      lax.fori_loop(0, s // BLOCK, blk, 0, unroll=ATTN_UNROLL)

  def col(off):
    return pl.BlockSpec((None, s, LANES), lambda b, p: (b, 0, off + p))

  sds = jax.ShapeDtypeStruct((bl, s, DIL_WIDTH), F32)
  return pl.pallas_call(
      body, name=name, grid=(bl, npair),
      in_specs=[pl.BlockSpec(memory_space=pltpu.SMEM), col(0), col(npair), col(2 * npair), col(0), col(0), col(0)],
      out_specs=[col(0), col(0), col(0)], out_shape=[sds, sds, sds],
      compiler_params=_params(("parallel", "parallel")),
  )(_alibi_slopes(), h3, h3, h3, out3, lse3, dcat3)


def _mem_heads(tq):
  lane = lax.broadcasted_iota(jnp.int32, (tq, LANES), 1)
  return lane < 64


def _mem_fwd(h3, qcol, mkv3, name, tq=512):
  bl, s, _ = h3.shape
  nm = mkv3.shape[1]
  tq = _tile(s, tq)

  def body(q_ref, kv_ref, o_ref):
    head0 = _mem_heads(tq)
    for lg in range(MEM_WIDTH // LANES):
      cs = slice(lg * LANES, (lg + 1) * LANES)
      q2 = q_ref[:, cs]
      mk = _bf(kv_ref[:, cs])
      mv = _bf(kv_ref[:, MEM_WIDTH + lg * LANES:MEM_WIDTH + (lg + 1) * LANES])
      outs = []
      for j in range(2):
        hm = head0 if j == 0 else jnp.logical_not(head0)
        qj = _bf(jnp.where(hm, q2, 0.0) * ATT_SCALE)
        sc = _dot_nt(qj, mk)
        mx = jnp.max(sc, axis=1, keepdims=True)
        pe = jnp.exp(sc - mx)
        den = jnp.sum(pe, axis=1, keepdims=True)
        outs.append(_dot(_bf(pe / den), mv))
      o_ref[:, cs] = _bf(jnp.where(head0, outs[0], outs[1]))

  return pl.pallas_call(
      body, name=name, grid=(bl, s // tq),
      in_specs=[pl.BlockSpec((None, tq, MEM_WIDTH), lambda b, i: (b, i, qcol)),
                pl.BlockSpec((None, nm, 2 * MEM_WIDTH), lambda b, i: (b, 0, 0))],
      out_specs=pl.BlockSpec((None, tq, MEM_WIDTH), lambda b, i: (b, i, 0)),
      out_shape=jax.ShapeDtypeStruct((bl, s, MEM_WIDTH), BF16),
      compiler_params=_params(("parallel", "parallel")),
  )(h3, mkv3)


def _mem_bwd(h3, qcol, mkv3, dcat3, name, tq=512):
  bl, s, _ = h3.shape
  nm = mkv3.shape[1]
  tq = _tile(s, tq)
  docol = dcat3.shape[2] // MEM_WIDTH - 1

  def body(q_ref, kv_ref, do_ref, dq_ref, dkv_ref):
    i = pl.program_id(1)

    @pl.when(i == 0)
    def _():
      dkv_ref[...] = jnp.zeros_like(dkv_ref)

    head0 = _mem_heads(tq)
    for lg in range(MEM_WIDTH // LANES):
      cs = slice(lg * LANES, (lg + 1) * LANES)
      vs = slice(MEM_WIDTH + lg * LANES, MEM_WIDTH + (lg + 1) * LANES)
      q2 = q_ref[:, cs]
      do2 = do_ref[:, cs]
      mk = _bf(kv_ref[:, cs])
      mv = _bf(kv_ref[:, vs])
      dq2 = jnp.zeros((tq, LANES), F32)
      dmk = jnp.zeros((nm, LANES), F32)
      dmv = jnp.zeros((nm, LANES), F32)
      for j in range(2):
        hm = head0 if j == 0 else jnp.logical_not(head0)
        qj = _bf(jnp.where(hm, q2, 0.0) * ATT_SCALE)
        doj = _bf(jnp.where(hm, do2, 0.0))
        sc = _dot_nt(qj, mk)
        mx = jnp.max(sc, axis=1, keepdims=True)
        pe = jnp.exp(sc - mx)
        pn = pe / jnp.sum(pe, axis=1, keepdims=True)
        pb = _bf(pn)
        dp = _dot_nt(doj, mv)
        dj = jnp.sum(pb.astype(F32) * dp, axis=1, keepdims=True)
        ds = _bf(pn * (dp - dj))
        dq2 = dq2 + jnp.where(hm, _dot(ds, mk), 0.0) * ATT_SCALE
        dmk = dmk + _dot_tn(ds, qj)
        dmv = dmv + _dot_tn(pb, doj)
      dq_ref[:, cs] = _bf(dq2)
      dkv_ref[:, cs] += dmk
      dkv_ref[:, vs] += dmv

  return pl.pallas_call(
      body, name=name, grid=(bl, s // tq),
      in_specs=[pl.BlockSpec((None, tq, MEM_WIDTH), lambda b, i: (b, i, qcol)),
                pl.BlockSpec((None, nm, 2 * MEM_WIDTH), lambda b, i: (b, 0, 0)),
                pl.BlockSpec((None, tq, MEM_WIDTH), lambda b, i: (b, i, docol))],
      out_specs=[pl.BlockSpec((None, tq, MEM_WIDTH), lambda b, i: (b, i, 0)),
                 pl.BlockSpec((None, nm, 2 * MEM_WIDTH), lambda b, i: (b, 0, 0))],
      out_shape=[jax.ShapeDtypeStruct((bl, s, MEM_WIDTH), BF16), jax.ShapeDtypeStruct((bl, nm, 2 * MEM_WIDTH), F32)],
      compiler_params=_params(("parallel", "arbitrary")),
  )(h3, mkv3, dcat3)


def _sgu_consts():
  ti = lax.broadcasted_iota(jnp.int32, (CHUNK, CHUNK), 0)
  si = lax.broadcasted_iota(jnp.int32, (CHUNK, CHUNK), 1)
  return si <= ti, si < 64


def _sgu_bias_lanes(b_s):
  return jnp.repeat(b_s.T, 64, axis=1)


def _sgu_fwd(h2, ln_g, ln_b, w_s, b_s, name, tr=512):
  t, _ = h2.shape
  tr = _tile(t, tr)
  nch = tr // CHUNK
  npair = N_SGU_GROUPS // 2

  def body(u_ref, v_ref, g_ref, b_ref, w_ref, bs_ref, o_ref, vn_sc):
    tril, head0 = _sgu_consts()
    xhat, _ = _ln_stats(_gelu(v_ref[...]))
    vn_sc[...] = _bf(xhat * g_ref[...] + b_ref[...])
    for jp in range(npair):
      cs = slice(jp * LANES, (jp + 1) * LANES)
      w0 = _bf(jnp.where(tril, w_ref[2 * jp], 0.0))
      w1 = _bf(jnp.where(tril, w_ref[2 * jp + 1], 0.0))
      bias = bs_ref[:, cs]
      for c in range(nch):
        rs = slice(c * CHUNK, (c + 1) * CHUNK)
        vb = vn_sc[rs, cs]
        mixed = jnp.where(head0, _dot(w0, vb), _dot(w1, vb)) + bias
        o_ref[rs, cs] = _bf(_gelu(u_ref[rs, cs]) * mixed)

  blk = lambda j: pl.BlockSpec((tr, SGU_WIDTH), lambda i: (i, j))
  vec = pl.BlockSpec((1, SGU_WIDTH), lambda i: (0, 0))
  return pl.pallas_call(
      body, name=name, grid=(t // tr,),
      in_specs=[blk(0), blk(1), vec, vec,
                pl.BlockSpec((N_SGU_GROUPS, CHUNK, CHUNK), lambda i: (0, 0, 0)),
                pl.BlockSpec((CHUNK, SGU_WIDTH), lambda i: (0, 0))],
      out_specs=blk(0), out_shape=jax.ShapeDtypeStruct((t, SGU_WIDTH), BF16),
      scratch_shapes=[pltpu.VMEM((tr, SGU_WIDTH), BF16)],
      compiler_params=_params(("parallel",)),
  )(h2, h2, ln_g.reshape(1, -1), ln_b.reshape(1, -1), w_s, _sgu_bias_lanes(b_s))


def _sgu_bwd(h2, dcat, ln_g, ln_b, w_s, b_s, name, tr=512):
  t, _ = h2.shape
  tr = _tile(t, tr)
  nch = tr // CHUNK
  npair = N_SGU_GROUPS // 2
  nsteps = t // tr

  def body(u_ref, v_ref, dm_ref, g_ref, b_ref, w_ref, bs_ref,
           du_ref, dv_ref, dw_ref, dbs_ref, dg_ref, db_ref, vn_sc, dmx_sc, dvn_sc, mix_sc, dbx_sc):
    i = pl.program_id(0)
    tril, head0 = _sgu_consts()

    @pl.when(i == 0)
    def _():
      dw_ref[...] = jnp.zeros_like(dw_ref)
      dg_ref[...] = jnp.zeros_like(dg_ref)
      db_ref[...] = jnp.zeros_like(db_ref)
      dbx_sc[...] = jnp.zeros_like(dbx_sc)

    gv, gv_der = _gelu_parts(v_ref[...])
    xhat, rstd = _ln_stats(gv)
    g = g_ref[...]
    vn_sc[...] = _bf(xhat * g + b_ref[...])
    gu, gu_der = _gelu_parts(u_ref[...])
    dmix = dm_ref[...]
    dmx_sc[...] = dmix * gu

    for jp in range(npair):
      cs = slice(jp * LANES, (jp + 1) * LANES)
      w0 = _bf(jnp.where(tril, w_ref[2 * jp], 0.0))
      w1 = _bf(jnp.where(tril, w_ref[2 * jp + 1], 0.0))
      bias = bs_ref[:, cs]
      dw0 = jnp.zeros((CHUNK, CHUNK), F32)
      dw1 = jnp.zeros((CHUNK, CHUNK), F32)
      dbx = jnp.zeros((CHUNK, LANES), F32)
      for c in range(nch):
        rs = slice(c * CHUNK, (c + 1) * CHUNK)
        vb = vn_sc[rs, cs]
        mix_sc[rs, cs] = jnp.where(head0, _dot(w0, vb), _dot(w1, vb)) + bias
        dmx = dmx_sc[rs, cs]
        d0 = _bf(jnp.where(head0, dmx, 0.0))
        d1 = _bf(jnp.where(head0, 0.0, dmx))
        dvn_sc[rs, cs] = _dot_tn(w0, d0) + _dot_tn(w1, d1)
        dw0 = dw0 + _dot_nt(d0, vb)
        dw1 = dw1 + _dot_nt(d1, vb)
        dbx = dbx + dmx
      dw_ref[2 * jp] += dw0
      dw_ref[2 * jp + 1] += dw1
      dbx_sc[:, cs] += dbx

    du_ref[...] = _bf(dmix * mix_sc[...] * gu_der)
    dvn = dvn_sc[...]
    dv_ref[...] = _bf(_ln_bwd(dvn, xhat, rstd, g) * gv_der)
    dg_ref[...] += jnp.sum(dvn * xhat, axis=0, keepdims=True)
    db_ref[...] += jnp.sum(dvn, axis=0, keepdims=True)

    @pl.when(i == nsteps - 1)
    def _():
      lane = lax.broadcasted_iota(jnp.int32, (CHUNK, LANES), 1)
      acc = jnp.zeros((CHUNK, LANES), F32)
      for gi in range(N_SGU_GROUPS):
        jp, j = gi // 2, gi % 2
        part = dbx_sc[:, jp * LANES:(jp + 1) * LANES]
        hm = (lane < 64) if j == 0 else (lane >= 64)
        colsum = jnp.sum(jnp.where(hm, part, 0.0), axis=1, keepdims=True)
        acc = jnp.where(lane == gi, colsum, acc)
        dw_ref[gi] = jnp.where(tril, dw_ref[gi], 0.0)
      dbs_ref[...] = acc

  blk = lambda j: pl.BlockSpec((tr, SGU_WIDTH), lambda i: (i, j))
  vec = pl.BlockSpec((1, SGU_WIDTH), lambda i: (0, 0))
  wspec = pl.BlockSpec((N_SGU_GROUPS, CHUNK, CHUNK), lambda i: (0, 0, 0))
  big = lambda dt: pltpu.VMEM((tr, SGU_WIDTH), dt)
  du, dv, dw, dbs, dg, db = pl.pallas_call(
      body, name=name, grid=(nsteps,),
      in_specs=[blk(0), blk(1), blk(0), vec, vec, wspec, pl.BlockSpec((CHUNK, SGU_WIDTH), lambda i: (0, 0))],
      out_specs=[blk(0), blk(0), wspec, pl.BlockSpec((CHUNK, LANES), lambda i: (0, 0)), vec, vec],
      out_shape=[jax.ShapeDtypeStruct((t, SGU_WIDTH), BF16), jax.ShapeDtypeStruct((t, SGU_WIDTH), BF16),
                 jax.ShapeDtypeStruct((N_SGU_GROUPS, CHUNK, CHUNK), F32), jax.ShapeDtypeStruct((CHUNK, LANES), F32),
                 jax.ShapeDtypeStruct((1, SGU_WIDTH), F32), jax.ShapeDtypeStruct((1, SGU_WIDTH), F32)],
      scratch_shapes=[big(BF16), big(F32), big(F32), big(F32), pltpu.VMEM((CHUNK, SGU_WIDTH), F32)],
      compiler_params=_params(("arbitrary",)),
  )(h2, h2, dcat, ln_g.reshape(1, -1), ln_b.reshape(1, -1), w_s, _sgu_bias_lanes(b_s))
  return du, dv, dw, dbs[:, :N_SGU_GROUPS].T, dg[0], db[0]


def _loss_head(xo, tgt, name, tm=512):
  m, d = xo.shape
  tm = _tile(m, tm)

  def body(x_ref, t_ref, dx_ref, l_ref):
    @pl.when(pl.program_id(0) == 0)
    def _():
      l_ref[...] = jnp.zeros_like(l_ref)

    diff = x_ref[...] - t_ref[...]
    dx_ref[...] = diff * (1.0 / d)
    rowsum = jnp.sum(diff * diff, axis=1, keepdims=True)
    tot = jnp.sum(rowsum, axis=0, keepdims=True) * (0.5 / d)
    l_ref[...] += jnp.broadcast_to(tot, l_ref.shape)

  row = pl.BlockSpec((tm, d), lambda i: (i, 0))
  dx, l = pl.pallas_call(
      body, name=name, grid=(m // tm,), in_specs=[row, row],
      out_specs=[row, pl.BlockSpec((8, LANES), lambda i: (0, 0))],
      out_shape=[jax.ShapeDtypeStruct((m, d), F32), jax.ShapeDtypeStruct((8, LANES), F32)],
      compiler_params=_params(("arbitrary",)),
  )(xo, tgt)
  return l[0, 0], dx


def _local_step(x3, mem3, tgt3, w):
  bl, s, d = x3.shape
  t = bl * s
  nm = mem3.shape[1]
  mem2 = mem3.reshape(bl * nm, d)
  x = x3.reshape(t, d)
  xb = x
  saved = []
  for i in range(DEPTH):
    j = i // 2
    attn = i % 2 == 0
    mkv = _mm(mem2, w["w_mem_kv"][i], "nn", F32, f"mkv_fwd_{i}", tm=1024, tn=512, tk=1024)
    mkv3 = mkv.reshape(bl, nm, 2 * MEM_WIDTH)
    w_in = w["a_w_in"][j] if attn else w["b_w_in"][j]
    h = _mm(xb, w_in, "nn", F32, f"in_proj_{i}", tm=512, tn=w_in.shape[1], tk=d)
    h3 = h.reshape(bl, s, -1)
    if attn:
      mix3, lse3 = _attn_fwd(h3, f"dil_attn_fwd_{i}")
      mix = mix3.reshape(t, DIL_WIDTH)
      qcol = 3 * DIL_WIDTH // MEM_WIDTH
    else:
      mix = _sgu_fwd(h, w["sgu_ln_g"][j], w["sgu_ln_b"][j], w["sgu_w_s"][j], w["sgu_b_s"][j], f"sgu_fwd_{i}")
      lse3 = None
      qcol = 2 * SGU_WIDTH // MEM_WIDTH
    mo = _mem_fwd(h3, qcol, mkv3, f"mem_attn_fwd_{i}").reshape(t, MEM_WIDTH)
    cat = jnp.concatenate([mix, mo], axis=1)
    z1, xm, xmb = _mm_res_ln(cat, w["w_out"][i], x, w["ln_mix_g"][i], w["ln_mix_b"][i], f"out_proj_ln_{i}", tk=1024)
    a, b, hm = _ffn_up(xmb, w["w_gate"][i], w["w_up"][i], f"ffn_up_{i}")
    z2, xo, xob = _mm_res_ln(hm, w["w_down"][i], xm, w["ln_ffn_g"][i], w["ln_ffn_b"][i], f"ffn_down_ln_{i}", tk=hm.shape[1])
    saved.append(dict(xb=xb, h=h, h3=h3, mkv3=mkv3, mix3=(mix3 if attn else None), lse3=lse3, cat=cat, z1=z1,
                      xmb=xmb, a=a, b=b, hm=hm, z2=z2, qcol=qcol))
    x, xb = xo, xob

  loss, dx = _loss_head(x, tgt3.reshape(t, d), "loss_head")

  names = ("a_w_in", "b_w_in", "sgu_ln_g", "sgu_ln_b", "sgu_w_s", "sgu_b_s", "w_mem_kv", "w_out",
           "ln_mix_g", "ln_mix_b", "w_gate", "w_up", "w_down", "ln_ffn_g", "ln_ffn_b")
  grads = {n: [None] * w[n].shape[0] for n in names}
  for i in reversed(range(DEPTH)):
    j = i // 2
    attn = i % 2 == 0
    sv = saved[i]
    dz2, dz2b, grads["ln_ffn_g"][i], grads["ln_ffn_b"][i] = _ln_bwd_call(dx, sv["z2"], w["ln_ffn_g"][i], f"ln_ffn_bwd_{i}")
    da, db = _ffn_bwd_hidden(dz2b, w["w_down"][i], sv["a"], sv["b"], f"ffn_bwd_hidden_{i}")
    grads["w_down"][i] = _mm(sv["hm"], dz2b, "tn", F32, f"dw_down_{i}", tm=1408, tn=1024, tk=1024)
    grads["w_gate"][i] = _mm(sv["xmb"], da, "tn", F32, f"dw_gate_{i}", tm=1024, tn=1408, tk=1024)
    grads["w_up"][i] = _mm(sv["xmb"], db, "tn", F32, f"dw_up_{i}", tm=1024, tn=1408, tk=1024)
    dxm = _ffn_bwd_input(da, db, w["w_gate"][i], w["w_up"][i], dz2, f"ffn_bwd_input_{i}")
    dz1, dz1b, grads["ln_mix_g"][i], grads["ln_mix_b"][i] = _ln_bwd_call(dxm, sv["z1"], w["ln_mix_g"][i], f"ln_mix_bwd_{i}")
    grads["w_out"][i] = _mm(sv["cat"], dz1b, "tn", F32, f"dw_out_{i}", tm=1024, tn=1024, tk=1024)
    dcat = _mm(dz1b, w["w_out"][i], "nt", F32, f"out_proj_bwd_{i}", tm=1024, tn=1024, tk=1024)
    dcat3 = dcat.reshape(bl, s, -1)
    dqm3, dmkv3 = _mem_bwd(sv["h3"], sv["qcol"], sv["mkv3"], dcat3, f"mem_attn_bwd_{i}")
    grads["w_mem_kv"][i] = _mm(mem2, dmkv3.reshape(bl * nm, 2 * MEM_WIDTH), "tn", F32, f"dw_mem_kv_{i}", tm=1024, tn=512, tk=1024)
    dqm = dqm3.reshape(t, MEM_WIDTH)
    if attn:
      dq3, dk3, dv3 = _attn_bwd(sv["h3"], sv["mix3"], sv["lse3"], dcat3, f"dil_attn_bwd_{i}")
      parts = [dq3.reshape(t, -1), dk3.reshape(t, -1), dv3.reshape(t, -1), dqm]
    else:
      du, dv, dws, dbs, dlg, dlb = _sgu_bwd(sv["h"], dcat, w["sgu_ln_g"][j], w["sgu_ln_b"][j], w["sgu_w_s"][j],
                                             w["sgu_b_s"][j], f"sgu_bwd_{i}")
      grads["sgu_w_s"][j], grads["sgu_b_s"][j], grads["sgu_ln_g"][j], grads["sgu_ln_b"][j] = dws, dbs, dlg, dlb
      parts = [du, dv, dqm]
    dh = jnp.concatenate(parts, axis=1)
    w_in = w["a_w_in"][j] if attn else w["b_w_in"][j]
    grads["a_w_in" if attn else "b_w_in"][j] = _mm(sv["xb"], dh, "tn", F32, f"dw_in_{i}", tm=1024, tn=1280 if attn else 896, tk=1024)
    dx = _mm(dh, w_in, "nt", F32, f"in_proj_bwd_{i}", add=dz1, add_scale=DN_ALPHA, tm=512, tn=d, tk=w_in.shape[1])
  return loss, dx.reshape(bl, s, d), grads


def _my_place():
  return lax.axis_index("x"), lax.axis_index("y"), lax.axis_index("c")


def _other_chips(x, y):
  return [(1 - x, y), (x, 1 - y), (1 - x, 1 - y)]


ANY = pl.BlockSpec(memory_space=pl.ANY)


def _all_gather_halves(wl, name):
  _, r, c_ = wl.shape

  def body(w_ref, g_ref, send_sems, recv_sems):
    x, y, c = _my_place()
    me = 2 * x + y
    sibling = (x, y, 1 - c)
    chips = _other_chips(x, y)

    def copy(k, src, dst, to):
      return pltpu.make_async_remote_copy(src_ref=src, dst_ref=dst, send_sem=send_sems.at[k], recv_sem=recv_sems.at[k],
                                          device_id=to, device_id_type=MESH_ID)

    first = [copy(k, w_ref.at[c], g_ref.at[me, c], (px, py, c)) for k, (px, py) in enumerate(chips)]
    for cp in first:
      cp.start()
    passed = []
    for k, (px, py) in enumerate(chips):
      landed = g_ref.at[2 * px + py, c]
      copy(k, landed, landed, (px, py, c)).wait_recv()
      fwd = copy(3 + k, landed, landed, sibling)
      fwd.start()
      passed.append(fwd)
    for k, (px, py) in enumerate(chips):
      theirs = g_ref.at[2 * px + py, 1 - c]
      copy(3 + k, theirs, theirs, sibling).wait_recv()
    for cp in first + passed:
      cp.wait_send()

  got = pl.pallas_call(
      body, name=name, in_specs=[ANY], out_specs=ANY,
      out_shape=jax.ShapeDtypeStruct((4, 2, r, c_), wl.dtype),
      scratch_shapes=[pltpu.SemaphoreType.DMA((6,)), pltpu.SemaphoreType.DMA((6,))],
  )(wl)
  chip = 2 * lax.axis_index("x") + lax.axis_index("y")
  return lax.dynamic_update_slice(got, wl[None], (chip, 0, 0, 0))


def _sibling_swap(v, name):
  def body(v_ref, o_ref, send_sem, recv_sem):
    x, y, c = _my_place()
    cp = pltpu.make_async_remote_copy(src_ref=v_ref, dst_ref=o_ref, send_sem=send_sem, recv_sem=recv_sem,
                                      device_id=(x, y, 1 - c), device_id_type=MESH_ID)
    cp.start()
    cp.wait()

  return pl.pallas_call(
      body, name=name, in_specs=[ANY], out_specs=ANY, out_shape=jax.ShapeDtypeStruct(v.shape, v.dtype),
      scratch_shapes=[pltpu.SemaphoreType.DMA, pltpu.SemaphoreType.DMA],
  )(v)


def _chip_exchange(q, name):
  _, r, c_ = q.shape

  def body(q_ref, o_ref, send_sems, recv_sems):
    x, y, c = _my_place()
    cps = []
    for k, (px, py) in enumerate(_other_chips(x, y)):
      cp = pltpu.make_async_remote_copy(src_ref=q_ref.at[2 * px + py], dst_ref=o_ref.at[k], send_sem=send_sems.at[k],
                                        recv_sem=recv_sems.at[k], device_id=(px, py, c), device_id_type=MESH_ID)
      cp.start()
      cps.append(cp)
    for cp in cps:
      cp.wait()

  return pl.pallas_call(
      body, name=name, in_specs=[ANY], out_specs=ANY, out_shape=jax.ShapeDtypeStruct((3, r, c_), q.dtype),
      scratch_shapes=[pltpu.SemaphoreType.DMA((3,)), pltpu.SemaphoreType.DMA((3,))],
  )(q)


def _share_halves(v, name):
  theirs = _sibling_swap(v, name)
  c = lax.axis_index("c")
  return jnp.where(c == 0, jnp.concatenate([v, theirs]), jnp.concatenate([theirs, v]))


def _half_spec(tr, c_, pick):
  return pl.BlockSpec((None, None, tr, c_), lambda s, r, place: (s, pick(place), r, 0))


def _cast_other_half(p, place, name, tr=512):
  _, _, r, c_ = p.shape
  tr = _tile(r, tr, 16)

  def body(place_ref, p_ref, o_ref):
    o_ref[...] = _bf(p_ref[...])

  out_spec = pl.BlockSpec((None, tr, c_), lambda s, rr, place: (s, rr, 0))
  return pl.pallas_call(
      body, name=name, out_shape=jax.ShapeDtypeStruct((4, r, c_), BF16),
      grid_spec=pltpu.PrefetchScalarGridSpec(num_scalar_prefetch=1, grid=(4, r // tr),
                                             in_specs=[_half_spec(tr, c_, lambda place: 1 - place[1])], out_specs=out_spec),
      compiler_params=_params(("parallel", "parallel")),
  )(place, p)


def _add_sibling(p, x1, place, name, tr=512):
  _, _, r, c_ = p.shape
  tr = _tile(r, tr, 16)

  def body(place_ref, p_ref, x_ref, o_ref):
    o_ref[...] = _bf(p_ref[...] + x_ref[...].astype(F32))

  row = pl.BlockSpec((None, tr, c_), lambda s, rr, place: (s, rr, 0))
  return pl.pallas_call(
      body, name=name, out_shape=jax.ShapeDtypeStruct((4, r, c_), BF16),
      grid_spec=pltpu.PrefetchScalarGridSpec(num_scalar_prefetch=1, grid=(4, r // tr),
                                             in_specs=[_half_spec(tr, c_, lambda place: place[1]), row], out_specs=row),
      compiler_params=_params(("parallel", "parallel")),
  )(place, p, x1)


def _sum_own(p, x1, x3, place, name, tr=512):
  _, _, r, c_ = p.shape
  tr = _tile(r, tr, 16)

  def body(place_ref, p_ref, x1_ref, x3_ref, o_ref):
    acc = p_ref[...] + x1_ref[...].astype(F32)
    for k in range(3):
      acc = acc + x3_ref[k].astype(F32)
    o_ref[...] = acc

  return pl.pallas_call(
      body, name=name, out_shape=jax.ShapeDtypeStruct((r, c_), F32),
      grid_spec=pltpu.PrefetchScalarGridSpec(
          num_scalar_prefetch=1, grid=(r // tr,),
          in_specs=[pl.BlockSpec((None, None, tr, c_), lambda rr, place: (place[0], place[1], rr, 0)),
                    pl.BlockSpec((None, tr, c_), lambda rr, place: (place[0], rr, 0)),
                    pl.BlockSpec((3, tr, c_), lambda rr, place: (0, rr, 0))],
          out_specs=pl.BlockSpec((tr, c_), lambda rr, place: (rr, 0))),
      compiler_params=_params(("parallel",)),
  )(place, p, x1, x3)


def _reduce_scatter(p):
  x, y, c = _my_place()
  place = jnp.stack([2 * x + y, c]).astype(jnp.int32)
  x1 = _sibling_swap(_cast_other_half(p, place, "rs_cast_other_half"), "rs_sibling_swap")
  q = _add_sibling(p, x1, place, "rs_add_sibling")
  x3 = _chip_exchange(q, "rs_chip_exchange")
  mine = _sum_own(p, x1, x3, place, "rs_sum_own")
  return _share_halves(mine, "rs_share_halves")


def _adamw(w, g, m, v, name):
  shape = w.shape
  cols = shape[-1]
  rows = w.size // cols
  tr = _tile(rows, max(8, (256 * 1024) // cols // 8 * 8), 8)

  def body(w_ref, g_ref, m_ref, v_ref, d_ref, nm_ref, nv_ref):
    gv = g_ref[...]
    nm = ADAM_B1 * m_ref[...] + (1.0 - ADAM_B1) * gv
    nv = ADAM_B2 * v_ref[...] + (1.0 - ADAM_B2) * (gv * gv)
    m_hat = nm / (1.0 - ADAM_B1 ** ADAM_STEP)
    v_hat = nv / (1.0 - ADAM_B2 ** ADAM_STEP)
    d_ref[...] = -ADAM_LR * (m_hat / (jnp.sqrt(v_hat) + ADAM_EPS) + ADAM_WD * w_ref[...])
    nm_ref[...] = nm
    nv_ref[...] = nv

  spec = pl.BlockSpec((tr, cols), lambda i: (i, 0))
  sds = jax.ShapeDtypeStruct((rows, cols), F32)
  outs = pl.pallas_call(
      body, name=name, grid=(rows // tr,), in_specs=[spec] * 4, out_specs=[spec] * 3, out_shape=[sds] * 3,
      compiler_params=_params(("parallel",)),
  )(*(t.reshape(rows, cols) for t in (w, g, m, v)))
  return tuple(o.reshape(shape) for o in outs)


SHARDED = (("a_w_in", 2), ("b_w_in", 2), ("w_mem_kv", 1), ("w_out", 1), ("w_gate", 2), ("w_up", 2), ("w_down", 1))
SMALL_SHARDED = (("sgu_ln_g", 1), ("sgu_ln_b", 1))
REPLICATED = ("sgu_w_s", "sgu_b_s", "ln_mix_g", "ln_mix_b", "ln_ffn_g", "ln_ffn_b")
SMALL_ORDER = ("sgu_w_s", "sgu_b_s", "ln_mix_g", "ln_mix_b", "ln_ffn_g", "ln_ffn_b", "sgu_ln_g", "sgu_ln_b")
ROW_ALIGN = 16


def _pad_to(v, n):
  return jnp.pad(v, (0, n - v.shape[0]))


def _round_up(n, a):
  return -(-n // a) * a


def _to_shard_major(full, axis):
  shp = full.shape
  cut = shp[:axis] + (4, shp[axis] // 4) + shp[axis + 1:]
  return jnp.moveaxis(full.reshape(cut), axis, 0).reshape(4, -1, FLAT_COLS)


def _from_shard_major(rows, shard_shape, axis):
  full = jnp.moveaxis(rows.reshape((4,) + tuple(shard_shape)), 0, axis)
  shp = full.shape
  return full.reshape(shp[:axis] + (shp[axis] * shp[axis + 1],) + shp[axis + 2:])


def _gather_weights(shards):
  segs = [shards[n].astype(BF16).reshape(-1, FLAT_COLS) for n, _ in SHARDED]
  small = jnp.concatenate([lax.bitcast_convert_type(shards[n], BF16).reshape(-1) for n, _ in SMALL_SHARDED])
  small_rows = _round_up(small.shape[0], ROW_ALIGN * FLAT_COLS) // FLAT_COLS
  segs.append(_pad_to(small, small_rows * FLAT_COLS).reshape(small_rows, FLAT_COLS))
  rows = sum(sg.shape[0] for sg in segs)
  rows_pad = _round_up(rows, 2 * ROW_ALIGN)
  if rows_pad > rows:
    segs.append(jnp.zeros((rows_pad - rows, FLAT_COLS), BF16))
  flat = jnp.concatenate(segs).reshape(2, rows_pad // 2, FLAT_COLS)
  g = _all_gather_halves(flat, "gather_weights").reshape(4, rows_pad, FLAT_COLS)
  out, off = {}, 0
  for n, axis in SHARDED:
    nr = shards[n].size // FLAT_COLS
    out[n] = _from_shard_major(g[:, off:off + nr], shards[n].shape, axis)
    off += nr
  small_g = g[:, off:off + small_rows].reshape(4, small_rows * FLAT_COLS)
  off = 0
  for n, axis in SMALL_SHARDED:
    sz = 2 * shards[n].size
    vals = lax.bitcast_convert_type(small_g[:, off:off + sz].reshape((4,) + shards[n].shape + (2,)), F32)
    out[n] = _from_shard_major(vals, shards[n].shape, axis)
    off += sz
  return out


def _reduce_grads(grads, shard_shapes):
  segs = []
  for n, axis in SHARDED:
    for g in grads[n]:
      segs.append(_to_shard_major(g, axis - 1))
  big_rows = sum(sg.shape[1] for sg in segs)
  small_full = {n: jnp.stack(grads[n]) for n in SMALL_ORDER}
  small = jnp.concatenate([small_full[n].reshape(-1) for n in SMALL_ORDER])
  n_small = _round_up(small.shape[0], 4 * 2 * 8 * FLAT_COLS)
  quarter_rows = n_small // (4 * FLAT_COLS)
  segs.append(_pad_to(small, n_small).reshape(4, quarter_rows, FLAT_COLS))
  rows_pad = _round_up(big_rows + quarter_rows, 2 * ROW_ALIGN)
  if rows_pad > big_rows + quarter_rows:
    segs.append(jnp.zeros((4, rows_pad - big_rows - quarter_rows, FLAT_COLS), F32))
  p = jnp.concatenate(segs, axis=1).reshape(4, 2, rows_pad // 2, FLAT_COLS)
  mine = _reduce_scatter(p)
  out, off = {}, 0
  for n, _ in SHARDED:
    nr = math.prod(shard_shapes[n]) // FLAT_COLS
    out[n] = mine[off:off + nr].reshape(shard_shapes[n])
    off += nr
  piece = mine[big_rows:big_rows + quarter_rows].reshape(2, quarter_rows // 2, FLAT_COLS)
  small_sum = _all_gather_halves(piece, "gather_small_grads").reshape(n_small)
  off = 0
  for n in SMALL_ORDER:
    sz = small_full[n].size
    out[n] = small_sum[off:off + sz].reshape(small_full[n].shape)
    off += sz
  return out


WEIGHT_NAMES = ("a_w_in", "b_w_in", "sgu_ln_g", "sgu_ln_b", "sgu_w_s", "sgu_b_s", "w_mem_kv", "w_out",
                "ln_mix_g", "ln_mix_b", "w_gate", "w_up", "w_down", "ln_ffn_g", "ln_ffn_b")


def kernel(x, mem, a_w_in, b_w_in, sgu_ln_g, sgu_ln_b, sgu_w_s, sgu_b_s, w_mem_kv, w_out, ln_mix_g, ln_mix_b, w_gate, w_up, w_down, ln_ffn_g, ln_ffn_b, loss_target, m_a_w_in, m_b_w_in, m_sgu_ln_g, m_sgu_ln_b, m_sgu_w_s, m_sgu_b_s, m_w_mem_kv, m_w_out, m_ln_mix_g, m_ln_mix_b, m_w_gate, m_w_up, m_w_down, m_ln_ffn_g, m_ln_ffn_b, v_a_w_in, v_b_w_in, v_sgu_ln_g, v_sgu_ln_b, v_sgu_w_s, v_sgu_b_s, v_w_mem_kv, v_w_out, v_ln_mix_g, v_ln_mix_b, v_w_gate, v_w_up, v_w_down, v_ln_ffn_g, v_ln_ffn_b):
  weights = dict(a_w_in=a_w_in, b_w_in=b_w_in, sgu_ln_g=sgu_ln_g, sgu_ln_b=sgu_ln_b, sgu_w_s=sgu_w_s, sgu_b_s=sgu_b_s,
                 w_mem_kv=w_mem_kv, w_out=w_out, ln_mix_g=ln_mix_g, ln_mix_b=ln_mix_b, w_gate=w_gate, w_up=w_up,
                 w_down=w_down, ln_ffn_g=ln_ffn_g, ln_ffn_b=ln_ffn_b)
  mom1 = dict(a_w_in=m_a_w_in, b_w_in=m_b_w_in, sgu_ln_g=m_sgu_ln_g, sgu_ln_b=m_sgu_ln_b, sgu_w_s=m_sgu_w_s,
              sgu_b_s=m_sgu_b_s, w_mem_kv=m_w_mem_kv, w_out=m_w_out, ln_mix_g=m_ln_mix_g, ln_mix_b=m_ln_mix_b,
              w_gate=m_w_gate, w_up=m_w_up, w_down=m_w_down, ln_ffn_g=m_ln_ffn_g, ln_ffn_b=m_ln_ffn_b)
  mom2 = dict(a_w_in=v_a_w_in, b_w_in=v_b_w_in, sgu_ln_g=v_sgu_ln_g, sgu_ln_b=v_sgu_ln_b, sgu_w_s=v_sgu_w_s,
              sgu_b_s=v_sgu_b_s, w_mem_kv=v_w_mem_kv, w_out=v_w_out, ln_mix_g=v_ln_mix_g, ln_mix_b=v_ln_mix_b,
              w_gate=v_w_gate, w_up=v_w_up, w_down=v_w_down, ln_ffn_g=v_ln_ffn_g, ln_ffn_b=v_ln_ffn_b)

  full = _gather_weights(weights)
  for n in REPLICATED:
    full[n] = weights[n]
  loss_part, grad_x, grads = _local_step(x, mem, loss_target, full)
  loss = lax.psum(loss_part, MESH_AXES)

  shard_shapes = {n: weights[n].shape for n, _ in SHARDED}
  red = _reduce_grads(grads, shard_shapes)
  chip = 2 * lax.axis_index("x") + lax.axis_index("y")
  for n, axis in SMALL_SHARDED:
    width = weights[n].shape[axis]
    red[n] = lax.dynamic_slice_in_dim(red[n], chip * width, width, axis)

  small_names = SMALL_ORDER
  def pack(d):
    flat = jnp.concatenate([d[n].reshape(-1) for n in small_names])
    return _pad_to(flat, _round_up(flat.shape[0], 8 * FLAT_COLS)).reshape(-1, FLAT_COLS)
  small_out = _adamw(pack(weights), pack(red), pack(mom1), pack(mom2), "adamw_small")
  delta, new_m, new_v = {}, {}, {}
  off = 0
  for n in small_names:
    sz = weights[n].size
    for dst, src in zip((delta, new_m, new_v), small_out):
      dst[n] = src.reshape(-1)[off:off + sz].reshape(weights[n].shape)
    off += sz
  for n, _ in SHARDED:
    delta[n], new_m[n], new_v[n] = _adamw(weights[n], red[n], mom1[n], mom2[n], f"adamw_{n}")

  return (loss, grad_x, *[red[n] for n in WEIGHT_NAMES], *[delta[n] for n in WEIGHT_NAMES],
          *[new_m[n] for n in WEIGHT_NAMES], *[new_v[n] for n in WEIGHT_NAMES])
```

```python
import functools
import math

import jax
import jax.numpy as jnp
from jax import lax
from jax.experimental import pallas as pl
from jax.experimental.pallas import tpu as pltpu

F32 = jnp.float32
BF16 = jnp.bfloat16

DEPTH = 4
HEAD_DIM = 64
N_DIL_HEADS = 12
DIL_WIDTH = N_DIL_HEADS * HEAD_DIM
DIL_PATTERNS = ((128, 1), (512, 4), (2048, 16))
BLOCK = 128
N_SGU_GROUPS = 12
SGU_WIDTH = N_SGU_GROUPS * 64
CHUNK = 128
N_MEM_HEADS = 4
MEM_WIDTH = N_MEM_HEADS * HEAD_DIM
DN_ALPHA = (2 * DEPTH) ** 0.25
LN_EPS = 1e-5
ATT_SCALE = HEAD_DIM ** -0.5
ADAM_LR = 0.001
ADAM_B1 = 0.9
ADAM_B2 = 0.999
ADAM_EPS = 1e-08
ADAM_WD = 0.01
ADAM_STEP = 10
NEG_BIG = -1e30
ATTN_UNROLL = 2

LANES = 128
FLAT_COLS = 1024
VMEM_LIMIT = 56 * 1024 * 1024
MESH_AXES = ("x", "y", "c")
MESH_ID = pl.DeviceIdType.MESH


def _tile(n, pref, align=LANES):
  if n <= pref:
    return n
  t = (pref // align) * align
  while t >= align:
    if n % t == 0:
      return t
    t -= align
  return n


def _params(sem):
  return pltpu.CompilerParams(dimension_semantics=sem, vmem_limit_bytes=VMEM_LIMIT)


def _dot(a, b):
  return jnp.dot(a, b, preferred_element_type=F32)


def _dot_nt(a, b):
  return lax.dot_general(a, b, (((1,), (1,)), ((), ())), preferred_element_type=F32)


def _dot_tn(a, b):
  return lax.dot_general(a, b, (((0,), (0,)), ((), ())), preferred_element_type=F32)


def _bf(v):
  return v.astype(BF16)


def _ln_stats(z):
  mu = jnp.mean(z, axis=-1, keepdims=True)
  zc = z - mu
  var = jnp.mean(zc * zc, axis=-1, keepdims=True)
  rstd = lax.rsqrt(var + LN_EPS)
  return zc * rstd, rstd


def _ln_bwd(dy, xhat, rstd, g):
  gdy = dy * g
  m1 = jnp.mean(gdy, axis=-1, keepdims=True)
  m2 = jnp.mean(gdy * xhat, axis=-1, keepdims=True)
  return rstd * (gdy - m1 - xhat * m2)


_GELU_C = math.sqrt(2.0 / math.pi)


def _gelu_parts(v):
  v2 = v * v
  t = jnp.tanh(_GELU_C * (v + 0.044715 * v * v2))
  val = 0.5 * v * (1.0 + t)
  der = 0.5 * (1.0 + t) + 0.5 * v * (1.0 - t * t) * (_GELU_C * (1.0 + 3.0 * 0.044715 * v2))
  return val, der


def _gelu(v):
  t = jnp.tanh(_GELU_C * (v + 0.044715 * v * v * v))
  return 0.5 * v * (1.0 + t)


def _sigmoid(v):
  return 1.0 / (1.0 + jnp.exp(-v))


def _mm(a, b, mode, out_dtype, name, add=None, add_scale=1.0, tm=512, tn=512, tk=512):
  if mode == "nn":
    (m, k), (k2, n) = a.shape, b.shape
  elif mode == "nt":
    (m, k), (n, k2) = a.shape, b.shape
  else:
    (k, m), (k2, n) = a.shape, b.shape
  assert k == k2, (a.shape, b.shape, mode)
  tm, tn, tk = _tile(m, tm), _tile(n, tn), _tile(k, tk)
  nk = k // tk
  if mode == "nn":
    a_spec = pl.BlockSpec((tm, tk), lambda i, j, kk: (i, kk))
    b_spec = pl.BlockSpec((tk, tn), lambda i, j, kk: (kk, j))
    dot = _dot
  elif mode == "nt":
    a_spec = pl.BlockSpec((tm, tk), lambda i, j, kk: (i, kk))
    b_spec = pl.BlockSpec((tn, tk), lambda i, j, kk: (j, kk))
    dot = _dot_nt
  else:
    a_spec = pl.BlockSpec((tk, tm), lambda i, j, kk: (kk, i))
    b_spec = pl.BlockSpec((tk, tn), lambda i, j, kk: (kk, j))
    dot = _dot_tn
  o_spec = pl.BlockSpec((tm, tn), lambda i, j, kk: (i, j))
  has_add = add is not None

  def body(*refs):
    if has_add:
      a_ref, b_ref, add_ref, o_ref, acc_ref = refs
    else:
      a_ref, b_ref, o_ref, acc_ref = refs
    kk = pl.program_id(2)

    @pl.when(kk == 0)
    def _():
      acc_ref[...] = jnp.zeros_like(acc_ref)

    acc_ref[...] += dot(_bf(a_ref[...]), _bf(b_ref[...]))

    @pl.when(kk == nk - 1)
    def _():
      r = acc_ref[...]
      if has_add:
        r = r + add_scale * add_ref[...].astype(F32)
      o_ref[...] = r.astype(out_dtype)

  in_specs = [a_spec, b_spec] + ([o_spec] if has_add else [])
  args = (a, b) + ((add,) if has_add else ())
  return pl.pallas_call(
      body, name=name, grid=(m // tm, n // tn, nk), in_specs=in_specs, out_specs=o_spec,
      out_shape=jax.ShapeDtypeStruct((m, n), out_dtype),
      scratch_shapes=[pltpu.VMEM((tm, tn), F32)],
      compiler_params=_params(("parallel", "parallel", "arbitrary")),
  )(*args)


def _mm_res_ln(a, w, res, g, b, name, tm=512, tk=512):
  m, k = a.shape
  d = w.shape[1]
  tm, tk = _tile(m, tm), _tile(k, tk)
  nk = k // tk

  def body(a_ref, w_ref, r_ref, g_ref, b_ref, z_ref, x_ref, xb_ref, acc_ref):
    kk = pl.program_id(1)

    @pl.when(kk == 0)
    def _():
      acc_ref[...] = jnp.zeros_like(acc_ref)

    acc_ref[...] += _dot(_bf(a_ref[...]), _bf(w_ref[...]))

    @pl.when(kk == nk - 1)
    def _():
      z = DN_ALPHA * r_ref[...] + acc_ref[...]
      xhat, _ = _ln_stats(z)
      xn = xhat * g_ref[...] + b_ref[...]
      z_ref[...] = z
      x_ref[...] = xn
      xb_ref[...] = _bf(xn)

  row = pl.BlockSpec((tm, d), lambda i, kk: (i, 0))
  vec = pl.BlockSpec((1, d), lambda i, kk: (0, 0))
  return pl.pallas_call(
      body, name=name, grid=(m // tm, nk),
      in_specs=[pl.BlockSpec((tm, tk), lambda i, kk: (i, kk)), pl.BlockSpec((tk, d), lambda i, kk: (kk, 0)), row, vec, vec],
      out_specs=[row, row, row],
      out_shape=[jax.ShapeDtypeStruct((m, d), F32), jax.ShapeDtypeStruct((m, d), F32), jax.ShapeDtypeStruct((m, d), BF16)],
      scratch_shapes=[pltpu.VMEM((tm, d), F32)],
      compiler_params=_params(("parallel", "arbitrary")),
  )(a, w, res, g.reshape(1, d), b.reshape(1, d))


def _ln_bwd_call(dy, z, g, name, tm=512):
  m, d = z.shape
  tm = _tile(m, tm)
  n = m // tm

  def body(dy_ref, z_ref, g_ref, dz_ref, dzb_ref, dg_ref, db_ref):
    i = pl.program_id(0)

    @pl.when(i == 0)
    def _():
      dg_ref[...] = jnp.zeros_like(dg_ref)
      db_ref[...] = jnp.zeros_like(db_ref)

    dy_v = dy_ref[...]
    xhat, rstd = _ln_stats(z_ref[...])
    dz = _ln_bwd(dy_v, xhat, rstd, g_ref[...])
    dz_ref[...] = dz
    dzb_ref[...] = _bf(dz)
    dg_ref[...] += jnp.sum(dy_v * xhat, axis=0, keepdims=True)
    db_ref[...] += jnp.sum(dy_v, axis=0, keepdims=True)

  row = pl.BlockSpec((tm, d), lambda i: (i, 0))
  vec = pl.BlockSpec((1, d), lambda i: (0, 0))
  dz, dzb, dg, db = pl.pallas_call(
      body, name=name, grid=(n,), in_specs=[row, row, vec], out_specs=[row, row, vec, vec],
      out_shape=[jax.ShapeDtypeStruct((m, d), F32), jax.ShapeDtypeStruct((m, d), BF16),
                 jax.ShapeDtypeStruct((1, d), F32), jax.ShapeDtypeStruct((1, d), F32)],
      compiler_params=_params(("arbitrary",)),
  )(dy, z, g.reshape(1, d))
  return dz, dzb, dg[0], db[0]


def _ffn_up(xb, wg, wu, name, tm=512, tn=1408):
  m, d = xb.shape
  f = wg.shape[1]
  tm, tn = _tile(m, tm), _tile(f, tn)

  def body(x_ref, wg_ref, wu_ref, a_ref, b_ref, h_ref):
    xv = x_ref[...]
    a = _dot(xv, wg_ref[...])
    b = _dot(xv, wu_ref[...])
    a_ref[...] = _bf(a)
    b_ref[...] = _bf(b)
    h_ref[...] = _bf(a * _sigmoid(a) * b)

  wspec = pl.BlockSpec((d, tn), lambda j, i: (0, j))
  ospec = pl.BlockSpec((tm, tn), lambda j, i: (i, j))
  sds = jax.ShapeDtypeStruct((m, f), BF16)
  return pl.pallas_call(
      body, name=name, grid=(f // tn, m // tm),
      in_specs=[pl.BlockSpec((tm, d), lambda j, i: (i, 0)), wspec, wspec],
      out_specs=[ospec, ospec, ospec], out_shape=[sds, sds, sds],
      compiler_params=_params(("parallel", "parallel")),
  )(xb, wg, wu)


def _ffn_bwd_hidden(dzb, wd, a, b, name, tm=512, tn=1408):
  m, d = dzb.shape
  f = wd.shape[0]
  tm, tn = _tile(m, tm), _tile(f, tn)

  def body(dz_ref, wd_ref, a_ref, b_ref, da_ref, db_ref):
    dh = _dot_nt(dz_ref[...], wd_ref[...])
    av = a_ref[...].astype(F32)
    bv = b_ref[...].astype(F32)
    sg = _sigmoid(av)
    da_ref[...] = _bf(dh * bv * (sg * (1.0 + av * (1.0 - sg))))
    db_ref[...] = _bf(dh * (av * sg))

  hspec = pl.BlockSpec((tm, tn), lambda j, i: (i, j))
  sds = jax.ShapeDtypeStruct((m, f), BF16)
  return pl.pallas_call(
      body, name=name, grid=(f // tn, m // tm),
      in_specs=[pl.BlockSpec((tm, d), lambda j, i: (i, 0)), pl.BlockSpec((tn, d), lambda j, i: (j, 0)), hspec, hspec],
      out_specs=[hspec, hspec], out_shape=[sds, sds],
      compiler_params=_params(("parallel", "parallel")),
  )(dzb, wd, a, b)


def _ln_bwd_tail(dy, z_ref, g_ref, dz_ref, dzb_ref, dg_ref, db_ref):
  @pl.when(pl.program_id(0) == 0)
  def _():
    dg_ref[...] = jnp.zeros_like(dg_ref)
    db_ref[...] = jnp.zeros_like(db_ref)

  xhat, rstd = _ln_stats(z_ref[...])
  dz = _ln_bwd(dy, xhat, rstd, g_ref[...])
  dz_ref[...] = dz
  dzb_ref[...] = _bf(dz)
  dg_ref[...] += jnp.sum(dy * xhat, axis=0, keepdims=True)
  db_ref[...] += jnp.sum(dy, axis=0, keepdims=True)


def _ln_bwd_outs(m, d, row, vec):
  return ([row, row, vec, vec],
          [jax.ShapeDtypeStruct((m, d), F32), jax.ShapeDtypeStruct((m, d), BF16),
           jax.ShapeDtypeStruct((1, d), F32), jax.ShapeDtypeStruct((1, d), F32)])


def _ffn_bwd_input_ln(da, db, wg, wu, dz2, z1, g, name, tm=512):
  m, f = da.shape
  d = wg.shape[0]
  tm = _tile(m, tm)

  def body(da_ref, db_ref, wg_ref, wu_ref, dz2_ref, z_ref, g_ref, dz_ref, dzb_ref, dg_ref, dbias_ref):
    dy = DN_ALPHA * dz2_ref[...] + _dot_nt(da_ref[...], wg_ref[...]) + _dot_nt(db_ref[...], wu_ref[...])
    _ln_bwd_tail(dy, z_ref, g_ref, dz_ref, dzb_ref, dg_ref, dbias_ref)

  hspec = pl.BlockSpec((tm, f), lambda i: (i, 0))
  wspec = pl.BlockSpec((d, f), lambda i: (0, 0), pipeline_mode=pl.Buffered(1))
  row = pl.BlockSpec((tm, d), lambda i: (i, 0))
  vec = pl.BlockSpec((1, d), lambda i: (0, 0))
  out_specs, out_shape = _ln_bwd_outs(m, d, row, vec)
  dz, dzb, dg, dbias = pl.pallas_call(
      body, name=name, grid=(m // tm,), in_specs=[hspec, hspec, wspec, wspec, row, row, vec],
      out_specs=out_specs, out_shape=out_shape, compiler_params=_params(("arbitrary",)),
  )(da, db, wg, wu, dz2, z1, g.reshape(1, d))
  return dz, dzb, dg[0], dbias[0]


def _in_proj_bwd_ln(dh, w_in, dz1, z2, g, name, tm=512):
  m, wd = dh.shape
  d = w_in.shape[0]
  tm = _tile(m, tm)

  def body(dh_ref, w_ref, dz1_ref, z_ref, g_ref, dz_ref, dzb_ref, dg_ref, dbias_ref):
    dy = DN_ALPHA * dz1_ref[...] + _dot_nt(dh_ref[...], w_ref[...])
    _ln_bwd_tail(dy, z_ref, g_ref, dz_ref, dzb_ref, dg_ref, dbias_ref)

  row = pl.BlockSpec((tm, d), lambda i: (i, 0))
  vec = pl.BlockSpec((1, d), lambda i: (0, 0))
  out_specs, out_shape = _ln_bwd_outs(m, d, row, vec)
  dz, dzb, dg, dbias = pl.pallas_call(
      body, name=name, grid=(m // tm,),
      in_specs=[pl.BlockSpec((tm, wd), lambda i: (i, 0)),
                pl.BlockSpec((d, wd), lambda i: (0, 0), pipeline_mode=pl.Buffered(1)), row, row, vec],
      out_specs=out_specs, out_shape=out_shape, compiler_params=_params(("arbitrary",)),
  )(dh, w_in, dz1, z2, g.reshape(1, d))
  return dz, dzb, dg[0], dbias[0]


def _alibi_slopes():
  n = N_DIL_HEADS
  return jnp.exp2(-8.0 * (jnp.arange(n, dtype=F32) + 1.0) / n).reshape(1, n)


def _band_consts():
  qi = lax.broadcasted_iota(jnp.int32, (BLOCK, BLOCK), 0)
  ki = lax.broadcasted_iota(jnp.int32, (BLOCK, BLOCK), 1)
  steps_cur = (qi - ki).astype(F32)
  steps_prev = (qi + BLOCK - ki).astype(F32)
  return ki < 64, steps_cur, steps_prev, ki <= qi, ki >= qi


def _rows(start, d):
  if d == 1:
    return pl.ds(pl.multiple_of(start, BLOCK), BLOCK)
  return pl.ds(start, BLOCK, stride=d)


def _fill_bias_tables(bias_sc, slope0, slope1):
  row = lax.broadcasted_iota(jnp.int32, (2 * BLOCK, 2 * BLOCK), 0)
  col = lax.broadcasted_iota(jnp.int32, (2 * BLOCK, 2 * BLOCK), 1)
  qi = jnp.bitwise_and(row, BLOCK - 1)
  ki = jnp.bitwise_and(col, BLOCK - 1)
  is_cur = col >= BLOCK
  steps = jnp.where(is_cur, qi - ki, qi + BLOCK - ki)
  valid = jnp.logical_and(steps >= 0, steps <= BLOCK)
  slope = jnp.where(row >= BLOCK, slope1, slope0)
  dist = slope * steps.astype(F32)
  for p, (_, d) in enumerate(DIL_PATTERNS):
    base = jnp.where(valid, -d * dist, NEG_BIG)
    bias_sc[2 * p] = base
    bias_sc[2 * p + 1] = jnp.where(is_cur, base, NEG_BIG)


def _stack_heads(v2, head0):
  return jnp.concatenate([jnp.where(head0, v2, 0.0), jnp.where(head0, 0.0, v2)], axis=0)


def _unstack_heads(v, head0):
  return jnp.where(head0, v[:BLOCK], v[BLOCK:])


def _block_rows(idx, d, nblk):
  r = idx // nblk
  n = idx % nblk
  cur = _rows(r + n * (BLOCK * d), d)
  prev = _rows(r + jnp.maximum(n - 1, 0) * (BLOCK * d), d)
  return cur, prev, n


def pair_tile(dt):
  return pltpu.VMEM((2 * BLOCK, 2 * BLOCK), dt)


def _two_stage_loop(nb, first_stage, second_stage, buf_a, buf_b):
  assert nb % 2 == 0

  def pair(t, carry):
    i = 2 * t + 1
    first_stage(i, buf_b)
    second_stage(i - 1, buf_a)
    first_stage(i + 1, buf_a)
    second_stage(i, buf_b)
    return carry

  first_stage(0, buf_a)
  lax.fori_loop(0, nb // 2 - 1, pair, 0)
  first_stage(nb - 1, buf_b)
  second_stage(nb - 2, buf_a)
  second_stage(nb - 1, buf_b)


def _attn_fwd(h3, name):
  bl, s, _ = h3.shape
  npair = N_DIL_HEADS // 2
  nb = s // BLOCK

  def body(sl_ref, q_ref, k_ref, v_ref, o_ref, lse_ref, o_sc, l_sc, bias_sc, s_a, s_b):
    hp = pl.program_id(1)
    head0 = lax.broadcasted_iota(jnp.int32, (BLOCK, LANES), 1) < 64
    _fill_bias_tables(bias_sc, sl_ref[0, 2 * hp], sl_ref[0, 2 * hp + 1])

    for p, (_, d) in enumerate(DIL_PATTERNS):
      nblk = (s // d) // BLOCK
      two = nblk > 1
      ks = slice(0, 2 * BLOCK) if two else slice(BLOCK, 2 * BLOCK)

      def scores(idx, buf, p=p, d=d, nblk=nblk, two=two, ks=ks):
        cur, prev, n = _block_rows(idx, d, nblk)
        qs = _bf(_stack_heads(q_ref[cur, :], head0) * ATT_SCALE)
        kb = _bf(jnp.concatenate([k_ref[prev, :], k_ref[cur, :]], axis=0)) if two else _bf(k_ref[cur, :])
        first = jnp.where(n == 0, 1, 0) if two else 0
        buf[:, ks] = _dot_nt(qs, kb) + bias_sc[2 * p + first, :, ks]

      def values(idx, buf, p=p, d=d, nblk=nblk, two=two, ks=ks):
        cur, prev, _ = _block_rows(idx, d, nblk)
        sc = buf[:, ks]
        mx = jnp.max(sc, axis=1, keepdims=True)
        pe = jnp.exp(sc - mx)
        den = jnp.sum(pe, axis=1, keepdims=True)
        vb = _bf(jnp.concatenate([v_ref[prev, :], v_ref[cur, :]], axis=0)) if two else _bf(v_ref[cur, :])
        acc = _dot(_bf(pe), vb) / den
        o_sc[p, cur, :] = _unstack_heads(acc, head0)
        l_sc[p, cur, :] = _unstack_heads(jnp.broadcast_to(mx + jnp.log(den), (2 * BLOCK, LANES)), head0)

      _two_stage_loop(nb, scores, values, s_a, s_b)

    def merge(i, carry):
      rows = pl.ds(pl.multiple_of(i * BLOCK, BLOCK), BLOCK)
      l0, l1, l2 = l_sc[0, rows, :], l_sc[1, rows, :], l_sc[2, rows, :]
      mx = jnp.maximum(jnp.maximum(l0, l1), l2)
      e0, e1, e2 = jnp.exp(l0 - mx), jnp.exp(l1 - mx), jnp.exp(l2 - mx)
      tot = e0 + e1 + e2
      o_ref[rows, :] = _bf((e0 * o_sc[0, rows, :] + e1 * o_sc[1, rows, :] + e2 * o_sc[2, rows, :]) / tot)
      lse_ref[rows, :] = mx + jnp.log(tot)
      return carry

    lax.fori_loop(0, nb, merge, 0)

  def col(off):
    return pl.BlockSpec((None, s, LANES), lambda b, p: (b, 0, off + p))

  return pl.pallas_call(
      body, name=name, grid=(bl, npair),
      in_specs=[pl.BlockSpec(memory_space=pltpu.SMEM), col(0), col(npair), col(2 * npair)],
      out_specs=[col(0), col(0)],
      out_shape=[jax.ShapeDtypeStruct((bl, s, DIL_WIDTH), BF16), jax.ShapeDtypeStruct((bl, s, DIL_WIDTH), F32)],
      scratch_shapes=[pltpu.VMEM((3, s, LANES), F32), pltpu.VMEM((3, s, LANES), F32),
                      pltpu.VMEM((6, 2 * BLOCK, 2 * BLOCK), F32), pair_tile(F32), pair_tile(F32)],
      compiler_params=_params(("parallel", "parallel")),
  )(_alibi_slopes(), h3, h3, h3)


def _attn_bwd(h3, out3, lse3, dcat3, name):
  bl, s, _ = h3.shape
  npair = N_DIL_HEADS // 2
  nb = s // BLOCK

  def body(sl_ref, q_ref, k_ref, v_ref, o_ref, l_ref, do_ref, dq_out, dk_out, dv_out,
           bias_sc, p_a, ds_a, p_b, ds_b, prod_sc, dq_ref, dk_ref, dv_ref):
    hp = pl.program_id(1)
    lane = lax.broadcasted_iota(jnp.int32, (BLOCK, LANES), 1)
    head0 = lane < 64
    _fill_bias_tables(bias_sc, sl_ref[0, 2 * hp], sl_ref[0, 2 * hp + 1])
    dq_ref[...] = jnp.zeros_like(dq_ref)
    dk_ref[...] = jnp.zeros_like(dk_ref)
    dv_ref[...] = jnp.zeros_like(dv_ref)
    prod_sc[...] = do_ref[...] * o_ref[...].astype(F32)

    def per_row(v2, pick0, pick1):
      return jnp.concatenate([jnp.sum(jnp.where(pick0, v2, 0.0), axis=1, keepdims=True),
                              jnp.sum(jnp.where(pick1, v2, 0.0), axis=1, keepdims=True)], axis=0)

    for p, (_, d) in enumerate(DIL_PATTERNS):
      nblk = (s // d) // BLOCK
      two = nblk > 1
      ks = slice(0, 2 * BLOCK) if two else slice(BLOCK, 2 * BLOCK)

      def operands(idx, d=d, nblk=nblk, two=two):
        cur, prev, n = _block_rows(idx, d, nblk)
        qs = _bf(_stack_heads(q_ref[cur, :], head0) * ATT_SCALE)
        dos = _bf(_stack_heads(do_ref[cur, :], head0))
        kb = _bf(jnp.concatenate([k_ref[prev, :], k_ref[cur, :]], axis=0)) if two else _bf(k_ref[cur, :])
        return cur, prev, n, qs, dos, kb

      def probs(idx, bufs, p=p, two=two, ks=ks, operands=operands):
        cur, prev, n, qs, dos, kb = operands(idx)
        vb = _bf(jnp.concatenate([v_ref[prev, :], v_ref[cur, :]], axis=0)) if two else _bf(v_ref[cur, :])
        lse = per_row(l_ref[cur, :], lane == 0, lane == 64)
        delta = per_row(prod_sc[cur, :], head0, jnp.logical_not(head0))
        first = jnp.where(n == 0, 1, 0) if two else 0
        pr = jnp.exp(_dot_nt(qs, kb) + bias_sc[2 * p + first, :, ks] - lse)
        bufs[0][:, ks] = _bf(pr)
        bufs[1][:, ks] = _bf(pr * (_dot_nt(dos, vb) - delta))

      def products(idx, bufs, two=two, ks=ks, operands=operands):
        cur, prev, _, qs, dos, kb = operands(idx)
        pr = bufs[0][:, ks]
        ds = bufs[1][:, ks]
        dq_ref[cur, :] += _unstack_heads(_dot(ds, kb), head0) * ATT_SCALE
        dkb = _dot_tn(ds, qs)
        dvb = _dot_tn(pr, dos)
        if two:
          dk_ref[prev, :] += dkb[:BLOCK]
          dv_ref[prev, :] += dvb[:BLOCK]
          dk_ref[cur, :] += dkb[BLOCK:]
          dv_ref[cur, :] += dvb[BLOCK:]
        else:
          dk_ref[cur, :] += dkb
          dv_ref[cur, :] += dvb

      _two_stage_loop(nb, probs, products, (p_a, ds_a), (p_b, ds_b))

    dq_out[...] = _bf(dq_ref[...])
    dk_out[...] = _bf(dk_ref[...])
    dv_out[...] = _bf(dv_ref[...])

  def col(off):
    return pl.BlockSpec((None, s, LANES), lambda b, p: (b, 0, off + p))

  sds = jax.ShapeDtypeStruct((bl, s, DIL_WIDTH), BF16)
  return pl.pallas_call(
      body, name=name, grid=(bl, npair),
      in_specs=[pl.BlockSpec(memory_space=pltpu.SMEM), col(0), col(npair), col(2 * npair), col(0), col(0), col(0)],
      out_specs=[col(0), col(0), col(0)], out_shape=[sds, sds, sds],
      scratch_shapes=[pltpu.VMEM((6, 2 * BLOCK, 2 * BLOCK), F32)] + [pair_tile(BF16)] * 4
      + [pltpu.VMEM((s, LANES), F32)] * 4,
      compiler_params=_params(("parallel", "parallel")),
  )(_alibi_slopes(), h3, h3, h3, out3, lse3, dcat3)


def _attn_fwd_old(h3, name):
  bl, s, _ = h3.shape
  npair = N_DIL_HEADS // 2

  def body(sl_ref, q_ref, k_ref, v_ref, o_ref, lse_ref, o_sc, l_sc):
    hp = pl.program_id(1)
    head0, steps_cur, steps_prev, mask_cur, mask_prev = _band_consts()
    slope = [sl_ref[0, 2 * hp], sl_ref[0, 2 * hp + 1]]

    for p, (_, d) in enumerate(DIL_PATTERNS):
      nblk = (s // d) // BLOCK
      has_prev_block = nblk > 1

      def blk(idx, carry, p=p, d=d, nblk=nblk, has_prev_block=has_prev_block):
        r = idx // nblk
        n = idx % nblk
        cur = _rows(r + n * (BLOCK * d), d)
        q2 = q_ref[cur, :]
        kc = _bf(k_ref[cur, :])
        vc = _bf(v_ref[cur, :])
        if has_prev_block:
          prev = _rows(r + jnp.maximum(n - 1, 0) * (BLOCK * d), d)
          kp = _bf(k_ref[prev, :])
          vp = _bf(v_ref[prev, :])
          first_block = jnp.where(n > 0, 0.0, NEG_BIG)
        outs, lses = [], []
        for j in range(2):
          hm = head0 if j == 0 else jnp.logical_not(head0)
          qj = _bf(jnp.where(hm, q2, 0.0) * ATT_SCALE)
          sc = _dot_nt(qj, kc) - (slope[j] * d) * steps_cur
          sc = jnp.where(mask_cur, sc, NEG_BIG)
          mx = jnp.max(sc, axis=1, keepdims=True)
          if has_prev_block:
            sp = _dot_nt(qj, kp) - (slope[j] * d) * steps_prev + first_block
            sp = jnp.where(mask_prev, sp, NEG_BIG)
            mx = jnp.maximum(mx, jnp.max(sp, axis=1, keepdims=True))
          pc = jnp.exp(sc - mx)
          den = jnp.sum(pc, axis=1, keepdims=True)
          acc = _dot(_bf(pc), vc)
          if has_prev_block:
            pp = jnp.exp(sp - mx)
            den = den + jnp.sum(pp, axis=1, keepdims=True)
            acc = acc + _dot(_bf(pp), vp)
          outs.append(acc / den)
          lses.append(mx + jnp.log(den))
        o_sc[p, cur, :] = jnp.where(head0, outs[0], outs[1])
        l_sc[p, cur, :] = jnp.where(head0, lses[0], lses[1])
        return carry

      lax.fori_loop(0, s // BLOCK, blk, 0, unroll=ATTN_UNROLL)

    def merge(i, carry):
      rows = pl.ds(pl.multiple_of(i * BLOCK, BLOCK), BLOCK)
      l0, l1, l2 = l_sc[0, rows, :], l_sc[1, rows, :], l_sc[2, rows, :]
      mx = jnp.maximum(jnp.maximum(l0, l1), l2)
      e0, e1, e2 = jnp.exp(l0 - mx), jnp.exp(l1 - mx), jnp.exp(l2 - mx)
      tot = e0 + e1 + e2
      o_ref[rows, :] = (e0 * o_sc[0, rows, :] + e1 * o_sc[1, rows, :] + e2 * o_sc[2, rows, :]) / tot
      lse_ref[rows, :] = mx + jnp.log(tot)
      return carry

    lax.fori_loop(0, s // BLOCK, merge, 0)

  def col(off):
    return pl.BlockSpec((None, s, LANES), lambda b, p: (b, 0, off + p))

  sds = jax.ShapeDtypeStruct((bl, s, DIL_WIDTH), F32)
  return pl.pallas_call(
      body, name=name, grid=(bl, npair),
      in_specs=[pl.BlockSpec(memory_space=pltpu.SMEM), col(0), col(npair), col(2 * npair)],
      out_specs=[col(0), col(0)], out_shape=[sds, sds],
      scratch_shapes=[pltpu.VMEM((3, s, LANES), F32), pltpu.VMEM((3, s, LANES), F32)],
      compiler_params=_params(("parallel", "parallel")),
  )(_alibi_slopes(), h3, h3, h3)


def _attn_bwd_old(h3, out3, lse3, dcat3, name):
  bl, s, _ = h3.shape
  npair = N_DIL_HEADS // 2

  def body(sl_ref, q_ref, k_ref, v_ref, o_ref, l_ref, do_ref, dq_ref, dk_ref, dv_ref):
    hp = pl.program_id(1)
    head0, steps_cur, steps_prev, mask_cur, mask_prev = _band_consts()
    lane = lax.broadcasted_iota(jnp.int32, (BLOCK, LANES), 1)
    slope = [sl_ref[0, 2 * hp], sl_ref[0, 2 * hp + 1]]
    dq_ref[...] = jnp.zeros_like(dq_ref)
    dk_ref[...] = jnp.zeros_like(dk_ref)
    dv_ref[...] = jnp.zeros_like(dv_ref)

    for p, (_, d) in enumerate(DIL_PATTERNS):
      nblk = (s // d) // BLOCK
      has_prev_block = nblk > 1

      def blk(idx, carry, d=d, nblk=nblk, has_prev_block=has_prev_block):
        r = idx // nblk
        n = idx % nblk
        cur = _rows(r + n * (BLOCK * d), d)
        q2 = q_ref[cur, :]
        do2 = do_ref[cur, :]
        l2 = l_ref[cur, :]
        prod = do2 * o_ref[cur, :]
        kc = _bf(k_ref[cur, :])
        vc = _bf(v_ref[cur, :])
        if has_prev_block:
          prev = _rows(r + jnp.maximum(n - 1, 0) * (BLOCK * d), d)
          kp = _bf(k_ref[prev, :])
          vp = _bf(v_ref[prev, :])
          first_block = jnp.where(n > 0, 0.0, NEG_BIG)
          dkp = jnp.zeros((BLOCK, LANES), F32)
          dvp = jnp.zeros((BLOCK, LANES), F32)
        dq2 = jnp.zeros((BLOCK, LANES), F32)
        dkc = jnp.zeros((BLOCK, LANES), F32)
        dvc = jnp.zeros((BLOCK, LANES), F32)
        for j in range(2):
          hm = head0 if j == 0 else jnp.logical_not(head0)
          qj = _bf(jnp.where(hm, q2, 0.0) * ATT_SCALE)
          doj = _bf(jnp.where(hm, do2, 0.0))
          lj = jnp.sum(jnp.where(lane == 64 * j, l2, 0.0), axis=1, keepdims=True)
          dj = jnp.sum(jnp.where(hm, prod, 0.0), axis=1, keepdims=True)
          sc = _dot_nt(qj, kc) - (slope[j] * d) * steps_cur
          pc = jnp.exp(jnp.where(mask_cur, sc - lj, NEG_BIG))
          dsc = _bf(pc * (_dot_nt(doj, vc) - dj))
          dq_j = _dot(dsc, kc)
          dkc = dkc + _dot_tn(dsc, qj)
          dvc = dvc + _dot_tn(_bf(pc), doj)
          if has_prev_block:
            sp = _dot_nt(qj, kp) - (slope[j] * d) * steps_prev + first_block
            pp = jnp.exp(jnp.where(mask_prev, sp - lj, NEG_BIG))
            dsp = _bf(pp * (_dot_nt(doj, vp) - dj))
            dq_j = dq_j + _dot(dsp, kp)
            dkp = dkp + _dot_tn(dsp, qj)
            dvp = dvp + _dot_tn(_bf(pp), doj)
          dq2 = dq2 + jnp.where(hm, dq_j, 0.0) * ATT_SCALE
        dq_ref[cur, :] += dq2
        dk_ref[cur, :] += dkc
        dv_ref[cur, :] += dvc
        if has_prev_block:
          dk_ref[prev, :] += dkp
          dv_ref[prev, :] += dvp
        return carry

      lax.fori_loop(0, s // BLOCK, blk, 0, unroll=ATTN_UNROLL)

  def col(off):
    return pl.BlockSpec((None, s, LANES), lambda b, p: (b, 0, off + p))

  sds = jax.ShapeDtypeStruct((bl, s, DIL_WIDTH), F32)
  return pl.pallas_call(
      body, name=name, grid=(bl, npair),
      in_specs=[pl.BlockSpec(memory_space=pltpu.SMEM), col(0), col(npair), col(2 * npair), col(0), col(0), col(0)],
      out_specs=[col(0), col(0), col(0)], out_shape=[sds, sds, sds],
      compiler_params=_params(("parallel", "parallel")),
  )(_alibi_slopes(), h3, h3, h3, out3, lse3, dcat3)


def _mem_heads(tq):
  lane = lax.broadcasted_iota(jnp.int32, (tq, LANES), 1)
  return lane < 64


def _mem_fwd(h3, qcol, mkv3, name, tq=512):
  bl, s, _ = h3.shape
  nm = mkv3.shape[1]
  tq = _tile(s, tq)

  def body(q_ref, kv_ref, o_ref):
    head0 = _mem_heads(tq)
    for lg in range(MEM_WIDTH // LANES):
      cs = slice(lg * LANES, (lg + 1) * LANES)
      q2 = q_ref[:, cs]
      mk = _bf(kv_ref[:, cs])
      mv = _bf(kv_ref[:, MEM_WIDTH + lg * LANES:MEM_WIDTH + (lg + 1) * LANES])
      outs = []
      for j in range(2):
        hm = head0 if j == 0 else jnp.logical_not(head0)
        qj = _bf(jnp.where(hm, q2, 0.0) * ATT_SCALE)
        sc = _dot_nt(qj, mk)
        mx = jnp.max(sc, axis=1, keepdims=True)
        pe = jnp.exp(sc - mx)
        den = jnp.sum(pe, axis=1, keepdims=True)
        outs.append(_dot(_bf(pe / den), mv))
      o_ref[:, cs] = _bf(jnp.where(head0, outs[0], outs[1]))

  return pl.pallas_call(
      body, name=name, grid=(bl, s // tq),
      in_specs=[pl.BlockSpec((None, tq, MEM_WIDTH), lambda b, i: (b, i, qcol)),
                pl.BlockSpec((None, nm, 2 * MEM_WIDTH), lambda b, i: (b, 0, 0))],
      out_specs=pl.BlockSpec((None, tq, MEM_WIDTH), lambda b, i: (b, i, 0)),
      out_shape=jax.ShapeDtypeStruct((bl, s, MEM_WIDTH), BF16),
      compiler_params=_params(("parallel", "parallel")),
  )(h3, mkv3)


def _mem_bwd(h3, qcol, mkv3, dcat3, name, tq=512):
  bl, s, _ = h3.shape
  nm = mkv3.shape[1]
  tq = _tile(s, tq)
  docol = dcat3.shape[2] // MEM_WIDTH - 1

  def body(q_ref, kv_ref, do_ref, dq_ref, dkv_ref):
    i = pl.program_id(1)

    @pl.when(i == 0)
    def _():
      dkv_ref[...] = jnp.zeros_like(dkv_ref)

    head0 = _mem_heads(tq)
    for lg in range(MEM_WIDTH // LANES):
      cs = slice(lg * LANES, (lg + 1) * LANES)
      vs = slice(MEM_WIDTH + lg * LANES, MEM_WIDTH + (lg + 1) * LANES)
      q2 = q_ref[:, cs]
      do2 = do_ref[:, cs]
      mk = _bf(kv_ref[:, cs])
      mv = _bf(kv_ref[:, vs])
      dq2 = jnp.zeros((tq, LANES), F32)
      dmk = jnp.zeros((nm, LANES), F32)
      dmv = jnp.zeros((nm, LANES), F32)
      for j in range(2):
        hm = head0 if j == 0 else jnp.logical_not(head0)
        qj = _bf(jnp.where(hm, q2, 0.0) * ATT_SCALE)
        doj = _bf(jnp.where(hm, do2, 0.0))
        sc = _dot_nt(qj, mk)
        mx = jnp.max(sc, axis=1, keepdims=True)
        pe = jnp.exp(sc - mx)
        pn = pe / jnp.sum(pe, axis=1, keepdims=True)
        pb = _bf(pn)
        dp = _dot_nt(doj, mv)
        dj = jnp.sum(pb.astype(F32) * dp, axis=1, keepdims=True)
        ds = _bf(pn * (dp - dj))
        dq2 = dq2 + jnp.where(hm, _dot(ds, mk), 0.0) * ATT_SCALE
        dmk = dmk + _dot_tn(ds, qj)
        dmv = dmv + _dot_tn(pb, doj)
      dq_ref[:, cs] = _bf(dq2)
      dkv_ref[:, cs] += dmk
      dkv_ref[:, vs] += dmv

  return pl.pallas_call(
      body, name=name, grid=(bl, s // tq),
      in_specs=[pl.BlockSpec((None, tq, MEM_WIDTH), lambda b, i: (b, i, qcol)),
                pl.BlockSpec((None, nm, 2 * MEM_WIDTH), lambda b, i: (b, 0, 0)),
                pl.BlockSpec((None, tq, MEM_WIDTH), lambda b, i: (b, i, docol))],
      out_specs=[pl.BlockSpec((None, tq, MEM_WIDTH), lambda b, i: (b, i, 0)),
                 pl.BlockSpec((None, nm, 2 * MEM_WIDTH), lambda b, i: (b, 0, 0))],
      out_shape=[jax.ShapeDtypeStruct((bl, s, MEM_WIDTH), BF16), jax.ShapeDtypeStruct((bl, nm, 2 * MEM_WIDTH), F32)],
      compiler_params=_params(("parallel", "arbitrary")),
  )(h3, mkv3, dcat3)


def _sgu_consts():
  ti = lax.broadcasted_iota(jnp.int32, (CHUNK, CHUNK), 0)
  si = lax.broadcasted_iota(jnp.int32, (CHUNK, CHUNK), 1)
  return si <= ti, si < 64


def _sgu_bias_lanes(b_s):
  return jnp.repeat(b_s.T, 64, axis=1)


def _sgu_fwd(h2, ln_g, ln_b, w_s, b_s, name, tr=512):
  t, _ = h2.shape
  tr = _tile(t, tr)
  nch = tr // CHUNK
  npair = N_SGU_GROUPS // 2

  def body(u_ref, v_ref, g_ref, b_ref, w_ref, bs_ref, o_ref, vn_sc):
    tril, head0 = _sgu_consts()
    xhat, _ = _ln_stats(_gelu(v_ref[...]))
    vn_sc[...] = _bf(xhat * g_ref[...] + b_ref[...])
    for jp in range(npair):
      cs = slice(jp * LANES, (jp + 1) * LANES)
      w0 = _bf(jnp.where(tril, w_ref[2 * jp], 0.0))
      w1 = _bf(jnp.where(tril, w_ref[2 * jp + 1], 0.0))
      bias = bs_ref[:, cs]
      for c in range(nch):
        rs = slice(c * CHUNK, (c + 1) * CHUNK)
        vb = vn_sc[rs, cs]
        mixed = jnp.where(head0, _dot(w0, vb), _dot(w1, vb)) + bias
        o_ref[rs, cs] = _bf(_gelu(u_ref[rs, cs]) * mixed)

  blk = lambda j: pl.BlockSpec((tr, SGU_WIDTH), lambda i: (i, j))
  vec = pl.BlockSpec((1, SGU_WIDTH), lambda i: (0, 0))
  return pl.pallas_call(
      body, name=name, grid=(t // tr,),
      in_specs=[blk(0), blk(1), vec, vec,
                pl.BlockSpec((N_SGU_GROUPS, CHUNK, CHUNK), lambda i: (0, 0, 0)),
                pl.BlockSpec((CHUNK, SGU_WIDTH), lambda i: (0, 0))],
      out_specs=blk(0), out_shape=jax.ShapeDtypeStruct((t, SGU_WIDTH), BF16),
      scratch_shapes=[pltpu.VMEM((tr, SGU_WIDTH), BF16)],
      compiler_params=_params(("parallel",)),
  )(h2, h2, ln_g.reshape(1, -1), ln_b.reshape(1, -1), w_s, _sgu_bias_lanes(b_s))


def _sgu_bwd(h2, dcat, ln_g, ln_b, w_s, b_s, name, tr=512):
  t, _ = h2.shape
  tr = _tile(t, tr)
  nch = tr // CHUNK
  npair = N_SGU_GROUPS // 2
  nsteps = t // tr

  def body(u_ref, v_ref, dm_ref, g_ref, b_ref, w_ref, bs_ref,
           du_ref, dv_ref, dw_ref, dbs_ref, dg_ref, db_ref, vn_sc, dmx_sc, dvn_sc, mix_sc, dbx_sc):
    i = pl.program_id(0)
    tril, head0 = _sgu_consts()

    @pl.when(i == 0)
    def _():
      dw_ref[...] = jnp.zeros_like(dw_ref)
      dg_ref[...] = jnp.zeros_like(dg_ref)
      db_ref[...] = jnp.zeros_like(db_ref)
      dbx_sc[...] = jnp.zeros_like(dbx_sc)

    gv, gv_der = _gelu_parts(v_ref[...])
    xhat, rstd = _ln_stats(gv)
    g = g_ref[...]
    vn_sc[...] = _bf(xhat * g + b_ref[...])
    gu, gu_der = _gelu_parts(u_ref[...])
    dmix = dm_ref[...]
    dmx_sc[...] = dmix * gu

    for jp in range(npair):
      cs = slice(jp * LANES, (jp + 1) * LANES)
      w0 = _bf(jnp.where(tril, w_ref[2 * jp], 0.0))
      w1 = _bf(jnp.where(tril, w_ref[2 * jp + 1], 0.0))
      bias = bs_ref[:, cs]
      dw0 = jnp.zeros((CHUNK, CHUNK), F32)
      dw1 = jnp.zeros((CHUNK, CHUNK), F32)
      dbx = jnp.zeros((CHUNK, LANES), F32)
      for c in range(nch):
        rs = slice(c * CHUNK, (c + 1) * CHUNK)
        vb = vn_sc[rs, cs]
        mix_sc[rs, cs] = jnp.where(head0, _dot(w0, vb), _dot(w1, vb)) + bias
        dmx = dmx_sc[rs, cs]
        d0 = _bf(jnp.where(head0, dmx, 0.0))
        d1 = _bf(jnp.where(head0, 0.0, dmx))
        dvn_sc[rs, cs] = _dot_tn(w0, d0) + _dot_tn(w1, d1)
        dw0 = dw0 + _dot_nt(d0, vb)
        dw1 = dw1 + _dot_nt(d1, vb)
        dbx = dbx + dmx
      dw_ref[2 * jp] += dw0
      dw_ref[2 * jp + 1] += dw1
      dbx_sc[:, cs] += dbx

    du_ref[...] = _bf(dmix * mix_sc[...] * gu_der)
    dvn = dvn_sc[...]
    dv_ref[...] = _bf(_ln_bwd(dvn, xhat, rstd, g) * gv_der)
    dg_ref[...] += jnp.sum(dvn * xhat, axis=0, keepdims=True)
    db_ref[...] += jnp.sum(dvn, axis=0, keepdims=True)

    @pl.when(i == nsteps - 1)
    def _():
      lane = lax.broadcasted_iota(jnp.int32, (CHUNK, LANES), 1)
      acc = jnp.zeros((CHUNK, LANES), F32)
      for gi in range(N_SGU_GROUPS):
        jp, j = gi // 2, gi % 2
        part = dbx_sc[:, jp * LANES:(jp + 1) * LANES]
        hm = (lane < 64) if j == 0 else (lane >= 64)
        colsum = jnp.sum(jnp.where(hm, part, 0.0), axis=1, keepdims=True)
        acc = jnp.where(lane == gi, colsum, acc)
        dw_ref[gi] = jnp.where(tril, dw_ref[gi], 0.0)
      dbs_ref[...] = acc

  blk = lambda j: pl.BlockSpec((tr, SGU_WIDTH), lambda i: (i, j))
  vec = pl.BlockSpec((1, SGU_WIDTH), lambda i: (0, 0))
  wspec = pl.BlockSpec((N_SGU_GROUPS, CHUNK, CHUNK), lambda i: (0, 0, 0))
  big = lambda dt: pltpu.VMEM((tr, SGU_WIDTH), dt)
  du, dv, dw, dbs, dg, db = pl.pallas_call(
      body, name=name, grid=(nsteps,),
      in_specs=[blk(0), blk(1), blk(0), vec, vec, wspec, pl.BlockSpec((CHUNK, SGU_WIDTH), lambda i: (0, 0))],
      out_specs=[blk(0), blk(0), wspec, pl.BlockSpec((CHUNK, LANES), lambda i: (0, 0)), vec, vec],
      out_shape=[jax.ShapeDtypeStruct((t, SGU_WIDTH), BF16), jax.ShapeDtypeStruct((t, SGU_WIDTH), BF16),
                 jax.ShapeDtypeStruct((N_SGU_GROUPS, CHUNK, CHUNK), F32), jax.ShapeDtypeStruct((CHUNK, LANES), F32),
                 jax.ShapeDtypeStruct((1, SGU_WIDTH), F32), jax.ShapeDtypeStruct((1, SGU_WIDTH), F32)],
      scratch_shapes=[big(BF16), big(F32), big(F32), big(F32), pltpu.VMEM((CHUNK, SGU_WIDTH), F32)],
      compiler_params=_params(("arbitrary",)),
  )(h2, h2, dcat, ln_g.reshape(1, -1), ln_b.reshape(1, -1), w_s, _sgu_bias_lanes(b_s))
  return du, dv, dw, dbs[:, :N_SGU_GROUPS].T, dg[0], db[0]


def _loss_head(xo, tgt, z, g, name, tm=512):
  m, d = xo.shape
  tm = _tile(m, tm)

  def body(x_ref, t_ref, z_ref, g_ref, l_ref, dz_ref, dzb_ref, dg_ref, dbias_ref):
    @pl.when(pl.program_id(0) == 0)
    def _():
      l_ref[...] = jnp.zeros_like(l_ref)

    diff = x_ref[...] - t_ref[...]
    rowsum = jnp.sum(diff * diff, axis=1, keepdims=True)
    tot = jnp.sum(rowsum, axis=0, keepdims=True) * (0.5 / d)
    l_ref[...] += jnp.broadcast_to(tot, l_ref.shape)
    _ln_bwd_tail(diff * (1.0 / d), z_ref, g_ref, dz_ref, dzb_ref, dg_ref, dbias_ref)

  row = pl.BlockSpec((tm, d), lambda i: (i, 0))
  vec = pl.BlockSpec((1, d), lambda i: (0, 0))
  out_specs, out_shape = _ln_bwd_outs(m, d, row, vec)
  l, dz, dzb, dg, dbias = pl.pallas_call(
      body, name=name, grid=(m // tm,), in_specs=[row, row, row, vec],
      out_specs=[pl.BlockSpec((8, LANES), lambda i: (0, 0))] + out_specs,
      out_shape=[jax.ShapeDtypeStruct((8, LANES), F32)] + out_shape,
      compiler_params=_params(("arbitrary",)),
  )(xo, tgt, z, g.reshape(1, d))
  return l[0, 0], dz, dzb, dg[0], dbias[0]


def _local_step(x3, mem3, tgt3, w):
  bl, s, d = x3.shape
  t = bl * s
  nm = mem3.shape[1]
  mem2 = mem3.reshape(bl * nm, d)
  x = x3.reshape(t, d)
  xb = x
  saved = []
  for i in range(DEPTH):
    j = i // 2
    attn = i % 2 == 0
    mkv = _mm(mem2, w["w_mem_kv"][i], "nn", F32, f"mkv_fwd_{i}", tm=1024, tn=512, tk=1024)
    mkv3 = mkv.reshape(bl, nm, 2 * MEM_WIDTH)
    w_in = w["a_w_in"][j] if attn else w["b_w_in"][j]
    h = _mm(xb, w_in, "nn", F32, f"in_proj_{i}", tm=512, tn=w_in.shape[1], tk=d)
    h3 = h.reshape(bl, s, -1)
    if attn:
      mix3, lse3 = _attn_fwd(h3, f"dil_attn_fwd_{i}")
      mix = mix3.reshape(t, DIL_WIDTH)
      qcol = 3 * DIL_WIDTH // MEM_WIDTH
    else:
      mix = _sgu_fwd(h, w["sgu_ln_g"][j], w["sgu_ln_b"][j], w["sgu_w_s"][j], w["sgu_b_s"][j], f"sgu_fwd_{i}")
      lse3 = None
      qcol = 2 * SGU_WIDTH // MEM_WIDTH
    mo = _mem_fwd(h3, qcol, mkv3, f"mem_attn_fwd_{i}").reshape(t, MEM_WIDTH)
    cat = jnp.concatenate([mix, mo], axis=1)
    z1, xm, xmb = _mm_res_ln(cat, w["w_out"][i], x, w["ln_mix_g"][i], w["ln_mix_b"][i], f"out_proj_ln_{i}", tk=1024)
    a, b, hm = _ffn_up(xmb, w["w_gate"][i], w["w_up"][i], f"ffn_up_{i}")
    z2, xo, xob = _mm_res_ln(hm, w["w_down"][i], xm, w["ln_ffn_g"][i], w["ln_ffn_b"][i], f"ffn_down_ln_{i}", tk=hm.shape[1])
    saved.append(dict(xb=xb, h=h, h3=h3, mkv3=mkv3, mix3=(mix3 if attn else None), lse3=lse3, cat=cat, z1=z1,
                      xmb=xmb, a=a, b=b, hm=hm, z2=z2, qcol=qcol))
    x, xb = xo, xob

  names = ("a_w_in", "b_w_in", "sgu_ln_g", "sgu_ln_b", "sgu_w_s", "sgu_b_s", "w_mem_kv", "w_out",
           "ln_mix_g", "ln_mix_b", "w_gate", "w_up", "w_down", "ln_ffn_g", "ln_ffn_b")
  grads = {n: [None] * w[n].shape[0] for n in names}
  last = DEPTH - 1
  loss, dz2, dz2b, grads["ln_ffn_g"][last], grads["ln_ffn_b"][last] = _loss_head(
      x, tgt3.reshape(t, d), saved[last]["z2"], w["ln_ffn_g"][last], "loss_head")
  dx = None
  for i in reversed(range(DEPTH)):
    j = i // 2
    attn = i % 2 == 0
    sv = saved[i]
    da, db = _ffn_bwd_hidden(dz2b, w["w_down"][i], sv["a"], sv["b"], f"ffn_bwd_hidden_{i}")
    grads["w_down"][i] = _mm(sv["hm"], dz2b, "tn", F32, f"dw_down_{i}", tm=1408, tn=1024, tk=1024)
    grads["w_gate"][i] = _mm(sv["xmb"], da, "tn", F32, f"dw_gate_{i}", tm=1024, tn=1408, tk=1024)
    grads["w_up"][i] = _mm(sv["xmb"], db, "tn", F32, f"dw_up_{i}", tm=1024, tn=1408, tk=1024)
    dz1, dz1b, grads["ln_mix_g"][i], grads["ln_mix_b"][i] = _ffn_bwd_input_ln(
        da, db, w["w_gate"][i], w["w_up"][i], dz2, sv["z1"], w["ln_mix_g"][i], f"ffn_bwd_input_ln_{i}")
    grads["w_out"][i] = _mm(sv["cat"], dz1b, "tn", F32, f"dw_out_{i}", tm=1024, tn=1024, tk=1024)
    dcat = _mm(dz1b, w["w_out"][i], "nt", F32, f"out_proj_bwd_{i}", tm=1024, tn=1024, tk=1024)
    dcat3 = dcat.reshape(bl, s, -1)
    dqm3, dmkv3 = _mem_bwd(sv["h3"], sv["qcol"], sv["mkv3"], dcat3, f"mem_attn_bwd_{i}")
    grads["w_mem_kv"][i] = _mm(mem2, dmkv3.reshape(bl * nm, 2 * MEM_WIDTH), "tn", F32, f"dw_mem_kv_{i}", tm=1024, tn=512, tk=1024)
    dqm = dqm3.reshape(t, MEM_WIDTH)
    if attn:
      dq3, dk3, dv3 = _attn_bwd(sv["h3"], sv["mix3"], sv["lse3"], dcat3, f"dil_attn_bwd_{i}")
      parts = [dq3.reshape(t, -1), dk3.reshape(t, -1), dv3.reshape(t, -1), dqm]
    else:
      du, dv, dws, dbs, dlg, dlb = _sgu_bwd(sv["h"], dcat, w["sgu_ln_g"][j], w["sgu_ln_b"][j], w["sgu_w_s"][j],
                                             w["sgu_b_s"][j], f"sgu_bwd_{i}")
      grads["sgu_w_s"][j], grads["sgu_b_s"][j], grads["sgu_ln_g"][j], grads["sgu_ln_b"][j] = dws, dbs, dlg, dlb
      parts = [du, dv, dqm]
    dh = jnp.concatenate(parts, axis=1)
    w_in = w["a_w_in"][j] if attn else w["b_w_in"][j]
    grads["a_w_in" if attn else "b_w_in"][j] = _mm(sv["xb"], dh, "tn", F32, f"dw_in_{i}", tm=1024, tn=1280 if attn else 896, tk=1024)
    if i > 0:
      dz2, dz2b, grads["ln_ffn_g"][i - 1], grads["ln_ffn_b"][i - 1] = _in_proj_bwd_ln(
          dh, w_in, dz1, saved[i - 1]["z2"], w["ln_ffn_g"][i - 1], f"in_proj_bwd_ln_{i}")
    else:
      dx = _mm(dh, w_in, "nt", F32, f"in_proj_bwd_{i}", add=dz1, add_scale=DN_ALPHA, tm=512, tn=d, tk=w_in.shape[1])
  return loss, dx.reshape(bl, s, d), grads


def _my_place():
  return lax.axis_index("x"), lax.axis_index("y"), lax.axis_index("c")


def _other_chips(x, y):
  return [(1 - x, y), (x, 1 - y), (1 - x, 1 - y)]


ANY = pl.BlockSpec(memory_space=pl.ANY)


def _all_gather_halves(wl, name):
  _, r, c_ = wl.shape

  def body(w_ref, g_ref, send_sems, recv_sems):
    x, y, c = _my_place()
    me = 2 * x + y
    sibling = (x, y, 1 - c)
    chips = _other_chips(x, y)

    def copy(k, src, dst, to):
      return pltpu.make_async_remote_copy(src_ref=src, dst_ref=dst, send_sem=send_sems.at[k], recv_sem=recv_sems.at[k],
                                          device_id=to, device_id_type=MESH_ID)

    first = [copy(k, w_ref.at[c], g_ref.at[me, c], (px, py, c)) for k, (px, py) in enumerate(chips)]
    for cp in first:
      cp.start()
    passed = []
    for k, (px, py) in enumerate(chips):
      landed = g_ref.at[2 * px + py, c]
      copy(k, landed, landed, (px, py, c)).wait_recv()
      fwd = copy(3 + k, landed, landed, sibling)
      fwd.start()
      passed.append(fwd)
    for k, (px, py) in enumerate(chips):
      theirs = g_ref.at[2 * px + py, 1 - c]
      copy(3 + k, theirs, theirs, sibling).wait_recv()
    for cp in first + passed:
      cp.wait_send()

  got = pl.pallas_call(
      body, name=name, in_specs=[ANY], out_specs=ANY,
      out_shape=jax.ShapeDtypeStruct((4, 2, r, c_), wl.dtype),
      scratch_shapes=[pltpu.SemaphoreType.DMA((6,)), pltpu.SemaphoreType.DMA((6,))],
  )(wl)
  chip = 2 * lax.axis_index("x") + lax.axis_index("y")
  return lax.dynamic_update_slice(got, wl[None], (chip, 0, 0, 0))


def _all_gather_relayed(wl, name):
  _, r, c_ = wl.shape
  h = r // 2
  assert h % ROW_ALIGN == 0

  def body(w_ref, g_ref, send_sems, recv_sems):
    x, y, c = _my_place()
    me = 2 * x + y
    sibling = (x, y, 1 - c)
    xn, yn, dg = _other_chips(x, y)

    def copy(k, src, dst, to):
      return pltpu.make_async_remote_copy(src_ref=src, dst_ref=dst, send_sem=send_sems.at[k], recv_sem=recv_sems.at[k],
                                          device_id=to, device_id_type=MESH_ID)

    def block(chip, half):
      return g_ref.at[2 * chip[0] + chip[1], half]

    top, bottom = pl.ds(0, h), pl.ds(h, h)
    sent = [copy(0, w_ref.at[c], block((x, y), c), (*xn, c)), copy(1, w_ref.at[c], block((x, y), c), (*yn, c))]
    for cp in sent:
      cp.start()
    copy(0, block(xn, c), block(xn, c), (*xn, c)).wait_recv()
    sent.append(copy(2, block(xn, c).at[top], block(xn, c).at[top], (*yn, c)))
    sent[-1].start()
    sent.append(copy(4, block(xn, c), block(xn, c), sibling))
    sent[-1].start()
    copy(1, block(yn, c), block(yn, c), (*yn, c)).wait_recv()
    sent.append(copy(3, block(yn, c).at[bottom], block(yn, c).at[bottom], (*xn, c)))
    sent[-1].start()
    sent.append(copy(5, block(yn, c), block(yn, c), sibling))
    sent[-1].start()
    copy(2, block(dg, c).at[top], block(dg, c).at[top], (*yn, c)).wait_recv()
    copy(3, block(dg, c).at[bottom], block(dg, c).at[bottom], (*xn, c)).wait_recv()
    sent.append(copy(6, block(dg, c), block(dg, c), sibling))
    sent[-1].start()
    for k, chip in ((4, xn), (5, yn), (6, dg)):
      copy(k, block(chip, 1 - c), block(chip, 1 - c), sibling).wait_recv()
    for cp in sent:
      cp.wait_send()

  got = pl.pallas_call(
      body, name=name, in_specs=[ANY], out_specs=ANY,
      out_shape=jax.ShapeDtypeStruct((4, 2, r, c_), wl.dtype),
      scratch_shapes=[pltpu.SemaphoreType.DMA((7,)), pltpu.SemaphoreType.DMA((7,))],
  )(wl)
  chip = 2 * lax.axis_index("x") + lax.axis_index("y")
  return lax.dynamic_update_slice(got, wl[None], (chip, 0, 0, 0))


def _sibling_swap(v, name):
  def body(v_ref, o_ref, send_sem, recv_sem):
    x, y, c = _my_place()
    cp = pltpu.make_async_remote_copy(src_ref=v_ref, dst_ref=o_ref, send_sem=send_sem, recv_sem=recv_sem,
                                      device_id=(x, y, 1 - c), device_id_type=MESH_ID)
    cp.start()
    cp.wait()

  return pl.pallas_call(
      body, name=name, in_specs=[ANY], out_specs=ANY, out_shape=jax.ShapeDtypeStruct(v.shape, v.dtype),
      scratch_shapes=[pltpu.SemaphoreType.DMA, pltpu.SemaphoreType.DMA],
  )(v)


def _chip_exchange(q, name):
  _, r, c_ = q.shape

  def body(q_ref, o_ref, send_sems, recv_sems):
    x, y, c = _my_place()
    cps = []
    for k, (px, py) in enumerate(_other_chips(x, y)):
      cp = pltpu.make_async_remote_copy(src_ref=q_ref.at[2 * px + py], dst_ref=o_ref.at[k], send_sem=send_sems.at[k],
                                        recv_sem=recv_sems.at[k], device_id=(px, py, c), device_id_type=MESH_ID)
      cp.start()
      cps.append(cp)
    for cp in cps:
      cp.wait()

  return pl.pallas_call(
      body, name=name, in_specs=[ANY], out_specs=ANY, out_shape=jax.ShapeDtypeStruct((3, r, c_), q.dtype),
      scratch_shapes=[pltpu.SemaphoreType.DMA((3,)), pltpu.SemaphoreType.DMA((3,))],
  )(q)


def _share_halves(v, name):
  theirs = _sibling_swap(v, name)
  c = lax.axis_index("c")
  return jnp.where(c == 0, jnp.concatenate([v, theirs]), jnp.concatenate([theirs, v]))


def _half_spec(tr, c_, pick):
  return pl.BlockSpec((None, None, tr, c_), lambda s, r, place: (s, pick(place), r, 0))


def _cast_other_half(p, place, name, tr=512):
  _, _, r, c_ = p.shape
  tr = _tile(r, tr, 16)

  def body(place_ref, p_ref, o_ref):
    o_ref[...] = _bf(p_ref[...])

  out_spec = pl.BlockSpec((None, tr, c_), lambda s, rr, place: (s, rr, 0))
  return pl.pallas_call(
      body, name=name, out_shape=jax.ShapeDtypeStruct((4, r, c_), BF16),
      grid_spec=pltpu.PrefetchScalarGridSpec(num_scalar_prefetch=1, grid=(4, r // tr),
                                             in_specs=[_half_spec(tr, c_, lambda place: 1 - place[1])], out_specs=out_spec),
      compiler_params=_params(("parallel", "parallel")),
  )(place, p)


def _add_sibling(p, x1, place, name, tr=512):
  _, _, r, c_ = p.shape
  tr = _tile(r, tr, 16)

  def body(place_ref, p_ref, x_ref, o_ref):
    o_ref[...] = _bf(p_ref[...] + x_ref[...].astype(F32))

  row = pl.BlockSpec((None, tr, c_), lambda s, rr, place: (s, rr, 0))
  return pl.pallas_call(
      body, name=name, out_shape=jax.ShapeDtypeStruct((4, r, c_), BF16),
      grid_spec=pltpu.PrefetchScalarGridSpec(num_scalar_prefetch=1, grid=(4, r // tr),
                                             in_specs=[_half_spec(tr, c_, lambda place: place[1]), row], out_specs=row),
      compiler_params=_params(("parallel", "parallel")),
  )(place, p, x1)


def _sum_own(p, x1, x3, place, name, tr=512):
  _, _, r, c_ = p.shape
  tr = _tile(r, tr, 16)

  def body(place_ref, p_ref, x1_ref, x3_ref, o_ref):
    acc = p_ref[...] + x1_ref[...].astype(F32)
    for k in range(3):
      acc = acc + x3_ref[k].astype(F32)
    o_ref[...] = acc

  return pl.pallas_call(
      body, name=name, out_shape=jax.ShapeDtypeStruct((r, c_), F32),
      grid_spec=pltpu.PrefetchScalarGridSpec(
          num_scalar_prefetch=1, grid=(r // tr,),
          in_specs=[pl.BlockSpec((None, None, tr, c_), lambda rr, place: (place[0], place[1], rr, 0)),
                    pl.BlockSpec((None, tr, c_), lambda rr, place: (place[0], rr, 0)),
                    pl.BlockSpec((3, tr, c_), lambda rr, place: (0, rr, 0))],
          out_specs=pl.BlockSpec((tr, c_), lambda rr, place: (rr, 0))),
      compiler_params=_params(("parallel",)),
  )(place, p, x1, x3)


def _reduce_scatter(p):
  x, y, c = _my_place()
  place = jnp.stack([2 * x + y, c]).astype(jnp.int32)
  x1 = _sibling_swap(_cast_other_half(p, place, "rs_cast_other_half"), "rs_sibling_swap")
  q = _add_sibling(p, x1, place, "rs_add_sibling")
  x3 = _chip_exchange(q, "rs_chip_exchange")
  mine = _sum_own(p, x1, x3, place, "rs_sum_own")
  return _share_halves(mine, "rs_share_halves")


def _adamw(w, g, m, v, name):
  shape = w.shape
  cols = shape[-1]
  rows = w.size // cols
  tr = _tile(rows, max(8, (256 * 1024) // cols // 8 * 8), 8)

  def body(w_ref, g_ref, m_ref, v_ref, d_ref, nm_ref, nv_ref):
    gv = g_ref[...]
    nm = ADAM_B1 * m_ref[...] + (1.0 - ADAM_B1) * gv
    nv = ADAM_B2 * v_ref[...] + (1.0 - ADAM_B2) * (gv * gv)
    m_hat = nm / (1.0 - ADAM_B1 ** ADAM_STEP)
    v_hat = nv / (1.0 - ADAM_B2 ** ADAM_STEP)
    d_ref[...] = -ADAM_LR * (m_hat / (jnp.sqrt(v_hat) + ADAM_EPS) + ADAM_WD * w_ref[...])
    nm_ref[...] = nm
    nv_ref[...] = nv

  spec = pl.BlockSpec((tr, cols), lambda i: (i, 0))
  sds = jax.ShapeDtypeStruct((rows, cols), F32)
  outs = pl.pallas_call(
      body, name=name, grid=(rows // tr,), in_specs=[spec] * 4, out_specs=[spec] * 3, out_shape=[sds] * 3,
      compiler_params=_params(("parallel",)),
  )(*(t.reshape(rows, cols) for t in (w, g, m, v)))
  return tuple(o.reshape(shape) for o in outs)


SHARDED = (("a_w_in", 2), ("b_w_in", 2), ("w_mem_kv", 1), ("w_out", 1), ("w_gate", 2), ("w_up", 2), ("w_down", 1))
SMALL_SHARDED = (("sgu_ln_g", 1), ("sgu_ln_b", 1))
REPLICATED = ("sgu_w_s", "sgu_b_s", "ln_mix_g", "ln_mix_b", "ln_ffn_g", "ln_ffn_b")
SMALL_ORDER = ("sgu_w_s", "sgu_b_s", "ln_mix_g", "ln_mix_b", "ln_ffn_g", "ln_ffn_b", "sgu_ln_g", "sgu_ln_b")
ROW_ALIGN = 16


def _pad_to(v, n):
  return jnp.pad(v, (0, n - v.shape[0]))


def _round_up(n, a):
  return -(-n // a) * a


def _to_shard_major(full, axis):
  shp = full.shape
  cut = shp[:axis] + (4, shp[axis] // 4) + shp[axis + 1:]
  return jnp.moveaxis(full.reshape(cut), axis, 0).reshape(4, -1, FLAT_COLS)


def _from_shard_major(rows, shard_shape, axis):
  full = jnp.moveaxis(rows.reshape((4,) + tuple(shard_shape)), 0, axis)
  shp = full.shape
  return full.reshape(shp[:axis] + (shp[axis] * shp[axis + 1],) + shp[axis + 2:])


def _gather_weights(shards):
  segs = [shards[n].astype(BF16).reshape(-1, FLAT_COLS) for n, _ in SHARDED]
  small = jnp.concatenate([lax.bitcast_convert_type(shards[n], BF16).reshape(-1) for n, _ in SMALL_SHARDED])
  small_rows = _round_up(small.shape[0], ROW_ALIGN * FLAT_COLS) // FLAT_COLS
  segs.append(_pad_to(small, small_rows * FLAT_COLS).reshape(small_rows, FLAT_COLS))
  rows = sum(sg.shape[0] for sg in segs)
  rows_pad = _round_up(rows, 4 * ROW_ALIGN)
  if rows_pad > rows:
    segs.append(jnp.zeros((rows_pad - rows, FLAT_COLS), BF16))
  flat = jnp.concatenate(segs).reshape(2, rows_pad // 2, FLAT_COLS)
  g = _all_gather_relayed(flat, "gather_weights").reshape(4, rows_pad, FLAT_COLS)
  out, off = {}, 0
  for n, axis in SHARDED:
    nr = shards[n].size // FLAT_COLS
    out[n] = _from_shard_major(g[:, off:off + nr], shards[n].shape, axis)
    off += nr
  small_g = g[:, off:off + small_rows].reshape(4, small_rows * FLAT_COLS)
  off = 0
  for n, axis in SMALL_SHARDED:
    sz = 2 * shards[n].size
    vals = lax.bitcast_convert_type(small_g[:, off:off + sz].reshape((4,) + shards[n].shape + (2,)), F32)
    out[n] = _from_shard_major(vals, shards[n].shape, axis)
    off += sz
  return out


def _reduce_grads(grads, shard_shapes):
  segs = []
  for n, axis in SHARDED:
    for g in grads[n]:
      segs.append(_to_shard_major(g, axis - 1))
  big_rows = sum(sg.shape[1] for sg in segs)
  small_full = {n: jnp.stack(grads[n]) for n in SMALL_ORDER}
  small = jnp.concatenate([small_full[n].reshape(-1) for n in SMALL_ORDER])
  n_small = _round_up(small.shape[0], 4 * 2 * 8 * FLAT_COLS)
  quarter_rows = n_small // (4 * FLAT_COLS)
  segs.append(_pad_to(small, n_small).reshape(4, quarter_rows, FLAT_COLS))
  rows_pad = _round_up(big_rows + quarter_rows, 2 * ROW_ALIGN)
  if rows_pad > big_rows + quarter_rows:
    segs.append(jnp.zeros((4, rows_pad - big_rows - quarter_rows, FLAT_COLS), F32))
  p = jnp.concatenate(segs, axis=1).reshape(4, 2, rows_pad // 2, FLAT_COLS)
  mine = _reduce_scatter(p)
  out, off = {}, 0
  for n, _ in SHARDED:
    nr = math.prod(shard_shapes[n]) // FLAT_COLS
    out[n] = mine[off:off + nr].reshape(shard_shapes[n])
    off += nr
  piece = mine[big_rows:big_rows + quarter_rows].reshape(2, quarter_rows // 2, FLAT_COLS)
  small_sum = _all_gather_halves(piece, "gather_small_grads").reshape(n_small)
  off = 0
  for n in SMALL_ORDER:
    sz = small_full[n].size
    out[n] = small_sum[off:off + sz].reshape(small_full[n].shape)
    off += sz
  return out


WEIGHT_NAMES = ("a_w_in", "b_w_in", "sgu_ln_g", "sgu_ln_b", "sgu_w_s", "sgu_b_s", "w_mem_kv", "w_out",
                "ln_mix_g", "ln_mix_b", "w_gate", "w_up", "w_down", "ln_ffn_g", "ln_ffn_b")


def kernel(x, mem, a_w_in, b_w_in, sgu_ln_g, sgu_ln_b, sgu_w_s, sgu_b_s, w_mem_kv, w_out, ln_mix_g, ln_mix_b, w_gate, w_up, w_down, ln_ffn_g, ln_ffn_b, loss_target, m_a_w_in, m_b_w_in, m_sgu_ln_g, m_sgu_ln_b, m_sgu_w_s, m_sgu_b_s, m_w_mem_kv, m_w_out, m_ln_mix_g, m_ln_mix_b, m_w_gate, m_w_up, m_w_down, m_ln_ffn_g, m_ln_ffn_b, v_a_w_in, v_b_w_in, v_sgu_ln_g, v_sgu_ln_b, v_sgu_w_s, v_sgu_b_s, v_w_mem_kv, v_w_out, v_ln_mix_g, v_ln_mix_b, v_w_gate, v_w_up, v_w_down, v_ln_ffn_g, v_ln_ffn_b):
  weights = dict(a_w_in=a_w_in, b_w_in=b_w_in, sgu_ln_g=sgu_ln_g, sgu_ln_b=sgu_ln_b, sgu_w_s=sgu_w_s, sgu_b_s=sgu_b_s,
                 w_mem_kv=w_mem_kv, w_out=w_out, ln_mix_g=ln_mix_g, ln_mix_b=ln_mix_b, w_gate=w_gate, w_up=w_up,
                 w_down=w_down, ln_ffn_g=ln_ffn_g, ln_ffn_b=ln_ffn_b)
  mom1 = dict(a_w_in=m_a_w_in, b_w_in=m_b_w_in, sgu_ln_g=m_sgu_ln_g, sgu_ln_b=m_sgu_ln_b, sgu_w_s=m_sgu_w_s,
              sgu_b_s=m_sgu_b_s, w_mem_kv=m_w_mem_kv, w_out=m_w_out, ln_mix_g=m_ln_mix_g, ln_mix_b=m_ln_mix_b,
              w_gate=m_w_gate, w_up=m_w_up, w_down=m_w_down, ln_ffn_g=m_ln_ffn_g, ln_ffn_b=m_ln_ffn_b)
  mom2 = dict(a_w_in=v_a_w_in, b_w_in=v_b_w_in, sgu_ln_g=v_sgu_ln_g, sgu_ln_b=v_sgu_ln_b, sgu_w_s=v_sgu_w_s,
              sgu_b_s=v_sgu_b_s, w_mem_kv=v_w_mem_kv, w_out=v_w_out, ln_mix_g=v_ln_mix_g, ln_mix_b=v_ln_mix_b,
              w_gate=v_w_gate, w_up=v_w_up, w_down=v_w_down, ln_ffn_g=v_ln_ffn_g, ln_ffn_b=v_ln_ffn_b)

  full = _gather_weights(weights)
  for n in REPLICATED:
    full[n] = weights[n]
  loss_part, grad_x, grads = _local_step(x, mem, loss_target, full)
  loss = lax.psum(loss_part, MESH_AXES)

  shard_shapes = {n: weights[n].shape for n, _ in SHARDED}
  red = _reduce_grads(grads, shard_shapes)
  chip = 2 * lax.axis_index("x") + lax.axis_index("y")
  for n, axis in SMALL_SHARDED:
    width = weights[n].shape[axis]
    red[n] = lax.dynamic_slice_in_dim(red[n], chip * width, width, axis)

  small_names = SMALL_ORDER
  def pack(d):
    flat = jnp.concatenate([d[n].reshape(-1) for n in small_names])
    return _pad_to(flat, _round_up(flat.shape[0], 8 * FLAT_COLS)).reshape(-1, FLAT_COLS)
  small_out = _adamw(pack(weights), pack(red), pack(mom1), pack(mom2), "adamw_small")
  delta, new_m, new_v = {}, {}, {}
  off = 0
  for n in small_names:
    sz = weights[n].size
    for dst, src in zip((delta, new_m, new_v), small_out):
      dst[n] = src.reshape(-1)[off:off + sz].reshape(weights[n].shape)
    off += sz
  for n, _ in SHARDED:
    delta[n], new_m[n], new_v[n] = _adamw(weights[n], red[n], mom1[n], mom2[n], f"adamw_{n}")

  return (loss, grad_x, *[red[n] for n in WEIGHT_NAMES], *[delta[n] for n in WEIGHT_NAMES],
          *[new_m[n] for n in WEIGHT_NAMES], *[new_v[n] for n in WEIGHT_NAMES])
```

```python
import functools
import math

import jax
import jax.numpy as jnp
from jax import lax
from jax.experimental import pallas as pl
from jax.experimental.pallas import tpu as pltpu

F32 = jnp.float32
BF16 = jnp.bfloat16

DEPTH = 4
HEAD_DIM = 64
N_DIL_HEADS = 12
DIL_WIDTH = N_DIL_HEADS * HEAD_DIM
DIL_PATTERNS = ((128, 1), (512, 4), (2048, 16))
BLOCK = 128
N_SGU_GROUPS = 12
SGU_WIDTH = N_SGU_GROUPS * 64
CHUNK = 128
N_MEM_HEADS = 4
MEM_WIDTH = N_MEM_HEADS * HEAD_DIM
DN_ALPHA = (2 * DEPTH) ** 0.25
LN_EPS = 1e-5
ATT_SCALE = HEAD_DIM ** -0.5
ADAM_LR = 0.001
ADAM_B1 = 0.9
ADAM_B2 = 0.999
ADAM_EPS = 1e-08
ADAM_WD = 0.01
ADAM_STEP = 10
NEG_BIG = -1e30
ATTN_UNROLL = 2

LANES = 128
FLAT_COLS = 1024
VMEM_LIMIT = 56 * 1024 * 1024
MESH_AXES = ("x", "y", "c")
MESH_ID = pl.DeviceIdType.MESH


def _tile(n, pref, align=LANES):
  if n <= pref:
    return n
  t = (pref // align) * align
  while t >= align:
    if n % t == 0:
      return t
    t -= align
  return n


def _params(sem):
  return pltpu.CompilerParams(dimension_semantics=sem, vmem_limit_bytes=VMEM_LIMIT)


def _dot(a, b):
  return jnp.dot(a, b, preferred_element_type=F32)


def _dot_nt(a, b):
  return lax.dot_general(a, b, (((1,), (1,)), ((), ())), preferred_element_type=F32)


def _dot_tn(a, b):
  return lax.dot_general(a, b, (((0,), (0,)), ((), ())), preferred_element_type=F32)


def _bf(v):
  return v.astype(BF16)


def _ln_stats(z):
  mu = jnp.mean(z, axis=-1, keepdims=True)
  zc = z - mu
  var = jnp.mean(zc * zc, axis=-1, keepdims=True)
  rstd = lax.rsqrt(var + LN_EPS)
  return zc * rstd, rstd


def _ln_bwd(dy, xhat, rstd, g):
  gdy = dy * g
  m1 = jnp.mean(gdy, axis=-1, keepdims=True)
  m2 = jnp.mean(gdy * xhat, axis=-1, keepdims=True)
  return rstd * (gdy - m1 - xhat * m2)


_GELU_C = math.sqrt(2.0 / math.pi)


def _gelu_parts(v):
  v2 = v * v
  t = jnp.tanh(_GELU_C * (v + 0.044715 * v * v2))
  val = 0.5 * v * (1.0 + t)
  der = 0.5 * (1.0 + t) + 0.5 * v * (1.0 - t * t) * (_GELU_C * (1.0 + 3.0 * 0.044715 * v2))
  return val, der


def _gelu(v):
  t = jnp.tanh(_GELU_C * (v + 0.044715 * v * v * v))
  return 0.5 * v * (1.0 + t)


def _sigmoid(v):
  return 1.0 / (1.0 + jnp.exp(-v))


def _mm(a, b, mode, out_dtype, name, add=None, add_scale=1.0, tm=512, tn=512, tk=512):
  if mode == "nn":
    (m, k), (k2, n) = a.shape, b.shape
  elif mode == "nt":
    (m, k), (n, k2) = a.shape, b.shape
  else:
    (k, m), (k2, n) = a.shape, b.shape
  assert k == k2, (a.shape, b.shape, mode)
  tm, tn, tk = _tile(m, tm), _tile(n, tn), _tile(k, tk)
  nk = k // tk
  if mode == "nn":
    a_spec = pl.BlockSpec((tm, tk), lambda i, j, kk: (i, kk))
    b_spec = pl.BlockSpec((tk, tn), lambda i, j, kk: (kk, j))
    dot = _dot
  elif mode == "nt":
    a_spec = pl.BlockSpec((tm, tk), lambda i, j, kk: (i, kk))
    b_spec = pl.BlockSpec((tn, tk), lambda i, j, kk: (j, kk))
    dot = _dot_nt
  else:
    a_spec = pl.BlockSpec((tk, tm), lambda i, j, kk: (kk, i))
    b_spec = pl.BlockSpec((tk, tn), lambda i, j, kk: (kk, j))
    dot = _dot_tn
  o_spec = pl.BlockSpec((tm, tn), lambda i, j, kk: (i, j))
  has_add = add is not None

  def body(*refs):
    if has_add:
      a_ref, b_ref, add_ref, o_ref, acc_ref = refs
    else:
      a_ref, b_ref, o_ref, acc_ref = refs
    kk = pl.program_id(2)

    @pl.when(kk == 0)
    def _():
      acc_ref[...] = jnp.zeros_like(acc_ref)

    acc_ref[...] += dot(_bf(a_ref[...]), _bf(b_ref[...]))

    @pl.when(kk == nk - 1)
    def _():
      r = acc_ref[...]
      if has_add:
        r = r + add_scale * add_ref[...].astype(F32)
      o_ref[...] = r.astype(out_dtype)

  in_specs = [a_spec, b_spec] + ([o_spec] if has_add else [])
  args = (a, b) + ((add,) if has_add else ())
  return pl.pallas_call(
      body, name=name, grid=(m // tm, n // tn, nk), in_specs=in_specs, out_specs=o_spec,
      out_shape=jax.ShapeDtypeStruct((m, n), out_dtype),
      scratch_shapes=[pltpu.VMEM((tm, tn), F32)],
      compiler_params=_params(("parallel", "parallel", "arbitrary")),
  )(*args)


def _mm_res_ln(a, w, res, g, b, name, tm=512, tk=512):
  m, k = a.shape
  d = w.shape[1]
  tm, tk = _tile(m, tm), _tile(k, tk)
  nk = k // tk

  def body(a_ref, w_ref, r_ref, g_ref, b_ref, z_ref, x_ref, xb_ref, acc_ref):
    kk = pl.program_id(1)

    @pl.when(kk == 0)
    def _():
      acc_ref[...] = jnp.zeros_like(acc_ref)

    acc_ref[...] += _dot(_bf(a_ref[...]), _bf(w_ref[...]))

    @pl.when(kk == nk - 1)
    def _():
      z = DN_ALPHA * r_ref[...] + acc_ref[...]
      xhat, _ = _ln_stats(z)
      xn = xhat * g_ref[...] + b_ref[...]
      z_ref[...] = z
      x_ref[...] = xn
      xb_ref[...] = _bf(xn)

  row = pl.BlockSpec((tm, d), lambda i, kk: (i, 0))
  vec = pl.BlockSpec((1, d), lambda i, kk: (0, 0))
  return pl.pallas_call(
      body, name=name, grid=(m // tm, nk),
      in_specs=[pl.BlockSpec((tm, tk), lambda i, kk: (i, kk)), pl.BlockSpec((tk, d), lambda i, kk: (kk, 0)), row, vec, vec],
      out_specs=[row, row, row],
      out_shape=[jax.ShapeDtypeStruct((m, d), F32), jax.ShapeDtypeStruct((m, d), F32), jax.ShapeDtypeStruct((m, d), BF16)],
      scratch_shapes=[pltpu.VMEM((tm, d), F32)],
      compiler_params=_params(("parallel", "arbitrary")),
  )(a, w, res, g.reshape(1, d), b.reshape(1, d))


def _ln_bwd_call(dy, z, g, name, tm=512):
  m, d = z.shape
  tm = _tile(m, tm)
  n = m // tm

  def body(dy_ref, z_ref, g_ref, dz_ref, dzb_ref, dg_ref, db_ref):
    i = pl.program_id(0)

    @pl.when(i == 0)
    def _():
      dg_ref[...] = jnp.zeros_like(dg_ref)
      db_ref[...] = jnp.zeros_like(db_ref)

    dy_v = dy_ref[...]
    xhat, rstd = _ln_stats(z_ref[...])
    dz = _ln_bwd(dy_v, xhat, rstd, g_ref[...])
    dz_ref[...] = dz
    dzb_ref[...] = _bf(dz)
    dg_ref[...] += jnp.sum(dy_v * xhat, axis=0, keepdims=True)
    db_ref[...] += jnp.sum(dy_v, axis=0, keepdims=True)

  row = pl.BlockSpec((tm, d), lambda i: (i, 0))
  vec = pl.BlockSpec((1, d), lambda i: (0, 0))
  dz, dzb, dg, db = pl.pallas_call(
      body, name=name, grid=(n,), in_specs=[row, row, vec], out_specs=[row, row, vec, vec],
      out_shape=[jax.ShapeDtypeStruct((m, d), F32), jax.ShapeDtypeStruct((m, d), BF16),
                 jax.ShapeDtypeStruct((1, d), F32), jax.ShapeDtypeStruct((1, d), F32)],
      compiler_params=_params(("arbitrary",)),
  )(dy, z, g.reshape(1, d))
  return dz, dzb, dg[0], db[0]


def _ffn_up(xb, wg, wu, name, tm=512, tn=1408):
  m, d = xb.shape
  f = wg.shape[0]
  tm, tn = _tile(m, tm), _tile(f, tn)

  def body(x_ref, wg_ref, wu_ref, a_ref, b_ref, h_ref):
    xv = x_ref[...]
    a = _dot_nt(xv, wg_ref[...])
    b = _dot_nt(xv, wu_ref[...])
    a_ref[...] = _bf(a)
    b_ref[...] = _bf(b)
    h_ref[...] = _bf(a * _sigmoid(a) * b)

  wspec = pl.BlockSpec((tn, d), lambda j, i: (j, 0))
  ospec = pl.BlockSpec((tm, tn), lambda j, i: (i, j))
  sds = jax.ShapeDtypeStruct((m, f), BF16)
  return pl.pallas_call(
      body, name=name, grid=(f // tn, m // tm),
      in_specs=[pl.BlockSpec((tm, d), lambda j, i: (i, 0)), wspec, wspec],
      out_specs=[ospec, ospec, ospec], out_shape=[sds, sds, sds],
      compiler_params=_params(("parallel", "parallel")),
  )(xb, wg, wu)


def _ffn_bwd_hidden(dzb, wd, a, b, name, tm=512, tn=1408):
  m, d = dzb.shape
  f = wd.shape[0]
  tm, tn = _tile(m, tm), _tile(f, tn)

  def body(dz_ref, wd_ref, a_ref, b_ref, da_ref, db_ref):
    dh = _dot_nt(dz_ref[...], wd_ref[...])
    av = a_ref[...].astype(F32)
    bv = b_ref[...].astype(F32)
    sg = _sigmoid(av)
    da_ref[...] = _bf(dh * bv * (sg * (1.0 + av * (1.0 - sg))))
    db_ref[...] = _bf(dh * (av * sg))

  hspec = pl.BlockSpec((tm, tn), lambda j, i: (i, j))
  sds = jax.ShapeDtypeStruct((m, f), BF16)
  return pl.pallas_call(
      body, name=name, grid=(f // tn, m // tm),
      in_specs=[pl.BlockSpec((tm, d), lambda j, i: (i, 0)), pl.BlockSpec((tn, d), lambda j, i: (j, 0)), hspec, hspec],
      out_specs=[hspec, hspec], out_shape=[sds, sds],
      compiler_params=_params(("parallel", "parallel")),
  )(dzb, wd, a, b)


def _ln_bwd_tail(dy, z_ref, g_ref, dz_ref, dzb_ref, dg_ref, db_ref):
  @pl.when(pl.program_id(0) == 0)
  def _():
    dg_ref[...] = jnp.zeros_like(dg_ref)
    db_ref[...] = jnp.zeros_like(db_ref)

  xhat, rstd = _ln_stats(z_ref[...])
  dz = _ln_bwd(dy, xhat, rstd, g_ref[...])
  dz_ref[...] = dz
  dzb_ref[...] = _bf(dz)
  dg_ref[...] += jnp.sum(dy * xhat, axis=0, keepdims=True)
  db_ref[...] += jnp.sum(dy, axis=0, keepdims=True)


def _ln_bwd_outs(m, d, row, vec):
  return ([row, row, vec, vec],
          [jax.ShapeDtypeStruct((m, d), F32), jax.ShapeDtypeStruct((m, d), BF16),
           jax.ShapeDtypeStruct((1, d), F32), jax.ShapeDtypeStruct((1, d), F32)])


def _ffn_bwd_input_ln(da, db, wg, wu, dz2, z1, g, name, tm=512):
  m, f = da.shape
  d = wg.shape[1]
  tm = _tile(m, tm)

  def body(da_ref, db_ref, wg_ref, wu_ref, dz2_ref, z_ref, g_ref, dz_ref, dzb_ref, dg_ref, dbias_ref):
    dy = DN_ALPHA * dz2_ref[...] + _dot(da_ref[...], wg_ref[...]) + _dot(db_ref[...], wu_ref[...])
    _ln_bwd_tail(dy, z_ref, g_ref, dz_ref, dzb_ref, dg_ref, dbias_ref)

  hspec = pl.BlockSpec((tm, f), lambda i: (i, 0))
  wspec = pl.BlockSpec((f, d), lambda i: (0, 0), pipeline_mode=pl.Buffered(1))
  row = pl.BlockSpec((tm, d), lambda i: (i, 0))
  vec = pl.BlockSpec((1, d), lambda i: (0, 0))
  out_specs, out_shape = _ln_bwd_outs(m, d, row, vec)
  dz, dzb, dg, dbias = pl.pallas_call(
      body, name=name, grid=(m // tm,), in_specs=[hspec, hspec, wspec, wspec, row, row, vec],
      out_specs=out_specs, out_shape=out_shape, compiler_params=_params(("arbitrary",)),
  )(da, db, wg, wu, dz2, z1, g.reshape(1, d))
  return dz, dzb, dg[0], dbias[0]


def _in_proj_bwd_ln(dh, w_in, dz1, z2, g, name, tm=512):
  m, wd = dh.shape
  d = w_in.shape[1]
  tm = _tile(m, tm)

  def body(dh_ref, w_ref, dz1_ref, z_ref, g_ref, dz_ref, dzb_ref, dg_ref, dbias_ref):
    dy = DN_ALPHA * dz1_ref[...] + _dot(dh_ref[...], w_ref[...])
    _ln_bwd_tail(dy, z_ref, g_ref, dz_ref, dzb_ref, dg_ref, dbias_ref)

  row = pl.BlockSpec((tm, d), lambda i: (i, 0))
  vec = pl.BlockSpec((1, d), lambda i: (0, 0))
  out_specs, out_shape = _ln_bwd_outs(m, d, row, vec)
  dz, dzb, dg, dbias = pl.pallas_call(
      body, name=name, grid=(m // tm,),
      in_specs=[pl.BlockSpec((tm, wd), lambda i: (i, 0)),
                pl.BlockSpec((wd, d), lambda i: (0, 0), pipeline_mode=pl.Buffered(1)), row, row, vec],
      out_specs=out_specs, out_shape=out_shape, compiler_params=_params(("arbitrary",)),
  )(dh, w_in, dz1, z2, g.reshape(1, d))
  return dz, dzb, dg[0], dbias[0]


def _alibi_slopes():
  n = N_DIL_HEADS
  return jnp.exp2(-8.0 * (jnp.arange(n, dtype=F32) + 1.0) / n).reshape(1, n)


def _band_consts():
  qi = lax.broadcasted_iota(jnp.int32, (BLOCK, BLOCK), 0)
  ki = lax.broadcasted_iota(jnp.int32, (BLOCK, BLOCK), 1)
  steps_cur = (qi - ki).astype(F32)
  steps_prev = (qi + BLOCK - ki).astype(F32)
  return ki < 64, steps_cur, steps_prev, ki <= qi, ki >= qi


def _rows(start, d):
  if d == 1:
    return pl.ds(pl.multiple_of(start, BLOCK), BLOCK)
  return pl.ds(start, BLOCK, stride=d)


def _fill_bias_tables(bias_sc, slope0, slope1):
  row = lax.broadcasted_iota(jnp.int32, (2 * BLOCK, 2 * BLOCK), 0)
  col = lax.broadcasted_iota(jnp.int32, (2 * BLOCK, 2 * BLOCK), 1)
  qi = jnp.bitwise_and(row, BLOCK - 1)
  ki = jnp.bitwise_and(col, BLOCK - 1)
  is_cur = col >= BLOCK
  steps = jnp.where(is_cur, qi - ki, qi + BLOCK - ki)
  valid = jnp.logical_and(steps >= 0, steps <= BLOCK)
  slope = jnp.where(row >= BLOCK, slope1, slope0)
  dist = slope * steps.astype(F32)
  for p, (_, d) in enumerate(DIL_PATTERNS):
    base = jnp.where(valid, -d * dist, NEG_BIG)
    bias_sc[2 * p] = base
    bias_sc[2 * p + 1] = jnp.where(is_cur, base, NEG_BIG)


def _stack_heads(v2, head0):
  return jnp.concatenate([jnp.where(head0, v2, 0.0), jnp.where(head0, 0.0, v2)], axis=0)


def _unstack_heads(v, head0):
  return jnp.where(head0, v[:BLOCK], v[BLOCK:])


def _block_rows(idx, d, nblk):
  r = idx // nblk
  n = idx % nblk
  cur = _rows(r + n * (BLOCK * d), d)
  prev = _rows(r + jnp.maximum(n - 1, 0) * (BLOCK * d), d)
  return cur, prev, n


def pair_tile(dt):
  return pltpu.VMEM((2 * BLOCK, 2 * BLOCK), dt)


def _two_stage_loop(nb, first_stage, second_stage, buf_a, buf_b):
  assert nb % 2 == 0

  def pair(t, carry):
    i = 2 * t + 1
    first_stage(i, buf_b)
    second_stage(i - 1, buf_a)
    first_stage(i + 1, buf_a)
    second_stage(i, buf_b)
    return carry

  first_stage(0, buf_a)
  lax.fori_loop(0, nb // 2 - 1, pair, 0)
  first_stage(nb - 1, buf_b)
  second_stage(nb - 2, buf_a)
  second_stage(nb - 1, buf_b)


def _attn_fwd(h3, name):
  bl, s, _ = h3.shape
  npair = N_DIL_HEADS // 2
  nb = s // BLOCK

  def body(sl_ref, q_ref, k_ref, v_ref, o_ref, lse_ref, o_sc, l_sc, bias_sc, s_a, s_b):
    hp = pl.program_id(1)
    head0 = lax.broadcasted_iota(jnp.int32, (BLOCK, LANES), 1) < 64
    _fill_bias_tables(bias_sc, sl_ref[0, 2 * hp], sl_ref[0, 2 * hp + 1])

    for p, (_, d) in enumerate(DIL_PATTERNS):
      nblk = (s // d) // BLOCK
      two = nblk > 1
      ks = slice(0, 2 * BLOCK) if two else slice(BLOCK, 2 * BLOCK)

      def scores(idx, buf, p=p, d=d, nblk=nblk, two=two, ks=ks):
        cur, prev, n = _block_rows(idx, d, nblk)
        qs = _bf(_stack_heads(q_ref[cur, :], head0) * ATT_SCALE)
        kb = _bf(jnp.concatenate([k_ref[prev, :], k_ref[cur, :]], axis=0)) if two else _bf(k_ref[cur, :])
        first = jnp.where(n == 0, 1, 0) if two else 0
        buf[:, ks] = _dot_nt(qs, kb) + bias_sc[2 * p + first, :, ks]

      def values(idx, buf, p=p, d=d, nblk=nblk, two=two, ks=ks):
        cur, prev, _ = _block_rows(idx, d, nblk)
        sc = buf[:, ks]
        mx = jnp.max(sc, axis=1, keepdims=True)
        pe = jnp.exp(sc - mx)
        den = jnp.sum(pe, axis=1, keepdims=True)
        vb = _bf(jnp.concatenate([v_ref[prev, :], v_ref[cur, :]], axis=0)) if two else _bf(v_ref[cur, :])
        acc = _dot(_bf(pe), vb) / den
        o_sc[p, cur, :] = _unstack_heads(acc, head0)
        l_sc[p, cur, :] = _unstack_heads(jnp.broadcast_to(mx + jnp.log(den), (2 * BLOCK, LANES)), head0)

      _two_stage_loop(nb, scores, values, s_a, s_b)

    def merge(i, carry):
      rows = pl.ds(pl.multiple_of(i * BLOCK, BLOCK), BLOCK)
      l0, l1, l2 = l_sc[0, rows, :], l_sc[1, rows, :], l_sc[2, rows, :]
      mx = jnp.maximum(jnp.maximum(l0, l1), l2)
      e0, e1, e2 = jnp.exp(l0 - mx), jnp.exp(l1 - mx), jnp.exp(l2 - mx)
      tot = e0 + e1 + e2
      o_ref[rows, :] = _bf((e0 * o_sc[0, rows, :] + e1 * o_sc[1, rows, :] + e2 * o_sc[2, rows, :]) / tot)
      lse_ref[rows, :] = mx + jnp.log(tot)
      return carry

    lax.fori_loop(0, nb, merge, 0)

  def col(off):
    return pl.BlockSpec((None, s, LANES), lambda b, p: (b, 0, off + p))

  return pl.pallas_call(
      body, name=name, grid=(bl, npair),
      in_specs=[pl.BlockSpec(memory_space=pltpu.SMEM), col(0), col(npair), col(2 * npair)],
      out_specs=[col(0), col(0)],
      out_shape=[jax.ShapeDtypeStruct((bl, s, DIL_WIDTH), BF16), jax.ShapeDtypeStruct((bl, s, DIL_WIDTH), F32)],
      scratch_shapes=[pltpu.VMEM((3, s, LANES), F32), pltpu.VMEM((3, s, LANES), F32),
                      pltpu.VMEM((6, 2 * BLOCK, 2 * BLOCK), F32), pair_tile(F32), pair_tile(F32)],
      compiler_params=_params(("parallel", "parallel")),
  )(_alibi_slopes(), h3, h3, h3)


def _attn_bwd(h3, out3, lse3, dcat3, name):
  bl, s, _ = h3.shape
  npair = N_DIL_HEADS // 2
  nb = s // BLOCK

  def body(sl_ref, q_ref, k_ref, v_ref, o_ref, l_ref, do_ref, dq_out, dk_out, dv_out,
           bias_sc, p_a, ds_a, p_b, ds_b, prod_sc, dq_ref, dk_ref, dv_ref):
    hp = pl.program_id(1)
    lane = lax.broadcasted_iota(jnp.int32, (BLOCK, LANES), 1)
    head0 = lane < 64
    _fill_bias_tables(bias_sc, sl_ref[0, 2 * hp], sl_ref[0, 2 * hp + 1])
    dq_ref[...] = jnp.zeros_like(dq_ref)
    dk_ref[...] = jnp.zeros_like(dk_ref)
    dv_ref[...] = jnp.zeros_like(dv_ref)
    prod_sc[...] = do_ref[...] * o_ref[...].astype(F32)

    def per_row(v2, pick0, pick1):
      return jnp.concatenate([jnp.sum(jnp.where(pick0, v2, 0.0), axis=1, keepdims=True),
                              jnp.sum(jnp.where(pick1, v2, 0.0), axis=1, keepdims=True)], axis=0)

    for p, (_, d) in enumerate(DIL_PATTERNS):
      nblk = (s // d) // BLOCK
      two = nblk > 1
      ks = slice(0, 2 * BLOCK) if two else slice(BLOCK, 2 * BLOCK)

      def operands(idx, d=d, nblk=nblk, two=two):
        cur, prev, n = _block_rows(idx, d, nblk)
        qs = _bf(_stack_heads(q_ref[cur, :], head0) * ATT_SCALE)
        dos = _bf(_stack_heads(do_ref[cur, :], head0))
        kb = _bf(jnp.concatenate([k_ref[prev, :], k_ref[cur, :]], axis=0)) if two else _bf(k_ref[cur, :])
        return cur, prev, n, qs, dos, kb

      def probs(idx, bufs, p=p, two=two, ks=ks, operands=operands):
        cur, prev, n, qs, dos, kb = operands(idx)
        vb = _bf(jnp.concatenate([v_ref[prev, :], v_ref[cur, :]], axis=0)) if two else _bf(v_ref[cur, :])
        lse = per_row(l_ref[cur, :], lane == 0, lane == 64)
        delta = per_row(prod_sc[cur, :], head0, jnp.logical_not(head0))
        first = jnp.where(n == 0, 1, 0) if two else 0
        pr = jnp.exp(_dot_nt(qs, kb) + bias_sc[2 * p + first, :, ks] - lse)
        bufs[0][:, ks] = _bf(pr)
        bufs[1][:, ks] = _bf(pr * (_dot_nt(dos, vb) - delta))

      def products(idx, bufs, two=two, ks=ks, operands=operands):
        cur, prev, _, qs, dos, kb = operands(idx)
        pr = bufs[0][:, ks]
        ds = bufs[1][:, ks]
        dq_ref[cur, :] += _unstack_heads(_dot(ds, kb), head0) * ATT_SCALE
        dkb = _dot_tn(ds, qs)
        dvb = _dot_tn(pr, dos)
        if two:
          dk_ref[prev, :] += dkb[:BLOCK]
          dv_ref[prev, :] += dvb[:BLOCK]
          dk_ref[cur, :] += dkb[BLOCK:]
          dv_ref[cur, :] += dvb[BLOCK:]
        else:
          dk_ref[cur, :] += dkb
          dv_ref[cur, :] += dvb

      _two_stage_loop(nb, probs, products, (p_a, ds_a), (p_b, ds_b))

    dq_out[...] = _bf(dq_ref[...])
    dk_out[...] = _bf(dk_ref[...])
    dv_out[...] = _bf(dv_ref[...])

  def col(off):
    return pl.BlockSpec((None, s, LANES), lambda b, p: (b, 0, off + p))

  sds = jax.ShapeDtypeStruct((bl, s, DIL_WIDTH), BF16)
  return pl.pallas_call(
      body, name=name, grid=(bl, npair),
      in_specs=[pl.BlockSpec(memory_space=pltpu.SMEM), col(0), col(npair), col(2 * npair), col(0), col(0), col(0)],
      out_specs=[col(0), col(0), col(0)], out_shape=[sds, sds, sds],
      scratch_shapes=[pltpu.VMEM((6, 2 * BLOCK, 2 * BLOCK), F32)] + [pair_tile(BF16)] * 4
      + [pltpu.VMEM((s, LANES), F32)] * 4,
      compiler_params=_params(("parallel", "parallel")),
  )(_alibi_slopes(), h3, h3, h3, out3, lse3, dcat3)


def _attn_fwd_old(h3, name):
  bl, s, _ = h3.shape
  npair = N_DIL_HEADS // 2

  def body(sl_ref, q_ref, k_ref, v_ref, o_ref, lse_ref, o_sc, l_sc):
    hp = pl.program_id(1)
    head0, steps_cur, steps_prev, mask_cur, mask_prev = _band_consts()
    slope = [sl_ref[0, 2 * hp], sl_ref[0, 2 * hp + 1]]

    for p, (_, d) in enumerate(DIL_PATTERNS):
      nblk = (s // d) // BLOCK
      has_prev_block = nblk > 1

      def blk(idx, carry, p=p, d=d, nblk=nblk, has_prev_block=has_prev_block):
        r = idx // nblk
        n = idx % nblk
        cur = _rows(r + n * (BLOCK * d), d)
        q2 = q_ref[cur, :]
        kc = _bf(k_ref[cur, :])
        vc = _bf(v_ref[cur, :])
        if has_prev_block:
          prev = _rows(r + jnp.maximum(n - 1, 0) * (BLOCK * d), d)
          kp = _bf(k_ref[prev, :])
          vp = _bf(v_ref[prev, :])
          first_block = jnp.where(n > 0, 0.0, NEG_BIG)
        outs, lses = [], []
        for j in range(2):
          hm = head0 if j == 0 else jnp.logical_not(head0)
          qj = _bf(jnp.where(hm, q2, 0.0) * ATT_SCALE)
          sc = _dot_nt(qj, kc) - (slope[j] * d) * steps_cur
          sc = jnp.where(mask_cur, sc, NEG_BIG)
          mx = jnp.max(sc, axis=1, keepdims=True)
          if has_prev_block:
            sp = _dot_nt(qj, kp) - (slope[j] * d) * steps_prev + first_block
            sp = jnp.where(mask_prev, sp, NEG_BIG)
            mx = jnp.maximum(mx, jnp.max(sp, axis=1, keepdims=True))
          pc = jnp.exp(sc - mx)
          den = jnp.sum(pc, axis=1, keepdims=True)
          acc = _dot(_bf(pc), vc)
          if has_prev_block:
            pp = jnp.exp(sp - mx)
            den = den + jnp.sum(pp, axis=1, keepdims=True)
            acc = acc + _dot(_bf(pp), vp)
          outs.append(acc / den)
          lses.append(mx + jnp.log(den))
        o_sc[p, cur, :] = jnp.where(head0, outs[0], outs[1])
        l_sc[p, cur, :] = jnp.where(head0, lses[0], lses[1])
        return carry

      lax.fori_loop(0, s // BLOCK, blk, 0, unroll=ATTN_UNROLL)

    def merge(i, carry):
      rows = pl.ds(pl.multiple_of(i * BLOCK, BLOCK), BLOCK)
      l0, l1, l2 = l_sc[0, rows, :], l_sc[1, rows, :], l_sc[2, rows, :]
      mx = jnp.maximum(jnp.maximum(l0, l1), l2)
      e0, e1, e2 = jnp.exp(l0 - mx), jnp.exp(l1 - mx), jnp.exp(l2 - mx)
      tot = e0 + e1 + e2
      o_ref[rows, :] = (e0 * o_sc[0, rows, :] + e1 * o_sc[1, rows, :] + e2 * o_sc[2, rows, :]) / tot
      lse_ref[rows, :] = mx + jnp.log(tot)
      return carry

    lax.fori_loop(0, s // BLOCK, merge, 0)

  def col(off):
    return pl.BlockSpec((None, s, LANES), lambda b, p: (b, 0, off + p))

  sds = jax.ShapeDtypeStruct((bl, s, DIL_WIDTH), F32)
  return pl.pallas_call(
      body, name=name, grid=(bl, npair),
      in_specs=[pl.BlockSpec(memory_space=pltpu.SMEM), col(0), col(npair), col(2 * npair)],
      out_specs=[col(0), col(0)], out_shape=[sds, sds],
      scratch_shapes=[pltpu.VMEM((3, s, LANES), F32), pltpu.VMEM((3, s, LANES), F32)],
      compiler_params=_params(("parallel", "parallel")),
  )(_alibi_slopes(), h3, h3, h3)


def _attn_bwd_old(h3, out3, lse3, dcat3, name):
  bl, s, _ = h3.shape
  npair = N_DIL_HEADS // 2

  def body(sl_ref, q_ref, k_ref, v_ref, o_ref, l_ref, do_ref, dq_ref, dk_ref, dv_ref):
    hp = pl.program_id(1)
    head0, steps_cur, steps_prev, mask_cur, mask_prev = _band_consts()
    lane = lax.broadcasted_iota(jnp.int32, (BLOCK, LANES), 1)
    slope = [sl_ref[0, 2 * hp], sl_ref[0, 2 * hp + 1]]
    dq_ref[...] = jnp.zeros_like(dq_ref)
    dk_ref[...] = jnp.zeros_like(dk_ref)
    dv_ref[...] = jnp.zeros_like(dv_ref)

    for p, (_, d) in enumerate(DIL_PATTERNS):
      nblk = (s // d) // BLOCK
      has_prev_block = nblk > 1

      def blk(idx, carry, d=d, nblk=nblk, has_prev_block=has_prev_block):
        r = idx // nblk
        n = idx % nblk
        cur = _rows(r + n * (BLOCK * d), d)
        q2 = q_ref[cur, :]
        do2 = do_ref[cur, :]
        l2 = l_ref[cur, :]
        prod = do2 * o_ref[cur, :]
        kc = _bf(k_ref[cur, :])
        vc = _bf(v_ref[cur, :])
        if has_prev_block:
          prev = _rows(r + jnp.maximum(n - 1, 0) * (BLOCK * d), d)
          kp = _bf(k_ref[prev, :])
          vp = _bf(v_ref[prev, :])
          first_block = jnp.where(n > 0, 0.0, NEG_BIG)
          dkp = jnp.zeros((BLOCK, LANES), F32)
          dvp = jnp.zeros((BLOCK, LANES), F32)
        dq2 = jnp.zeros((BLOCK, LANES), F32)
        dkc = jnp.zeros((BLOCK, LANES), F32)
        dvc = jnp.zeros((BLOCK, LANES), F32)
        for j in range(2):
          hm = head0 if j == 0 else jnp.logical_not(head0)
          qj = _bf(jnp.where(hm, q2, 0.0) * ATT_SCALE)
          doj = _bf(jnp.where(hm, do2, 0.0))
          lj = jnp.sum(jnp.where(lane == 64 * j, l2, 0.0), axis=1, keepdims=True)
          dj = jnp.sum(jnp.where(hm, prod, 0.0), axis=1, keepdims=True)
          sc = _dot_nt(qj, kc) - (slope[j] * d) * steps_cur
          pc = jnp.exp(jnp.where(mask_cur, sc - lj, NEG_BIG))
          dsc = _bf(pc * (_dot_nt(doj, vc) - dj))
          dq_j = _dot(dsc, kc)
          dkc = dkc + _dot_tn(dsc, qj)
          dvc = dvc + _dot_tn(_bf(pc), doj)
          if has_prev_block:
            sp = _dot_nt(qj, kp) - (slope[j] * d) * steps_prev + first_block
            pp = jnp.exp(jnp.where(mask_prev, sp - lj, NEG_BIG))
            dsp = _bf(pp * (_dot_nt(doj, vp) - dj))
            dq_j = dq_j + _dot(dsp, kp)
            dkp = dkp + _dot_tn(dsp, qj)
            dvp = dvp + _dot_tn(_bf(pp), doj)
          dq2 = dq2 + jnp.where(hm, dq_j, 0.0) * ATT_SCALE
        dq_ref[cur, :] += dq2
        dk_ref[cur, :] += dkc
        dv_ref[cur, :] += dvc
        if has_prev_block:
          dk_ref[prev, :] += dkp
          dv_ref[prev, :] += dvp
        return carry

      lax.fori_loop(0, s // BLOCK, blk, 0, unroll=ATTN_UNROLL)

  def col(off):
    return pl.BlockSpec((None, s, LANES), lambda b, p: (b, 0, off + p))

  sds = jax.ShapeDtypeStruct((bl, s, DIL_WIDTH), F32)
  return pl.pallas_call(
      body, name=name, grid=(bl, npair),
      in_specs=[pl.BlockSpec(memory_space=pltpu.SMEM), col(0), col(npair), col(2 * npair), col(0), col(0), col(0)],
      out_specs=[col(0), col(0), col(0)], out_shape=[sds, sds, sds],
      compiler_params=_params(("parallel", "parallel")),
  )(_alibi_slopes(), h3, h3, h3, out3, lse3, dcat3)


def _mem_heads(tq):
  lane = lax.broadcasted_iota(jnp.int32, (tq, LANES), 1)
  return lane < 64


def _mem_fwd(h3, qcol, mkv3, name, tq=512):
  bl, s, _ = h3.shape
  nm = mkv3.shape[1]
  tq = _tile(s, tq)

  def body(q_ref, kv_ref, o_ref):
    head0 = _mem_heads(tq)
    for lg in range(MEM_WIDTH // LANES):
      cs = slice(lg * LANES, (lg + 1) * LANES)
      q2 = q_ref[:, cs]
      mk = _bf(kv_ref[:, cs])
      mv = _bf(kv_ref[:, MEM_WIDTH + lg * LANES:MEM_WIDTH + (lg + 1) * LANES])
      outs = []
      for j in range(2):
        hm = head0 if j == 0 else jnp.logical_not(head0)
        qj = _bf(jnp.where(hm, q2, 0.0) * ATT_SCALE)
        sc = _dot_nt(qj, mk)
        mx = jnp.max(sc, axis=1, keepdims=True)
        pe = jnp.exp(sc - mx)
        den = jnp.sum(pe, axis=1, keepdims=True)
        outs.append(_dot(_bf(pe / den), mv))
      o_ref[:, cs] = _bf(jnp.where(head0, outs[0], outs[1]))

  return pl.pallas_call(
      body, name=name, grid=(bl, s // tq),
      in_specs=[pl.BlockSpec((None, tq, MEM_WIDTH), lambda b, i: (b, i, qcol)),
                pl.BlockSpec((None, nm, 2 * MEM_WIDTH), lambda b, i: (b, 0, 0))],
      out_specs=pl.BlockSpec((None, tq, MEM_WIDTH), lambda b, i: (b, i, 0)),
      out_shape=jax.ShapeDtypeStruct((bl, s, MEM_WIDTH), BF16),
      compiler_params=_params(("parallel", "parallel")),
  )(h3, mkv3)


def _mem_bwd(h3, qcol, mkv3, dcat3, name, tq=512):
  bl, s, _ = h3.shape
  nm = mkv3.shape[1]
  tq = _tile(s, tq)
  docol = dcat3.shape[2] // MEM_WIDTH - 1

  def body(q_ref, kv_ref, do_ref, dq_ref, dkv_ref):
    i = pl.program_id(1)

    @pl.when(i == 0)
    def _():
      dkv_ref[...] = jnp.zeros_like(dkv_ref)

    head0 = _mem_heads(tq)
    for lg in range(MEM_WIDTH // LANES):
      cs = slice(lg * LANES, (lg + 1) * LANES)
      vs = slice(MEM_WIDTH + lg * LANES, MEM_WIDTH + (lg + 1) * LANES)
      q2 = q_ref[:, cs]
      do2 = do_ref[:, cs]
      mk = _bf(kv_ref[:, cs])
      mv = _bf(kv_ref[:, vs])
      dq2 = jnp.zeros((tq, LANES), F32)
      dmk = jnp.zeros((nm, LANES), F32)
      dmv = jnp.zeros((nm, LANES), F32)
      for j in range(2):
        hm = head0 if j == 0 else jnp.logical_not(head0)
        qj = _bf(jnp.where(hm, q2, 0.0) * ATT_SCALE)
        doj = _bf(jnp.where(hm, do2, 0.0))
        sc = _dot_nt(qj, mk)
        mx = jnp.max(sc, axis=1, keepdims=True)
        pe = jnp.exp(sc - mx)
        pn = pe / jnp.sum(pe, axis=1, keepdims=True)
        pb = _bf(pn)
        dp = _dot_nt(doj, mv)
        dj = jnp.sum(pb.astype(F32) * dp, axis=1, keepdims=True)
        ds = _bf(pn * (dp - dj))
        dq2 = dq2 + jnp.where(hm, _dot(ds, mk), 0.0) * ATT_SCALE
        dmk = dmk + _dot_tn(ds, qj)
        dmv = dmv + _dot_tn(pb, doj)
      dq_ref[:, cs] = _bf(dq2)
      dkv_ref[:, cs] += dmk
      dkv_ref[:, vs] += dmv

  return pl.pallas_call(
      body, name=name, grid=(bl, s // tq),
      in_specs=[pl.BlockSpec((None, tq, MEM_WIDTH), lambda b, i: (b, i, qcol)),
                pl.BlockSpec((None, nm, 2 * MEM_WIDTH), lambda b, i: (b, 0, 0)),
                pl.BlockSpec((None, tq, MEM_WIDTH), lambda b, i: (b, i, docol))],
      out_specs=[pl.BlockSpec((None, tq, MEM_WIDTH), lambda b, i: (b, i, 0)),
                 pl.BlockSpec((None, nm, 2 * MEM_WIDTH), lambda b, i: (b, 0, 0))],
      out_shape=[jax.ShapeDtypeStruct((bl, s, MEM_WIDTH), BF16), jax.ShapeDtypeStruct((bl, nm, 2 * MEM_WIDTH), F32)],
      compiler_params=_params(("parallel", "arbitrary")),
  )(h3, mkv3, dcat3)


def _sgu_consts():
  ti = lax.broadcasted_iota(jnp.int32, (CHUNK, CHUNK), 0)
  si = lax.broadcasted_iota(jnp.int32, (CHUNK, CHUNK), 1)
  return si <= ti, si < 64


def _sgu_bias_lanes(b_s):
  return jnp.repeat(b_s.T, 64, axis=1)


def _sgu_fwd(h2, ln_g, ln_b, w_s, b_s, name, tr=512):
  t, _ = h2.shape
  tr = _tile(t, tr)
  nch = tr // CHUNK
  npair = N_SGU_GROUPS // 2

  def body(u_ref, v_ref, g_ref, b_ref, w_ref, bs_ref, o_ref, vn_sc):
    tril, head0 = _sgu_consts()
    xhat, _ = _ln_stats(_gelu(v_ref[...]))
    vn_sc[...] = _bf(xhat * g_ref[...] + b_ref[...])
    for jp in range(npair):
      cs = slice(jp * LANES, (jp + 1) * LANES)
      w0 = _bf(jnp.where(tril, w_ref[2 * jp], 0.0))
      w1 = _bf(jnp.where(tril, w_ref[2 * jp + 1], 0.0))
      bias = bs_ref[:, cs]
      for c in range(nch):
        rs = slice(c * CHUNK, (c + 1) * CHUNK)
        vb = vn_sc[rs, cs]
        mixed = jnp.where(head0, _dot(w0, vb), _dot(w1, vb)) + bias
        o_ref[rs, cs] = _bf(_gelu(u_ref[rs, cs]) * mixed)

  blk = lambda j: pl.BlockSpec((tr, SGU_WIDTH), lambda i: (i, j))
  vec = pl.BlockSpec((1, SGU_WIDTH), lambda i: (0, 0))
  return pl.pallas_call(
      body, name=name, grid=(t // tr,),
      in_specs=[blk(0), blk(1), vec, vec,
                pl.BlockSpec((N_SGU_GROUPS, CHUNK, CHUNK), lambda i: (0, 0, 0)),
                pl.BlockSpec((CHUNK, SGU_WIDTH), lambda i: (0, 0))],
      out_specs=blk(0), out_shape=jax.ShapeDtypeStruct((t, SGU_WIDTH), BF16),
      scratch_shapes=[pltpu.VMEM((tr, SGU_WIDTH), BF16)],
      compiler_params=_params(("parallel",)),
  )(h2, h2, ln_g.reshape(1, -1), ln_b.reshape(1, -1), w_s, _sgu_bias_lanes(b_s))


def _sgu_bwd(h2, dcat, ln_g, ln_b, w_s, b_s, name, tr=512):
  t, _ = h2.shape
  tr = _tile(t, tr)
  nch = tr // CHUNK
  npair = N_SGU_GROUPS // 2
  nsteps = t // tr

  def body(u_ref, v_ref, dm_ref, g_ref, b_ref, w_ref, bs_ref,
           du_ref, dv_ref, dw_ref, dbs_ref, dg_ref, db_ref, vn_sc, dmx_sc, dvn_sc, mix_sc, dbx_sc):
    i = pl.program_id(0)
    tril, head0 = _sgu_consts()

    @pl.when(i == 0)
    def _():
      dw_ref[...] = jnp.zeros_like(dw_ref)
      dg_ref[...] = jnp.zeros_like(dg_ref)
      db_ref[...] = jnp.zeros_like(db_ref)
      dbx_sc[...] = jnp.zeros_like(dbx_sc)

    gv, gv_der = _gelu_parts(v_ref[...])
    xhat, rstd = _ln_stats(gv)
    g = g_ref[...]
    vn_sc[...] = _bf(xhat * g + b_ref[...])
    gu, gu_der = _gelu_parts(u_ref[...])
    dmix = dm_ref[...]
    dmx_sc[...] = dmix * gu

    for jp in range(npair):
      cs = slice(jp * LANES, (jp + 1) * LANES)
      w0 = _bf(jnp.where(tril, w_ref[2 * jp], 0.0))
      w1 = _bf(jnp.where(tril, w_ref[2 * jp + 1], 0.0))
      bias = bs_ref[:, cs]
      dw0 = jnp.zeros((CHUNK, CHUNK), F32)
      dw1 = jnp.zeros((CHUNK, CHUNK), F32)
      dbx = jnp.zeros((CHUNK, LANES), F32)
      for c in range(nch):
        rs = slice(c * CHUNK, (c + 1) * CHUNK)
        vb = vn_sc[rs, cs]
        mix_sc[rs, cs] = jnp.where(head0, _dot(w0, vb), _dot(w1, vb)) + bias
        dmx = dmx_sc[rs, cs]
        d0 = _bf(jnp.where(head0, dmx, 0.0))
        d1 = _bf(jnp.where(head0, 0.0, dmx))
        dvn_sc[rs, cs] = _dot_tn(w0, d0) + _dot_tn(w1, d1)
        dw0 = dw0 + _dot_nt(d0, vb)
        dw1 = dw1 + _dot_nt(d1, vb)
        dbx = dbx + dmx
      dw_ref[2 * jp] += dw0
      dw_ref[2 * jp + 1] += dw1
      dbx_sc[:, cs] += dbx

    du_ref[...] = _bf(dmix * mix_sc[...] * gu_der)
    dvn = dvn_sc[...]
    dv_ref[...] = _bf(_ln_bwd(dvn, xhat, rstd, g) * gv_der)
    dg_ref[...] += jnp.sum(dvn * xhat, axis=0, keepdims=True)
    db_ref[...] += jnp.sum(dvn, axis=0, keepdims=True)

    @pl.when(i == nsteps - 1)
    def _():
      lane = lax.broadcasted_iota(jnp.int32, (CHUNK, LANES), 1)
      acc = jnp.zeros((CHUNK, LANES), F32)
      for gi in range(N_SGU_GROUPS):
        jp, j = gi // 2, gi % 2
        part = dbx_sc[:, jp * LANES:(jp + 1) * LANES]
        hm = (lane < 64) if j == 0 else (lane >= 64)
        colsum = jnp.sum(jnp.where(hm, part, 0.0), axis=1, keepdims=True)
        acc = jnp.where(lane == gi, colsum, acc)
        dw_ref[gi] = jnp.where(tril, dw_ref[gi], 0.0)
      dbs_ref[...] = acc

  blk = lambda j: pl.BlockSpec((tr, SGU_WIDTH), lambda i: (i, j))
  vec = pl.BlockSpec((1, SGU_WIDTH), lambda i: (0, 0))
  wspec = pl.BlockSpec((N_SGU_GROUPS, CHUNK, CHUNK), lambda i: (0, 0, 0))
  big = lambda dt: pltpu.VMEM((tr, SGU_WIDTH), dt)
  du, dv, dw, dbs, dg, db = pl.pallas_call(
      body, name=name, grid=(nsteps,),
      in_specs=[blk(0), blk(1), blk(0), vec, vec, wspec, pl.BlockSpec((CHUNK, SGU_WIDTH), lambda i: (0, 0))],
      out_specs=[blk(0), blk(0), wspec, pl.BlockSpec((CHUNK, LANES), lambda i: (0, 0)), vec, vec],
      out_shape=[jax.ShapeDtypeStruct((t, SGU_WIDTH), BF16), jax.ShapeDtypeStruct((t, SGU_WIDTH), BF16),
                 jax.ShapeDtypeStruct((N_SGU_GROUPS, CHUNK, CHUNK), F32), jax.ShapeDtypeStruct((CHUNK, LANES), F32),
                 jax.ShapeDtypeStruct((1, SGU_WIDTH), F32), jax.ShapeDtypeStruct((1, SGU_WIDTH), F32)],
      scratch_shapes=[big(BF16), big(F32), big(F32), big(F32), pltpu.VMEM((CHUNK, SGU_WIDTH), F32)],
      compiler_params=_params(("arbitrary",)),
  )(h2, h2, dcat, ln_g.reshape(1, -1), ln_b.reshape(1, -1), w_s, _sgu_bias_lanes(b_s))
  return du, dv, dw, dbs[:, :N_SGU_GROUPS].T, dg[0], db[0]


def _loss_head(xo, tgt, z, g, name, tm=512):
  m, d = xo.shape
  tm = _tile(m, tm)

  def body(x_ref, t_ref, z_ref, g_ref, l_ref, dz_ref, dzb_ref, dg_ref, dbias_ref):
    @pl.when(pl.program_id(0) == 0)
    def _():
      l_ref[...] = jnp.zeros_like(l_ref)

    diff = x_ref[...] - t_ref[...]
    rowsum = jnp.sum(diff * diff, axis=1, keepdims=True)
    tot = jnp.sum(rowsum, axis=0, keepdims=True) * (0.5 / d)
    l_ref[...] += jnp.broadcast_to(tot, l_ref.shape)
    _ln_bwd_tail(diff * (1.0 / d), z_ref, g_ref, dz_ref, dzb_ref, dg_ref, dbias_ref)

  row = pl.BlockSpec((tm, d), lambda i: (i, 0))
  vec = pl.BlockSpec((1, d), lambda i: (0, 0))
  out_specs, out_shape = _ln_bwd_outs(m, d, row, vec)
  l, dz, dzb, dg, dbias = pl.pallas_call(
      body, name=name, grid=(m // tm,), in_specs=[row, row, row, vec],
      out_specs=[pl.BlockSpec((8, LANES), lambda i: (0, 0))] + out_specs,
      out_shape=[jax.ShapeDtypeStruct((8, LANES), F32)] + out_shape,
      compiler_params=_params(("arbitrary",)),
  )(xo, tgt, z, g.reshape(1, d))
  return l[0, 0], dz, dzb, dg[0], dbias[0]


def _local_step(x3, mem3, tgt3, w):
  bl, s, d = x3.shape
  t = bl * s
  nm = mem3.shape[1]
  mem2 = mem3.reshape(bl * nm, d)
  x = x3.reshape(t, d)
  xb = x
  saved = []
  for i in range(DEPTH):
    j = i // 2
    attn = i % 2 == 0
    mkv = _mm(mem2, w["w_mem_kv"][i], "nn", F32, f"mkv_fwd_{i}", tm=1024, tn=512, tk=1024)
    mkv3 = mkv.reshape(bl, nm, 2 * MEM_WIDTH)
    w_in = w["a_w_in"][j] if attn else w["b_w_in"][j]
    h = _mm(xb, w_in, "nt", F32, f"in_proj_{i}", tm=512, tn=w_in.shape[0], tk=d)
    h3 = h.reshape(bl, s, -1)
    if attn:
      mix3, lse3 = _attn_fwd(h3, f"dil_attn_fwd_{i}")
      mix = mix3.reshape(t, DIL_WIDTH)
      qcol = 3 * DIL_WIDTH // MEM_WIDTH
    else:
      mix = _sgu_fwd(h, w["sgu_ln_g"][j], w["sgu_ln_b"][j], w["sgu_w_s"][j], w["sgu_b_s"][j], f"sgu_fwd_{i}")
      lse3 = None
      qcol = 2 * SGU_WIDTH // MEM_WIDTH
    mo = _mem_fwd(h3, qcol, mkv3, f"mem_attn_fwd_{i}").reshape(t, MEM_WIDTH)
    cat = jnp.concatenate([mix, mo], axis=1)
    z1, xm, xmb = _mm_res_ln(cat, w["w_out"][i], x, w["ln_mix_g"][i], w["ln_mix_b"][i], f"out_proj_ln_{i}", tk=1024)
    a, b, hm = _ffn_up(xmb, w["w_gate"][i], w["w_up"][i], f"ffn_up_{i}")
    z2, xo, xob = _mm_res_ln(hm, w["w_down"][i], xm, w["ln_ffn_g"][i], w["ln_ffn_b"][i], f"ffn_down_ln_{i}", tk=hm.shape[1])
    saved.append(dict(xb=xb, h=h, h3=h3, mkv3=mkv3, mix3=(mix3 if attn else None), lse3=lse3, cat=cat, z1=z1,
                      xmb=xmb, a=a, b=b, hm=hm, z2=z2, qcol=qcol))
    x, xb = xo, xob

  names = ("a_w_in", "b_w_in", "sgu_ln_g", "sgu_ln_b", "sgu_w_s", "sgu_b_s", "w_mem_kv", "w_out",
           "ln_mix_g", "ln_mix_b", "w_gate", "w_up", "w_down", "ln_ffn_g", "ln_ffn_b")
  grads = {n: [None] * w[n].shape[0] for n in names}
  last = DEPTH - 1
  loss, dz2, dz2b, grads["ln_ffn_g"][last], grads["ln_ffn_b"][last] = _loss_head(
      x, tgt3.reshape(t, d), saved[last]["z2"], w["ln_ffn_g"][last], "loss_head")
  dx = None
  for i in reversed(range(DEPTH)):
    j = i // 2
    attn = i % 2 == 0
    sv = saved[i]
    da, db = _ffn_bwd_hidden(dz2b, w["w_down"][i], sv["a"], sv["b"], f"ffn_bwd_hidden_{i}")
    grads["w_down"][i] = _mm(sv["hm"], dz2b, "tn", F32, f"dw_down_{i}", tm=1408, tn=1024, tk=1024)
    grads["w_gate"][i] = _mm(da, sv["xmb"], "tn", F32, f"dw_gate_{i}", tm=1408, tn=1024, tk=1024)
    grads["w_up"][i] = _mm(db, sv["xmb"], "tn", F32, f"dw_up_{i}", tm=1408, tn=1024, tk=1024)
    dz1, dz1b, grads["ln_mix_g"][i], grads["ln_mix_b"][i] = _ffn_bwd_input_ln(
        da, db, w["w_gate"][i], w["w_up"][i], dz2, sv["z1"], w["ln_mix_g"][i], f"ffn_bwd_input_ln_{i}")
    grads["w_out"][i] = _mm(sv["cat"], dz1b, "tn", F32, f"dw_out_{i}", tm=1024, tn=1024, tk=1024)
    dcat = _mm(dz1b, w["w_out"][i], "nt", F32, f"out_proj_bwd_{i}", tm=1024, tn=1024, tk=1024)
    dcat3 = dcat.reshape(bl, s, -1)
    dqm3, dmkv3 = _mem_bwd(sv["h3"], sv["qcol"], sv["mkv3"], dcat3, f"mem_attn_bwd_{i}")
    grads["w_mem_kv"][i] = _mm(mem2, dmkv3.reshape(bl * nm, 2 * MEM_WIDTH), "tn", F32, f"dw_mem_kv_{i}", tm=1024, tn=512, tk=1024)
    dqm = dqm3.reshape(t, MEM_WIDTH)
    if attn:
      dq3, dk3, dv3 = _attn_bwd(sv["h3"], sv["mix3"], sv["lse3"], dcat3, f"dil_attn_bwd_{i}")
      parts = [dq3.reshape(t, -1), dk3.reshape(t, -1), dv3.reshape(t, -1), dqm]
    else:
      du, dv, dws, dbs, dlg, dlb = _sgu_bwd(sv["h"], dcat, w["sgu_ln_g"][j], w["sgu_ln_b"][j], w["sgu_w_s"][j],
                                             w["sgu_b_s"][j], f"sgu_bwd_{i}")
      grads["sgu_w_s"][j], grads["sgu_b_s"][j], grads["sgu_ln_g"][j], grads["sgu_ln_b"][j] = dws, dbs, dlg, dlb
      parts = [du, dv, dqm]
    dh = jnp.concatenate(parts, axis=1)
    w_in = w["a_w_in"][j] if attn else w["b_w_in"][j]
    grads["a_w_in" if attn else "b_w_in"][j] = _mm(dh, sv["xb"], "tn", F32, f"dw_in_{i}", tm=1280 if attn else 896, tn=1024, tk=1024)
    if i > 0:
      dz2, dz2b, grads["ln_ffn_g"][i - 1], grads["ln_ffn_b"][i - 1] = _in_proj_bwd_ln(
          dh, w_in, dz1, saved[i - 1]["z2"], w["ln_ffn_g"][i - 1], f"in_proj_bwd_ln_{i}")
    else:
      dx = _mm(dh, w_in, "nn", F32, f"in_proj_bwd_{i}", add=dz1, add_scale=DN_ALPHA, tm=512, tn=d, tk=w_in.shape[0])
  return loss, dx.reshape(bl, s, d), grads


def _my_place():
  return lax.axis_index("x"), lax.axis_index("y"), lax.axis_index("c")


def _other_chips(x, y):
  return [(1 - x, y), (x, 1 - y), (1 - x, 1 - y)]


ANY = pl.BlockSpec(memory_space=pl.ANY)


def _all_gather_halves(wl, name):
  _, r, c_ = wl.shape

  def body(w_ref, g_ref, send_sems, recv_sems):
    x, y, c = _my_place()
    me = 2 * x + y
    sibling = (x, y, 1 - c)
    chips = _other_chips(x, y)

    def copy(k, src, dst, to):
      return pltpu.make_async_remote_copy(src_ref=src, dst_ref=dst, send_sem=send_sems.at[k], recv_sem=recv_sems.at[k],
                                          device_id=to, device_id_type=MESH_ID)

    first = [copy(k, w_ref.at[c], g_ref.at[me, c], (px, py, c)) for k, (px, py) in enumerate(chips)]
    for cp in first:
      cp.start()
    passed = []
    for k, (px, py) in enumerate(chips):
      landed = g_ref.at[2 * px + py, c]
      copy(k, landed, landed, (px, py, c)).wait_recv()
      fwd = copy(3 + k, landed, landed, sibling)
      fwd.start()
      passed.append(fwd)
    for k, (px, py) in enumerate(chips):
      theirs = g_ref.at[2 * px + py, 1 - c]
      copy(3 + k, theirs, theirs, sibling).wait_recv()
    for cp in first + passed:
      cp.wait_send()

  got = pl.pallas_call(
      body, name=name, in_specs=[ANY], out_specs=ANY,
      out_shape=jax.ShapeDtypeStruct((4, 2, r, c_), wl.dtype),
      scratch_shapes=[pltpu.SemaphoreType.DMA((6,)), pltpu.SemaphoreType.DMA((6,))],
  )(wl)
  chip = 2 * lax.axis_index("x") + lax.axis_index("y")
  return lax.dynamic_update_slice(got, wl[None], (chip, 0, 0, 0))


def _all_gather_relayed(wl, name):
  _, r, c_ = wl.shape
  h = r // 2
  assert h % ROW_ALIGN == 0

  def body(w_ref, g_ref, send_sems, recv_sems):
    x, y, c = _my_place()
    me = 2 * x + y
    sibling = (x, y, 1 - c)
    xn, yn, dg = _other_chips(x, y)

    def copy(k, src, dst, to):
      return pltpu.make_async_remote_copy(src_ref=src, dst_ref=dst, send_sem=send_sems.at[k], recv_sem=recv_sems.at[k],
                                          device_id=to, device_id_type=MESH_ID)

    def block(chip, half):
      return g_ref.at[2 * chip[0] + chip[1], half]

    top, bottom = pl.ds(0, h), pl.ds(h, h)
    sent = [copy(0, w_ref.at[c], block((x, y), c), (*xn, c)), copy(1, w_ref.at[c], block((x, y), c), (*yn, c))]
    for cp in sent:
      cp.start()
    copy(0, block(xn, c), block(xn, c), (*xn, c)).wait_recv()
    sent.append(copy(2, block(xn, c).at[top], block(xn, c).at[top], (*yn, c)))
    sent[-1].start()
    sent.append(copy(4, block(xn, c), block(xn, c), sibling))
    sent[-1].start()
    copy(1, block(yn, c), block(yn, c), (*yn, c)).wait_recv()
    sent.append(copy(3, block(yn, c).at[bottom], block(yn, c).at[bottom], (*xn, c)))
    sent[-1].start()
    sent.append(copy(5, block(yn, c), block(yn, c), sibling))
    sent[-1].start()
    copy(2, block(dg, c).at[top], block(dg, c).at[top], (*yn, c)).wait_recv()
    copy(3, block(dg, c).at[bottom], block(dg, c).at[bottom], (*xn, c)).wait_recv()
    sent.append(copy(6, block(dg, c), block(dg, c), sibling))
    sent[-1].start()
    for k, chip in ((4, xn), (5, yn), (6, dg)):
      copy(k, block(chip, 1 - c), block(chip, 1 - c), sibling).wait_recv()
    for cp in sent:
      cp.wait_send()

  got = pl.pallas_call(
      body, name=name, in_specs=[ANY], out_specs=ANY,
      out_shape=jax.ShapeDtypeStruct((4, 2, r, c_), wl.dtype),
      scratch_shapes=[pltpu.SemaphoreType.DMA((7,)), pltpu.SemaphoreType.DMA((7,))],
  )(wl)
  chip = 2 * lax.axis_index("x") + lax.axis_index("y")
  return lax.dynamic_update_slice(got, wl[None], (chip, 0, 0, 0))


def _sibling_swap(v, name):
  def body(v_ref, o_ref, send_sem, recv_sem):
    x, y, c = _my_place()
    cp = pltpu.make_async_remote_copy(src_ref=v_ref, dst_ref=o_ref, send_sem=send_sem, recv_sem=recv_sem,
                                      device_id=(x, y, 1 - c), device_id_type=MESH_ID)
    cp.start()
    cp.wait()

  return pl.pallas_call(
      body, name=name, in_specs=[ANY], out_specs=ANY, out_shape=jax.ShapeDtypeStruct(v.shape, v.dtype),
      scratch_shapes=[pltpu.SemaphoreType.DMA, pltpu.SemaphoreType.DMA],
  )(v)


def _chip_exchange(q, name):
  _, r, c_ = q.shape

  def body(q_ref, o_ref, send_sems, recv_sems):
    x, y, c = _my_place()
    cps = []
    for k, (px, py) in enumerate(_other_chips(x, y)):
      cp = pltpu.make_async_remote_copy(src_ref=q_ref.at[2 * px + py], dst_ref=o_ref.at[k], send_sem=send_sems.at[k],
                                        recv_sem=recv_sems.at[k], device_id=(px, py, c), device_id_type=MESH_ID)
      cp.start()
      cps.append(cp)
    for cp in cps:
      cp.wait()

  return pl.pallas_call(
      body, name=name, in_specs=[ANY], out_specs=ANY, out_shape=jax.ShapeDtypeStruct((3, r, c_), q.dtype),
      scratch_shapes=[pltpu.SemaphoreType.DMA((3,)), pltpu.SemaphoreType.DMA((3,))],
  )(q)


def _share_halves(v, name):
  theirs = _sibling_swap(v, name)
  c = lax.axis_index("c")
  return jnp.where(c == 0, jnp.concatenate([v, theirs]), jnp.concatenate([theirs, v]))


def _half_spec(tr, c_, pick):
  return pl.BlockSpec((None, None, tr, c_), lambda s, r, place: (s, pick(place), r, 0))


def _cast_other_half(p, place, name, tr=512):
  _, _, r, c_ = p.shape
  tr = _tile(r, tr, 16)

  def body(place_ref, p_ref, o_ref):
    o_ref[...] = _bf(p_ref[...])

  out_spec = pl.BlockSpec((None, tr, c_), lambda s, rr, place: (s, rr, 0))
  return pl.pallas_call(
      body, name=name, out_shape=jax.ShapeDtypeStruct((4, r, c_), BF16),
      grid_spec=pltpu.PrefetchScalarGridSpec(num_scalar_prefetch=1, grid=(4, r // tr),
                                             in_specs=[_half_spec(tr, c_, lambda place: 1 - place[1])], out_specs=out_spec),
      compiler_params=_params(("parallel", "parallel")),
  )(place, p)


def _add_sibling(p, x1, place, name, tr=512):
  _, _, r, c_ = p.shape
  tr = _tile(r, tr, 16)

  def body(place_ref, p_ref, x_ref, o_ref):
    o_ref[...] = _bf(p_ref[...] + x_ref[...].astype(F32))

  row = pl.BlockSpec((None, tr, c_), lambda s, rr, place: (s, rr, 0))
  return pl.pallas_call(
      body, name=name, out_shape=jax.ShapeDtypeStruct((4, r, c_), BF16),
      grid_spec=pltpu.PrefetchScalarGridSpec(num_scalar_prefetch=1, grid=(4, r // tr),
                                             in_specs=[_half_spec(tr, c_, lambda place: place[1]), row], out_specs=row),
      compiler_params=_params(("parallel", "parallel")),
  )(place, p, x1)


def _sum_own(p, x1, x3, place, name, tr=512):
  _, _, r, c_ = p.shape
  tr = _tile(r, tr, 16)

  def body(place_ref, p_ref, x1_ref, x3_ref, o_ref):
    acc = p_ref[...] + x1_ref[...].astype(F32)
    for k in range(3):
      acc = acc + x3_ref[k].astype(F32)
    o_ref[...] = acc

  return pl.pallas_call(
      body, name=name, out_shape=jax.ShapeDtypeStruct((r, c_), F32),
      grid_spec=pltpu.PrefetchScalarGridSpec(
          num_scalar_prefetch=1, grid=(r // tr,),
          in_specs=[pl.BlockSpec((None, None, tr, c_), lambda rr, place: (place[0], place[1], rr, 0)),
                    pl.BlockSpec((None, tr, c_), lambda rr, place: (place[0], rr, 0)),
                    pl.BlockSpec((3, tr, c_), lambda rr, place: (0, rr, 0))],
          out_specs=pl.BlockSpec((tr, c_), lambda rr, place: (rr, 0))),
      compiler_params=_params(("parallel",)),
  )(place, p, x1, x3)


def _reduce_scatter(p):
  x, y, c = _my_place()
  place = jnp.stack([2 * x + y, c]).astype(jnp.int32)
  x1 = _sibling_swap(_cast_other_half(p, place, "rs_cast_other_half"), "rs_sibling_swap")
  q = _add_sibling(p, x1, place, "rs_add_sibling")
  x3 = _chip_exchange(q, "rs_chip_exchange")
  mine = _sum_own(p, x1, x3, place, "rs_sum_own")
  return _share_halves(mine, "rs_share_halves")


def _adamw(w, g, m, v, name):
  shape = w.shape
  cols = shape[-1]
  rows = w.size // cols
  tr = _tile(rows, max(8, (256 * 1024) // cols // 8 * 8), 8)

  def body(w_ref, g_ref, m_ref, v_ref, d_ref, nm_ref, nv_ref):
    gv = g_ref[...]
    nm = ADAM_B1 * m_ref[...] + (1.0 - ADAM_B1) * gv
    nv = ADAM_B2 * v_ref[...] + (1.0 - ADAM_B2) * (gv * gv)
    m_hat = nm / (1.0 - ADAM_B1 ** ADAM_STEP)
    v_hat = nv / (1.0 - ADAM_B2 ** ADAM_STEP)
    d_ref[...] = -ADAM_LR * (m_hat / (jnp.sqrt(v_hat) + ADAM_EPS) + ADAM_WD * w_ref[...])
    nm_ref[...] = nm
    nv_ref[...] = nv

  spec = pl.BlockSpec((tr, cols), lambda i: (i, 0))
  sds = jax.ShapeDtypeStruct((rows, cols), F32)
  outs = pl.pallas_call(
      body, name=name, grid=(rows // tr,), in_specs=[spec] * 4, out_specs=[spec] * 3, out_shape=[sds] * 3,
      compiler_params=_params(("parallel",)),
  )(*(t.reshape(rows, cols) for t in (w, g, m, v)))
  return tuple(o.reshape(shape) for o in outs)


SHARDED = (("a_w_in", True), ("b_w_in", True), ("w_mem_kv", False), ("w_out", False), ("w_gate", True),
           ("w_up", True), ("w_down", False))
SMALL_SHARDED = (("sgu_ln_g", 1), ("sgu_ln_b", 1))
REPLICATED = ("sgu_w_s", "sgu_b_s", "ln_mix_g", "ln_mix_b", "ln_ffn_g", "ln_ffn_b")
SMALL_ORDER = ("sgu_w_s", "sgu_b_s", "ln_mix_g", "ln_mix_b", "ln_ffn_g", "ln_ffn_b", "sgu_ln_g", "sgu_ln_b")
ROW_ALIGN = 16


def _pad_to(v, n):
  return jnp.pad(v, (0, n - v.shape[0]))


def _round_up(n, a):
  return -(-n // a) * a


def _exchange_form(t, transposed):
  return jnp.swapaxes(t, 1, 2) if transposed else t


def _to_shard_major(full, axis):
  shp = full.shape
  cut = shp[:axis] + (4, shp[axis] // 4) + shp[axis + 1:]
  return jnp.moveaxis(full.reshape(cut), axis, 0).reshape(4, -1, FLAT_COLS)


def _from_shard_major(rows, shard_shape, axis):
  full = jnp.moveaxis(rows.reshape((4,) + tuple(shard_shape)), 0, axis)
  shp = full.shape
  return full.reshape(shp[:axis] + (shp[axis] * shp[axis + 1],) + shp[axis + 2:])


def _gather_weights(shards):
  blocks = {n: _exchange_form(shards[n], tr) for n, tr in SHARDED}
  segs = [blocks[n].astype(BF16).reshape(-1, FLAT_COLS) for n, _ in SHARDED]
  small = jnp.concatenate([lax.bitcast_convert_type(shards[n], BF16).reshape(-1) for n, _ in SMALL_SHARDED])
  small_rows = _round_up(small.shape[0], ROW_ALIGN * FLAT_COLS) // FLAT_COLS
  segs.append(_pad_to(small, small_rows * FLAT_COLS).reshape(small_rows, FLAT_COLS))
  rows = sum(sg.shape[0] for sg in segs)
  rows_pad = _round_up(rows, 4 * ROW_ALIGN)
  if rows_pad > rows:
    segs.append(jnp.zeros((rows_pad - rows, FLAT_COLS), BF16))
  flat = jnp.concatenate(segs).reshape(2, rows_pad // 2, FLAT_COLS)
  g = _all_gather_relayed(flat, "gather_weights").reshape(4, rows_pad, FLAT_COLS)
  out, off = {}, 0
  for n, _ in SHARDED:
    nr = blocks[n].size // FLAT_COLS
    out[n] = _from_shard_major(g[:, off:off + nr], blocks[n].shape, 1)
    off += nr
  small_g = g[:, off:off + small_rows].reshape(4, small_rows * FLAT_COLS)
  off = 0
  for n, axis in SMALL_SHARDED:
    sz = 2 * shards[n].size
    vals = lax.bitcast_convert_type(small_g[:, off:off + sz].reshape((4,) + shards[n].shape + (2,)), F32)
    out[n] = _from_shard_major(vals, shards[n].shape, axis)
    off += sz
  return out


def _reduce_grads(grads, shard_shapes):
  segs = []
  for n, _ in SHARDED:
    for g in grads[n]:
      segs.append(_to_shard_major(g, 0))
  big_rows = sum(sg.shape[1] for sg in segs)
  small_full = {n: jnp.stack(grads[n]) for n in SMALL_ORDER}
  small = jnp.concatenate([small_full[n].reshape(-1) for n in SMALL_ORDER])
  n_small = _round_up(small.shape[0], 4 * 2 * 8 * FLAT_COLS)
  quarter_rows = n_small // (4 * FLAT_COLS)
  segs.append(_pad_to(small, n_small).reshape(4, quarter_rows, FLAT_COLS))
  rows_pad = _round_up(big_rows + quarter_rows, 2 * ROW_ALIGN)
  if rows_pad > big_rows + quarter_rows:
    segs.append(jnp.zeros((4, rows_pad - big_rows - quarter_rows, FLAT_COLS), F32))
  p = jnp.concatenate(segs, axis=1).reshape(4, 2, rows_pad // 2, FLAT_COLS)
  mine = _reduce_scatter(p)
  out, off = {}, 0
  for n, tr in SHARDED:
    layers, rows, cols = shard_shapes[n]
    nr = layers * rows * cols // FLAT_COLS
    block = mine[off:off + nr]
    out[n] = jnp.swapaxes(block.reshape(layers, cols, rows), 1, 2) if tr else block.reshape(layers, rows, cols)
    off += nr
  piece = mine[big_rows:big_rows + quarter_rows].reshape(2, quarter_rows // 2, FLAT_COLS)
  small_sum = _all_gather_halves(piece, "gather_small_grads").reshape(n_small)
  off = 0
  for n in SMALL_ORDER:
    sz = small_full[n].size
    out[n] = small_sum[off:off + sz].reshape(small_full[n].shape)
    off += sz
  return out


WEIGHT_NAMES = ("a_w_in", "b_w_in", "sgu_ln_g", "sgu_ln_b", "sgu_w_s", "sgu_b_s", "w_mem_kv", "w_out",
                "ln_mix_g", "ln_mix_b", "w_gate", "w_up", "w_down", "ln_ffn_g", "ln_ffn_b")


def kernel(x, mem, a_w_in, b_w_in, sgu_ln_g, sgu_ln_b, sgu_w_s, sgu_b_s, w_mem_kv, w_out, ln_mix_g, ln_mix_b, w_gate, w_up, w_down, ln_ffn_g, ln_ffn_b, loss_target, m_a_w_in, m_b_w_in, m_sgu_ln_g, m_sgu_ln_b, m_sgu_w_s, m_sgu_b_s, m_w_mem_kv, m_w_out, m_ln_mix_g, m_ln_mix_b, m_w_gate, m_w_up, m_w_down, m_ln_ffn_g, m_ln_ffn_b, v_a_w_in, v_b_w_in, v_sgu_ln_g, v_sgu_ln_b, v_sgu_w_s, v_sgu_b_s, v_w_mem_kv, v_w_out, v_ln_mix_g, v_ln_mix_b, v_w_gate, v_w_up, v_w_down, v_ln_ffn_g, v_ln_ffn_b):
  weights = dict(a_w_in=a_w_in, b_w_in=b_w_in, sgu_ln_g=sgu_ln_g, sgu_ln_b=sgu_ln_b, sgu_w_s=sgu_w_s, sgu_b_s=sgu_b_s,
                 w_mem_kv=w_mem_kv, w_out=w_out, ln_mix_g=ln_mix_g, ln_mix_b=ln_mix_b, w_gate=w_gate, w_up=w_up,
                 w_down=w_down, ln_ffn_g=ln_ffn_g, ln_ffn_b=ln_ffn_b)
  mom1 = dict(a_w_in=m_a_w_in, b_w_in=m_b_w_in, sgu_ln_g=m_sgu_ln_g, sgu_ln_b=m_sgu_ln_b, sgu_w_s=m_sgu_w_s,
              sgu_b_s=m_sgu_b_s, w_mem_kv=m_w_mem_kv, w_out=m_w_out, ln_mix_g=m_ln_mix_g, ln_mix_b=m_ln_mix_b,
              w_gate=m_w_gate, w_up=m_w_up, w_down=m_w_down, ln_ffn_g=m_ln_ffn_g, ln_ffn_b=m_ln_ffn_b)
  mom2 = dict(a_w_in=v_a_w_in, b_w_in=v_b_w_in, sgu_ln_g=v_sgu_ln_g, sgu_ln_b=v_sgu_ln_b, sgu_w_s=v_sgu_w_s,
              sgu_b_s=v_sgu_b_s, w_mem_kv=v_w_mem_kv, w_out=v_w_out, ln_mix_g=v_ln_mix_g, ln_mix_b=v_ln_mix_b,
              w_gate=v_w_gate, w_up=v_w_up, w_down=v_w_down, ln_ffn_g=v_ln_ffn_g, ln_ffn_b=v_ln_ffn_b)

  full = _gather_weights(weights)
  for n in REPLICATED:
    full[n] = weights[n]
  loss_part, grad_x, grads = _local_step(x, mem, loss_target, full)
  loss = lax.psum(loss_part, MESH_AXES)

  shard_shapes = {n: weights[n].shape for n, _ in SHARDED}
  red = _reduce_grads(grads, shard_shapes)
  chip = 2 * lax.axis_index("x") + lax.axis_index("y")
  for n, axis in SMALL_SHARDED:
    width = weights[n].shape[axis]
    red[n] = lax.dynamic_slice_in_dim(red[n], chip * width, width, axis)

  small_names = SMALL_ORDER
  def pack(d):
    flat = jnp.concatenate([d[n].reshape(-1) for n in small_names])
    return _pad_to(flat, _round_up(flat.shape[0], 8 * FLAT_COLS)).reshape(-1, FLAT_COLS)
  small_out = _adamw(pack(weights), pack(red), pack(mom1), pack(mom2), "adamw_small")
  delta, new_m, new_v = {}, {}, {}
  off = 0
  for n in small_names:
    sz = weights[n].size
    for dst, src in zip((delta, new_m, new_v), small_out):
      dst[n] = src.reshape(-1)[off:off + sz].reshape(weights[n].shape)
    off += sz
  for n, _ in SHARDED:
    delta[n], new_m[n], new_v[n] = _adamw(weights[n], red[n], mom1[n], mom2[n], f"adamw_{n}")

  return (loss, grad_x, *[red[n] for n in WEIGHT_NAMES], *[delta[n] for n in WEIGHT_NAMES],
          *[new_m[n] for n in WEIGHT_NAMES], *[new_v[n] for n in WEIGHT_NAMES])
```

```python
import functools
import math

import jax
import jax.numpy as jnp
from jax import lax
from jax.experimental import pallas as pl
from jax.experimental.pallas import tpu as pltpu

F32 = jnp.float32
BF16 = jnp.bfloat16

DEPTH = 4
HEAD_DIM = 64
N_DIL_HEADS = 12
DIL_WIDTH = N_DIL_HEADS * HEAD_DIM
DIL_PATTERNS = ((128, 1), (512, 4), (2048, 16))
BLOCK = 128
N_SGU_GROUPS = 12
SGU_WIDTH = N_SGU_GROUPS * 64
CHUNK = 128
N_MEM_HEADS = 4
MEM_WIDTH = N_MEM_HEADS * HEAD_DIM
DN_ALPHA = (2 * DEPTH) ** 0.25
LN_EPS = 1e-5
ATT_SCALE = HEAD_DIM ** -0.5
ADAM_LR = 0.001
ADAM_B1 = 0.9
ADAM_B2 = 0.999
ADAM_EPS = 1e-08
ADAM_WD = 0.01
ADAM_STEP = 10
NEG_BIG = -1e30
ATTN_UNROLL = 2

LANES = 128
FLAT_COLS = 1024
VMEM_LIMIT = 56 * 1024 * 1024
MESH_AXES = ("x", "y", "c")
MESH_ID = pl.DeviceIdType.MESH


def _tile(n, pref, align=LANES):
  if n <= pref:
    return n
  t = (pref // align) * align
  while t >= align:
    if n % t == 0:
      return t
    t -= align
  return n


def _params(sem):
  return pltpu.CompilerParams(dimension_semantics=sem, vmem_limit_bytes=VMEM_LIMIT)


def _dot(a, b):
  return jnp.dot(a, b, preferred_element_type=F32)


def _dot_nt(a, b):
  return lax.dot_general(a, b, (((1,), (1,)), ((), ())), preferred_element_type=F32)


def _dot_tn(a, b):
  return lax.dot_general(a, b, (((0,), (0,)), ((), ())), preferred_element_type=F32)


def _bf(v):
  return v.astype(BF16)


def _ln_stats(z):
  mu = jnp.mean(z, axis=-1, keepdims=True)
  zc = z - mu
  var = jnp.mean(zc * zc, axis=-1, keepdims=True)
  rstd = lax.rsqrt(var + LN_EPS)
  return zc * rstd, rstd


def _ln_bwd(dy, xhat, rstd, g):
  gdy = dy * g
  m1 = jnp.mean(gdy, axis=-1, keepdims=True)
  m2 = jnp.mean(gdy * xhat, axis=-1, keepdims=True)
  return rstd * (gdy - m1 - xhat * m2)


_GELU_C = math.sqrt(2.0 / math.pi)


def _gelu_parts(v):
  v2 = v * v
  t = jnp.tanh(_GELU_C * (v + 0.044715 * v * v2))
  val = 0.5 * v * (1.0 + t)
  der = 0.5 * (1.0 + t) + 0.5 * v * (1.0 - t * t) * (_GELU_C * (1.0 + 3.0 * 0.044715 * v2))
  return val, der


def _gelu(v):
  t = jnp.tanh(_GELU_C * (v + 0.044715 * v * v * v))
  return 0.5 * v * (1.0 + t)


def _sigmoid(v):
  return 1.0 / (1.0 + jnp.exp(-v))


def _mm(a, b, mode, out_dtype, name, add=None, add_scale=1.0, tm=512, tn=512, tk=512):
  if mode == "nn":
    (m, k), (k2, n) = a.shape, b.shape
  elif mode == "nt":
    (m, k), (n, k2) = a.shape, b.shape
  else:
    (k, m), (k2, n) = a.shape, b.shape
  assert k == k2, (a.shape, b.shape, mode)
  tm, tn, tk = _tile(m, tm), _tile(n, tn), _tile(k, tk)
  nk = k // tk
  if mode == "nn":
    a_spec = pl.BlockSpec((tm, tk), lambda i, j, kk: (i, kk))
    b_spec = pl.BlockSpec((tk, tn), lambda i, j, kk: (kk, j))
    dot = _dot
  elif mode == "nt":
    a_spec = pl.BlockSpec((tm, tk), lambda i, j, kk: (i, kk))
    b_spec = pl.BlockSpec((tn, tk), lambda i, j, kk: (j, kk))
    dot = _dot_nt
  else:
    a_spec = pl.BlockSpec((tk, tm), lambda i, j, kk: (kk, i))
    b_spec = pl.BlockSpec((tk, tn), lambda i, j, kk: (kk, j))
    dot = _dot_tn
  o_spec = pl.BlockSpec((tm, tn), lambda i, j, kk: (i, j))
  has_add = add is not None

  def body(*refs):
    if has_add:
      a_ref, b_ref, add_ref, o_ref, acc_ref = refs
    else:
      a_ref, b_ref, o_ref, acc_ref = refs
    kk = pl.program_id(2)

    @pl.when(kk == 0)
    def _():
      acc_ref[...] = jnp.zeros_like(acc_ref)

    acc_ref[...] += dot(_bf(a_ref[...]), _bf(b_ref[...]))

    @pl.when(kk == nk - 1)
    def _():
      r = acc_ref[...]
      if has_add:
        r = r + add_scale * add_ref[...].astype(F32)
      o_ref[...] = r.astype(out_dtype)

  in_specs = [a_spec, b_spec] + ([o_spec] if has_add else [])
  args = (a, b) + ((add,) if has_add else ())
  return pl.pallas_call(
      body, name=name, grid=(m // tm, n // tn, nk), in_specs=in_specs, out_specs=o_spec,
      out_shape=jax.ShapeDtypeStruct((m, n), out_dtype),
      scratch_shapes=[pltpu.VMEM((tm, tn), F32)],
      compiler_params=_params(("parallel", "parallel", "arbitrary")),
  )(*args)


def _mm_res_ln(a, w, res, g, b, name, tm=512, tk=512):
  m, k = a.shape
  d = w.shape[1]
  tm, tk = _tile(m, tm), _tile(k, tk)
  nk = k // tk

  def body(a_ref, w_ref, r_ref, g_ref, b_ref, z_ref, x_ref, xb_ref, acc_ref):
    kk = pl.program_id(1)

    @pl.when(kk == 0)
    def _():
      acc_ref[...] = jnp.zeros_like(acc_ref)

    acc_ref[...] += _dot(_bf(a_ref[...]), _bf(w_ref[...]))

    @pl.when(kk == nk - 1)
    def _():
      z = DN_ALPHA * r_ref[...] + acc_ref[...]
      xhat, _ = _ln_stats(z)
      xn = xhat * g_ref[...] + b_ref[...]
      z_ref[...] = z
      x_ref[...] = xn
      xb_ref[...] = _bf(xn)

  row = pl.BlockSpec((tm, d), lambda i, kk: (i, 0))
  vec = pl.BlockSpec((1, d), lambda i, kk: (0, 0))
  return pl.pallas_call(
      body, name=name, grid=(m // tm, nk),
      in_specs=[pl.BlockSpec((tm, tk), lambda i, kk: (i, kk)), pl.BlockSpec((tk, d), lambda i, kk: (kk, 0)), row, vec, vec],
      out_specs=[row, row, row],
      out_shape=[jax.ShapeDtypeStruct((m, d), F32), jax.ShapeDtypeStruct((m, d), F32), jax.ShapeDtypeStruct((m, d), BF16)],
      scratch_shapes=[pltpu.VMEM((tm, d), F32)],
      compiler_params=_params(("parallel", "arbitrary")),
  )(a, w, res, g.reshape(1, d), b.reshape(1, d))


def _ln_bwd_call(dy, z, g, name, tm=512):
  m, d = z.shape
  tm = _tile(m, tm)
  n = m // tm

  def body(dy_ref, z_ref, g_ref, dz_ref, dzb_ref, dg_ref, db_ref):
    i = pl.program_id(0)

    @pl.when(i == 0)
    def _():
      dg_ref[...] = jnp.zeros_like(dg_ref)
      db_ref[...] = jnp.zeros_like(db_ref)

    dy_v = dy_ref[...]
    xhat, rstd = _ln_stats(z_ref[...])
    dz = _ln_bwd(dy_v, xhat, rstd, g_ref[...])
    dz_ref[...] = dz
    dzb_ref[...] = _bf(dz)
    dg_ref[...] += jnp.sum(dy_v * xhat, axis=0, keepdims=True)
    db_ref[...] += jnp.sum(dy_v, axis=0, keepdims=True)

  row = pl.BlockSpec((tm, d), lambda i: (i, 0))
  vec = pl.BlockSpec((1, d), lambda i: (0, 0))
  dz, dzb, dg, db = pl.pallas_call(
      body, name=name, grid=(n,), in_specs=[row, row, vec], out_specs=[row, row, vec, vec],
      out_shape=[jax.ShapeDtypeStruct((m, d), F32), jax.ShapeDtypeStruct((m, d), BF16),
                 jax.ShapeDtypeStruct((1, d), F32), jax.ShapeDtypeStruct((1, d), F32)],
      compiler_params=_params(("arbitrary",)),
  )(dy, z, g.reshape(1, d))
  return dz, dzb, dg[0], db[0]


def _ffn_up(xb, wg, wu, name, tm=512, tn=1408):
  m, d = xb.shape
  f = wg.shape[0]
  tm, tn = _tile(m, tm), _tile(f, tn)

  def body(x_ref, wg_ref, wu_ref, a_ref, b_ref, h_ref):
    xv = x_ref[...]
    a = _dot_nt(xv, wg_ref[...])
    b = _dot_nt(xv, wu_ref[...])
    a_ref[...] = _bf(a)
    b_ref[...] = _bf(b)
    h_ref[...] = _bf(a * _sigmoid(a) * b)

  wspec = pl.BlockSpec((tn, d), lambda j, i: (j, 0))
  ospec = pl.BlockSpec((tm, tn), lambda j, i: (i, j))
  sds = jax.ShapeDtypeStruct((m, f), BF16)
  return pl.pallas_call(
      body, name=name, grid=(f // tn, m // tm),
      in_specs=[pl.BlockSpec((tm, d), lambda j, i: (i, 0)), wspec, wspec],
      out_specs=[ospec, ospec, ospec], out_shape=[sds, sds, sds],
      compiler_params=_params(("parallel", "parallel")),
  )(xb, wg, wu)


def _ffn_bwd_hidden(dzb, wd, a, b, name, tm=512, tn=1408):
  m, d = dzb.shape
  f = wd.shape[0]
  tm, tn = _tile(m, tm), _tile(f, tn)

  def body(dz_ref, wd_ref, a_ref, b_ref, da_ref, db_ref):
    dh = _dot_nt(dz_ref[...], wd_ref[...])
    av = a_ref[...].astype(F32)
    bv = b_ref[...].astype(F32)
    sg = _sigmoid(av)
    da_ref[...] = _bf(dh * bv * (sg * (1.0 + av * (1.0 - sg))))
    db_ref[...] = _bf(dh * (av * sg))

  hspec = pl.BlockSpec((tm, tn), lambda j, i: (i, j))
  sds = jax.ShapeDtypeStruct((m, f), BF16)
  return pl.pallas_call(
      body, name=name, grid=(f // tn, m // tm),
      in_specs=[pl.BlockSpec((tm, d), lambda j, i: (i, 0)), pl.BlockSpec((tn, d), lambda j, i: (j, 0)), hspec, hspec],
      out_specs=[hspec, hspec], out_shape=[sds, sds],
      compiler_params=_params(("parallel", "parallel")),
  )(dzb, wd, a, b)


def _ln_bwd_tail(dy, z_ref, g_ref, dz_ref, dzb_ref, dg_ref, db_ref):
  @pl.when(pl.program_id(0) == 0)
  def _():
    dg_ref[...] = jnp.zeros_like(dg_ref)
    db_ref[...] = jnp.zeros_like(db_ref)

  xhat, rstd = _ln_stats(z_ref[...])
  dz = _ln_bwd(dy, xhat, rstd, g_ref[...])
  dz_ref[...] = dz
  dzb_ref[...] = _bf(dz)
  dg_ref[...] += jnp.sum(dy * xhat, axis=0, keepdims=True)
  db_ref[...] += jnp.sum(dy, axis=0, keepdims=True)


def _ln_bwd_outs(m, d, row, vec):
  return ([row, row, vec, vec],
          [jax.ShapeDtypeStruct((m, d), F32), jax.ShapeDtypeStruct((m, d), BF16),
           jax.ShapeDtypeStruct((1, d), F32), jax.ShapeDtypeStruct((1, d), F32)])


def _ffn_bwd_input_ln(da, db, wg, wu, dz2, z1, g, name, tm=512):
  m, f = da.shape
  d = wg.shape[1]
  tm = _tile(m, tm)

  def body(da_ref, db_ref, wg_ref, wu_ref, dz2_ref, z_ref, g_ref, dz_ref, dzb_ref, dg_ref, dbias_ref):
    dy = DN_ALPHA * dz2_ref[...] + _dot(da_ref[...], wg_ref[...]) + _dot(db_ref[...], wu_ref[...])
    _ln_bwd_tail(dy, z_ref, g_ref, dz_ref, dzb_ref, dg_ref, dbias_ref)

  hspec = pl.BlockSpec((tm, f), lambda i: (i, 0))
  wspec = pl.BlockSpec((f, d), lambda i: (0, 0), pipeline_mode=pl.Buffered(1))
  row = pl.BlockSpec((tm, d), lambda i: (i, 0))
  vec = pl.BlockSpec((1, d), lambda i: (0, 0))
  out_specs, out_shape = _ln_bwd_outs(m, d, row, vec)
  dz, dzb, dg, dbias = pl.pallas_call(
      body, name=name, grid=(m // tm,), in_specs=[hspec, hspec, wspec, wspec, row, row, vec],
      out_specs=out_specs, out_shape=out_shape, compiler_params=_params(("arbitrary",)),
  )(da, db, wg, wu, dz2, z1, g.reshape(1, d))
  return dz, dzb, dg[0], dbias[0]


def _in_proj_bwd_ln(dh, w_in, dz1, z2, g, name, tm=512):
  m, wd = dh.shape
  d = w_in.shape[1]
  tm = _tile(m, tm)

  def body(dh_ref, w_ref, dz1_ref, z_ref, g_ref, dz_ref, dzb_ref, dg_ref, dbias_ref):
    dy = DN_ALPHA * dz1_ref[...] + _dot(dh_ref[...], w_ref[...])
    _ln_bwd_tail(dy, z_ref, g_ref, dz_ref, dzb_ref, dg_ref, dbias_ref)

  row = pl.BlockSpec((tm, d), lambda i: (i, 0))
  vec = pl.BlockSpec((1, d), lambda i: (0, 0))
  out_specs, out_shape = _ln_bwd_outs(m, d, row, vec)
  dz, dzb, dg, dbias = pl.pallas_call(
      body, name=name, grid=(m // tm,),
      in_specs=[pl.BlockSpec((tm, wd), lambda i: (i, 0)),
                pl.BlockSpec((wd, d), lambda i: (0, 0), pipeline_mode=pl.Buffered(1)), row, row, vec],
      out_specs=out_specs, out_shape=out_shape, compiler_params=_params(("arbitrary",)),
  )(dh, w_in, dz1, z2, g.reshape(1, d))
  return dz, dzb, dg[0], dbias[0]


def _alibi_slopes():
  n = N_DIL_HEADS
  return jnp.exp2(-8.0 * (jnp.arange(n, dtype=F32) + 1.0) / n).reshape(1, n)


def _band_consts():
  qi = lax.broadcasted_iota(jnp.int32, (BLOCK, BLOCK), 0)
  ki = lax.broadcasted_iota(jnp.int32, (BLOCK, BLOCK), 1)
  steps_cur = (qi - ki).astype(F32)
  steps_prev = (qi + BLOCK - ki).astype(F32)
  return ki < 64, steps_cur, steps_prev, ki <= qi, ki >= qi


def _rows(start, d):
  if d == 1:
    return pl.ds(pl.multiple_of(start, BLOCK), BLOCK)
  return pl.ds(start, BLOCK, stride=d)


def _fill_bias_tables(bias_sc, slope0, slope1):
  row = lax.broadcasted_iota(jnp.int32, (2 * BLOCK, 2 * BLOCK), 0)
  col = lax.broadcasted_iota(jnp.int32, (2 * BLOCK, 2 * BLOCK), 1)
  qi = jnp.bitwise_and(row, BLOCK - 1)
  ki = jnp.bitwise_and(col, BLOCK - 1)
  is_cur = col >= BLOCK
  steps = jnp.where(is_cur, qi - ki, qi + BLOCK - ki)
  valid = jnp.logical_and(steps >= 0, steps <= BLOCK)
  slope = jnp.where(row >= BLOCK, slope1, slope0)
  dist = slope * steps.astype(F32)
  for p, (_, d) in enumerate(DIL_PATTERNS):
    base = jnp.where(valid, -d * dist, NEG_BIG)
    bias_sc[2 * p] = base
    bias_sc[2 * p + 1] = jnp.where(is_cur, base, NEG_BIG)


def _stack_heads(v2, head0):
  return jnp.concatenate([jnp.where(head0, v2, 0.0), jnp.where(head0, 0.0, v2)], axis=0)


def _unstack_heads(v, head0):
  return jnp.where(head0, v[:BLOCK], v[BLOCK:])


def _block_rows(idx, d, nblk):
  r = idx // nblk
  n = idx % nblk
  cur = _rows(r + n * (BLOCK * d), d)
  prev = _rows(r + jnp.maximum(n - 1, 0) * (BLOCK * d), d)
  return cur, prev, n


def pair_tile(dt):
  return pltpu.VMEM((2 * BLOCK, 2 * BLOCK), dt)


def _two_stage_loop(nb, first_stage, second_stage, buf_a, buf_b):
  assert nb % 2 == 0

  def pair(t, carry):
    i = 2 * t + 1
    first_stage(i, buf_b)
    second_stage(i - 1, buf_a)
    first_stage(i + 1, buf_a)
    second_stage(i, buf_b)
    return carry

  first_stage(0, buf_a)
  lax.fori_loop(0, nb // 2 - 1, pair, 0)
  first_stage(nb - 1, buf_b)
  second_stage(nb - 2, buf_a)
  second_stage(nb - 1, buf_b)


def _attn_fwd(h3, name):
  bl, s, _ = h3.shape
  npair = N_DIL_HEADS // 2
  nb = s // BLOCK

  def body(sl_ref, q_ref, k_ref, v_ref, o_ref, lse_ref, o_sc, l_sc, bias_sc, s_a, s_b):
    hp = pl.program_id(1)
    head0 = lax.broadcasted_iota(jnp.int32, (BLOCK, LANES), 1) < 64
    _fill_bias_tables(bias_sc, sl_ref[0, 2 * hp], sl_ref[0, 2 * hp + 1])

    for p, (_, d) in enumerate(DIL_PATTERNS):
      nblk = (s // d) // BLOCK
      two = nblk > 1
      ks = slice(0, 2 * BLOCK) if two else slice(BLOCK, 2 * BLOCK)

      def scores(idx, buf, p=p, d=d, nblk=nblk, two=two, ks=ks):
        cur, prev, n = _block_rows(idx, d, nblk)
        qs = _bf(_stack_heads(q_ref[cur, :], head0) * ATT_SCALE)
        kb = _bf(jnp.concatenate([k_ref[prev, :], k_ref[cur, :]], axis=0)) if two else _bf(k_ref[cur, :])
        first = jnp.where(n == 0, 1, 0) if two else 0
        buf[:, ks] = _dot_nt(qs, kb) + bias_sc[2 * p + first, :, ks]

      def values(idx, buf, p=p, d=d, nblk=nblk, two=two, ks=ks):
        cur, prev, _ = _block_rows(idx, d, nblk)
        sc = buf[:, ks]
        mx = jnp.max(sc, axis=1, keepdims=True)
        pe = jnp.exp(sc - mx)
        den = jnp.sum(pe, axis=1, keepdims=True)
        vb = _bf(jnp.concatenate([v_ref[prev, :], v_ref[cur, :]], axis=0)) if two else _bf(v_ref[cur, :])
        acc = _dot(_bf(pe), vb) / den
        o_sc[p, cur, :] = _unstack_heads(acc, head0)
        l_sc[p, cur, :] = _unstack_heads(jnp.broadcast_to(mx + jnp.log(den), (2 * BLOCK, LANES)), head0)

      _two_stage_loop(nb, scores, values, s_a, s_b)

    def merge(i, carry):
      rows = pl.ds(pl.multiple_of(i * BLOCK, BLOCK), BLOCK)
      l0, l1, l2 = l_sc[0, rows, :], l_sc[1, rows, :], l_sc[2, rows, :]
      mx = jnp.maximum(jnp.maximum(l0, l1), l2)
      e0, e1, e2 = jnp.exp(l0 - mx), jnp.exp(l1 - mx), jnp.exp(l2 - mx)
      tot = e0 + e1 + e2
      o_ref[rows, :] = _bf((e0 * o_sc[0, rows, :] + e1 * o_sc[1, rows, :] + e2 * o_sc[2, rows, :]) / tot)
      lse_ref[rows, :] = mx + jnp.log(tot)
      return carry

    lax.fori_loop(0, nb, merge, 0)

  def col(off):
    return pl.BlockSpec((None, s, LANES), lambda b, p: (b, 0, off + p))

  return pl.pallas_call(
      body, name=name, grid=(bl, npair),
      in_specs=[pl.BlockSpec(memory_space=pltpu.SMEM), col(0), col(npair), col(2 * npair)],
      out_specs=[col(0), col(0)],
      out_shape=[jax.ShapeDtypeStruct((bl, s, DIL_WIDTH), BF16), jax.ShapeDtypeStruct((bl, s, DIL_WIDTH), F32)],
      scratch_shapes=[pltpu.VMEM((3, s, LANES), F32), pltpu.VMEM((3, s, LANES), F32),
                      pltpu.VMEM((6, 2 * BLOCK, 2 * BLOCK), F32), pair_tile(F32), pair_tile(F32)],
      compiler_params=_params(("parallel", "parallel")),
  )(_alibi_slopes(), h3, h3, h3)


def _attn_bwd(h3, out3, lse3, dcat3, name, exchange=None):
  bl, s, _ = h3.shape
  npair = N_DIL_HEADS // 2
  nb = s // BLOCK
  hosted = exchange is not None

  def body(*refs):
    if hosted:
      (sl_ref, q_ref, k_ref, v_ref, o_ref, l_ref, do_ref, ex_ref, dq_out, dk_out, dv_out, got_ref,
       bias_sc, p_a, ds_a, p_b, ds_b, prod_sc, dq_ref, dk_ref, dv_ref, send_sems, recv_sems) = refs
      step = pl.program_id(0) * npair + pl.program_id(1)

      @pl.when(step == 0)
      def _():
        for cp in _chip_exchange_copies(ex_ref, got_ref, send_sems, recv_sems):
          cp.start()
    else:
      (sl_ref, q_ref, k_ref, v_ref, o_ref, l_ref, do_ref, dq_out, dk_out, dv_out,
       bias_sc, p_a, ds_a, p_b, ds_b, prod_sc, dq_ref, dk_ref, dv_ref) = refs
    hp = pl.program_id(1)
    lane = lax.broadcasted_iota(jnp.int32, (BLOCK, LANES), 1)
    head0 = lane < 64
    _fill_bias_tables(bias_sc, sl_ref[0, 2 * hp], sl_ref[0, 2 * hp + 1])
    dq_ref[...] = jnp.zeros_like(dq_ref)
    dk_ref[...] = jnp.zeros_like(dk_ref)
    dv_ref[...] = jnp.zeros_like(dv_ref)
    prod_sc[...] = do_ref[...] * o_ref[...].astype(F32)

    def per_row(v2, pick0, pick1):
      return jnp.concatenate([jnp.sum(jnp.where(pick0, v2, 0.0), axis=1, keepdims=True),
                              jnp.sum(jnp.where(pick1, v2, 0.0), axis=1, keepdims=True)], axis=0)

    for p, (_, d) in enumerate(DIL_PATTERNS):
      nblk = (s // d) // BLOCK
      two = nblk > 1
      ks = slice(0, 2 * BLOCK) if two else slice(BLOCK, 2 * BLOCK)

      def operands(idx, d=d, nblk=nblk, two=two):
        cur, prev, n = _block_rows(idx, d, nblk)
        qs = _bf(_stack_heads(q_ref[cur, :], head0) * ATT_SCALE)
        dos = _bf(_stack_heads(do_ref[cur, :], head0))
        kb = _bf(jnp.concatenate([k_ref[prev, :], k_ref[cur, :]], axis=0)) if two else _bf(k_ref[cur, :])
        return cur, prev, n, qs, dos, kb

      def probs(idx, bufs, p=p, two=two, ks=ks, operands=operands):
        cur, prev, n, qs, dos, kb = operands(idx)
        vb = _bf(jnp.concatenate([v_ref[prev, :], v_ref[cur, :]], axis=0)) if two else _bf(v_ref[cur, :])
        lse = per_row(l_ref[cur, :], lane == 0, lane == 64)
        delta = per_row(prod_sc[cur, :], head0, jnp.logical_not(head0))
        first = jnp.where(n == 0, 1, 0) if two else 0
        pr = jnp.exp(_dot_nt(qs, kb) + bias_sc[2 * p + first, :, ks] - lse)
        bufs[0][:, ks] = _bf(pr)
        bufs[1][:, ks] = _bf(pr * (_dot_nt(dos, vb) - delta))

      def products(idx, bufs, two=two, ks=ks, operands=operands):
        cur, prev, _, qs, dos, kb = operands(idx)
        pr = bufs[0][:, ks]
        ds = bufs[1][:, ks]
        dq_ref[cur, :] += _unstack_heads(_dot(ds, kb), head0) * ATT_SCALE
        dkb = _dot_tn(ds, qs)
        dvb = _dot_tn(pr, dos)
        if two:
          dk_ref[prev, :] += dkb[:BLOCK]
          dv_ref[prev, :] += dvb[:BLOCK]
          dk_ref[cur, :] += dkb[BLOCK:]
          dv_ref[cur, :] += dvb[BLOCK:]
        else:
          dk_ref[cur, :] += dkb
          dv_ref[cur, :] += dvb

      _two_stage_loop(nb, probs, products, (p_a, ds_a), (p_b, ds_b))

    dq_out[...] = _bf(dq_ref[...])
    dk_out[...] = _bf(dk_ref[...])
    dv_out[...] = _bf(dv_ref[...])

    if hosted:
      @pl.when(step == bl * npair - 1)
      def _():
        for cp in _chip_exchange_copies(ex_ref, got_ref, send_sems, recv_sems):
          cp.wait()

  def col(off):
    return pl.BlockSpec((None, s, LANES), lambda b, p: (b, 0, off + p))

  sds = jax.ShapeDtypeStruct((bl, s, DIL_WIDTH), BF16)
  in_specs = [pl.BlockSpec(memory_space=pltpu.SMEM), col(0), col(npair), col(2 * npair), col(0), col(0), col(0)]
  out_specs, out_shape = [col(0), col(0), col(0)], [sds, sds, sds]
  scratch = [pltpu.VMEM((6, 2 * BLOCK, 2 * BLOCK), F32)] + [pair_tile(BF16)] * 4 + [pltpu.VMEM((s, LANES), F32)] * 4
  args = (_alibi_slopes(), h3, h3, h3, out3, lse3, dcat3)
  if hosted:
    in_specs.append(ANY)
    out_specs.append(ANY)
    out_shape.append(jax.ShapeDtypeStruct((3,) + exchange.shape[1:], exchange.dtype))
    scratch += [pltpu.SemaphoreType.DMA((3,)), pltpu.SemaphoreType.DMA((3,))]
    args += (exchange,)
  sem = ("arbitrary", "arbitrary") if hosted else ("parallel", "parallel")
  return pl.pallas_call(
      body, name=name, grid=(bl, npair), in_specs=in_specs, out_specs=out_specs, out_shape=out_shape,
      scratch_shapes=scratch, compiler_params=_params(sem),
  )(*args)


def _attn_fwd_old(h3, name):
  bl, s, _ = h3.shape
  npair = N_DIL_HEADS // 2

  def body(sl_ref, q_ref, k_ref, v_ref, o_ref, lse_ref, o_sc, l_sc):
    hp = pl.program_id(1)
    head0, steps_cur, steps_prev, mask_cur, mask_prev = _band_consts()
    slope = [sl_ref[0, 2 * hp], sl_ref[0, 2 * hp + 1]]

    for p, (_, d) in enumerate(DIL_PATTERNS):
      nblk = (s // d) // BLOCK
      has_prev_block = nblk > 1

      def blk(idx, carry, p=p, d=d, nblk=nblk, has_prev_block=has_prev_block):
        r = idx // nblk
        n = idx % nblk
        cur = _rows(r + n * (BLOCK * d), d)
        q2 = q_ref[cur, :]
        kc = _bf(k_ref[cur, :])
        vc = _bf(v_ref[cur, :])
        if has_prev_block:
          prev = _rows(r + jnp.maximum(n - 1, 0) * (BLOCK * d), d)
          kp = _bf(k_ref[prev, :])
          vp = _bf(v_ref[prev, :])
          first_block = jnp.where(n > 0, 0.0, NEG_BIG)
        outs, lses = [], []
        for j in range(2):
          hm = head0 if j == 0 else jnp.logical_not(head0)
          qj = _bf(jnp.where(hm, q2, 0.0) * ATT_SCALE)
          sc = _dot_nt(qj, kc) - (slope[j] * d) * steps_cur
          sc = jnp.where(mask_cur, sc, NEG_BIG)
          mx = jnp.max(sc, axis=1, keepdims=True)
          if has_prev_block:
            sp = _dot_nt(qj, kp) - (slope[j] * d) * steps_prev + first_block
            sp = jnp.where(mask_prev, sp, NEG_BIG)
            mx = jnp.maximum(mx, jnp.max(sp, axis=1, keepdims=True))
          pc = jnp.exp(sc - mx)
          den = jnp.sum(pc, axis=1, keepdims=True)
          acc = _dot(_bf(pc), vc)
          if has_prev_block:
            pp = jnp.exp(sp - mx)
            den = den + jnp.sum(pp, axis=1, keepdims=True)
            acc = acc + _dot(_bf(pp), vp)
          outs.append(acc / den)
          lses.append(mx + jnp.log(den))
        o_sc[p, cur, :] = jnp.where(head0, outs[0], outs[1])
        l_sc[p, cur, :] = jnp.where(head0, lses[0], lses[1])
        return carry

      lax.fori_loop(0, s // BLOCK, blk, 0, unroll=ATTN_UNROLL)

    def merge(i, carry):
      rows = pl.ds(pl.multiple_of(i * BLOCK, BLOCK), BLOCK)
      l0, l1, l2 = l_sc[0, rows, :], l_sc[1, rows, :], l_sc[2, rows, :]
      mx = jnp.maximum(jnp.maximum(l0, l1), l2)
      e0, e1, e2 = jnp.exp(l0 - mx), jnp.exp(l1 - mx), jnp.exp(l2 - mx)
      tot = e0 + e1 + e2
      o_ref[rows, :] = (e0 * o_sc[0, rows, :] + e1 * o_sc[1, rows, :] + e2 * o_sc[2, rows, :]) / tot
      lse_ref[rows, :] = mx + jnp.log(tot)
      return carry

    lax.fori_loop(0, s // BLOCK, merge, 0)

  def col(off):
    return pl.BlockSpec((None, s, LANES), lambda b, p: (b, 0, off + p))

  sds = jax.ShapeDtypeStruct((bl, s, DIL_WIDTH), F32)
  return pl.pallas_call(
      body, name=name, grid=(bl, npair),
      in_specs=[pl.BlockSpec(memory_space=pltpu.SMEM), col(0), col(npair), col(2 * npair)],
      out_specs=[col(0), col(0)], out_shape=[sds, sds],
      scratch_shapes=[pltpu.VMEM((3, s, LANES), F32), pltpu.VMEM((3, s, LANES), F32)],
      compiler_params=_params(("parallel", "parallel")),
  )(_alibi_slopes(), h3, h3, h3)


def _attn_bwd_old(h3, out3, lse3, dcat3, name):
  bl, s, _ = h3.shape
  npair = N_DIL_HEADS // 2

  def body(sl_ref, q_ref, k_ref, v_ref, o_ref, l_ref, do_ref, dq_ref, dk_ref, dv_ref):
    hp = pl.program_id(1)
    head0, steps_cur, steps_prev, mask_cur, mask_prev = _band_consts()
    lane = lax.broadcasted_iota(jnp.int32, (BLOCK, LANES), 1)
    slope = [sl_ref[0, 2 * hp], sl_ref[0, 2 * hp + 1]]
    dq_ref[...] = jnp.zeros_like(dq_ref)
    dk_ref[...] = jnp.zeros_like(dk_ref)
    dv_ref[...] = jnp.zeros_like(dv_ref)

    for p, (_, d) in enumerate(DIL_PATTERNS):
      nblk = (s // d) // BLOCK
      has_prev_block = nblk > 1

      def blk(idx, carry, d=d, nblk=nblk, has_prev_block=has_prev_block):
        r = idx // nblk
        n = idx % nblk
        cur = _rows(r + n * (BLOCK * d), d)
        q2 = q_ref[cur, :]
        do2 = do_ref[cur, :]
        l2 = l_ref[cur, :]
        prod = do2 * o_ref[cur, :]
        kc = _bf(k_ref[cur, :])
        vc = _bf(v_ref[cur, :])
        if has_prev_block:
          prev = _rows(r + jnp.maximum(n - 1, 0) * (BLOCK * d), d)
          kp = _bf(k_ref[prev, :])
          vp = _bf(v_ref[prev, :])
          first_block = jnp.where(n > 0, 0.0, NEG_BIG)
          dkp = jnp.zeros((BLOCK, LANES), F32)
          dvp = jnp.zeros((BLOCK, LANES), F32)
        dq2 = jnp.zeros((BLOCK, LANES), F32)
        dkc = jnp.zeros((BLOCK, LANES), F32)
        dvc = jnp.zeros((BLOCK, LANES), F32)
        for j in range(2):
          hm = head0 if j == 0 else jnp.logical_not(head0)
          qj = _bf(jnp.where(hm, q2, 0.0) * ATT_SCALE)
          doj = _bf(jnp.where(hm, do2, 0.0))
          lj = jnp.sum(jnp.where(lane == 64 * j, l2, 0.0), axis=1, keepdims=True)
          dj = jnp.sum(jnp.where(hm, prod, 0.0), axis=1, keepdims=True)
          sc = _dot_nt(qj, kc) - (slope[j] * d) * steps_cur
          pc = jnp.exp(jnp.where(mask_cur, sc - lj, NEG_BIG))
          dsc = _bf(pc * (_dot_nt(doj, vc) - dj))
          dq_j = _dot(dsc, kc)
          dkc = dkc + _dot_tn(dsc, qj)
          dvc = dvc + _dot_tn(_bf(pc), doj)
          if has_prev_block:
            sp = _dot_nt(qj, kp) - (slope[j] * d) * steps_prev + first_block
            pp = jnp.exp(jnp.where(mask_prev, sp - lj, NEG_BIG))
            dsp = _bf(pp * (_dot_nt(doj, vp) - dj))
            dq_j = dq_j + _dot(dsp, kp)
            dkp = dkp + _dot_tn(dsp, qj)
            dvp = dvp + _dot_tn(_bf(pp), doj)
          dq2 = dq2 + jnp.where(hm, dq_j, 0.0) * ATT_SCALE
        dq_ref[cur, :] += dq2
        dk_ref[cur, :] += dkc
        dv_ref[cur, :] += dvc
        if has_prev_block:
          dk_ref[prev, :] += dkp
          dv_ref[prev, :] += dvp
        return carry

      lax.fori_loop(0, s // BLOCK, blk, 0, unroll=ATTN_UNROLL)

  def col(off):
    return pl.BlockSpec((None, s, LANES), lambda b, p: (b, 0, off + p))

  sds = jax.ShapeDtypeStruct((bl, s, DIL_WIDTH), F32)
  return pl.pallas_call(
      body, name=name, grid=(bl, npair),
      in_specs=[pl.BlockSpec(memory_space=pltpu.SMEM), col(0), col(npair), col(2 * npair), col(0), col(0), col(0)],
      out_specs=[col(0), col(0), col(0)], out_shape=[sds, sds, sds],
      compiler_params=_params(("parallel", "parallel")),
  )(_alibi_slopes(), h3, h3, h3, out3, lse3, dcat3)


def _mem_heads(tq):
  lane = lax.broadcasted_iota(jnp.int32, (tq, LANES), 1)
  return lane < 64


def _mem_fwd(h3, qcol, mkv3, name, tq=512):
  bl, s, _ = h3.shape
  nm = mkv3.shape[1]
  tq = _tile(s, tq)

  def body(q_ref, kv_ref, o_ref):
    head0 = _mem_heads(tq)
    for lg in range(MEM_WIDTH // LANES):
      cs = slice(lg * LANES, (lg + 1) * LANES)
      q2 = q_ref[:, cs]
      mk = _bf(kv_ref[:, cs])
      mv = _bf(kv_ref[:, MEM_WIDTH + lg * LANES:MEM_WIDTH + (lg + 1) * LANES])
      outs = []
      for j in range(2):
        hm = head0 if j == 0 else jnp.logical_not(head0)
        qj = _bf(jnp.where(hm, q2, 0.0) * ATT_SCALE)
        sc = _dot_nt(qj, mk)
        mx = jnp.max(sc, axis=1, keepdims=True)
        pe = jnp.exp(sc - mx)
        den = jnp.sum(pe, axis=1, keepdims=True)
        outs.append(_dot(_bf(pe / den), mv))
      o_ref[:, cs] = _bf(jnp.where(head0, outs[0], outs[1]))

  return pl.pallas_call(
      body, name=name, grid=(bl, s // tq),
      in_specs=[pl.BlockSpec((None, tq, MEM_WIDTH), lambda b, i: (b, i, qcol)),
                pl.BlockSpec((None, nm, 2 * MEM_WIDTH), lambda b, i: (b, 0, 0))],
      out_specs=pl.BlockSpec((None, tq, MEM_WIDTH), lambda b, i: (b, i, 0)),
      out_shape=jax.ShapeDtypeStruct((bl, s, MEM_WIDTH), BF16),
      compiler_params=_params(("parallel", "parallel")),
  )(h3, mkv3)


def _mem_bwd(h3, qcol, mkv3, dcat3, name, tq=512):
  bl, s, _ = h3.shape
  nm = mkv3.shape[1]
  tq = _tile(s, tq)
  docol = dcat3.shape[2] // MEM_WIDTH - 1

  def body(q_ref, kv_ref, do_ref, dq_ref, dkv_ref):
    i = pl.program_id(1)

    @pl.when(i == 0)
    def _():
      dkv_ref[...] = jnp.zeros_like(dkv_ref)

    head0 = _mem_heads(tq)
    for lg in range(MEM_WIDTH // LANES):
      cs = slice(lg * LANES, (lg + 1) * LANES)
      vs = slice(MEM_WIDTH + lg * LANES, MEM_WIDTH + (lg + 1) * LANES)
      q2 = q_ref[:, cs]
      do2 = do_ref[:, cs]
      mk = _bf(kv_ref[:, cs])
      mv = _bf(kv_ref[:, vs])
      dq2 = jnp.zeros((tq, LANES), F32)
      dmk = jnp.zeros((nm, LANES), F32)
      dmv = jnp.zeros((nm, LANES), F32)
      for j in range(2):
        hm = head0 if j == 0 else jnp.logical_not(head0)
        qj = _bf(jnp.where(hm, q2, 0.0) * ATT_SCALE)
        doj = _bf(jnp.where(hm, do2, 0.0))
        sc = _dot_nt(qj, mk)
        mx = jnp.max(sc, axis=1, keepdims=True)
        pe = jnp.exp(sc - mx)
        pn = pe / jnp.sum(pe, axis=1, keepdims=True)
        pb = _bf(pn)
        dp = _dot_nt(doj, mv)
        dj = jnp.sum(pb.astype(F32) * dp, axis=1, keepdims=True)
        ds = _bf(pn * (dp - dj))
        dq2 = dq2 + jnp.where(hm, _dot(ds, mk), 0.0) * ATT_SCALE
        dmk = dmk + _dot_tn(ds, qj)
        dmv = dmv + _dot_tn(pb, doj)
      dq_ref[:, cs] = _bf(dq2)
      dkv_ref[:, cs] += dmk
      dkv_ref[:, vs] += dmv

  return pl.pallas_call(
      body, name=name, grid=(bl, s // tq),
      in_specs=[pl.BlockSpec((None, tq, MEM_WIDTH), lambda b, i: (b, i, qcol)),
                pl.BlockSpec((None, nm, 2 * MEM_WIDTH), lambda b, i: (b, 0, 0)),
                pl.BlockSpec((None, tq, MEM_WIDTH), lambda b, i: (b, i, docol))],
      out_specs=[pl.BlockSpec((None, tq, MEM_WIDTH), lambda b, i: (b, i, 0)),
                 pl.BlockSpec((None, nm, 2 * MEM_WIDTH), lambda b, i: (b, 0, 0))],
      out_shape=[jax.ShapeDtypeStruct((bl, s, MEM_WIDTH), BF16), jax.ShapeDtypeStruct((bl, nm, 2 * MEM_WIDTH), F32)],
      compiler_params=_params(("parallel", "arbitrary")),
  )(h3, mkv3, dcat3)


def _sgu_consts():
  ti = lax.broadcasted_iota(jnp.int32, (CHUNK, CHUNK), 0)
  si = lax.broadcasted_iota(jnp.int32, (CHUNK, CHUNK), 1)
  return si <= ti, si < 64


def _sgu_bias_lanes(b_s):
  return jnp.repeat(b_s.T, 64, axis=1)


def _sgu_fwd(h2, ln_g, ln_b, w_s, b_s, name, tr=512):
  t, _ = h2.shape
  tr = _tile(t, tr)
  nch = tr // CHUNK
  npair = N_SGU_GROUPS // 2

  def body(u_ref, v_ref, g_ref, b_ref, w_ref, bs_ref, o_ref, vn_sc):
    tril, head0 = _sgu_consts()
    xhat, _ = _ln_stats(_gelu(v_ref[...]))
    vn_sc[...] = _bf(xhat * g_ref[...] + b_ref[...])
    for jp in range(npair):
      cs = slice(jp * LANES, (jp + 1) * LANES)
      w0 = _bf(jnp.where(tril, w_ref[2 * jp], 0.0))
      w1 = _bf(jnp.where(tril, w_ref[2 * jp + 1], 0.0))
      bias = bs_ref[:, cs]
      for c in range(nch):
        rs = slice(c * CHUNK, (c + 1) * CHUNK)
        vb = vn_sc[rs, cs]
        mixed = jnp.where(head0, _dot(w0, vb), _dot(w1, vb)) + bias
        o_ref[rs, cs] = _bf(_gelu(u_ref[rs, cs]) * mixed)

  blk = lambda j: pl.BlockSpec((tr, SGU_WIDTH), lambda i: (i, j))
  vec = pl.BlockSpec((1, SGU_WIDTH), lambda i: (0, 0))
  return pl.pallas_call(
      body, name=name, grid=(t // tr,),
      in_specs=[blk(0), blk(1), vec, vec,
                pl.BlockSpec((N_SGU_GROUPS, CHUNK, CHUNK), lambda i: (0, 0, 0)),
                pl.BlockSpec((CHUNK, SGU_WIDTH), lambda i: (0, 0))],
      out_specs=blk(0), out_shape=jax.ShapeDtypeStruct((t, SGU_WIDTH), BF16),
      scratch_shapes=[pltpu.VMEM((tr, SGU_WIDTH), BF16)],
      compiler_params=_params(("parallel",)),
  )(h2, h2, ln_g.reshape(1, -1), ln_b.reshape(1, -1), w_s, _sgu_bias_lanes(b_s))


def _sgu_bwd(h2, dcat, ln_g, ln_b, w_s, b_s, name, tr=512):
  t, _ = h2.shape
  tr = _tile(t, tr)
  nch = tr // CHUNK
  npair = N_SGU_GROUPS // 2
  nsteps = t // tr

  def body(u_ref, v_ref, dm_ref, g_ref, b_ref, w_ref, bs_ref,
           du_ref, dv_ref, dw_ref, dbs_ref, dg_ref, db_ref, vn_sc, dmx_sc, dvn_sc, mix_sc, dbx_sc):
    i = pl.program_id(0)
    tril, head0 = _sgu_consts()

    @pl.when(i == 0)
    def _():
      dw_ref[...] = jnp.zeros_like(dw_ref)
      dg_ref[...] = jnp.zeros_like(dg_ref)
      db_ref[...] = jnp.zeros_like(db_ref)
      dbx_sc[...] = jnp.zeros_like(dbx_sc)

    gv, gv_der = _gelu_parts(v_ref[...])
    xhat, rstd = _ln_stats(gv)
    g = g_ref[...]
    vn_sc[...] = _bf(xhat * g + b_ref[...])
    gu, gu_der = _gelu_parts(u_ref[...])
    dmix = dm_ref[...]
    dmx_sc[...] = dmix * gu

    for jp in range(npair):
      cs = slice(jp * LANES, (jp + 1) * LANES)
      w0 = _bf(jnp.where(tril, w_ref[2 * jp], 0.0))
      w1 = _bf(jnp.where(tril, w_ref[2 * jp + 1], 0.0))
      bias = bs_ref[:, cs]
      dw0 = jnp.zeros((CHUNK, CHUNK), F32)
      dw1 = jnp.zeros((CHUNK, CHUNK), F32)
      dbx = jnp.zeros((CHUNK, LANES), F32)
      for c in range(nch):
        rs = slice(c * CHUNK, (c + 1) * CHUNK)
        vb = vn_sc[rs, cs]
        mix_sc[rs, cs] = jnp.where(head0, _dot(w0, vb), _dot(w1, vb)) + bias
        dmx = dmx_sc[rs, cs]
        d0 = _bf(jnp.where(head0, dmx, 0.0))
        d1 = _bf(jnp.where(head0, 0.0, dmx))
        dvn_sc[rs, cs] = _dot_tn(w0, d0) + _dot_tn(w1, d1)
        dw0 = dw0 + _dot_nt(d0, vb)
        dw1 = dw1 + _dot_nt(d1, vb)
        dbx = dbx + dmx
      dw_ref[2 * jp] += dw0
      dw_ref[2 * jp + 1] += dw1
      dbx_sc[:, cs] += dbx

    du_ref[...] = _bf(dmix * mix_sc[...] * gu_der)
    dvn = dvn_sc[...]
    dv_ref[...] = _bf(_ln_bwd(dvn, xhat, rstd, g) * gv_der)
    dg_ref[...] += jnp.sum(dvn * xhat, axis=0, keepdims=True)
    db_ref[...] += jnp.sum(dvn, axis=0, keepdims=True)

    @pl.when(i == nsteps - 1)
    def _():
      lane = lax.broadcasted_iota(jnp.int32, (CHUNK, LANES), 1)
      acc = jnp.zeros((CHUNK, LANES), F32)
      for gi in range(N_SGU_GROUPS):
        jp, j = gi // 2, gi % 2
        part = dbx_sc[:, jp * LANES:(jp + 1) * LANES]
        hm = (lane < 64) if j == 0 else (lane >= 64)
        colsum = jnp.sum(jnp.where(hm, part, 0.0), axis=1, keepdims=True)
        acc = jnp.where(lane == gi, colsum, acc)
        dw_ref[gi] = jnp.where(tril, dw_ref[gi], 0.0)
      dbs_ref[...] = acc

  blk = lambda j: pl.BlockSpec((tr, SGU_WIDTH), lambda i: (i, j))
  vec = pl.BlockSpec((1, SGU_WIDTH), lambda i: (0, 0))
  wspec = pl.BlockSpec((N_SGU_GROUPS, CHUNK, CHUNK), lambda i: (0, 0, 0))
  big = lambda dt: pltpu.VMEM((tr, SGU_WIDTH), dt)
  du, dv, dw, dbs, dg, db = pl.pallas_call(
      body, name=name, grid=(nsteps,),
      in_specs=[blk(0), blk(1), blk(0), vec, vec, wspec, pl.BlockSpec((CHUNK, SGU_WIDTH), lambda i: (0, 0))],
      out_specs=[blk(0), blk(0), wspec, pl.BlockSpec((CHUNK, LANES), lambda i: (0, 0)), vec, vec],
      out_shape=[jax.ShapeDtypeStruct((t, SGU_WIDTH), BF16), jax.ShapeDtypeStruct((t, SGU_WIDTH), BF16),
                 jax.ShapeDtypeStruct((N_SGU_GROUPS, CHUNK, CHUNK), F32), jax.ShapeDtypeStruct((CHUNK, LANES), F32),
                 jax.ShapeDtypeStruct((1, SGU_WIDTH), F32), jax.ShapeDtypeStruct((1, SGU_WIDTH), F32)],
      scratch_shapes=[big(BF16), big(F32), big(F32), big(F32), pltpu.VMEM((CHUNK, SGU_WIDTH), F32)],
      compiler_params=_params(("arbitrary",)),
  )(h2, h2, dcat, ln_g.reshape(1, -1), ln_b.reshape(1, -1), w_s, _sgu_bias_lanes(b_s))
  return du, dv, dw, dbs[:, :N_SGU_GROUPS].T, dg[0], db[0]


def _loss_head(xo, tgt, z, g, name, tm=512):
  m, d = xo.shape
  tm = _tile(m, tm)

  def body(x_ref, t_ref, z_ref, g_ref, l_ref, dz_ref, dzb_ref, dg_ref, dbias_ref):
    @pl.when(pl.program_id(0) == 0)
    def _():
      l_ref[...] = jnp.zeros_like(l_ref)

    diff = x_ref[...] - t_ref[...]
    rowsum = jnp.sum(diff * diff, axis=1, keepdims=True)
    tot = jnp.sum(rowsum, axis=0, keepdims=True) * (0.5 / d)
    l_ref[...] += jnp.broadcast_to(tot, l_ref.shape)
    _ln_bwd_tail(diff * (1.0 / d), z_ref, g_ref, dz_ref, dzb_ref, dg_ref, dbias_ref)

  row = pl.BlockSpec((tm, d), lambda i: (i, 0))
  vec = pl.BlockSpec((1, d), lambda i: (0, 0))
  out_specs, out_shape = _ln_bwd_outs(m, d, row, vec)
  l, dz, dzb, dg, dbias = pl.pallas_call(
      body, name=name, grid=(m // tm,), in_specs=[row, row, row, vec],
      out_specs=[pl.BlockSpec((8, LANES), lambda i: (0, 0))] + out_specs,
      out_shape=[jax.ShapeDtypeStruct((8, LANES), F32)] + out_shape,
      compiler_params=_params(("arbitrary",)),
  )(xo, tgt, z, g.reshape(1, d))
  return l[0, 0], dz, dzb, dg[0], dbias[0]


def _local_step(x3, mem3, tgt3, w, early_exchange=None):
  bl, s, d = x3.shape
  t = bl * s
  nm = mem3.shape[1]
  mem2 = mem3.reshape(bl * nm, d)
  x = x3.reshape(t, d)
  xb = x
  saved = []
  for i in range(DEPTH):
    j = i // 2
    attn = i % 2 == 0
    mkv = _mm(mem2, w["w_mem_kv"][i], "nn", F32, f"mkv_fwd_{i}", tm=1024, tn=512, tk=1024)
    mkv3 = mkv.reshape(bl, nm, 2 * MEM_WIDTH)
    w_in = w["a_w_in"][j] if attn else w["b_w_in"][j]
    h = _mm(xb, w_in, "nt", F32, f"in_proj_{i}", tm=512, tn=w_in.shape[0], tk=d)
    h3 = h.reshape(bl, s, -1)
    if attn:
      mix3, lse3 = _attn_fwd(h3, f"dil_attn_fwd_{i}")
      mix = mix3.reshape(t, DIL_WIDTH)
      qcol = 3 * DIL_WIDTH // MEM_WIDTH
    else:
      mix = _sgu_fwd(h, w["sgu_ln_g"][j], w["sgu_ln_b"][j], w["sgu_w_s"][j], w["sgu_b_s"][j], f"sgu_fwd_{i}")
      lse3 = None
      qcol = 2 * SGU_WIDTH // MEM_WIDTH
    mo = _mem_fwd(h3, qcol, mkv3, f"mem_attn_fwd_{i}").reshape(t, MEM_WIDTH)
    cat = jnp.concatenate([mix, mo], axis=1)
    z1, xm, xmb = _mm_res_ln(cat, w["w_out"][i], x, w["ln_mix_g"][i], w["ln_mix_b"][i], f"out_proj_ln_{i}", tk=1024)
    a, b, hm = _ffn_up(xmb, w["w_gate"][i], w["w_up"][i], f"ffn_up_{i}")
    z2, xo, xob = _mm_res_ln(hm, w["w_down"][i], xm, w["ln_ffn_g"][i], w["ln_ffn_b"][i], f"ffn_down_ln_{i}", tk=hm.shape[1])
    saved.append(dict(xb=xb, h=h, h3=h3, mkv3=mkv3, mix3=(mix3 if attn else None), lse3=lse3, cat=cat, z1=z1,
                      xmb=xmb, a=a, b=b, hm=hm, z2=z2, qcol=qcol))
    x, xb = xo, xob

  names = ("a_w_in", "b_w_in", "sgu_ln_g", "sgu_ln_b", "sgu_w_s", "sgu_b_s", "w_mem_kv", "w_out",
           "ln_mix_g", "ln_mix_b", "w_gate", "w_up", "w_down", "ln_ffn_g", "ln_ffn_b")
  grads = {n: [None] * w[n].shape[0] for n in names}
  last = DEPTH - 1
  loss, dz2, dz2b, grads["ln_ffn_g"][last], grads["ln_ffn_b"][last] = _loss_head(
      x, tgt3.reshape(t, d), saved[last]["z2"], w["ln_ffn_g"][last], "loss_head")
  dx = None
  for i in reversed(range(DEPTH)):
    j = i // 2
    attn = i % 2 == 0
    sv = saved[i]
    da, db = _ffn_bwd_hidden(dz2b, w["w_down"][i], sv["a"], sv["b"], f"ffn_bwd_hidden_{i}")
    grads["w_down"][i] = _mm(sv["hm"], dz2b, "tn", F32, f"dw_down_{i}", tm=1408, tn=1024, tk=1024)
    grads["w_gate"][i] = _mm(da, sv["xmb"], "tn", F32, f"dw_gate_{i}", tm=1408, tn=1024, tk=1024)
    grads["w_up"][i] = _mm(db, sv["xmb"], "tn", F32, f"dw_up_{i}", tm=1408, tn=1024, tk=1024)
    dz1, dz1b, grads["ln_mix_g"][i], grads["ln_mix_b"][i] = _ffn_bwd_input_ln(
        da, db, w["w_gate"][i], w["w_up"][i], dz2, sv["z1"], w["ln_mix_g"][i], f"ffn_bwd_input_ln_{i}")
    grads["w_out"][i] = _mm(sv["cat"], dz1b, "tn", F32, f"dw_out_{i}", tm=1024, tn=1024, tk=1024)
    dcat = _mm(dz1b, w["w_out"][i], "nt", F32, f"out_proj_bwd_{i}", tm=1024, tn=1024, tk=1024)
    dcat3 = dcat.reshape(bl, s, -1)
    dqm3, dmkv3 = _mem_bwd(sv["h3"], sv["qcol"], sv["mkv3"], dcat3, f"mem_attn_bwd_{i}")
    grads["w_mem_kv"][i] = _mm(mem2, dmkv3.reshape(bl * nm, 2 * MEM_WIDTH), "tn", F32, f"dw_mem_kv_{i}", tm=1024, tn=512, tk=1024)
    dqm = dqm3.reshape(t, MEM_WIDTH)
    if attn and i == 0 and early_exchange is not None:
      q, (pack, state) = early_exchange(grads)
      dq3, dk3, dv3, x3 = _attn_bwd(sv["h3"], sv["mix3"], sv["lse3"], dcat3, f"dil_attn_bwd_{i}", exchange=q)
      grads["early_exchange"] = (pack, state, x3)
      parts = [dq3.reshape(t, -1), dk3.reshape(t, -1), dv3.reshape(t, -1), dqm]
    elif attn:
      dq3, dk3, dv3 = _attn_bwd(sv["h3"], sv["mix3"], sv["lse3"], dcat3, f"dil_attn_bwd_{i}")
      parts = [dq3.reshape(t, -1), dk3.reshape(t, -1), dv3.reshape(t, -1), dqm]
    else:
      du, dv, dws, dbs, dlg, dlb = _sgu_bwd(sv["h"], dcat, w["sgu_ln_g"][j], w["sgu_ln_b"][j], w["sgu_w_s"][j],
                                             w["sgu_b_s"][j], f"sgu_bwd_{i}")
      grads["sgu_w_s"][j], grads["sgu_b_s"][j], grads["sgu_ln_g"][j], grads["sgu_ln_b"][j] = dws, dbs, dlg, dlb
      parts = [du, dv, dqm]
    dh = jnp.concatenate(parts, axis=1)
    w_in = w["a_w_in"][j] if attn else w["b_w_in"][j]
    grads["a_w_in" if attn else "b_w_in"][j] = _mm(dh, sv["xb"], "tn", F32, f"dw_in_{i}", tm=1280 if attn else 896, tn=1024, tk=1024)
    if i > 0:
      dz2, dz2b, grads["ln_ffn_g"][i - 1], grads["ln_ffn_b"][i - 1] = _in_proj_bwd_ln(
          dh, w_in, dz1, saved[i - 1]["z2"], w["ln_ffn_g"][i - 1], f"in_proj_bwd_ln_{i}")
    else:
      dx = _mm(dh, w_in, "nn", F32, f"in_proj_bwd_{i}", add=dz1, add_scale=DN_ALPHA, tm=512, tn=d, tk=w_in.shape[0])
  return loss, dx.reshape(bl, s, d), grads


def _my_place():
  return lax.axis_index("x"), lax.axis_index("y"), lax.axis_index("c")


def _other_chips(x, y):
  return [(1 - x, y), (x, 1 - y), (1 - x, 1 - y)]


ANY = pl.BlockSpec(memory_space=pl.ANY)


def _all_gather_halves(wl, name):
  _, r, c_ = wl.shape

  def body(w_ref, g_ref, send_sems, recv_sems):
    x, y, c = _my_place()
    me = 2 * x + y
    sibling = (x, y, 1 - c)
    chips = _other_chips(x, y)

    def copy(k, src, dst, to):
      return pltpu.make_async_remote_copy(src_ref=src, dst_ref=dst, send_sem=send_sems.at[k], recv_sem=recv_sems.at[k],
                                          device_id=to, device_id_type=MESH_ID)

    first = [copy(k, w_ref.at[c], g_ref.at[me, c], (px, py, c)) for k, (px, py) in enumerate(chips)]
    for cp in first:
      cp.start()
    passed = []
    for k, (px, py) in enumerate(chips):
      landed = g_ref.at[2 * px + py, c]
      copy(k, landed, landed, (px, py, c)).wait_recv()
      fwd = copy(3 + k, landed, landed, sibling)
      fwd.start()
      passed.append(fwd)
    for k, (px, py) in enumerate(chips):
      theirs = g_ref.at[2 * px + py, 1 - c]
      copy(3 + k, theirs, theirs, sibling).wait_recv()
    for cp in first + passed:
      cp.wait_send()

  got = pl.pallas_call(
      body, name=name, in_specs=[ANY], out_specs=ANY,
      out_shape=jax.ShapeDtypeStruct((4, 2, r, c_), wl.dtype),
      scratch_shapes=[pltpu.SemaphoreType.DMA((6,)), pltpu.SemaphoreType.DMA((6,))],
  )(wl)
  chip = 2 * lax.axis_index("x") + lax.axis_index("y")
  return lax.dynamic_update_slice(got, wl[None], (chip, 0, 0, 0))


def _all_gather_relayed(wl, name):
  _, r, c_ = wl.shape
  h = r // 2
  assert h % ROW_ALIGN == 0

  def body(w_ref, g_ref, send_sems, recv_sems):
    x, y, c = _my_place()
    me = 2 * x + y
    sibling = (x, y, 1 - c)
    xn, yn, dg = _other_chips(x, y)

    def copy(k, src, dst, to):
      return pltpu.make_async_remote_copy(src_ref=src, dst_ref=dst, send_sem=send_sems.at[k], recv_sem=recv_sems.at[k],
                                          device_id=to, device_id_type=MESH_ID)

    def block(chip, half):
      return g_ref.at[2 * chip[0] + chip[1], half]

    top, bottom = pl.ds(0, h), pl.ds(h, h)
    sent = [copy(0, w_ref.at[c], block((x, y), c), (*xn, c)), copy(1, w_ref.at[c], block((x, y), c), (*yn, c))]
    for cp in sent:
      cp.start()
    copy(0, block(xn, c), block(xn, c), (*xn, c)).wait_recv()
    sent.append(copy(2, block(xn, c).at[top], block(xn, c).at[top], (*yn, c)))
    sent[-1].start()
    sent.append(copy(4, block(xn, c), block(xn, c), sibling))
    sent[-1].start()
    copy(1, block(yn, c), block(yn, c), (*yn, c)).wait_recv()
    sent.append(copy(3, block(yn, c).at[bottom], block(yn, c).at[bottom], (*xn, c)))
    sent[-1].start()
    sent.append(copy(5, block(yn, c), block(yn, c), sibling))
    sent[-1].start()
    copy(2, block(dg, c).at[top], block(dg, c).at[top], (*yn, c)).wait_recv()
    copy(3, block(dg, c).at[bottom], block(dg, c).at[bottom], (*xn, c)).wait_recv()
    sent.append(copy(6, block(dg, c), block(dg, c), sibling))
    sent[-1].start()
    for k, chip in ((4, xn), (5, yn), (6, dg)):
      copy(k, block(chip, 1 - c), block(chip, 1 - c), sibling).wait_recv()
    for cp in sent:
      cp.wait_send()

  got = pl.pallas_call(
      body, name=name, in_specs=[ANY], out_specs=ANY,
      out_shape=jax.ShapeDtypeStruct((4, 2, r, c_), wl.dtype),
      scratch_shapes=[pltpu.SemaphoreType.DMA((7,)), pltpu.SemaphoreType.DMA((7,))],
  )(wl)
  chip = 2 * lax.axis_index("x") + lax.axis_index("y")
  return lax.dynamic_update_slice(got, wl[None], (chip, 0, 0, 0))


def _sibling_swap(v, name):
  def body(v_ref, o_ref, send_sem, recv_sem):
    x, y, c = _my_place()
    cp = pltpu.make_async_remote_copy(src_ref=v_ref, dst_ref=o_ref, send_sem=send_sem, recv_sem=recv_sem,
                                      device_id=(x, y, 1 - c), device_id_type=MESH_ID)
    cp.start()
    cp.wait()

  return pl.pallas_call(
      body, name=name, in_specs=[ANY], out_specs=ANY, out_shape=jax.ShapeDtypeStruct(v.shape, v.dtype),
      scratch_shapes=[pltpu.SemaphoreType.DMA, pltpu.SemaphoreType.DMA],
  )(v)


def _chip_exchange_copies(q_ref, o_ref, send_sems, recv_sems):
  x, y, c = _my_place()
  return [pltpu.make_async_remote_copy(src_ref=q_ref.at[2 * px + py], dst_ref=o_ref.at[k], send_sem=send_sems.at[k],
                                       recv_sem=recv_sems.at[k], device_id=(px, py, c), device_id_type=MESH_ID)
          for k, (px, py) in enumerate(_other_chips(x, y))]


def _chip_exchange(q, name):
  _, r, c_ = q.shape

  def body(q_ref, o_ref, send_sems, recv_sems):
    cps = _chip_exchange_copies(q_ref, o_ref, send_sems, recv_sems)
    for cp in cps:
      cp.start()
    for cp in cps:
      cp.wait()

  return pl.pallas_call(
      body, name=name, in_specs=[ANY], out_specs=ANY, out_shape=jax.ShapeDtypeStruct((3, r, c_), q.dtype),
      scratch_shapes=[pltpu.SemaphoreType.DMA((3,)), pltpu.SemaphoreType.DMA((3,))],
  )(q)


def _share_halves(v, name):
  theirs = _sibling_swap(v, name)
  c = lax.axis_index("c")
  return jnp.where(c == 0, jnp.concatenate([v, theirs]), jnp.concatenate([theirs, v]))


def _half_spec(tr, c_, pick):
  return pl.BlockSpec((None, None, tr, c_), lambda s, r, place: (s, pick(place), r, 0))


def _cast_other_half(p, place, name, tr=512):
  _, _, r, c_ = p.shape
  tr = _tile(r, tr, 16)

  def body(place_ref, p_ref, o_ref):
    o_ref[...] = _bf(p_ref[...])

  out_spec = pl.BlockSpec((None, tr, c_), lambda s, rr, place: (s, rr, 0))
  return pl.pallas_call(
      body, name=name, out_shape=jax.ShapeDtypeStruct((4, r, c_), BF16),
      grid_spec=pltpu.PrefetchScalarGridSpec(num_scalar_prefetch=1, grid=(4, r // tr),
                                             in_specs=[_half_spec(tr, c_, lambda place: 1 - place[1])], out_specs=out_spec),
      compiler_params=_params(("parallel", "parallel")),
  )(place, p)


def _add_sibling(p, x1, place, name, tr=512):
  _, _, r, c_ = p.shape
  tr = _tile(r, tr, 16)

  def body(place_ref, p_ref, x_ref, o_ref):
    o_ref[...] = _bf(p_ref[...] + x_ref[...].astype(F32))

  row = pl.BlockSpec((None, tr, c_), lambda s, rr, place: (s, rr, 0))
  return pl.pallas_call(
      body, name=name, out_shape=jax.ShapeDtypeStruct((4, r, c_), BF16),
      grid_spec=pltpu.PrefetchScalarGridSpec(num_scalar_prefetch=1, grid=(4, r // tr),
                                             in_specs=[_half_spec(tr, c_, lambda place: place[1]), row], out_specs=row),
      compiler_params=_params(("parallel", "parallel")),
  )(place, p, x1)


def _sum_own(p, x1, x3, place, name, tr=512):
  _, _, r, c_ = p.shape
  tr = _tile(r, tr, 16)

  def body(place_ref, p_ref, x1_ref, x3_ref, o_ref):
    acc = p_ref[...] + x1_ref[...].astype(F32)
    for k in range(3):
      acc = acc + x3_ref[k].astype(F32)
    o_ref[...] = acc

  return pl.pallas_call(
      body, name=name, out_shape=jax.ShapeDtypeStruct((r, c_), F32),
      grid_spec=pltpu.PrefetchScalarGridSpec(
          num_scalar_prefetch=1, grid=(r // tr,),
          in_specs=[pl.BlockSpec((None, None, tr, c_), lambda rr, place: (place[0], place[1], rr, 0)),
                    pl.BlockSpec((None, tr, c_), lambda rr, place: (place[0], rr, 0)),
                    pl.BlockSpec((3, tr, c_), lambda rr, place: (0, rr, 0))],
          out_specs=pl.BlockSpec((tr, c_), lambda rr, place: (rr, 0))),
      compiler_params=_params(("parallel",)),
  )(place, p, x1, x3)


def _reduce_scatter_begin(p, tag):
  x, y, c = _my_place()
  place = jnp.stack([2 * x + y, c]).astype(jnp.int32)
  x1 = _sibling_swap(_cast_other_half(p, place, f"rs_cast_other_half_{tag}"), f"rs_sibling_swap_{tag}")
  return _add_sibling(p, x1, place, f"rs_add_sibling_{tag}"), (p, x1, place)


def _reduce_scatter_end(state, x3, tag):
  p, x1, place = state
  return _share_halves(_sum_own(p, x1, x3, place, f"rs_sum_own_{tag}"), f"rs_share_halves_{tag}")


def _adamw(w, g, m, v, name):
  shape = w.shape
  cols = shape[-1]
  rows = w.size // cols
  tr = _tile(rows, max(8, (256 * 1024) // cols // 8 * 8), 8)

  def body(w_ref, g_ref, m_ref, v_ref, d_ref, nm_ref, nv_ref):
    gv = g_ref[...]
    nm = ADAM_B1 * m_ref[...] + (1.0 - ADAM_B1) * gv
    nv = ADAM_B2 * v_ref[...] + (1.0 - ADAM_B2) * (gv * gv)
    m_hat = nm / (1.0 - ADAM_B1 ** ADAM_STEP)
    v_hat = nv / (1.0 - ADAM_B2 ** ADAM_STEP)
    d_ref[...] = -ADAM_LR * (m_hat / (jnp.sqrt(v_hat) + ADAM_EPS) + ADAM_WD * w_ref[...])
    nm_ref[...] = nm
    nv_ref[...] = nv

  spec = pl.BlockSpec((tr, cols), lambda i: (i, 0))
  sds = jax.ShapeDtypeStruct((rows, cols), F32)
  outs = pl.pallas_call(
      body, name=name, grid=(rows // tr,), in_specs=[spec] * 4, out_specs=[spec] * 3, out_shape=[sds] * 3,
      compiler_params=_params(("parallel",)),
  )(*(t.reshape(rows, cols) for t in (w, g, m, v)))
  return tuple(o.reshape(shape) for o in outs)


SHARDED = (("a_w_in", True), ("b_w_in", True), ("w_mem_kv", False), ("w_out", False), ("w_gate", True),
           ("w_up", True), ("w_down", False))
SMALL_SHARDED = (("sgu_ln_g", 1), ("sgu_ln_b", 1))
REPLICATED = ("sgu_w_s", "sgu_b_s", "ln_mix_g", "ln_mix_b", "ln_ffn_g", "ln_ffn_b")
SMALL_ORDER = ("sgu_w_s", "sgu_b_s", "ln_mix_g", "ln_mix_b", "ln_ffn_g", "ln_ffn_b", "sgu_ln_g", "sgu_ln_b")
ROW_ALIGN = 16


def _pad_to(v, n):
  return jnp.pad(v, (0, n - v.shape[0]))


def _round_up(n, a):
  return -(-n // a) * a


def _exchange_form(t, transposed):
  return jnp.swapaxes(t, 1, 2) if transposed else t


def _to_shard_major(full, axis):
  shp = full.shape
  cut = shp[:axis] + (4, shp[axis] // 4) + shp[axis + 1:]
  return jnp.moveaxis(full.reshape(cut), axis, 0).reshape(4, -1, FLAT_COLS)


def _from_shard_major(rows, shard_shape, axis):
  full = jnp.moveaxis(rows.reshape((4,) + tuple(shard_shape)), 0, axis)
  shp = full.shape
  return full.reshape(shp[:axis] + (shp[axis] * shp[axis + 1],) + shp[axis + 2:])


def _gather_weights(shards):
  blocks = {n: _exchange_form(shards[n], tr) for n, tr in SHARDED}
  segs = [blocks[n].astype(BF16).reshape(-1, FLAT_COLS) for n, _ in SHARDED]
  small = jnp.concatenate([lax.bitcast_convert_type(shards[n], BF16).reshape(-1) for n, _ in SMALL_SHARDED])
  small_rows = _round_up(small.shape[0], ROW_ALIGN * FLAT_COLS) // FLAT_COLS
  segs.append(_pad_to(small, small_rows * FLAT_COLS).reshape(small_rows, FLAT_COLS))
  rows = sum(sg.shape[0] for sg in segs)
  rows_pad = _round_up(rows, 4 * ROW_ALIGN)
  if rows_pad > rows:
    segs.append(jnp.zeros((rows_pad - rows, FLAT_COLS), BF16))
  flat = jnp.concatenate(segs).reshape(2, rows_pad // 2, FLAT_COLS)
  g = _all_gather_relayed(flat, "gather_weights").reshape(4, rows_pad, FLAT_COLS)
  out, off = {}, 0
  for n, _ in SHARDED:
    nr = blocks[n].size // FLAT_COLS
    out[n] = _from_shard_major(g[:, off:off + nr], blocks[n].shape, 1)
    off += nr
  small_g = g[:, off:off + small_rows].reshape(4, small_rows * FLAT_COLS)
  off = 0
  for n, axis in SMALL_SHARDED:
    sz = 2 * shards[n].size
    vals = lax.bitcast_convert_type(small_g[:, off:off + sz].reshape((4,) + shards[n].shape + (2,)), F32)
    out[n] = _from_shard_major(vals, shards[n].shape, axis)
    off += sz
  return out


def _reduce_grads(grads, shard_shapes):
  early, state, x3 = grads.pop("early_exchange")
  mine_early = _reduce_scatter_end(state, x3, "early")
  late = _GradPack([(n, l, g) for n, _ in SHARDED for l, g in enumerate(grads[n]) if (n, l) not in early.rows])
  q, state = _reduce_scatter_begin(late.p, "late")
  mine_late = _reduce_scatter_end(state, _chip_exchange(q, "rs_chip_exchange_late"), "late")
  out = {}
  for n, tr in SHARDED:
    layers, rows, cols = shard_shapes[n]
    blocks = []
    for l in range(layers):
      pack, mine = (early, mine_early) if (n, l) in early.rows else (late, mine_late)
      off, nr = pack.rows[(n, l)]
      block = mine[off:off + nr]
      blocks.append(block.reshape(cols, rows).T if tr else block.reshape(rows, cols))
    out[n] = jnp.stack(blocks)
  off, quarter_rows = early.rows["small"]
  piece = mine_early[off:off + quarter_rows].reshape(2, quarter_rows // 2, FLAT_COLS)
  small_sum = _all_gather_halves(piece, "gather_small_grads").reshape(-1)
  off = 0
  for n in SMALL_ORDER:
    shape = (len(grads[n]),) + grads[n][0].shape
    sz = math.prod(shape)
    out[n] = small_sum[off:off + sz].reshape(shape)
    off += sz
  return out


class _GradPack:
  def __init__(self, items, small=None):
    segs, self.rows, off = [], {}, 0
    for n, l, g in items:
      seg = _to_shard_major(g, 0)
      self.rows[(n, l)] = (off, seg.shape[1])
      segs.append(seg)
      off += seg.shape[1]
    if small is not None:
      flat = jnp.concatenate([jnp.stack(small[n]).reshape(-1) for n in SMALL_ORDER])
      n_small = _round_up(flat.shape[0], 4 * 2 * 8 * FLAT_COLS)
      quarter_rows = n_small // (4 * FLAT_COLS)
      self.rows["small"] = (off, quarter_rows)
      segs.append(_pad_to(flat, n_small).reshape(4, quarter_rows, FLAT_COLS))
      off += quarter_rows
    rows_pad = _round_up(off, 2 * ROW_ALIGN)
    if rows_pad > off:
      segs.append(jnp.zeros((4, rows_pad - off, FLAT_COLS), F32))
    self.p = jnp.concatenate(segs, axis=1).reshape(4, 2, rows_pad // 2, FLAT_COLS)


def _early_exchange_begin(grads):
  items = [(n, l, g) for n, _ in SHARDED for l, g in enumerate(grads[n]) if g is not None]
  pack = _GradPack(items, small={n: grads[n] for n in SMALL_ORDER})
  q, state = _reduce_scatter_begin(pack.p, "early")
  return q, (pack, state)


WEIGHT_NAMES = ("a_w_in", "b_w_in", "sgu_ln_g", "sgu_ln_b", "sgu_w_s", "sgu_b_s", "w_mem_kv", "w_out",
                "ln_mix_g", "ln_mix_b", "w_gate", "w_up", "w_down", "ln_ffn_g", "ln_ffn_b")


def kernel(x, mem, a_w_in, b_w_in, sgu_ln_g, sgu_ln_b, sgu_w_s, sgu_b_s, w_mem_kv, w_out, ln_mix_g, ln_mix_b, w_gate, w_up, w_down, ln_ffn_g, ln_ffn_b, loss_target, m_a_w_in, m_b_w_in, m_sgu_ln_g, m_sgu_ln_b, m_sgu_w_s, m_sgu_b_s, m_w_mem_kv, m_w_out, m_ln_mix_g, m_ln_mix_b, m_w_gate, m_w_up, m_w_down, m_ln_ffn_g, m_ln_ffn_b, v_a_w_in, v_b_w_in, v_sgu_ln_g, v_sgu_ln_b, v_sgu_w_s, v_sgu_b_s, v_w_mem_kv, v_w_out, v_ln_mix_g, v_ln_mix_b, v_w_gate, v_w_up, v_w_down, v_ln_ffn_g, v_ln_ffn_b):
  weights = dict(a_w_in=a_w_in, b_w_in=b_w_in, sgu_ln_g=sgu_ln_g, sgu_ln_b=sgu_ln_b, sgu_w_s=sgu_w_s, sgu_b_s=sgu_b_s,
                 w_mem_kv=w_mem_kv, w_out=w_out, ln_mix_g=ln_mix_g, ln_mix_b=ln_mix_b, w_gate=w_gate, w_up=w_up,
                 w_down=w_down, ln_ffn_g=ln_ffn_g, ln_ffn_b=ln_ffn_b)
  mom1 = dict(a_w_in=m_a_w_in, b_w_in=m_b_w_in, sgu_ln_g=m_sgu_ln_g, sgu_ln_b=m_sgu_ln_b, sgu_w_s=m_sgu_w_s,
              sgu_b_s=m_sgu_b_s, w_mem_kv=m_w_mem_kv, w_out=m_w_out, ln_mix_g=m_ln_mix_g, ln_mix_b=m_ln_mix_b,
              w_gate=m_w_gate, w_up=m_w_up, w_down=m_w_down, ln_ffn_g=m_ln_ffn_g, ln_ffn_b=m_ln_ffn_b)
  mom2 = dict(a_w_in=v_a_w_in, b_w_in=v_b_w_in, sgu_ln_g=v_sgu_ln_g, sgu_ln_b=v_sgu_ln_b, sgu_w_s=v_sgu_w_s,
              sgu_b_s=v_sgu_b_s, w_mem_kv=v_w_mem_kv, w_out=v_w_out, ln_mix_g=v_ln_mix_g, ln_mix_b=v_ln_mix_b,
              w_gate=v_w_gate, w_up=v_w_up, w_down=v_w_down, ln_ffn_g=v_ln_ffn_g, ln_ffn_b=v_ln_ffn_b)

  full = _gather_weights(weights)
  for n in REPLICATED:
    full[n] = weights[n]
  loss_part, grad_x, grads = _local_step(x, mem, loss_target, full, early_exchange=_early_exchange_begin)
  loss = lax.psum(loss_part, MESH_AXES)

  shard_shapes = {n: weights[n].shape for n, _ in SHARDED}
  red = _reduce_grads(grads, shard_shapes)
  chip = 2 * lax.axis_index("x") + lax.axis_index("y")
  for n, axis in SMALL_SHARDED:
    width = weights[n].shape[axis]
    red[n] = lax.dynamic_slice_in_dim(red[n], chip * width, width, axis)

  small_names = SMALL_ORDER
  def pack(d):
    flat = jnp.concatenate([d[n].reshape(-1) for n in small_names])
    return _pad_to(flat, _round_up(flat.shape[0], 8 * FLAT_COLS)).reshape(-1, FLAT_COLS)
  small_out = _adamw(pack(weights), pack(red), pack(mom1), pack(mom2), "adamw_small")
  delta, new_m, new_v = {}, {}, {}
  off = 0
  for n in small_names:
    sz = weights[n].size
    for dst, src in zip((delta, new_m, new_v), small_out):
      dst[n] = src.reshape(-1)[off:off + sz].reshape(weights[n].shape)
    off += sz
  for n, _ in SHARDED:
    delta[n], new_m[n], new_v[n] = _adamw(weights[n], red[n], mom1[n], mom2[n], f"adamw_{n}")

  return (loss, grad_x, *[red[n] for n in WEIGHT_NAMES], *[delta[n] for n in WEIGHT_NAMES],
          *[new_m[n] for n in WEIGHT_NAMES], *[new_v[n] for n in WEIGHT_NAMES])
```

```python
import functools
import math

import jax
import jax.numpy as jnp
from jax import lax
from jax.experimental import pallas as pl
from jax.experimental.pallas import tpu as pltpu

F32 = jnp.float32
BF16 = jnp.bfloat16

DEPTH = 4
HEAD_DIM = 64
N_DIL_HEADS = 12
DIL_WIDTH = N_DIL_HEADS * HEAD_DIM
DIL_PATTERNS = ((128, 1), (512, 4), (2048, 16))
BLOCK = 128
N_SGU_GROUPS = 12
SGU_WIDTH = N_SGU_GROUPS * 64
CHUNK = 128
N_MEM_HEADS = 4
MEM_WIDTH = N_MEM_HEADS * HEAD_DIM
DN_ALPHA = (2 * DEPTH) ** 0.25
LN_EPS = 1e-5
ATT_SCALE = HEAD_DIM ** -0.5
ADAM_LR = 0.001
ADAM_B1 = 0.9
ADAM_B2 = 0.999
ADAM_EPS = 1e-08
ADAM_WD = 0.01
ADAM_STEP = 10
NEG_BIG = -1e30
ATTN_UNROLL = 2

LANES = 128
FLAT_COLS = 1024
VMEM_LIMIT = 56 * 1024 * 1024
MESH_AXES = ("x", "y", "c")
MESH_ID = pl.DeviceIdType.MESH


def _tile(n, pref, align=LANES):
  if n <= pref:
    return n
  t = (pref // align) * align
  while t >= align:
    if n % t == 0:
      return t
    t -= align
  return n


def _params(sem):
  return pltpu.CompilerParams(dimension_semantics=sem, vmem_limit_bytes=VMEM_LIMIT)


def _dot(a, b):
  return jnp.dot(a, b, preferred_element_type=F32)


def _dot_nt(a, b):
  return lax.dot_general(a, b, (((1,), (1,)), ((), ())), preferred_element_type=F32)


def _dot_tn(a, b):
  return lax.dot_general(a, b, (((0,), (0,)), ((), ())), preferred_element_type=F32)


def _bf(v):
  return v.astype(BF16)


def _ln_stats(z):
  mu = jnp.mean(z, axis=-1, keepdims=True)
  zc = z - mu
  var = jnp.mean(zc * zc, axis=-1, keepdims=True)
  rstd = lax.rsqrt(var + LN_EPS)
  return zc * rstd, rstd


def _ln_bwd(dy, xhat, rstd, g):
  gdy = dy * g
  m1 = jnp.mean(gdy, axis=-1, keepdims=True)
  m2 = jnp.mean(gdy * xhat, axis=-1, keepdims=True)
  return rstd * (gdy - m1 - xhat * m2)


_GELU_C = math.sqrt(2.0 / math.pi)


def _gelu_parts(v):
  v2 = v * v
  t = jnp.tanh(_GELU_C * (v + 0.044715 * v * v2))
  val = 0.5 * v * (1.0 + t)
  der = 0.5 * (1.0 + t) + 0.5 * v * (1.0 - t * t) * (_GELU_C * (1.0 + 3.0 * 0.044715 * v2))
  return val, der


def _gelu(v):
  t = jnp.tanh(_GELU_C * (v + 0.044715 * v * v * v))
  return 0.5 * v * (1.0 + t)


def _sigmoid(v):
  return 1.0 / (1.0 + jnp.exp(-v))


def _mm(a, b, mode, out_dtype, name, add=None, add_scale=1.0, tm=512, tn=512, tk=512):
  if mode == "nn":
    (m, k), (k2, n) = a.shape, b.shape
  elif mode == "nt":
    (m, k), (n, k2) = a.shape, b.shape
  else:
    (k, m), (k2, n) = a.shape, b.shape
  assert k == k2, (a.shape, b.shape, mode)
  tm, tn, tk = _tile(m, tm), _tile(n, tn), _tile(k, tk)
  nk = k // tk
  if mode == "nn":
    a_spec = pl.BlockSpec((tm, tk), lambda i, j, kk: (i, kk))
    b_spec = pl.BlockSpec((tk, tn), lambda i, j, kk: (kk, j))
    dot = _dot
  elif mode == "nt":
    a_spec = pl.BlockSpec((tm, tk), lambda i, j, kk: (i, kk))
    b_spec = pl.BlockSpec((tn, tk), lambda i, j, kk: (j, kk))
    dot = _dot_nt
  else:
    a_spec = pl.BlockSpec((tk, tm), lambda i, j, kk: (kk, i))
    b_spec = pl.BlockSpec((tk, tn), lambda i, j, kk: (kk, j))
    dot = _dot_tn
  o_spec = pl.BlockSpec((tm, tn), lambda i, j, kk: (i, j))
  has_add = add is not None

  def body(*refs):
    if has_add:
      a_ref, b_ref, add_ref, o_ref, acc_ref = refs
    else:
      a_ref, b_ref, o_ref, acc_ref = refs
    kk = pl.program_id(2)

    @pl.when(kk == 0)
    def _():
      acc_ref[...] = jnp.zeros_like(acc_ref)

    acc_ref[...] += dot(_bf(a_ref[...]), _bf(b_ref[...]))

    @pl.when(kk == nk - 1)
    def _():
      r = acc_ref[...]
      if has_add:
        r = r + add_scale * add_ref[...].astype(F32)
      o_ref[...] = r.astype(out_dtype)

  in_specs = [a_spec, b_spec] + ([o_spec] if has_add else [])
  args = (a, b) + ((add,) if has_add else ())
  return pl.pallas_call(
      body, name=name, grid=(m // tm, n // tn, nk), in_specs=in_specs, out_specs=o_spec,
      out_shape=jax.ShapeDtypeStruct((m, n), out_dtype),
      scratch_shapes=[pltpu.VMEM((tm, tn), F32)],
      compiler_params=_params(("parallel", "parallel", "arbitrary")),
  )(*args)


def _mm_res_ln(a, w, res, g, b, name, tm=512, tk=512):
  m, k = a.shape
  d = w.shape[1]
  tm, tk = _tile(m, tm), _tile(k, tk)
  nk = k // tk

  def body(a_ref, w_ref, r_ref, g_ref, b_ref, z_ref, x_ref, xb_ref, acc_ref):
    kk = pl.program_id(1)

    @pl.when(kk == 0)
    def _():
      acc_ref[...] = jnp.zeros_like(acc_ref)

    acc_ref[...] += _dot(_bf(a_ref[...]), _bf(w_ref[...]))

    @pl.when(kk == nk - 1)
    def _():
      z = DN_ALPHA * r_ref[...] + acc_ref[...]
      xhat, _ = _ln_stats(z)
      xn = xhat * g_ref[...] + b_ref[...]
      z_ref[...] = z
      x_ref[...] = xn
      xb_ref[...] = _bf(xn)

  row = pl.BlockSpec((tm, d), lambda i, kk: (i, 0))
  vec = pl.BlockSpec((1, d), lambda i, kk: (0, 0))
  return pl.pallas_call(
      body, name=name, grid=(m // tm, nk),
      in_specs=[pl.BlockSpec((tm, tk), lambda i, kk: (i, kk)), pl.BlockSpec((tk, d), lambda i, kk: (kk, 0)), row, vec, vec],
      out_specs=[row, row, row],
      out_shape=[jax.ShapeDtypeStruct((m, d), F32), jax.ShapeDtypeStruct((m, d), F32), jax.ShapeDtypeStruct((m, d), BF16)],
      scratch_shapes=[pltpu.VMEM((tm, d), F32)],
      compiler_params=_params(("parallel", "arbitrary")),
  )(a, w, res, g.reshape(1, d), b.reshape(1, d))


def _ln_bwd_call(dy, z, g, name, tm=512):
  m, d = z.shape
  tm = _tile(m, tm)
  n = m // tm

  def body(dy_ref, z_ref, g_ref, dz_ref, dzb_ref, dg_ref, db_ref):
    i = pl.program_id(0)

    @pl.when(i == 0)
    def _():
      dg_ref[...] = jnp.zeros_like(dg_ref)
      db_ref[...] = jnp.zeros_like(db_ref)

    dy_v = dy_ref[...]
    xhat, rstd = _ln_stats(z_ref[...])
    dz = _ln_bwd(dy_v, xhat, rstd, g_ref[...])
    dz_ref[...] = dz
    dzb_ref[...] = _bf(dz)
    dg_ref[...] += jnp.sum(dy_v * xhat, axis=0, keepdims=True)
    db_ref[...] += jnp.sum(dy_v, axis=0, keepdims=True)

  row = pl.BlockSpec((tm, d), lambda i: (i, 0))
  vec = pl.BlockSpec((1, d), lambda i: (0, 0))
  dz, dzb, dg, db = pl.pallas_call(
      body, name=name, grid=(n,), in_specs=[row, row, vec], out_specs=[row, row, vec, vec],
      out_shape=[jax.ShapeDtypeStruct((m, d), F32), jax.ShapeDtypeStruct((m, d), BF16),
                 jax.ShapeDtypeStruct((1, d), F32), jax.ShapeDtypeStruct((1, d), F32)],
      compiler_params=_params(("arbitrary",)),
  )(dy, z, g.reshape(1, d))
  return dz, dzb, dg[0], db[0]


def _ffn_up(xb, wg, wu, name, tm=512, tn=1408):
  m, d = xb.shape
  f = wg.shape[0]
  tm, tn = _tile(m, tm), _tile(f, tn)

  def body(x_ref, wg_ref, wu_ref, a_ref, b_ref, h_ref):
    xv = x_ref[...]
    a = _dot_nt(xv, wg_ref[...])
    b = _dot_nt(xv, wu_ref[...])
    a_ref[...] = _bf(a)
    b_ref[...] = _bf(b)
    h_ref[...] = _bf(a * _sigmoid(a) * b)

  wspec = pl.BlockSpec((tn, d), lambda j, i: (j, 0))
  ospec = pl.BlockSpec((tm, tn), lambda j, i: (i, j))
  sds = jax.ShapeDtypeStruct((m, f), BF16)
  return pl.pallas_call(
      body, name=name, grid=(f // tn, m // tm),
      in_specs=[pl.BlockSpec((tm, d), lambda j, i: (i, 0)), wspec, wspec],
      out_specs=[ospec, ospec, ospec], out_shape=[sds, sds, sds],
      compiler_params=_params(("parallel", "parallel")),
  )(xb, wg, wu)


def _ffn_bwd_hidden(dzb, wd, a, b, name, tm=512, tn=1408):
  m, d = dzb.shape
  f = wd.shape[0]
  tm, tn = _tile(m, tm), _tile(f, tn)

  def body(dz_ref, wd_ref, a_ref, b_ref, da_ref, db_ref):
    dh = _dot_nt(dz_ref[...], wd_ref[...])
    av = a_ref[...].astype(F32)
    bv = b_ref[...].astype(F32)
    sg = _sigmoid(av)
    da_ref[...] = _bf(dh * bv * (sg * (1.0 + av * (1.0 - sg))))
    db_ref[...] = _bf(dh * (av * sg))

  hspec = pl.BlockSpec((tm, tn), lambda j, i: (i, j))
  sds = jax.ShapeDtypeStruct((m, f), BF16)
  return pl.pallas_call(
      body, name=name, grid=(f // tn, m // tm),
      in_specs=[pl.BlockSpec((tm, d), lambda j, i: (i, 0)), pl.BlockSpec((tn, d), lambda j, i: (j, 0)), hspec, hspec],
      out_specs=[hspec, hspec], out_shape=[sds, sds],
      compiler_params=_params(("parallel", "parallel")),
  )(dzb, wd, a, b)


def _ln_bwd_tail(dy, z_ref, g_ref, dz_ref, dzb_ref, dg_ref, db_ref):
  @pl.when(pl.program_id(0) == 0)
  def _():
    dg_ref[...] = jnp.zeros_like(dg_ref)
    db_ref[...] = jnp.zeros_like(db_ref)

  xhat, rstd = _ln_stats(z_ref[...])
  dz = _ln_bwd(dy, xhat, rstd, g_ref[...])
  dz_ref[...] = dz
  dzb_ref[...] = _bf(dz)
  dg_ref[...] += jnp.sum(dy * xhat, axis=0, keepdims=True)
  db_ref[...] += jnp.sum(dy, axis=0, keepdims=True)


def _ln_bwd_outs(m, d, row, vec):
  return ([row, row, vec, vec],
          [jax.ShapeDtypeStruct((m, d), F32), jax.ShapeDtypeStruct((m, d), BF16),
           jax.ShapeDtypeStruct((1, d), F32), jax.ShapeDtypeStruct((1, d), F32)])


def _ffn_bwd_input_ln(da, db, wg, wu, dz2, z1, g, name, tm=512):
  m, f = da.shape
  d = wg.shape[1]
  tm = _tile(m, tm)

  def body(da_ref, db_ref, wg_ref, wu_ref, dz2_ref, z_ref, g_ref, dz_ref, dzb_ref, dg_ref, dbias_ref):
    dy = DN_ALPHA * dz2_ref[...] + _dot(da_ref[...], wg_ref[...]) + _dot(db_ref[...], wu_ref[...])
    _ln_bwd_tail(dy, z_ref, g_ref, dz_ref, dzb_ref, dg_ref, dbias_ref)

  hspec = pl.BlockSpec((tm, f), lambda i: (i, 0))
  wspec = pl.BlockSpec((f, d), lambda i: (0, 0), pipeline_mode=pl.Buffered(1))
  row = pl.BlockSpec((tm, d), lambda i: (i, 0))
  vec = pl.BlockSpec((1, d), lambda i: (0, 0))
  out_specs, out_shape = _ln_bwd_outs(m, d, row, vec)
  dz, dzb, dg, dbias = pl.pallas_call(
      body, name=name, grid=(m // tm,), in_specs=[hspec, hspec, wspec, wspec, row, row, vec],
      out_specs=out_specs, out_shape=out_shape, compiler_params=_params(("arbitrary",)),
  )(da, db, wg, wu, dz2, z1, g.reshape(1, d))
  return dz, dzb, dg[0], dbias[0]


def _in_proj_bwd_ln(dh, w_in, dz1, z2, g, name, tm=512):
  m, wd = dh.shape
  d = w_in.shape[1]
  tm = _tile(m, tm)

  def body(dh_ref, w_ref, dz1_ref, z_ref, g_ref, dz_ref, dzb_ref, dg_ref, dbias_ref):
    dy = DN_ALPHA * dz1_ref[...] + _dot(dh_ref[...], w_ref[...])
    _ln_bwd_tail(dy, z_ref, g_ref, dz_ref, dzb_ref, dg_ref, dbias_ref)

  row = pl.BlockSpec((tm, d), lambda i: (i, 0))
  vec = pl.BlockSpec((1, d), lambda i: (0, 0))
  out_specs, out_shape = _ln_bwd_outs(m, d, row, vec)
  dz, dzb, dg, dbias = pl.pallas_call(
      body, name=name, grid=(m // tm,),
      in_specs=[pl.BlockSpec((tm, wd), lambda i: (i, 0)),
                pl.BlockSpec((wd, d), lambda i: (0, 0), pipeline_mode=pl.Buffered(1)), row, row, vec],
      out_specs=out_specs, out_shape=out_shape, compiler_params=_params(("arbitrary",)),
  )(dh, w_in, dz1, z2, g.reshape(1, d))
  return dz, dzb, dg[0], dbias[0]


def _alibi_slopes():
  n = N_DIL_HEADS
  return jnp.exp2(-8.0 * (jnp.arange(n, dtype=F32) + 1.0) / n).reshape(1, n)


def _band_consts():
  qi = lax.broadcasted_iota(jnp.int32, (BLOCK, BLOCK), 0)
  ki = lax.broadcasted_iota(jnp.int32, (BLOCK, BLOCK), 1)
  steps_cur = (qi - ki).astype(F32)
  steps_prev = (qi + BLOCK - ki).astype(F32)
  return ki < 64, steps_cur, steps_prev, ki <= qi, ki >= qi


def _rows(start, d):
  if d == 1:
    return pl.ds(pl.multiple_of(start, BLOCK), BLOCK)
  return pl.ds(start, BLOCK, stride=d)


def _fill_bias_tables(bias_sc, slope0, slope1):
  row = lax.broadcasted_iota(jnp.int32, (2 * BLOCK, 2 * BLOCK), 0)
  col = lax.broadcasted_iota(jnp.int32, (2 * BLOCK, 2 * BLOCK), 1)
  qi = jnp.bitwise_and(row, BLOCK - 1)
  ki = jnp.bitwise_and(col, BLOCK - 1)
  is_cur = col >= BLOCK
  steps = jnp.where(is_cur, qi - ki, qi + BLOCK - ki)
  valid = jnp.logical_and(steps >= 0, steps <= BLOCK)
  slope = jnp.where(row >= BLOCK, slope1, slope0)
  dist = slope * steps.astype(F32)
  for p, (_, d) in enumerate(DIL_PATTERNS):
    base = jnp.where(valid, -d * dist, NEG_BIG)
    bias_sc[2 * p] = base
    bias_sc[2 * p + 1] = jnp.where(is_cur, base, NEG_BIG)


def _stack_heads(v2, head0):
  return jnp.concatenate([jnp.where(head0, v2, 0.0), jnp.where(head0, 0.0, v2)], axis=0)


def _unstack_heads(v, head0):
  return jnp.where(head0, v[:BLOCK], v[BLOCK:])


def _block_rows(idx, d, nblk):
  r = idx // nblk
  n = idx % nblk
  cur = _rows(r + n * (BLOCK * d), d)
  prev = _rows(r + jnp.maximum(n - 1, 0) * (BLOCK * d), d)
  return cur, prev, n


def pair_tile(dt):
  return pltpu.VMEM((2 * BLOCK, 2 * BLOCK), dt)


def _two_stage_loop(nb, first_stage, second_stage, buf_a, buf_b):
  assert nb % 2 == 0

  def pair(t, carry):
    i = 2 * t + 1
    first_stage(i, buf_b)
    second_stage(i - 1, buf_a)
    first_stage(i + 1, buf_a)
    second_stage(i, buf_b)
    return carry

  first_stage(0, buf_a)
  lax.fori_loop(0, nb // 2 - 1, pair, 0)
  first_stage(nb - 1, buf_b)
  second_stage(nb - 2, buf_a)
  second_stage(nb - 1, buf_b)


def _attn_fwd(h3, name, gather=None):
  bl, s, _ = h3.shape
  npair = N_DIL_HEADS // 2
  nb = s // BLOCK
  hosted = gather is not None
  steps = bl * npair

  def body(*refs):
    if hosted:
      sl_ref, q_ref, k_ref, v_ref, w_ref, o_ref, lse_ref, g_ref, o_sc, l_sc, bias_sc, s_a, s_b, send_sems, recv_sems = refs
      step = pl.program_id(0) * npair + pl.program_id(1)
      for phase, at in enumerate((0, (3 * steps) // 4)):
        @pl.when(step == at)
        def _(phase=phase):
          _relayed_gather_phase(phase, w_ref, g_ref, send_sems, recv_sems)
    else:
      sl_ref, q_ref, k_ref, v_ref, o_ref, lse_ref, o_sc, l_sc, bias_sc, s_a, s_b = refs
    hp = pl.program_id(1)
    head0 = lax.broadcasted_iota(jnp.int32, (BLOCK, LANES), 1) < 64
    _fill_bias_tables(bias_sc, sl_ref[0, 2 * hp], sl_ref[0, 2 * hp + 1])

    for p, (_, d) in enumerate(DIL_PATTERNS):
      nblk = (s // d) // BLOCK
      two = nblk > 1
      ks = slice(0, 2 * BLOCK) if two else slice(BLOCK, 2 * BLOCK)

      def scores(idx, buf, p=p, d=d, nblk=nblk, two=two, ks=ks):
        cur, prev, n = _block_rows(idx, d, nblk)
        qs = _bf(_stack_heads(q_ref[cur, :], head0) * ATT_SCALE)
        kb = _bf(jnp.concatenate([k_ref[prev, :], k_ref[cur, :]], axis=0)) if two else _bf(k_ref[cur, :])
        first = jnp.where(n == 0, 1, 0) if two else 0
        buf[:, ks] = _dot_nt(qs, kb) + bias_sc[2 * p + first, :, ks]

      def values(idx, buf, p=p, d=d, nblk=nblk, two=two, ks=ks):
        cur, prev, _ = _block_rows(idx, d, nblk)
        sc = buf[:, ks]
        mx = jnp.max(sc, axis=1, keepdims=True)
        pe = jnp.exp(sc - mx)
        den = jnp.sum(pe, axis=1, keepdims=True)
        vb = _bf(jnp.concatenate([v_ref[prev, :], v_ref[cur, :]], axis=0)) if two else _bf(v_ref[cur, :])
        acc = _dot(_bf(pe), vb) / den
        o_sc[p, cur, :] = _unstack_heads(acc, head0)
        l_sc[p, cur, :] = _unstack_heads(jnp.broadcast_to(mx + jnp.log(den), (2 * BLOCK, LANES)), head0)

      _two_stage_loop(nb, scores, values, s_a, s_b)

    def merge(i, carry):
      rows = pl.ds(pl.multiple_of(i * BLOCK, BLOCK), BLOCK)
      l0, l1, l2 = l_sc[0, rows, :], l_sc[1, rows, :], l_sc[2, rows, :]
      mx = jnp.maximum(jnp.maximum(l0, l1), l2)
      e0, e1, e2 = jnp.exp(l0 - mx), jnp.exp(l1 - mx), jnp.exp(l2 - mx)
      tot = e0 + e1 + e2
      o_ref[rows, :] = _bf((e0 * o_sc[0, rows, :] + e1 * o_sc[1, rows, :] + e2 * o_sc[2, rows, :]) / tot)
      lse_ref[rows, :] = mx + jnp.log(tot)
      return carry

    lax.fori_loop(0, nb, merge, 0)

    if hosted:
      @pl.when(step == steps - 1)
      def _():
        _relayed_gather_phase(2, w_ref, g_ref, send_sems, recv_sems)

  def col(off):
    return pl.BlockSpec((None, s, LANES), lambda b, p: (b, 0, off + p))

  in_specs = [pl.BlockSpec(memory_space=pltpu.SMEM), col(0), col(npair), col(2 * npair)]
  out_specs = [col(0), col(0)]
  out_shape = [jax.ShapeDtypeStruct((bl, s, DIL_WIDTH), BF16), jax.ShapeDtypeStruct((bl, s, DIL_WIDTH), F32)]
  scratch = [pltpu.VMEM((3, s, LANES), F32), pltpu.VMEM((3, s, LANES), F32),
             pltpu.VMEM((6, 2 * BLOCK, 2 * BLOCK), F32), pair_tile(F32), pair_tile(F32)]
  args = (_alibi_slopes(), h3, h3, h3)
  if hosted:
    assert (gather.shape[1] // 2) % ROW_ALIGN == 0 and steps >= 4
    in_specs.append(ANY)
    out_specs.append(ANY)
    out_shape.append(jax.ShapeDtypeStruct((4,) + gather.shape, gather.dtype))
    scratch += [pltpu.SemaphoreType.DMA((N_RELAY_COPIES,)), pltpu.SemaphoreType.DMA((N_RELAY_COPIES,))]
    args += (gather,)
  sem = ("arbitrary", "arbitrary") if hosted else ("parallel", "parallel")
  outs = list(pl.pallas_call(
      body, name=name, grid=(bl, npair), in_specs=in_specs, out_specs=out_specs, out_shape=out_shape,
      scratch_shapes=scratch, compiler_params=_params(sem),
  )(*args))
  if hosted:
    outs[2] = _place_own_block(outs[2], gather)
  return outs


def _attn_bwd(h3, out3, lse3, dcat3, name, exchange=None):
  bl, s, _ = h3.shape
  npair = N_DIL_HEADS // 2
  nb = s // BLOCK
  hosted = exchange is not None

  def body(*refs):
    if hosted:
      (sl_ref, q_ref, k_ref, v_ref, o_ref, l_ref, do_ref, ex_ref, dq_out, dk_out, dv_out, got_ref,
       bias_sc, p_a, ds_a, p_b, ds_b, prod_sc, dq_ref, dk_ref, dv_ref, send_sems, recv_sems) = refs
      step = pl.program_id(0) * npair + pl.program_id(1)

      @pl.when(step == 0)
      def _():
        for cp in _chip_exchange_copies(ex_ref, got_ref, send_sems, recv_sems):
          cp.start()
    else:
      (sl_ref, q_ref, k_ref, v_ref, o_ref, l_ref, do_ref, dq_out, dk_out, dv_out,
       bias_sc, p_a, ds_a, p_b, ds_b, prod_sc, dq_ref, dk_ref, dv_ref) = refs
    hp = pl.program_id(1)
    lane = lax.broadcasted_iota(jnp.int32, (BLOCK, LANES), 1)
    head0 = lane < 64
    _fill_bias_tables(bias_sc, sl_ref[0, 2 * hp], sl_ref[0, 2 * hp + 1])
    dq_ref[...] = jnp.zeros_like(dq_ref)
    dk_ref[...] = jnp.zeros_like(dk_ref)
    dv_ref[...] = jnp.zeros_like(dv_ref)
    prod_sc[...] = do_ref[...] * o_ref[...].astype(F32)

    def per_row(v2, pick0, pick1):
      return jnp.concatenate([jnp.sum(jnp.where(pick0, v2, 0.0), axis=1, keepdims=True),
                              jnp.sum(jnp.where(pick1, v2, 0.0), axis=1, keepdims=True)], axis=0)

    for p, (_, d) in enumerate(DIL_PATTERNS):
      nblk = (s // d) // BLOCK
      two = nblk > 1
      ks = slice(0, 2 * BLOCK) if two else slice(BLOCK, 2 * BLOCK)

      def operands(idx, d=d, nblk=nblk, two=two):
        cur, prev, n = _block_rows(idx, d, nblk)
        qs = _bf(_stack_heads(q_ref[cur, :], head0) * ATT_SCALE)
        dos = _bf(_stack_heads(do_ref[cur, :], head0))
        kb = _bf(jnp.concatenate([k_ref[prev, :], k_ref[cur, :]], axis=0)) if two else _bf(k_ref[cur, :])
        return cur, prev, n, qs, dos, kb

      def probs(idx, bufs, p=p, two=two, ks=ks, operands=operands):
        cur, prev, n, qs, dos, kb = operands(idx)
        vb = _bf(jnp.concatenate([v_ref[prev, :], v_ref[cur, :]], axis=0)) if two else _bf(v_ref[cur, :])
        lse = per_row(l_ref[cur, :], lane == 0, lane == 64)
        delta = per_row(prod_sc[cur, :], head0, jnp.logical_not(head0))
        first = jnp.where(n == 0, 1, 0) if two else 0
        pr = jnp.exp(_dot_nt(qs, kb) + bias_sc[2 * p + first, :, ks] - lse)
        bufs[0][:, ks] = _bf(pr)
        bufs[1][:, ks] = _bf(pr * (_dot_nt(dos, vb) - delta))

      def products(idx, bufs, two=two, ks=ks, operands=operands):
        cur, prev, _, qs, dos, kb = operands(idx)
        pr = bufs[0][:, ks]
        ds = bufs[1][:, ks]
        dq_ref[cur, :] += _unstack_heads(_dot(ds, kb), head0) * ATT_SCALE
        dkb = _dot_tn(ds, qs)
        dvb = _dot_tn(pr, dos)
        if two:
          dk_ref[prev, :] += dkb[:BLOCK]
          dv_ref[prev, :] += dvb[:BLOCK]
          dk_ref[cur, :] += dkb[BLOCK:]
          dv_ref[cur, :] += dvb[BLOCK:]
        else:
          dk_ref[cur, :] += dkb
          dv_ref[cur, :] += dvb

      _two_stage_loop(nb, probs, products, (p_a, ds_a), (p_b, ds_b))

    dq_out[...] = _bf(dq_ref[...])
    dk_out[...] = _bf(dk_ref[...])
    dv_out[...] = _bf(dv_ref[...])

    if hosted:
      @pl.when(step == bl * npair - 1)
      def _():
        for cp in _chip_exchange_copies(ex_ref, got_ref, send_sems, recv_sems):
          cp.wait()

  def col(off):
    return pl.BlockSpec((None, s, LANES), lambda b, p: (b, 0, off + p))

  sds = jax.ShapeDtypeStruct((bl, s, DIL_WIDTH), BF16)
  in_specs = [pl.BlockSpec(memory_space=pltpu.SMEM), col(0), col(npair), col(2 * npair), col(0), col(0), col(0)]
  out_specs, out_shape = [col(0), col(0), col(0)], [sds, sds, sds]
  scratch = [pltpu.VMEM((6, 2 * BLOCK, 2 * BLOCK), F32)] + [pair_tile(BF16)] * 4 + [pltpu.VMEM((s, LANES), F32)] * 4
  args = (_alibi_slopes(), h3, h3, h3, out3, lse3, dcat3)
  if hosted:
    in_specs.append(ANY)
    out_specs.append(ANY)
    out_shape.append(jax.ShapeDtypeStruct((3,) + exchange.shape[1:], exchange.dtype))
    scratch += [pltpu.SemaphoreType.DMA((3,)), pltpu.SemaphoreType.DMA((3,))]
    args += (exchange,)
  sem = ("arbitrary", "arbitrary") if hosted else ("parallel", "parallel")
  return pl.pallas_call(
      body, name=name, grid=(bl, npair), in_specs=in_specs, out_specs=out_specs, out_shape=out_shape,
      scratch_shapes=scratch, compiler_params=_params(sem),
  )(*args)


def _attn_fwd_old(h3, name):
  bl, s, _ = h3.shape
  npair = N_DIL_HEADS // 2

  def body(sl_ref, q_ref, k_ref, v_ref, o_ref, lse_ref, o_sc, l_sc):
    hp = pl.program_id(1)
    head0, steps_cur, steps_prev, mask_cur, mask_prev = _band_consts()
    slope = [sl_ref[0, 2 * hp], sl_ref[0, 2 * hp + 1]]

    for p, (_, d) in enumerate(DIL_PATTERNS):
      nblk = (s // d) // BLOCK
      has_prev_block = nblk > 1

      def blk(idx, carry, p=p, d=d, nblk=nblk, has_prev_block=has_prev_block):
        r = idx // nblk
        n = idx % nblk
        cur = _rows(r + n * (BLOCK * d), d)
        q2 = q_ref[cur, :]
        kc = _bf(k_ref[cur, :])
        vc = _bf(v_ref[cur, :])
        if has_prev_block:
          prev = _rows(r + jnp.maximum(n - 1, 0) * (BLOCK * d), d)
          kp = _bf(k_ref[prev, :])
          vp = _bf(v_ref[prev, :])
          first_block = jnp.where(n > 0, 0.0, NEG_BIG)
        outs, lses = [], []
        for j in range(2):
          hm = head0 if j == 0 else jnp.logical_not(head0)
          qj = _bf(jnp.where(hm, q2, 0.0) * ATT_SCALE)
          sc = _dot_nt(qj, kc) - (slope[j] * d) * steps_cur
          sc = jnp.where(mask_cur, sc, NEG_BIG)
          mx = jnp.max(sc, axis=1, keepdims=True)
          if has_prev_block:
            sp = _dot_nt(qj, kp) - (slope[j] * d) * steps_prev + first_block
            sp = jnp.where(mask_prev, sp, NEG_BIG)
            mx = jnp.maximum(mx, jnp.max(sp, axis=1, keepdims=True))
          pc = jnp.exp(sc - mx)
          den = jnp.sum(pc, axis=1, keepdims=True)
          acc = _dot(_bf(pc), vc)
          if has_prev_block:
            pp = jnp.exp(sp - mx)
            den = den + jnp.sum(pp, axis=1, keepdims=True)
            acc = acc + _dot(_bf(pp), vp)
          outs.append(acc / den)
          lses.append(mx + jnp.log(den))
        o_sc[p, cur, :] = jnp.where(head0, outs[0], outs[1])
        l_sc[p, cur, :] = jnp.where(head0, lses[0], lses[1])
        return carry

      lax.fori_loop(0, s // BLOCK, blk, 0, unroll=ATTN_UNROLL)

    def merge(i, carry):
      rows = pl.ds(pl.multiple_of(i * BLOCK, BLOCK), BLOCK)
      l0, l1, l2 = l_sc[0, rows, :], l_sc[1, rows, :], l_sc[2, rows, :]
      mx = jnp.maximum(jnp.maximum(l0, l1), l2)
      e0, e1, e2 = jnp.exp(l0 - mx), jnp.exp(l1 - mx), jnp.exp(l2 - mx)
      tot = e0 + e1 + e2
      o_ref[rows, :] = (e0 * o_sc[0, rows, :] + e1 * o_sc[1, rows, :] + e2 * o_sc[2, rows, :]) / tot
      lse_ref[rows, :] = mx + jnp.log(tot)
      return carry

    lax.fori_loop(0, s // BLOCK, merge, 0)

  def col(off):
    return pl.BlockSpec((None, s, LANES), lambda b, p: (b, 0, off + p))

  sds = jax.ShapeDtypeStruct((bl, s, DIL_WIDTH), F32)
  return pl.pallas_call(
      body, name=name, grid=(bl, npair),
      in_specs=[pl.BlockSpec(memory_space=pltpu.SMEM), col(0), col(npair), col(2 * npair)],
      out_specs=[col(0), col(0)], out_shape=[sds, sds],
      scratch_shapes=[pltpu.VMEM((3, s, LANES), F32), pltpu.VMEM((3, s, LANES), F32)],
      compiler_params=_params(("parallel", "parallel")),
  )(_alibi_slopes(), h3, h3, h3)


def _attn_bwd_old(h3, out3, lse3, dcat3, name):
  bl, s, _ = h3.shape
  npair = N_DIL_HEADS // 2

  def body(sl_ref, q_ref, k_ref, v_ref, o_ref, l_ref, do_ref, dq_ref, dk_ref, dv_ref):
    hp = pl.program_id(1)
    head0, steps_cur, steps_prev, mask_cur, mask_prev = _band_consts()
    lane = lax.broadcasted_iota(jnp.int32, (BLOCK, LANES), 1)
    slope = [sl_ref[0, 2 * hp], sl_ref[0, 2 * hp + 1]]
    dq_ref[...] = jnp.zeros_like(dq_ref)
    dk_ref[...] = jnp.zeros_like(dk_ref)
    dv_ref[...] = jnp.zeros_like(dv_ref)

    for p, (_, d) in enumerate(DIL_PATTERNS):
      nblk = (s // d) // BLOCK
      has_prev_block = nblk > 1

      def blk(idx, carry, d=d, nblk=nblk, has_prev_block=has_prev_block):
        r = idx // nblk
        n = idx % nblk
        cur = _rows(r + n * (BLOCK * d), d)
        q2 = q_ref[cur, :]
        do2 = do_ref[cur, :]
        l2 = l_ref[cur, :]
        prod = do2 * o_ref[cur, :]
        kc = _bf(k_ref[cur, :])
        vc = _bf(v_ref[cur, :])
        if has_prev_block:
          prev = _rows(r + jnp.maximum(n - 1, 0) * (BLOCK * d), d)
          kp = _bf(k_ref[prev, :])
          vp = _bf(v_ref[prev, :])
          first_block = jnp.where(n > 0, 0.0, NEG_BIG)
          dkp = jnp.zeros((BLOCK, LANES), F32)
          dvp = jnp.zeros((BLOCK, LANES), F32)
        dq2 = jnp.zeros((BLOCK, LANES), F32)
        dkc = jnp.zeros((BLOCK, LANES), F32)
        dvc = jnp.zeros((BLOCK, LANES), F32)
        for j in range(2):
          hm = head0 if j == 0 else jnp.logical_not(head0)
          qj = _bf(jnp.where(hm, q2, 0.0) * ATT_SCALE)
          doj = _bf(jnp.where(hm, do2, 0.0))
          lj = jnp.sum(jnp.where(lane == 64 * j, l2, 0.0), axis=1, keepdims=True)
          dj = jnp.sum(jnp.where(hm, prod, 0.0), axis=1, keepdims=True)
          sc = _dot_nt(qj, kc) - (slope[j] * d) * steps_cur
          pc = jnp.exp(jnp.where(mask_cur, sc - lj, NEG_BIG))
          dsc = _bf(pc * (_dot_nt(doj, vc) - dj))
          dq_j = _dot(dsc, kc)
          dkc = dkc + _dot_tn(dsc, qj)
          dvc = dvc + _dot_tn(_bf(pc), doj)
          if has_prev_block:
            sp = _dot_nt(qj, kp) - (slope[j] * d) * steps_prev + first_block
            pp = jnp.exp(jnp.where(mask_prev, sp - lj, NEG_BIG))
            dsp = _bf(pp * (_dot_nt(doj, vp) - dj))
            dq_j = dq_j + _dot(dsp, kp)
            dkp = dkp + _dot_tn(dsp, qj)
            dvp = dvp + _dot_tn(_bf(pp), doj)
          dq2 = dq2 + jnp.where(hm, dq_j, 0.0) * ATT_SCALE
        dq_ref[cur, :] += dq2
        dk_ref[cur, :] += dkc
        dv_ref[cur, :] += dvc
        if has_prev_block:
          dk_ref[prev, :] += dkp
          dv_ref[prev, :] += dvp
        return carry

      lax.fori_loop(0, s // BLOCK, blk, 0, unroll=ATTN_UNROLL)

  def col(off):
    return pl.BlockSpec((None, s, LANES), lambda b, p: (b, 0, off + p))

  sds = jax.ShapeDtypeStruct((bl, s, DIL_WIDTH), F32)
  return pl.pallas_call(
      body, name=name, grid=(bl, npair),
      in_specs=[pl.BlockSpec(memory_space=pltpu.SMEM), col(0), col(npair), col(2 * npair), col(0), col(0), col(0)],
      out_specs=[col(0), col(0), col(0)], out_shape=[sds, sds, sds],
      compiler_params=_params(("parallel", "parallel")),
  )(_alibi_slopes(), h3, h3, h3, out3, lse3, dcat3)


def _mem_heads(tq):
  lane = lax.broadcasted_iota(jnp.int32, (tq, LANES), 1)
  return lane < 64


def _mem_fwd(h3, qcol, mkv3, name, tq=512):
  bl, s, _ = h3.shape
  nm = mkv3.shape[1]
  tq = _tile(s, tq)

  def body(q_ref, kv_ref, o_ref):
    head0 = _mem_heads(tq)
    for lg in range(MEM_WIDTH // LANES):
      cs = slice(lg * LANES, (lg + 1) * LANES)
      q2 = q_ref[:, cs]
      mk = _bf(kv_ref[:, cs])
      mv = _bf(kv_ref[:, MEM_WIDTH + lg * LANES:MEM_WIDTH + (lg + 1) * LANES])
      outs = []
      for j in range(2):
        hm = head0 if j == 0 else jnp.logical_not(head0)
        qj = _bf(jnp.where(hm, q2, 0.0) * ATT_SCALE)
        sc = _dot_nt(qj, mk)
        mx = jnp.max(sc, axis=1, keepdims=True)
        pe = jnp.exp(sc - mx)
        den = jnp.sum(pe, axis=1, keepdims=True)
        outs.append(_dot(_bf(pe / den), mv))
      o_ref[:, cs] = _bf(jnp.where(head0, outs[0], outs[1]))

  return pl.pallas_call(
      body, name=name, grid=(bl, s // tq),
      in_specs=[pl.BlockSpec((None, tq, MEM_WIDTH), lambda b, i: (b, i, qcol)),
                pl.BlockSpec((None, nm, 2 * MEM_WIDTH), lambda b, i: (b, 0, 0))],
      out_specs=pl.BlockSpec((None, tq, MEM_WIDTH), lambda b, i: (b, i, 0)),
      out_shape=jax.ShapeDtypeStruct((bl, s, MEM_WIDTH), BF16),
      compiler_params=_params(("parallel", "parallel")),
  )(h3, mkv3)


def _mem_bwd(h3, qcol, mkv3, dcat3, name, tq=512):
  bl, s, _ = h3.shape
  nm = mkv3.shape[1]
  tq = _tile(s, tq)
  docol = dcat3.shape[2] // MEM_WIDTH - 1

  def body(q_ref, kv_ref, do_ref, dq_ref, dkv_ref):
    i = pl.program_id(1)

    @pl.when(i == 0)
    def _():
      dkv_ref[...] = jnp.zeros_like(dkv_ref)

    head0 = _mem_heads(tq)
    for lg in range(MEM_WIDTH // LANES):
      cs = slice(lg * LANES, (lg + 1) * LANES)
      vs = slice(MEM_WIDTH + lg * LANES, MEM_WIDTH + (lg + 1) * LANES)
      q2 = q_ref[:, cs]
      do2 = do_ref[:, cs]
      mk = _bf(kv_ref[:, cs])
      mv = _bf(kv_ref[:, vs])
      dq2 = jnp.zeros((tq, LANES), F32)
      dmk = jnp.zeros((nm, LANES), F32)
      dmv = jnp.zeros((nm, LANES), F32)
      for j in range(2):
        hm = head0 if j == 0 else jnp.logical_not(head0)
        qj = _bf(jnp.where(hm, q2, 0.0) * ATT_SCALE)
        doj = _bf(jnp.where(hm, do2, 0.0))
        sc = _dot_nt(qj, mk)
        mx = jnp.max(sc, axis=1, keepdims=True)
        pe = jnp.exp(sc - mx)
        pn = pe / jnp.sum(pe, axis=1, keepdims=True)
        pb = _bf(pn)
        dp = _dot_nt(doj, mv)
        dj = jnp.sum(pb.astype(F32) * dp, axis=1, keepdims=True)
        ds = _bf(pn * (dp - dj))
        dq2 = dq2 + jnp.where(hm, _dot(ds, mk), 0.0) * ATT_SCALE
        dmk = dmk + _dot_tn(ds, qj)
        dmv = dmv + _dot_tn(pb, doj)
      dq_ref[:, cs] = _bf(dq2)
      dkv_ref[:, cs] += dmk
      dkv_ref[:, vs] += dmv

  return pl.pallas_call(
      body, name=name, grid=(bl, s // tq),
      in_specs=[pl.BlockSpec((None, tq, MEM_WIDTH), lambda b, i: (b, i, qcol)),
                pl.BlockSpec((None, nm, 2 * MEM_WIDTH), lambda b, i: (b, 0, 0)),
                pl.BlockSpec((None, tq, MEM_WIDTH), lambda b, i: (b, i, docol))],
      out_specs=[pl.BlockSpec((None, tq, MEM_WIDTH), lambda b, i: (b, i, 0)),
                 pl.BlockSpec((None, nm, 2 * MEM_WIDTH), lambda b, i: (b, 0, 0))],
      out_shape=[jax.ShapeDtypeStruct((bl, s, MEM_WIDTH), BF16), jax.ShapeDtypeStruct((bl, nm, 2 * MEM_WIDTH), F32)],
      compiler_params=_params(("parallel", "arbitrary")),
  )(h3, mkv3, dcat3)


def _sgu_consts():
  ti = lax.broadcasted_iota(jnp.int32, (CHUNK, CHUNK), 0)
  si = lax.broadcasted_iota(jnp.int32, (CHUNK, CHUNK), 1)
  return si <= ti, si < 64


def _sgu_bias_lanes(b_s):
  return jnp.repeat(b_s.T, 64, axis=1)


def _sgu_fwd(h2, ln_g, ln_b, w_s, b_s, name, tr=512):
  t, _ = h2.shape
  tr = _tile(t, tr)
  nch = tr // CHUNK
  npair = N_SGU_GROUPS // 2

  def body(u_ref, v_ref, g_ref, b_ref, w_ref, bs_ref, o_ref, vn_sc):
    tril, head0 = _sgu_consts()
    xhat, _ = _ln_stats(_gelu(v_ref[...]))
    vn_sc[...] = _bf(xhat * g_ref[...] + b_ref[...])
    for jp in range(npair):
      cs = slice(jp * LANES, (jp + 1) * LANES)
      w0 = _bf(jnp.where(tril, w_ref[2 * jp], 0.0))
      w1 = _bf(jnp.where(tril, w_ref[2 * jp + 1], 0.0))
      bias = bs_ref[:, cs]
      for c in range(nch):
        rs = slice(c * CHUNK, (c + 1) * CHUNK)
        vb = vn_sc[rs, cs]
        mixed = jnp.where(head0, _dot(w0, vb), _dot(w1, vb)) + bias
        o_ref[rs, cs] = _bf(_gelu(u_ref[rs, cs]) * mixed)

  blk = lambda j: pl.BlockSpec((tr, SGU_WIDTH), lambda i: (i, j))
  vec = pl.BlockSpec((1, SGU_WIDTH), lambda i: (0, 0))
  return pl.pallas_call(
      body, name=name, grid=(t // tr,),
      in_specs=[blk(0), blk(1), vec, vec,
                pl.BlockSpec((N_SGU_GROUPS, CHUNK, CHUNK), lambda i: (0, 0, 0)),
                pl.BlockSpec((CHUNK, SGU_WIDTH), lambda i: (0, 0))],
      out_specs=blk(0), out_shape=jax.ShapeDtypeStruct((t, SGU_WIDTH), BF16),
      scratch_shapes=[pltpu.VMEM((tr, SGU_WIDTH), BF16)],
      compiler_params=_params(("parallel",)),
  )(h2, h2, ln_g.reshape(1, -1), ln_b.reshape(1, -1), w_s, _sgu_bias_lanes(b_s))


def _sgu_bwd(h2, dcat, ln_g, ln_b, w_s, b_s, name, tr=512):
  t, _ = h2.shape
  tr = _tile(t, tr)
  nch = tr // CHUNK
  npair = N_SGU_GROUPS // 2
  nsteps = t // tr

  def body(u_ref, v_ref, dm_ref, g_ref, b_ref, w_ref, bs_ref,
           du_ref, dv_ref, dw_ref, dbs_ref, dg_ref, db_ref, vn_sc, dmx_sc, dvn_sc, mix_sc, dbx_sc):
    i = pl.program_id(0)
    tril, head0 = _sgu_consts()

    @pl.when(i == 0)
    def _():
      dw_ref[...] = jnp.zeros_like(dw_ref)
      dg_ref[...] = jnp.zeros_like(dg_ref)
      db_ref[...] = jnp.zeros_like(db_ref)
      dbx_sc[...] = jnp.zeros_like(dbx_sc)

    gv, gv_der = _gelu_parts(v_ref[...])
    xhat, rstd = _ln_stats(gv)
    g = g_ref[...]
    vn_sc[...] = _bf(xhat * g + b_ref[...])
    gu, gu_der = _gelu_parts(u_ref[...])
    dmix = dm_ref[...]
    dmx_sc[...] = dmix * gu

    for jp in range(npair):
      cs = slice(jp * LANES, (jp + 1) * LANES)
      w0 = _bf(jnp.where(tril, w_ref[2 * jp], 0.0))
      w1 = _bf(jnp.where(tril, w_ref[2 * jp + 1], 0.0))
      bias = bs_ref[:, cs]
      dw0 = jnp.zeros((CHUNK, CHUNK), F32)
      dw1 = jnp.zeros((CHUNK, CHUNK), F32)
      dbx = jnp.zeros((CHUNK, LANES), F32)
      for c in range(nch):
        rs = slice(c * CHUNK, (c + 1) * CHUNK)
        vb = vn_sc[rs, cs]
        mix_sc[rs, cs] = jnp.where(head0, _dot(w0, vb), _dot(w1, vb)) + bias
        dmx = dmx_sc[rs, cs]
        d0 = _bf(jnp.where(head0, dmx, 0.0))
        d1 = _bf(jnp.where(head0, 0.0, dmx))
        dvn_sc[rs, cs] = _dot_tn(w0, d0) + _dot_tn(w1, d1)
        dw0 = dw0 + _dot_nt(d0, vb)
        dw1 = dw1 + _dot_nt(d1, vb)
        dbx = dbx + dmx
      dw_ref[2 * jp] += dw0
      dw_ref[2 * jp + 1] += dw1
      dbx_sc[:, cs] += dbx

    du_ref[...] = _bf(dmix * mix_sc[...] * gu_der)
    dvn = dvn_sc[...]
    dv_ref[...] = _bf(_ln_bwd(dvn, xhat, rstd, g) * gv_der)
    dg_ref[...] += jnp.sum(dvn * xhat, axis=0, keepdims=True)
    db_ref[...] += jnp.sum(dvn, axis=0, keepdims=True)

    @pl.when(i == nsteps - 1)
    def _():
      lane = lax.broadcasted_iota(jnp.int32, (CHUNK, LANES), 1)
      acc = jnp.zeros((CHUNK, LANES), F32)
      for gi in range(N_SGU_GROUPS):
        jp, j = gi // 2, gi % 2
        part = dbx_sc[:, jp * LANES:(jp + 1) * LANES]
        hm = (lane < 64) if j == 0 else (lane >= 64)
        colsum = jnp.sum(jnp.where(hm, part, 0.0), axis=1, keepdims=True)
        acc = jnp.where(lane == gi, colsum, acc)
        dw_ref[gi] = jnp.where(tril, dw_ref[gi], 0.0)
      dbs_ref[...] = acc

  blk = lambda j: pl.BlockSpec((tr, SGU_WIDTH), lambda i: (i, j))
  vec = pl.BlockSpec((1, SGU_WIDTH), lambda i: (0, 0))
  wspec = pl.BlockSpec((N_SGU_GROUPS, CHUNK, CHUNK), lambda i: (0, 0, 0))
  big = lambda dt: pltpu.VMEM((tr, SGU_WIDTH), dt)
  du, dv, dw, dbs, dg, db = pl.pallas_call(
      body, name=name, grid=(nsteps,),
      in_specs=[blk(0), blk(1), blk(0), vec, vec, wspec, pl.BlockSpec((CHUNK, SGU_WIDTH), lambda i: (0, 0))],
      out_specs=[blk(0), blk(0), wspec, pl.BlockSpec((CHUNK, LANES), lambda i: (0, 0)), vec, vec],
      out_shape=[jax.ShapeDtypeStruct((t, SGU_WIDTH), BF16), jax.ShapeDtypeStruct((t, SGU_WIDTH), BF16),
                 jax.ShapeDtypeStruct((N_SGU_GROUPS, CHUNK, CHUNK), F32), jax.ShapeDtypeStruct((CHUNK, LANES), F32),
                 jax.ShapeDtypeStruct((1, SGU_WIDTH), F32), jax.ShapeDtypeStruct((1, SGU_WIDTH), F32)],
      scratch_shapes=[big(BF16), big(F32), big(F32), big(F32), pltpu.VMEM((CHUNK, SGU_WIDTH), F32)],
      compiler_params=_params(("arbitrary",)),
  )(h2, h2, dcat, ln_g.reshape(1, -1), ln_b.reshape(1, -1), w_s, _sgu_bias_lanes(b_s))
  return du, dv, dw, dbs[:, :N_SGU_GROUPS].T, dg[0], db[0]


def _loss_head(xo, tgt, z, g, name, tm=512):
  m, d = xo.shape
  tm = _tile(m, tm)

  def body(x_ref, t_ref, z_ref, g_ref, l_ref, dz_ref, dzb_ref, dg_ref, dbias_ref):
    @pl.when(pl.program_id(0) == 0)
    def _():
      l_ref[...] = jnp.zeros_like(l_ref)

    diff = x_ref[...] - t_ref[...]
    rowsum = jnp.sum(diff * diff, axis=1, keepdims=True)
    tot = jnp.sum(rowsum, axis=0, keepdims=True) * (0.5 / d)
    l_ref[...] += jnp.broadcast_to(tot, l_ref.shape)
    _ln_bwd_tail(diff * (1.0 / d), z_ref, g_ref, dz_ref, dzb_ref, dg_ref, dbias_ref)

  row = pl.BlockSpec((tm, d), lambda i: (i, 0))
  vec = pl.BlockSpec((1, d), lambda i: (0, 0))
  out_specs, out_shape = _ln_bwd_outs(m, d, row, vec)
  l, dz, dzb, dg, dbias = pl.pallas_call(
      body, name=name, grid=(m // tm,), in_specs=[row, row, row, vec],
      out_specs=[pl.BlockSpec((8, LANES), lambda i: (0, 0))] + out_specs,
      out_shape=[jax.ShapeDtypeStruct((8, LANES), F32)] + out_shape,
      compiler_params=_params(("arbitrary",)),
  )(xo, tgt, z, g.reshape(1, d))
  return l[0, 0], dz, dzb, dg[0], dbias[0]


def _local_step(x3, mem3, tgt3, w, late_weights=None, early_exchange=None):
  w = dict(w)
  bl, s, d = x3.shape
  t = bl * s
  nm = mem3.shape[1]
  mem2 = mem3.reshape(bl * nm, d)
  x = x3.reshape(t, d)
  xb = x
  saved = []
  for i in range(DEPTH):
    j = i // 2
    attn = i % 2 == 0
    mkv = _mm(mem2, w["w_mem_kv"][i], "nn", F32, f"mkv_fwd_{i}", tm=1024, tn=512, tk=1024)
    mkv3 = mkv.reshape(bl, nm, 2 * MEM_WIDTH)
    w_in = w["a_w_in"][j] if attn else w["b_w_in"][j]
    h = _mm(xb, w_in, "nt", F32, f"in_proj_{i}", tm=512, tn=w_in.shape[0], tk=d)
    h3 = h.reshape(bl, s, -1)
    if attn and i == 0 and late_weights is not None:
      mix3, lse3, gathered = _attn_fwd(h3, f"dil_attn_fwd_{i}", gather=late_weights.flat)
      for n, layers in late_weights.unpack(gathered).items():
        w[n] = {**w.get(n, {}), **layers}
    elif attn:
      mix3, lse3 = _attn_fwd(h3, f"dil_attn_fwd_{i}")
    if attn:
      mix = mix3.reshape(t, DIL_WIDTH)
      qcol = 3 * DIL_WIDTH // MEM_WIDTH
    else:
      mix = _sgu_fwd(h, w["sgu_ln_g"][j], w["sgu_ln_b"][j], w["sgu_w_s"][j], w["sgu_b_s"][j], f"sgu_fwd_{i}")
      lse3 = None
      qcol = 2 * SGU_WIDTH // MEM_WIDTH
    mo = _mem_fwd(h3, qcol, mkv3, f"mem_attn_fwd_{i}").reshape(t, MEM_WIDTH)
    cat = jnp.concatenate([mix, mo], axis=1)
    z1, xm, xmb = _mm_res_ln(cat, w["w_out"][i], x, w["ln_mix_g"][i], w["ln_mix_b"][i], f"out_proj_ln_{i}", tk=1024)
    a, b, hm = _ffn_up(xmb, w["w_gate"][i], w["w_up"][i], f"ffn_up_{i}")
    z2, xo, xob = _mm_res_ln(hm, w["w_down"][i], xm, w["ln_ffn_g"][i], w["ln_ffn_b"][i], f"ffn_down_ln_{i}", tk=hm.shape[1])
    saved.append(dict(xb=xb, h=h, h3=h3, mkv3=mkv3, mix3=(mix3 if attn else None), lse3=lse3, cat=cat, z1=z1,
                      xmb=xmb, a=a, b=b, hm=hm, z2=z2, qcol=qcol))
    x, xb = xo, xob

  names = ("a_w_in", "b_w_in", "sgu_ln_g", "sgu_ln_b", "sgu_w_s", "sgu_b_s", "w_mem_kv", "w_out",
           "ln_mix_g", "ln_mix_b", "w_gate", "w_up", "w_down", "ln_ffn_g", "ln_ffn_b")
  grads = {n: [None] * len(w[n]) for n in names}
  last = DEPTH - 1
  loss, dz2, dz2b, grads["ln_ffn_g"][last], grads["ln_ffn_b"][last] = _loss_head(
      x, tgt3.reshape(t, d), saved[last]["z2"], w["ln_ffn_g"][last], "loss_head")
  dx = None
  for i in reversed(range(DEPTH)):
    j = i // 2
    attn = i % 2 == 0
    sv = saved[i]
    da, db = _ffn_bwd_hidden(dz2b, w["w_down"][i], sv["a"], sv["b"], f"ffn_bwd_hidden_{i}")
    grads["w_down"][i] = _mm(sv["hm"], dz2b, "tn", F32, f"dw_down_{i}", tm=1408, tn=1024, tk=1024)
    grads["w_gate"][i] = _mm(da, sv["xmb"], "tn", F32, f"dw_gate_{i}", tm=1408, tn=1024, tk=1024)
    grads["w_up"][i] = _mm(db, sv["xmb"], "tn", F32, f"dw_up_{i}", tm=1408, tn=1024, tk=1024)
    dz1, dz1b, grads["ln_mix_g"][i], grads["ln_mix_b"][i] = _ffn_bwd_input_ln(
        da, db, w["w_gate"][i], w["w_up"][i], dz2, sv["z1"], w["ln_mix_g"][i], f"ffn_bwd_input_ln_{i}")
    grads["w_out"][i] = _mm(sv["cat"], dz1b, "tn", F32, f"dw_out_{i}", tm=1024, tn=1024, tk=1024)
    dcat = _mm(dz1b, w["w_out"][i], "nt", F32, f"out_proj_bwd_{i}", tm=1024, tn=1024, tk=1024)
    dcat3 = dcat.reshape(bl, s, -1)
    dqm3, dmkv3 = _mem_bwd(sv["h3"], sv["qcol"], sv["mkv3"], dcat3, f"mem_attn_bwd_{i}")
    grads["w_mem_kv"][i] = _mm(mem2, dmkv3.reshape(bl * nm, 2 * MEM_WIDTH), "tn", F32, f"dw_mem_kv_{i}", tm=1024, tn=512, tk=1024)
    dqm = dqm3.reshape(t, MEM_WIDTH)
    if attn and i == 0 and early_exchange is not None:
      q, (pack, state) = early_exchange(grads)
      dq3, dk3, dv3, x3 = _attn_bwd(sv["h3"], sv["mix3"], sv["lse3"], dcat3, f"dil_attn_bwd_{i}", exchange=q)
      grads["early_exchange"] = (pack, state, x3)
      parts = [dq3.reshape(t, -1), dk3.reshape(t, -1), dv3.reshape(t, -1), dqm]
    elif attn:
      dq3, dk3, dv3 = _attn_bwd(sv["h3"], sv["mix3"], sv["lse3"], dcat3, f"dil_attn_bwd_{i}")
      parts = [dq3.reshape(t, -1), dk3.reshape(t, -1), dv3.reshape(t, -1), dqm]
    else:
      du, dv, dws, dbs, dlg, dlb = _sgu_bwd(sv["h"], dcat, w["sgu_ln_g"][j], w["sgu_ln_b"][j], w["sgu_w_s"][j],
                                             w["sgu_b_s"][j], f"sgu_bwd_{i}")
      grads["sgu_w_s"][j], grads["sgu_b_s"][j], grads["sgu_ln_g"][j], grads["sgu_ln_b"][j] = dws, dbs, dlg, dlb
      parts = [du, dv, dqm]
    dh = jnp.concatenate(parts, axis=1)
    w_in = w["a_w_in"][j] if attn else w["b_w_in"][j]
    grads["a_w_in" if attn else "b_w_in"][j] = _mm(dh, sv["xb"], "tn", F32, f"dw_in_{i}", tm=1280 if attn else 896, tn=1024, tk=1024)
    if i > 0:
      dz2, dz2b, grads["ln_ffn_g"][i - 1], grads["ln_ffn_b"][i - 1] = _in_proj_bwd_ln(
          dh, w_in, dz1, saved[i - 1]["z2"], w["ln_ffn_g"][i - 1], f"in_proj_bwd_ln_{i}")
    else:
      dx = _mm(dh, w_in, "nn", F32, f"in_proj_bwd_{i}", add=dz1, add_scale=DN_ALPHA, tm=512, tn=d, tk=w_in.shape[0])
  return loss, dx.reshape(bl, s, d), grads


def _my_place():
  return lax.axis_index("x"), lax.axis_index("y"), lax.axis_index("c")


def _other_chips(x, y):
  return [(1 - x, y), (x, 1 - y), (1 - x, 1 - y)]


ANY = pl.BlockSpec(memory_space=pl.ANY)


def _all_gather_halves(wl, name):
  _, r, c_ = wl.shape

  def body(w_ref, g_ref, send_sems, recv_sems):
    x, y, c = _my_place()
    me = 2 * x + y
    sibling = (x, y, 1 - c)
    chips = _other_chips(x, y)

    def copy(k, src, dst, to):
      return pltpu.make_async_remote_copy(src_ref=src, dst_ref=dst, send_sem=send_sems.at[k], recv_sem=recv_sems.at[k],
                                          device_id=to, device_id_type=MESH_ID)

    first = [copy(k, w_ref.at[c], g_ref.at[me, c], (px, py, c)) for k, (px, py) in enumerate(chips)]
    for cp in first:
      cp.start()
    passed = []
    for k, (px, py) in enumerate(chips):
      landed = g_ref.at[2 * px + py, c]
      copy(k, landed, landed, (px, py, c)).wait_recv()
      fwd = copy(3 + k, landed, landed, sibling)
      fwd.start()
      passed.append(fwd)
    for k, (px, py) in enumerate(chips):
      theirs = g_ref.at[2 * px + py, 1 - c]
      copy(3 + k, theirs, theirs, sibling).wait_recv()
    for cp in first + passed:
      cp.wait_send()

  got = pl.pallas_call(
      body, name=name, in_specs=[ANY], out_specs=ANY,
      out_shape=jax.ShapeDtypeStruct((4, 2, r, c_), wl.dtype),
      scratch_shapes=[pltpu.SemaphoreType.DMA((6,)), pltpu.SemaphoreType.DMA((6,))],
  )(wl)
  chip = 2 * lax.axis_index("x") + lax.axis_index("y")
  return lax.dynamic_update_slice(got, wl[None], (chip, 0, 0, 0))


def _relayed_gather_phase(phase, w_ref, g_ref, send_sems, recv_sems):
  h = w_ref.shape[1] // 2
  x, y, c = _my_place()
  sibling = (x, y, 1 - c)
  xn, yn, dg = _other_chips(x, y)

  def copy(k, src, dst, to):
    return pltpu.make_async_remote_copy(src_ref=src, dst_ref=dst, send_sem=send_sems.at[k], recv_sem=recv_sems.at[k],
                                        device_id=to, device_id_type=MESH_ID)

  def block(chip, half):
    return g_ref.at[2 * chip[0] + chip[1], half]

  def same(k, ref, to):
    return copy(k, ref, ref, to)

  top, bottom = pl.ds(0, h), pl.ds(h, h)
  sends = [copy(0, w_ref.at[c], block((x, y), c), (*xn, c)), copy(1, w_ref.at[c], block((x, y), c), (*yn, c)),
           same(2, block(xn, c).at[top], (*yn, c)), same(3, block(yn, c).at[bottom], (*xn, c)),
           same(4, block(xn, c), sibling), same(5, block(yn, c), sibling), same(6, block(dg, c), sibling)]
  if phase == 0:
    sends[0].start()
    sends[1].start()
  elif phase == 1:
    same(0, block(xn, c), (*xn, c)).wait_recv()
    sends[2].start()
    sends[4].start()
    same(1, block(yn, c), (*yn, c)).wait_recv()
    sends[3].start()
    sends[5].start()
  else:
    same(2, block(dg, c).at[top], (*yn, c)).wait_recv()
    same(3, block(dg, c).at[bottom], (*xn, c)).wait_recv()
    sends[6].start()
    for k, chip in ((4, xn), (5, yn), (6, dg)):
      same(k, block(chip, 1 - c), sibling).wait_recv()
    for cp in sends:
      cp.wait_send()


N_RELAY_COPIES = 7


def _place_own_block(got, wl):
  chip = 2 * lax.axis_index("x") + lax.axis_index("y")
  return lax.dynamic_update_slice(got, wl[None], (chip, 0, 0, 0))


def _all_gather_relayed(wl, name):
  _, r, c_ = wl.shape
  assert (r // 2) % ROW_ALIGN == 0

  def body(w_ref, g_ref, send_sems, recv_sems):
    for phase in range(3):
      _relayed_gather_phase(phase, w_ref, g_ref, send_sems, recv_sems)

  got = pl.pallas_call(
      body, name=name, in_specs=[ANY], out_specs=ANY,
      out_shape=jax.ShapeDtypeStruct((4, 2, r, c_), wl.dtype),
      scratch_shapes=[pltpu.SemaphoreType.DMA((N_RELAY_COPIES,)), pltpu.SemaphoreType.DMA((N_RELAY_COPIES,))],
  )(wl)
  return _place_own_block(got, wl)


def _sibling_swap(v, name):
  def body(v_ref, o_ref, send_sem, recv_sem):
    x, y, c = _my_place()
    cp = pltpu.make_async_remote_copy(src_ref=v_ref, dst_ref=o_ref, send_sem=send_sem, recv_sem=recv_sem,
                                      device_id=(x, y, 1 - c), device_id_type=MESH_ID)
    cp.start()
    cp.wait()

  return pl.pallas_call(
      body, name=name, in_specs=[ANY], out_specs=ANY, out_shape=jax.ShapeDtypeStruct(v.shape, v.dtype),
      scratch_shapes=[pltpu.SemaphoreType.DMA, pltpu.SemaphoreType.DMA],
  )(v)


def _chip_exchange_copies(q_ref, o_ref, send_sems, recv_sems):
  x, y, c = _my_place()
  return [pltpu.make_async_remote_copy(src_ref=q_ref.at[2 * px + py], dst_ref=o_ref.at[k], send_sem=send_sems.at[k],
                                       recv_sem=recv_sems.at[k], device_id=(px, py, c), device_id_type=MESH_ID)
          for k, (px, py) in enumerate(_other_chips(x, y))]


def _chip_exchange(q, name):
  _, r, c_ = q.shape

  def body(q_ref, o_ref, send_sems, recv_sems):
    cps = _chip_exchange_copies(q_ref, o_ref, send_sems, recv_sems)
    for cp in cps:
      cp.start()
    for cp in cps:
      cp.wait()

  return pl.pallas_call(
      body, name=name, in_specs=[ANY], out_specs=ANY, out_shape=jax.ShapeDtypeStruct((3, r, c_), q.dtype),
      scratch_shapes=[pltpu.SemaphoreType.DMA((3,)), pltpu.SemaphoreType.DMA((3,))],
  )(q)


def _share_halves(v, name):
  theirs = _sibling_swap(v, name)
  c = lax.axis_index("c")
  return jnp.where(c == 0, jnp.concatenate([v, theirs]), jnp.concatenate([theirs, v]))


def _half_spec(tr, c_, pick):
  return pl.BlockSpec((None, None, tr, c_), lambda s, r, place: (s, pick(place), r, 0))


def _cast_other_half(p, place, name, tr=512):
  _, _, r, c_ = p.shape
  tr = _tile(r, tr, 16)

  def body(place_ref, p_ref, o_ref):
    o_ref[...] = _bf(p_ref[...])

  out_spec = pl.BlockSpec((None, tr, c_), lambda s, rr, place: (s, rr, 0))
  return pl.pallas_call(
      body, name=name, out_shape=jax.ShapeDtypeStruct((4, r, c_), BF16),
      grid_spec=pltpu.PrefetchScalarGridSpec(num_scalar_prefetch=1, grid=(4, r // tr),
                                             in_specs=[_half_spec(tr, c_, lambda place: 1 - place[1])], out_specs=out_spec),
      compiler_params=_params(("parallel", "parallel")),
  )(place, p)


def _add_sibling(p, x1, place, name, tr=512):
  _, _, r, c_ = p.shape
  tr = _tile(r, tr, 16)

  def body(place_ref, p_ref, x_ref, o_ref):
    o_ref[...] = _bf(p_ref[...] + x_ref[...].astype(F32))

  row = pl.BlockSpec((None, tr, c_), lambda s, rr, place: (s, rr, 0))
  return pl.pallas_call(
      body, name=name, out_shape=jax.ShapeDtypeStruct((4, r, c_), BF16),
      grid_spec=pltpu.PrefetchScalarGridSpec(num_scalar_prefetch=1, grid=(4, r // tr),
                                             in_specs=[_half_spec(tr, c_, lambda place: place[1]), row], out_specs=row),
      compiler_params=_params(("parallel", "parallel")),
  )(place, p, x1)


def _sum_own(p, x1, x3, place, name, tr=512):
  _, _, r, c_ = p.shape
  tr = _tile(r, tr, 16)

  def body(place_ref, p_ref, x1_ref, x3_ref, o_ref):
    acc = p_ref[...] + x1_ref[...].astype(F32)
    for k in range(3):
      acc = acc + x3_ref[k].astype(F32)
    o_ref[...] = acc

  return pl.pallas_call(
      body, name=name, out_shape=jax.ShapeDtypeStruct((r, c_), F32),
      grid_spec=pltpu.PrefetchScalarGridSpec(
          num_scalar_prefetch=1, grid=(r // tr,),
          in_specs=[pl.BlockSpec((None, None, tr, c_), lambda rr, place: (place[0], place[1], rr, 0)),
                    pl.BlockSpec((None, tr, c_), lambda rr, place: (place[0], rr, 0)),
                    pl.BlockSpec((3, tr, c_), lambda rr, place: (0, rr, 0))],
          out_specs=pl.BlockSpec((tr, c_), lambda rr, place: (rr, 0))),
      compiler_params=_params(("parallel",)),
  )(place, p, x1, x3)


def _reduce_scatter_begin(p, tag):
  x, y, c = _my_place()
  place = jnp.stack([2 * x + y, c]).astype(jnp.int32)
  x1 = _sibling_swap(_cast_other_half(p, place, f"rs_cast_other_half_{tag}"), f"rs_sibling_swap_{tag}")
  return _add_sibling(p, x1, place, f"rs_add_sibling_{tag}"), (p, x1, place)


def _reduce_scatter_end(state, x3, tag):
  p, x1, place = state
  return _share_halves(_sum_own(p, x1, x3, place, f"rs_sum_own_{tag}"), f"rs_share_halves_{tag}")


def _adamw(w, g, m, v, name):
  shape = w.shape
  cols = shape[-1]
  rows = w.size // cols
  tr = _tile(rows, max(8, (256 * 1024) // cols // 8 * 8), 8)

  def body(w_ref, g_ref, m_ref, v_ref, d_ref, nm_ref, nv_ref):
    gv = g_ref[...]
    nm = ADAM_B1 * m_ref[...] + (1.0 - ADAM_B1) * gv
    nv = ADAM_B2 * v_ref[...] + (1.0 - ADAM_B2) * (gv * gv)
    m_hat = nm / (1.0 - ADAM_B1 ** ADAM_STEP)
    v_hat = nv / (1.0 - ADAM_B2 ** ADAM_STEP)
    d_ref[...] = -ADAM_LR * (m_hat / (jnp.sqrt(v_hat) + ADAM_EPS) + ADAM_WD * w_ref[...])
    nm_ref[...] = nm
    nv_ref[...] = nv

  spec = pl.BlockSpec((tr, cols), lambda i: (i, 0))
  sds = jax.ShapeDtypeStruct((rows, cols), F32)
  outs = pl.pallas_call(
      body, name=name, grid=(rows // tr,), in_specs=[spec] * 4, out_specs=[spec] * 3, out_shape=[sds] * 3,
      compiler_params=_params(("parallel",)),
  )(*(t.reshape(rows, cols) for t in (w, g, m, v)))
  return tuple(o.reshape(shape) for o in outs)


SHARDED = (("a_w_in", True), ("b_w_in", True), ("w_mem_kv", False), ("w_out", False), ("w_gate", True),
           ("w_up", True), ("w_down", False))
SMALL_SHARDED = (("sgu_ln_g", 1), ("sgu_ln_b", 1))
REPLICATED = ("sgu_w_s", "sgu_b_s", "ln_mix_g", "ln_mix_b", "ln_ffn_g", "ln_ffn_b")
SMALL_ORDER = ("sgu_w_s", "sgu_b_s", "ln_mix_g", "ln_mix_b", "ln_ffn_g", "ln_ffn_b", "sgu_ln_g", "sgu_ln_b")
ROW_ALIGN = 16


def _pad_to(v, n):
  return jnp.pad(v, (0, n - v.shape[0]))


def _round_up(n, a):
  return -(-n // a) * a


def _exchange_form(t, transposed):
  return jnp.swapaxes(t, 1, 2) if transposed else t


def _to_shard_major(full, axis):
  shp = full.shape
  cut = shp[:axis] + (4, shp[axis] // 4) + shp[axis + 1:]
  return jnp.moveaxis(full.reshape(cut), axis, 0).reshape(4, -1, FLAT_COLS)


def _from_shard_major(rows, shard_shape, axis):
  full = jnp.moveaxis(rows.reshape((4,) + tuple(shard_shape)), 0, axis)
  shp = full.shape
  return full.reshape(shp[:axis] + (shp[axis] * shp[axis + 1],) + shp[axis + 2:])


class _WeightPack:
  def __init__(self, items, small=()):
    segs, self.rows, self.small, off = [], {}, [], 0
    for n, l, b in items:
      seg = b.reshape(-1, FLAT_COLS)
      self.rows[(n, l)] = (off, seg.shape[0], b.shape)
      segs.append(seg)
      off += seg.shape[0]
    if small:
      flat = jnp.concatenate([lax.bitcast_convert_type(v, BF16).reshape(-1) for _, v in small])
      rows = _round_up(flat.shape[0], ROW_ALIGN * FLAT_COLS) // FLAT_COLS
      self.small = [(n, v.shape) for n, v in small]
      self.small_rows = (off, rows)
      segs.append(_pad_to(flat, rows * FLAT_COLS).reshape(rows, FLAT_COLS))
      off += rows
    rows_pad = _round_up(off, 4 * ROW_ALIGN)
    if rows_pad > off:
      segs.append(jnp.zeros((rows_pad - off, FLAT_COLS), BF16))
    self.flat = jnp.concatenate(segs).reshape(2, rows_pad // 2, FLAT_COLS)

  def unpack(self, gathered):
    g = gathered.reshape(4, -1, FLAT_COLS)
    out = {}
    for (n, l), (off, nr, shape) in self.rows.items():
      out.setdefault(n, {})[l] = g[:, off:off + nr].reshape((4 * shape[0],) + shape[1:])
    if self.small:
      off, rows = self.small_rows
      flat = g[:, off:off + rows].reshape(4, rows * FLAT_COLS)
      pos = 0
      for n, shape in self.small:
        sz = 2 * math.prod(shape)
        vals = lax.bitcast_convert_type(flat[:, pos:pos + sz].reshape((4,) + shape + (2,)), F32)
        out[n] = _from_shard_major(vals, shape, len(shape) - 1)
        pos += sz
    return out


FIRST_WEIGHTS = (("a_w_in", 0), ("w_mem_kv", 0))


def _weight_packs(shards):
  blocks = {(n, l): _exchange_form(shards[n], tr)[l].astype(BF16)
            for n, tr in SHARDED for l in range(shards[n].shape[0])}
  first = _WeightPack([(n, l, blocks[(n, l)]) for n, l in FIRST_WEIGHTS],
                      small=[(n, shards[n]) for n, _ in SMALL_SHARDED])
  late = _WeightPack([(n, l, b) for (n, l), b in blocks.items() if (n, l) not in FIRST_WEIGHTS])
  return first, late


def _reduce_grads(grads, shard_shapes):
  early, state, x3 = grads.pop("early_exchange")
  mine_early = _reduce_scatter_end(state, x3, "early")
  late = _GradPack([(n, l, g) for n, _ in SHARDED for l, g in enumerate(grads[n]) if (n, l) not in early.rows])
  q, state = _reduce_scatter_begin(late.p, "late")
  mine_late = _reduce_scatter_end(state, _chip_exchange(q, "rs_chip_exchange_late"), "late")
  out = {}
  for n, tr in SHARDED:
    layers, rows, cols = shard_shapes[n]
    blocks = []
    for l in range(layers):
      pack, mine = (early, mine_early) if (n, l) in early.rows else (late, mine_late)
      off, nr = pack.rows[(n, l)]
      block = mine[off:off + nr]
      blocks.append(block.reshape(cols, rows).T if tr else block.reshape(rows, cols))
    out[n] = jnp.stack(blocks)
  off, quarter_rows = early.rows["small"]
  piece = mine_early[off:off + quarter_rows].reshape(2, quarter_rows // 2, FLAT_COLS)
  small_sum = _all_gather_halves(piece, "gather_small_grads").reshape(-1)
  off = 0
  for n in SMALL_ORDER:
    shape = (len(grads[n]),) + grads[n][0].shape
    sz = math.prod(shape)
    out[n] = small_sum[off:off + sz].reshape(shape)
    off += sz
  return out


class _GradPack:
  def __init__(self, items, small=None):
    segs, self.rows, off = [], {}, 0
    for n, l, g in items:
      seg = _to_shard_major(g, 0)
      self.rows[(n, l)] = (off, seg.shape[1])
      segs.append(seg)
      off += seg.shape[1]
    if small is not None:
      flat = jnp.concatenate([jnp.stack(small[n]).reshape(-1) for n in SMALL_ORDER])
      n_small = _round_up(flat.shape[0], 4 * 2 * 8 * FLAT_COLS)
      quarter_rows = n_small // (4 * FLAT_COLS)
      self.rows["small"] = (off, quarter_rows)
      segs.append(_pad_to(flat, n_small).reshape(4, quarter_rows, FLAT_COLS))
      off += quarter_rows
    rows_pad = _round_up(off, 2 * ROW_ALIGN)
    if rows_pad > off:
      segs.append(jnp.zeros((4, rows_pad - off, FLAT_COLS), F32))
    self.p = jnp.concatenate(segs, axis=1).reshape(4, 2, rows_pad // 2, FLAT_COLS)


def _early_exchange_begin(grads):
  items = [(n, l, g) for n, _ in SHARDED for l, g in enumerate(grads[n]) if g is not None]
  pack = _GradPack(items, small={n: grads[n] for n in SMALL_ORDER})
  q, state = _reduce_scatter_begin(pack.p, "early")
  return q, (pack, state)


WEIGHT_NAMES = ("a_w_in", "b_w_in", "sgu_ln_g", "sgu_ln_b", "sgu_w_s", "sgu_b_s", "w_mem_kv", "w_out",
                "ln_mix_g", "ln_mix_b", "w_gate", "w_up", "w_down", "ln_ffn_g", "ln_ffn_b")


def kernel(x, mem, a_w_in, b_w_in, sgu_ln_g, sgu_ln_b, sgu_w_s, sgu_b_s, w_mem_kv, w_out, ln_mix_g, ln_mix_b, w_gate, w_up, w_down, ln_ffn_g, ln_ffn_b, loss_target, m_a_w_in, m_b_w_in, m_sgu_ln_g, m_sgu_ln_b, m_sgu_w_s, m_sgu_b_s, m_w_mem_kv, m_w_out, m_ln_mix_g, m_ln_mix_b, m_w_gate, m_w_up, m_w_down, m_ln_ffn_g, m_ln_ffn_b, v_a_w_in, v_b_w_in, v_sgu_ln_g, v_sgu_ln_b, v_sgu_w_s, v_sgu_b_s, v_w_mem_kv, v_w_out, v_ln_mix_g, v_ln_mix_b, v_w_gate, v_w_up, v_w_down, v_ln_ffn_g, v_ln_ffn_b):
  weights = dict(a_w_in=a_w_in, b_w_in=b_w_in, sgu_ln_g=sgu_ln_g, sgu_ln_b=sgu_ln_b, sgu_w_s=sgu_w_s, sgu_b_s=sgu_b_s,
                 w_mem_kv=w_mem_kv, w_out=w_out, ln_mix_g=ln_mix_g, ln_mix_b=ln_mix_b, w_gate=w_gate, w_up=w_up,
                 w_down=w_down, ln_ffn_g=ln_ffn_g, ln_ffn_b=ln_ffn_b)
  mom1 = dict(a_w_in=m_a_w_in, b_w_in=m_b_w_in, sgu_ln_g=m_sgu_ln_g, sgu_ln_b=m_sgu_ln_b, sgu_w_s=m_sgu_w_s,
              sgu_b_s=m_sgu_b_s, w_mem_kv=m_w_mem_kv, w_out=m_w_out, ln_mix_g=m_ln_mix_g, ln_mix_b=m_ln_mix_b,
              w_gate=m_w_gate, w_up=m_w_up, w_down=m_w_down, ln_ffn_g=m_ln_ffn_g, ln_ffn_b=m_ln_ffn_b)
  mom2 = dict(a_w_in=v_a_w_in, b_w_in=v_b_w_in, sgu_ln_g=v_sgu_ln_g, sgu_ln_b=v_sgu_ln_b, sgu_w_s=v_sgu_w_s,
              sgu_b_s=v_sgu_b_s, w_mem_kv=v_w_mem_kv, w_out=v_w_out, ln_mix_g=v_ln_mix_g, ln_mix_b=v_ln_mix_b,
              w_gate=v_w_gate, w_up=v_w_up, w_down=v_w_down, ln_ffn_g=v_ln_ffn_g, ln_ffn_b=v_ln_ffn_b)

  first, late = _weight_packs(weights)
  full = first.unpack(_all_gather_relayed(first.flat, "gather_first_weights"))
  for n in REPLICATED:
    full[n] = weights[n]
  loss_part, grad_x, grads = _local_step(x, mem, loss_target, full, late_weights=late,
                                         early_exchange=_early_exchange_begin)
  loss = lax.psum(loss_part, MESH_AXES)

  shard_shapes = {n: weights[n].shape for n, _ in SHARDED}
  red = _reduce_grads(grads, shard_shapes)
  chip = 2 * lax.axis_index("x") + lax.axis_index("y")
  for n, axis in SMALL_SHARDED:
    width = weights[n].shape[axis]
    red[n] = lax.dynamic_slice_in_dim(red[n], chip * width, width, axis)

  small_names = SMALL_ORDER
  def pack(d):
    flat = jnp.concatenate([d[n].reshape(-1) for n in small_names])
    return _pad_to(flat, _round_up(flat.shape[0], 8 * FLAT_COLS)).reshape(-1, FLAT_COLS)
  small_out = _adamw(pack(weights), pack(red), pack(mom1), pack(mom2), "adamw_small")
  delta, new_m, new_v = {}, {}, {}
  off = 0
  for n in small_names:
    sz = weights[n].size
    for dst, src in zip((delta, new_m, new_v), small_out):
      dst[n] = src.reshape(-1)[off:off + sz].reshape(weights[n].shape)
    off += sz
  for n, _ in SHARDED:
    delta[n], new_m[n], new_v[n] = _adamw(weights[n], red[n], mom1[n], mom2[n], f"adamw_{n}")

  return (loss, grad_x, *[red[n] for n in WEIGHT_NAMES], *[delta[n] for n in WEIGHT_NAMES],
          *[new_m[n] for n in WEIGHT_NAMES], *[new_v[n] for n in WEIGHT_NAMES])
```

```python
import functools
import math

import jax
import jax.numpy as jnp
from jax import lax
from jax.experimental import pallas as pl
from jax.experimental.pallas import tpu as pltpu

F32 = jnp.float32
BF16 = jnp.bfloat16

DEPTH = 4
HEAD_DIM = 64
N_DIL_HEADS = 12
DIL_WIDTH = N_DIL_HEADS * HEAD_DIM
DIL_PATTERNS = ((128, 1), (512, 4), (2048, 16))
BLOCK = 128
N_SGU_GROUPS = 12
SGU_WIDTH = N_SGU_GROUPS * 64
CHUNK = 128
N_MEM_HEADS = 4
MEM_WIDTH = N_MEM_HEADS * HEAD_DIM
DN_ALPHA = (2 * DEPTH) ** 0.25
LN_EPS = 1e-5
ATT_SCALE = HEAD_DIM ** -0.5
ADAM_LR = 0.001
ADAM_B1 = 0.9
ADAM_B2 = 0.999
ADAM_EPS = 1e-08
ADAM_WD = 0.01
ADAM_STEP = 10
NEG_BIG = -1e30
ATTN_UNROLL = 2

LANES = 128
FLAT_COLS = 1024
VMEM_LIMIT = 56 * 1024 * 1024
MESH_AXES = ("x", "y", "c")
MESH_ID = pl.DeviceIdType.MESH


def _tile(n, pref, align=LANES):
  if n <= pref:
    return n
  t = (pref // align) * align
  while t >= align:
    if n % t == 0:
      return t
    t -= align
  return n


def _params(sem):
  return pltpu.CompilerParams(dimension_semantics=sem, vmem_limit_bytes=VMEM_LIMIT)


def _dot(a, b):
  return jnp.dot(a, b, preferred_element_type=F32)


def _dot_nt(a, b):
  return lax.dot_general(a, b, (((1,), (1,)), ((), ())), preferred_element_type=F32)


def _dot_tn(a, b):
  return lax.dot_general(a, b, (((0,), (0,)), ((), ())), preferred_element_type=F32)


def _bf(v):
  return v.astype(BF16)


def _ln_stats(z):
  mu = jnp.mean(z, axis=-1, keepdims=True)
  zc = z - mu
  var = jnp.mean(zc * zc, axis=-1, keepdims=True)
  rstd = lax.rsqrt(var + LN_EPS)
  return zc * rstd, rstd


def _ln_bwd(dy, xhat, rstd, g):
  gdy = dy * g
  m1 = jnp.mean(gdy, axis=-1, keepdims=True)
  m2 = jnp.mean(gdy * xhat, axis=-1, keepdims=True)
  return rstd * (gdy - m1 - xhat * m2)


_GELU_C = math.sqrt(2.0 / math.pi)


def _gelu_parts(v):
  v2 = v * v
  t = jnp.tanh(_GELU_C * (v + 0.044715 * v * v2))
  val = 0.5 * v * (1.0 + t)
  der = 0.5 * (1.0 + t) + 0.5 * v * (1.0 - t * t) * (_GELU_C * (1.0 + 3.0 * 0.044715 * v2))
  return val, der


def _gelu(v):
  t = jnp.tanh(_GELU_C * (v + 0.044715 * v * v * v))
  return 0.5 * v * (1.0 + t)


def _sigmoid(v):
  return 1.0 / (1.0 + jnp.exp(-v))


def _mm(a, b, mode, out_dtype, name, add=None, add_scale=1.0, tm=512, tn=512, tk=512):
  if mode == "nn":
    (m, k), (k2, n) = a.shape, b.shape
  elif mode == "nt":
    (m, k), (n, k2) = a.shape, b.shape
  else:
    (k, m), (k2, n) = a.shape, b.shape
  assert k == k2, (a.shape, b.shape, mode)
  tm, tn, tk = _tile(m, tm), _tile(n, tn), _tile(k, tk)
  nk = k // tk
  if mode == "nn":
    a_spec = pl.BlockSpec((tm, tk), lambda i, j, kk: (i, kk))
    b_spec = pl.BlockSpec((tk, tn), lambda i, j, kk: (kk, j))
    dot = _dot
  elif mode == "nt":
    a_spec = pl.BlockSpec((tm, tk), lambda i, j, kk: (i, kk))
    b_spec = pl.BlockSpec((tn, tk), lambda i, j, kk: (j, kk))
    dot = _dot_nt
  else:
    a_spec = pl.BlockSpec((tk, tm), lambda i, j, kk: (kk, i))
    b_spec = pl.BlockSpec((tk, tn), lambda i, j, kk: (kk, j))
    dot = _dot_tn
  o_spec = pl.BlockSpec((tm, tn), lambda i, j, kk: (i, j))
  has_add = add is not None

  def body(*refs):
    if has_add:
      a_ref, b_ref, add_ref, o_ref, acc_ref = refs
    else:
      a_ref, b_ref, o_ref, acc_ref = refs
    kk = pl.program_id(2)

    @pl.when(kk == 0)
    def _():
      acc_ref[...] = jnp.zeros_like(acc_ref)

    acc_ref[...] += dot(_bf(a_ref[...]), _bf(b_ref[...]))

    @pl.when(kk == nk - 1)
    def _():
      r = acc_ref[...]
      if has_add:
        r = r + add_scale * add_ref[...].astype(F32)
      o_ref[...] = r.astype(out_dtype)

  in_specs = [a_spec, b_spec] + ([o_spec] if has_add else [])
  args = (a, b) + ((add,) if has_add else ())
  return pl.pallas_call(
      body, name=name, grid=(m // tm, n // tn, nk), in_specs=in_specs, out_specs=o_spec,
      out_shape=jax.ShapeDtypeStruct((m, n), out_dtype),
      scratch_shapes=[pltpu.VMEM((tm, tn), F32)],
      compiler_params=_params(("parallel", "parallel", "arbitrary")),
  )(*args)


def _mm_res_ln(a, w, res, g, b, name, tm=512, tk=512):
  m, k = a.shape
  d = w.shape[1]
  tm, tk = _tile(m, tm), _tile(k, tk)
  nk = k // tk

  def body(a_ref, w_ref, r_ref, g_ref, b_ref, z_ref, x_ref, xb_ref, acc_ref):
    kk = pl.program_id(1)

    @pl.when(kk == 0)
    def _():
      acc_ref[...] = jnp.zeros_like(acc_ref)

    acc_ref[...] += _dot(_bf(a_ref[...]), _bf(w_ref[...]))

    @pl.when(kk == nk - 1)
    def _():
      z = DN_ALPHA * r_ref[...] + acc_ref[...]
      xhat, _ = _ln_stats(z)
      xn = xhat * g_ref[...] + b_ref[...]
      z_ref[...] = z
      x_ref[...] = xn
      xb_ref[...] = _bf(xn)

  row = pl.BlockSpec((tm, d), lambda i, kk: (i, 0))
  vec = pl.BlockSpec((1, d), lambda i, kk: (0, 0))
  return pl.pallas_call(
      body, name=name, grid=(m // tm, nk),
      in_specs=[pl.BlockSpec((tm, tk), lambda i, kk: (i, kk)), pl.BlockSpec((tk, d), lambda i, kk: (kk, 0)), row, vec, vec],
      out_specs=[row, row, row],
      out_shape=[jax.ShapeDtypeStruct((m, d), F32), jax.ShapeDtypeStruct((m, d), F32), jax.ShapeDtypeStruct((m, d), BF16)],
      scratch_shapes=[pltpu.VMEM((tm, d), F32)],
      compiler_params=_params(("parallel", "arbitrary")),
  )(a, w, res, g.reshape(1, d), b.reshape(1, d))


def _ln_bwd_call(dy, z, g, name, tm=512):
  m, d = z.shape
  tm = _tile(m, tm)
  n = m // tm

  def body(dy_ref, z_ref, g_ref, dz_ref, dzb_ref, dg_ref, db_ref):
    i = pl.program_id(0)

    @pl.when(i == 0)
    def _():
      dg_ref[...] = jnp.zeros_like(dg_ref)
      db_ref[...] = jnp.zeros_like(db_ref)

    dy_v = dy_ref[...]
    xhat, rstd = _ln_stats(z_ref[...])
    dz = _ln_bwd(dy_v, xhat, rstd, g_ref[...])
    dz_ref[...] = dz
    dzb_ref[...] = _bf(dz)
    dg_ref[...] += jnp.sum(dy_v * xhat, axis=0, keepdims=True)
    db_ref[...] += jnp.sum(dy_v, axis=0, keepdims=True)

  row = pl.BlockSpec((tm, d), lambda i: (i, 0))
  vec = pl.BlockSpec((1, d), lambda i: (0, 0))
  dz, dzb, dg, db = pl.pallas_call(
      body, name=name, grid=(n,), in_specs=[row, row, vec], out_specs=[row, row, vec, vec],
      out_shape=[jax.ShapeDtypeStruct((m, d), F32), jax.ShapeDtypeStruct((m, d), BF16),
                 jax.ShapeDtypeStruct((1, d), F32), jax.ShapeDtypeStruct((1, d), F32)],
      compiler_params=_params(("arbitrary",)),
  )(dy, z, g.reshape(1, d))
  return dz, dzb, dg[0], db[0]


def _ffn_up(xb, wg, wu, name, tm=512, tn=1408):
  m, d = xb.shape
  f = wg.shape[0]
  tm, tn = _tile(m, tm), _tile(f, tn)

  def body(x_ref, wg_ref, wu_ref, ga_ref, gb_ref, h_ref):
    xv = x_ref[...]
    a = _dot_nt(xv, wg_ref[...])
    b = _dot_nt(xv, wu_ref[...])
    sg = _sigmoid(a)
    silu = a * sg
    ga_ref[...] = _bf(b * (sg + silu * (1.0 - sg)))
    gb_ref[...] = _bf(silu)
    h_ref[...] = _bf(silu * b)

  wspec = pl.BlockSpec((tn, d), lambda j, i: (j, 0))
  ospec = pl.BlockSpec((tm, tn), lambda j, i: (i, j))
  sds = jax.ShapeDtypeStruct((m, f), BF16)
  return pl.pallas_call(
      body, name=name, grid=(f // tn, m // tm),
      in_specs=[pl.BlockSpec((tm, d), lambda j, i: (i, 0)), wspec, wspec],
      out_specs=[ospec, ospec, ospec], out_shape=[sds, sds, sds],
      compiler_params=_params(("parallel", "parallel")),
  )(xb, wg, wu)


def _ffn_bwd_hidden(dzb, wd, ga, gb, name, tm=512, tn=1408):
  m, d = dzb.shape
  f = wd.shape[0]
  tm, tn = _tile(m, tm), _tile(f, tn)

  def body(dz_ref, wd_ref, ga_ref, gb_ref, da_ref, db_ref):
    dh = _dot_nt(dz_ref[...], wd_ref[...])
    da_ref[...] = _bf(dh * ga_ref[...].astype(F32))
    db_ref[...] = _bf(dh * gb_ref[...].astype(F32))

  hspec = pl.BlockSpec((tm, tn), lambda j, i: (i, j))
  sds = jax.ShapeDtypeStruct((m, f), BF16)
  return pl.pallas_call(
      body, name=name, grid=(f // tn, m // tm),
      in_specs=[pl.BlockSpec((tm, d), lambda j, i: (i, 0)), pl.BlockSpec((tn, d), lambda j, i: (j, 0)), hspec, hspec],
      out_specs=[hspec, hspec], out_shape=[sds, sds],
      compiler_params=_params(("parallel", "parallel")),
  )(dzb, wd, ga, gb)


def _ln_bwd_tail(dy, z_ref, g_ref, dz_ref, dzb_ref, dg_ref, db_ref):
  @pl.when(pl.program_id(0) == 0)
  def _():
    dg_ref[...] = jnp.zeros_like(dg_ref)
    db_ref[...] = jnp.zeros_like(db_ref)

  xhat, rstd = _ln_stats(z_ref[...])
  dz = _ln_bwd(dy, xhat, rstd, g_ref[...])
  dz_ref[...] = dz
  dzb_ref[...] = _bf(dz)
  dg_ref[...] += jnp.sum(dy * xhat, axis=0, keepdims=True)
  db_ref[...] += jnp.sum(dy, axis=0, keepdims=True)


def _ln_bwd_outs(m, d, row, vec):
  return ([row, row, vec, vec],
          [jax.ShapeDtypeStruct((m, d), F32), jax.ShapeDtypeStruct((m, d), BF16),
           jax.ShapeDtypeStruct((1, d), F32), jax.ShapeDtypeStruct((1, d), F32)])


def _ffn_bwd_input_ln(da, db, wg, wu, dz2, z1, g, name, tm=512):
  m, f = da.shape
  d = wg.shape[1]
  tm = _tile(m, tm)

  def body(da_ref, db_ref, wg_ref, wu_ref, dz2_ref, z_ref, g_ref, dz_ref, dzb_ref, dg_ref, dbias_ref):
    dy = DN_ALPHA * dz2_ref[...] + _dot(da_ref[...], wg_ref[...]) + _dot(db_ref[...], wu_ref[...])
    _ln_bwd_tail(dy, z_ref, g_ref, dz_ref, dzb_ref, dg_ref, dbias_ref)

  hspec = pl.BlockSpec((tm, f), lambda i: (i, 0))
  wspec = pl.BlockSpec((f, d), lambda i: (0, 0), pipeline_mode=pl.Buffered(1))
  row = pl.BlockSpec((tm, d), lambda i: (i, 0))
  vec = pl.BlockSpec((1, d), lambda i: (0, 0))
  out_specs, out_shape = _ln_bwd_outs(m, d, row, vec)
  dz, dzb, dg, dbias = pl.pallas_call(
      body, name=name, grid=(m // tm,), in_specs=[hspec, hspec, wspec, wspec, row, row, vec],
      out_specs=out_specs, out_shape=out_shape, compiler_params=_params(("arbitrary",)),
  )(da, db, wg, wu, dz2, z1, g.reshape(1, d))
  return dz, dzb, dg[0], dbias[0]


def _in_proj_bwd_ln(dh, w_in, dz1, z2, g, name, tm=512):
  m, wd = dh.shape
  d = w_in.shape[1]
  tm = _tile(m, tm)

  def body(dh_ref, w_ref, dz1_ref, z_ref, g_ref, dz_ref, dzb_ref, dg_ref, dbias_ref):
    dy = DN_ALPHA * dz1_ref[...] + _dot(dh_ref[...], w_ref[...])
    _ln_bwd_tail(dy, z_ref, g_ref, dz_ref, dzb_ref, dg_ref, dbias_ref)

  row = pl.BlockSpec((tm, d), lambda i: (i, 0))
  vec = pl.BlockSpec((1, d), lambda i: (0, 0))
  out_specs, out_shape = _ln_bwd_outs(m, d, row, vec)
  dz, dzb, dg, dbias = pl.pallas_call(
      body, name=name, grid=(m // tm,),
      in_specs=[pl.BlockSpec((tm, wd), lambda i: (i, 0)),
                pl.BlockSpec((wd, d), lambda i: (0, 0), pipeline_mode=pl.Buffered(1)), row, row, vec],
      out_specs=out_specs, out_shape=out_shape, compiler_params=_params(("arbitrary",)),
  )(dh, w_in, dz1, z2, g.reshape(1, d))
  return dz, dzb, dg[0], dbias[0]


def _alibi_slopes():
  n = N_DIL_HEADS
  return jnp.exp2(-8.0 * (jnp.arange(n, dtype=F32) + 1.0) / n).reshape(1, n)


def _band_consts():
  qi = lax.broadcasted_iota(jnp.int32, (BLOCK, BLOCK), 0)
  ki = lax.broadcasted_iota(jnp.int32, (BLOCK, BLOCK), 1)
  steps_cur = (qi - ki).astype(F32)
  steps_prev = (qi + BLOCK - ki).astype(F32)
  return ki < 64, steps_cur, steps_prev, ki <= qi, ki >= qi


def _rows(start, d):
  if d == 1:
    return pl.ds(pl.multiple_of(start, BLOCK), BLOCK)
  return pl.ds(start, BLOCK, stride=d)


def _fill_bias_tables(bias_sc, slope0, slope1):
  row = lax.broadcasted_iota(jnp.int32, (2 * BLOCK, 2 * BLOCK), 0)
  col = lax.broadcasted_iota(jnp.int32, (2 * BLOCK, 2 * BLOCK), 1)
  qi = jnp.bitwise_and(row, BLOCK - 1)
  ki = jnp.bitwise_and(col, BLOCK - 1)
  is_cur = col >= BLOCK
  steps = jnp.where(is_cur, qi - ki, qi + BLOCK - ki)
  valid = jnp.logical_and(steps >= 0, steps <= BLOCK)
  slope = jnp.where(row >= BLOCK, slope1, slope0)
  dist = slope * steps.astype(F32)
  for p, (_, d) in enumerate(DIL_PATTERNS):
    base = jnp.where(valid, -d * dist, NEG_BIG)
    bias_sc[2 * p] = base
    bias_sc[2 * p + 1] = jnp.where(is_cur, base, NEG_BIG)


def _stack_heads(v2, head0):
  return jnp.concatenate([jnp.where(head0, v2, 0.0), jnp.where(head0, 0.0, v2)], axis=0)


def _unstack_heads(v, head0):
  return jnp.where(head0, v[:BLOCK], v[BLOCK:])


def _block_rows(idx, d, nblk):
  r = idx // nblk
  n = idx % nblk
  cur = _rows(r + n * (BLOCK * d), d)
  prev = _rows(r + jnp.maximum(n - 1, 0) * (BLOCK * d), d)
  return cur, prev, n


def pair_tile(dt):
  return pltpu.VMEM((2 * BLOCK, 2 * BLOCK), dt)


def _two_stage_loop(nb, first_stage, second_stage, buf_a, buf_b):
  assert nb % 2 == 0

  def pair(t, carry):
    i = 2 * t + 1
    first_stage(i, buf_b)
    second_stage(i - 1, buf_a)
    first_stage(i + 1, buf_a)
    second_stage(i, buf_b)
    return carry

  first_stage(0, buf_a)
  lax.fori_loop(0, nb // 2 - 1, pair, 0)
  first_stage(nb - 1, buf_b)
  second_stage(nb - 2, buf_a)
  second_stage(nb - 1, buf_b)


def _attn_fwd(h3, name, gather=None):
  bl, s, _ = h3.shape
  npair = N_DIL_HEADS // 2
  nb = s // BLOCK
  hosted = gather is not None
  steps = bl * npair

  def body(*refs):
    if hosted:
      sl_ref, q_ref, k_ref, v_ref, w_ref, o_ref, lse_ref, g_ref, o_sc, l_sc, bias_sc, s_a, s_b, send_sems, recv_sems = refs
      step = pl.program_id(0) * npair + pl.program_id(1)
      for phase, at in enumerate((0, (3 * steps) // 4)):
        @pl.when(step == at)
        def _(phase=phase):
          _relayed_gather_phase(phase, w_ref, g_ref, send_sems, recv_sems)
    else:
      sl_ref, q_ref, k_ref, v_ref, o_ref, lse_ref, o_sc, l_sc, bias_sc, s_a, s_b = refs
    hp = pl.program_id(1)
    head0 = lax.broadcasted_iota(jnp.int32, (BLOCK, LANES), 1) < 64
    _fill_bias_tables(bias_sc, sl_ref[0, 2 * hp], sl_ref[0, 2 * hp + 1])

    for p, (_, d) in enumerate(DIL_PATTERNS):
      nblk = (s // d) // BLOCK
      two = nblk > 1
      ks = slice(0, 2 * BLOCK) if two else slice(BLOCK, 2 * BLOCK)

      def scores(idx, buf, p=p, d=d, nblk=nblk, two=two, ks=ks):
        cur, prev, n = _block_rows(idx, d, nblk)
        qs = _bf(_stack_heads(q_ref[cur, :], head0) * ATT_SCALE)
        kb = _bf(jnp.concatenate([k_ref[prev, :], k_ref[cur, :]], axis=0)) if two else _bf(k_ref[cur, :])
        first = jnp.where(n == 0, 1, 0) if two else 0
        buf[:, ks] = _dot_nt(qs, kb) + bias_sc[2 * p + first, :, ks]

      def values(idx, buf, p=p, d=d, nblk=nblk, two=two, ks=ks):
        cur, prev, _ = _block_rows(idx, d, nblk)
        sc = buf[:, ks]
        mx = jnp.max(sc, axis=1, keepdims=True)
        pe = jnp.exp(sc - mx)
        den = jnp.sum(pe, axis=1, keepdims=True)
        vb = _bf(jnp.concatenate([v_ref[prev, :], v_ref[cur, :]], axis=0)) if two else _bf(v_ref[cur, :])
        acc = _dot(_bf(pe), vb) / den
        o_sc[p, cur, :] = _unstack_heads(acc, head0)
        l_sc[p, cur, :] = _unstack_heads(jnp.broadcast_to(mx + jnp.log(den), (2 * BLOCK, LANES)), head0)

      _two_stage_loop(nb, scores, values, s_a, s_b)

    def merge(i, carry):
      rows = pl.ds(pl.multiple_of(i * BLOCK, BLOCK), BLOCK)
      l0, l1, l2 = l_sc[0, rows, :], l_sc[1, rows, :], l_sc[2, rows, :]
      mx = jnp.maximum(jnp.maximum(l0, l1), l2)
      e0, e1, e2 = jnp.exp(l0 - mx), jnp.exp(l1 - mx), jnp.exp(l2 - mx)
      tot = e0 + e1 + e2
      o_ref[rows, :] = _bf((e0 * o_sc[0, rows, :] + e1 * o_sc[1, rows, :] + e2 * o_sc[2, rows, :]) / tot)
      lse_ref[rows, :] = mx + jnp.log(tot)
      return carry

    lax.fori_loop(0, nb, merge, 0)

    if hosted:
      @pl.when(step == steps - 1)
      def _():
        _relayed_gather_phase(2, w_ref, g_ref, send_sems, recv_sems)

  def col(off):
    return pl.BlockSpec((None, s, LANES), lambda b, p: (b, 0, off + p))

  in_specs = [pl.BlockSpec(memory_space=pltpu.SMEM), col(0), col(npair), col(2 * npair)]
  out_specs = [col(0), col(0)]
  out_shape = [jax.ShapeDtypeStruct((bl, s, DIL_WIDTH), BF16), jax.ShapeDtypeStruct((bl, s, DIL_WIDTH), F32)]
  scratch = [pltpu.VMEM((3, s, LANES), F32), pltpu.VMEM((3, s, LANES), F32),
             pltpu.VMEM((6, 2 * BLOCK, 2 * BLOCK), F32), pair_tile(F32), pair_tile(F32)]
  args = (_alibi_slopes(), h3, h3, h3)
  if hosted:
    assert (gather.shape[1] // 2) % ROW_ALIGN == 0 and steps >= 4
    in_specs.append(ANY)
    out_specs.append(ANY)
    out_shape.append(jax.ShapeDtypeStruct((4,) + gather.shape, gather.dtype))
    scratch += [pltpu.SemaphoreType.DMA((N_RELAY_COPIES,)), pltpu.SemaphoreType.DMA((N_RELAY_COPIES,))]
    args += (gather,)
  sem = ("arbitrary", "arbitrary") if hosted else ("parallel", "parallel")
  outs = list(pl.pallas_call(
      body, name=name, grid=(bl, npair), in_specs=in_specs, out_specs=out_specs, out_shape=out_shape,
      scratch_shapes=scratch, compiler_params=_params(sem),
  )(*args))
  if hosted:
    outs[2] = _place_own_block(outs[2], gather)
  return outs


def _attn_bwd(h3, out3, lse3, dcat3, name, exchange=None):
  bl, s, _ = h3.shape
  npair = N_DIL_HEADS // 2
  nb = s // BLOCK
  hosted = exchange is not None

  def body(*refs):
    if hosted:
      (sl_ref, q_ref, k_ref, v_ref, o_ref, l_ref, do_ref, ex_ref, dq_out, dk_out, dv_out, got_ref,
       bias_sc, p_a, ds_a, p_b, ds_b, prod_sc, dq_ref, dk_ref, dv_ref, send_sems, recv_sems) = refs
      step = pl.program_id(0) * npair + pl.program_id(1)

      @pl.when(step == 0)
      def _():
        for cp in _chip_exchange_copies(ex_ref, got_ref, send_sems, recv_sems):
          cp.start()
    else:
      (sl_ref, q_ref, k_ref, v_ref, o_ref, l_ref, do_ref, dq_out, dk_out, dv_out,
       bias_sc, p_a, ds_a, p_b, ds_b, prod_sc, dq_ref, dk_ref, dv_ref) = refs
    hp = pl.program_id(1)
    lane = lax.broadcasted_iota(jnp.int32, (BLOCK, LANES), 1)
    head0 = lane < 64
    _fill_bias_tables(bias_sc, sl_ref[0, 2 * hp], sl_ref[0, 2 * hp + 1])
    dq_ref[...] = jnp.zeros_like(dq_ref)
    dk_ref[...] = jnp.zeros_like(dk_ref)
    dv_ref[...] = jnp.zeros_like(dv_ref)
    prod_sc[...] = do_ref[...] * o_ref[...].astype(F32)

    def per_row(v2, pick0, pick1):
      return jnp.concatenate([jnp.sum(jnp.where(pick0, v2, 0.0), axis=1, keepdims=True),
                              jnp.sum(jnp.where(pick1, v2, 0.0), axis=1, keepdims=True)], axis=0)

    for p, (_, d) in enumerate(DIL_PATTERNS):
      nblk = (s // d) // BLOCK
      two = nblk > 1
      ks = slice(0, 2 * BLOCK) if two else slice(BLOCK, 2 * BLOCK)

      def operands(idx, d=d, nblk=nblk, two=two):
        cur, prev, n = _block_rows(idx, d, nblk)
        qs = _bf(_stack_heads(q_ref[cur, :], head0) * ATT_SCALE)
        dos = _bf(_stack_heads(do_ref[cur, :], head0))
        kb = _bf(jnp.concatenate([k_ref[prev, :], k_ref[cur, :]], axis=0)) if two else _bf(k_ref[cur, :])
        return cur, prev, n, qs, dos, kb

      def probs(idx, bufs, p=p, two=two, ks=ks, operands=operands):
        cur, prev, n, qs, dos, kb = operands(idx)
        vb = _bf(jnp.concatenate([v_ref[prev, :], v_ref[cur, :]], axis=0)) if two else _bf(v_ref[cur, :])
        lse = per_row(l_ref[cur, :], lane == 0, lane == 64)
        delta = per_row(prod_sc[cur, :], head0, jnp.logical_not(head0))
        first = jnp.where(n == 0, 1, 0) if two else 0
        pr = jnp.exp(_dot_nt(qs, kb) + bias_sc[2 * p + first, :, ks] - lse)
        bufs[0][:, ks] = _bf(pr)
        bufs[1][:, ks] = _bf(pr * (_dot_nt(dos, vb) - delta))

      def products(idx, bufs, two=two, ks=ks, operands=operands):
        cur, prev, _, qs, dos, kb = operands(idx)
        pr = bufs[0][:, ks]
        ds = bufs[1][:, ks]
        dq_ref[cur, :] += _unstack_heads(_dot(ds, kb), head0) * ATT_SCALE
        dkb = _dot_tn(ds, qs)
        dvb = _dot_tn(pr, dos)
        if two:
          dk_ref[prev, :] += dkb[:BLOCK]
          dv_ref[prev, :] += dvb[:BLOCK]
          dk_ref[cur, :] += dkb[BLOCK:]
          dv_ref[cur, :] += dvb[BLOCK:]
        else:
          dk_ref[cur, :] += dkb
          dv_ref[cur, :] += dvb

      _two_stage_loop(nb, probs, products, (p_a, ds_a), (p_b, ds_b))

    dq_out[...] = _bf(dq_ref[...])
    dk_out[...] = _bf(dk_ref[...])
    dv_out[...] = _bf(dv_ref[...])

    if hosted:
      @pl.when(step == bl * npair - 1)
      def _():
        for cp in _chip_exchange_copies(ex_ref, got_ref, send_sems, recv_sems):
          cp.wait()

  def col(off):
    return pl.BlockSpec((None, s, LANES), lambda b, p: (b, 0, off + p))

  sds = jax.ShapeDtypeStruct((bl, s, DIL_WIDTH), BF16)
  in_specs = [pl.BlockSpec(memory_space=pltpu.SMEM), col(0), col(npair), col(2 * npair), col(0), col(0), col(0)]
  out_specs, out_shape = [col(0), col(0), col(0)], [sds, sds, sds]
  scratch = [pltpu.VMEM((6, 2 * BLOCK, 2 * BLOCK), F32)] + [pair_tile(BF16)] * 4 + [pltpu.VMEM((s, LANES), F32)] * 4
  args = (_alibi_slopes(), h3, h3, h3, out3, lse3, dcat3)
  if hosted:
    in_specs.append(ANY)
    out_specs.append(ANY)
    out_shape.append(jax.ShapeDtypeStruct((3,) + exchange.shape[1:], exchange.dtype))
    scratch += [pltpu.SemaphoreType.DMA((3,)), pltpu.SemaphoreType.DMA((3,))]
    args += (exchange,)
  sem = ("arbitrary", "arbitrary") if hosted else ("parallel", "parallel")
  return pl.pallas_call(
      body, name=name, grid=(bl, npair), in_specs=in_specs, out_specs=out_specs, out_shape=out_shape,
      scratch_shapes=scratch, compiler_params=_params(sem),
  )(*args)


def _attn_fwd_old(h3, name):
  bl, s, _ = h3.shape
  npair = N_DIL_HEADS // 2

  def body(sl_ref, q_ref, k_ref, v_ref, o_ref, lse_ref, o_sc, l_sc):
    hp = pl.program_id(1)
    head0, steps_cur, steps_prev, mask_cur, mask_prev = _band_consts()
    slope = [sl_ref[0, 2 * hp], sl_ref[0, 2 * hp + 1]]

    for p, (_, d) in enumerate(DIL_PATTERNS):
      nblk = (s // d) // BLOCK
      has_prev_block = nblk > 1

      def blk(idx, carry, p=p, d=d, nblk=nblk, has_prev_block=has_prev_block):
        r = idx // nblk
        n = idx % nblk
        cur = _rows(r + n * (BLOCK * d), d)
        q2 = q_ref[cur, :]
        kc = _bf(k_ref[cur, :])
        vc = _bf(v_ref[cur, :])
        if has_prev_block:
          prev = _rows(r + jnp.maximum(n - 1, 0) * (BLOCK * d), d)
          kp = _bf(k_ref[prev, :])
          vp = _bf(v_ref[prev, :])
          first_block = jnp.where(n > 0, 0.0, NEG_BIG)
        outs, lses = [], []
        for j in range(2):
          hm = head0 if j == 0 else jnp.logical_not(head0)
          qj = _bf(jnp.where(hm, q2, 0.0) * ATT_SCALE)
          sc = _dot_nt(qj, kc) - (slope[j] * d) * steps_cur
          sc = jnp.where(mask_cur, sc, NEG_BIG)
          mx = jnp.max(sc, axis=1, keepdims=True)
          if has_prev_block:
            sp = _dot_nt(qj, kp) - (slope[j] * d) * steps_prev + first_block
            sp = jnp.where(mask_prev, sp, NEG_BIG)
            mx = jnp.maximum(mx, jnp.max(sp, axis=1, keepdims=True))
          pc = jnp.exp(sc - mx)
          den = jnp.sum(pc, axis=1, keepdims=True)
          acc = _dot(_bf(pc), vc)
          if has_prev_block:
            pp = jnp.exp(sp - mx)
            den = den + jnp.sum(pp, axis=1, keepdims=True)
            acc = acc + _dot(_bf(pp), vp)
          outs.append(acc / den)
          lses.append(mx + jnp.log(den))
        o_sc[p, cur, :] = jnp.where(head0, outs[0], outs[1])
        l_sc[p, cur, :] = jnp.where(head0, lses[0], lses[1])
        return carry

      lax.fori_loop(0, s // BLOCK, blk, 0, unroll=ATTN_UNROLL)

    def merge(i, carry):
      rows = pl.ds(pl.multiple_of(i * BLOCK, BLOCK), BLOCK)
      l0, l1, l2 = l_sc[0, rows, :], l_sc[1, rows, :], l_sc[2, rows, :]
      mx = jnp.maximum(jnp.maximum(l0, l1), l2)
      e0, e1, e2 = jnp.exp(l0 - mx), jnp.exp(l1 - mx), jnp.exp(l2 - mx)
      tot = e0 + e1 + e2
      o_ref[rows, :] = (e0 * o_sc[0, rows, :] + e1 * o_sc[1, rows, :] + e2 * o_sc[2, rows, :]) / tot
      lse_ref[rows, :] = mx + jnp.log(tot)
      return carry

    lax.fori_loop(0, s // BLOCK, merge, 0)

  def col(off):
    return pl.BlockSpec((None, s, LANES), lambda b, p: (b, 0, off + p))

  sds = jax.ShapeDtypeStruct((bl, s, DIL_WIDTH), F32)
  return pl.pallas_call(
      body, name=name, grid=(bl, npair),
      in_specs=[pl.BlockSpec(memory_space=pltpu.SMEM), col(0), col(npair), col(2 * npair)],
      out_specs=[col(0), col(0)], out_shape=[sds, sds],
      scratch_shapes=[pltpu.VMEM((3, s, LANES), F32), pltpu.VMEM((3, s, LANES), F32)],
      compiler_params=_params(("parallel", "parallel")),
  )(_alibi_slopes(), h3, h3, h3)


def _attn_bwd_old(h3, out3, lse3, dcat3, name):
  bl, s, _ = h3.shape
  npair = N_DIL_HEADS // 2

  def body(sl_ref, q_ref, k_ref, v_ref, o_ref, l_ref, do_ref, dq_ref, dk_ref, dv_ref):
    hp = pl.program_id(1)
    head0, steps_cur, steps_prev, mask_cur, mask_prev = _band_consts()
    lane = lax.broadcasted_iota(jnp.int32, (BLOCK, LANES), 1)
    slope = [sl_ref[0, 2 * hp], sl_ref[0, 2 * hp + 1]]
    dq_ref[...] = jnp.zeros_like(dq_ref)
    dk_ref[...] = jnp.zeros_like(dk_ref)
    dv_ref[...] = jnp.zeros_like(dv_ref)

    for p, (_, d) in enumerate(DIL_PATTERNS):
      nblk = (s // d) // BLOCK
      has_prev_block = nblk > 1

      def blk(idx, carry, d=d, nblk=nblk, has_prev_block=has_prev_block):
        r = idx // nblk
        n = idx % nblk
        cur = _rows(r + n * (BLOCK * d), d)
        q2 = q_ref[cur, :]
        do2 = do_ref[cur, :]
        l2 = l_ref[cur, :]
        prod = do2 * o_ref[cur, :]
        kc = _bf(k_ref[cur, :])
        vc = _bf(v_ref[cur, :])
        if has_prev_block:
          prev = _rows(r + jnp.maximum(n - 1, 0) * (BLOCK * d), d)
          kp = _bf(k_ref[prev, :])
          vp = _bf(v_ref[prev, :])
          first_block = jnp.where(n > 0, 0.0, NEG_BIG)
          dkp = jnp.zeros((BLOCK, LANES), F32)
          dvp = jnp.zeros((BLOCK, LANES), F32)
        dq2 = jnp.zeros((BLOCK, LANES), F32)
        dkc = jnp.zeros((BLOCK, LANES), F32)
        dvc = jnp.zeros((BLOCK, LANES), F32)
        for j in range(2):
          hm = head0 if j == 0 else jnp.logical_not(head0)
          qj = _bf(jnp.where(hm, q2, 0.0) * ATT_SCALE)
          doj = _bf(jnp.where(hm, do2, 0.0))
          lj = jnp.sum(jnp.where(lane == 64 * j, l2, 0.0), axis=1, keepdims=True)
          dj = jnp.sum(jnp.where(hm, prod, 0.0), axis=1, keepdims=True)
          sc = _dot_nt(qj, kc) - (slope[j] * d) * steps_cur
          pc = jnp.exp(jnp.where(mask_cur, sc - lj, NEG_BIG))
          dsc = _bf(pc * (_dot_nt(doj, vc) - dj))
          dq_j = _dot(dsc, kc)
          dkc = dkc + _dot_tn(dsc, qj)
          dvc = dvc + _dot_tn(_bf(pc), doj)
          if has_prev_block:
            sp = _dot_nt(qj, kp) - (slope[j] * d) * steps_prev + first_block
            pp = jnp.exp(jnp.where(mask_prev, sp - lj, NEG_BIG))
            dsp = _bf(pp * (_dot_nt(doj, vp) - dj))
            dq_j = dq_j + _dot(dsp, kp)
            dkp = dkp + _dot_tn(dsp, qj)
            dvp = dvp + _dot_tn(_bf(pp), doj)
          dq2 = dq2 + jnp.where(hm, dq_j, 0.0) * ATT_SCALE
        dq_ref[cur, :] += dq2
        dk_ref[cur, :] += dkc
        dv_ref[cur, :] += dvc
        if has_prev_block:
          dk_ref[prev, :] += dkp
          dv_ref[prev, :] += dvp
        return carry

      lax.fori_loop(0, s // BLOCK, blk, 0, unroll=ATTN_UNROLL)

  def col(off):
    return pl.BlockSpec((None, s, LANES), lambda b, p: (b, 0, off + p))

  sds = jax.ShapeDtypeStruct((bl, s, DIL_WIDTH), F32)
  return pl.pallas_call(
      body, name=name, grid=(bl, npair),
      in_specs=[pl.BlockSpec(memory_space=pltpu.SMEM), col(0), col(npair), col(2 * npair), col(0), col(0), col(0)],
      out_specs=[col(0), col(0), col(0)], out_shape=[sds, sds, sds],
      compiler_params=_params(("parallel", "parallel")),
  )(_alibi_slopes(), h3, h3, h3, out3, lse3, dcat3)


def _mem_heads(tq):
  lane = lax.broadcasted_iota(jnp.int32, (tq, LANES), 1)
  return lane < 64


def _mem_fwd(h3, qcol, mkv3, name, tq=512):
  bl, s, _ = h3.shape
  nm = mkv3.shape[1]
  tq = _tile(s, tq)

  def body(q_ref, kv_ref, o_ref):
    head0 = _mem_heads(tq)
    for lg in range(MEM_WIDTH // LANES):
      cs = slice(lg * LANES, (lg + 1) * LANES)
      q2 = q_ref[:, cs]
      mk = _bf(kv_ref[:, cs])
      mv = _bf(kv_ref[:, MEM_WIDTH + lg * LANES:MEM_WIDTH + (lg + 1) * LANES])
      outs = []
      for j in range(2):
        hm = head0 if j == 0 else jnp.logical_not(head0)
        qj = _bf(jnp.where(hm, q2, 0.0) * ATT_SCALE)
        sc = _dot_nt(qj, mk)
        mx = jnp.max(sc, axis=1, keepdims=True)
        pe = jnp.exp(sc - mx)
        den = jnp.sum(pe, axis=1, keepdims=True)
        outs.append(_dot(_bf(pe / den), mv))
      o_ref[:, cs] = _bf(jnp.where(head0, outs[0], outs[1]))

  return pl.pallas_call(
      body, name=name, grid=(bl, s // tq),
      in_specs=[pl.BlockSpec((None, tq, MEM_WIDTH), lambda b, i: (b, i, qcol)),
                pl.BlockSpec((None, nm, 2 * MEM_WIDTH), lambda b, i: (b, 0, 0))],
      out_specs=pl.BlockSpec((None, tq, MEM_WIDTH), lambda b, i: (b, i, 0)),
      out_shape=jax.ShapeDtypeStruct((bl, s, MEM_WIDTH), BF16),
      compiler_params=_params(("parallel", "parallel")),
  )(h3, mkv3)


def _mem_bwd(h3, qcol, mkv3, dcat3, name, tq=512):
  bl, s, _ = h3.shape
  nm = mkv3.shape[1]
  tq = _tile(s, tq)
  docol = dcat3.shape[2] // MEM_WIDTH - 1

  def body(q_ref, kv_ref, do_ref, dq_ref, dkv_ref):
    i = pl.program_id(1)

    @pl.when(i == 0)
    def _():
      dkv_ref[...] = jnp.zeros_like(dkv_ref)

    head0 = _mem_heads(tq)
    for lg in range(MEM_WIDTH // LANES):
      cs = slice(lg * LANES, (lg + 1) * LANES)
      vs = slice(MEM_WIDTH + lg * LANES, MEM_WIDTH + (lg + 1) * LANES)
      q2 = q_ref[:, cs]
      do2 = do_ref[:, cs]
      mk = _bf(kv_ref[:, cs])
      mv = _bf(kv_ref[:, vs])
      dq2 = jnp.zeros((tq, LANES), F32)
      dmk = jnp.zeros((nm, LANES), F32)
      dmv = jnp.zeros((nm, LANES), F32)
      for j in range(2):
        hm = head0 if j == 0 else jnp.logical_not(head0)
        qj = _bf(jnp.where(hm, q2, 0.0) * ATT_SCALE)
        doj = _bf(jnp.where(hm, do2, 0.0))
        sc = _dot_nt(qj, mk)
        mx = jnp.max(sc, axis=1, keepdims=True)
        pe = jnp.exp(sc - mx)
        pn = pe / jnp.sum(pe, axis=1, keepdims=True)
        pb = _bf(pn)
        dp = _dot_nt(doj, mv)
        dj = jnp.sum(pb.astype(F32) * dp, axis=1, keepdims=True)
        ds = _bf(pn * (dp - dj))
        dq2 = dq2 + jnp.where(hm, _dot(ds, mk), 0.0) * ATT_SCALE
        dmk = dmk + _dot_tn(ds, qj)
        dmv = dmv + _dot_tn(pb, doj)
      dq_ref[:, cs] = _bf(dq2)
      dkv_ref[:, cs] += dmk
      dkv_ref[:, vs] += dmv

  return pl.pallas_call(
      body, name=name, grid=(bl, s // tq),
      in_specs=[pl.BlockSpec((None, tq, MEM_WIDTH), lambda b, i: (b, i, qcol)),
                pl.BlockSpec((None, nm, 2 * MEM_WIDTH), lambda b, i: (b, 0, 0)),
                pl.BlockSpec((None, tq, MEM_WIDTH), lambda b, i: (b, i, docol))],
      out_specs=[pl.BlockSpec((None, tq, MEM_WIDTH), lambda b, i: (b, i, 0)),
                 pl.BlockSpec((None, nm, 2 * MEM_WIDTH), lambda b, i: (b, 0, 0))],
      out_shape=[jax.ShapeDtypeStruct((bl, s, MEM_WIDTH), BF16), jax.ShapeDtypeStruct((bl, nm, 2 * MEM_WIDTH), F32)],
      compiler_params=_params(("parallel", "arbitrary")),
  )(h3, mkv3, dcat3)


def _sgu_consts():
  ti = lax.broadcasted_iota(jnp.int32, (CHUNK, CHUNK), 0)
  si = lax.broadcasted_iota(jnp.int32, (CHUNK, CHUNK), 1)
  return si <= ti, si < 64


def _sgu_bias_lanes(b_s):
  return jnp.repeat(b_s.T, 64, axis=1)


def _sgu_fwd(h2, ln_g, ln_b, w_s, b_s, name, tr=512):
  t, _ = h2.shape
  tr = _tile(t, tr)
  nch = tr // CHUNK
  npair = N_SGU_GROUPS // 2

  def body(u_ref, v_ref, g_ref, b_ref, w_ref, bs_ref, o_ref, vn_sc):
    tril, head0 = _sgu_consts()
    xhat, _ = _ln_stats(_gelu(v_ref[...]))
    vn_sc[...] = _bf(xhat * g_ref[...] + b_ref[...])
    for jp in range(npair):
      cs = slice(jp * LANES, (jp + 1) * LANES)
      w0 = _bf(jnp.where(tril, w_ref[2 * jp], 0.0))
      w1 = _bf(jnp.where(tril, w_ref[2 * jp + 1], 0.0))
      bias = bs_ref[:, cs]
      for c in range(nch):
        rs = slice(c * CHUNK, (c + 1) * CHUNK)
        vb = vn_sc[rs, cs]
        mixed = jnp.where(head0, _dot(w0, vb), _dot(w1, vb)) + bias
        o_ref[rs, cs] = _bf(_gelu(u_ref[rs, cs]) * mixed)

  blk = lambda j: pl.BlockSpec((tr, SGU_WIDTH), lambda i: (i, j))
  vec = pl.BlockSpec((1, SGU_WIDTH), lambda i: (0, 0))
  return pl.pallas_call(
      body, name=name, grid=(t // tr,),
      in_specs=[blk(0), blk(1), vec, vec,
                pl.BlockSpec((N_SGU_GROUPS, CHUNK, CHUNK), lambda i: (0, 0, 0)),
                pl.BlockSpec((CHUNK, SGU_WIDTH), lambda i: (0, 0))],
      out_specs=blk(0), out_shape=jax.ShapeDtypeStruct((t, SGU_WIDTH), BF16),
      scratch_shapes=[pltpu.VMEM((tr, SGU_WIDTH), BF16)],
      compiler_params=_params(("parallel",)),
  )(h2, h2, ln_g.reshape(1, -1), ln_b.reshape(1, -1), w_s, _sgu_bias_lanes(b_s))


def _sgu_bwd(h2, dcat, ln_g, ln_b, w_s, b_s, name, tr=512):
  t, _ = h2.shape
  tr = _tile(t, tr)
  nch = tr // CHUNK
  npair = N_SGU_GROUPS // 2
  nsteps = t // tr

  def body(u_ref, v_ref, dm_ref, g_ref, b_ref, w_ref, bs_ref,
           du_ref, dv_ref, dw_ref, dbs_ref, dg_ref, db_ref, vn_sc, dmx_sc, dvn_sc, mix_sc, dbx_sc):
    i = pl.program_id(0)
    tril, head0 = _sgu_consts()

    @pl.when(i == 0)
    def _():
      dw_ref[...] = jnp.zeros_like(dw_ref)
      dg_ref[...] = jnp.zeros_like(dg_ref)
      db_ref[...] = jnp.zeros_like(db_ref)
      dbx_sc[...] = jnp.zeros_like(dbx_sc)

    gv, gv_der = _gelu_parts(v_ref[...])
    xhat, rstd = _ln_stats(gv)
    g = g_ref[...]
    vn_sc[...] = _bf(xhat * g + b_ref[...])
    gu, gu_der = _gelu_parts(u_ref[...])
    dmix = dm_ref[...]
    dmx_sc[...] = dmix * gu

    for jp in range(npair):
      cs = slice(jp * LANES, (jp + 1) * LANES)
      w0 = _bf(jnp.where(tril, w_ref[2 * jp], 0.0))
      w1 = _bf(jnp.where(tril, w_ref[2 * jp + 1], 0.0))
      bias = bs_ref[:, cs]
      dw0 = jnp.zeros((CHUNK, CHUNK), F32)
      dw1 = jnp.zeros((CHUNK, CHUNK), F32)
      dbx = jnp.zeros((CHUNK, LANES), F32)
      for c in range(nch):
        rs = slice(c * CHUNK, (c + 1) * CHUNK)
        vb = vn_sc[rs, cs]
        mix_sc[rs, cs] = jnp.where(head0, _dot(w0, vb), _dot(w1, vb)) + bias
        dmx = dmx_sc[rs, cs]
        d0 = _bf(jnp.where(head0, dmx, 0.0))
        d1 = _bf(jnp.where(head0, 0.0, dmx))
        dvn_sc[rs, cs] = _dot_tn(w0, d0) + _dot_tn(w1, d1)
        dw0 = dw0 + _dot_nt(d0, vb)
        dw1 = dw1 + _dot_nt(d1, vb)
        dbx = dbx + dmx
      dw_ref[2 * jp] += dw0
      dw_ref[2 * jp + 1] += dw1
      dbx_sc[:, cs] += dbx

    du_ref[...] = _bf(dmix * mix_sc[...] * gu_der)
    dvn = dvn_sc[...]
    dv_ref[...] = _bf(_ln_bwd(dvn, xhat, rstd, g) * gv_der)
    dg_ref[...] += jnp.sum(dvn * xhat, axis=0, keepdims=True)
    db_ref[...] += jnp.sum(dvn, axis=0, keepdims=True)

    @pl.when(i == nsteps - 1)
    def _():
      lane = lax.broadcasted_iota(jnp.int32, (CHUNK, LANES), 1)
      acc = jnp.zeros((CHUNK, LANES), F32)
      for gi in range(N_SGU_GROUPS):
        jp, j = gi // 2, gi % 2
        part = dbx_sc[:, jp * LANES:(jp + 1) * LANES]
        hm = (lane < 64) if j == 0 else (lane >= 64)
        colsum = jnp.sum(jnp.where(hm, part, 0.0), axis=1, keepdims=True)
        acc = jnp.where(lane == gi, colsum, acc)
        dw_ref[gi] = jnp.where(tril, dw_ref[gi], 0.0)
      dbs_ref[...] = acc

  blk = lambda j: pl.BlockSpec((tr, SGU_WIDTH), lambda i: (i, j))
  vec = pl.BlockSpec((1, SGU_WIDTH), lambda i: (0, 0))
  wspec = pl.BlockSpec((N_SGU_GROUPS, CHUNK, CHUNK), lambda i: (0, 0, 0))
  big = lambda dt: pltpu.VMEM((tr, SGU_WIDTH), dt)
  du, dv, dw, dbs, dg, db = pl.pallas_call(
      body, name=name, grid=(nsteps,),
      in_specs=[blk(0), blk(1), blk(0), vec, vec, wspec, pl.BlockSpec((CHUNK, SGU_WIDTH), lambda i: (0, 0))],
      out_specs=[blk(0), blk(0), wspec, pl.BlockSpec((CHUNK, LANES), lambda i: (0, 0)), vec, vec],
      out_shape=[jax.ShapeDtypeStruct((t, SGU_WIDTH), BF16), jax.ShapeDtypeStruct((t, SGU_WIDTH), BF16),
                 jax.ShapeDtypeStruct((N_SGU_GROUPS, CHUNK, CHUNK), F32), jax.ShapeDtypeStruct((CHUNK, LANES), F32),
                 jax.ShapeDtypeStruct((1, SGU_WIDTH), F32), jax.ShapeDtypeStruct((1, SGU_WIDTH), F32)],
      scratch_shapes=[big(BF16), big(F32), big(F32), big(F32), pltpu.VMEM((CHUNK, SGU_WIDTH), F32)],
      compiler_params=_params(("arbitrary",)),
  )(h2, h2, dcat, ln_g.reshape(1, -1), ln_b.reshape(1, -1), w_s, _sgu_bias_lanes(b_s))
  return du, dv, dw, dbs[:, :N_SGU_GROUPS].T, dg[0], db[0]


def _loss_head(xo, tgt, z, g, name, tm=512):
  m, d = xo.shape
  tm = _tile(m, tm)

  def body(x_ref, t_ref, z_ref, g_ref, l_ref, dz_ref, dzb_ref, dg_ref, dbias_ref):
    @pl.when(pl.program_id(0) == 0)
    def _():
      l_ref[...] = jnp.zeros_like(l_ref)

    diff = x_ref[...] - t_ref[...]
    rowsum = jnp.sum(diff * diff, axis=1, keepdims=True)
    tot = jnp.sum(rowsum, axis=0, keepdims=True) * (0.5 / d)
    l_ref[...] += jnp.broadcast_to(tot, l_ref.shape)
    _ln_bwd_tail(diff * (1.0 / d), z_ref, g_ref, dz_ref, dzb_ref, dg_ref, dbias_ref)

  row = pl.BlockSpec((tm, d), lambda i: (i, 0))
  vec = pl.BlockSpec((1, d), lambda i: (0, 0))
  out_specs, out_shape = _ln_bwd_outs(m, d, row, vec)
  l, dz, dzb, dg, dbias = pl.pallas_call(
      body, name=name, grid=(m // tm,), in_specs=[row, row, row, vec],
      out_specs=[pl.BlockSpec((8, LANES), lambda i: (0, 0))] + out_specs,
      out_shape=[jax.ShapeDtypeStruct((8, LANES), F32)] + out_shape,
      compiler_params=_params(("arbitrary",)),
  )(xo, tgt, z, g.reshape(1, d))
  return l[0, 0], dz, dzb, dg[0], dbias[0]


def _local_step(x3, mem3, tgt3, w, late_weights=None, early_exchange=None):
  w = dict(w)
  bl, s, d = x3.shape
  t = bl * s
  nm = mem3.shape[1]
  mem2 = mem3.reshape(bl * nm, d)
  x = x3.reshape(t, d)
  xb = x
  saved = []
  for i in range(DEPTH):
    j = i // 2
    attn = i % 2 == 0
    mkv = _mm(mem2, w["w_mem_kv"][i], "nn", F32, f"mkv_fwd_{i}", tm=1024, tn=512, tk=1024)
    mkv3 = mkv.reshape(bl, nm, 2 * MEM_WIDTH)
    w_in = w["a_w_in"][j] if attn else w["b_w_in"][j]
    h = _mm(xb, w_in, "nt", F32, f"in_proj_{i}", tm=512, tn=w_in.shape[0], tk=d)
    h3 = h.reshape(bl, s, -1)
    if attn and late_weights is not None and i in late_weights:
      mix3, lse3, gathered = _attn_fwd(h3, f"dil_attn_fwd_{i}", gather=late_weights[i].flat)
      for n, layers in late_weights[i].unpack(gathered).items():
        w[n] = {**w.get(n, {}), **layers}
    elif attn:
      mix3, lse3 = _attn_fwd(h3, f"dil_attn_fwd_{i}")
    if attn:
      mix = mix3.reshape(t, DIL_WIDTH)
      qcol = 3 * DIL_WIDTH // MEM_WIDTH
    else:
      mix = _sgu_fwd(h, w["sgu_ln_g"][j], w["sgu_ln_b"][j], w["sgu_w_s"][j], w["sgu_b_s"][j], f"sgu_fwd_{i}")
      lse3 = None
      qcol = 2 * SGU_WIDTH // MEM_WIDTH
    mo = _mem_fwd(h3, qcol, mkv3, f"mem_attn_fwd_{i}").reshape(t, MEM_WIDTH)
    cat = jnp.concatenate([mix, mo], axis=1)
    z1, xm, xmb = _mm_res_ln(cat, w["w_out"][i], x, w["ln_mix_g"][i], w["ln_mix_b"][i], f"out_proj_ln_{i}", tk=1024)
    ga, gb, hm = _ffn_up(xmb, w["w_gate"][i], w["w_up"][i], f"ffn_up_{i}")
    z2, xo, xob = _mm_res_ln(hm, w["w_down"][i], xm, w["ln_ffn_g"][i], w["ln_ffn_b"][i], f"ffn_down_ln_{i}", tk=hm.shape[1])
    saved.append(dict(xb=xb, h=h, h3=h3, mkv3=mkv3, mix3=(mix3 if attn else None), lse3=lse3, cat=cat, z1=z1,
                      xmb=xmb, ga=ga, gb=gb, hm=hm, z2=z2, qcol=qcol))
    x, xb = xo, xob

  names = ("a_w_in", "b_w_in", "sgu_ln_g", "sgu_ln_b", "sgu_w_s", "sgu_b_s", "w_mem_kv", "w_out",
           "ln_mix_g", "ln_mix_b", "w_gate", "w_up", "w_down", "ln_ffn_g", "ln_ffn_b")
  grads = {n: [None] * len(w[n]) for n in names}
  last = DEPTH - 1
  loss, dz2, dz2b, grads["ln_ffn_g"][last], grads["ln_ffn_b"][last] = _loss_head(
      x, tgt3.reshape(t, d), saved[last]["z2"], w["ln_ffn_g"][last], "loss_head")
  dx = None
  for i in reversed(range(DEPTH)):
    j = i // 2
    attn = i % 2 == 0
    sv = saved[i]
    da, db = _ffn_bwd_hidden(dz2b, w["w_down"][i], sv["ga"], sv["gb"], f"ffn_bwd_hidden_{i}")
    grads["w_down"][i] = _mm(sv["hm"], dz2b, "tn", F32, f"dw_down_{i}", tm=1408, tn=1024, tk=1024)
    grads["w_gate"][i] = _mm(da, sv["xmb"], "tn", F32, f"dw_gate_{i}", tm=1408, tn=1024, tk=1024)
    grads["w_up"][i] = _mm(db, sv["xmb"], "tn", F32, f"dw_up_{i}", tm=1408, tn=1024, tk=1024)
    dz1, dz1b, grads["ln_mix_g"][i], grads["ln_mix_b"][i] = _ffn_bwd_input_ln(
        da, db, w["w_gate"][i], w["w_up"][i], dz2, sv["z1"], w["ln_mix_g"][i], f"ffn_bwd_input_ln_{i}")
    grads["w_out"][i] = _mm(sv["cat"], dz1b, "tn", F32, f"dw_out_{i}", tm=1024, tn=1024, tk=1024)
    dcat = _mm(dz1b, w["w_out"][i], "nt", F32, f"out_proj_bwd_{i}", tm=1024, tn=1024, tk=1024)
    dcat3 = dcat.reshape(bl, s, -1)
    dqm3, dmkv3 = _mem_bwd(sv["h3"], sv["qcol"], sv["mkv3"], dcat3, f"mem_attn_bwd_{i}")
    grads["w_mem_kv"][i] = _mm(mem2, dmkv3.reshape(bl * nm, 2 * MEM_WIDTH), "tn", F32, f"dw_mem_kv_{i}", tm=1024, tn=512, tk=1024)
    dqm = dqm3.reshape(t, MEM_WIDTH)
    if attn and i == 0 and early_exchange is not None:
      q, (pack, state) = early_exchange(grads)
      dq3, dk3, dv3, x3 = _attn_bwd(sv["h3"], sv["mix3"], sv["lse3"], dcat3, f"dil_attn_bwd_{i}", exchange=q)
      grads["early_exchange"] = (pack, state, x3)
      parts = [dq3.reshape(t, -1), dk3.reshape(t, -1), dv3.reshape(t, -1), dqm]
    elif attn:
      dq3, dk3, dv3 = _attn_bwd(sv["h3"], sv["mix3"], sv["lse3"], dcat3, f"dil_attn_bwd_{i}")
      parts = [dq3.reshape(t, -1), dk3.reshape(t, -1), dv3.reshape(t, -1), dqm]
    else:
      du, dv, dws, dbs, dlg, dlb = _sgu_bwd(sv["h"], dcat, w["sgu_ln_g"][j], w["sgu_ln_b"][j], w["sgu_w_s"][j],
                                             w["sgu_b_s"][j], f"sgu_bwd_{i}")
      grads["sgu_w_s"][j], grads["sgu_b_s"][j], grads["sgu_ln_g"][j], grads["sgu_ln_b"][j] = dws, dbs, dlg, dlb
      parts = [du, dv, dqm]
    dh = jnp.concatenate(parts, axis=1)
    w_in = w["a_w_in"][j] if attn else w["b_w_in"][j]
    grads["a_w_in" if attn else "b_w_in"][j] = _mm(dh, sv["xb"], "tn", F32, f"dw_in_{i}", tm=1280 if attn else 896, tn=1024, tk=1024)
    if i > 0:
      dz2, dz2b, grads["ln_ffn_g"][i - 1], grads["ln_ffn_b"][i - 1] = _in_proj_bwd_ln(
          dh, w_in, dz1, saved[i - 1]["z2"], w["ln_ffn_g"][i - 1], f"in_proj_bwd_ln_{i}")
    else:
      dx = _mm(dh, w_in, "nn", F32, f"in_proj_bwd_{i}", add=dz1, add_scale=DN_ALPHA, tm=512, tn=d, tk=w_in.shape[0])
  return loss, dx.reshape(bl, s, d), grads


def _my_place():
  return lax.axis_index("x"), lax.axis_index("y"), lax.axis_index("c")


def _other_chips(x, y):
  return [(1 - x, y), (x, 1 - y), (1 - x, 1 - y)]


ANY = pl.BlockSpec(memory_space=pl.ANY)


def _all_gather_halves(wl, name):
  _, r, c_ = wl.shape

  def body(w_ref, g_ref, send_sems, recv_sems):
    x, y, c = _my_place()
    me = 2 * x + y
    sibling = (x, y, 1 - c)
    chips = _other_chips(x, y)

    def copy(k, src, dst, to):
      return pltpu.make_async_remote_copy(src_ref=src, dst_ref=dst, send_sem=send_sems.at[k], recv_sem=recv_sems.at[k],
                                          device_id=to, device_id_type=MESH_ID)

    first = [copy(k, w_ref.at[c], g_ref.at[me, c], (px, py, c)) for k, (px, py) in enumerate(chips)]
    for cp in first:
      cp.start()
    passed = []
    for k, (px, py) in enumerate(chips):
      landed = g_ref.at[2 * px + py, c]
      copy(k, landed, landed, (px, py, c)).wait_recv()
      fwd = copy(3 + k, landed, landed, sibling)
      fwd.start()
      passed.append(fwd)
    for k, (px, py) in enumerate(chips):
      theirs = g_ref.at[2 * px + py, 1 - c]
      copy(3 + k, theirs, theirs, sibling).wait_recv()
    for cp in first + passed:
      cp.wait_send()

  got = pl.pallas_call(
      body, name=name, in_specs=[ANY], out_specs=ANY,
      out_shape=jax.ShapeDtypeStruct((4, 2, r, c_), wl.dtype),
      scratch_shapes=[pltpu.SemaphoreType.DMA((6,)), pltpu.SemaphoreType.DMA((6,))],
  )(wl)
  chip = 2 * lax.axis_index("x") + lax.axis_index("y")
  return lax.dynamic_update_slice(got, wl[None], (chip, 0, 0, 0))


def _relayed_gather_phase(phase, w_ref, g_ref, send_sems, recv_sems):
  h = w_ref.shape[1] // 2
  x, y, c = _my_place()
  sibling = (x, y, 1 - c)
  xn, yn, dg = _other_chips(x, y)

  def copy(k, src, dst, to):
    return pltpu.make_async_remote_copy(src_ref=src, dst_ref=dst, send_sem=send_sems.at[k], recv_sem=recv_sems.at[k],
                                        device_id=to, device_id_type=MESH_ID)

  def block(chip, half):
    return g_ref.at[2 * chip[0] + chip[1], half]

  def same(k, ref, to):
    return copy(k, ref, ref, to)

  top, bottom = pl.ds(0, h), pl.ds(h, h)
  sends = [copy(0, w_ref.at[c], block((x, y), c), (*xn, c)), copy(1, w_ref.at[c], block((x, y), c), (*yn, c)),
           same(2, block(xn, c).at[top], (*yn, c)), same(3, block(yn, c).at[bottom], (*xn, c)),
           same(4, block(xn, c), sibling), same(5, block(yn, c), sibling), same(6, block(dg, c), sibling)]
  if phase == 0:
    sends[0].start()
    sends[1].start()
  elif phase == 1:
    same(0, block(xn, c), (*xn, c)).wait_recv()
    sends[2].start()
    sends[4].start()
    same(1, block(yn, c), (*yn, c)).wait_recv()
    sends[3].start()
    sends[5].start()
  else:
    same(2, block(dg, c).at[top], (*yn, c)).wait_recv()
    same(3, block(dg, c).at[bottom], (*xn, c)).wait_recv()
    sends[6].start()
    for k, chip in ((4, xn), (5, yn), (6, dg)):
      same(k, block(chip, 1 - c), sibling).wait_recv()
    for cp in sends:
      cp.wait_send()


N_RELAY_COPIES = 7


def _place_own_block(got, wl):
  chip = 2 * lax.axis_index("x") + lax.axis_index("y")
  return lax.dynamic_update_slice(got, wl[None], (chip, 0, 0, 0))


def _all_gather_relayed(wl, name):
  _, r, c_ = wl.shape
  assert (r // 2) % ROW_ALIGN == 0

  def body(w_ref, g_ref, send_sems, recv_sems):
    for phase in range(3):
      _relayed_gather_phase(phase, w_ref, g_ref, send_sems, recv_sems)

  got = pl.pallas_call(
      body, name=name, in_specs=[ANY], out_specs=ANY,
      out_shape=jax.ShapeDtypeStruct((4, 2, r, c_), wl.dtype),
      scratch_shapes=[pltpu.SemaphoreType.DMA((N_RELAY_COPIES,)), pltpu.SemaphoreType.DMA((N_RELAY_COPIES,))],
  )(wl)
  return _place_own_block(got, wl)


def _sibling_swap(v, name):
  def body(v_ref, o_ref, send_sem, recv_sem):
    x, y, c = _my_place()
    cp = pltpu.make_async_remote_copy(src_ref=v_ref, dst_ref=o_ref, send_sem=send_sem, recv_sem=recv_sem,
                                      device_id=(x, y, 1 - c), device_id_type=MESH_ID)
    cp.start()
    cp.wait()

  return pl.pallas_call(
      body, name=name, in_specs=[ANY], out_specs=ANY, out_shape=jax.ShapeDtypeStruct(v.shape, v.dtype),
      scratch_shapes=[pltpu.SemaphoreType.DMA, pltpu.SemaphoreType.DMA],
  )(v)


def _chip_exchange_copies(q_ref, o_ref, send_sems, recv_sems):
  x, y, c = _my_place()
  return [pltpu.make_async_remote_copy(src_ref=q_ref.at[2 * px + py], dst_ref=o_ref.at[k], send_sem=send_sems.at[k],
                                       recv_sem=recv_sems.at[k], device_id=(px, py, c), device_id_type=MESH_ID)
          for k, (px, py) in enumerate(_other_chips(x, y))]


def _chip_exchange(q, name):
  _, r, c_ = q.shape

  def body(q_ref, o_ref, send_sems, recv_sems):
    cps = _chip_exchange_copies(q_ref, o_ref, send_sems, recv_sems)
    for cp in cps:
      cp.start()
    for cp in cps:
      cp.wait()

  return pl.pallas_call(
      body, name=name, in_specs=[ANY], out_specs=ANY, out_shape=jax.ShapeDtypeStruct((3, r, c_), q.dtype),
      scratch_shapes=[pltpu.SemaphoreType.DMA((3,)), pltpu.SemaphoreType.DMA((3,))],
  )(q)


def _share_halves(v, name):
  theirs = _sibling_swap(v, name)
  c = lax.axis_index("c")
  return jnp.where(c == 0, jnp.concatenate([v, theirs]), jnp.concatenate([theirs, v]))


def _half_spec(tr, c_, pick):
  return pl.BlockSpec((None, None, tr, c_), lambda s, r, place: (s, pick(place), r, 0))


def _cast_other_half(p, place, name, tr=512):
  _, _, r, c_ = p.shape
  tr = _tile(r, tr, 16)

  def body(place_ref, p_ref, o_ref):
    o_ref[...] = _bf(p_ref[...])

  out_spec = pl.BlockSpec((None, tr, c_), lambda s, rr, place: (s, rr, 0))
  return pl.pallas_call(
      body, name=name, out_shape=jax.ShapeDtypeStruct((4, r, c_), BF16),
      grid_spec=pltpu.PrefetchScalarGridSpec(num_scalar_prefetch=1, grid=(4, r // tr),
                                             in_specs=[_half_spec(tr, c_, lambda place: 1 - place[1])], out_specs=out_spec),
      compiler_params=_params(("parallel", "parallel")),
  )(place, p)


def _add_sibling(p, x1, place, name, tr=512):
  _, _, r, c_ = p.shape
  tr = _tile(r, tr, 16)

  def body(place_ref, p_ref, x_ref, o_ref):
    o_ref[...] = _bf(p_ref[...] + x_ref[...].astype(F32))

  row = pl.BlockSpec((None, tr, c_), lambda s, rr, place: (s, rr, 0))
  return pl.pallas_call(
      body, name=name, out_shape=jax.ShapeDtypeStruct((4, r, c_), BF16),
      grid_spec=pltpu.PrefetchScalarGridSpec(num_scalar_prefetch=1, grid=(4, r // tr),
                                             in_specs=[_half_spec(tr, c_, lambda place: place[1]), row], out_specs=row),
      compiler_params=_params(("parallel", "parallel")),
  )(place, p, x1)


def _sum_own(p, x1, x3, place, name, tr=512):
  _, _, r, c_ = p.shape
  tr = _tile(r, tr, 16)

  def body(place_ref, p_ref, x1_ref, x3_ref, o_ref):
    acc = p_ref[...] + x1_ref[...].astype(F32)
    for k in range(3):
      acc = acc + x3_ref[k].astype(F32)
    o_ref[...] = acc

  return pl.pallas_call(
      body, name=name, out_shape=jax.ShapeDtypeStruct((r, c_), F32),
      grid_spec=pltpu.PrefetchScalarGridSpec(
          num_scalar_prefetch=1, grid=(r // tr,),
          in_specs=[pl.BlockSpec((None, None, tr, c_), lambda rr, place: (place[0], place[1], rr, 0)),
                    pl.BlockSpec((None, tr, c_), lambda rr, place: (place[0], rr, 0)),
                    pl.BlockSpec((3, tr, c_), lambda rr, place: (0, rr, 0))],
          out_specs=pl.BlockSpec((tr, c_), lambda rr, place: (rr, 0))),
      compiler_params=_params(("parallel",)),
  )(place, p, x1, x3)


def _reduce_scatter_begin(p, tag):
  x, y, c = _my_place()
  place = jnp.stack([2 * x + y, c]).astype(jnp.int32)
  x1 = _sibling_swap(_cast_other_half(p, place, f"rs_cast_other_half_{tag}"), f"rs_sibling_swap_{tag}")
  return _add_sibling(p, x1, place, f"rs_add_sibling_{tag}"), (p, x1, place)


def _reduce_scatter_end(state, x3, tag):
  p, x1, place = state
  return _share_halves(_sum_own(p, x1, x3, place, f"rs_sum_own_{tag}"), f"rs_share_halves_{tag}")


def _adamw(w, g, m, v, name):
  shape = w.shape
  cols = shape[-1]
  rows = w.size // cols
  tr = _tile(rows, max(8, (256 * 1024) // cols // 8 * 8), 8)

  def body(w_ref, g_ref, m_ref, v_ref, d_ref, nm_ref, nv_ref):
    gv = g_ref[...]
    nm = ADAM_B1 * m_ref[...] + (1.0 - ADAM_B1) * gv
    nv = ADAM_B2 * v_ref[...] + (1.0 - ADAM_B2) * (gv * gv)
    m_hat = nm / (1.0 - ADAM_B1 ** ADAM_STEP)
    v_hat = nv / (1.0 - ADAM_B2 ** ADAM_STEP)
    d_ref[...] = -ADAM_LR * (m_hat / (jnp.sqrt(v_hat) + ADAM_EPS) + ADAM_WD * w_ref[...])
    nm_ref[...] = nm
    nv_ref[...] = nv

  spec = pl.BlockSpec((tr, cols), lambda i: (i, 0))
  sds = jax.ShapeDtypeStruct((rows, cols), F32)
  outs = pl.pallas_call(
      body, name=name, grid=(rows // tr,), in_specs=[spec] * 4, out_specs=[spec] * 3, out_shape=[sds] * 3,
      compiler_params=_params(("parallel",)),
  )(*(t.reshape(rows, cols) for t in (w, g, m, v)))
  return tuple(o.reshape(shape) for o in outs)


SHARDED = (("a_w_in", True), ("b_w_in", True), ("w_mem_kv", False), ("w_out", False), ("w_gate", True),
           ("w_up", True), ("w_down", False))
SMALL_SHARDED = (("sgu_ln_g", 1), ("sgu_ln_b", 1))
REPLICATED = ("sgu_w_s", "sgu_b_s", "ln_mix_g", "ln_mix_b", "ln_ffn_g", "ln_ffn_b")
SMALL_ORDER = ("sgu_w_s", "sgu_b_s", "ln_mix_g", "ln_mix_b", "ln_ffn_g", "ln_ffn_b", "sgu_ln_g", "sgu_ln_b")
ROW_ALIGN = 16


def _pad_to(v, n):
  return jnp.pad(v, (0, n - v.shape[0]))


def _round_up(n, a):
  return -(-n // a) * a


def _exchange_form(t, transposed):
  return jnp.swapaxes(t, 1, 2) if transposed else t


def _to_shard_major(full, axis):
  shp = full.shape
  cut = shp[:axis] + (4, shp[axis] // 4) + shp[axis + 1:]
  return jnp.moveaxis(full.reshape(cut), axis, 0).reshape(4, -1, FLAT_COLS)


def _from_shard_major(rows, shard_shape, axis):
  full = jnp.moveaxis(rows.reshape((4,) + tuple(shard_shape)), 0, axis)
  shp = full.shape
  return full.reshape(shp[:axis] + (shp[axis] * shp[axis + 1],) + shp[axis + 2:])


class _WeightPack:
  def __init__(self, items, small=()):
    segs, self.rows, self.small, off = [], {}, [], 0
    for n, l, b in items:
      seg = b.reshape(-1, FLAT_COLS)
      self.rows[(n, l)] = (off, seg.shape[0], b.shape)
      segs.append(seg)
      off += seg.shape[0]
    if small:
      flat = jnp.concatenate([lax.bitcast_convert_type(v, BF16).reshape(-1) for _, v in small])
      rows = _round_up(flat.shape[0], ROW_ALIGN * FLAT_COLS) // FLAT_COLS
      self.small = [(n, v.shape) for n, v in small]
      self.small_rows = (off, rows)
      segs.append(_pad_to(flat, rows * FLAT_COLS).reshape(rows, FLAT_COLS))
      off += rows
    rows_pad = _round_up(off, 4 * ROW_ALIGN)
    if rows_pad > off:
      segs.append(jnp.zeros((rows_pad - off, FLAT_COLS), BF16))
    self.flat = jnp.concatenate(segs).reshape(2, rows_pad // 2, FLAT_COLS)

  def unpack(self, gathered):
    g = gathered.reshape(4, -1, FLAT_COLS)
    out = {}
    for (n, l), (off, nr, shape) in self.rows.items():
      out.setdefault(n, {})[l] = g[:, off:off + nr].reshape((4 * shape[0],) + shape[1:])
    if self.small:
      off, rows = self.small_rows
      flat = g[:, off:off + rows].reshape(4, rows * FLAT_COLS)
      pos = 0
      for n, shape in self.small:
        sz = 2 * math.prod(shape)
        vals = lax.bitcast_convert_type(flat[:, pos:pos + sz].reshape((4,) + shape + (2,)), F32)
        out[n] = _from_shard_major(vals, shape, len(shape) - 1)
        pos += sz
    return out


FIRST_WEIGHTS = (("a_w_in", 0), ("w_mem_kv", 0))


def _weight_packs(shards):
  blocks = {(n, l): _exchange_form(shards[n], tr)[l].astype(BF16)
            for n, tr in SHARDED for l in range(shards[n].shape[0])}
  first = _WeightPack([(n, l, blocks[(n, l)]) for n, l in FIRST_WEIGHTS],
                      small=[(n, shards[n]) for n, _ in SMALL_SHARDED])

  def model_layer(n, l):
    return {"a_w_in": 2 * l, "b_w_in": 2 * l + 1}.get(n, l)

  rest = [(n, l, b) for (n, l), b in blocks.items() if (n, l) not in FIRST_WEIGHTS]
  last = DEPTH - 1
  return first, {0: _WeightPack([it for it in rest if model_layer(it[0], it[1]) < last]),
                 2: _WeightPack([it for it in rest if model_layer(it[0], it[1]) == last])}


def _reduce_grads(grads, shard_shapes):
  early, state, x3 = grads.pop("early_exchange")
  mine_early = _reduce_scatter_end(state, x3, "early")
  late = _GradPack([(n, l, g) for n, _ in SHARDED for l, g in enumerate(grads[n]) if (n, l) not in early.rows])
  q, state = _reduce_scatter_begin(late.p, "late")
  mine_late = _reduce_scatter_end(state, _chip_exchange(q, "rs_chip_exchange_late"), "late")
  out = {}
  for n, tr in SHARDED:
    layers, rows, cols = shard_shapes[n]
    blocks = []
    for l in range(layers):
      pack, mine = (early, mine_early) if (n, l) in early.rows else (late, mine_late)
      off, nr = pack.rows[(n, l)]
      block = mine[off:off + nr]
      blocks.append(block.reshape(cols, rows).T if tr else block.reshape(rows, cols))
    out[n] = jnp.stack(blocks)
  off, quarter_rows = early.rows["small"]
  piece = mine_early[off:off + quarter_rows].reshape(2, quarter_rows // 2, FLAT_COLS)
  small_sum = _all_gather_halves(piece, "gather_small_grads").reshape(-1)
  off = 0
  for n in SMALL_ORDER:
    shape = (len(grads[n]),) + grads[n][0].shape
    sz = math.prod(shape)
    out[n] = small_sum[off:off + sz].reshape(shape)
    off += sz
  return out


class _GradPack:
  def __init__(self, items, small=None):
    segs, self.rows, off = [], {}, 0
    for n, l, g in items:
      seg = _to_shard_major(g, 0)
      self.rows[(n, l)] = (off, seg.shape[1])
      segs.append(seg)
      off += seg.shape[1]
    if small is not None:
      flat = jnp.concatenate([jnp.stack(small[n]).reshape(-1) for n in SMALL_ORDER])
      n_small = _round_up(flat.shape[0], 4 * 2 * 8 * FLAT_COLS)
      quarter_rows = n_small // (4 * FLAT_COLS)
      self.rows["small"] = (off, quarter_rows)
      segs.append(_pad_to(flat, n_small).reshape(4, quarter_rows, FLAT_COLS))
      off += quarter_rows
    rows_pad = _round_up(off, 2 * ROW_ALIGN)
    if rows_pad > off:
      segs.append(jnp.zeros((4, rows_pad - off, FLAT_COLS), F32))
    self.p = jnp.concatenate(segs, axis=1).reshape(4, 2, rows_pad // 2, FLAT_COLS)


def _early_exchange_begin(grads):
  items = [(n, l, g) for n, _ in SHARDED for l, g in enumerate(grads[n]) if g is not None]
  pack = _GradPack(items, small={n: grads[n] for n in SMALL_ORDER})
  q, state = _reduce_scatter_begin(pack.p, "early")
  return q, (pack, state)


WEIGHT_NAMES = ("a_w_in", "b_w_in", "sgu_ln_g", "sgu_ln_b", "sgu_w_s", "sgu_b_s", "w_mem_kv", "w_out",
                "ln_mix_g", "ln_mix_b", "w_gate", "w_up", "w_down", "ln_ffn_g", "ln_ffn_b")


def kernel(x, mem, a_w_in, b_w_in, sgu_ln_g, sgu_ln_b, sgu_w_s, sgu_b_s, w_mem_kv, w_out, ln_mix_g, ln_mix_b, w_gate, w_up, w_down, ln_ffn_g, ln_ffn_b, loss_target, m_a_w_in, m_b_w_in, m_sgu_ln_g, m_sgu_ln_b, m_sgu_w_s, m_sgu_b_s, m_w_mem_kv, m_w_out, m_ln_mix_g, m_ln_mix_b, m_w_gate, m_w_up, m_w_down, m_ln_ffn_g, m_ln_ffn_b, v_a_w_in, v_b_w_in, v_sgu_ln_g, v_sgu_ln_b, v_sgu_w_s, v_sgu_b_s, v_w_mem_kv, v_w_out, v_ln_mix_g, v_ln_mix_b, v_w_gate, v_w_up, v_w_down, v_ln_ffn_g, v_ln_ffn_b):
  weights = dict(a_w_in=a_w_in, b_w_in=b_w_in, sgu_ln_g=sgu_ln_g, sgu_ln_b=sgu_ln_b, sgu_w_s=sgu_w_s, sgu_b_s=sgu_b_s,
                 w_mem_kv=w_mem_kv, w_out=w_out, ln_mix_g=ln_mix_g, ln_mix_b=ln_mix_b, w_gate=w_gate, w_up=w_up,
                 w_down=w_down, ln_ffn_g=ln_ffn_g, ln_ffn_b=ln_ffn_b)
  mom1 = dict(a_w_in=m_a_w_in, b_w_in=m_b_w_in, sgu_ln_g=m_sgu_ln_g, sgu_ln_b=m_sgu_ln_b, sgu_w_s=m_sgu_w_s,
              sgu_b_s=m_sgu_b_s, w_mem_kv=m_w_mem_kv, w_out=m_w_out, ln_mix_g=m_ln_mix_g, ln_mix_b=m_ln_mix_b,
              w_gate=m_w_gate, w_up=m_w_up, w_down=m_w_down, ln_ffn_g=m_ln_ffn_g, ln_ffn_b=m_ln_ffn_b)
  mom2 = dict(a_w_in=v_a_w_in, b_w_in=v_b_w_in, sgu_ln_g=v_sgu_ln_g, sgu_ln_b=v_sgu_ln_b, sgu_w_s=v_sgu_w_s,
              sgu_b_s=v_sgu_b_s, w_mem_kv=v_w_mem_kv, w_out=v_w_out, ln_mix_g=v_ln_mix_g, ln_mix_b=v_ln_mix_b,
              w_gate=v_w_gate, w_up=v_w_up, w_down=v_w_down, ln_ffn_g=v_ln_ffn_g, ln_ffn_b=v_ln_ffn_b)

  first, late = _weight_packs(weights)
  full = first.unpack(_all_gather_relayed(first.flat, "gather_first_weights"))
  for n in REPLICATED:
    full[n] = weights[n]
  loss_part, grad_x, grads = _local_step(x, mem, loss_target, full, late_weights=late,
                                         early_exchange=_early_exchange_begin)
  loss = lax.psum(loss_part, MESH_AXES)

  shard_shapes = {n: weights[n].shape for n, _ in SHARDED}
  red = _reduce_grads(grads, shard_shapes)
  chip = 2 * lax.axis_index("x") + lax.axis_index("y")
  for n, axis in SMALL_SHARDED:
    width = weights[n].shape[axis]
    red[n] = lax.dynamic_slice_in_dim(red[n], chip * width, width, axis)

  small_names = SMALL_ORDER
  def pack(d):
    flat = jnp.concatenate([d[n].reshape(-1) for n in small_names])
    return _pad_to(flat, _round_up(flat.shape[0], 8 * FLAT_COLS)).reshape(-1, FLAT_COLS)
  small_out = _adamw(pack(weights), pack(red), pack(mom1), pack(mom2), "adamw_small")
  delta, new_m, new_v = {}, {}, {}
  off = 0
  for n in small_names:
    sz = weights[n].size
    for dst, src in zip((delta, new_m, new_v), small_out):
      dst[n] = src.reshape(-1)[off:off + sz].reshape(weights[n].shape)
    off += sz
  for n, _ in SHARDED:
    delta[n], new_m[n], new_v[n] = _adamw(weights[n], red[n], mom1[n], mom2[n], f"adamw_{n}")

  return (loss, grad_x, *[red[n] for n in WEIGHT_NAMES], *[delta[n] for n in WEIGHT_NAMES],
          *[new_m[n] for n in WEIGHT_NAMES], *[new_v[n] for n in WEIGHT_NAMES])
```

```python
import math

import jax
import jax.numpy as jnp
from jax import lax
from jax.experimental import pallas as pl
from jax.experimental.pallas import tpu as pltpu

F32 = jnp.float32
BF16 = jnp.bfloat16

DEPTH = 4
HEAD_DIM = 64
N_DIL_HEADS = 12
DIL_WIDTH = N_DIL_HEADS * HEAD_DIM
DIL_PATTERNS = ((128, 1), (512, 4), (2048, 16))
BLOCK = 128
N_SGU_GROUPS = 12
SGU_WIDTH = N_SGU_GROUPS * 64
CHUNK = 128
N_MEM_HEADS = 4
MEM_WIDTH = N_MEM_HEADS * HEAD_DIM
DN_ALPHA = (2 * DEPTH) ** 0.25
LN_EPS = 1e-5
ATT_SCALE = HEAD_DIM ** -0.5
ADAM_LR = 0.001
ADAM_B1 = 0.9
ADAM_B2 = 0.999
ADAM_EPS = 1e-08
ADAM_WD = 0.01
ADAM_STEP = 10
NEG_BIG = -1e30

LANES = 128
FLAT_COLS = 1024
VMEM_LIMIT = 56 * 1024 * 1024
MESH_AXES = ("x", "y", "c")
MESH_ID = pl.DeviceIdType.MESH


def _tile(n, pref, align=LANES):
  if n <= pref:
    return n
  t = (pref // align) * align
  while t >= align:
    if n % t == 0:
      return t
    t -= align
  return n


def _params(sem):
  return pltpu.CompilerParams(dimension_semantics=sem, vmem_limit_bytes=VMEM_LIMIT)


def _dot(a, b):
  return jnp.dot(a, b, preferred_element_type=F32)


def _dot_nt(a, b):
  return lax.dot_general(a, b, (((1,), (1,)), ((), ())), preferred_element_type=F32)


def _dot_tn(a, b):
  return lax.dot_general(a, b, (((0,), (0,)), ((), ())), preferred_element_type=F32)


def _bf(v):
  return v.astype(BF16)


def _ln_stats(z):
  mu = jnp.mean(z, axis=-1, keepdims=True)
  zc = z - mu
  var = jnp.mean(zc * zc, axis=-1, keepdims=True)
  rstd = lax.rsqrt(var + LN_EPS)
  return zc * rstd, rstd


def _ln_bwd(dy, xhat, rstd, g):
  gdy = dy * g
  m1 = jnp.mean(gdy, axis=-1, keepdims=True)
  m2 = jnp.mean(gdy * xhat, axis=-1, keepdims=True)
  return rstd * (gdy - m1 - xhat * m2)


_GELU_C = math.sqrt(2.0 / math.pi)


def _gelu_parts(v):
  v2 = v * v
  t = jnp.tanh(_GELU_C * (v + 0.044715 * v * v2))
  val = 0.5 * v * (1.0 + t)
  der = 0.5 * (1.0 + t) + 0.5 * v * (1.0 - t * t) * (_GELU_C * (1.0 + 3.0 * 0.044715 * v2))
  return val, der


def _gelu(v):
  t = jnp.tanh(_GELU_C * (v + 0.044715 * v * v * v))
  return 0.5 * v * (1.0 + t)


def _sigmoid(v):
  return 1.0 / (1.0 + jnp.exp(-v))


def _mm(a, b, mode, out_dtype, name, add=None, add_scale=1.0, tm=512, tn=512, tk=512):
  if mode == "nn":
    (m, k), (k2, n) = a.shape, b.shape
  elif mode == "nt":
    (m, k), (n, k2) = a.shape, b.shape
  else:
    (k, m), (k2, n) = a.shape, b.shape
  assert k == k2, (a.shape, b.shape, mode)
  tm, tn, tk = _tile(m, tm), _tile(n, tn), _tile(k, tk)
  nk = k // tk
  if mode == "nn":
    a_spec = pl.BlockSpec((tm, tk), lambda i, j, kk: (i, kk))
    b_spec = pl.BlockSpec((tk, tn), lambda i, j, kk: (kk, j))
    dot = _dot
  elif mode == "nt":
    a_spec = pl.BlockSpec((tm, tk), lambda i, j, kk: (i, kk))
    b_spec = pl.BlockSpec((tn, tk), lambda i, j, kk: (j, kk))
    dot = _dot_nt
  else:
    a_spec = pl.BlockSpec((tk, tm), lambda i, j, kk: (kk, i))
    b_spec = pl.BlockSpec((tk, tn), lambda i, j, kk: (kk, j))
    dot = _dot_tn
  o_spec = pl.BlockSpec((tm, tn), lambda i, j, kk: (i, j))
  has_add = add is not None

  def body(*refs):
    if has_add:
      a_ref, b_ref, add_ref, o_ref, acc_ref = refs
    else:
      a_ref, b_ref, o_ref, acc_ref = refs
    kk = pl.program_id(2)

    @pl.when(kk == 0)
    def _():
      acc_ref[...] = jnp.zeros_like(acc_ref)

    acc_ref[...] += dot(_bf(a_ref[...]), _bf(b_ref[...]))

    @pl.when(kk == nk - 1)
    def _():
      r = acc_ref[...]
      if has_add:
        r = r + add_scale * add_ref[...].astype(F32)
      o_ref[...] = r.astype(out_dtype)

  in_specs = [a_spec, b_spec] + ([o_spec] if has_add else [])
  args = (a, b) + ((add,) if has_add else ())
  return pl.pallas_call(
      body, name=name, grid=(m // tm, n // tn, nk), in_specs=in_specs, out_specs=o_spec,
      out_shape=jax.ShapeDtypeStruct((m, n), out_dtype),
      scratch_shapes=[pltpu.VMEM((tm, tn), F32)],
      compiler_params=_params(("parallel", "parallel", "arbitrary")),
  )(*args)


def _mm_res_ln(a, w, res, g, b, name, tm=512, tk=512):
  m, k = a.shape
  d = w.shape[1]
  tm, tk = _tile(m, tm), _tile(k, tk)
  nk = k // tk

  def body(a_ref, w_ref, r_ref, g_ref, b_ref, z_ref, x_ref, xb_ref, acc_ref):
    kk = pl.program_id(1)

    @pl.when(kk == 0)
    def _():
      acc_ref[...] = jnp.zeros_like(acc_ref)

    acc_ref[...] += _dot(_bf(a_ref[...]), _bf(w_ref[...]))

    @pl.when(kk == nk - 1)
    def _():
      z = DN_ALPHA * r_ref[...] + acc_ref[...]
      xhat, _ = _ln_stats(z)
      xn = xhat * g_ref[...] + b_ref[...]
      z_ref[...] = z
      x_ref[...] = xn
      xb_ref[...] = _bf(xn)

  row = pl.BlockSpec((tm, d), lambda i, kk: (i, 0))
  vec = pl.BlockSpec((1, d), lambda i, kk: (0, 0))
  return pl.pallas_call(
      body, name=name, grid=(m // tm, nk),
      in_specs=[pl.BlockSpec((tm, tk), lambda i, kk: (i, kk)), pl.BlockSpec((tk, d), lambda i, kk: (kk, 0)), row, vec, vec],
      out_specs=[row, row, row],
      out_shape=[jax.ShapeDtypeStruct((m, d), F32), jax.ShapeDtypeStruct((m, d), F32), jax.ShapeDtypeStruct((m, d), BF16)],
      scratch_shapes=[pltpu.VMEM((tm, d), F32)],
      compiler_params=_params(("parallel", "arbitrary")),
  )(a, w, res, g.reshape(1, d), b.reshape(1, d))


def _ffn_up(xb, wg, wu, name, tm=512, tn=1408):
  m, d = xb.shape
  f = wg.shape[0]
  tm, tn = _tile(m, tm), _tile(f, tn)

  def body(x_ref, wg_ref, wu_ref, ga_ref, gb_ref, h_ref):
    xv = x_ref[...]
    a = _dot_nt(xv, wg_ref[...])
    b = _dot_nt(xv, wu_ref[...])
    sg = _sigmoid(a)
    silu = a * sg
    ga_ref[...] = _bf(b * (sg + silu * (1.0 - sg)))
    gb_ref[...] = _bf(silu)
    h_ref[...] = _bf(silu * b)

  wspec = pl.BlockSpec((tn, d), lambda j, i: (j, 0))
  ospec = pl.BlockSpec((tm, tn), lambda j, i: (i, j))
  sds = jax.ShapeDtypeStruct((m, f), BF16)
  return pl.pallas_call(
      body, name=name, grid=(f // tn, m // tm),
      in_specs=[pl.BlockSpec((tm, d), lambda j, i: (i, 0)), wspec, wspec],
      out_specs=[ospec, ospec, ospec], out_shape=[sds, sds, sds],
      compiler_params=_params(("parallel", "parallel")),
  )(xb, wg, wu)


def _ffn_bwd_hidden(dzb, wd, ga, gb, name, tm=512, tn=1408):
  m, d = dzb.shape
  f = wd.shape[0]
  tm, tn = _tile(m, tm), _tile(f, tn)

  def body(dz_ref, wd_ref, ga_ref, gb_ref, da_ref, db_ref):
    dh = _dot_nt(dz_ref[...], wd_ref[...])
    da_ref[...] = _bf(dh * ga_ref[...].astype(F32))
    db_ref[...] = _bf(dh * gb_ref[...].astype(F32))

  hspec = pl.BlockSpec((tm, tn), lambda j, i: (i, j))
  sds = jax.ShapeDtypeStruct((m, f), BF16)
  return pl.pallas_call(
      body, name=name, grid=(f // tn, m // tm),
      in_specs=[pl.BlockSpec((tm, d), lambda j, i: (i, 0)), pl.BlockSpec((tn, d), lambda j, i: (j, 0)), hspec, hspec],
      out_specs=[hspec, hspec], out_shape=[sds, sds],
      compiler_params=_params(("parallel", "parallel")),
  )(dzb, wd, ga, gb)


def _ln_bwd_tail(dy, z_ref, g_ref, dz_ref, dzb_ref, dg_ref, db_ref):
  @pl.when(pl.program_id(0) == 0)
  def _():
    dg_ref[...] = jnp.zeros_like(dg_ref)
    db_ref[...] = jnp.zeros_like(db_ref)

  xhat, rstd = _ln_stats(z_ref[...])
  dz = _ln_bwd(dy, xhat, rstd, g_ref[...])
  dz_ref[...] = dz
  dzb_ref[...] = _bf(dz)
  dg_ref[...] += jnp.sum(dy * xhat, axis=0, keepdims=True)
  db_ref[...] += jnp.sum(dy, axis=0, keepdims=True)


def _ln_bwd_outs(m, d, row, vec):
  return ([row, row, vec, vec],
          [jax.ShapeDtypeStruct((m, d), F32), jax.ShapeDtypeStruct((m, d), BF16),
           jax.ShapeDtypeStruct((1, d), F32), jax.ShapeDtypeStruct((1, d), F32)])


def _ffn_bwd_input_ln(da, db, wg, wu, dz2, z1, g, name, tm=512):
  m, f = da.shape
  d = wg.shape[1]
  tm = _tile(m, tm)

  def body(da_ref, db_ref, wg_ref, wu_ref, dz2_ref, z_ref, g_ref, dz_ref, dzb_ref, dg_ref, dbias_ref):
    dy = DN_ALPHA * dz2_ref[...] + _dot(da_ref[...], wg_ref[...]) + _dot(db_ref[...], wu_ref[...])
    _ln_bwd_tail(dy, z_ref, g_ref, dz_ref, dzb_ref, dg_ref, dbias_ref)

  hspec = pl.BlockSpec((tm, f), lambda i: (i, 0))
  wspec = pl.BlockSpec((f, d), lambda i: (0, 0), pipeline_mode=pl.Buffered(1))
  row = pl.BlockSpec((tm, d), lambda i: (i, 0))
  vec = pl.BlockSpec((1, d), lambda i: (0, 0))
  out_specs, out_shape = _ln_bwd_outs(m, d, row, vec)
  dz, dzb, dg, dbias = pl.pallas_call(
      body, name=name, grid=(m // tm,), in_specs=[hspec, hspec, wspec, wspec, row, row, vec],
      out_specs=out_specs, out_shape=out_shape, compiler_params=_params(("arbitrary",)),
  )(da, db, wg, wu, dz2, z1, g.reshape(1, d))
  return dz, dzb, dg[0], dbias[0]


def _in_proj_bwd_ln(dh, w_in, dz1, z2, g, name, tm=512):
  m, wd = dh.shape
  d = w_in.shape[1]
  tm = _tile(m, tm)

  def body(dh_ref, w_ref, dz1_ref, z_ref, g_ref, dz_ref, dzb_ref, dg_ref, dbias_ref):
    dy = DN_ALPHA * dz1_ref[...] + _dot(dh_ref[...], w_ref[...])
    _ln_bwd_tail(dy, z_ref, g_ref, dz_ref, dzb_ref, dg_ref, dbias_ref)

  row = pl.BlockSpec((tm, d), lambda i: (i, 0))
  vec = pl.BlockSpec((1, d), lambda i: (0, 0))
  out_specs, out_shape = _ln_bwd_outs(m, d, row, vec)
  dz, dzb, dg, dbias = pl.pallas_call(
      body, name=name, grid=(m // tm,),
      in_specs=[pl.BlockSpec((tm, wd), lambda i: (i, 0)),
                pl.BlockSpec((wd, d), lambda i: (0, 0), pipeline_mode=pl.Buffered(1)), row, row, vec],
      out_specs=out_specs, out_shape=out_shape, compiler_params=_params(("arbitrary",)),
  )(dh, w_in, dz1, z2, g.reshape(1, d))
  return dz, dzb, dg[0], dbias[0]


def _alibi_slopes():
  n = N_DIL_HEADS
  return jnp.exp2(-8.0 * (jnp.arange(n, dtype=F32) + 1.0) / n).reshape(1, n)


def _rows(start, d):
  if d == 1:
    return pl.ds(pl.multiple_of(start, BLOCK), BLOCK)
  return pl.ds(start, BLOCK, stride=d)


def _fill_bias_tables(bias_sc, slope0, slope1):
  row = lax.broadcasted_iota(jnp.int32, (2 * BLOCK, 2 * BLOCK), 0)
  col = lax.broadcasted_iota(jnp.int32, (2 * BLOCK, 2 * BLOCK), 1)
  qi = jnp.bitwise_and(row, BLOCK - 1)
  ki = jnp.bitwise_and(col, BLOCK - 1)
  is_cur = col >= BLOCK
  steps = jnp.where(is_cur, qi - ki, qi + BLOCK - ki)
  valid = jnp.logical_and(steps >= 0, steps <= BLOCK)
  slope = jnp.where(row >= BLOCK, slope1, slope0)
  dist = slope * steps.astype(F32)
  for p, (_, d) in enumerate(DIL_PATTERNS):
    base = jnp.where(valid, -d * dist, NEG_BIG)
    bias_sc[2 * p] = base
    bias_sc[2 * p + 1] = jnp.where(is_cur, base, NEG_BIG)


def _stack_heads(v2, head0):
  return jnp.concatenate([jnp.where(head0, v2, 0.0), jnp.where(head0, 0.0, v2)], axis=0)


def _unstack_heads(v, head0):
  return jnp.where(head0, v[:BLOCK], v[BLOCK:])


def _block_rows(idx, d, nblk):
  r = idx // nblk
  n = idx % nblk
  cur = _rows(r + n * (BLOCK * d), d)
  prev = _rows(r + jnp.maximum(n - 1, 0) * (BLOCK * d), d)
  return cur, prev, n


def pair_tile(dt):
  return pltpu.VMEM((2 * BLOCK, 2 * BLOCK), dt)


def _two_stage_loop(nb, first_stage, second_stage, buf_a, buf_b):
  assert nb % 2 == 0

  def pair(t, carry):
    i = 2 * t + 1
    first_stage(i, buf_b)
    second_stage(i - 1, buf_a)
    first_stage(i + 1, buf_a)
    second_stage(i, buf_b)
    return carry

  first_stage(0, buf_a)
  lax.fori_loop(0, nb // 2 - 1, pair, 0)
  first_stage(nb - 1, buf_b)
  second_stage(nb - 2, buf_a)
  second_stage(nb - 1, buf_b)


def _attn_fwd(h3, name, gather=None):
  bl, s, _ = h3.shape
  npair = N_DIL_HEADS // 2
  nb = s // BLOCK
  hosted = gather is not None
  steps = bl * npair

  def body(*refs):
    if hosted:
      sl_ref, q_ref, k_ref, v_ref, w_ref, o_ref, lse_ref, g_ref, o_sc, l_sc, bias_sc, s_a, s_b, send_sems, recv_sems = refs
      step = pl.program_id(0) * npair + pl.program_id(1)
      for phase, at in enumerate((0, (3 * steps) // 4)):
        @pl.when(step == at)
        def _(phase=phase):
          _relayed_gather_phase(phase, w_ref, g_ref, send_sems, recv_sems)
    else:
      sl_ref, q_ref, k_ref, v_ref, o_ref, lse_ref, o_sc, l_sc, bias_sc, s_a, s_b = refs
    hp = pl.program_id(1)
    head0 = lax.broadcasted_iota(jnp.int32, (BLOCK, LANES), 1) < 64
    _fill_bias_tables(bias_sc, sl_ref[0, 2 * hp], sl_ref[0, 2 * hp + 1])

    for p, (_, d) in enumerate(DIL_PATTERNS):
      nblk = (s // d) // BLOCK
      two = nblk > 1
      ks = slice(0, 2 * BLOCK) if two else slice(BLOCK, 2 * BLOCK)

      def scores(idx, buf, p=p, d=d, nblk=nblk, two=two, ks=ks):
        cur, prev, n = _block_rows(idx, d, nblk)
        qs = _bf(_stack_heads(q_ref[cur, :], head0) * ATT_SCALE)
        kb = _bf(jnp.concatenate([k_ref[prev, :], k_ref[cur, :]], axis=0)) if two else _bf(k_ref[cur, :])
        first = jnp.where(n == 0, 1, 0) if two else 0
        buf[:, ks] = _dot_nt(qs, kb) + bias_sc[2 * p + first, :, ks]

      def values(idx, buf, p=p, d=d, nblk=nblk, two=two, ks=ks):
        cur, prev, _ = _block_rows(idx, d, nblk)
        sc = buf[:, ks]
        mx = jnp.max(sc, axis=1, keepdims=True)
        pe = jnp.exp(sc - mx)
        den = jnp.sum(pe, axis=1, keepdims=True)
        vb = _bf(jnp.concatenate([v_ref[prev, :], v_ref[cur, :]], axis=0)) if two else _bf(v_ref[cur, :])
        acc = _dot(_bf(pe), vb) / den
        o_sc[p, cur, :] = _unstack_heads(acc, head0)
        l_sc[p, cur, :] = _unstack_heads(jnp.broadcast_to(mx + jnp.log(den), (2 * BLOCK, LANES)), head0)

      _two_stage_loop(nb, scores, values, s_a, s_b)

    def merge(i, carry):
      rows = pl.ds(pl.multiple_of(i * BLOCK, BLOCK), BLOCK)
      l0, l1, l2 = l_sc[0, rows, :], l_sc[1, rows, :], l_sc[2, rows, :]
      mx = jnp.maximum(jnp.maximum(l0, l1), l2)
      e0, e1, e2 = jnp.exp(l0 - mx), jnp.exp(l1 - mx), jnp.exp(l2 - mx)
      tot = e0 + e1 + e2
      o_ref[rows, :] = _bf((e0 * o_sc[0, rows, :] + e1 * o_sc[1, rows, :] + e2 * o_sc[2, rows, :]) / tot)
      lse_ref[rows, :] = mx + jnp.log(tot)
      return carry

    lax.fori_loop(0, nb, merge, 0)

    if hosted:
      @pl.when(step == steps - 1)
      def _():
        _relayed_gather_phase(2, w_ref, g_ref, send_sems, recv_sems)

  def col(off):
    return pl.BlockSpec((None, s, LANES), lambda b, p: (b, 0, off + p))

  in_specs = [pl.BlockSpec(memory_space=pltpu.SMEM), col(0), col(npair), col(2 * npair)]
  out_specs = [col(0), col(0)]
  out_shape = [jax.ShapeDtypeStruct((bl, s, DIL_WIDTH), BF16), jax.ShapeDtypeStruct((bl, s, DIL_WIDTH), F32)]
  scratch = [pltpu.VMEM((3, s, LANES), F32), pltpu.VMEM((3, s, LANES), F32),
             pltpu.VMEM((6, 2 * BLOCK, 2 * BLOCK), F32), pair_tile(F32), pair_tile(F32)]
  args = (_alibi_slopes(), h3, h3, h3)
  if hosted:
    assert (gather.shape[1] // 2) % ROW_ALIGN == 0 and steps >= 4
    in_specs.append(ANY)
    out_specs.append(ANY)
    out_shape.append(jax.ShapeDtypeStruct((4,) + gather.shape, gather.dtype))
    scratch += [pltpu.SemaphoreType.DMA((N_RELAY_COPIES,)), pltpu.SemaphoreType.DMA((N_RELAY_COPIES,))]
    args += (gather,)
  sem = ("arbitrary", "arbitrary") if hosted else ("parallel", "parallel")
  outs = list(pl.pallas_call(
      body, name=name, grid=(bl, npair), in_specs=in_specs, out_specs=out_specs, out_shape=out_shape,
      scratch_shapes=scratch, compiler_params=_params(sem),
  )(*args))
  if hosted:
    outs[2] = _place_own_block(outs[2], gather)
  return outs


def _attn_bwd(h3, out3, lse3, dcat3, name, exchange=None):
  bl, s, _ = h3.shape
  npair = N_DIL_HEADS // 2
  nb = s // BLOCK
  hosted = exchange is not None

  def body(*refs):
    if hosted:
      (sl_ref, q_ref, k_ref, v_ref, o_ref, l_ref, do_ref, ex_ref, dq_out, dk_out, dv_out, got_ref,
       bias_sc, p_a, ds_a, p_b, ds_b, prod_sc, dq_ref, dk_ref, dv_ref, send_sems, recv_sems) = refs
      step = pl.program_id(0) * npair + pl.program_id(1)

      @pl.when(step == 0)
      def _():
        for cp in _chip_exchange_copies(ex_ref, got_ref, send_sems, recv_sems):
          cp.start()
    else:
      (sl_ref, q_ref, k_ref, v_ref, o_ref, l_ref, do_ref, dq_out, dk_out, dv_out,
       bias_sc, p_a, ds_a, p_b, ds_b, prod_sc, dq_ref, dk_ref, dv_ref) = refs
    hp = pl.program_id(1)
    lane = lax.broadcasted_iota(jnp.int32, (BLOCK, LANES), 1)
    head0 = lane < 64
    _fill_bias_tables(bias_sc, sl_ref[0, 2 * hp], sl_ref[0, 2 * hp + 1])
    dq_ref[...] = jnp.zeros_like(dq_ref)
    dk_ref[...] = jnp.zeros_like(dk_ref)
    dv_ref[...] = jnp.zeros_like(dv_ref)
    prod_sc[...] = do_ref[...] * o_ref[...].astype(F32)

    def per_row(v2, pick0, pick1):
      return jnp.concatenate([jnp.sum(jnp.where(pick0, v2, 0.0), axis=1, keepdims=True),
                              jnp.sum(jnp.where(pick1, v2, 0.0), axis=1, keepdims=True)], axis=0)

    for p, (_, d) in enumerate(DIL_PATTERNS):
      nblk = (s // d) // BLOCK
      two = nblk > 1
      ks = slice(0, 2 * BLOCK) if two else slice(BLOCK, 2 * BLOCK)

      def operands(idx, d=d, nblk=nblk, two=two):
        cur, prev, n = _block_rows(idx, d, nblk)
        qs = _bf(_stack_heads(q_ref[cur, :], head0) * ATT_SCALE)
        dos = _bf(_stack_heads(do_ref[cur, :], head0))
        kb = _bf(jnp.concatenate([k_ref[prev, :], k_ref[cur, :]], axis=0)) if two else _bf(k_ref[cur, :])
        return cur, prev, n, qs, dos, kb

      def probs(idx, bufs, p=p, two=two, ks=ks, operands=operands):
        cur, prev, n, qs, dos, kb = operands(idx)
        vb = _bf(jnp.concatenate([v_ref[prev, :], v_ref[cur, :]], axis=0)) if two else _bf(v_ref[cur, :])
        lse = per_row(l_ref[cur, :], lane == 0, lane == 64)
        delta = per_row(prod_sc[cur, :], head0, jnp.logical_not(head0))
        first = jnp.where(n == 0, 1, 0) if two else 0
        pr = jnp.exp(_dot_nt(qs, kb) + bias_sc[2 * p + first, :, ks] - lse)
        bufs[0][:, ks] = _bf(pr)
        bufs[1][:, ks] = _bf(pr * (_dot_nt(dos, vb) - delta))

      def products(idx, bufs, two=two, ks=ks, operands=operands):
        cur, prev, _, qs, dos, kb = operands(idx)
        pr = bufs[0][:, ks]
        ds = bufs[1][:, ks]
        dq_ref[cur, :] += _unstack_heads(_dot(ds, kb), head0) * ATT_SCALE
        dkb = _dot_tn(ds, qs)
        dvb = _dot_tn(pr, dos)
        if two:
          dk_ref[prev, :] += dkb[:BLOCK]
          dv_ref[prev, :] += dvb[:BLOCK]
          dk_ref[cur, :] += dkb[BLOCK:]
          dv_ref[cur, :] += dvb[BLOCK:]
        else:
          dk_ref[cur, :] += dkb
          dv_ref[cur, :] += dvb

      _two_stage_loop(nb, probs, products, (p_a, ds_a), (p_b, ds_b))

    dq_out[...] = _bf(dq_ref[...])
    dk_out[...] = _bf(dk_ref[...])
    dv_out[...] = _bf(dv_ref[...])

    if hosted:
      @pl.when(step == bl * npair - 1)
      def _():
        for cp in _chip_exchange_copies(ex_ref, got_ref, send_sems, recv_sems):
          cp.wait()

  def col(off):
    return pl.BlockSpec((None, s, LANES), lambda b, p: (b, 0, off + p))

  sds = jax.ShapeDtypeStruct((bl, s, DIL_WIDTH), BF16)
  in_specs = [pl.BlockSpec(memory_space=pltpu.SMEM), col(0), col(npair), col(2 * npair), col(0), col(0), col(0)]
  out_specs, out_shape = [col(0), col(0), col(0)], [sds, sds, sds]
  scratch = [pltpu.VMEM((6, 2 * BLOCK, 2 * BLOCK), F32)] + [pair_tile(BF16)] * 4 + [pltpu.VMEM((s, LANES), F32)] * 4
  args = (_alibi_slopes(), h3, h3, h3, out3, lse3, dcat3)
  if hosted:
    in_specs.append(ANY)
    out_specs.append(ANY)
    out_shape.append(jax.ShapeDtypeStruct((3,) + exchange.shape[1:], exchange.dtype))
    scratch += [pltpu.SemaphoreType.DMA((3,)), pltpu.SemaphoreType.DMA((3,))]
    args += (exchange,)
  sem = ("arbitrary", "arbitrary") if hosted else ("parallel", "parallel")
  return pl.pallas_call(
      body, name=name, grid=(bl, npair), in_specs=in_specs, out_specs=out_specs, out_shape=out_shape,
      scratch_shapes=scratch, compiler_params=_params(sem),
  )(*args)


def _mem_heads(tq):
  lane = lax.broadcasted_iota(jnp.int32, (tq, LANES), 1)
  return lane < 64


def _mem_fwd(h3, qcol, mkv3, name, tq=512):
  bl, s, _ = h3.shape
  nm = mkv3.shape[1]
  tq = _tile(s, tq)

  def body(q_ref, kv_ref, o_ref):
    head0 = _mem_heads(tq)
    for lg in range(MEM_WIDTH // LANES):
      cs = slice(lg * LANES, (lg + 1) * LANES)
      q2 = q_ref[:, cs]
      mk = _bf(kv_ref[:, cs])
      mv = _bf(kv_ref[:, MEM_WIDTH + lg * LANES:MEM_WIDTH + (lg + 1) * LANES])
      outs = []
      for j in range(2):
        hm = head0 if j == 0 else jnp.logical_not(head0)
        qj = _bf(jnp.where(hm, q2, 0.0) * ATT_SCALE)
        sc = _dot_nt(qj, mk)
        mx = jnp.max(sc, axis=1, keepdims=True)
        pe = jnp.exp(sc - mx)
        den = jnp.sum(pe, axis=1, keepdims=True)
        outs.append(_dot(_bf(pe / den), mv))
      o_ref[:, cs] = _bf(jnp.where(head0, outs[0], outs[1]))

  return pl.pallas_call(
      body, name=name, grid=(bl, s // tq),
      in_specs=[pl.BlockSpec((None, tq, MEM_WIDTH), lambda b, i: (b, i, qcol)),
                pl.BlockSpec((None, nm, 2 * MEM_WIDTH), lambda b, i: (b, 0, 0))],
      out_specs=pl.BlockSpec((None, tq, MEM_WIDTH), lambda b, i: (b, i, 0)),
      out_shape=jax.ShapeDtypeStruct((bl, s, MEM_WIDTH), BF16),
      compiler_params=_params(("parallel", "parallel")),
  )(h3, mkv3)


def _mem_bwd(h3, qcol, mkv3, dcat3, name, tq=512):
  bl, s, _ = h3.shape
  nm = mkv3.shape[1]
  tq = _tile(s, tq)
  docol = dcat3.shape[2] // MEM_WIDTH - 1

  def body(q_ref, kv_ref, do_ref, dq_ref, dkv_ref):
    i = pl.program_id(1)

    @pl.when(i == 0)
    def _():
      dkv_ref[...] = jnp.zeros_like(dkv_ref)

    head0 = _mem_heads(tq)
    for lg in range(MEM_WIDTH // LANES):
      cs = slice(lg * LANES, (lg + 1) * LANES)
      vs = slice(MEM_WIDTH + lg * LANES, MEM_WIDTH + (lg + 1) * LANES)
      q2 = q_ref[:, cs]
      do2 = do_ref[:, cs]
      mk = _bf(kv_ref[:, cs])
      mv = _bf(kv_ref[:, vs])
      dq2 = jnp.zeros((tq, LANES), F32)
      dmk = jnp.zeros((nm, LANES), F32)
      dmv = jnp.zeros((nm, LANES), F32)
      for j in range(2):
        hm = head0 if j == 0 else jnp.logical_not(head0)
        qj = _bf(jnp.where(hm, q2, 0.0) * ATT_SCALE)
        doj = _bf(jnp.where(hm, do2, 0.0))
        sc = _dot_nt(qj, mk)
        mx = jnp.max(sc, axis=1, keepdims=True)
        pe = jnp.exp(sc - mx)
        pn = pe / jnp.sum(pe, axis=1, keepdims=True)
        pb = _bf(pn)
        dp = _dot_nt(doj, mv)
        dj = jnp.sum(pb.astype(F32) * dp, axis=1, keepdims=True)
        ds = _bf(pn * (dp - dj))
        dq2 = dq2 + jnp.where(hm, _dot(ds, mk), 0.0) * ATT_SCALE
        dmk = dmk + _dot_tn(ds, qj)
        dmv = dmv + _dot_tn(pb, doj)
      dq_ref[:, cs] = _bf(dq2)
      dkv_ref[:, cs] += dmk
      dkv_ref[:, vs] += dmv

  return pl.pallas_call(
      body, name=name, grid=(bl, s // tq),
      in_specs=[pl.BlockSpec((None, tq, MEM_WIDTH), lambda b, i: (b, i, qcol)),
                pl.BlockSpec((None, nm, 2 * MEM_WIDTH), lambda b, i: (b, 0, 0)),
                pl.BlockSpec((None, tq, MEM_WIDTH), lambda b, i: (b, i, docol))],
      out_specs=[pl.BlockSpec((None, tq, MEM_WIDTH), lambda b, i: (b, i, 0)),
                 pl.BlockSpec((None, nm, 2 * MEM_WIDTH), lambda b, i: (b, 0, 0))],
      out_shape=[jax.ShapeDtypeStruct((bl, s, MEM_WIDTH), BF16), jax.ShapeDtypeStruct((bl, nm, 2 * MEM_WIDTH), F32)],
      compiler_params=_params(("parallel", "arbitrary")),
  )(h3, mkv3, dcat3)


def _sgu_consts():
  ti = lax.broadcasted_iota(jnp.int32, (CHUNK, CHUNK), 0)
  si = lax.broadcasted_iota(jnp.int32, (CHUNK, CHUNK), 1)
  return si <= ti, si < 64


def _sgu_bias_lanes(b_s):
  return jnp.repeat(b_s.T, 64, axis=1)


def _sgu_fwd(h2, ln_g, ln_b, w_s, b_s, name, tr=512):
  t, _ = h2.shape
  tr = _tile(t, tr)
  nch = tr // CHUNK
  npair = N_SGU_GROUPS // 2

  def body(u_ref, v_ref, g_ref, b_ref, w_ref, bs_ref, o_ref, vn_sc):
    tril, head0 = _sgu_consts()
    xhat, _ = _ln_stats(_gelu(v_ref[...]))
    vn_sc[...] = _bf(xhat * g_ref[...] + b_ref[...])
    for jp in range(npair):
      cs = slice(jp * LANES, (jp + 1) * LANES)
      w0 = _bf(jnp.where(tril, w_ref[2 * jp], 0.0))
      w1 = _bf(jnp.where(tril, w_ref[2 * jp + 1], 0.0))
      bias = bs_ref[:, cs]
      for c in range(nch):
        rs = slice(c * CHUNK, (c + 1) * CHUNK)
        vb = vn_sc[rs, cs]
        mixed = jnp.where(head0, _dot(w0, vb), _dot(w1, vb)) + bias
        o_ref[rs, cs] = _bf(_gelu(u_ref[rs, cs]) * mixed)

  blk = lambda j: pl.BlockSpec((tr, SGU_WIDTH), lambda i: (i, j))
  vec = pl.BlockSpec((1, SGU_WIDTH), lambda i: (0, 0))
  return pl.pallas_call(
      body, name=name, grid=(t // tr,),
      in_specs=[blk(0), blk(1), vec, vec,
                pl.BlockSpec((N_SGU_GROUPS, CHUNK, CHUNK), lambda i: (0, 0, 0)),
                pl.BlockSpec((CHUNK, SGU_WIDTH), lambda i: (0, 0))],
      out_specs=blk(0), out_shape=jax.ShapeDtypeStruct((t, SGU_WIDTH), BF16),
      scratch_shapes=[pltpu.VMEM((tr, SGU_WIDTH), BF16)],
      compiler_params=_params(("parallel",)),
  )(h2, h2, ln_g.reshape(1, -1), ln_b.reshape(1, -1), w_s, _sgu_bias_lanes(b_s))


def _sgu_bwd(h2, dcat, ln_g, ln_b, w_s, b_s, name, tr=512):
  t, _ = h2.shape
  tr = _tile(t, tr)
  nch = tr // CHUNK
  npair = N_SGU_GROUPS // 2
  nsteps = t // tr

  def body(u_ref, v_ref, dm_ref, g_ref, b_ref, w_ref, bs_ref,
           du_ref, dv_ref, dw_ref, dbs_ref, dg_ref, db_ref, vn_sc, dmx_sc, dvn_sc, mix_sc, dbx_sc):
    i = pl.program_id(0)
    tril, head0 = _sgu_consts()

    @pl.when(i == 0)
    def _():
      dw_ref[...] = jnp.zeros_like(dw_ref)
      dg_ref[...] = jnp.zeros_like(dg_ref)
      db_ref[...] = jnp.zeros_like(db_ref)
      dbx_sc[...] = jnp.zeros_like(dbx_sc)

    gv, gv_der = _gelu_parts(v_ref[...])
    xhat, rstd = _ln_stats(gv)
    g = g_ref[...]
    vn_sc[...] = _bf(xhat * g + b_ref[...])
    gu, gu_der = _gelu_parts(u_ref[...])
    dmix = dm_ref[...]
    dmx_sc[...] = dmix * gu

    for jp in range(npair):
      cs = slice(jp * LANES, (jp + 1) * LANES)
      w0 = _bf(jnp.where(tril, w_ref[2 * jp], 0.0))
      w1 = _bf(jnp.where(tril, w_ref[2 * jp + 1], 0.0))
      bias = bs_ref[:, cs]
      dw0 = jnp.zeros((CHUNK, CHUNK), F32)
      dw1 = jnp.zeros((CHUNK, CHUNK), F32)
      dbx = jnp.zeros((CHUNK, LANES), F32)
      for c in range(nch):
        rs = slice(c * CHUNK, (c + 1) * CHUNK)
        vb = vn_sc[rs, cs]
        mix_sc[rs, cs] = jnp.where(head0, _dot(w0, vb), _dot(w1, vb)) + bias
        dmx = dmx_sc[rs, cs]
        d0 = _bf(jnp.where(head0, dmx, 0.0))
        d1 = _bf(jnp.where(head0, 0.0, dmx))
        dvn_sc[rs, cs] = _dot_tn(w0, d0) + _dot_tn(w1, d1)
        dw0 = dw0 + _dot_nt(d0, vb)
        dw1 = dw1 + _dot_nt(d1, vb)
        dbx = dbx + dmx
      dw_ref[2 * jp] += dw0
      dw_ref[2 * jp + 1] += dw1
      dbx_sc[:, cs] += dbx

    du_ref[...] = _bf(dmix * mix_sc[...] * gu_der)
    dvn = dvn_sc[...]
    dv_ref[...] = _bf(_ln_bwd(dvn, xhat, rstd, g) * gv_der)
    dg_ref[...] += jnp.sum(dvn * xhat, axis=0, keepdims=True)
    db_ref[...] += jnp.sum(dvn, axis=0, keepdims=True)

    @pl.when(i == nsteps - 1)
    def _():
      lane = lax.broadcasted_iota(jnp.int32, (CHUNK, LANES), 1)
      acc = jnp.zeros((CHUNK, LANES), F32)
      for gi in range(N_SGU_GROUPS):
        jp, j = gi // 2, gi % 2
        part = dbx_sc[:, jp * LANES:(jp + 1) * LANES]
        hm = (lane < 64) if j == 0 else (lane >= 64)
        colsum = jnp.sum(jnp.where(hm, part, 0.0), axis=1, keepdims=True)
        acc = jnp.where(lane == gi, colsum, acc)
        dw_ref[gi] = jnp.where(tril, dw_ref[gi], 0.0)
      dbs_ref[...] = acc

  blk = lambda j: pl.BlockSpec((tr, SGU_WIDTH), lambda i: (i, j))
  vec = pl.BlockSpec((1, SGU_WIDTH), lambda i: (0, 0))
  wspec = pl.BlockSpec((N_SGU_GROUPS, CHUNK, CHUNK), lambda i: (0, 0, 0))
  big = lambda dt: pltpu.VMEM((tr, SGU_WIDTH), dt)
  du, dv, dw, dbs, dg, db = pl.pallas_call(
      body, name=name, grid=(nsteps,),
      in_specs=[blk(0), blk(1), blk(0), vec, vec, wspec, pl.BlockSpec((CHUNK, SGU_WIDTH), lambda i: (0, 0))],
      out_specs=[blk(0), blk(0), wspec, pl.BlockSpec((CHUNK, LANES), lambda i: (0, 0)), vec, vec],
      out_shape=[jax.ShapeDtypeStruct((t, SGU_WIDTH), BF16), jax.ShapeDtypeStruct((t, SGU_WIDTH), BF16),
                 jax.ShapeDtypeStruct((N_SGU_GROUPS, CHUNK, CHUNK), F32), jax.ShapeDtypeStruct((CHUNK, LANES), F32),
                 jax.ShapeDtypeStruct((1, SGU_WIDTH), F32), jax.ShapeDtypeStruct((1, SGU_WIDTH), F32)],
      scratch_shapes=[big(BF16), big(F32), big(F32), big(F32), pltpu.VMEM((CHUNK, SGU_WIDTH), F32)],
      compiler_params=_params(("arbitrary",)),
  )(h2, h2, dcat, ln_g.reshape(1, -1), ln_b.reshape(1, -1), w_s, _sgu_bias_lanes(b_s))
  return du, dv, dw, dbs[:, :N_SGU_GROUPS].T, dg[0], db[0]


def _loss_head(xo, tgt, z, g, name, tm=512):
  m, d = xo.shape
  tm = _tile(m, tm)

  def body(x_ref, t_ref, z_ref, g_ref, l_ref, dz_ref, dzb_ref, dg_ref, dbias_ref):
    @pl.when(pl.program_id(0) == 0)
    def _():
      l_ref[...] = jnp.zeros_like(l_ref)

    diff = x_ref[...] - t_ref[...]
    rowsum = jnp.sum(diff * diff, axis=1, keepdims=True)
    tot = jnp.sum(rowsum, axis=0, keepdims=True) * (0.5 / d)
    l_ref[...] += jnp.broadcast_to(tot, l_ref.shape)
    _ln_bwd_tail(diff * (1.0 / d), z_ref, g_ref, dz_ref, dzb_ref, dg_ref, dbias_ref)

  row = pl.BlockSpec((tm, d), lambda i: (i, 0))
  vec = pl.BlockSpec((1, d), lambda i: (0, 0))
  out_specs, out_shape = _ln_bwd_outs(m, d, row, vec)
  l, dz, dzb, dg, dbias = pl.pallas_call(
      body, name=name, grid=(m // tm,), in_specs=[row, row, row, vec],
      out_specs=[pl.BlockSpec((8, LANES), lambda i: (0, 0))] + out_specs,
      out_shape=[jax.ShapeDtypeStruct((8, LANES), F32)] + out_shape,
      compiler_params=_params(("arbitrary",)),
  )(xo, tgt, z, g.reshape(1, d))
  return l[0, 0], dz, dzb, dg[0], dbias[0]


def _local_step(x3, mem3, tgt3, w, late_weights=None, early_exchange=None):
  w = dict(w)
  bl, s, d = x3.shape
  t = bl * s
  nm = mem3.shape[1]
  mem2 = mem3.reshape(bl * nm, d)
  x = x3.reshape(t, d)
  xb = x
  saved = []
  for i in range(DEPTH):
    j = i // 2
    attn = i % 2 == 0
    mkv = _mm(mem2, w["w_mem_kv"][i], "nn", F32, f"mkv_fwd_{i}", tm=1024, tn=512, tk=1024)
    mkv3 = mkv.reshape(bl, nm, 2 * MEM_WIDTH)
    w_in = w["a_w_in"][j] if attn else w["b_w_in"][j]
    h = _mm(xb, w_in, "nt", F32, f"in_proj_{i}", tm=512, tn=w_in.shape[0], tk=d)
    h3 = h.reshape(bl, s, -1)
    if attn and late_weights is not None and i in late_weights:
      mix3, lse3, gathered = _attn_fwd(h3, f"dil_attn_fwd_{i}", gather=late_weights[i].flat)
      for n, layers in late_weights[i].unpack(gathered).items():
        w[n] = {**w.get(n, {}), **layers}
    elif attn:
      mix3, lse3 = _attn_fwd(h3, f"dil_attn_fwd_{i}")
    if attn:
      mix = mix3.reshape(t, DIL_WIDTH)
      qcol = 3 * DIL_WIDTH // MEM_WIDTH
    else:
      mix = _sgu_fwd(h, w["sgu_ln_g"][j], w["sgu_ln_b"][j], w["sgu_w_s"][j], w["sgu_b_s"][j], f"sgu_fwd_{i}")
      lse3 = None
      qcol = 2 * SGU_WIDTH // MEM_WIDTH
    mo = _mem_fwd(h3, qcol, mkv3, f"mem_attn_fwd_{i}").reshape(t, MEM_WIDTH)
    cat = jnp.concatenate([mix, mo], axis=1)
    z1, xm, xmb = _mm_res_ln(cat, w["w_out"][i], x, w["ln_mix_g"][i], w["ln_mix_b"][i], f"out_proj_ln_{i}", tk=1024)
    ga, gb, hm = _ffn_up(xmb, w["w_gate"][i], w["w_up"][i], f"ffn_up_{i}")
    z2, xo, xob = _mm_res_ln(hm, w["w_down"][i], xm, w["ln_ffn_g"][i], w["ln_ffn_b"][i], f"ffn_down_ln_{i}", tk=hm.shape[1])
    saved.append(dict(xb=xb, h=h, h3=h3, mkv3=mkv3, mix3=(mix3 if attn else None), lse3=lse3, cat=cat, z1=z1,
                      xmb=xmb, ga=ga, gb=gb, hm=hm, z2=z2, qcol=qcol))
    x, xb = xo, xob

  names = ("a_w_in", "b_w_in", "sgu_ln_g", "sgu_ln_b", "sgu_w_s", "sgu_b_s", "w_mem_kv", "w_out",
           "ln_mix_g", "ln_mix_b", "w_gate", "w_up", "w_down", "ln_ffn_g", "ln_ffn_b")
  grads = {n: [None] * len(w[n]) for n in names}
  last = DEPTH - 1
  loss, dz2, dz2b, grads["ln_ffn_g"][last], grads["ln_ffn_b"][last] = _loss_head(
      x, tgt3.reshape(t, d), saved[last]["z2"], w["ln_ffn_g"][last], "loss_head")
  dx = None
  for i in reversed(range(DEPTH)):
    j = i // 2
    attn = i % 2 == 0
    sv = saved[i]
    da, db = _ffn_bwd_hidden(dz2b, w["w_down"][i], sv["ga"], sv["gb"], f"ffn_bwd_hidden_{i}")
    grads["w_down"][i] = _mm(sv["hm"], dz2b, "tn", F32, f"dw_down_{i}", tm=1408, tn=1024, tk=1024)
    grads["w_gate"][i] = _mm(da, sv["xmb"], "tn", F32, f"dw_gate_{i}", tm=1408, tn=1024, tk=1024)
    grads["w_up"][i] = _mm(db, sv["xmb"], "tn", F32, f"dw_up_{i}", tm=1408, tn=1024, tk=1024)
    dz1, dz1b, grads["ln_mix_g"][i], grads["ln_mix_b"][i] = _ffn_bwd_input_ln(
        da, db, w["w_gate"][i], w["w_up"][i], dz2, sv["z1"], w["ln_mix_g"][i], f"ffn_bwd_input_ln_{i}")
    grads["w_out"][i] = _mm(sv["cat"], dz1b, "tn", F32, f"dw_out_{i}", tm=1024, tn=1024, tk=1024)
    dcat = _mm(dz1b, w["w_out"][i], "nt", F32, f"out_proj_bwd_{i}", tm=1024, tn=1024, tk=1024)
    dcat3 = dcat.reshape(bl, s, -1)
    dqm3, dmkv3 = _mem_bwd(sv["h3"], sv["qcol"], sv["mkv3"], dcat3, f"mem_attn_bwd_{i}")
    grads["w_mem_kv"][i] = _mm(mem2, dmkv3.reshape(bl * nm, 2 * MEM_WIDTH), "tn", F32, f"dw_mem_kv_{i}", tm=1024, tn=512, tk=1024)
    dqm = dqm3.reshape(t, MEM_WIDTH)
    if attn and i == 0 and early_exchange is not None:
      q, (pack, state) = early_exchange(grads)
      dq3, dk3, dv3, x3 = _attn_bwd(sv["h3"], sv["mix3"], sv["lse3"], dcat3, f"dil_attn_bwd_{i}", exchange=q)
      grads["early_exchange"] = (pack, state, x3)
      parts = [dq3.reshape(t, -1), dk3.reshape(t, -1), dv3.reshape(t, -1), dqm]
    elif attn:
      dq3, dk3, dv3 = _attn_bwd(sv["h3"], sv["mix3"], sv["lse3"], dcat3, f"dil_attn_bwd_{i}")
      parts = [dq3.reshape(t, -1), dk3.reshape(t, -1), dv3.reshape(t, -1), dqm]
    else:
      du, dv, dws, dbs, dlg, dlb = _sgu_bwd(sv["h"], dcat, w["sgu_ln_g"][j], w["sgu_ln_b"][j], w["sgu_w_s"][j],
                                             w["sgu_b_s"][j], f"sgu_bwd_{i}")
      grads["sgu_w_s"][j], grads["sgu_b_s"][j], grads["sgu_ln_g"][j], grads["sgu_ln_b"][j] = dws, dbs, dlg, dlb
      parts = [du, dv, dqm]
    dh = jnp.concatenate(parts, axis=1)
    w_in = w["a_w_in"][j] if attn else w["b_w_in"][j]
    grads["a_w_in" if attn else "b_w_in"][j] = _mm(dh, sv["xb"], "tn", F32, f"dw_in_{i}", tm=1280 if attn else 896, tn=1024, tk=1024)
    if i > 0:
      dz2, dz2b, grads["ln_ffn_g"][i - 1], grads["ln_ffn_b"][i - 1] = _in_proj_bwd_ln(
          dh, w_in, dz1, saved[i - 1]["z2"], w["ln_ffn_g"][i - 1], f"in_proj_bwd_ln_{i}")
    else:
      dx = _mm(dh, w_in, "nn", F32, f"in_proj_bwd_{i}", add=dz1, add_scale=DN_ALPHA, tm=512, tn=d, tk=w_in.shape[0])
  return loss, dx.reshape(bl, s, d), grads


def _my_place():
  return lax.axis_index("x"), lax.axis_index("y"), lax.axis_index("c")


def _other_chips(x, y):
  return [(1 - x, y), (x, 1 - y), (1 - x, 1 - y)]


ANY = pl.BlockSpec(memory_space=pl.ANY)


def _all_gather_halves(wl, name):
  _, r, c_ = wl.shape

  def body(w_ref, g_ref, send_sems, recv_sems):
    x, y, c = _my_place()
    me = 2 * x + y
    sibling = (x, y, 1 - c)
    chips = _other_chips(x, y)

    def copy(k, src, dst, to):
      return pltpu.make_async_remote_copy(src_ref=src, dst_ref=dst, send_sem=send_sems.at[k], recv_sem=recv_sems.at[k],
                                          device_id=to, device_id_type=MESH_ID)

    first = [copy(k, w_ref.at[c], g_ref.at[me, c], (px, py, c)) for k, (px, py) in enumerate(chips)]
    for cp in first:
      cp.start()
    passed = []
    for k, (px, py) in enumerate(chips):
      landed = g_ref.at[2 * px + py, c]
      copy(k, landed, landed, (px, py, c)).wait_recv()
      fwd = copy(3 + k, landed, landed, sibling)
      fwd.start()
      passed.append(fwd)
    for k, (px, py) in enumerate(chips):
      theirs = g_ref.at[2 * px + py, 1 - c]
      copy(3 + k, theirs, theirs, sibling).wait_recv()
    for cp in first + passed:
      cp.wait_send()

  got = pl.pallas_call(
      body, name=name, in_specs=[ANY], out_specs=ANY,
      out_shape=jax.ShapeDtypeStruct((4, 2, r, c_), wl.dtype),
      scratch_shapes=[pltpu.SemaphoreType.DMA((6,)), pltpu.SemaphoreType.DMA((6,))],
  )(wl)
  chip = 2 * lax.axis_index("x") + lax.axis_index("y")
  return lax.dynamic_update_slice(got, wl[None], (chip, 0, 0, 0))


def _relayed_gather_phase(phase, w_ref, g_ref, send_sems, recv_sems):
  h = w_ref.shape[1] // 2
  x, y, c = _my_place()
  sibling = (x, y, 1 - c)
  xn, yn, dg = _other_chips(x, y)

  def copy(k, src, dst, to):
    return pltpu.make_async_remote_copy(src_ref=src, dst_ref=dst, send_sem=send_sems.at[k], recv_sem=recv_sems.at[k],
                                        device_id=to, device_id_type=MESH_ID)

  def block(chip, half):
    return g_ref.at[2 * chip[0] + chip[1], half]

  def same(k, ref, to):
    return copy(k, ref, ref, to)

  top, bottom = pl.ds(0, h), pl.ds(h, h)
  sends = [copy(0, w_ref.at[c], block((x, y), c), (*xn, c)), copy(1, w_ref.at[c], block((x, y), c), (*yn, c)),
           same(2, block(xn, c).at[top], (*yn, c)), same(3, block(yn, c).at[bottom], (*xn, c)),
           same(4, block(xn, c), sibling), same(5, block(yn, c), sibling), same(6, block(dg, c), sibling)]
  if phase == 0:
    sends[0].start()
    sends[1].start()
  elif phase == 1:
    same(0, block(xn, c), (*xn, c)).wait_recv()
    sends[2].start()
    sends[4].start()
    same(1, block(yn, c), (*yn, c)).wait_recv()
    sends[3].start()
    sends[5].start()
  else:
    same(2, block(dg, c).at[top], (*yn, c)).wait_recv()
    same(3, block(dg, c).at[bottom], (*xn, c)).wait_recv()
    sends[6].start()
    for k, chip in ((4, xn), (5, yn), (6, dg)):
      same(k, block(chip, 1 - c), sibling).wait_recv()
    for cp in sends:
      cp.wait_send()


N_RELAY_COPIES = 7


def _place_own_block(got, wl):
  chip = 2 * lax.axis_index("x") + lax.axis_index("y")
  return lax.dynamic_update_slice(got, wl[None], (chip, 0, 0, 0))


def _all_gather_relayed(wl, name):
  _, r, c_ = wl.shape
  assert (r // 2) % ROW_ALIGN == 0

  def body(w_ref, g_ref, send_sems, recv_sems):
    for phase in range(3):
      _relayed_gather_phase(phase, w_ref, g_ref, send_sems, recv_sems)

  got = pl.pallas_call(
      body, name=name, in_specs=[ANY], out_specs=ANY,
      out_shape=jax.ShapeDtypeStruct((4, 2, r, c_), wl.dtype),
      scratch_shapes=[pltpu.SemaphoreType.DMA((N_RELAY_COPIES,)), pltpu.SemaphoreType.DMA((N_RELAY_COPIES,))],
  )(wl)
  return _place_own_block(got, wl)


def _sibling_swap(v, name):
  def body(v_ref, o_ref, send_sem, recv_sem):
    x, y, c = _my_place()
    cp = pltpu.make_async_remote_copy(src_ref=v_ref, dst_ref=o_ref, send_sem=send_sem, recv_sem=recv_sem,
                                      device_id=(x, y, 1 - c), device_id_type=MESH_ID)
    cp.start()
    cp.wait()

  return pl.pallas_call(
      body, name=name, in_specs=[ANY], out_specs=ANY, out_shape=jax.ShapeDtypeStruct(v.shape, v.dtype),
      scratch_shapes=[pltpu.SemaphoreType.DMA, pltpu.SemaphoreType.DMA],
  )(v)


def _chip_exchange_copies(q_ref, o_ref, send_sems, recv_sems):
  x, y, c = _my_place()
  return [pltpu.make_async_remote_copy(src_ref=q_ref.at[2 * px + py], dst_ref=o_ref.at[k], send_sem=send_sems.at[k],
                                       recv_sem=recv_sems.at[k], device_id=(px, py, c), device_id_type=MESH_ID)
          for k, (px, py) in enumerate(_other_chips(x, y))]


def _chip_exchange(q, name):
  _, r, c_ = q.shape

  def body(q_ref, o_ref, send_sems, recv_sems):
    cps = _chip_exchange_copies(q_ref, o_ref, send_sems, recv_sems)
    for cp in cps:
      cp.start()
    for cp in cps:
      cp.wait()

  return pl.pallas_call(
      body, name=name, in_specs=[ANY], out_specs=ANY, out_shape=jax.ShapeDtypeStruct((3, r, c_), q.dtype),
      scratch_shapes=[pltpu.SemaphoreType.DMA((3,)), pltpu.SemaphoreType.DMA((3,))],
  )(q)


def _share_halves(both, name):
  _, r, c_ = both.shape

  def body(b_ref, o_ref, send_sem, recv_sem):
    x, y, c = _my_place()
    cp = pltpu.make_async_remote_copy(src_ref=b_ref.at[c], dst_ref=o_ref.at[c], send_sem=send_sem, recv_sem=recv_sem,
                                      device_id=(x, y, 1 - c), device_id_type=MESH_ID)
    cp.start()
    cp.wait()

  full = pl.pallas_call(
      body, name=name, in_specs=[ANY], out_specs=ANY, out_shape=jax.ShapeDtypeStruct(both.shape, both.dtype),
      input_output_aliases={0: 0},
      scratch_shapes=[pltpu.SemaphoreType.DMA, pltpu.SemaphoreType.DMA],
  )(both)
  return full.reshape(2 * r, c_)


def _half_spec(tr, c_, pick):
  return pl.BlockSpec((None, None, tr, c_), lambda s, r, place: (s, pick(place), r, 0))


def _cast_other_half(p, place, name, tr=512):
  _, _, r, c_ = p.shape
  tr = _tile(r, tr, 16)

  def body(place_ref, p_ref, o_ref):
    o_ref[...] = _bf(p_ref[...])

  out_spec = pl.BlockSpec((None, tr, c_), lambda s, rr, place: (s, rr, 0))
  return pl.pallas_call(
      body, name=name, out_shape=jax.ShapeDtypeStruct((4, r, c_), BF16),
      grid_spec=pltpu.PrefetchScalarGridSpec(num_scalar_prefetch=1, grid=(4, r // tr),
                                             in_specs=[_half_spec(tr, c_, lambda place: 1 - place[1])], out_specs=out_spec),
      compiler_params=_params(("parallel", "parallel")),
  )(place, p)


def _add_sibling(p, x1, place, name, tr=512):
  _, _, r, c_ = p.shape
  tr = _tile(r, tr, 16)

  def body(place_ref, p_ref, x_ref, o_ref):
    o_ref[...] = _bf(p_ref[...] + x_ref[...].astype(F32))

  row = pl.BlockSpec((None, tr, c_), lambda s, rr, place: (s, rr, 0))
  return pl.pallas_call(
      body, name=name, out_shape=jax.ShapeDtypeStruct((4, r, c_), BF16),
      grid_spec=pltpu.PrefetchScalarGridSpec(num_scalar_prefetch=1, grid=(4, r // tr),
                                             in_specs=[_half_spec(tr, c_, lambda place: place[1]), row], out_specs=row),
      compiler_params=_params(("parallel", "parallel")),
  )(place, p, x1)


def _sum_own(p, x1, x3, place, name, tr=512):
  _, _, r, c_ = p.shape
  tr = _tile(r, tr, 16)

  def body(place_ref, p_ref, x1_ref, x3_ref, o_ref):
    acc = p_ref[...] + x1_ref[...].astype(F32)
    for k in range(3):
      acc = acc + x3_ref[k].astype(F32)
    o_ref[...] = acc

  return pl.pallas_call(
      body, name=name, out_shape=jax.ShapeDtypeStruct((2, r, c_), F32),
      grid_spec=pltpu.PrefetchScalarGridSpec(
          num_scalar_prefetch=1, grid=(r // tr,),
          in_specs=[pl.BlockSpec((None, None, tr, c_), lambda rr, place: (place[0], place[1], rr, 0)),
                    pl.BlockSpec((None, tr, c_), lambda rr, place: (place[0], rr, 0)),
                    pl.BlockSpec((3, tr, c_), lambda rr, place: (0, rr, 0))],
          out_specs=pl.BlockSpec((None, tr, c_), lambda rr, place: (place[1], rr, 0))),
      compiler_params=_params(("parallel",)),
  )(place, p, x1, x3)


def _reduce_scatter_begin(p, tag):
  x, y, c = _my_place()
  place = jnp.stack([2 * x + y, c]).astype(jnp.int32)
  x1 = _sibling_swap(_cast_other_half(p, place, f"rs_cast_other_half_{tag}"), f"rs_sibling_swap_{tag}")
  return _add_sibling(p, x1, place, f"rs_add_sibling_{tag}"), (p, x1, place)


def _reduce_scatter_end(state, x3, tag):
  p, x1, place = state
  return _share_halves(_sum_own(p, x1, x3, place, f"rs_sum_own_{tag}"), f"rs_share_halves_{tag}")


def _adamw(w, g, m, v, name):
  shape = w.shape
  cols = shape[-1]
  rows = w.size // cols
  tr = _tile(rows, max(8, (256 * 1024) // cols // 8 * 8), 8)

  def body(w_ref, g_ref, m_ref, v_ref, d_ref, nm_ref, nv_ref):
    gv = g_ref[...]
    nm = ADAM_B1 * m_ref[...] + (1.0 - ADAM_B1) * gv
    nv = ADAM_B2 * v_ref[...] + (1.0 - ADAM_B2) * (gv * gv)
    m_hat = nm / (1.0 - ADAM_B1 ** ADAM_STEP)
    v_hat = nv / (1.0 - ADAM_B2 ** ADAM_STEP)
    d_ref[...] = -ADAM_LR * (m_hat / (jnp.sqrt(v_hat) + ADAM_EPS) + ADAM_WD * w_ref[...])
    nm_ref[...] = nm
    nv_ref[...] = nv

  spec = pl.BlockSpec((tr, cols), lambda i: (i, 0))
  sds = jax.ShapeDtypeStruct((rows, cols), F32)
  outs = pl.pallas_call(
      body, name=name, grid=(rows // tr,), in_specs=[spec] * 4, out_specs=[spec] * 3, out_shape=[sds] * 3,
      compiler_params=_params(("parallel",)),
  )(*(t.reshape(rows, cols) for t in (w, g, m, v)))
  return tuple(o.reshape(shape) for o in outs)


SHARDED = (("a_w_in", True), ("b_w_in", True), ("w_mem_kv", False), ("w_out", False), ("w_gate", True),
           ("w_up", True), ("w_down", False))
SMALL_SHARDED = (("sgu_ln_g", 1), ("sgu_ln_b", 1))
REPLICATED = ("sgu_w_s", "sgu_b_s", "ln_mix_g", "ln_mix_b", "ln_ffn_g", "ln_ffn_b")
SMALL_ORDER = ("sgu_w_s", "sgu_b_s", "ln_mix_g", "ln_mix_b", "ln_ffn_g", "ln_ffn_b", "sgu_ln_g", "sgu_ln_b")
ROW_ALIGN = 16


def _pad_to(v, n):
  return jnp.pad(v, (0, n - v.shape[0]))


def _round_up(n, a):
  return -(-n // a) * a


def _exchange_form(t, transposed):
  return jnp.swapaxes(t, 1, 2) if transposed else t


def _to_shard_major(full, axis):
  shp = full.shape
  cut = shp[:axis] + (4, shp[axis] // 4) + shp[axis + 1:]
  return jnp.moveaxis(full.reshape(cut), axis, 0).reshape(4, -1, FLAT_COLS)


def _from_shard_major(rows, shard_shape, axis):
  full = jnp.moveaxis(rows.reshape((4,) + tuple(shard_shape)), 0, axis)
  shp = full.shape
  return full.reshape(shp[:axis] + (shp[axis] * shp[axis + 1],) + shp[axis + 2:])


class _WeightPack:
  def __init__(self, items, small=()):
    segs, self.rows, self.small, off = [], {}, [], 0
    for n, l, b in items:
      seg = b.reshape(-1, FLAT_COLS)
      self.rows[(n, l)] = (off, seg.shape[0], b.shape)
      segs.append(seg)
      off += seg.shape[0]
    if small:
      flat = jnp.concatenate([lax.bitcast_convert_type(v, BF16).reshape(-1) for _, v in small])
      rows = _round_up(flat.shape[0], ROW_ALIGN * FLAT_COLS) // FLAT_COLS
      self.small = [(n, v.shape) for n, v in small]
      self.small_rows = (off, rows)
      segs.append(_pad_to(flat, rows * FLAT_COLS).reshape(rows, FLAT_COLS))
      off += rows
    rows_pad = _round_up(off, 4 * ROW_ALIGN)
    if rows_pad > off:
      segs.append(jnp.zeros((rows_pad - off, FLAT_COLS), BF16))
    self.flat = jnp.concatenate(segs).reshape(2, rows_pad // 2, FLAT_COLS)

  def unpack(self, gathered):
    g = gathered.reshape(4, -1, FLAT_COLS)
    out = {}
    for (n, l), (off, nr, shape) in self.rows.items():
      out.setdefault(n, {})[l] = g[:, off:off + nr].reshape((4 * shape[0],) + shape[1:])
    if self.small:
      off, rows = self.small_rows
      flat = g[:, off:off + rows].reshape(4, rows * FLAT_COLS)
      pos = 0
      for n, shape in self.small:
        sz = 2 * math.prod(shape)
        vals = lax.bitcast_convert_type(flat[:, pos:pos + sz].reshape((4,) + shape + (2,)), F32)
        out[n] = _from_shard_major(vals, shape, len(shape) - 1)
        pos += sz
    return out


FIRST_WEIGHTS = (("a_w_in", 0), ("w_mem_kv", 0))


def _weight_packs(shards):
  blocks = {(n, l): _exchange_form(shards[n], tr)[l].astype(BF16)
            for n, tr in SHARDED for l in range(shards[n].shape[0])}
  first = _WeightPack([(n, l, blocks[(n, l)]) for n, l in FIRST_WEIGHTS],
                      small=[(n, shards[n]) for n, _ in SMALL_SHARDED])

  def model_layer(n, l):
    return {"a_w_in": 2 * l, "b_w_in": 2 * l + 1}.get(n, l)

  rest = [(n, l, b) for (n, l), b in blocks.items() if (n, l) not in FIRST_WEIGHTS]
  last = DEPTH - 1
  return first, {0: _WeightPack([it for it in rest if model_layer(it[0], it[1]) < last]),
                 2: _WeightPack([it for it in rest if model_layer(it[0], it[1]) == last])}


def _reduce_grads(grads, shard_shapes):
  early, state, x3 = grads.pop("early_exchange")
  mine_early = _reduce_scatter_end(state, x3, "early")
  late = _GradPack([(n, l, g) for n, _ in SHARDED for l, g in enumerate(grads[n]) if (n, l) not in early.rows])
  q, state = _reduce_scatter_begin(late.p, "late")
  mine_late = _reduce_scatter_end(state, _chip_exchange(q, "rs_chip_exchange_late"), "late")
  out = {}
  for n, tr in SHARDED:
    layers, rows, cols = shard_shapes[n]
    blocks = []
    for l in range(layers):
      pack, mine = (early, mine_early) if (n, l) in early.rows else (late, mine_late)
      off, nr = pack.rows[(n, l)]
      block = mine[off:off + nr]
      blocks.append(block.reshape(cols, rows).T if tr else block.reshape(rows, cols))
    out[n] = jnp.stack(blocks)
  off, quarter_rows = early.rows["small"]
  piece = mine_early[off:off + quarter_rows].reshape(2, quarter_rows // 2, FLAT_COLS)
  small_sum = _all_gather_halves(piece, "gather_small_grads").reshape(-1)
  off = 0
  for n in SMALL_ORDER:
    shape = (len(grads[n]),) + grads[n][0].shape
    sz = math.prod(shape)
    out[n] = small_sum[off:off + sz].reshape(shape)
    off += sz
  return out


class _GradPack:
  def __init__(self, items, small=None):
    segs, self.rows, off = [], {}, 0
    for n, l, g in items:
      seg = _to_shard_major(g, 0)
      self.rows[(n, l)] = (off, seg.shape[1])
      segs.append(seg)
      off += seg.shape[1]
    if small is not None:
      flat = jnp.concatenate([jnp.stack(small[n]).reshape(-1) for n in SMALL_ORDER])
      n_small = _round_up(flat.shape[0], 4 * 2 * 8 * FLAT_COLS)
      quarter_rows = n_small // (4 * FLAT_COLS)
      self.rows["small"] = (off, quarter_rows)
      segs.append(_pad_to(flat, n_small).reshape(4, quarter_rows, FLAT_COLS))
      off += quarter_rows
    rows_pad = _round_up(off, 2 * ROW_ALIGN)
    if rows_pad > off:
      segs.append(jnp.zeros((4, rows_pad - off, FLAT_COLS), F32))
    self.p = jnp.concatenate(segs, axis=1).reshape(4, 2, rows_pad // 2, FLAT_COLS)


def _early_exchange_begin(grads):
  items = [(n, l, g) for n, _ in SHARDED for l, g in enumerate(grads[n]) if g is not None]
  pack = _GradPack(items, small={n: grads[n] for n in SMALL_ORDER})
  q, state = _reduce_scatter_begin(pack.p, "early")
  return q, (pack, state)


WEIGHT_NAMES = ("a_w_in", "b_w_in", "sgu_ln_g", "sgu_ln_b", "sgu_w_s", "sgu_b_s", "w_mem_kv", "w_out",
                "ln_mix_g", "ln_mix_b", "w_gate", "w_up", "w_down", "ln_ffn_g", "ln_ffn_b")


def kernel(x, mem, a_w_in, b_w_in, sgu_ln_g, sgu_ln_b, sgu_w_s, sgu_b_s, w_mem_kv, w_out, ln_mix_g, ln_mix_b, w_gate, w_up, w_down, ln_ffn_g, ln_ffn_b, loss_target, m_a_w_in, m_b_w_in, m_sgu_ln_g, m_sgu_ln_b, m_sgu_w_s, m_sgu_b_s, m_w_mem_kv, m_w_out, m_ln_mix_g, m_ln_mix_b, m_w_gate, m_w_up, m_w_down, m_ln_ffn_g, m_ln_ffn_b, v_a_w_in, v_b_w_in, v_sgu_ln_g, v_sgu_ln_b, v_sgu_w_s, v_sgu_b_s, v_w_mem_kv, v_w_out, v_ln_mix_g, v_ln_mix_b, v_w_gate, v_w_up, v_w_down, v_ln_ffn_g, v_ln_ffn_b):
  weights = dict(a_w_in=a_w_in, b_w_in=b_w_in, sgu_ln_g=sgu_ln_g, sgu_ln_b=sgu_ln_b, sgu_w_s=sgu_w_s, sgu_b_s=sgu_b_s,
                 w_mem_kv=w_mem_kv, w_out=w_out, ln_mix_g=ln_mix_g, ln_mix_b=ln_mix_b, w_gate=w_gate, w_up=w_up,
                 w_down=w_down, ln_ffn_g=ln_ffn_g, ln_ffn_b=ln_ffn_b)
  mom1 = dict(a_w_in=m_a_w_in, b_w_in=m_b_w_in, sgu_ln_g=m_sgu_ln_g, sgu_ln_b=m_sgu_ln_b, sgu_w_s=m_sgu_w_s,
              sgu_b_s=m_sgu_b_s, w_mem_kv=m_w_mem_kv, w_out=m_w_out, ln_mix_g=m_ln_mix_g, ln_mix_b=m_ln_mix_b,
              w_gate=m_w_gate, w_up=m_w_up, w_down=m_w_down, ln_ffn_g=m_ln_ffn_g, ln_ffn_b=m_ln_ffn_b)
  mom2 = dict(a_w_in=v_a_w_in, b_w_in=v_b_w_in, sgu_ln_g=v_sgu_ln_g, sgu_ln_b=v_sgu_ln_b, sgu_w_s=v_sgu_w_s,
              sgu_b_s=v_sgu_b_s, w_mem_kv=v_w_mem_kv, w_out=v_w_out, ln_mix_g=v_ln_mix_g, ln_mix_b=v_ln_mix_b,
              w_gate=v_w_gate, w_up=v_w_up, w_down=v_w_down, ln_ffn_g=v_ln_ffn_g, ln_ffn_b=v_ln_ffn_b)

  first, late = _weight_packs(weights)
  full = first.unpack(_all_gather_relayed(first.flat, "gather_first_weights"))
  for n in REPLICATED:
    full[n] = weights[n]
  loss_part, grad_x, grads = _local_step(x, mem, loss_target, full, late_weights=late,
                                         early_exchange=_early_exchange_begin)
  loss = lax.psum(loss_part, MESH_AXES)

  shard_shapes = {n: weights[n].shape for n, _ in SHARDED}
  red = _reduce_grads(grads, shard_shapes)
  chip = 2 * lax.axis_index("x") + lax.axis_index("y")
  for n, axis in SMALL_SHARDED:
    width = weights[n].shape[axis]
    red[n] = lax.dynamic_slice_in_dim(red[n], chip * width, width, axis)

  small_names = SMALL_ORDER
  def pack(d):
    flat = jnp.concatenate([d[n].reshape(-1) for n in small_names])
    return _pad_to(flat, _round_up(flat.shape[0], 8 * FLAT_COLS)).reshape(-1, FLAT_COLS)
  small_out = _adamw(pack(weights), pack(red), pack(mom1), pack(mom2), "adamw_small")
  delta, new_m, new_v = {}, {}, {}
  off = 0
  for n in small_names:
    sz = weights[n].size
    for dst, src in zip((delta, new_m, new_v), small_out):
      dst[n] = src.reshape(-1)[off:off + sz].reshape(weights[n].shape)
    off += sz
  for n, _ in SHARDED:
    delta[n], new_m[n], new_v[n] = _adamw(weights[n], red[n], mom1[n], mom2[n], f"adamw_{n}")

  return (loss, grad_x, *[red[n] for n in WEIGHT_NAMES], *[delta[n] for n in WEIGHT_NAMES],
          *[new_m[n] for n in WEIGHT_NAMES], *[new_v[n] for n in WEIGHT_NAMES])
```

```python
import math

import jax
import jax.numpy as jnp
from jax import lax
from jax.experimental import pallas as pl
from jax.experimental.pallas import tpu as pltpu

F32 = jnp.float32
BF16 = jnp.bfloat16

DEPTH = 4
HEAD_DIM = 64
N_DIL_HEADS = 12
DIL_WIDTH = N_DIL_HEADS * HEAD_DIM
DIL_PATTERNS = ((128, 1), (512, 4), (2048, 16))
BLOCK = 128
N_SGU_GROUPS = 12
SGU_WIDTH = N_SGU_GROUPS * 64
CHUNK = 128
N_MEM_HEADS = 4
MEM_WIDTH = N_MEM_HEADS * HEAD_DIM
DN_ALPHA = (2 * DEPTH) ** 0.25
LN_EPS = 1e-5
ATT_SCALE = HEAD_DIM ** -0.5
ADAM_LR = 0.001
ADAM_B1 = 0.9
ADAM_B2 = 0.999
ADAM_EPS = 1e-08
ADAM_WD = 0.01
ADAM_STEP = 10
NEG_BIG = -1e30

LANES = 128
FLAT_COLS = 1024
VMEM_LIMIT = 56 * 1024 * 1024
MESH_AXES = ("x", "y", "c")
MESH_ID = pl.DeviceIdType.MESH


def _tile(n, pref, align=LANES):
  if n <= pref:
    return n
  t = (pref // align) * align
  while t >= align:
    if n % t == 0:
      return t
    t -= align
  return n


def _params(sem):
  return pltpu.CompilerParams(dimension_semantics=sem, vmem_limit_bytes=VMEM_LIMIT)


def _dot(a, b):
  return jnp.dot(a, b, preferred_element_type=F32)


def _dot_nt(a, b):
  return lax.dot_general(a, b, (((1,), (1,)), ((), ())), preferred_element_type=F32)


def _dot_tn(a, b):
  return lax.dot_general(a, b, (((0,), (0,)), ((), ())), preferred_element_type=F32)


def _bf(v):
  return v.astype(BF16)


def _ln_stats(z):
  mu = jnp.mean(z, axis=-1, keepdims=True)
  zc = z - mu
  var = jnp.mean(zc * zc, axis=-1, keepdims=True)
  rstd = lax.rsqrt(var + LN_EPS)
  return zc * rstd, rstd


def _ln_bwd(dy, xhat, rstd, g):
  gdy = dy * g
  m1 = jnp.mean(gdy, axis=-1, keepdims=True)
  m2 = jnp.mean(gdy * xhat, axis=-1, keepdims=True)
  return rstd * (gdy - m1 - xhat * m2)


_GELU_C = math.sqrt(2.0 / math.pi)


def _gelu_parts(v):
  v2 = v * v
  t = jnp.tanh(_GELU_C * (v + 0.044715 * v * v2))
  val = 0.5 * v * (1.0 + t)
  der = 0.5 * (1.0 + t) + 0.5 * v * (1.0 - t * t) * (_GELU_C * (1.0 + 3.0 * 0.044715 * v2))
  return val, der


def _gelu(v):
  t = jnp.tanh(_GELU_C * (v + 0.044715 * v * v * v))
  return 0.5 * v * (1.0 + t)


def _sigmoid(v):
  return 1.0 / (1.0 + jnp.exp(-v))


def _mm(a, b, mode, out_dtype, name, add=None, add_scale=1.0, tm=512, tn=512, tk=512):
  if mode == "nn":
    (m, k), (k2, n) = a.shape, b.shape
  elif mode == "nt":
    (m, k), (n, k2) = a.shape, b.shape
  else:
    (k, m), (k2, n) = a.shape, b.shape
  assert k == k2, (a.shape, b.shape, mode)
  tm, tn, tk = _tile(m, tm), _tile(n, tn), _tile(k, tk)
  nk = k // tk
  if mode == "nn":
    a_spec = pl.BlockSpec((tm, tk), lambda i, j, kk: (i, kk))
    b_spec = pl.BlockSpec((tk, tn), lambda i, j, kk: (kk, j))
    dot = _dot
  elif mode == "nt":
    a_spec = pl.BlockSpec((tm, tk), lambda i, j, kk: (i, kk))
    b_spec = pl.BlockSpec((tn, tk), lambda i, j, kk: (j, kk))
    dot = _dot_nt
  else:
    a_spec = pl.BlockSpec((tk, tm), lambda i, j, kk: (kk, i))
    b_spec = pl.BlockSpec((tk, tn), lambda i, j, kk: (kk, j))
    dot = _dot_tn
  o_spec = pl.BlockSpec((tm, tn), lambda i, j, kk: (i, j))
  has_add = add is not None

  def body(*refs):
    if has_add:
      a_ref, b_ref, add_ref, o_ref, acc_ref = refs
    else:
      a_ref, b_ref, o_ref, acc_ref = refs
    kk = pl.program_id(2)

    @pl.when(kk == 0)
    def _():
      acc_ref[...] = jnp.zeros_like(acc_ref)

    acc_ref[...] += dot(_bf(a_ref[...]), _bf(b_ref[...]))

    @pl.when(kk == nk - 1)
    def _():
      r = acc_ref[...]
      if has_add:
        r = r + add_scale * add_ref[...].astype(F32)
      o_ref[...] = r.astype(out_dtype)

  in_specs = [a_spec, b_spec] + ([o_spec] if has_add else [])
  args = (a, b) + ((add,) if has_add else ())
  return pl.pallas_call(
      body, name=name, grid=(m // tm, n // tn, nk), in_specs=in_specs, out_specs=o_spec,
      out_shape=jax.ShapeDtypeStruct((m, n), out_dtype),
      scratch_shapes=[pltpu.VMEM((tm, tn), F32)],
      compiler_params=_params(("parallel", "parallel", "arbitrary")),
  )(*args)


def _mm_res_ln(a, w, res, g, b, name, tm=512, tk=512):
  m, k = a.shape
  d = w.shape[1]
  tm, tk = _tile(m, tm), _tile(k, tk)
  nk = k // tk

  def body(a_ref, w_ref, r_ref, g_ref, b_ref, z_ref, x_ref, xb_ref, acc_ref):
    kk = pl.program_id(1)

    @pl.when(kk == 0)
    def _():
      acc_ref[...] = jnp.zeros_like(acc_ref)

    acc_ref[...] += _dot(_bf(a_ref[...]), _bf(w_ref[...]))

    @pl.when(kk == nk - 1)
    def _():
      z = DN_ALPHA * r_ref[...] + acc_ref[...]
      xhat, _ = _ln_stats(z)
      xn = xhat * g_ref[...] + b_ref[...]
      z_ref[...] = z
      x_ref[...] = xn
      xb_ref[...] = _bf(xn)

  row = pl.BlockSpec((tm, d), lambda i, kk: (i, 0))
  vec = pl.BlockSpec((1, d), lambda i, kk: (0, 0))
  return pl.pallas_call(
      body, name=name, grid=(m // tm, nk),
      in_specs=[pl.BlockSpec((tm, tk), lambda i, kk: (i, kk)), pl.BlockSpec((tk, d), lambda i, kk: (kk, 0)), row, vec, vec],
      out_specs=[row, row, row],
      out_shape=[jax.ShapeDtypeStruct((m, d), F32), jax.ShapeDtypeStruct((m, d), F32), jax.ShapeDtypeStruct((m, d), BF16)],
      scratch_shapes=[pltpu.VMEM((tm, d), F32)],
      compiler_params=_params(("parallel", "arbitrary")),
  )(a, w, res, g.reshape(1, d), b.reshape(1, d))


def _ffn_up(xb, wg, wu, name, tm=512, tn=1408):
  m, d = xb.shape
  f = wg.shape[0]
  tm, tn = _tile(m, tm), _tile(f, tn)

  def body(x_ref, wg_ref, wu_ref, ga_ref, gb_ref, h_ref):
    xv = x_ref[...]
    a = _dot_nt(xv, wg_ref[...])
    b = _dot_nt(xv, wu_ref[...])
    sg = _sigmoid(a)
    silu = a * sg
    ga_ref[...] = _bf(b * (sg + silu * (1.0 - sg)))
    gb_ref[...] = _bf(silu)
    h_ref[...] = _bf(silu * b)

  wspec = pl.BlockSpec((tn, d), lambda j, i: (j, 0))
  ospec = pl.BlockSpec((tm, tn), lambda j, i: (i, j))
  sds = jax.ShapeDtypeStruct((m, f), BF16)
  return pl.pallas_call(
      body, name=name, grid=(f // tn, m // tm),
      in_specs=[pl.BlockSpec((tm, d), lambda j, i: (i, 0)), wspec, wspec],
      out_specs=[ospec, ospec, ospec], out_shape=[sds, sds, sds],
      compiler_params=_params(("parallel", "parallel")),
  )(xb, wg, wu)


def _ffn_bwd_hidden(dzb, wd, ga, gb, name, tm=512, tn=1408):
  m, d = dzb.shape
  f = wd.shape[0]
  tm, tn = _tile(m, tm), _tile(f, tn)

  def body(dz_ref, wd_ref, ga_ref, gb_ref, da_ref, db_ref):
    dh = _dot_nt(dz_ref[...], wd_ref[...])
    da_ref[...] = _bf(dh * ga_ref[...].astype(F32))
    db_ref[...] = _bf(dh * gb_ref[...].astype(F32))

  hspec = pl.BlockSpec((tm, tn), lambda j, i: (i, j))
  sds = jax.ShapeDtypeStruct((m, f), BF16)
  return pl.pallas_call(
      body, name=name, grid=(f // tn, m // tm),
      in_specs=[pl.BlockSpec((tm, d), lambda j, i: (i, 0)), pl.BlockSpec((tn, d), lambda j, i: (j, 0)), hspec, hspec],
      out_specs=[hspec, hspec], out_shape=[sds, sds],
      compiler_params=_params(("parallel", "parallel")),
  )(dzb, wd, ga, gb)


def _ln_bwd_tail(dy, z_ref, g_ref, dz_ref, dzb_ref, dg_ref, db_ref):
  @pl.when(pl.program_id(0) == 0)
  def _():
    dg_ref[...] = jnp.zeros_like(dg_ref)
    db_ref[...] = jnp.zeros_like(db_ref)

  xhat, rstd = _ln_stats(z_ref[...])
  dz = _ln_bwd(dy, xhat, rstd, g_ref[...])
  dz_ref[...] = dz
  dzb_ref[...] = _bf(dz)
  dg_ref[...] += jnp.sum(dy * xhat, axis=0, keepdims=True)
  db_ref[...] += jnp.sum(dy, axis=0, keepdims=True)


def _ln_bwd_outs(m, d, row, vec):
  return ([row, row, vec, vec],
          [jax.ShapeDtypeStruct((m, d), F32), jax.ShapeDtypeStruct((m, d), BF16),
           jax.ShapeDtypeStruct((1, d), F32), jax.ShapeDtypeStruct((1, d), F32)])


def _ffn_bwd_input_ln(da, db, wg, wu, dz2, z1, g, name, tm=512):
  m, f = da.shape
  d = wg.shape[1]
  tm = _tile(m, tm)

  def body(da_ref, db_ref, wg_ref, wu_ref, dz2_ref, z_ref, g_ref, dz_ref, dzb_ref, dg_ref, dbias_ref):
    dy = DN_ALPHA * dz2_ref[...] + _dot(da_ref[...], wg_ref[...]) + _dot(db_ref[...], wu_ref[...])
    _ln_bwd_tail(dy, z_ref, g_ref, dz_ref, dzb_ref, dg_ref, dbias_ref)

  hspec = pl.BlockSpec((tm, f), lambda i: (i, 0))
  wspec = pl.BlockSpec((f, d), lambda i: (0, 0), pipeline_mode=pl.Buffered(1))
  row = pl.BlockSpec((tm, d), lambda i: (i, 0))
  vec = pl.BlockSpec((1, d), lambda i: (0, 0))
  out_specs, out_shape = _ln_bwd_outs(m, d, row, vec)
  dz, dzb, dg, dbias = pl.pallas_call(
      body, name=name, grid=(m // tm,), in_specs=[hspec, hspec, wspec, wspec, row, row, vec],
      out_specs=out_specs, out_shape=out_shape, compiler_params=_params(("arbitrary",)),
  )(da, db, wg, wu, dz2, z1, g.reshape(1, d))
  return dz, dzb, dg[0], dbias[0]


def _in_proj_bwd_ln(dh, w_in, dz1, z2, g, name, tm=512):
  m, wd = dh.shape
  d = w_in.shape[1]
  tm = _tile(m, tm)

  def body(dh_ref, w_ref, dz1_ref, z_ref, g_ref, dz_ref, dzb_ref, dg_ref, dbias_ref):
    dy = DN_ALPHA * dz1_ref[...] + _dot(dh_ref[...], w_ref[...])
    _ln_bwd_tail(dy, z_ref, g_ref, dz_ref, dzb_ref, dg_ref, dbias_ref)

  row = pl.BlockSpec((tm, d), lambda i: (i, 0))
  vec = pl.BlockSpec((1, d), lambda i: (0, 0))
  out_specs, out_shape = _ln_bwd_outs(m, d, row, vec)
  dz, dzb, dg, dbias = pl.pallas_call(
      body, name=name, grid=(m // tm,),
      in_specs=[pl.BlockSpec((tm, wd), lambda i: (i, 0)),
                pl.BlockSpec((wd, d), lambda i: (0, 0), pipeline_mode=pl.Buffered(1)), row, row, vec],
      out_specs=out_specs, out_shape=out_shape, compiler_params=_params(("arbitrary",)),
  )(dh, w_in, dz1, z2, g.reshape(1, d))
  return dz, dzb, dg[0], dbias[0]


def _alibi_slopes():
  n = N_DIL_HEADS
  return jnp.exp2(-8.0 * (jnp.arange(n, dtype=F32) + 1.0) / n).reshape(1, n)


def _rows(start, d):
  if d == 1:
    return pl.ds(pl.multiple_of(start, BLOCK), BLOCK)
  return pl.ds(start, BLOCK, stride=d)


def _fill_bias_tables(bias_sc, slope0, slope1):
  row = lax.broadcasted_iota(jnp.int32, (2 * BLOCK, 2 * BLOCK), 0)
  col = lax.broadcasted_iota(jnp.int32, (2 * BLOCK, 2 * BLOCK), 1)
  qi = jnp.bitwise_and(row, BLOCK - 1)
  ki = jnp.bitwise_and(col, BLOCK - 1)
  is_cur = col >= BLOCK
  steps = jnp.where(is_cur, qi - ki, qi + BLOCK - ki)
  valid = jnp.logical_and(steps >= 0, steps <= BLOCK)
  slope = jnp.where(row >= BLOCK, slope1, slope0)
  dist = slope * steps.astype(F32)
  for p, (_, d) in enumerate(DIL_PATTERNS):
    base = jnp.where(valid, -d * dist, NEG_BIG)
    bias_sc[2 * p] = base
    bias_sc[2 * p + 1] = jnp.where(is_cur, base, NEG_BIG)


def _stack_heads(v2, head0):
  return jnp.concatenate([jnp.where(head0, v2, 0.0), jnp.where(head0, 0.0, v2)], axis=0)


def _unstack_heads(v, head0):
  return jnp.where(head0, v[:BLOCK], v[BLOCK:])


def _block_rows(idx, d, nblk):
  r = idx // nblk
  n = idx % nblk
  cur = _rows(r + n * (BLOCK * d), d)
  prev = _rows(r + jnp.maximum(n - 1, 0) * (BLOCK * d), d)
  return cur, prev, n


def pair_tile(dt):
  return pltpu.VMEM((2 * BLOCK, 2 * BLOCK), dt)


def _two_stage_loop(nb, first_stage, second_stage, bufs):
  a, b, c, d = bufs
  assert nb % 4 == 0 and nb >= 8

  def quad(u, carry):
    i = 4 * u
    first_stage(i + 2, c)
    first_stage(i + 3, d)
    second_stage(i, a)
    second_stage(i + 1, b)
    first_stage(i + 4, a)
    first_stage(i + 5, b)
    second_stage(i + 2, c)
    second_stage(i + 3, d)
    return carry

  first_stage(0, a)
  first_stage(1, b)
  lax.fori_loop(0, nb // 4 - 1, quad, 0)
  i = nb - 4
  first_stage(i + 2, c)
  first_stage(i + 3, d)
  for k, buf in enumerate(bufs):
    second_stage(i + k, buf)


def _attn_fwd(h3, name, gather=None):
  bl, s, _ = h3.shape
  npair = N_DIL_HEADS // 2
  nb = s // BLOCK
  hosted = gather is not None
  steps = bl * npair

  def body(*refs):
    if hosted:
      sl_ref, q_ref, k_ref, v_ref, w_ref, o_ref, lse_ref, g_ref, o_sc, l_sc, bias_sc, *s_bufs, send_sems, recv_sems = refs
      step = pl.program_id(0) * npair + pl.program_id(1)
      for phase, at in enumerate((0, (3 * steps) // 4)):
        @pl.when(step == at)
        def _(phase=phase):
          _relayed_gather_phase(phase, w_ref, g_ref, send_sems, recv_sems)
    else:
      sl_ref, q_ref, k_ref, v_ref, o_ref, lse_ref, o_sc, l_sc, bias_sc, *s_bufs = refs
    hp = pl.program_id(1)
    head0 = lax.broadcasted_iota(jnp.int32, (BLOCK, LANES), 1) < 64
    _fill_bias_tables(bias_sc, sl_ref[0, 2 * hp], sl_ref[0, 2 * hp + 1])

    for p, (_, d) in enumerate(DIL_PATTERNS):
      nblk = (s // d) // BLOCK
      two = nblk > 1
      ks = slice(0, 2 * BLOCK) if two else slice(BLOCK, 2 * BLOCK)

      def scores(idx, buf, p=p, d=d, nblk=nblk, two=two, ks=ks):
        cur, prev, n = _block_rows(idx, d, nblk)
        qs = _bf(_stack_heads(q_ref[cur, :], head0) * ATT_SCALE)
        kb = _bf(jnp.concatenate([k_ref[prev, :], k_ref[cur, :]], axis=0)) if two else _bf(k_ref[cur, :])
        first = jnp.where(n == 0, 1, 0) if two else 0
        buf[:, ks] = _dot_nt(qs, kb) + bias_sc[2 * p + first, :, ks]

      def values(idx, buf, p=p, d=d, nblk=nblk, two=two, ks=ks):
        cur, prev, _ = _block_rows(idx, d, nblk)
        sc = buf[:, ks]
        mx = jnp.max(sc, axis=1, keepdims=True)
        pe = jnp.exp(sc - mx)
        den = jnp.sum(pe, axis=1, keepdims=True)
        vb = _bf(jnp.concatenate([v_ref[prev, :], v_ref[cur, :]], axis=0)) if two else _bf(v_ref[cur, :])
        acc = _dot(_bf(pe), vb) / den
        o_sc[p, cur, :] = _unstack_heads(acc, head0)
        l_sc[p, cur, :] = _unstack_heads(jnp.broadcast_to(mx + jnp.log(den), (2 * BLOCK, LANES)), head0)

      _two_stage_loop(nb, scores, values, s_bufs)

    def merge(i, carry):
      rows = pl.ds(pl.multiple_of(i * BLOCK, BLOCK), BLOCK)
      l0, l1, l2 = l_sc[0, rows, :], l_sc[1, rows, :], l_sc[2, rows, :]
      mx = jnp.maximum(jnp.maximum(l0, l1), l2)
      e0, e1, e2 = jnp.exp(l0 - mx), jnp.exp(l1 - mx), jnp.exp(l2 - mx)
      tot = e0 + e1 + e2
      o_ref[rows, :] = _bf((e0 * o_sc[0, rows, :] + e1 * o_sc[1, rows, :] + e2 * o_sc[2, rows, :]) / tot)
      lse_ref[rows, :] = mx + jnp.log(tot)
      return carry

    lax.fori_loop(0, nb, merge, 0)

    if hosted:
      @pl.when(step == steps - 1)
      def _():
        _relayed_gather_phase(2, w_ref, g_ref, send_sems, recv_sems)

  def col(off):
    return pl.BlockSpec((None, s, LANES), lambda b, p: (b, 0, off + p))

  in_specs = [pl.BlockSpec(memory_space=pltpu.SMEM), col(0), col(npair), col(2 * npair)]
  out_specs = [col(0), col(0)]
  out_shape = [jax.ShapeDtypeStruct((bl, s, DIL_WIDTH), BF16), jax.ShapeDtypeStruct((bl, s, DIL_WIDTH), F32)]
  scratch = [pltpu.VMEM((3, s, LANES), F32), pltpu.VMEM((3, s, LANES), F32),
             pltpu.VMEM((6, 2 * BLOCK, 2 * BLOCK), F32)] + [pair_tile(F32)] * 4
  args = (_alibi_slopes(), h3, h3, h3)
  if hosted:
    assert (gather.shape[1] // 2) % ROW_ALIGN == 0 and steps >= 4
    in_specs.append(ANY)
    out_specs.append(ANY)
    out_shape.append(jax.ShapeDtypeStruct((4,) + gather.shape, gather.dtype))
    scratch += [pltpu.SemaphoreType.DMA((N_RELAY_COPIES,)), pltpu.SemaphoreType.DMA((N_RELAY_COPIES,))]
    args += (gather,)
  sem = ("arbitrary", "arbitrary") if hosted else ("parallel", "parallel")
  outs = list(pl.pallas_call(
      body, name=name, grid=(bl, npair), in_specs=in_specs, out_specs=out_specs, out_shape=out_shape,
      scratch_shapes=scratch, compiler_params=_params(sem),
  )(*args))
  if hosted:
    outs[2] = _place_own_block(outs[2], gather)
  return outs


def _attn_bwd(h3, out3, lse3, dcat3, name, exchange=None):
  bl, s, _ = h3.shape
  npair = N_DIL_HEADS // 2
  nb = s // BLOCK
  hosted = exchange is not None

  def body(*refs):
    if hosted:
      (sl_ref, q_ref, k_ref, v_ref, o_ref, l_ref, do_ref, ex_ref, dq_out, dk_out, dv_out, got_ref,
       bias_sc, *pd, prod_sc, dq_ref, dk_ref, dv_ref, send_sems, recv_sems) = refs
      step = pl.program_id(0) * npair + pl.program_id(1)

      @pl.when(step == 0)
      def _():
        for cp in _chip_exchange_copies(ex_ref, got_ref, send_sems, recv_sems):
          cp.start()
    else:
      (sl_ref, q_ref, k_ref, v_ref, o_ref, l_ref, do_ref, dq_out, dk_out, dv_out,
       bias_sc, *pd, prod_sc, dq_ref, dk_ref, dv_ref) = refs
    pd_bufs = list(zip(pd[0::2], pd[1::2]))
    hp = pl.program_id(1)
    lane = lax.broadcasted_iota(jnp.int32, (BLOCK, LANES), 1)
    head0 = lane < 64
    _fill_bias_tables(bias_sc, sl_ref[0, 2 * hp], sl_ref[0, 2 * hp + 1])
    dq_ref[...] = jnp.zeros_like(dq_ref)
    dk_ref[...] = jnp.zeros_like(dk_ref)
    dv_ref[...] = jnp.zeros_like(dv_ref)
    prod_sc[...] = do_ref[...] * o_ref[...].astype(F32)

    def per_row(v2, pick0, pick1):
      return jnp.concatenate([jnp.sum(jnp.where(pick0, v2, 0.0), axis=1, keepdims=True),
                              jnp.sum(jnp.where(pick1, v2, 0.0), axis=1, keepdims=True)], axis=0)

    for p, (_, d) in enumerate(DIL_PATTERNS):
      nblk = (s // d) // BLOCK
      two = nblk > 1
      ks = slice(0, 2 * BLOCK) if two else slice(BLOCK, 2 * BLOCK)

      def operands(idx, d=d, nblk=nblk, two=two):
        cur, prev, n = _block_rows(idx, d, nblk)
        qs = _bf(_stack_heads(q_ref[cur, :], head0) * ATT_SCALE)
        dos = _bf(_stack_heads(do_ref[cur, :], head0))
        kb = _bf(jnp.concatenate([k_ref[prev, :], k_ref[cur, :]], axis=0)) if two else _bf(k_ref[cur, :])
        return cur, prev, n, qs, dos, kb

      def probs(idx, bufs, p=p, two=two, ks=ks, operands=operands):
        cur, prev, n, qs, dos, kb = operands(idx)
        vb = _bf(jnp.concatenate([v_ref[prev, :], v_ref[cur, :]], axis=0)) if two else _bf(v_ref[cur, :])
        lse = per_row(l_ref[cur, :], lane == 0, lane == 64)
        delta = per_row(prod_sc[cur, :], head0, jnp.logical_not(head0))
        first = jnp.where(n == 0, 1, 0) if two else 0
        pr = jnp.exp(_dot_nt(qs, kb) + bias_sc[2 * p + first, :, ks] - lse)
        bufs[0][:, ks] = _bf(pr)
        bufs[1][:, ks] = _bf(pr * (_dot_nt(dos, vb) - delta))

      def products(idx, bufs, two=two, ks=ks, operands=operands):
        cur, prev, _, qs, dos, kb = operands(idx)
        pr = bufs[0][:, ks]
        ds = bufs[1][:, ks]
        dq_ref[cur, :] += _unstack_heads(_dot(ds, kb), head0) * ATT_SCALE
        dkb = _dot_tn(ds, qs)
        dvb = _dot_tn(pr, dos)
        if two:
          dk_ref[prev, :] += dkb[:BLOCK]
          dv_ref[prev, :] += dvb[:BLOCK]
          dk_ref[cur, :] += dkb[BLOCK:]
          dv_ref[cur, :] += dvb[BLOCK:]
        else:
          dk_ref[cur, :] += dkb
          dv_ref[cur, :] += dvb

      _two_stage_loop(nb, probs, products, pd_bufs)

    dq_out[...] = _bf(dq_ref[...])
    dk_out[...] = _bf(dk_ref[...])
    dv_out[...] = _bf(dv_ref[...])

    if hosted:
      @pl.when(step == bl * npair - 1)
      def _():
        for cp in _chip_exchange_copies(ex_ref, got_ref, send_sems, recv_sems):
          cp.wait()

  def col(off):
    return pl.BlockSpec((None, s, LANES), lambda b, p: (b, 0, off + p))

  sds = jax.ShapeDtypeStruct((bl, s, DIL_WIDTH), BF16)
  in_specs = [pl.BlockSpec(memory_space=pltpu.SMEM), col(0), col(npair), col(2 * npair), col(0), col(0), col(0)]
  out_specs, out_shape = [col(0), col(0), col(0)], [sds, sds, sds]
  scratch = [pltpu.VMEM((6, 2 * BLOCK, 2 * BLOCK), F32)] + [pair_tile(BF16)] * 8 + [pltpu.VMEM((s, LANES), F32)] * 4
  args = (_alibi_slopes(), h3, h3, h3, out3, lse3, dcat3)
  if hosted:
    in_specs.append(ANY)
    out_specs.append(ANY)
    out_shape.append(jax.ShapeDtypeStruct((3,) + exchange.shape[1:], exchange.dtype))
    scratch += [pltpu.SemaphoreType.DMA((3,)), pltpu.SemaphoreType.DMA((3,))]
    args += (exchange,)
  sem = ("arbitrary", "arbitrary") if hosted else ("parallel", "parallel")
  return pl.pallas_call(
      body, name=name, grid=(bl, npair), in_specs=in_specs, out_specs=out_specs, out_shape=out_shape,
      scratch_shapes=scratch, compiler_params=_params(sem),
  )(*args)


def _mem_heads(tq):
  lane = lax.broadcasted_iota(jnp.int32, (tq, LANES), 1)
  return lane < 64


def _mem_fwd(h3, qcol, mkv3, name, tq=512):
  bl, s, _ = h3.shape
  nm = mkv3.shape[1]
  tq = _tile(s, tq)

  def body(q_ref, kv_ref, o_ref):
    head0 = _mem_heads(tq)
    for lg in range(MEM_WIDTH // LANES):
      cs = slice(lg * LANES, (lg + 1) * LANES)
      q2 = q_ref[:, cs]
      mk = _bf(kv_ref[:, cs])
      mv = _bf(kv_ref[:, MEM_WIDTH + lg * LANES:MEM_WIDTH + (lg + 1) * LANES])
      outs = []
      for j in range(2):
        hm = head0 if j == 0 else jnp.logical_not(head0)
        qj = _bf(jnp.where(hm, q2, 0.0) * ATT_SCALE)
        sc = _dot_nt(qj, mk)
        mx = jnp.max(sc, axis=1, keepdims=True)
        pe = jnp.exp(sc - mx)
        den = jnp.sum(pe, axis=1, keepdims=True)
        outs.append(_dot(_bf(pe / den), mv))
      o_ref[:, cs] = _bf(jnp.where(head0, outs[0], outs[1]))

  return pl.pallas_call(
      body, name=name, grid=(bl, s // tq),
      in_specs=[pl.BlockSpec((None, tq, MEM_WIDTH), lambda b, i: (b, i, qcol)),
                pl.BlockSpec((None, nm, 2 * MEM_WIDTH), lambda b, i: (b, 0, 0))],
      out_specs=pl.BlockSpec((None, tq, MEM_WIDTH), lambda b, i: (b, i, 0)),
      out_shape=jax.ShapeDtypeStruct((bl, s, MEM_WIDTH), BF16),
      compiler_params=_params(("parallel", "parallel")),
  )(h3, mkv3)


def _mem_bwd(h3, qcol, mkv3, dcat3, name, tq=512):
  bl, s, _ = h3.shape
  nm = mkv3.shape[1]
  tq = _tile(s, tq)
  docol = dcat3.shape[2] // MEM_WIDTH - 1

  def body(q_ref, kv_ref, do_ref, dq_ref, dkv_ref):
    i = pl.program_id(1)

    @pl.when(i == 0)
    def _():
      dkv_ref[...] = jnp.zeros_like(dkv_ref)

    head0 = _mem_heads(tq)
    for lg in range(MEM_WIDTH // LANES):
      cs = slice(lg * LANES, (lg + 1) * LANES)
      vs = slice(MEM_WIDTH + lg * LANES, MEM_WIDTH + (lg + 1) * LANES)
      q2 = q_ref[:, cs]
      do2 = do_ref[:, cs]
      mk = _bf(kv_ref[:, cs])
      mv = _bf(kv_ref[:, vs])
      dq2 = jnp.zeros((tq, LANES), F32)
      dmk = jnp.zeros((nm, LANES), F32)
      dmv = jnp.zeros((nm, LANES), F32)
      for j in range(2):
        hm = head0 if j == 0 else jnp.logical_not(head0)
        qj = _bf(jnp.where(hm, q2, 0.0) * ATT_SCALE)
        doj = _bf(jnp.where(hm, do2, 0.0))
        sc = _dot_nt(qj, mk)
        mx = jnp.max(sc, axis=1, keepdims=True)
        pe = jnp.exp(sc - mx)
        pn = pe / jnp.sum(pe, axis=1, keepdims=True)
        pb = _bf(pn)
        dp = _dot_nt(doj, mv)
        dj = jnp.sum(pb.astype(F32) * dp, axis=1, keepdims=True)
        ds = _bf(pn * (dp - dj))
        dq2 = dq2 + jnp.where(hm, _dot(ds, mk), 0.0) * ATT_SCALE
        dmk = dmk + _dot_tn(ds, qj)
        dmv = dmv + _dot_tn(pb, doj)
      dq_ref[:, cs] = _bf(dq2)
      dkv_ref[:, cs] += dmk
      dkv_ref[:, vs] += dmv

  return pl.pallas_call(
      body, name=name, grid=(bl, s // tq),
      in_specs=[pl.BlockSpec((None, tq, MEM_WIDTH), lambda b, i: (b, i, qcol)),
                pl.BlockSpec((None, nm, 2 * MEM_WIDTH), lambda b, i: (b, 0, 0)),
                pl.BlockSpec((None, tq, MEM_WIDTH), lambda b, i: (b, i, docol))],
      out_specs=[pl.BlockSpec((None, tq, MEM_WIDTH), lambda b, i: (b, i, 0)),
                 pl.BlockSpec((None, nm, 2 * MEM_WIDTH), lambda b, i: (b, 0, 0))],
      out_shape=[jax.ShapeDtypeStruct((bl, s, MEM_WIDTH), BF16), jax.ShapeDtypeStruct((bl, nm, 2 * MEM_WIDTH), F32)],
      compiler_params=_params(("parallel", "arbitrary")),
  )(h3, mkv3, dcat3)


def _sgu_consts():
  ti = lax.broadcasted_iota(jnp.int32, (CHUNK, CHUNK), 0)
  si = lax.broadcasted_iota(jnp.int32, (CHUNK, CHUNK), 1)
  return si <= ti, si < 64


def _sgu_bias_lanes(b_s):
  return jnp.repeat(b_s.T, 64, axis=1)


def _sgu_fwd(h2, ln_g, ln_b, w_s, b_s, name, tr=512):
  t, _ = h2.shape
  tr = _tile(t, tr)
  nch = tr // CHUNK
  npair = N_SGU_GROUPS // 2

  def body(u_ref, v_ref, g_ref, b_ref, w_ref, bs_ref, o_ref, vn_sc):
    tril, head0 = _sgu_consts()
    xhat, _ = _ln_stats(_gelu(v_ref[...]))
    vn_sc[...] = _bf(xhat * g_ref[...] + b_ref[...])
    for jp in range(npair):
      cs = slice(jp * LANES, (jp + 1) * LANES)
      w0 = _bf(jnp.where(tril, w_ref[2 * jp], 0.0))
      w1 = _bf(jnp.where(tril, w_ref[2 * jp + 1], 0.0))
      bias = bs_ref[:, cs]
      for c in range(nch):
        rs = slice(c * CHUNK, (c + 1) * CHUNK)
        vb = vn_sc[rs, cs]
        mixed = jnp.where(head0, _dot(w0, vb), _dot(w1, vb)) + bias
        o_ref[rs, cs] = _bf(_gelu(u_ref[rs, cs]) * mixed)

  blk = lambda j: pl.BlockSpec((tr, SGU_WIDTH), lambda i: (i, j))
  vec = pl.BlockSpec((1, SGU_WIDTH), lambda i: (0, 0))
  return pl.pallas_call(
      body, name=name, grid=(t // tr,),
      in_specs=[blk(0), blk(1), vec, vec,
                pl.BlockSpec((N_SGU_GROUPS, CHUNK, CHUNK), lambda i: (0, 0, 0)),
                pl.BlockSpec((CHUNK, SGU_WIDTH), lambda i: (0, 0))],
      out_specs=blk(0), out_shape=jax.ShapeDtypeStruct((t, SGU_WIDTH), BF16),
      scratch_shapes=[pltpu.VMEM((tr, SGU_WIDTH), BF16)],
      compiler_params=_params(("parallel",)),
  )(h2, h2, ln_g.reshape(1, -1), ln_b.reshape(1, -1), w_s, _sgu_bias_lanes(b_s))


def _sgu_bwd(h2, dcat, ln_g, ln_b, w_s, b_s, name, tr=512):
  t, _ = h2.shape
  tr = _tile(t, tr)
  nch = tr // CHUNK
  npair = N_SGU_GROUPS // 2
  nsteps = t // tr

  def body(u_ref, v_ref, dm_ref, g_ref, b_ref, w_ref, bs_ref,
           du_ref, dv_ref, dw_ref, dbs_ref, dg_ref, db_ref, vn_sc, dmx_sc, dvn_sc, mix_sc, dbx_sc):
    i = pl.program_id(0)
    tril, head0 = _sgu_consts()

    @pl.when(i == 0)
    def _():
      dw_ref[...] = jnp.zeros_like(dw_ref)
      dg_ref[...] = jnp.zeros_like(dg_ref)
      db_ref[...] = jnp.zeros_like(db_ref)
      dbx_sc[...] = jnp.zeros_like(dbx_sc)

    gv, gv_der = _gelu_parts(v_ref[...])
    xhat, rstd = _ln_stats(gv)
    g = g_ref[...]
    vn_sc[...] = _bf(xhat * g + b_ref[...])
    gu, gu_der = _gelu_parts(u_ref[...])
    dmix = dm_ref[...]
    dmx_sc[...] = dmix * gu

    for jp in range(npair):
      cs = slice(jp * LANES, (jp + 1) * LANES)
      w0 = _bf(jnp.where(tril, w_ref[2 * jp], 0.0))
      w1 = _bf(jnp.where(tril, w_ref[2 * jp + 1], 0.0))
      bias = bs_ref[:, cs]
      dw0 = jnp.zeros((CHUNK, CHUNK), F32)
      dw1 = jnp.zeros((CHUNK, CHUNK), F32)
      dbx = jnp.zeros((CHUNK, LANES), F32)
      for c in range(nch):
        rs = slice(c * CHUNK, (c + 1) * CHUNK)
        vb = vn_sc[rs, cs]
        mix_sc[rs, cs] = jnp.where(head0, _dot(w0, vb), _dot(w1, vb)) + bias
        dmx = dmx_sc[rs, cs]
        d0 = _bf(jnp.where(head0, dmx, 0.0))
        d1 = _bf(jnp.where(head0, 0.0, dmx))
        dvn_sc[rs, cs] = _dot_tn(w0, d0) + _dot_tn(w1, d1)
        dw0 = dw0 + _dot_nt(d0, vb)
        dw1 = dw1 + _dot_nt(d1, vb)
        dbx = dbx + dmx
      dw_ref[2 * jp] += dw0
      dw_ref[2 * jp + 1] += dw1
      dbx_sc[:, cs] += dbx

    du_ref[...] = _bf(dmix * mix_sc[...] * gu_der)
    dvn = dvn_sc[...]
    dv_ref[...] = _bf(_ln_bwd(dvn, xhat, rstd, g) * gv_der)
    dg_ref[...] += jnp.sum(dvn * xhat, axis=0, keepdims=True)
    db_ref[...] += jnp.sum(dvn, axis=0, keepdims=True)

    @pl.when(i == nsteps - 1)
    def _():
      lane = lax.broadcasted_iota(jnp.int32, (CHUNK, LANES), 1)
      acc = jnp.zeros((CHUNK, LANES), F32)
      for gi in range(N_SGU_GROUPS):
        jp, j = gi // 2, gi % 2
        part = dbx_sc[:, jp * LANES:(jp + 1) * LANES]
        hm = (lane < 64) if j == 0 else (lane >= 64)
        colsum = jnp.sum(jnp.where(hm, part, 0.0), axis=1, keepdims=True)
        acc = jnp.where(lane == gi, colsum, acc)
        dw_ref[gi] = jnp.where(tril, dw_ref[gi], 0.0)
      dbs_ref[...] = acc

  blk = lambda j: pl.BlockSpec((tr, SGU_WIDTH), lambda i: (i, j))
  vec = pl.BlockSpec((1, SGU_WIDTH), lambda i: (0, 0))
  wspec = pl.BlockSpec((N_SGU_GROUPS, CHUNK, CHUNK), lambda i: (0, 0, 0))
  big = lambda dt: pltpu.VMEM((tr, SGU_WIDTH), dt)
  du, dv, dw, dbs, dg, db = pl.pallas_call(
      body, name=name, grid=(nsteps,),
      in_specs=[blk(0), blk(1), blk(0), vec, vec, wspec, pl.BlockSpec((CHUNK, SGU_WIDTH), lambda i: (0, 0))],
      out_specs=[blk(0), blk(0), wspec, pl.BlockSpec((CHUNK, LANES), lambda i: (0, 0)), vec, vec],
      out_shape=[jax.ShapeDtypeStruct((t, SGU_WIDTH), BF16), jax.ShapeDtypeStruct((t, SGU_WIDTH), BF16),
                 jax.ShapeDtypeStruct((N_SGU_GROUPS, CHUNK, CHUNK), F32), jax.ShapeDtypeStruct((CHUNK, LANES), F32),
                 jax.ShapeDtypeStruct((1, SGU_WIDTH), F32), jax.ShapeDtypeStruct((1, SGU_WIDTH), F32)],
      scratch_shapes=[big(BF16), big(F32), big(F32), big(F32), pltpu.VMEM((CHUNK, SGU_WIDTH), F32)],
      compiler_params=_params(("arbitrary",)),
  )(h2, h2, dcat, ln_g.reshape(1, -1), ln_b.reshape(1, -1), w_s, _sgu_bias_lanes(b_s))
  return du, dv, dw, dbs[:, :N_SGU_GROUPS].T, dg[0], db[0]


def _loss_head(xo, tgt, z, g, name, tm=512):
  m, d = xo.shape
  tm = _tile(m, tm)

  def body(x_ref, t_ref, z_ref, g_ref, l_ref, dz_ref, dzb_ref, dg_ref, dbias_ref):
    @pl.when(pl.program_id(0) == 0)
    def _():
      l_ref[...] = jnp.zeros_like(l_ref)

    diff = x_ref[...] - t_ref[...]
    rowsum = jnp.sum(diff * diff, axis=1, keepdims=True)
    tot = jnp.sum(rowsum, axis=0, keepdims=True) * (0.5 / d)
    l_ref[...] += jnp.broadcast_to(tot, l_ref.shape)
    _ln_bwd_tail(diff * (1.0 / d), z_ref, g_ref, dz_ref, dzb_ref, dg_ref, dbias_ref)

  row = pl.BlockSpec((tm, d), lambda i: (i, 0))
  vec = pl.BlockSpec((1, d), lambda i: (0, 0))
  out_specs, out_shape = _ln_bwd_outs(m, d, row, vec)
  l, dz, dzb, dg, dbias = pl.pallas_call(
      body, name=name, grid=(m // tm,), in_specs=[row, row, row, vec],
      out_specs=[pl.BlockSpec((8, LANES), lambda i: (0, 0))] + out_specs,
      out_shape=[jax.ShapeDtypeStruct((8, LANES), F32)] + out_shape,
      compiler_params=_params(("arbitrary",)),
  )(xo, tgt, z, g.reshape(1, d))
  return l[0, 0], dz, dzb, dg[0], dbias[0]


def _local_step(x3, mem3, tgt3, w, late_weights=None, early_exchange=None):
  w = dict(w)
  bl, s, d = x3.shape
  t = bl * s
  nm = mem3.shape[1]
  mem2 = mem3.reshape(bl * nm, d)
  x = x3.reshape(t, d)
  xb = x
  saved = []
  for i in range(DEPTH):
    j = i // 2
    attn = i % 2 == 0
    mkv = _mm(mem2, w["w_mem_kv"][i], "nn", F32, f"mkv_fwd_{i}", tm=1024, tn=512, tk=1024)
    mkv3 = mkv.reshape(bl, nm, 2 * MEM_WIDTH)
    w_in = w["a_w_in"][j] if attn else w["b_w_in"][j]
    h = _mm(xb, w_in, "nt", F32, f"in_proj_{i}", tm=512, tn=w_in.shape[0], tk=d)
    h3 = h.reshape(bl, s, -1)
    if attn and late_weights is not None and i in late_weights:
      mix3, lse3, gathered = _attn_fwd(h3, f"dil_attn_fwd_{i}", gather=late_weights[i].flat)
      for n, layers in late_weights[i].unpack(gathered).items():
        w[n] = {**w.get(n, {}), **layers}
    elif attn:
      mix3, lse3 = _attn_fwd(h3, f"dil_attn_fwd_{i}")
    if attn:
      mix = mix3.reshape(t, DIL_WIDTH)
      qcol = 3 * DIL_WIDTH // MEM_WIDTH
    else:
      mix = _sgu_fwd(h, w["sgu_ln_g"][j], w["sgu_ln_b"][j], w["sgu_w_s"][j], w["sgu_b_s"][j], f"sgu_fwd_{i}")
      lse3 = None
      qcol = 2 * SGU_WIDTH // MEM_WIDTH
    mo = _mem_fwd(h3, qcol, mkv3, f"mem_attn_fwd_{i}").reshape(t, MEM_WIDTH)
    cat = jnp.concatenate([mix, mo], axis=1)
    z1, xm, xmb = _mm_res_ln(cat, w["w_out"][i], x, w["ln_mix_g"][i], w["ln_mix_b"][i], f"out_proj_ln_{i}", tk=1024)
    ga, gb, hm = _ffn_up(xmb, w["w_gate"][i], w["w_up"][i], f"ffn_up_{i}")
    z2, xo, xob = _mm_res_ln(hm, w["w_down"][i], xm, w["ln_ffn_g"][i], w["ln_ffn_b"][i], f"ffn_down_ln_{i}", tk=hm.shape[1])
    saved.append(dict(xb=xb, h=h, h3=h3, mkv3=mkv3, mix3=(mix3 if attn else None), lse3=lse3, cat=cat, z1=z1,
                      xmb=xmb, ga=ga, gb=gb, hm=hm, z2=z2, qcol=qcol))
    x, xb = xo, xob

  names = ("a_w_in", "b_w_in", "sgu_ln_g", "sgu_ln_b", "sgu_w_s", "sgu_b_s", "w_mem_kv", "w_out",
           "ln_mix_g", "ln_mix_b", "w_gate", "w_up", "w_down", "ln_ffn_g", "ln_ffn_b")
  grads = {n: [None] * len(w[n]) for n in names}
  last = DEPTH - 1
  loss, dz2, dz2b, grads["ln_ffn_g"][last], grads["ln_ffn_b"][last] = _loss_head(
      x, tgt3.reshape(t, d), saved[last]["z2"], w["ln_ffn_g"][last], "loss_head")
  dx = None
  for i in reversed(range(DEPTH)):
    j = i // 2
    attn = i % 2 == 0
    sv = saved[i]
    da, db = _ffn_bwd_hidden(dz2b, w["w_down"][i], sv["ga"], sv["gb"], f"ffn_bwd_hidden_{i}")
    grads["w_down"][i] = _mm(sv["hm"], dz2b, "tn", F32, f"dw_down_{i}", tm=1408, tn=1024, tk=1024)
    grads["w_gate"][i] = _mm(da, sv["xmb"], "tn", F32, f"dw_gate_{i}", tm=1408, tn=1024, tk=1024)
    grads["w_up"][i] = _mm(db, sv["xmb"], "tn", F32, f"dw_up_{i}", tm=1408, tn=1024, tk=1024)
    dz1, dz1b, grads["ln_mix_g"][i], grads["ln_mix_b"][i] = _ffn_bwd_input_ln(
        da, db, w["w_gate"][i], w["w_up"][i], dz2, sv["z1"], w["ln_mix_g"][i], f"ffn_bwd_input_ln_{i}")
    grads["w_out"][i] = _mm(sv["cat"], dz1b, "tn", F32, f"dw_out_{i}", tm=1024, tn=1024, tk=1024)
    dcat = _mm(dz1b, w["w_out"][i], "nt", F32, f"out_proj_bwd_{i}", tm=1024, tn=1024, tk=1024)
    dcat3 = dcat.reshape(bl, s, -1)
    dqm3, dmkv3 = _mem_bwd(sv["h3"], sv["qcol"], sv["mkv3"], dcat3, f"mem_attn_bwd_{i}")
    grads["w_mem_kv"][i] = _mm(mem2, dmkv3.reshape(bl * nm, 2 * MEM_WIDTH), "tn", F32, f"dw_mem_kv_{i}", tm=1024, tn=512, tk=1024)
    dqm = dqm3.reshape(t, MEM_WIDTH)
    if attn and i == 0 and early_exchange is not None:
      q, (pack, state) = early_exchange(grads)
      dq3, dk3, dv3, x3 = _attn_bwd(sv["h3"], sv["mix3"], sv["lse3"], dcat3, f"dil_attn_bwd_{i}", exchange=q)
      grads["early_exchange"] = (pack, state, x3)
      parts = [dq3.reshape(t, -1), dk3.reshape(t, -1), dv3.reshape(t, -1), dqm]
    elif attn:
      dq3, dk3, dv3 = _attn_bwd(sv["h3"], sv["mix3"], sv["lse3"], dcat3, f"dil_attn_bwd_{i}")
      parts = [dq3.reshape(t, -1), dk3.reshape(t, -1), dv3.reshape(t, -1), dqm]
    else:
      du, dv, dws, dbs, dlg, dlb = _sgu_bwd(sv["h"], dcat, w["sgu_ln_g"][j], w["sgu_ln_b"][j], w["sgu_w_s"][j],
                                             w["sgu_b_s"][j], f"sgu_bwd_{i}")
      grads["sgu_w_s"][j], grads["sgu_b_s"][j], grads["sgu_ln_g"][j], grads["sgu_ln_b"][j] = dws, dbs, dlg, dlb
      parts = [du, dv, dqm]
    dh = jnp.concatenate(parts, axis=1)
    w_in = w["a_w_in"][j] if attn else w["b_w_in"][j]
    grads["a_w_in" if attn else "b_w_in"][j] = _mm(dh, sv["xb"], "tn", F32, f"dw_in_{i}", tm=1280 if attn else 896, tn=1024, tk=1024)
    if i > 0:
      dz2, dz2b, grads["ln_ffn_g"][i - 1], grads["ln_ffn_b"][i - 1] = _in_proj_bwd_ln(
          dh, w_in, dz1, saved[i - 1]["z2"], w["ln_ffn_g"][i - 1], f"in_proj_bwd_ln_{i}")
    else:
      dx = _mm(dh, w_in, "nn", F32, f"in_proj_bwd_{i}", add=dz1, add_scale=DN_ALPHA, tm=512, tn=d, tk=w_in.shape[0])
  return loss, dx.reshape(bl, s, d), grads


def _my_place():
  return lax.axis_index("x"), lax.axis_index("y"), lax.axis_index("c")


def _other_chips(x, y):
  return [(1 - x, y), (x, 1 - y), (1 - x, 1 - y)]


ANY = pl.BlockSpec(memory_space=pl.ANY)


def _all_gather_halves(wl, name):
  _, r, c_ = wl.shape

  def body(w_ref, g_ref, send_sems, recv_sems):
    x, y, c = _my_place()
    me = 2 * x + y
    sibling = (x, y, 1 - c)
    chips = _other_chips(x, y)

    def copy(k, src, dst, to):
      return pltpu.make_async_remote_copy(src_ref=src, dst_ref=dst, send_sem=send_sems.at[k], recv_sem=recv_sems.at[k],
                                          device_id=to, device_id_type=MESH_ID)

    first = [copy(k, w_ref.at[c], g_ref.at[me, c], (px, py, c)) for k, (px, py) in enumerate(chips)]
    for cp in first:
      cp.start()
    passed = []
    for k, (px, py) in enumerate(chips):
      landed = g_ref.at[2 * px + py, c]
      copy(k, landed, landed, (px, py, c)).wait_recv()
      fwd = copy(3 + k, landed, landed, sibling)
      fwd.start()
      passed.append(fwd)
    for k, (px, py) in enumerate(chips):
      theirs = g_ref.at[2 * px + py, 1 - c]
      copy(3 + k, theirs, theirs, sibling).wait_recv()
    for cp in first + passed:
      cp.wait_send()

  got = pl.pallas_call(
      body, name=name, in_specs=[ANY], out_specs=ANY,
      out_shape=jax.ShapeDtypeStruct((4, 2, r, c_), wl.dtype),
      scratch_shapes=[pltpu.SemaphoreType.DMA((6,)), pltpu.SemaphoreType.DMA((6,))],
  )(wl)
  chip = 2 * lax.axis_index("x") + lax.axis_index("y")
  return lax.dynamic_update_slice(got, wl[None], (chip, 0, 0, 0))


def _relayed_gather_phase(phase, w_ref, g_ref, send_sems, recv_sems):
  h = w_ref.shape[1] // 2
  x, y, c = _my_place()
  sibling = (x, y, 1 - c)
  xn, yn, dg = _other_chips(x, y)

  def copy(k, src, dst, to):
    return pltpu.make_async_remote_copy(src_ref=src, dst_ref=dst, send_sem=send_sems.at[k], recv_sem=recv_sems.at[k],
                                        device_id=to, device_id_type=MESH_ID)

  def block(chip, half):
    return g_ref.at[2 * chip[0] + chip[1], half]

  def same(k, ref, to):
    return copy(k, ref, ref, to)

  top, bottom = pl.ds(0, h), pl.ds(h, h)
  sends = [copy(0, w_ref.at[c], block((x, y), c), (*xn, c)), copy(1, w_ref.at[c], block((x, y), c), (*yn, c)),
           same(2, block(xn, c).at[top], (*yn, c)), same(3, block(yn, c).at[bottom], (*xn, c)),
           same(4, block(xn, c), sibling), same(5, block(yn, c), sibling), same(6, block(dg, c), sibling)]
  if phase == 0:
    sends[0].start()
    sends[1].start()
  elif phase == 1:
    same(0, block(xn, c), (*xn, c)).wait_recv()
    sends[2].start()
    sends[4].start()
    same(1, block(yn, c), (*yn, c)).wait_recv()
    sends[3].start()
    sends[5].start()
  else:
    same(2, block(dg, c).at[top], (*yn, c)).wait_recv()
    same(3, block(dg, c).at[bottom], (*xn, c)).wait_recv()
    sends[6].start()
    for k, chip in ((4, xn), (5, yn), (6, dg)):
      same(k, block(chip, 1 - c), sibling).wait_recv()
    for cp in sends:
      cp.wait_send()


N_RELAY_COPIES = 7


def _place_own_block(got, wl):
  chip = 2 * lax.axis_index("x") + lax.axis_index("y")
  return lax.dynamic_update_slice(got, wl[None], (chip, 0, 0, 0))


def _all_gather_relayed(wl, name):
  _, r, c_ = wl.shape
  assert (r // 2) % ROW_ALIGN == 0

  def body(w_ref, g_ref, send_sems, recv_sems):
    for phase in range(3):
      _relayed_gather_phase(phase, w_ref, g_ref, send_sems, recv_sems)

  got = pl.pallas_call(
      body, name=name, in_specs=[ANY], out_specs=ANY,
      out_shape=jax.ShapeDtypeStruct((4, 2, r, c_), wl.dtype),
      scratch_shapes=[pltpu.SemaphoreType.DMA((N_RELAY_COPIES,)), pltpu.SemaphoreType.DMA((N_RELAY_COPIES,))],
  )(wl)
  return _place_own_block(got, wl)


def _sibling_swap(v, name):
  def body(v_ref, o_ref, send_sem, recv_sem):
    x, y, c = _my_place()
    cp = pltpu.make_async_remote_copy(src_ref=v_ref, dst_ref=o_ref, send_sem=send_sem, recv_sem=recv_sem,
                                      device_id=(x, y, 1 - c), device_id_type=MESH_ID)
    cp.start()
    cp.wait()

  return pl.pallas_call(
      body, name=name, in_specs=[ANY], out_specs=ANY, out_shape=jax.ShapeDtypeStruct(v.shape, v.dtype),
      scratch_shapes=[pltpu.SemaphoreType.DMA, pltpu.SemaphoreType.DMA],
  )(v)


def _chip_exchange_copies(q_ref, o_ref, send_sems, recv_sems):
  x, y, c = _my_place()
  return [pltpu.make_async_remote_copy(src_ref=q_ref.at[2 * px + py], dst_ref=o_ref.at[k], send_sem=send_sems.at[k],
                                       recv_sem=recv_sems.at[k], device_id=(px, py, c), device_id_type=MESH_ID)
          for k, (px, py) in enumerate(_other_chips(x, y))]


def _chip_exchange(q, name):
  _, r, c_ = q.shape

  def body(q_ref, o_ref, send_sems, recv_sems):
    cps = _chip_exchange_copies(q_ref, o_ref, send_sems, recv_sems)
    for cp in cps:
      cp.start()
    for cp in cps:
      cp.wait()

  return pl.pallas_call(
      body, name=name, in_specs=[ANY], out_specs=ANY, out_shape=jax.ShapeDtypeStruct((3, r, c_), q.dtype),
      scratch_shapes=[pltpu.SemaphoreType.DMA((3,)), pltpu.SemaphoreType.DMA((3,))],
  )(q)


def _share_halves(both, name):
  _, r, c_ = both.shape

  def body(b_ref, o_ref, send_sem, recv_sem):
    x, y, c = _my_place()
    cp = pltpu.make_async_remote_copy(src_ref=b_ref.at[c], dst_ref=o_ref.at[c], send_sem=send_sem, recv_sem=recv_sem,
                                      device_id=(x, y, 1 - c), device_id_type=MESH_ID)
    cp.start()
    cp.wait()

  full = pl.pallas_call(
      body, name=name, in_specs=[ANY], out_specs=ANY, out_shape=jax.ShapeDtypeStruct(both.shape, both.dtype),
      input_output_aliases={0: 0},
      scratch_shapes=[pltpu.SemaphoreType.DMA, pltpu.SemaphoreType.DMA],
  )(both)
  return full.reshape(2 * r, c_)


def _half_spec(tr, c_, pick):
  return pl.BlockSpec((None, None, tr, c_), lambda s, r, place: (s, pick(place), r, 0))


def _cast_other_half(p, place, name, tr=512):
  _, _, r, c_ = p.shape
  tr = _tile(r, tr, 16)

  def body(place_ref, p_ref, o_ref):
    o_ref[...] = _bf(p_ref[...])

  out_spec = pl.BlockSpec((None, tr, c_), lambda s, rr, place: (s, rr, 0))
  return pl.pallas_call(
      body, name=name, out_shape=jax.ShapeDtypeStruct((4, r, c_), BF16),
      grid_spec=pltpu.PrefetchScalarGridSpec(num_scalar_prefetch=1, grid=(4, r // tr),
                                             in_specs=[_half_spec(tr, c_, lambda place: 1 - place[1])], out_specs=out_spec),
      compiler_params=_params(("parallel", "parallel")),
  )(place, p)


def _add_sibling(p, x1, place, name, tr=512):
  _, _, r, c_ = p.shape
  tr = _tile(r, tr, 16)

  def body(place_ref, p_ref, x_ref, o_ref):
    o_ref[...] = _bf(p_ref[...] + x_ref[...].astype(F32))

  row = pl.BlockSpec((None, tr, c_), lambda s, rr, place: (s, rr, 0))
  return pl.pallas_call(
      body, name=name, out_shape=jax.ShapeDtypeStruct((4, r, c_), BF16),
      grid_spec=pltpu.PrefetchScalarGridSpec(num_scalar_prefetch=1, grid=(4, r // tr),
                                             in_specs=[_half_spec(tr, c_, lambda place: place[1]), row], out_specs=row),
      compiler_params=_params(("parallel", "parallel")),
  )(place, p, x1)


def _sum_own(p, x1, x3, place, name, tr=512):
  _, _, r, c_ = p.shape
  tr = _tile(r, tr, 16)

  def body(place_ref, p_ref, x1_ref, x3_ref, o_ref):
    acc = p_ref[...] + x1_ref[...].astype(F32)
    for k in range(3):
      acc = acc + x3_ref[k].astype(F32)
    o_ref[...] = acc

  return pl.pallas_call(
      body, name=name, out_shape=jax.ShapeDtypeStruct((2, r, c_), F32),
      grid_spec=pltpu.PrefetchScalarGridSpec(
          num_scalar_prefetch=1, grid=(r // tr,),
          in_specs=[pl.BlockSpec((None, None, tr, c_), lambda rr, place: (place[0], place[1], rr, 0)),
                    pl.BlockSpec((None, tr, c_), lambda rr, place: (place[0], rr, 0)),
                    pl.BlockSpec((3, tr, c_), lambda rr, place: (0, rr, 0))],
          out_specs=pl.BlockSpec((None, tr, c_), lambda rr, place: (place[1], rr, 0))),
      compiler_params=_params(("parallel",)),
  )(place, p, x1, x3)


def _reduce_scatter_begin(p, tag):
  x, y, c = _my_place()
  place = jnp.stack([2 * x + y, c]).astype(jnp.int32)
  x1 = _sibling_swap(_cast_other_half(p, place, f"rs_cast_other_half_{tag}"), f"rs_sibling_swap_{tag}")
  return _add_sibling(p, x1, place, f"rs_add_sibling_{tag}"), (p, x1, place)


def _reduce_scatter_end(state, x3, tag):
  p, x1, place = state
  return _share_halves(_sum_own(p, x1, x3, place, f"rs_sum_own_{tag}"), f"rs_share_halves_{tag}")


def _adamw(w, g, m, v, name):
  shape = w.shape
  cols = shape[-1]
  rows = w.size // cols
  tr = _tile(rows, max(8, (256 * 1024) // cols // 8 * 8), 8)

  def body(w_ref, g_ref, m_ref, v_ref, d_ref, nm_ref, nv_ref):
    gv = g_ref[...]
    nm = ADAM_B1 * m_ref[...] + (1.0 - ADAM_B1) * gv
    nv = ADAM_B2 * v_ref[...] + (1.0 - ADAM_B2) * (gv * gv)
    m_hat = nm / (1.0 - ADAM_B1 ** ADAM_STEP)
    v_hat = nv / (1.0 - ADAM_B2 ** ADAM_STEP)
    d_ref[...] = -ADAM_LR * (m_hat / (jnp.sqrt(v_hat) + ADAM_EPS) + ADAM_WD * w_ref[...])
    nm_ref[...] = nm
    nv_ref[...] = nv

  spec = pl.BlockSpec((tr, cols), lambda i: (i, 0))
  sds = jax.ShapeDtypeStruct((rows, cols), F32)
  outs = pl.pallas_call(
      body, name=name, grid=(rows // tr,), in_specs=[spec] * 4, out_specs=[spec] * 3, out_shape=[sds] * 3,
      compiler_params=_params(("parallel",)),
  )(*(t.reshape(rows, cols) for t in (w, g, m, v)))
  return tuple(o.reshape(shape) for o in outs)


SHARDED = (("a_w_in", True), ("b_w_in", True), ("w_mem_kv", False), ("w_out", False), ("w_gate", True),
           ("w_up", True), ("w_down", False))
SMALL_SHARDED = (("sgu_ln_g", 1), ("sgu_ln_b", 1))
REPLICATED = ("sgu_w_s", "sgu_b_s", "ln_mix_g", "ln_mix_b", "ln_ffn_g", "ln_ffn_b")
SMALL_ORDER = ("sgu_w_s", "sgu_b_s", "ln_mix_g", "ln_mix_b", "ln_ffn_g", "ln_ffn_b", "sgu_ln_g", "sgu_ln_b")
ROW_ALIGN = 16


def _pad_to(v, n):
  return jnp.pad(v, (0, n - v.shape[0]))


def _round_up(n, a):
  return -(-n // a) * a


def _exchange_form(t, transposed):
  return jnp.swapaxes(t, 1, 2) if transposed else t


def _to_shard_major(full, axis):
  shp = full.shape
  cut = shp[:axis] + (4, shp[axis] // 4) + shp[axis + 1:]
  return jnp.moveaxis(full.reshape(cut), axis, 0).reshape(4, -1, FLAT_COLS)


def _from_shard_major(rows, shard_shape, axis):
  full = jnp.moveaxis(rows.reshape((4,) + tuple(shard_shape)), 0, axis)
  shp = full.shape
  return full.reshape(shp[:axis] + (shp[axis] * shp[axis + 1],) + shp[axis + 2:])


class _WeightPack:
  def __init__(self, items, small=()):
    segs, self.rows, self.small, off = [], {}, [], 0
    for n, l, b in items:
      seg = b.reshape(-1, FLAT_COLS)
      self.rows[(n, l)] = (off, seg.shape[0], b.shape)
      segs.append(seg)
      off += seg.shape[0]
    if small:
      flat = jnp.concatenate([lax.bitcast_convert_type(v, BF16).reshape(-1) for _, v in small])
      rows = _round_up(flat.shape[0], ROW_ALIGN * FLAT_COLS) // FLAT_COLS
      self.small = [(n, v.shape) for n, v in small]
      self.small_rows = (off, rows)
      segs.append(_pad_to(flat, rows * FLAT_COLS).reshape(rows, FLAT_COLS))
      off += rows
    rows_pad = _round_up(off, 4 * ROW_ALIGN)
    if rows_pad > off:
      segs.append(jnp.zeros((rows_pad - off, FLAT_COLS), BF16))
    self.flat = jnp.concatenate(segs).reshape(2, rows_pad // 2, FLAT_COLS)

  def unpack(self, gathered):
    g = gathered.reshape(4, -1, FLAT_COLS)
    out = {}
    for (n, l), (off, nr, shape) in self.rows.items():
      out.setdefault(n, {})[l] = g[:, off:off + nr].reshape((4 * shape[0],) + shape[1:])
    if self.small:
      off, rows = self.small_rows
      flat = g[:, off:off + rows].reshape(4, rows * FLAT_COLS)
      pos = 0
      for n, shape in self.small:
        sz = 2 * math.prod(shape)
        vals = lax.bitcast_convert_type(flat[:, pos:pos + sz].reshape((4,) + shape + (2,)), F32)
        out[n] = _from_shard_major(vals, shape, len(shape) - 1)
        pos += sz
    return out


FIRST_WEIGHTS = (("a_w_in", 0), ("w_mem_kv", 0))


def _weight_packs(shards):
  blocks = {(n, l): _exchange_form(shards[n], tr)[l].astype(BF16)
            for n, tr in SHARDED for l in range(shards[n].shape[0])}
  first = _WeightPack([(n, l, blocks[(n, l)]) for n, l in FIRST_WEIGHTS],
                      small=[(n, shards[n]) for n, _ in SMALL_SHARDED])

  def model_layer(n, l):
    return {"a_w_in": 2 * l, "b_w_in": 2 * l + 1}.get(n, l)

  rest = [(n, l, b) for (n, l), b in blocks.items() if (n, l) not in FIRST_WEIGHTS]
  last = DEPTH - 1
  return first, {0: _WeightPack([it for it in rest if model_layer(it[0], it[1]) < last]),
                 2: _WeightPack([it for it in rest if model_layer(it[0], it[1]) == last])}


def _reduce_grads(grads, shard_shapes):
  early, state, x3 = grads.pop("early_exchange")
  mine_early = _reduce_scatter_end(state, x3, "early")
  late = _GradPack([(n, l, g) for n, _ in SHARDED for l, g in enumerate(grads[n]) if (n, l) not in early.rows])
  q, state = _reduce_scatter_begin(late.p, "late")
  mine_late = _reduce_scatter_end(state, _chip_exchange(q, "rs_chip_exchange_late"), "late")
  out = {}
  for n, tr in SHARDED:
    layers, rows, cols = shard_shapes[n]
    blocks = []
    for l in range(layers):
      pack, mine = (early, mine_early) if (n, l) in early.rows else (late, mine_late)
      off, nr = pack.rows[(n, l)]
      block = mine[off:off + nr]
      blocks.append(block.reshape(cols, rows).T if tr else block.reshape(rows, cols))
    out[n] = jnp.stack(blocks)
  off, quarter_rows = early.rows["small"]
  piece = mine_early[off:off + quarter_rows].reshape(2, quarter_rows // 2, FLAT_COLS)
  small_sum = _all_gather_halves(piece, "gather_small_grads").reshape(-1)
  off = 0
  for n in SMALL_ORDER:
    shape = (len(grads[n]),) + grads[n][0].shape
    sz = math.prod(shape)
    out[n] = small_sum[off:off + sz].reshape(shape)
    off += sz
  return out


class _GradPack:
  def __init__(self, items, small=None):
    segs, self.rows, off = [], {}, 0
    for n, l, g in items:
      seg = _to_shard_major(g, 0)
      self.rows[(n, l)] = (off, seg.shape[1])
      segs.append(seg)
      off += seg.shape[1]
    if small is not None:
      flat = jnp.concatenate([jnp.stack(small[n]).reshape(-1) for n in SMALL_ORDER])
      n_small = _round_up(flat.shape[0], 4 * 2 * 8 * FLAT_COLS)
      quarter_rows = n_small // (4 * FLAT_COLS)
      self.rows["small"] = (off, quarter_rows)
      segs.append(_pad_to(flat, n_small).reshape(4, quarter_rows, FLAT_COLS))
      off += quarter_rows
    rows_pad = _round_up(off, 2 * ROW_ALIGN)
    if rows_pad > off:
      segs.append(jnp.zeros((4, rows_pad - off, FLAT_COLS), F32))
    self.p = jnp.concatenate(segs, axis=1).reshape(4, 2, rows_pad // 2, FLAT_COLS)


def _early_exchange_begin(grads):
  items = [(n, l, g) for n, _ in SHARDED for l, g in enumerate(grads[n]) if g is not None]
  pack = _GradPack(items, small={n: grads[n] for n in SMALL_ORDER})
  q, state = _reduce_scatter_begin(pack.p, "early")
  return q, (pack, state)


WEIGHT_NAMES = ("a_w_in", "b_w_in", "sgu_ln_g", "sgu_ln_b", "sgu_w_s", "sgu_b_s", "w_mem_kv", "w_out",
                "ln_mix_g", "ln_mix_b", "w_gate", "w_up", "w_down", "ln_ffn_g", "ln_ffn_b")


def kernel(x, mem, a_w_in, b_w_in, sgu_ln_g, sgu_ln_b, sgu_w_s, sgu_b_s, w_mem_kv, w_out, ln_mix_g, ln_mix_b, w_gate, w_up, w_down, ln_ffn_g, ln_ffn_b, loss_target, m_a_w_in, m_b_w_in, m_sgu_ln_g, m_sgu_ln_b, m_sgu_w_s, m_sgu_b_s, m_w_mem_kv, m_w_out, m_ln_mix_g, m_ln_mix_b, m_w_gate, m_w_up, m_w_down, m_ln_ffn_g, m_ln_ffn_b, v_a_w_in, v_b_w_in, v_sgu_ln_g, v_sgu_ln_b, v_sgu_w_s, v_sgu_b_s, v_w_mem_kv, v_w_out, v_ln_mix_g, v_ln_mix_b, v_w_gate, v_w_up, v_w_down, v_ln_ffn_g, v_ln_ffn_b):
  weights = dict(a_w_in=a_w_in, b_w_in=b_w_in, sgu_ln_g=sgu_ln_g, sgu_ln_b=sgu_ln_b, sgu_w_s=sgu_w_s, sgu_b_s=sgu_b_s,
                 w_mem_kv=w_mem_kv, w_out=w_out, ln_mix_g=ln_mix_g, ln_mix_b=ln_mix_b, w_gate=w_gate, w_up=w_up,
                 w_down=w_down, ln_ffn_g=ln_ffn_g, ln_ffn_b=ln_ffn_b)
  mom1 = dict(a_w_in=m_a_w_in, b_w_in=m_b_w_in, sgu_ln_g=m_sgu_ln_g, sgu_ln_b=m_sgu_ln_b, sgu_w_s=m_sgu_w_s,
              sgu_b_s=m_sgu_b_s, w_mem_kv=m_w_mem_kv, w_out=m_w_out, ln_mix_g=m_ln_mix_g, ln_mix_b=m_ln_mix_b,
              w_gate=m_w_gate, w_up=m_w_up, w_down=m_w_down, ln_ffn_g=m_ln_ffn_g, ln_ffn_b=m_ln_ffn_b)
  mom2 = dict(a_w_in=v_a_w_in, b_w_in=v_b_w_in, sgu_ln_g=v_sgu_ln_g, sgu_ln_b=v_sgu_ln_b, sgu_w_s=v_sgu_w_s,
              sgu_b_s=v_sgu_b_s, w_mem_kv=v_w_mem_kv, w_out=v_w_out, ln_mix_g=v_ln_mix_g, ln_mix_b=v_ln_mix_b,
              w_gate=v_w_gate, w_up=v_w_up, w_down=v_w_down, ln_ffn_g=v_ln_ffn_g, ln_ffn_b=v_ln_ffn_b)

  first, late = _weight_packs(weights)
  full = first.unpack(_all_gather_relayed(first.flat, "gather_first_weights"))
  for n in REPLICATED:
    full[n] = weights[n]
  loss_part, grad_x, grads = _local_step(x, mem, loss_target, full, late_weights=late,
                                         early_exchange=_early_exchange_begin)
  loss = lax.psum(loss_part, MESH_AXES)

  shard_shapes = {n: weights[n].shape for n, _ in SHARDED}
  red = _reduce_grads(grads, shard_shapes)
  chip = 2 * lax.axis_index("x") + lax.axis_index("y")
  for n, axis in SMALL_SHARDED:
    width = weights[n].shape[axis]
    red[n] = lax.dynamic_slice_in_dim(red[n], chip * width, width, axis)

  small_names = SMALL_ORDER
  def pack(d):
    flat = jnp.concatenate([d[n].reshape(-1) for n in small_names])
    return _pad_to(flat, _round_up(flat.shape[0], 8 * FLAT_COLS)).reshape(-1, FLAT_COLS)
  small_out = _adamw(pack(weights), pack(red), pack(mom1), pack(mom2), "adamw_small")
  delta, new_m, new_v = {}, {}, {}
  off = 0
  for n in small_names:
    sz = weights[n].size
    for dst, src in zip((delta, new_m, new_v), small_out):
      dst[n] = src.reshape(-1)[off:off + sz].reshape(weights[n].shape)
    off += sz
  for n, _ in SHARDED:
    delta[n], new_m[n], new_v[n] = _adamw(weights[n], red[n], mom1[n], mom2[n], f"adamw_{n}")

  return (loss, grad_x, *[red[n] for n in WEIGHT_NAMES], *[delta[n] for n in WEIGHT_NAMES],
          *[new_m[n] for n in WEIGHT_NAMES], *[new_v[n] for n in WEIGHT_NAMES])
```

```python
import math

import jax
import jax.numpy as jnp
from jax import lax
from jax.experimental import pallas as pl
from jax.experimental.pallas import tpu as pltpu

F32 = jnp.float32
BF16 = jnp.bfloat16

DEPTH = 4
HEAD_DIM = 64
N_DIL_HEADS = 12
DIL_WIDTH = N_DIL_HEADS * HEAD_DIM
DIL_PATTERNS = ((128, 1), (512, 4), (2048, 16))
BLOCK = 128
N_SGU_GROUPS = 12
SGU_WIDTH = N_SGU_GROUPS * 64
CHUNK = 128
N_MEM_HEADS = 4
MEM_WIDTH = N_MEM_HEADS * HEAD_DIM
DN_ALPHA = (2 * DEPTH) ** 0.25
LN_EPS = 1e-5
ATT_SCALE = HEAD_DIM ** -0.5
ADAM_LR = 0.001
ADAM_B1 = 0.9
ADAM_B2 = 0.999
ADAM_EPS = 1e-08
ADAM_WD = 0.01
ADAM_STEP = 10
NEG_BIG = -1e30

LANES = 128
FLAT_COLS = 1024
VMEM_LIMIT = 56 * 1024 * 1024
MESH_AXES = ("x", "y", "c")
MESH_ID = pl.DeviceIdType.MESH


def _tile(n, pref, align=LANES):
  if n <= pref:
    return n
  t = (pref // align) * align
  while t >= align:
    if n % t == 0:
      return t
    t -= align
  return n


def _params(sem):
  return pltpu.CompilerParams(dimension_semantics=sem, vmem_limit_bytes=VMEM_LIMIT)


def _dot(a, b):
  return jnp.dot(a, b, preferred_element_type=F32)


def _dot_nt(a, b):
  return lax.dot_general(a, b, (((1,), (1,)), ((), ())), preferred_element_type=F32)


def _dot_tn(a, b):
  return lax.dot_general(a, b, (((0,), (0,)), ((), ())), preferred_element_type=F32)


def _bf(v):
  return v.astype(BF16)


def _ln_stats(z):
  mu = jnp.mean(z, axis=-1, keepdims=True)
  zc = z - mu
  var = jnp.mean(zc * zc, axis=-1, keepdims=True)
  rstd = lax.rsqrt(var + LN_EPS)
  return zc * rstd, rstd


def _ln_bwd(dy, xhat, rstd, g):
  gdy = dy * g
  m1 = jnp.mean(gdy, axis=-1, keepdims=True)
  m2 = jnp.mean(gdy * xhat, axis=-1, keepdims=True)
  return rstd * (gdy - m1 - xhat * m2)


_GELU_C = math.sqrt(2.0 / math.pi)


def _gelu_parts(v):
  v2 = v * v
  t = jnp.tanh(_GELU_C * (v + 0.044715 * v * v2))
  val = 0.5 * v * (1.0 + t)
  der = 0.5 * (1.0 + t) + 0.5 * v * (1.0 - t * t) * (_GELU_C * (1.0 + 3.0 * 0.044715 * v2))
  return val, der


def _gelu(v):
  t = jnp.tanh(_GELU_C * (v + 0.044715 * v * v * v))
  return 0.5 * v * (1.0 + t)


def _sigmoid(v):
  return 1.0 / (1.0 + jnp.exp(-v))


def _mm(a, b, mode, out_dtype, name, add=None, add_scale=1.0, tm=512, tn=512, tk=512):
  if mode == "nn":
    (m, k), (k2, n) = a.shape, b.shape
  elif mode == "nt":
    (m, k), (n, k2) = a.shape, b.shape
  else:
    (k, m), (k2, n) = a.shape, b.shape
  assert k == k2, (a.shape, b.shape, mode)
  tm, tn, tk = _tile(m, tm), _tile(n, tn), _tile(k, tk)
  nk = k // tk
  if mode == "nn":
    a_spec = pl.BlockSpec((tm, tk), lambda i, j, kk: (i, kk))
    b_spec = pl.BlockSpec((tk, tn), lambda i, j, kk: (kk, j))
    dot = _dot
  elif mode == "nt":
    a_spec = pl.BlockSpec((tm, tk), lambda i, j, kk: (i, kk))
    b_spec = pl.BlockSpec((tn, tk), lambda i, j, kk: (j, kk))
    dot = _dot_nt
  else:
    a_spec = pl.BlockSpec((tk, tm), lambda i, j, kk: (kk, i))
    b_spec = pl.BlockSpec((tk, tn), lambda i, j, kk: (kk, j))
    dot = _dot_tn
  o_spec = pl.BlockSpec((tm, tn), lambda i, j, kk: (i, j))
  has_add = add is not None

  def body(*refs):
    if has_add:
      a_ref, b_ref, add_ref, o_ref, acc_ref = refs
    else:
      a_ref, b_ref, o_ref, acc_ref = refs
    kk = pl.program_id(2)

    @pl.when(kk == 0)
    def _():
      acc_ref[...] = jnp.zeros_like(acc_ref)

    acc_ref[...] += dot(_bf(a_ref[...]), _bf(b_ref[...]))

    @pl.when(kk == nk - 1)
    def _():
      r = acc_ref[...]
      if has_add:
        r = r + add_scale * add_ref[...].astype(F32)
      o_ref[...] = r.astype(out_dtype)

  in_specs = [a_spec, b_spec] + ([o_spec] if has_add else [])
  args = (a, b) + ((add,) if has_add else ())
  return pl.pallas_call(
      body, name=name, grid=(m // tm, n // tn, nk), in_specs=in_specs, out_specs=o_spec,
      out_shape=jax.ShapeDtypeStruct((m, n), out_dtype),
      scratch_shapes=[pltpu.VMEM((tm, tn), F32)],
      compiler_params=_params(("parallel", "parallel", "arbitrary")),
  )(*args)


def _mm_res_ln(a, w, res, g, b, name, tm=512, tk=512):
  m, k = a.shape
  d = w.shape[1]
  tm, tk = _tile(m, tm), _tile(k, tk)
  nk = k // tk

  def body(a_ref, w_ref, r_ref, g_ref, b_ref, z_ref, x_ref, xb_ref, acc_ref):
    kk = pl.program_id(1)

    @pl.when(kk == 0)
    def _():
      acc_ref[...] = jnp.zeros_like(acc_ref)

    acc_ref[...] += _dot(_bf(a_ref[...]), _bf(w_ref[...]))

    @pl.when(kk == nk - 1)
    def _():
      z = DN_ALPHA * r_ref[...] + acc_ref[...]
      xhat, _ = _ln_stats(z)
      xn = xhat * g_ref[...] + b_ref[...]
      z_ref[...] = z
      x_ref[...] = xn
      xb_ref[...] = _bf(xn)

  row = pl.BlockSpec((tm, d), lambda i, kk: (i, 0))
  vec = pl.BlockSpec((1, d), lambda i, kk: (0, 0))
  return pl.pallas_call(
      body, name=name, grid=(m // tm, nk),
      in_specs=[pl.BlockSpec((tm, tk), lambda i, kk: (i, kk)), pl.BlockSpec((tk, d), lambda i, kk: (kk, 0)), row, vec, vec],
      out_specs=[row, row, row],
      out_shape=[jax.ShapeDtypeStruct((m, d), F32), jax.ShapeDtypeStruct((m, d), F32), jax.ShapeDtypeStruct((m, d), BF16)],
      scratch_shapes=[pltpu.VMEM((tm, d), F32)],
      compiler_params=_params(("parallel", "arbitrary")),
  )(a, w, res, g.reshape(1, d), b.reshape(1, d))


def _ffn_up(xb, wg, wu, name, tm=512, tn=1408):
  m, d = xb.shape
  f = wg.shape[0]
  tm, tn = _tile(m, tm), _tile(f, tn)

  def body(x_ref, wg_ref, wu_ref, ga_ref, gb_ref, h_ref):
    xv = x_ref[...]
    a = _dot_nt(xv, wg_ref[...])
    b = _dot_nt(xv, wu_ref[...])
    sg = _sigmoid(a)
    silu = a * sg
    ga_ref[...] = _bf(b * (sg + silu * (1.0 - sg)))
    gb_ref[...] = _bf(silu)
    h_ref[...] = _bf(silu * b)

  wspec = pl.BlockSpec((tn, d), lambda j, i: (j, 0))
  ospec = pl.BlockSpec((tm, tn), lambda j, i: (i, j))
  sds = jax.ShapeDtypeStruct((m, f), BF16)
  return pl.pallas_call(
      body, name=name, grid=(f // tn, m // tm),
      in_specs=[pl.BlockSpec((tm, d), lambda j, i: (i, 0)), wspec, wspec],
      out_specs=[ospec, ospec, ospec], out_shape=[sds, sds, sds],
      compiler_params=_params(("parallel", "parallel")),
  )(xb, wg, wu)


def _ffn_bwd_hidden(dzb, wd, ga, gb, name, tm=512, tn=1408):
  m, d = dzb.shape
  f = wd.shape[0]
  tm, tn = _tile(m, tm), _tile(f, tn)

  def body(dz_ref, wd_ref, ga_ref, gb_ref, da_ref, db_ref):
    dh = _dot_nt(dz_ref[...], wd_ref[...])
    da_ref[...] = _bf(dh * ga_ref[...].astype(F32))
    db_ref[...] = _bf(dh * gb_ref[...].astype(F32))

  hspec = pl.BlockSpec((tm, tn), lambda j, i: (i, j))
  sds = jax.ShapeDtypeStruct((m, f), BF16)
  return pl.pallas_call(
      body, name=name, grid=(f // tn, m // tm),
      in_specs=[pl.BlockSpec((tm, d), lambda j, i: (i, 0)), pl.BlockSpec((tn, d), lambda j, i: (j, 0)), hspec, hspec],
      out_specs=[hspec, hspec], out_shape=[sds, sds],
      compiler_params=_params(("parallel", "parallel")),
  )(dzb, wd, ga, gb)


def _ln_bwd_tail(dy, z_ref, g_ref, dz_ref, dzb_ref, dg_ref, db_ref):
  @pl.when(pl.program_id(0) == 0)
  def _():
    dg_ref[...] = jnp.zeros_like(dg_ref)
    db_ref[...] = jnp.zeros_like(db_ref)

  xhat, rstd = _ln_stats(z_ref[...])
  dz = _ln_bwd(dy, xhat, rstd, g_ref[...])
  dz_ref[...] = dz
  dzb_ref[...] = _bf(dz)
  dg_ref[...] += jnp.sum(dy * xhat, axis=0, keepdims=True)
  db_ref[...] += jnp.sum(dy, axis=0, keepdims=True)


def _ln_bwd_outs(m, d, row, vec):
  return ([row, row, vec, vec],
          [jax.ShapeDtypeStruct((m, d), F32), jax.ShapeDtypeStruct((m, d), BF16),
           jax.ShapeDtypeStruct((1, d), F32), jax.ShapeDtypeStruct((1, d), F32)])


def _ffn_bwd_input_ln(da, db, wg, wu, dz2, z1, g, name, tm=512):
  m, f = da.shape
  d = wg.shape[1]
  tm = _tile(m, tm)

  def body(da_ref, db_ref, wg_ref, wu_ref, dz2_ref, z_ref, g_ref, dz_ref, dzb_ref, dg_ref, dbias_ref):
    dy = DN_ALPHA * dz2_ref[...] + _dot(da_ref[...], wg_ref[...]) + _dot(db_ref[...], wu_ref[...])
    _ln_bwd_tail(dy, z_ref, g_ref, dz_ref, dzb_ref, dg_ref, dbias_ref)

  hspec = pl.BlockSpec((tm, f), lambda i: (i, 0))
  wspec = pl.BlockSpec((f, d), lambda i: (0, 0), pipeline_mode=pl.Buffered(1))
  row = pl.BlockSpec((tm, d), lambda i: (i, 0))
  vec = pl.BlockSpec((1, d), lambda i: (0, 0))
  out_specs, out_shape = _ln_bwd_outs(m, d, row, vec)
  dz, dzb, dg, dbias = pl.pallas_call(
      body, name=name, grid=(m // tm,), in_specs=[hspec, hspec, wspec, wspec, row, row, vec],
      out_specs=out_specs, out_shape=out_shape, compiler_params=_params(("arbitrary",)),
  )(da, db, wg, wu, dz2, z1, g.reshape(1, d))
  return dz, dzb, dg[0], dbias[0]


def _in_proj_bwd_ln(dh, w_in, dz1, z2, g, name, tm=512):
  m, wd = dh.shape
  d = w_in.shape[1]
  tm = _tile(m, tm)

  def body(dh_ref, w_ref, dz1_ref, z_ref, g_ref, dz_ref, dzb_ref, dg_ref, dbias_ref):
    dy = DN_ALPHA * dz1_ref[...] + _dot(dh_ref[...], w_ref[...])
    _ln_bwd_tail(dy, z_ref, g_ref, dz_ref, dzb_ref, dg_ref, dbias_ref)

  row = pl.BlockSpec((tm, d), lambda i: (i, 0))
  vec = pl.BlockSpec((1, d), lambda i: (0, 0))
  out_specs, out_shape = _ln_bwd_outs(m, d, row, vec)
  dz, dzb, dg, dbias = pl.pallas_call(
      body, name=name, grid=(m // tm,),
      in_specs=[pl.BlockSpec((tm, wd), lambda i: (i, 0)),
                pl.BlockSpec((wd, d), lambda i: (0, 0), pipeline_mode=pl.Buffered(1)), row, row, vec],
      out_specs=out_specs, out_shape=out_shape, compiler_params=_params(("arbitrary",)),
  )(dh, w_in, dz1, z2, g.reshape(1, d))
  return dz, dzb, dg[0], dbias[0]


def _alibi_slopes():
  n = N_DIL_HEADS
  return jnp.exp2(-8.0 * (jnp.arange(n, dtype=F32) + 1.0) / n).reshape(1, n)


def _rows(start, d):
  if d == 1:
    return pl.ds(pl.multiple_of(start, BLOCK), BLOCK)
  return pl.ds(start, BLOCK, stride=d)


def _fill_bias_tables(bias_sc, slope0, slope1):
  row = lax.broadcasted_iota(jnp.int32, (2 * BLOCK, 2 * BLOCK), 0)
  col = lax.broadcasted_iota(jnp.int32, (2 * BLOCK, 2 * BLOCK), 1)
  qi = jnp.bitwise_and(row, BLOCK - 1)
  ki = jnp.bitwise_and(col, BLOCK - 1)
  is_cur = col >= BLOCK
  steps = jnp.where(is_cur, qi - ki, qi + BLOCK - ki)
  valid = jnp.logical_and(steps >= 0, steps <= BLOCK)
  slope = jnp.where(row >= BLOCK, slope1, slope0)
  dist = slope * steps.astype(F32)
  for p, (_, d) in enumerate(DIL_PATTERNS):
    base = jnp.where(valid, -d * dist, NEG_BIG)
    bias_sc[2 * p] = base
    bias_sc[2 * p + 1] = jnp.where(is_cur, base, NEG_BIG)


def _stack_heads(v2, head0):
  return jnp.concatenate([jnp.where(head0, v2, 0.0), jnp.where(head0, 0.0, v2)], axis=0)


def _unstack_heads(v, head0):
  return jnp.where(head0, v[:BLOCK], v[BLOCK:])


def _block_rows(idx, d, nblk):
  r = idx // nblk
  n = idx % nblk
  cur = _rows(r + n * (BLOCK * d), d)
  prev = _rows(r + jnp.maximum(n - 1, 0) * (BLOCK * d), d)
  return cur, prev, n


def pair_tile(dt):
  return pltpu.VMEM((2 * BLOCK, 2 * BLOCK), dt)


def _two_stage_loop(nb, first_stage, second_stage, bufs):
  a, b, c, d = bufs
  assert nb % 4 == 0 and nb >= 8

  def quad(u, carry):
    i = 4 * u
    first_stage(i + 2, c)
    first_stage(i + 3, d)
    second_stage(i, a)
    second_stage(i + 1, b)
    first_stage(i + 4, a)
    first_stage(i + 5, b)
    second_stage(i + 2, c)
    second_stage(i + 3, d)
    return carry

  first_stage(0, a)
  first_stage(1, b)
  lax.fori_loop(0, nb // 4 - 1, quad, 0)
  i = nb - 4
  first_stage(i + 2, c)
  first_stage(i + 3, d)
  for k, buf in enumerate(bufs):
    second_stage(i + k, buf)


def _attn_fwd(h3, name, gather=None):
  bl, s, _ = h3.shape
  npair = N_DIL_HEADS // 2
  nb = s // BLOCK
  hosted = gather is not None
  steps = bl * npair

  def body(*refs):
    if hosted:
      sl_ref, q_ref, k_ref, v_ref, w_ref, o_ref, lse_ref, g_ref, o_sc, l_sc, bias_sc, *s_bufs, send_sems, recv_sems = refs
      step = pl.program_id(0) * npair + pl.program_id(1)
      for phase, at in enumerate((0, (3 * steps) // 4)):
        @pl.when(step == at)
        def _(phase=phase):
          _relayed_gather_phase(phase, w_ref, g_ref, send_sems, recv_sems)
    else:
      sl_ref, q_ref, k_ref, v_ref, o_ref, lse_ref, o_sc, l_sc, bias_sc, *s_bufs = refs
    hp = pl.program_id(1)
    head0 = lax.broadcasted_iota(jnp.int32, (BLOCK, LANES), 1) < 64
    _fill_bias_tables(bias_sc, sl_ref[0, 2 * hp], sl_ref[0, 2 * hp + 1])

    for p, (_, d) in enumerate(DIL_PATTERNS):
      nblk = (s // d) // BLOCK
      two = nblk > 1
      ks = slice(0, 2 * BLOCK) if two else slice(BLOCK, 2 * BLOCK)

      def scores(idx, buf, p=p, d=d, nblk=nblk, two=two, ks=ks):
        cur, prev, n = _block_rows(idx, d, nblk)
        qs = _bf(_stack_heads(q_ref[cur, :], head0) * ATT_SCALE)
        kb = _bf(jnp.concatenate([k_ref[prev, :], k_ref[cur, :]], axis=0)) if two else _bf(k_ref[cur, :])
        first = jnp.where(n == 0, 1, 0) if two else 0
        buf[:, ks] = _dot_nt(qs, kb) + bias_sc[2 * p + first, :, ks]

      def values(idx, buf, p=p, d=d, nblk=nblk, two=two, ks=ks):
        cur, prev, _ = _block_rows(idx, d, nblk)
        sc = buf[:, ks]
        mx = jnp.max(sc, axis=1, keepdims=True)
        pe = jnp.exp(sc - mx)
        den = jnp.sum(pe, axis=1, keepdims=True)
        vb = _bf(jnp.concatenate([v_ref[prev, :], v_ref[cur, :]], axis=0)) if two else _bf(v_ref[cur, :])
        acc = _dot(_bf(pe), vb) / den
        o_sc[p, cur, :] = _unstack_heads(acc, head0)
        l_sc[p, cur, :] = _unstack_heads(jnp.broadcast_to(mx + jnp.log(den), (2 * BLOCK, LANES)), head0)

      _two_stage_loop(nb, scores, values, s_bufs)

    def merge(i, carry):
      rows = pl.ds(pl.multiple_of(i * BLOCK, BLOCK), BLOCK)
      l0, l1, l2 = l_sc[0, rows, :], l_sc[1, rows, :], l_sc[2, rows, :]
      mx = jnp.maximum(jnp.maximum(l0, l1), l2)
      e0, e1, e2 = jnp.exp(l0 - mx), jnp.exp(l1 - mx), jnp.exp(l2 - mx)
      tot = e0 + e1 + e2
      o_ref[rows, :] = _bf((e0 * o_sc[0, rows, :] + e1 * o_sc[1, rows, :] + e2 * o_sc[2, rows, :]) / tot)
      lse_ref[rows, :] = mx + jnp.log(tot)
      return carry

    lax.fori_loop(0, nb, merge, 0)

    if hosted:
      @pl.when(step == steps - 1)
      def _():
        _relayed_gather_phase(2, w_ref, g_ref, send_sems, recv_sems)

  def col(off):
    return pl.BlockSpec((None, s, LANES), lambda b, p: (b, 0, off + p))

  in_specs = [pl.BlockSpec(memory_space=pltpu.SMEM), col(0), col(npair), col(2 * npair)]
  out_specs = [col(0), col(0)]
  out_shape = [jax.ShapeDtypeStruct((bl, s, DIL_WIDTH), BF16), jax.ShapeDtypeStruct((bl, s, DIL_WIDTH), F32)]
  scratch = [pltpu.VMEM((3, s, LANES), F32), pltpu.VMEM((3, s, LANES), F32),
             pltpu.VMEM((6, 2 * BLOCK, 2 * BLOCK), F32)] + [pair_tile(F32)] * 4
  args = (_alibi_slopes(), h3, h3, h3)
  if hosted:
    assert (gather.shape[1] // 2) % ROW_ALIGN == 0 and steps >= 4
    in_specs.append(ANY)
    out_specs.append(ANY)
    out_shape.append(jax.ShapeDtypeStruct((4,) + gather.shape, gather.dtype))
    scratch += [pltpu.SemaphoreType.DMA((N_RELAY_COPIES,)), pltpu.SemaphoreType.DMA((N_RELAY_COPIES,))]
    args += (gather,)
  sem = ("arbitrary", "arbitrary") if hosted else ("parallel", "parallel")
  outs = list(pl.pallas_call(
      body, name=name, grid=(bl, npair), in_specs=in_specs, out_specs=out_specs, out_shape=out_shape,
      scratch_shapes=scratch, compiler_params=_params(sem),
  )(*args))
  if hosted:
    outs[2] = _place_own_block(outs[2], gather)
  return outs


def _attn_bwd(h3, out3, lse3, dcat3, name, exchange=None):
  bl, s, _ = h3.shape
  npair = N_DIL_HEADS // 2
  nb = s // BLOCK
  hosted = exchange is not None

  def body(*refs):
    if hosted:
      (sl_ref, q_ref, k_ref, v_ref, o_ref, l_ref, do_ref, ex_ref, dq_out, dk_out, dv_out, got_ref,
       bias_sc, *pd, prod_sc, dq_ref, dk_ref, dv_ref, send_sems, recv_sems) = refs
      step = pl.program_id(0) * npair + pl.program_id(1)

      @pl.when(step == 0)
      def _():
        for cp in _chip_exchange_copies(ex_ref, got_ref, send_sems, recv_sems):
          cp.start()
    else:
      (sl_ref, q_ref, k_ref, v_ref, o_ref, l_ref, do_ref, dq_out, dk_out, dv_out,
       bias_sc, *pd, prod_sc, dq_ref, dk_ref, dv_ref) = refs
    pd_bufs = list(zip(pd[0::2], pd[1::2]))
    hp = pl.program_id(1)
    lane = lax.broadcasted_iota(jnp.int32, (BLOCK, LANES), 1)
    head0 = lane < 64
    _fill_bias_tables(bias_sc, sl_ref[0, 2 * hp], sl_ref[0, 2 * hp + 1])
    dq_ref[...] = jnp.zeros_like(dq_ref)
    dk_ref[...] = jnp.zeros_like(dk_ref)
    dv_ref[...] = jnp.zeros_like(dv_ref)
    prod_sc[...] = do_ref[...] * o_ref[...].astype(F32)

    def per_row(v2, pick0, pick1):
      return jnp.concatenate([jnp.sum(jnp.where(pick0, v2, 0.0), axis=1, keepdims=True),
                              jnp.sum(jnp.where(pick1, v2, 0.0), axis=1, keepdims=True)], axis=0)

    for p, (_, d) in enumerate(DIL_PATTERNS):
      nblk = (s // d) // BLOCK
      two = nblk > 1
      ks = slice(0, 2 * BLOCK) if two else slice(BLOCK, 2 * BLOCK)

      def operands(idx, d=d, nblk=nblk, two=two):
        cur, prev, n = _block_rows(idx, d, nblk)
        qs = _bf(_stack_heads(q_ref[cur, :], head0) * ATT_SCALE)
        dos = _bf(_stack_heads(do_ref[cur, :], head0))
        kb = _bf(jnp.concatenate([k_ref[prev, :], k_ref[cur, :]], axis=0)) if two else _bf(k_ref[cur, :])
        return cur, prev, n, qs, dos, kb

      def probs(idx, bufs, p=p, two=two, ks=ks, operands=operands):
        cur, prev, n, qs, dos, kb = operands(idx)
        vb = _bf(jnp.concatenate([v_ref[prev, :], v_ref[cur, :]], axis=0)) if two else _bf(v_ref[cur, :])
        lse = per_row(l_ref[cur, :], lane == 0, lane == 64)
        delta = per_row(prod_sc[cur, :], head0, jnp.logical_not(head0))
        first = jnp.where(n == 0, 1, 0) if two else 0
        pr = jnp.exp(_dot_nt(qs, kb) + bias_sc[2 * p + first, :, ks] - lse)
        bufs[0][:, ks] = _bf(pr)
        bufs[1][:, ks] = _bf(pr * (_dot_nt(dos, vb) - delta))

      def products(idx, bufs, two=two, ks=ks, operands=operands):
        cur, prev, _, qs, dos, kb = operands(idx)
        pr = bufs[0][:, ks]
        ds = bufs[1][:, ks]
        dq_ref[cur, :] += _unstack_heads(_dot(ds, kb), head0) * ATT_SCALE
        dkb = _dot_tn(ds, qs)
        dvb = _dot_tn(pr, dos)
        if two:
          dk_ref[prev, :] += dkb[:BLOCK]
          dv_ref[prev, :] += dvb[:BLOCK]
          dk_ref[cur, :] += dkb[BLOCK:]
          dv_ref[cur, :] += dvb[BLOCK:]
        else:
          dk_ref[cur, :] += dkb
          dv_ref[cur, :] += dvb

      _two_stage_loop(nb, probs, products, pd_bufs)

    dq_out[...] = _bf(dq_ref[...])
    dk_out[...] = _bf(dk_ref[...])
    dv_out[...] = _bf(dv_ref[...])

    if hosted:
      @pl.when(step == bl * npair - 1)
      def _():
        for cp in _chip_exchange_copies(ex_ref, got_ref, send_sems, recv_sems):
          cp.wait()

  def col(off):
    return pl.BlockSpec((None, s, LANES), lambda b, p: (b, 0, off + p))

  sds = jax.ShapeDtypeStruct((bl, s, DIL_WIDTH), BF16)
  in_specs = [pl.BlockSpec(memory_space=pltpu.SMEM), col(0), col(npair), col(2 * npair), col(0), col(0), col(0)]
  out_specs, out_shape = [col(0), col(0), col(0)], [sds, sds, sds]
  scratch = [pltpu.VMEM((6, 2 * BLOCK, 2 * BLOCK), F32)] + [pair_tile(BF16)] * 8 + [pltpu.VMEM((s, LANES), F32)] * 4
  args = (_alibi_slopes(), h3, h3, h3, out3, lse3, dcat3)
  if hosted:
    in_specs.append(ANY)
    out_specs.append(ANY)
    out_shape.append(jax.ShapeDtypeStruct((3,) + exchange.shape[1:], exchange.dtype))
    scratch += [pltpu.SemaphoreType.DMA((3,)), pltpu.SemaphoreType.DMA((3,))]
    args += (exchange,)
  sem = ("arbitrary", "arbitrary") if hosted else ("parallel", "parallel")
  return pl.pallas_call(
      body, name=name, grid=(bl, npair), in_specs=in_specs, out_specs=out_specs, out_shape=out_shape,
      scratch_shapes=scratch, compiler_params=_params(sem),
  )(*args)


def _mem_heads(tq):
  lane = lax.broadcasted_iota(jnp.int32, (tq, LANES), 1)
  return lane < 64


def _mem_fwd(h3, qcol, mkv3, name, tq=512):
  bl, s, _ = h3.shape
  nm = mkv3.shape[1]
  tq = _tile(s, tq)

  def body(q_ref, kv_ref, o_ref):
    head0 = _mem_heads(tq)
    for lg in range(MEM_WIDTH // LANES):
      cs = slice(lg * LANES, (lg + 1) * LANES)
      q2 = q_ref[:, cs]
      mk = _bf(kv_ref[:, cs])
      mv = _bf(kv_ref[:, MEM_WIDTH + lg * LANES:MEM_WIDTH + (lg + 1) * LANES])
      outs = []
      for j in range(2):
        hm = head0 if j == 0 else jnp.logical_not(head0)
        qj = _bf(jnp.where(hm, q2, 0.0) * ATT_SCALE)
        sc = _dot_nt(qj, mk)
        mx = jnp.max(sc, axis=1, keepdims=True)
        pe = jnp.exp(sc - mx)
        den = jnp.sum(pe, axis=1, keepdims=True)
        outs.append(_dot(_bf(pe / den), mv))
      o_ref[:, cs] = _bf(jnp.where(head0, outs[0], outs[1]))

  return pl.pallas_call(
      body, name=name, grid=(bl, s // tq),
      in_specs=[pl.BlockSpec((None, tq, MEM_WIDTH), lambda b, i: (b, i, qcol)),
                pl.BlockSpec((None, nm, 2 * MEM_WIDTH), lambda b, i: (b, 0, 0))],
      out_specs=pl.BlockSpec((None, tq, MEM_WIDTH), lambda b, i: (b, i, 0)),
      out_shape=jax.ShapeDtypeStruct((bl, s, MEM_WIDTH), BF16),
      compiler_params=_params(("parallel", "parallel")),
  )(h3, mkv3)


def _mem_bwd(h3, qcol, mkv3, dcat3, name, tq=512):
  bl, s, _ = h3.shape
  nm = mkv3.shape[1]
  tq = _tile(s, tq)
  docol = dcat3.shape[2] // MEM_WIDTH - 1

  def body(q_ref, kv_ref, do_ref, dq_ref, dkv_ref):
    i = pl.program_id(1)

    @pl.when(i == 0)
    def _():
      dkv_ref[...] = jnp.zeros_like(dkv_ref)

    head0 = _mem_heads(tq)
    for lg in range(MEM_WIDTH // LANES):
      cs = slice(lg * LANES, (lg + 1) * LANES)
      vs = slice(MEM_WIDTH + lg * LANES, MEM_WIDTH + (lg + 1) * LANES)
      q2 = q_ref[:, cs]
      do2 = do_ref[:, cs]
      mk = _bf(kv_ref[:, cs])
      mv = _bf(kv_ref[:, vs])
      dq2 = jnp.zeros((tq, LANES), F32)
      dmk = jnp.zeros((nm, LANES), F32)
      dmv = jnp.zeros((nm, LANES), F32)
      for j in range(2):
        hm = head0 if j == 0 else jnp.logical_not(head0)
        qj = _bf(jnp.where(hm, q2, 0.0) * ATT_SCALE)
        doj = _bf(jnp.where(hm, do2, 0.0))
        sc = _dot_nt(qj, mk)
        mx = jnp.max(sc, axis=1, keepdims=True)
        pe = jnp.exp(sc - mx)
        pn = pe / jnp.sum(pe, axis=1, keepdims=True)
        pb = _bf(pn)
        dp = _dot_nt(doj, mv)
        dj = jnp.sum(pb.astype(F32) * dp, axis=1, keepdims=True)
        ds = _bf(pn * (dp - dj))
        dq2 = dq2 + jnp.where(hm, _dot(ds, mk), 0.0) * ATT_SCALE
        dmk = dmk + _dot_tn(ds, qj)
        dmv = dmv + _dot_tn(pb, doj)
      dq_ref[:, cs] = _bf(dq2)
      dkv_ref[:, cs] += dmk
      dkv_ref[:, vs] += dmv

  return pl.pallas_call(
      body, name=name, grid=(bl, s // tq),
      in_specs=[pl.BlockSpec((None, tq, MEM_WIDTH), lambda b, i: (b, i, qcol)),
                pl.BlockSpec((None, nm, 2 * MEM_WIDTH), lambda b, i: (b, 0, 0)),
                pl.BlockSpec((None, tq, MEM_WIDTH), lambda b, i: (b, i, docol))],
      out_specs=[pl.BlockSpec((None, tq, MEM_WIDTH), lambda b, i: (b, i, 0)),
                 pl.BlockSpec((None, nm, 2 * MEM_WIDTH), lambda b, i: (b, 0, 0))],
      out_shape=[jax.ShapeDtypeStruct((bl, s, MEM_WIDTH), BF16), jax.ShapeDtypeStruct((bl, nm, 2 * MEM_WIDTH), F32)],
      compiler_params=_params(("parallel", "arbitrary")),
  )(h3, mkv3, dcat3)


def _sgu_consts():
  ti = lax.broadcasted_iota(jnp.int32, (CHUNK, CHUNK), 0)
  si = lax.broadcasted_iota(jnp.int32, (CHUNK, CHUNK), 1)
  return si <= ti, si < 64


def _sgu_bias_lanes(b_s):
  return jnp.repeat(b_s.T, 64, axis=1)


def _sgu_fwd(h2, ln_g, ln_b, w_s, b_s, name, tr=512):
  t, _ = h2.shape
  tr = _tile(t, tr)
  nch = tr // CHUNK
  npair = N_SGU_GROUPS // 2

  def body(u_ref, v_ref, g_ref, b_ref, w_ref, bs_ref, o_ref, vn_sc):
    tril, head0 = _sgu_consts()
    xhat, _ = _ln_stats(_gelu(v_ref[...]))
    vn_sc[...] = _bf(xhat * g_ref[...] + b_ref[...])
    for jp in range(npair):
      cs = slice(jp * LANES, (jp + 1) * LANES)
      w0 = _bf(jnp.where(tril, w_ref[2 * jp], 0.0))
      w1 = _bf(jnp.where(tril, w_ref[2 * jp + 1], 0.0))
      bias = bs_ref[:, cs]
      for c in range(nch):
        rs = slice(c * CHUNK, (c + 1) * CHUNK)
        vb = vn_sc[rs, cs]
        mixed = jnp.where(head0, _dot(w0, vb), _dot(w1, vb)) + bias
        o_ref[rs, cs] = _bf(_gelu(u_ref[rs, cs]) * mixed)

  blk = lambda j: pl.BlockSpec((tr, SGU_WIDTH), lambda i: (i, j))
  vec = pl.BlockSpec((1, SGU_WIDTH), lambda i: (0, 0))
  return pl.pallas_call(
      body, name=name, grid=(t // tr,),
      in_specs=[blk(0), blk(1), vec, vec,
                pl.BlockSpec((N_SGU_GROUPS, CHUNK, CHUNK), lambda i: (0, 0, 0)),
                pl.BlockSpec((CHUNK, SGU_WIDTH), lambda i: (0, 0))],
      out_specs=blk(0), out_shape=jax.ShapeDtypeStruct((t, SGU_WIDTH), BF16),
      scratch_shapes=[pltpu.VMEM((tr, SGU_WIDTH), BF16)],
      compiler_params=_params(("parallel",)),
  )(h2, h2, ln_g.reshape(1, -1), ln_b.reshape(1, -1), w_s, _sgu_bias_lanes(b_s))


def _sgu_bwd(h2, dcat, ln_g, ln_b, w_s, b_s, name, tr=512):
  t, _ = h2.shape
  tr = _tile(t, tr)
  nch = tr // CHUNK
  npair = N_SGU_GROUPS // 2
  nsteps = t // tr

  def body(u_ref, v_ref, dm_ref, g_ref, b_ref, w_ref, bs_ref,
           du_ref, dv_ref, dw_ref, dbs_ref, dg_ref, db_ref, vn_sc, dmx_sc, dvn_sc, mix_sc, dbx_sc):
    i = pl.program_id(0)
    tril, head0 = _sgu_consts()

    @pl.when(i == 0)
    def _():
      dw_ref[...] = jnp.zeros_like(dw_ref)
      dg_ref[...] = jnp.zeros_like(dg_ref)
      db_ref[...] = jnp.zeros_like(db_ref)
      dbx_sc[...] = jnp.zeros_like(dbx_sc)

    gv, gv_der = _gelu_parts(v_ref[...])
    xhat, rstd = _ln_stats(gv)
    g = g_ref[...]
    vn_sc[...] = _bf(xhat * g + b_ref[...])
    gu, gu_der = _gelu_parts(u_ref[...])
    dmix = dm_ref[...]
    dmx_sc[...] = dmix * gu

    for jp in range(npair):
      cs = slice(jp * LANES, (jp + 1) * LANES)
      w0 = _bf(jnp.where(tril, w_ref[2 * jp], 0.0))
      w1 = _bf(jnp.where(tril, w_ref[2 * jp + 1], 0.0))
      bias = bs_ref[:, cs]
      dw0 = jnp.zeros((CHUNK, CHUNK), F32)
      dw1 = jnp.zeros((CHUNK, CHUNK), F32)
      dbx = jnp.zeros((CHUNK, LANES), F32)
      for c in range(nch):
        rs = slice(c * CHUNK, (c + 1) * CHUNK)
        vb = vn_sc[rs, cs]
        mix_sc[rs, cs] = jnp.where(head0, _dot(w0, vb), _dot(w1, vb)) + bias
        dmx = dmx_sc[rs, cs]
        d0 = _bf(jnp.where(head0, dmx, 0.0))
        d1 = _bf(jnp.where(head0, 0.0, dmx))
        dvn_sc[rs, cs] = _dot_tn(w0, d0) + _dot_tn(w1, d1)
        dw0 = dw0 + _dot_nt(d0, vb)
        dw1 = dw1 + _dot_nt(d1, vb)
        dbx = dbx + dmx
      dw_ref[2 * jp] += dw0
      dw_ref[2 * jp + 1] += dw1
      dbx_sc[:, cs] += dbx

    du_ref[...] = _bf(dmix * mix_sc[...] * gu_der)
    dvn = dvn_sc[...]
    dv_ref[...] = _bf(_ln_bwd(dvn, xhat, rstd, g) * gv_der)
    dg_ref[...] += jnp.sum(dvn * xhat, axis=0, keepdims=True)
    db_ref[...] += jnp.sum(dvn, axis=0, keepdims=True)

    @pl.when(i == nsteps - 1)
    def _():
      lane = lax.broadcasted_iota(jnp.int32, (CHUNK, LANES), 1)
      acc = jnp.zeros((CHUNK, LANES), F32)
      for gi in range(N_SGU_GROUPS):
        jp, j = gi // 2, gi % 2
        part = dbx_sc[:, jp * LANES:(jp + 1) * LANES]
        hm = (lane < 64) if j == 0 else (lane >= 64)
        colsum = jnp.sum(jnp.where(hm, part, 0.0), axis=1, keepdims=True)
        acc = jnp.where(lane == gi, colsum, acc)
        dw_ref[gi] = jnp.where(tril, dw_ref[gi], 0.0)
      dbs_ref[...] = acc

  blk = lambda j: pl.BlockSpec((tr, SGU_WIDTH), lambda i: (i, j))
  vec = pl.BlockSpec((1, SGU_WIDTH), lambda i: (0, 0))
  wspec = pl.BlockSpec((N_SGU_GROUPS, CHUNK, CHUNK), lambda i: (0, 0, 0))
  big = lambda dt: pltpu.VMEM((tr, SGU_WIDTH), dt)
  du, dv, dw, dbs, dg, db = pl.pallas_call(
      body, name=name, grid=(nsteps,),
      in_specs=[blk(0), blk(1), blk(0), vec, vec, wspec, pl.BlockSpec((CHUNK, SGU_WIDTH), lambda i: (0, 0))],
      out_specs=[blk(0), blk(0), wspec, pl.BlockSpec((CHUNK, LANES), lambda i: (0, 0)), vec, vec],
      out_shape=[jax.ShapeDtypeStruct((t, SGU_WIDTH), BF16), jax.ShapeDtypeStruct((t, SGU_WIDTH), BF16),
                 jax.ShapeDtypeStruct((N_SGU_GROUPS, CHUNK, CHUNK), F32), jax.ShapeDtypeStruct((CHUNK, LANES), F32),
                 jax.ShapeDtypeStruct((1, SGU_WIDTH), F32), jax.ShapeDtypeStruct((1, SGU_WIDTH), F32)],
      scratch_shapes=[big(BF16), big(F32), big(F32), big(F32), pltpu.VMEM((CHUNK, SGU_WIDTH), F32)],
      compiler_params=_params(("arbitrary",)),
  )(h2, h2, dcat, ln_g.reshape(1, -1), ln_b.reshape(1, -1), w_s, _sgu_bias_lanes(b_s))
  return du, dv, dw, dbs[:, :N_SGU_GROUPS].T, dg[0], db[0]


def _loss_head(xo, tgt, z, g, name, tm=512):
  m, d = xo.shape
  tm = _tile(m, tm)

  def body(x_ref, t_ref, z_ref, g_ref, l_ref, dz_ref, dzb_ref, dg_ref, dbias_ref):
    @pl.when(pl.program_id(0) == 0)
    def _():
      l_ref[...] = jnp.zeros_like(l_ref)

    diff = x_ref[...] - t_ref[...]
    rowsum = jnp.sum(diff * diff, axis=1, keepdims=True)
    tot = jnp.sum(rowsum, axis=0, keepdims=True) * (0.5 / d)
    l_ref[...] += jnp.broadcast_to(tot, l_ref.shape)
    _ln_bwd_tail(diff * (1.0 / d), z_ref, g_ref, dz_ref, dzb_ref, dg_ref, dbias_ref)

  row = pl.BlockSpec((tm, d), lambda i: (i, 0))
  vec = pl.BlockSpec((1, d), lambda i: (0, 0))
  out_specs, out_shape = _ln_bwd_outs(m, d, row, vec)
  l, dz, dzb, dg, dbias = pl.pallas_call(
      body, name=name, grid=(m // tm,), in_specs=[row, row, row, vec],
      out_specs=[pl.BlockSpec((8, LANES), lambda i: (0, 0))] + out_specs,
      out_shape=[jax.ShapeDtypeStruct((8, LANES), F32)] + out_shape,
      compiler_params=_params(("arbitrary",)),
  )(xo, tgt, z, g.reshape(1, d))
  return l[0, 0], dz, dzb, dg[0], dbias[0]


def _local_step(x3, mem3, tgt3, w, late_weights=None, early_exchange=None):
  w = dict(w)
  bl, s, d = x3.shape
  t = bl * s
  nm = mem3.shape[1]
  mem2 = mem3.reshape(bl * nm, d)
  x = x3.reshape(t, d)
  xb = x
  saved = []
  for i in range(DEPTH):
    j = i // 2
    attn = i % 2 == 0
    mkv = _mm(mem2, w["w_mem_kv"][i], "nn", F32, f"mkv_fwd_{i}", tm=1024, tn=512, tk=1024)
    mkv3 = mkv.reshape(bl, nm, 2 * MEM_WIDTH)
    w_in = w["a_w_in"][j] if attn else w["b_w_in"][j]
    h = _mm(xb, w_in, "nt", F32, f"in_proj_{i}", tm=512, tn=w_in.shape[0], tk=d)
    h3 = h.reshape(bl, s, -1)
    if attn and late_weights is not None and i in late_weights:
      mix3, lse3, gathered = _attn_fwd(h3, f"dil_attn_fwd_{i}", gather=late_weights[i].flat)
      for n, layers in late_weights[i].unpack(gathered).items():
        w[n] = {**w.get(n, {}), **layers}
    elif attn:
      mix3, lse3 = _attn_fwd(h3, f"dil_attn_fwd_{i}")
    if attn:
      mix = mix3.reshape(t, DIL_WIDTH)
      qcol = 3 * DIL_WIDTH // MEM_WIDTH
    else:
      mix = _sgu_fwd(h, w["sgu_ln_g"][j], w["sgu_ln_b"][j], w["sgu_w_s"][j], w["sgu_b_s"][j], f"sgu_fwd_{i}")
      lse3 = None
      qcol = 2 * SGU_WIDTH // MEM_WIDTH
    mo = _mem_fwd(h3, qcol, mkv3, f"mem_attn_fwd_{i}").reshape(t, MEM_WIDTH)
    cat = jnp.concatenate([mix, mo], axis=1)
    z1, xm, xmb = _mm_res_ln(cat, w["w_out"][i], x, w["ln_mix_g"][i], w["ln_mix_b"][i], f"out_proj_ln_{i}", tk=1024)
    ga, gb, hm = _ffn_up(xmb, w["w_gate"][i], w["w_up"][i], f"ffn_up_{i}")
    z2, xo, xob = _mm_res_ln(hm, w["w_down"][i], xm, w["ln_ffn_g"][i], w["ln_ffn_b"][i], f"ffn_down_ln_{i}", tk=hm.shape[1])
    saved.append(dict(xb=xb, h=h, h3=h3, mkv3=mkv3, mix3=(mix3 if attn else None), lse3=lse3, cat=cat, z1=z1,
                      xmb=xmb, ga=ga, gb=gb, hm=hm, z2=z2, qcol=qcol))
    x, xb = xo, xob

  names = ("a_w_in", "b_w_in", "sgu_ln_g", "sgu_ln_b", "sgu_w_s", "sgu_b_s", "w_mem_kv", "w_out",
           "ln_mix_g", "ln_mix_b", "w_gate", "w_up", "w_down", "ln_ffn_g", "ln_ffn_b")
  grads = {n: [None] * len(w[n]) for n in names}
  last = DEPTH - 1
  loss, dz2, dz2b, grads["ln_ffn_g"][last], grads["ln_ffn_b"][last] = _loss_head(
      x, tgt3.reshape(t, d), saved[last]["z2"], w["ln_ffn_g"][last], "loss_head")
  dx = None
  for i in reversed(range(DEPTH)):
    j = i // 2
    attn = i % 2 == 0
    sv = saved[i]
    da, db = _ffn_bwd_hidden(dz2b, w["w_down"][i], sv["ga"], sv["gb"], f"ffn_bwd_hidden_{i}")
    grads["w_down"][i] = _mm(sv["hm"], dz2b, "tn", F32, f"dw_down_{i}", tm=1408, tn=1024, tk=1024)
    grads["w_gate"][i] = _mm(da, sv["xmb"], "tn", F32, f"dw_gate_{i}", tm=1408, tn=1024, tk=1024)
    grads["w_up"][i] = _mm(db, sv["xmb"], "tn", F32, f"dw_up_{i}", tm=1408, tn=1024, tk=1024)
    dz1, dz1b, grads["ln_mix_g"][i], grads["ln_mix_b"][i] = _ffn_bwd_input_ln(
        da, db, w["w_gate"][i], w["w_up"][i], dz2, sv["z1"], w["ln_mix_g"][i], f"ffn_bwd_input_ln_{i}")
    grads["w_out"][i] = _mm(sv["cat"], dz1b, "tn", F32, f"dw_out_{i}", tm=1024, tn=1024, tk=1024)
    dcat = _mm(dz1b, w["w_out"][i], "nt", F32, f"out_proj_bwd_{i}", tm=1024, tn=1024, tk=1024)
    dcat3 = dcat.reshape(bl, s, -1)
    dqm3, dmkv3 = _mem_bwd(sv["h3"], sv["qcol"], sv["mkv3"], dcat3, f"mem_attn_bwd_{i}")
    grads["w_mem_kv"][i] = _mm(mem2, dmkv3.reshape(bl * nm, 2 * MEM_WIDTH), "tn", F32, f"dw_mem_kv_{i}", tm=1024, tn=512, tk=1024)
    dqm = dqm3.reshape(t, MEM_WIDTH)
    if attn and i == 0 and early_exchange is not None:
      q, (pack, state) = early_exchange(grads)
      dq3, dk3, dv3, x3 = _attn_bwd(sv["h3"], sv["mix3"], sv["lse3"], dcat3, f"dil_attn_bwd_{i}", exchange=q)
      grads["early_exchange"] = (pack, state, x3)
      parts = [dq3.reshape(t, -1), dk3.reshape(t, -1), dv3.reshape(t, -1), dqm]
    elif attn:
      dq3, dk3, dv3 = _attn_bwd(sv["h3"], sv["mix3"], sv["lse3"], dcat3, f"dil_attn_bwd_{i}")
      parts = [dq3.reshape(t, -1), dk3.reshape(t, -1), dv3.reshape(t, -1), dqm]
    else:
      du, dv, dws, dbs, dlg, dlb = _sgu_bwd(sv["h"], dcat, w["sgu_ln_g"][j], w["sgu_ln_b"][j], w["sgu_w_s"][j],
                                             w["sgu_b_s"][j], f"sgu_bwd_{i}")
      grads["sgu_w_s"][j], grads["sgu_b_s"][j], grads["sgu_ln_g"][j], grads["sgu_ln_b"][j] = dws, dbs, dlg, dlb
      parts = [du, dv, dqm]
    dh = jnp.concatenate(parts, axis=1)
    w_in = w["a_w_in"][j] if attn else w["b_w_in"][j]
    grads["a_w_in" if attn else "b_w_in"][j] = _mm(dh, sv["xb"], "tn", F32, f"dw_in_{i}", tm=1280 if attn else 896, tn=1024, tk=1024)
    if i > 0:
      dz2, dz2b, grads["ln_ffn_g"][i - 1], grads["ln_ffn_b"][i - 1] = _in_proj_bwd_ln(
          dh, w_in, dz1, saved[i - 1]["z2"], w["ln_ffn_g"][i - 1], f"in_proj_bwd_ln_{i}")
    else:
      dx = _mm(dh, w_in, "nn", F32, f"in_proj_bwd_{i}", add=dz1, add_scale=DN_ALPHA, tm=512, tn=d, tk=w_in.shape[0])
  return loss, dx.reshape(bl, s, d), grads


def _my_place():
  return lax.axis_index("x"), lax.axis_index("y"), lax.axis_index("c")


def _other_chips(x, y):
  return [(1 - x, y), (x, 1 - y), (1 - x, 1 - y)]


ANY = pl.BlockSpec(memory_space=pl.ANY)


def _all_gather_halves(wl, name):
  _, r, c_ = wl.shape

  def body(w_ref, g_ref, send_sems, recv_sems):
    x, y, c = _my_place()
    me = 2 * x + y
    sibling = (x, y, 1 - c)
    chips = _other_chips(x, y)

    def copy(k, src, dst, to):
      return pltpu.make_async_remote_copy(src_ref=src, dst_ref=dst, send_sem=send_sems.at[k], recv_sem=recv_sems.at[k],
                                          device_id=to, device_id_type=MESH_ID)

    first = [copy(k, w_ref.at[c], g_ref.at[me, c], (px, py, c)) for k, (px, py) in enumerate(chips)]
    for cp in first:
      cp.start()
    passed = []
    for k, (px, py) in enumerate(chips):
      landed = g_ref.at[2 * px + py, c]
      copy(k, landed, landed, (px, py, c)).wait_recv()
      fwd = copy(3 + k, landed, landed, sibling)
      fwd.start()
      passed.append(fwd)
    for k, (px, py) in enumerate(chips):
      theirs = g_ref.at[2 * px + py, 1 - c]
      copy(3 + k, theirs, theirs, sibling).wait_recv()
    for cp in first + passed:
      cp.wait_send()

  got = pl.pallas_call(
      body, name=name, in_specs=[ANY], out_specs=ANY,
      out_shape=jax.ShapeDtypeStruct((4, 2, r, c_), wl.dtype),
      scratch_shapes=[pltpu.SemaphoreType.DMA((6,)), pltpu.SemaphoreType.DMA((6,))],
  )(wl)
  chip = 2 * lax.axis_index("x") + lax.axis_index("y")
  return lax.dynamic_update_slice(got, wl[None], (chip, 0, 0, 0))


def _relayed_gather_phase(phase, w_ref, g_ref, send_sems, recv_sems):
  h = w_ref.shape[1] // 2
  x, y, c = _my_place()
  sibling = (x, y, 1 - c)
  xn, yn, dg = _other_chips(x, y)

  def copy(k, src, dst, to):
    return pltpu.make_async_remote_copy(src_ref=src, dst_ref=dst, send_sem=send_sems.at[k], recv_sem=recv_sems.at[k],
                                        device_id=to, device_id_type=MESH_ID)

  def block(chip, half):
    return g_ref.at[2 * chip[0] + chip[1], half]

  def same(k, ref, to):
    return copy(k, ref, ref, to)

  top, bottom = pl.ds(0, h), pl.ds(h, h)
  sends = [copy(0, w_ref.at[c], block((x, y), c), (*xn, c)), copy(1, w_ref.at[c], block((x, y), c), (*yn, c)),
           same(2, block(xn, c).at[top], (*yn, c)), same(3, block(yn, c).at[bottom], (*xn, c)),
           same(4, block(xn, c), sibling), same(5, block(yn, c), sibling), same(6, block(dg, c), sibling)]
  if phase == 0:
    sends[0].start()
    sends[1].start()
  elif phase == 1:
    same(0, block(xn, c), (*xn, c)).wait_recv()
    sends[2].start()
    sends[4].start()
    same(1, block(yn, c), (*yn, c)).wait_recv()
    sends[3].start()
    sends[5].start()
  else:
    same(2, block(dg, c).at[top], (*yn, c)).wait_recv()
    same(3, block(dg, c).at[bottom], (*xn, c)).wait_recv()
    sends[6].start()
    for k, chip in ((4, xn), (5, yn), (6, dg)):
      same(k, block(chip, 1 - c), sibling).wait_recv()
    for cp in sends:
      cp.wait_send()


N_RELAY_COPIES = 7


def _place_own_block(got, wl):
  chip = 2 * lax.axis_index("x") + lax.axis_index("y")
  return lax.dynamic_update_slice(got, wl[None], (chip, 0, 0, 0))


def _all_gather_relayed(wl, name):
  _, r, c_ = wl.shape
  assert (r // 2) % ROW_ALIGN == 0

  def body(w_ref, g_ref, send_sems, recv_sems):
    for phase in range(3):
      _relayed_gather_phase(phase, w_ref, g_ref, send_sems, recv_sems)

  got = pl.pallas_call(
      body, name=name, in_specs=[ANY], out_specs=ANY,
      out_shape=jax.ShapeDtypeStruct((4, 2, r, c_), wl.dtype),
      scratch_shapes=[pltpu.SemaphoreType.DMA((N_RELAY_COPIES,)), pltpu.SemaphoreType.DMA((N_RELAY_COPIES,))],
  )(wl)
  return _place_own_block(got, wl)


def _sibling_swap(v, name):
  def body(v_ref, o_ref, send_sem, recv_sem):
    x, y, c = _my_place()
    cp = pltpu.make_async_remote_copy(src_ref=v_ref, dst_ref=o_ref, send_sem=send_sem, recv_sem=recv_sem,
                                      device_id=(x, y, 1 - c), device_id_type=MESH_ID)
    cp.start()
    cp.wait()

  return pl.pallas_call(
      body, name=name, in_specs=[ANY], out_specs=ANY, out_shape=jax.ShapeDtypeStruct(v.shape, v.dtype),
      scratch_shapes=[pltpu.SemaphoreType.DMA, pltpu.SemaphoreType.DMA],
  )(v)


def _chip_exchange_copies(q_ref, o_ref, send_sems, recv_sems):
  x, y, c = _my_place()
  return [pltpu.make_async_remote_copy(src_ref=q_ref.at[2 * px + py], dst_ref=o_ref.at[k], send_sem=send_sems.at[k],
                                       recv_sem=recv_sems.at[k], device_id=(px, py, c), device_id_type=MESH_ID)
          for k, (px, py) in enumerate(_other_chips(x, y))]


def _chip_exchange(q, name):
  _, r, c_ = q.shape

  def body(q_ref, o_ref, send_sems, recv_sems):
    cps = _chip_exchange_copies(q_ref, o_ref, send_sems, recv_sems)
    for cp in cps:
      cp.start()
    for cp in cps:
      cp.wait()

  return pl.pallas_call(
      body, name=name, in_specs=[ANY], out_specs=ANY, out_shape=jax.ShapeDtypeStruct((3, r, c_), q.dtype),
      scratch_shapes=[pltpu.SemaphoreType.DMA((3,)), pltpu.SemaphoreType.DMA((3,))],
  )(q)


def _share_halves(both, name):
  _, r, c_ = both.shape

  def body(b_ref, o_ref, send_sem, recv_sem):
    x, y, c = _my_place()
    cp = pltpu.make_async_remote_copy(src_ref=b_ref.at[c], dst_ref=o_ref.at[c], send_sem=send_sem, recv_sem=recv_sem,
                                      device_id=(x, y, 1 - c), device_id_type=MESH_ID)
    cp.start()
    cp.wait()

  full = pl.pallas_call(
      body, name=name, in_specs=[ANY], out_specs=ANY, out_shape=jax.ShapeDtypeStruct(both.shape, both.dtype),
      input_output_aliases={0: 0},
      scratch_shapes=[pltpu.SemaphoreType.DMA, pltpu.SemaphoreType.DMA],
  )(both)
  return full.reshape(2 * r, c_)


def _half_spec(tr, c_, pick):
  return pl.BlockSpec((None, None, tr, c_), lambda s, r, place: (s, pick(place), r, 0))


def _cast_other_half(p, place, name, tr=512):
  _, _, r, c_ = p.shape
  tr = _tile(r, tr, 16)

  def body(place_ref, p_ref, o_ref):
    o_ref[...] = _bf(p_ref[...])

  out_spec = pl.BlockSpec((None, tr, c_), lambda s, rr, place: (s, rr, 0))
  return pl.pallas_call(
      body, name=name, out_shape=jax.ShapeDtypeStruct((4, r, c_), BF16),
      grid_spec=pltpu.PrefetchScalarGridSpec(num_scalar_prefetch=1, grid=(4, r // tr),
                                             in_specs=[_half_spec(tr, c_, lambda place: 1 - place[1])], out_specs=out_spec),
      compiler_params=_params(("parallel", "parallel")),
  )(place, p)


def _add_sibling(p, x1, place, name, tr=512):
  _, _, r, c_ = p.shape
  tr = _tile(r, tr, 16)

  def body(place_ref, p_ref, x_ref, o_ref):
    o_ref[...] = _bf(p_ref[...] + x_ref[...].astype(F32))

  row = pl.BlockSpec((None, tr, c_), lambda s, rr, place: (s, rr, 0))
  return pl.pallas_call(
      body, name=name, out_shape=jax.ShapeDtypeStruct((4, r, c_), BF16),
      grid_spec=pltpu.PrefetchScalarGridSpec(num_scalar_prefetch=1, grid=(4, r // tr),
                                             in_specs=[_half_spec(tr, c_, lambda place: place[1]), row], out_specs=row),
      compiler_params=_params(("parallel", "parallel")),
  )(place, p, x1)


def _sum_own(p, x1, x3, place, name, tr=512):
  _, _, r, c_ = p.shape
  tr = _tile(r, tr, 16)

  def body(place_ref, p_ref, x1_ref, x3_ref, o_ref):
    acc = p_ref[...] + x1_ref[...].astype(F32)
    for k in range(3):
      acc = acc + x3_ref[k].astype(F32)
    o_ref[...] = acc

  return pl.pallas_call(
      body, name=name, out_shape=jax.ShapeDtypeStruct((2, r, c_), F32),
      grid_spec=pltpu.PrefetchScalarGridSpec(
          num_scalar_prefetch=1, grid=(r // tr,),
          in_specs=[pl.BlockSpec((None, None, tr, c_), lambda rr, place: (place[0], place[1], rr, 0)),
                    pl.BlockSpec((None, tr, c_), lambda rr, place: (place[0], rr, 0)),
                    pl.BlockSpec((3, tr, c_), lambda rr, place: (0, rr, 0))],
          out_specs=pl.BlockSpec((None, tr, c_), lambda rr, place: (place[1], rr, 0))),
      compiler_params=_params(("parallel",)),
  )(place, p, x1, x3)


def _reduce_scatter_begin(p, tag):
  x, y, c = _my_place()
  place = jnp.stack([2 * x + y, c]).astype(jnp.int32)
  x1 = _sibling_swap(_cast_other_half(p, place, f"rs_cast_other_half_{tag}"), f"rs_sibling_swap_{tag}")
  return _add_sibling(p, x1, place, f"rs_add_sibling_{tag}"), (p, x1, place)


def _reduce_scatter_end(state, x3, tag):
  p, x1, place = state
  return _share_halves(_sum_own(p, x1, x3, place, f"rs_sum_own_{tag}"), f"rs_share_halves_{tag}")


def _adamw(w, g, m, v, name):
  shape = w.shape
  cols = shape[-1]
  rows = w.size // cols
  tr = _tile(rows, max(8, (256 * 1024) // cols // 8 * 8), 8)

  def body(w_ref, g_ref, m_ref, v_ref, d_ref, nm_ref, nv_ref):
    gv = g_ref[...]
    nm = ADAM_B1 * m_ref[...] + (1.0 - ADAM_B1) * gv
    nv = ADAM_B2 * v_ref[...] + (1.0 - ADAM_B2) * (gv * gv)
    m_hat = nm / (1.0 - ADAM_B1 ** ADAM_STEP)
    v_hat = nv / (1.0 - ADAM_B2 ** ADAM_STEP)
    d_ref[...] = -ADAM_LR * (m_hat / (jnp.sqrt(v_hat) + ADAM_EPS) + ADAM_WD * w_ref[...])
    nm_ref[...] = nm
    nv_ref[...] = nv

  spec = pl.BlockSpec((tr, cols), lambda i: (i, 0))
  sds = jax.ShapeDtypeStruct((rows, cols), F32)
  outs = pl.pallas_call(
      body, name=name, grid=(rows // tr,), in_specs=[spec] * 4, out_specs=[spec] * 3, out_shape=[sds] * 3,
      compiler_params=_params(("parallel",)),
  )(*(t.reshape(rows, cols) for t in (w, g, m, v)))
  return tuple(o.reshape(shape) for o in outs)


SHARDED = (("a_w_in", True), ("b_w_in", True), ("w_mem_kv", False), ("w_out", False), ("w_gate", True),
           ("w_up", True), ("w_down", False))
SMALL_SHARDED = (("sgu_ln_g", 1), ("sgu_ln_b", 1))
REPLICATED = ("sgu_w_s", "sgu_b_s", "ln_mix_g", "ln_mix_b", "ln_ffn_g", "ln_ffn_b")
SMALL_ORDER = ("sgu_w_s", "sgu_b_s", "ln_mix_g", "ln_mix_b", "ln_ffn_g", "ln_ffn_b", "sgu_ln_g", "sgu_ln_b")
ROW_ALIGN = 16


def _pad_to(v, n):
  return jnp.pad(v, (0, n - v.shape[0]))


def _round_up(n, a):
  return -(-n // a) * a


def _exchange_form(t, transposed):
  return jnp.swapaxes(t, 1, 2) if transposed else t


def _to_shard_major(full, axis):
  shp = full.shape
  cut = shp[:axis] + (4, shp[axis] // 4) + shp[axis + 1:]
  return jnp.moveaxis(full.reshape(cut), axis, 0).reshape(4, -1, FLAT_COLS)


def _from_shard_major(rows, shard_shape, axis):
  full = jnp.moveaxis(rows.reshape((4,) + tuple(shard_shape)), 0, axis)
  shp = full.shape
  return full.reshape(shp[:axis] + (shp[axis] * shp[axis + 1],) + shp[axis + 2:])


class _WeightPack:
  def __init__(self, items, small=()):
    segs, self.rows, self.small, off = [], {}, [], 0
    for n, l, b in items:
      seg = b.reshape(-1, FLAT_COLS)
      self.rows[(n, l)] = (off, seg.shape[0], b.shape)
      segs.append(seg)
      off += seg.shape[0]
    if small:
      flat = jnp.concatenate([lax.bitcast_convert_type(v, BF16).reshape(-1) for _, v in small])
      rows = _round_up(flat.shape[0], ROW_ALIGN * FLAT_COLS) // FLAT_COLS
      self.small = [(n, v.shape) for n, v in small]
      self.small_rows = (off, rows)
      segs.append(_pad_to(flat, rows * FLAT_COLS).reshape(rows, FLAT_COLS))
      off += rows
    rows_pad = _round_up(off, 4 * ROW_ALIGN)
    if rows_pad > off:
      segs.append(jnp.zeros((rows_pad - off, FLAT_COLS), BF16))
    self.flat = jnp.concatenate(segs).reshape(2, rows_pad // 2, FLAT_COLS)

  def unpack(self, gathered):
    g = gathered.reshape(4, -1, FLAT_COLS)
    out = {}
    for (n, l), (off, nr, shape) in self.rows.items():
      out.setdefault(n, {})[l] = g[:, off:off + nr].reshape((4 * shape[0],) + shape[1:])
    if self.small:
      off, rows = self.small_rows
      flat = g[:, off:off + rows].reshape(4, rows * FLAT_COLS)
      pos = 0
      for n, shape in self.small:
        sz = 2 * math.prod(shape)
        vals = lax.bitcast_convert_type(flat[:, pos:pos + sz].reshape((4,) + shape + (2,)), F32)
        out[n] = _from_shard_major(vals, shape, len(shape) - 1)
        pos += sz
    return out


FIRST_WEIGHTS = (("a_w_in", 0), ("w_mem_kv", 0))


def _weight_packs(shards):
  blocks = {(n, l): _exchange_form(shards[n], tr)[l].astype(BF16)
            for n, tr in SHARDED for l in range(shards[n].shape[0])}
  first = _WeightPack([(n, l, blocks[(n, l)]) for n, l in FIRST_WEIGHTS],
                      small=[(n, shards[n]) for n, _ in SMALL_SHARDED])

  def model_layer(n, l):
    return {"a_w_in": 2 * l, "b_w_in": 2 * l + 1}.get(n, l)

  def in_second(n, l):
    return model_layer(n, l) == 3 or (model_layer(n, l) == 2 and n not in ("a_w_in", "w_mem_kv"))

  rest = [(n, l, b) for (n, l), b in blocks.items() if (n, l) not in FIRST_WEIGHTS]
  return first, {0: _WeightPack([it for it in rest if not in_second(it[0], it[1])]),
                 2: _WeightPack([it for it in rest if in_second(it[0], it[1])])}


def _reduce_grads(grads, shard_shapes):
  early, state, x3 = grads.pop("early_exchange")
  mine_early = _reduce_scatter_end(state, x3, "early")
  late = _GradPack([(n, l, g) for n, _ in SHARDED for l, g in enumerate(grads[n]) if (n, l) not in early.rows])
  q, state = _reduce_scatter_begin(late.p, "late")
  mine_late = _reduce_scatter_end(state, _chip_exchange(q, "rs_chip_exchange_late"), "late")
  out = {}
  for n, tr in SHARDED:
    layers, rows, cols = shard_shapes[n]
    blocks = []
    for l in range(layers):
      pack, mine = (early, mine_early) if (n, l) in early.rows else (late, mine_late)
      off, nr = pack.rows[(n, l)]
      block = mine[off:off + nr]
      blocks.append(block.reshape(cols, rows).T if tr else block.reshape(rows, cols))
    out[n] = jnp.stack(blocks)
  off, quarter_rows = early.rows["small"]
  piece = mine_early[off:off + quarter_rows].reshape(2, quarter_rows // 2, FLAT_COLS)
  small_sum = _all_gather_halves(piece, "gather_small_grads").reshape(-1)
  off = 0
  for n in SMALL_ORDER:
    shape = (len(grads[n]),) + grads[n][0].shape
    sz = math.prod(shape)
    out[n] = small_sum[off:off + sz].reshape(shape)
    off += sz
  return out


class _GradPack:
  def __init__(self, items, small=None):
    segs, self.rows, off = [], {}, 0
    for n, l, g in items:
      seg = _to_shard_major(g, 0)
      self.rows[(n, l)] = (off, seg.shape[1])
      segs.append(seg)
      off += seg.shape[1]
    if small is not None:
      flat = jnp.concatenate([jnp.stack(small[n]).reshape(-1) for n in SMALL_ORDER])
      n_small = _round_up(flat.shape[0], 4 * 2 * 8 * FLAT_COLS)
      quarter_rows = n_small // (4 * FLAT_COLS)
      self.rows["small"] = (off, quarter_rows)
      segs.append(_pad_to(flat, n_small).reshape(4, quarter_rows, FLAT_COLS))
      off += quarter_rows
    rows_pad = _round_up(off, 2 * ROW_ALIGN)
    if rows_pad > off:
      segs.append(jnp.zeros((4, rows_pad - off, FLAT_COLS), F32))
    self.p = jnp.concatenate(segs, axis=1).reshape(4, 2, rows_pad // 2, FLAT_COLS)


def _early_exchange_begin(grads):
  items = [(n, l, g) for n, _ in SHARDED for l, g in enumerate(grads[n]) if g is not None]
  pack = _GradPack(items, small={n: grads[n] for n in SMALL_ORDER})
  q, state = _reduce_scatter_begin(pack.p, "early")
  return q, (pack, state)


WEIGHT_NAMES = ("a_w_in", "b_w_in", "sgu_ln_g", "sgu_ln_b", "sgu_w_s", "sgu_b_s", "w_mem_kv", "w_out",
                "ln_mix_g", "ln_mix_b", "w_gate", "w_up", "w_down", "ln_ffn_g", "ln_ffn_b")


def kernel(x, mem, a_w_in, b_w_in, sgu_ln_g, sgu_ln_b, sgu_w_s, sgu_b_s, w_mem_kv, w_out, ln_mix_g, ln_mix_b, w_gate, w_up, w_down, ln_ffn_g, ln_ffn_b, loss_target, m_a_w_in, m_b_w_in, m_sgu_ln_g, m_sgu_ln_b, m_sgu_w_s, m_sgu_b_s, m_w_mem_kv, m_w_out, m_ln_mix_g, m_ln_mix_b, m_w_gate, m_w_up, m_w_down, m_ln_ffn_g, m_ln_ffn_b, v_a_w_in, v_b_w_in, v_sgu_ln_g, v_sgu_ln_b, v_sgu_w_s, v_sgu_b_s, v_w_mem_kv, v_w_out, v_ln_mix_g, v_ln_mix_b, v_w_gate, v_w_up, v_w_down, v_ln_ffn_g, v_ln_ffn_b):
  weights = dict(a_w_in=a_w_in, b_w_in=b_w_in, sgu_ln_g=sgu_ln_g, sgu_ln_b=sgu_ln_b, sgu_w_s=sgu_w_s, sgu_b_s=sgu_b_s,
                 w_mem_kv=w_mem_kv, w_out=w_out, ln_mix_g=ln_mix_g, ln_mix_b=ln_mix_b, w_gate=w_gate, w_up=w_up,
                 w_down=w_down, ln_ffn_g=ln_ffn_g, ln_ffn_b=ln_ffn_b)
  mom1 = dict(a_w_in=m_a_w_in, b_w_in=m_b_w_in, sgu_ln_g=m_sgu_ln_g, sgu_ln_b=m_sgu_ln_b, sgu_w_s=m_sgu_w_s,
              sgu_b_s=m_sgu_b_s, w_mem_kv=m_w_mem_kv, w_out=m_w_out, ln_mix_g=m_ln_mix_g, ln_mix_b=m_ln_mix_b,
              w_gate=m_w_gate, w_up=m_w_up, w_down=m_w_down, ln_ffn_g=m_ln_ffn_g, ln_ffn_b=m_ln_ffn_b)
  mom2 = dict(a_w_in=v_a_w_in, b_w_in=v_b_w_in, sgu_ln_g=v_sgu_ln_g, sgu_ln_b=v_sgu_ln_b, sgu_w_s=v_sgu_w_s,
              sgu_b_s=v_sgu_b_s, w_mem_kv=v_w_mem_kv, w_out=v_w_out, ln_mix_g=v_ln_mix_g, ln_mix_b=v_ln_mix_b,
              w_gate=v_w_gate, w_up=v_w_up, w_down=v_w_down, ln_ffn_g=v_ln_ffn_g, ln_ffn_b=v_ln_ffn_b)

  first, late = _weight_packs(weights)
  full = first.unpack(_all_gather_relayed(first.flat, "gather_first_weights"))
  for n in REPLICATED:
    full[n] = weights[n]
  loss_part, grad_x, grads = _local_step(x, mem, loss_target, full, late_weights=late,
                                         early_exchange=_early_exchange_begin)
  loss = lax.psum(loss_part, MESH_AXES)

  shard_shapes = {n: weights[n].shape for n, _ in SHARDED}
  red = _reduce_grads(grads, shard_shapes)
  chip = 2 * lax.axis_index("x") + lax.axis_index("y")
  for n, axis in SMALL_SHARDED:
    width = weights[n].shape[axis]
    red[n] = lax.dynamic_slice_in_dim(red[n], chip * width, width, axis)

  small_names = SMALL_ORDER
  def pack(d):
    flat = jnp.concatenate([d[n].reshape(-1) for n in small_names])
    return _pad_to(flat, _round_up(flat.shape[0], 8 * FLAT_COLS)).reshape(-1, FLAT_COLS)
  small_out = _adamw(pack(weights), pack(red), pack(mom1), pack(mom2), "adamw_small")
  delta, new_m, new_v = {}, {}, {}
  off = 0
  for n in small_names:
    sz = weights[n].size
    for dst, src in zip((delta, new_m, new_v), small_out):
      dst[n] = src.reshape(-1)[off:off + sz].reshape(weights[n].shape)
    off += sz
  for n, _ in SHARDED:
    delta[n], new_m[n], new_v[n] = _adamw(weights[n], red[n], mom1[n], mom2[n], f"adamw_{n}")

  return (loss, grad_x, *[red[n] for n in WEIGHT_NAMES], *[delta[n] for n in WEIGHT_NAMES],
          *[new_m[n] for n in WEIGHT_NAMES], *[new_v[n] for n in WEIGHT_NAMES])
```

```python
import math

import jax
import jax.numpy as jnp
from jax import lax
from jax.experimental import pallas as pl
from jax.experimental.pallas import tpu as pltpu

F32 = jnp.float32
BF16 = jnp.bfloat16

DEPTH = 4
HEAD_DIM = 64
N_DIL_HEADS = 12
DIL_WIDTH = N_DIL_HEADS * HEAD_DIM
DIL_PATTERNS = ((128, 1), (512, 4), (2048, 16))
BLOCK = 128
N_SGU_GROUPS = 12
SGU_GROUP_DIM = 64
SGU_WIDTH = N_SGU_GROUPS * SGU_GROUP_DIM
CHUNK = 128
N_MEM_HEADS = 4
MEM_WIDTH = N_MEM_HEADS * HEAD_DIM
DN_ALPHA = (2 * DEPTH) ** 0.25
LN_EPS = 1e-5
ATT_SCALE = HEAD_DIM ** -0.5
ADAM_LR = 0.001
ADAM_B1 = 0.9
ADAM_B2 = 0.999
ADAM_EPS = 1e-08
ADAM_WD = 0.01
ADAM_STEP = 10
NEG_BIG = -1e30

LANES = 128
FLAT_COLS = 1024
VMEM_LIMIT = 56 * 1024 * 1024
MESH_AXES = ("x", "y", "c")
MESH_ID = pl.DeviceIdType.MESH


def _tile(n, pref, align=LANES):
  if n <= pref:
    return n
  t = (pref // align) * align
  while t >= align:
    if n % t == 0:
      return t
    t -= align
  return n


def _params(sem):
  return pltpu.CompilerParams(dimension_semantics=sem, vmem_limit_bytes=VMEM_LIMIT)


def _dot(a, b):
  return jnp.dot(a, b, preferred_element_type=F32)


def _dot_nt(a, b):
  return lax.dot_general(a, b, (((1,), (1,)), ((), ())), preferred_element_type=F32)


def _dot_tn(a, b):
  return lax.dot_general(a, b, (((0,), (0,)), ((), ())), preferred_element_type=F32)


def _bf(v):
  return v.astype(BF16)


def _ln_stats(z):
  mu = jnp.mean(z, axis=-1, keepdims=True)
  zc = z - mu
  var = jnp.mean(zc * zc, axis=-1, keepdims=True)
  rstd = lax.rsqrt(var + LN_EPS)
  return zc * rstd, rstd


def _ln_bwd(dy, xhat, rstd, g):
  gdy = dy * g
  m1 = jnp.mean(gdy, axis=-1, keepdims=True)
  m2 = jnp.mean(gdy * xhat, axis=-1, keepdims=True)
  return rstd * (gdy - m1 - xhat * m2)


_GELU_C = math.sqrt(2.0 / math.pi)


def _gelu_parts(v):
  v2 = v * v
  t = jnp.tanh(_GELU_C * (v + 0.044715 * v * v2))
  val = 0.5 * v * (1.0 + t)
  der = 0.5 * (1.0 + t) + 0.5 * v * (1.0 - t * t) * (_GELU_C * (1.0 + 3.0 * 0.044715 * v2))
  return val, der


def _gelu(v):
  t = jnp.tanh(_GELU_C * (v + 0.044715 * v * v * v))
  return 0.5 * v * (1.0 + t)


def _sigmoid(v):
  return 1.0 / (1.0 + jnp.exp(-v))


def _mm(a, b, mode, out_dtype, name, add=None, add_scale=1.0, tm=512, tn=512, tk=512):
  if mode == "nn":
    (m, k), (k2, n) = a.shape, b.shape
  elif mode == "nt":
    (m, k), (n, k2) = a.shape, b.shape
  else:
    (k, m), (k2, n) = a.shape, b.shape
  assert k == k2, (a.shape, b.shape, mode)
  tm, tn, tk = _tile(m, tm), _tile(n, tn), _tile(k, tk)
  nk = k // tk
  if mode == "nn":
    a_spec = pl.BlockSpec((tm, tk), lambda i, j, kk: (i, kk))
    b_spec = pl.BlockSpec((tk, tn), lambda i, j, kk: (kk, j))
    dot = _dot
  elif mode == "nt":
    a_spec = pl.BlockSpec((tm, tk), lambda i, j, kk: (i, kk))
    b_spec = pl.BlockSpec((tn, tk), lambda i, j, kk: (j, kk))
    dot = _dot_nt
  else:
    a_spec = pl.BlockSpec((tk, tm), lambda i, j, kk: (kk, i))
    b_spec = pl.BlockSpec((tk, tn), lambda i, j, kk: (kk, j))
    dot = _dot_tn
  o_spec = pl.BlockSpec((tm, tn), lambda i, j, kk: (i, j))
  has_add = add is not None

  def body(*refs):
    if has_add:
      a_ref, b_ref, add_ref, o_ref, acc_ref = refs
    else:
      a_ref, b_ref, o_ref, acc_ref = refs
    kk = pl.program_id(2)

    @pl.when(kk == 0)
    def _():
      acc_ref[...] = jnp.zeros_like(acc_ref)

    acc_ref[...] += dot(_bf(a_ref[...]), _bf(b_ref[...]))

    @pl.when(kk == nk - 1)
    def _():
      r = acc_ref[...]
      if has_add:
        r = r + add_scale * add_ref[...].astype(F32)
      o_ref[...] = r.astype(out_dtype)

  in_specs = [a_spec, b_spec] + ([o_spec] if has_add else [])
  args = (a, b) + ((add,) if has_add else ())
  return pl.pallas_call(
      body, name=name, grid=(m // tm, n // tn, nk), in_specs=in_specs, out_specs=o_spec,
      out_shape=jax.ShapeDtypeStruct((m, n), out_dtype),
      scratch_shapes=[pltpu.VMEM((tm, tn), F32)],
      compiler_params=_params(("parallel", "parallel", "arbitrary")),
  )(*args)


def _mm_res_ln(a, w, res, g, b, name, tm=512, tk=512):
  m, k = a.shape
  d = w.shape[1]
  tm, tk = _tile(m, tm), _tile(k, tk)
  nk = k // tk

  def body(a_ref, w_ref, r_ref, g_ref, b_ref, z_ref, x_ref, xb_ref, acc_ref):
    kk = pl.program_id(1)

    @pl.when(kk == 0)
    def _():
      acc_ref[...] = jnp.zeros_like(acc_ref)

    acc_ref[...] += _dot(_bf(a_ref[...]), _bf(w_ref[...]))

    @pl.when(kk == nk - 1)
    def _():
      z = DN_ALPHA * r_ref[...] + acc_ref[...]
      xhat, _ = _ln_stats(z)
      xn = xhat * g_ref[...] + b_ref[...]
      z_ref[...] = z
      x_ref[...] = xn
      xb_ref[...] = _bf(xn)

  row = pl.BlockSpec((tm, d), lambda i, kk: (i, 0))
  vec = pl.BlockSpec((1, d), lambda i, kk: (0, 0))
  return pl.pallas_call(
      body, name=name, grid=(m // tm, nk),
      in_specs=[pl.BlockSpec((tm, tk), lambda i, kk: (i, kk)), pl.BlockSpec((tk, d), lambda i, kk: (kk, 0)), row, vec, vec],
      out_specs=[row, row, row],
      out_shape=[jax.ShapeDtypeStruct((m, d), F32), jax.ShapeDtypeStruct((m, d), F32), jax.ShapeDtypeStruct((m, d), BF16)],
      scratch_shapes=[pltpu.VMEM((tm, d), F32)],
      compiler_params=_params(("parallel", "arbitrary")),
  )(a, w, res, g.reshape(1, d), b.reshape(1, d))


def _ffn_up(xb, wg, wu, name, tm=512, tn=1408):
  m, d = xb.shape
  f = wg.shape[0]
  tm, tn = _tile(m, tm), _tile(f, tn)

  def body(x_ref, wg_ref, wu_ref, ga_ref, gb_ref, h_ref):
    xv = x_ref[...]
    a = _dot_nt(xv, wg_ref[...])
    b = _dot_nt(xv, wu_ref[...])
    sg = _sigmoid(a)
    silu = a * sg
    ga_ref[...] = _bf(b * (sg + silu * (1.0 - sg)))
    gb_ref[...] = _bf(silu)
    h_ref[...] = _bf(silu * b)

  wspec = pl.BlockSpec((tn, d), lambda j, i: (j, 0))
  ospec = pl.BlockSpec((tm, tn), lambda j, i: (i, j))
  sds = jax.ShapeDtypeStruct((m, f), BF16)
  return pl.pallas_call(
      body, name=name, grid=(f // tn, m // tm),
      in_specs=[pl.BlockSpec((tm, d), lambda j, i: (i, 0)), wspec, wspec],
      out_specs=[ospec, ospec, ospec], out_shape=[sds, sds, sds],
      compiler_params=_params(("parallel", "parallel")),
  )(xb, wg, wu)


def _ffn_bwd_hidden(dzb, wd, ga, gb, name, tm=512, tn=1408):
  m, d = dzb.shape
  f = wd.shape[0]
  tm, tn = _tile(m, tm), _tile(f, tn)

  def body(dz_ref, wd_ref, ga_ref, gb_ref, da_ref, db_ref):
    dh = _dot_nt(dz_ref[...], wd_ref[...])
    da_ref[...] = _bf(dh * ga_ref[...].astype(F32))
    db_ref[...] = _bf(dh * gb_ref[...].astype(F32))

  hspec = pl.BlockSpec((tm, tn), lambda j, i: (i, j))
  sds = jax.ShapeDtypeStruct((m, f), BF16)
  return pl.pallas_call(
      body, name=name, grid=(f // tn, m // tm),
      in_specs=[pl.BlockSpec((tm, d), lambda j, i: (i, 0)), pl.BlockSpec((tn, d), lambda j, i: (j, 0)), hspec, hspec],
      out_specs=[hspec, hspec], out_shape=[sds, sds],
      compiler_params=_params(("parallel", "parallel")),
  )(dzb, wd, ga, gb)


def _ln_bwd_tail(dy, z_ref, g_ref, dz_ref, dzb_ref, dg_ref, db_ref):
  @pl.when(pl.program_id(0) == 0)
  def _():
    dg_ref[...] = jnp.zeros_like(dg_ref)
    db_ref[...] = jnp.zeros_like(db_ref)

  xhat, rstd = _ln_stats(z_ref[...])
  dz = _ln_bwd(dy, xhat, rstd, g_ref[...])
  dz_ref[...] = dz
  dzb_ref[...] = _bf(dz)
  dg_ref[...] += jnp.sum(dy * xhat, axis=0, keepdims=True)
  db_ref[...] += jnp.sum(dy, axis=0, keepdims=True)


def _ln_bwd_outs(m, d, row, vec):
  return ([row, row, vec, vec],
          [jax.ShapeDtypeStruct((m, d), F32), jax.ShapeDtypeStruct((m, d), BF16),
           jax.ShapeDtypeStruct((1, d), F32), jax.ShapeDtypeStruct((1, d), F32)])


def _ffn_bwd_input_ln(da, db, wg, wu, dz2, z1, g, name, tm=512):
  m, f = da.shape
  d = wg.shape[1]
  tm = _tile(m, tm)

  def body(da_ref, db_ref, wg_ref, wu_ref, dz2_ref, z_ref, g_ref, dz_ref, dzb_ref, dg_ref, dbias_ref):
    dy = DN_ALPHA * dz2_ref[...] + _dot(da_ref[...], wg_ref[...]) + _dot(db_ref[...], wu_ref[...])
    _ln_bwd_tail(dy, z_ref, g_ref, dz_ref, dzb_ref, dg_ref, dbias_ref)

  hspec = pl.BlockSpec((tm, f), lambda i: (i, 0))
  wspec = pl.BlockSpec((f, d), lambda i: (0, 0), pipeline_mode=pl.Buffered(1))
  row = pl.BlockSpec((tm, d), lambda i: (i, 0))
  vec = pl.BlockSpec((1, d), lambda i: (0, 0))
  out_specs, out_shape = _ln_bwd_outs(m, d, row, vec)
  dz, dzb, dg, dbias = pl.pallas_call(
      body, name=name, grid=(m // tm,), in_specs=[hspec, hspec, wspec, wspec, row, row, vec],
      out_specs=out_specs, out_shape=out_shape, compiler_params=_params(("arbitrary",)),
  )(da, db, wg, wu, dz2, z1, g.reshape(1, d))
  return dz, dzb, dg[0], dbias[0]


def _in_proj_bwd_ln(dh, w_in, dz1, z2, g, name, tm=512):
  m, wd = dh.shape
  d = w_in.shape[1]
  tm = _tile(m, tm)

  def body(dh_ref, w_ref, dz1_ref, z_ref, g_ref, dz_ref, dzb_ref, dg_ref, dbias_ref):
    dy = DN_ALPHA * dz1_ref[...] + _dot(dh_ref[...], w_ref[...])
    _ln_bwd_tail(dy, z_ref, g_ref, dz_ref, dzb_ref, dg_ref, dbias_ref)

  row = pl.BlockSpec((tm, d), lambda i: (i, 0))
  vec = pl.BlockSpec((1, d), lambda i: (0, 0))
  out_specs, out_shape = _ln_bwd_outs(m, d, row, vec)
  dz, dzb, dg, dbias = pl.pallas_call(
      body, name=name, grid=(m // tm,),
      in_specs=[pl.BlockSpec((tm, wd), lambda i: (i, 0)),
                pl.BlockSpec((wd, d), lambda i: (0, 0), pipeline_mode=pl.Buffered(1)), row, row, vec],
      out_specs=out_specs, out_shape=out_shape, compiler_params=_params(("arbitrary",)),
  )(dh, w_in, dz1, z2, g.reshape(1, d))
  return dz, dzb, dg[0], dbias[0]


def _alibi_slopes():
  n = N_DIL_HEADS
  return jnp.exp2(-8.0 * (jnp.arange(n, dtype=F32) + 1.0) / n).reshape(1, n)


def _rows(start, d):
  if d == 1:
    return pl.ds(pl.multiple_of(start, BLOCK), BLOCK)
  return pl.ds(start, BLOCK, stride=d)


def _fill_bias_tables(bias_sc, slope0, slope1):
  row = lax.broadcasted_iota(jnp.int32, (2 * BLOCK, 2 * BLOCK), 0)
  col = lax.broadcasted_iota(jnp.int32, (2 * BLOCK, 2 * BLOCK), 1)
  qi = jnp.bitwise_and(row, BLOCK - 1)
  ki = jnp.bitwise_and(col, BLOCK - 1)
  is_cur = col >= BLOCK
  steps = jnp.where(is_cur, qi - ki, qi + BLOCK - ki)
  valid = jnp.logical_and(steps >= 0, steps <= BLOCK)
  slope = jnp.where(row >= BLOCK, slope1, slope0)
  dist = slope * steps.astype(F32)
  for p, (_, d) in enumerate(DIL_PATTERNS):
    base = jnp.where(valid, -d * dist, NEG_BIG)
    bias_sc[2 * p] = base
    bias_sc[2 * p + 1] = jnp.where(is_cur, base, NEG_BIG)


def _stack_heads(v2, head0):
  return jnp.concatenate([jnp.where(head0, v2, 0.0), jnp.where(head0, 0.0, v2)], axis=0)


def _unstack_heads(v, head0):
  return jnp.where(head0, v[:BLOCK], v[BLOCK:])


def _block_rows(idx, d, nblk):
  r = idx // nblk
  n = idx % nblk
  cur = _rows(r + n * (BLOCK * d), d)
  prev = _rows(r + jnp.maximum(n - 1, 0) * (BLOCK * d), d)
  return cur, prev, n


def pair_tile(dt):
  return pltpu.VMEM((2 * BLOCK, 2 * BLOCK), dt)


def _two_stage_loop(nb, first_stage, second_stage, bufs):
  a, b, c, d = bufs
  assert nb % 4 == 0 and nb >= 8

  def quad(u, carry):
    i = 4 * u
    first_stage(i + 2, c)
    first_stage(i + 3, d)
    second_stage(i, a)
    second_stage(i + 1, b)
    first_stage(i + 4, a)
    first_stage(i + 5, b)
    second_stage(i + 2, c)
    second_stage(i + 3, d)
    return carry

  first_stage(0, a)
  first_stage(1, b)
  lax.fori_loop(0, nb // 4 - 1, quad, 0)
  i = nb - 4
  first_stage(i + 2, c)
  first_stage(i + 3, d)
  for k, buf in enumerate(bufs):
    second_stage(i + k, buf)


def _attn_fwd(h3, name, gather=None):
  bl, s, _ = h3.shape
  npair = N_DIL_HEADS // 2
  nb = s // BLOCK
  hosted = gather is not None
  steps = bl * npair

  def body(*refs):
    if hosted:
      sl_ref, q_ref, k_ref, v_ref, w_ref, o_ref, lse_ref, g_ref, o_sc, l_sc, bias_sc, *s_bufs, send_sems, recv_sems = refs
      step = pl.program_id(0) * npair + pl.program_id(1)
      for phase, at in enumerate((0, (3 * steps) // 4)):
        @pl.when(step == at)
        def _(phase=phase):
          _relayed_gather_phase(phase, w_ref, g_ref, send_sems, recv_sems)
    else:
      sl_ref, q_ref, k_ref, v_ref, o_ref, lse_ref, o_sc, l_sc, bias_sc, *s_bufs = refs
    hp = pl.program_id(1)
    head0 = lax.broadcasted_iota(jnp.int32, (BLOCK, LANES), 1) < HEAD_DIM
    _fill_bias_tables(bias_sc, sl_ref[0, 2 * hp], sl_ref[0, 2 * hp + 1])

    for p, (_, d) in enumerate(DIL_PATTERNS):
      nblk = (s // d) // BLOCK
      two = nblk > 1
      ks = slice(0, 2 * BLOCK) if two else slice(BLOCK, 2 * BLOCK)

      def scores(idx, buf, p=p, d=d, nblk=nblk, two=two, ks=ks):
        cur, prev, n = _block_rows(idx, d, nblk)
        qs = _bf(_stack_heads(q_ref[cur, :], head0) * ATT_SCALE)
        kb = _bf(jnp.concatenate([k_ref[prev, :], k_ref[cur, :]], axis=0)) if two else _bf(k_ref[cur, :])
        first = jnp.where(n == 0, 1, 0) if two else 0
        buf[:, ks] = _dot_nt(qs, kb) + bias_sc[2 * p + first, :, ks]

      def values(idx, buf, p=p, d=d, nblk=nblk, two=two, ks=ks):
        cur, prev, _ = _block_rows(idx, d, nblk)
        sc = buf[:, ks]
        mx = jnp.max(sc, axis=1, keepdims=True)
        pe = jnp.exp(sc - mx)
        den = jnp.sum(pe, axis=1, keepdims=True)
        vb = _bf(jnp.concatenate([v_ref[prev, :], v_ref[cur, :]], axis=0)) if two else _bf(v_ref[cur, :])
        acc = _dot(_bf(pe), vb) * (1.0 / den)
        o_sc[p, cur, :] = _unstack_heads(acc, head0)
        l_sc[p, cur, :] = _unstack_heads(jnp.broadcast_to(mx + jnp.log(den), (2 * BLOCK, LANES)), head0)

      _two_stage_loop(nb, scores, values, s_bufs)

    def merge(i, carry):
      rows = pl.ds(pl.multiple_of(i * BLOCK, BLOCK), BLOCK)
      l0, l1, l2 = l_sc[0, rows, :], l_sc[1, rows, :], l_sc[2, rows, :]
      mx = jnp.maximum(jnp.maximum(l0, l1), l2)
      e0, e1, e2 = jnp.exp(l0 - mx), jnp.exp(l1 - mx), jnp.exp(l2 - mx)
      tot = e0 + e1 + e2
      o_ref[rows, :] = _bf((e0 * o_sc[0, rows, :] + e1 * o_sc[1, rows, :] + e2 * o_sc[2, rows, :]) * (1.0 / tot))
      lse_ref[rows, :] = mx + jnp.log(tot)
      return carry

    lax.fori_loop(0, nb, merge, 0)

    if hosted:
      @pl.when(step == steps - 1)
      def _():
        _relayed_gather_phase(2, w_ref, g_ref, send_sems, recv_sems)

  def col(off):
    return pl.BlockSpec((None, s, LANES), lambda b, p: (b, 0, off + p))

  in_specs = [pl.BlockSpec(memory_space=pltpu.SMEM), col(0), col(npair), col(2 * npair)]
  out_specs = [col(0), col(0)]
  out_shape = [jax.ShapeDtypeStruct((bl, s, DIL_WIDTH), BF16), jax.ShapeDtypeStruct((bl, s, DIL_WIDTH), F32)]
  scratch = [pltpu.VMEM((3, s, LANES), F32), pltpu.VMEM((3, s, LANES), F32),
             pltpu.VMEM((6, 2 * BLOCK, 2 * BLOCK), F32)] + [pair_tile(F32)] * 4
  args = (_alibi_slopes(), h3, h3, h3)
  if hosted:
    assert (gather.shape[1] // 2) % ROW_ALIGN == 0 and steps >= 4
    in_specs.append(ANY)
    out_specs.append(ANY)
    out_shape.append(jax.ShapeDtypeStruct((4,) + gather.shape, gather.dtype))
    scratch += [pltpu.SemaphoreType.DMA((N_RELAY_COPIES,)), pltpu.SemaphoreType.DMA((N_RELAY_COPIES,))]
    args += (gather,)
  sem = ("arbitrary", "arbitrary") if hosted else ("parallel", "parallel")
  outs = list(pl.pallas_call(
      body, name=name, grid=(bl, npair), in_specs=in_specs, out_specs=out_specs, out_shape=out_shape,
      scratch_shapes=scratch, compiler_params=_params(sem),
  )(*args))
  if hosted:
    outs[2] = _place_own_block(outs[2], gather)
  return outs


def _attn_bwd(h3, out3, lse3, dcat3, name, exchange=None):
  bl, s, _ = h3.shape
  npair = N_DIL_HEADS // 2
  nb = s // BLOCK
  hosted = exchange is not None

  def body(*refs):
    if hosted:
      (sl_ref, q_ref, k_ref, v_ref, o_ref, l_ref, do_ref, ex_ref, dq_out, dk_out, dv_out, got_ref,
       bias_sc, *pd, prod_sc, dq_ref, dk_ref, dv_ref, send_sems, recv_sems) = refs
      step = pl.program_id(0) * npair + pl.program_id(1)

      @pl.when(step == 0)
      def _():
        for cp in _chip_exchange_copies(ex_ref, got_ref, send_sems, recv_sems):
          cp.start()
    else:
      (sl_ref, q_ref, k_ref, v_ref, o_ref, l_ref, do_ref, dq_out, dk_out, dv_out,
       bias_sc, *pd, prod_sc, dq_ref, dk_ref, dv_ref) = refs
    pd_bufs = list(zip(pd[0::2], pd[1::2]))
    hp = pl.program_id(1)
    lane = lax.broadcasted_iota(jnp.int32, (BLOCK, LANES), 1)
    head0 = lane < HEAD_DIM
    _fill_bias_tables(bias_sc, sl_ref[0, 2 * hp], sl_ref[0, 2 * hp + 1])
    dq_ref[...] = jnp.zeros_like(dq_ref)
    dk_ref[...] = jnp.zeros_like(dk_ref)
    dv_ref[...] = jnp.zeros_like(dv_ref)
    prod_sc[...] = do_ref[...] * o_ref[...].astype(F32)

    def per_row(v2, pick0, pick1):
      return jnp.concatenate([jnp.sum(jnp.where(pick0, v2, 0.0), axis=1, keepdims=True),
                              jnp.sum(jnp.where(pick1, v2, 0.0), axis=1, keepdims=True)], axis=0)

    for p, (_, d) in enumerate(DIL_PATTERNS):
      nblk = (s // d) // BLOCK
      two = nblk > 1
      ks = slice(0, 2 * BLOCK) if two else slice(BLOCK, 2 * BLOCK)

      def operands(idx, d=d, nblk=nblk, two=two):
        cur, prev, n = _block_rows(idx, d, nblk)
        qs = _bf(_stack_heads(q_ref[cur, :], head0) * ATT_SCALE)
        dos = _bf(_stack_heads(do_ref[cur, :], head0))
        kb = _bf(jnp.concatenate([k_ref[prev, :], k_ref[cur, :]], axis=0)) if two else _bf(k_ref[cur, :])
        return cur, prev, n, qs, dos, kb

      def probs(idx, bufs, p=p, two=two, ks=ks, operands=operands):
        cur, prev, n, qs, dos, kb = operands(idx)
        vb = _bf(jnp.concatenate([v_ref[prev, :], v_ref[cur, :]], axis=0)) if two else _bf(v_ref[cur, :])
        lse = per_row(l_ref[cur, :], lane == 0, lane == HEAD_DIM)
        delta = per_row(prod_sc[cur, :], head0, jnp.logical_not(head0))
        first = jnp.where(n == 0, 1, 0) if two else 0
        pr = jnp.exp(_dot_nt(qs, kb) + bias_sc[2 * p + first, :, ks] - lse)
        bufs[0][:, ks] = _bf(pr)
        bufs[1][:, ks] = _bf(pr * (_dot_nt(dos, vb) - delta))

      def products(idx, bufs, two=two, ks=ks, operands=operands):
        cur, prev, _, qs, dos, kb = operands(idx)
        pr = bufs[0][:, ks]
        ds = bufs[1][:, ks]
        dq_ref[cur, :] += _unstack_heads(_dot(ds, kb), head0) * ATT_SCALE
        dkb = _dot_tn(ds, qs)
        dvb = _dot_tn(pr, dos)
        if two:
          dk_ref[prev, :] += dkb[:BLOCK]
          dv_ref[prev, :] += dvb[:BLOCK]
          dk_ref[cur, :] += dkb[BLOCK:]
          dv_ref[cur, :] += dvb[BLOCK:]
        else:
          dk_ref[cur, :] += dkb
          dv_ref[cur, :] += dvb

      _two_stage_loop(nb, probs, products, pd_bufs)

    dq_out[...] = _bf(dq_ref[...])
    dk_out[...] = _bf(dk_ref[...])
    dv_out[...] = _bf(dv_ref[...])

    if hosted:
      @pl.when(step == bl * npair - 1)
      def _():
        for cp in _chip_exchange_copies(ex_ref, got_ref, send_sems, recv_sems):
          cp.wait()

  def col(off):
    return pl.BlockSpec((None, s, LANES), lambda b, p: (b, 0, off + p))

  sds = jax.ShapeDtypeStruct((bl, s, DIL_WIDTH), BF16)
  in_specs = [pl.BlockSpec(memory_space=pltpu.SMEM), col(0), col(npair), col(2 * npair), col(0), col(0), col(0)]
  out_specs, out_shape = [col(0), col(0), col(0)], [sds, sds, sds]
  scratch = [pltpu.VMEM((6, 2 * BLOCK, 2 * BLOCK), F32)] + [pair_tile(BF16)] * 8 + [pltpu.VMEM((s, LANES), F32)] * 4
  args = (_alibi_slopes(), h3, h3, h3, out3, lse3, dcat3)
  if hosted:
    in_specs.append(ANY)
    out_specs.append(ANY)
    out_shape.append(jax.ShapeDtypeStruct((3,) + exchange.shape[1:], exchange.dtype))
    scratch += [pltpu.SemaphoreType.DMA((3,)), pltpu.SemaphoreType.DMA((3,))]
    args += (exchange,)
  sem = ("arbitrary", "arbitrary") if hosted else ("parallel", "parallel")
  return pl.pallas_call(
      body, name=name, grid=(bl, npair), in_specs=in_specs, out_specs=out_specs, out_shape=out_shape,
      scratch_shapes=scratch, compiler_params=_params(sem),
  )(*args)


def _mem_heads(tq):
  lane = lax.broadcasted_iota(jnp.int32, (tq, LANES), 1)
  return lane < HEAD_DIM


def _mem_fwd(h3, qcol, mkv3, name, tq=512):
  bl, s, _ = h3.shape
  nm = mkv3.shape[1]
  tq = _tile(s, tq)

  def body(q_ref, kv_ref, o_ref):
    head0 = _mem_heads(tq)
    for lg in range(MEM_WIDTH // LANES):
      cs = slice(lg * LANES, (lg + 1) * LANES)
      q2 = q_ref[:, cs]
      mk = _bf(kv_ref[:, cs])
      mv = _bf(kv_ref[:, MEM_WIDTH + lg * LANES:MEM_WIDTH + (lg + 1) * LANES])
      outs = []
      for j in range(2):
        hm = head0 if j == 0 else jnp.logical_not(head0)
        qj = _bf(jnp.where(hm, q2, 0.0) * ATT_SCALE)
        sc = _dot_nt(qj, mk)
        mx = jnp.max(sc, axis=1, keepdims=True)
        pe = jnp.exp(sc - mx)
        den = jnp.sum(pe, axis=1, keepdims=True)
        outs.append(_dot(_bf(pe * (1.0 / den)), mv))
      o_ref[:, cs] = _bf(jnp.where(head0, outs[0], outs[1]))

  return pl.pallas_call(
      body, name=name, grid=(bl, s // tq),
      in_specs=[pl.BlockSpec((None, tq, MEM_WIDTH), lambda b, i: (b, i, qcol)),
                pl.BlockSpec((None, nm, 2 * MEM_WIDTH), lambda b, i: (b, 0, 0))],
      out_specs=pl.BlockSpec((None, tq, MEM_WIDTH), lambda b, i: (b, i, 0)),
      out_shape=jax.ShapeDtypeStruct((bl, s, MEM_WIDTH), BF16),
      compiler_params=_params(("parallel", "parallel")),
  )(h3, mkv3)


def _mem_bwd(h3, qcol, mkv3, dcat3, name, tq=512):
  bl, s, _ = h3.shape
  nm = mkv3.shape[1]
  tq = _tile(s, tq)
  docol = dcat3.shape[2] // MEM_WIDTH - 1

  def body(q_ref, kv_ref, do_ref, dq_ref, dkv_ref):
    i = pl.program_id(1)

    @pl.when(i == 0)
    def _():
      dkv_ref[...] = jnp.zeros_like(dkv_ref)

    head0 = _mem_heads(tq)
    for lg in range(MEM_WIDTH // LANES):
      cs = slice(lg * LANES, (lg + 1) * LANES)
      vs = slice(MEM_WIDTH + lg * LANES, MEM_WIDTH + (lg + 1) * LANES)
      q2 = q_ref[:, cs]
      do2 = do_ref[:, cs]
      mk = _bf(kv_ref[:, cs])
      mv = _bf(kv_ref[:, vs])
      dq2 = jnp.zeros((tq, LANES), F32)
      dmk = jnp.zeros((nm, LANES), F32)
      dmv = jnp.zeros((nm, LANES), F32)
      for j in range(2):
        hm = head0 if j == 0 else jnp.logical_not(head0)
        qj = _bf(jnp.where(hm, q2, 0.0) * ATT_SCALE)
        doj = _bf(jnp.where(hm, do2, 0.0))
        sc = _dot_nt(qj, mk)
        mx = jnp.max(sc, axis=1, keepdims=True)
        pe = jnp.exp(sc - mx)
        pn = pe * (1.0 / jnp.sum(pe, axis=1, keepdims=True))
        pb = _bf(pn)
        dp = _dot_nt(doj, mv)
        dj = jnp.sum(pb.astype(F32) * dp, axis=1, keepdims=True)
        ds = _bf(pn * (dp - dj))
        dq2 = dq2 + jnp.where(hm, _dot(ds, mk), 0.0) * ATT_SCALE
        dmk = dmk + _dot_tn(ds, qj)
        dmv = dmv + _dot_tn(pb, doj)
      dq_ref[:, cs] = _bf(dq2)
      dkv_ref[:, cs] += dmk
      dkv_ref[:, vs] += dmv

  return pl.pallas_call(
      body, name=name, grid=(bl, s // tq),
      in_specs=[pl.BlockSpec((None, tq, MEM_WIDTH), lambda b, i: (b, i, qcol)),
                pl.BlockSpec((None, nm, 2 * MEM_WIDTH), lambda b, i: (b, 0, 0)),
                pl.BlockSpec((None, tq, MEM_WIDTH), lambda b, i: (b, i, docol))],
      out_specs=[pl.BlockSpec((None, tq, MEM_WIDTH), lambda b, i: (b, i, 0)),
                 pl.BlockSpec((None, nm, 2 * MEM_WIDTH), lambda b, i: (b, 0, 0))],
      out_shape=[jax.ShapeDtypeStruct((bl, s, MEM_WIDTH), BF16), jax.ShapeDtypeStruct((bl, nm, 2 * MEM_WIDTH), F32)],
      compiler_params=_params(("parallel", "arbitrary")),
  )(h3, mkv3, dcat3)


def _sgu_consts():
  ti = lax.broadcasted_iota(jnp.int32, (CHUNK, CHUNK), 0)
  si = lax.broadcasted_iota(jnp.int32, (CHUNK, CHUNK), 1)
  return si <= ti, si < SGU_GROUP_DIM


def _sgu_bias_lanes(b_s):
  return jnp.repeat(b_s.T, SGU_GROUP_DIM, axis=1)


def _sgu_fwd(h2, ln_g, ln_b, w_s, b_s, name, tr=512):
  t, _ = h2.shape
  tr = _tile(t, tr)
  nch = tr // CHUNK
  npair = N_SGU_GROUPS // 2

  def body(u_ref, v_ref, g_ref, b_ref, w_ref, bs_ref, o_ref, vn_sc):
    tril, head0 = _sgu_consts()
    xhat, _ = _ln_stats(_gelu(v_ref[...]))
    vn_sc[...] = _bf(xhat * g_ref[...] + b_ref[...])
    for jp in range(npair):
      cs = slice(jp * LANES, (jp + 1) * LANES)
      w0 = _bf(jnp.where(tril, w_ref[2 * jp], 0.0))
      w1 = _bf(jnp.where(tril, w_ref[2 * jp + 1], 0.0))
      bias = bs_ref[:, cs]
      for c in range(nch):
        rs = slice(c * CHUNK, (c + 1) * CHUNK)
        vb = vn_sc[rs, cs]
        mixed = jnp.where(head0, _dot(w0, vb), _dot(w1, vb)) + bias
        o_ref[rs, cs] = _bf(_gelu(u_ref[rs, cs]) * mixed)

  blk = lambda j: pl.BlockSpec((tr, SGU_WIDTH), lambda i: (i, j))
  vec = pl.BlockSpec((1, SGU_WIDTH), lambda i: (0, 0))
  return pl.pallas_call(
      body, name=name, grid=(t // tr,),
      in_specs=[blk(0), blk(1), vec, vec,
                pl.BlockSpec((N_SGU_GROUPS, CHUNK, CHUNK), lambda i: (0, 0, 0)),
                pl.BlockSpec((CHUNK, SGU_WIDTH), lambda i: (0, 0))],
      out_specs=blk(0), out_shape=jax.ShapeDtypeStruct((t, SGU_WIDTH), BF16),
      scratch_shapes=[pltpu.VMEM((tr, SGU_WIDTH), BF16)],
      compiler_params=_params(("parallel",)),
  )(h2, h2, ln_g.reshape(1, -1), ln_b.reshape(1, -1), w_s, _sgu_bias_lanes(b_s))


def _sgu_bwd(h2, dcat, ln_g, ln_b, w_s, b_s, name, tr=512):
  t, _ = h2.shape
  tr = _tile(t, tr)
  nch = tr // CHUNK
  npair = N_SGU_GROUPS // 2
  nsteps = t // tr

  def body(u_ref, v_ref, dm_ref, g_ref, b_ref, w_ref, bs_ref,
           du_ref, dv_ref, dw_ref, dbs_ref, dg_ref, db_ref, vn_sc, dmx_sc, dvn_sc, mix_sc, dbx_sc):
    i = pl.program_id(0)
    tril, head0 = _sgu_consts()

    @pl.when(i == 0)
    def _():
      dw_ref[...] = jnp.zeros_like(dw_ref)
      dg_ref[...] = jnp.zeros_like(dg_ref)
      db_ref[...] = jnp.zeros_like(db_ref)
      dbx_sc[...] = jnp.zeros_like(dbx_sc)

    gv, gv_der = _gelu_parts(v_ref[...])
    xhat, rstd = _ln_stats(gv)
    g = g_ref[...]
    vn_sc[...] = _bf(xhat * g + b_ref[...])
    gu, gu_der = _gelu_parts(u_ref[...])
    dmix = dm_ref[...]
    dmx_sc[...] = dmix * gu

    for jp in range(npair):
      cs = slice(jp * LANES, (jp + 1) * LANES)
      w0 = _bf(jnp.where(tril, w_ref[2 * jp], 0.0))
      w1 = _bf(jnp.where(tril, w_ref[2 * jp + 1], 0.0))
      bias = bs_ref[:, cs]
      dw0 = jnp.zeros((CHUNK, CHUNK), F32)
      dw1 = jnp.zeros((CHUNK, CHUNK), F32)
      dbx = jnp.zeros((CHUNK, LANES), F32)
      for c in range(nch):
        rs = slice(c * CHUNK, (c + 1) * CHUNK)
        vb = vn_sc[rs, cs]
        mix_sc[rs, cs] = jnp.where(head0, _dot(w0, vb), _dot(w1, vb)) + bias
        dmx = dmx_sc[rs, cs]
        d0 = _bf(jnp.where(head0, dmx, 0.0))
        d1 = _bf(jnp.where(head0, 0.0, dmx))
        dvn_sc[rs, cs] = _dot_tn(w0, d0) + _dot_tn(w1, d1)
        dw0 = dw0 + _dot_nt(d0, vb)
        dw1 = dw1 + _dot_nt(d1, vb)
        dbx = dbx + dmx
      dw_ref[2 * jp] += dw0
      dw_ref[2 * jp + 1] += dw1
      dbx_sc[:, cs] += dbx

    du_ref[...] = _bf(dmix * mix_sc[...] * gu_der)
    dvn = dvn_sc[...]
    dv_ref[...] = _bf(_ln_bwd(dvn, xhat, rstd, g) * gv_der)
    dg_ref[...] += jnp.sum(dvn * xhat, axis=0, keepdims=True)
    db_ref[...] += jnp.sum(dvn, axis=0, keepdims=True)

    @pl.when(i == nsteps - 1)
    def _():
      lane = lax.broadcasted_iota(jnp.int32, (CHUNK, LANES), 1)
      acc = jnp.zeros((CHUNK, LANES), F32)
      for gi in range(N_SGU_GROUPS):
        jp, j = gi // 2, gi % 2
        part = dbx_sc[:, jp * LANES:(jp + 1) * LANES]
        hm = (lane < SGU_GROUP_DIM) if j == 0 else (lane >= SGU_GROUP_DIM)
        colsum = jnp.sum(jnp.where(hm, part, 0.0), axis=1, keepdims=True)
        acc = jnp.where(lane == gi, colsum, acc)
        dw_ref[gi] = jnp.where(tril, dw_ref[gi], 0.0)
      dbs_ref[...] = acc

  blk = lambda j: pl.BlockSpec((tr, SGU_WIDTH), lambda i: (i, j))
  vec = pl.BlockSpec((1, SGU_WIDTH), lambda i: (0, 0))
  wspec = pl.BlockSpec((N_SGU_GROUPS, CHUNK, CHUNK), lambda i: (0, 0, 0))
  big = lambda dt: pltpu.VMEM((tr, SGU_WIDTH), dt)
  du, dv, dw, dbs, dg, db = pl.pallas_call(
      body, name=name, grid=(nsteps,),
      in_specs=[blk(0), blk(1), blk(0), vec, vec, wspec, pl.BlockSpec((CHUNK, SGU_WIDTH), lambda i: (0, 0))],
      out_specs=[blk(0), blk(0), wspec, pl.BlockSpec((CHUNK, LANES), lambda i: (0, 0)), vec, vec],
      out_shape=[jax.ShapeDtypeStruct((t, SGU_WIDTH), BF16), jax.ShapeDtypeStruct((t, SGU_WIDTH), BF16),
                 jax.ShapeDtypeStruct((N_SGU_GROUPS, CHUNK, CHUNK), F32), jax.ShapeDtypeStruct((CHUNK, LANES), F32),
                 jax.ShapeDtypeStruct((1, SGU_WIDTH), F32), jax.ShapeDtypeStruct((1, SGU_WIDTH), F32)],
      scratch_shapes=[big(BF16), big(F32), big(F32), big(F32), pltpu.VMEM((CHUNK, SGU_WIDTH), F32)],
      compiler_params=_params(("arbitrary",)),
  )(h2, h2, dcat, ln_g.reshape(1, -1), ln_b.reshape(1, -1), w_s, _sgu_bias_lanes(b_s))
  return du, dv, dw, dbs[:, :N_SGU_GROUPS].T, dg[0], db[0]


def _loss_head(xo, tgt, z, g, name, tm=512):
  m, d = xo.shape
  tm = _tile(m, tm)

  def body(x_ref, t_ref, z_ref, g_ref, l_ref, dz_ref, dzb_ref, dg_ref, dbias_ref):
    @pl.when(pl.program_id(0) == 0)
    def _():
      l_ref[...] = jnp.zeros_like(l_ref)

    diff = x_ref[...] - t_ref[...]
    rowsum = jnp.sum(diff * diff, axis=1, keepdims=True)
    tot = jnp.sum(rowsum, axis=0, keepdims=True) * (0.5 / d)
    l_ref[...] += jnp.broadcast_to(tot, l_ref.shape)
    _ln_bwd_tail(diff * (1.0 / d), z_ref, g_ref, dz_ref, dzb_ref, dg_ref, dbias_ref)

  row = pl.BlockSpec((tm, d), lambda i: (i, 0))
  vec = pl.BlockSpec((1, d), lambda i: (0, 0))
  out_specs, out_shape = _ln_bwd_outs(m, d, row, vec)
  l, dz, dzb, dg, dbias = pl.pallas_call(
      body, name=name, grid=(m // tm,), in_specs=[row, row, row, vec],
      out_specs=[pl.BlockSpec((8, LANES), lambda i: (0, 0))] + out_specs,
      out_shape=[jax.ShapeDtypeStruct((8, LANES), F32)] + out_shape,
      compiler_params=_params(("arbitrary",)),
  )(xo, tgt, z, g.reshape(1, d))
  return l[0, 0], dz, dzb, dg[0], dbias[0]


def _local_step(x3, mem3, tgt3, w, late_weights=None, early_exchange=None):
  w = dict(w)
  bl, s, d = x3.shape
  t = bl * s
  nm = mem3.shape[1]
  mem2 = mem3.reshape(bl * nm, d)
  x = x3.reshape(t, d)
  xb = x
  saved = []
  for i in range(DEPTH):
    j = i // 2
    attn = i % 2 == 0
    mkv = _mm(mem2, w["w_mem_kv"][i], "nn", F32, f"mkv_fwd_{i}", tm=1024, tn=512, tk=1024)
    mkv3 = mkv.reshape(bl, nm, 2 * MEM_WIDTH)
    w_in = w["a_w_in"][j] if attn else w["b_w_in"][j]
    h = _mm(xb, w_in, "nt", F32, f"in_proj_{i}", tm=512, tn=w_in.shape[0], tk=d)
    h3 = h.reshape(bl, s, -1)
    if attn and late_weights is not None and i in late_weights:
      mix3, lse3, gathered = _attn_fwd(h3, f"dil_attn_fwd_{i}", gather=late_weights[i].flat)
      for n, layers in late_weights[i].unpack(gathered).items():
        w[n] = {**w.get(n, {}), **layers}
    elif attn:
      mix3, lse3 = _attn_fwd(h3, f"dil_attn_fwd_{i}")
    if attn:
      mix = mix3.reshape(t, DIL_WIDTH)
      qcol = 3 * DIL_WIDTH // MEM_WIDTH
    else:
      mix = _sgu_fwd(h, w["sgu_ln_g"][j], w["sgu_ln_b"][j], w["sgu_w_s"][j], w["sgu_b_s"][j], f"sgu_fwd_{i}")
      lse3 = None
      qcol = 2 * SGU_WIDTH // MEM_WIDTH
    mo = _mem_fwd(h3, qcol, mkv3, f"mem_attn_fwd_{i}").reshape(t, MEM_WIDTH)
    cat = jnp.concatenate([mix, mo], axis=1)
    z1, xm, xmb = _mm_res_ln(cat, w["w_out"][i], x, w["ln_mix_g"][i], w["ln_mix_b"][i], f"out_proj_ln_{i}", tk=1024)
    ga, gb, hm = _ffn_up(xmb, w["w_gate"][i], w["w_up"][i], f"ffn_up_{i}")
    z2, xo, xob = _mm_res_ln(hm, w["w_down"][i], xm, w["ln_ffn_g"][i], w["ln_ffn_b"][i], f"ffn_down_ln_{i}", tk=hm.shape[1])
    saved.append(dict(xb=xb, h=h, h3=h3, mkv3=mkv3, mix3=(mix3 if attn else None), lse3=lse3, cat=cat, z1=z1,
                      xmb=xmb, ga=ga, gb=gb, hm=hm, z2=z2, qcol=qcol))
    x, xb = xo, xob

  names = ("a_w_in", "b_w_in", "sgu_ln_g", "sgu_ln_b", "sgu_w_s", "sgu_b_s", "w_mem_kv", "w_out",
           "ln_mix_g", "ln_mix_b", "w_gate", "w_up", "w_down", "ln_ffn_g", "ln_ffn_b")
  grads = {n: [None] * len(w[n]) for n in names}
  last = DEPTH - 1
  loss, dz2, dz2b, grads["ln_ffn_g"][last], grads["ln_ffn_b"][last] = _loss_head(
      x, tgt3.reshape(t, d), saved[last]["z2"], w["ln_ffn_g"][last], "loss_head")
  dx = None
  for i in reversed(range(DEPTH)):
    j = i // 2
    attn = i % 2 == 0
    sv = saved[i]
    da, db = _ffn_bwd_hidden(dz2b, w["w_down"][i], sv["ga"], sv["gb"], f"ffn_bwd_hidden_{i}")
    grads["w_down"][i] = _mm(sv["hm"], dz2b, "tn", F32, f"dw_down_{i}", tm=1408, tn=1024, tk=1024)
    grads["w_gate"][i] = _mm(da, sv["xmb"], "tn", F32, f"dw_gate_{i}", tm=1408, tn=1024, tk=1024)
    grads["w_up"][i] = _mm(db, sv["xmb"], "tn", F32, f"dw_up_{i}", tm=1408, tn=1024, tk=1024)
    dz1, dz1b, grads["ln_mix_g"][i], grads["ln_mix_b"][i] = _ffn_bwd_input_ln(
        da, db, w["w_gate"][i], w["w_up"][i], dz2, sv["z1"], w["ln_mix_g"][i], f"ffn_bwd_input_ln_{i}")
    grads["w_out"][i] = _mm(sv["cat"], dz1b, "tn", F32, f"dw_out_{i}", tm=1024, tn=1024, tk=1024)
    dcat = _mm(dz1b, w["w_out"][i], "nt", F32, f"out_proj_bwd_{i}", tm=1024, tn=1024, tk=1024)
    dcat3 = dcat.reshape(bl, s, -1)
    dqm3, dmkv3 = _mem_bwd(sv["h3"], sv["qcol"], sv["mkv3"], dcat3, f"mem_attn_bwd_{i}")
    grads["w_mem_kv"][i] = _mm(mem2, dmkv3.reshape(bl * nm, 2 * MEM_WIDTH), "tn", F32, f"dw_mem_kv_{i}", tm=1024, tn=512, tk=1024)
    dqm = dqm3.reshape(t, MEM_WIDTH)
    if attn and i == 0 and early_exchange is not None:
      q, (pack, state) = early_exchange(grads)
      dq3, dk3, dv3, x3 = _attn_bwd(sv["h3"], sv["mix3"], sv["lse3"], dcat3, f"dil_attn_bwd_{i}", exchange=q)
      grads["early_exchange"] = (pack, state, x3)
      parts = [dq3.reshape(t, -1), dk3.reshape(t, -1), dv3.reshape(t, -1), dqm]
    elif attn:
      dq3, dk3, dv3 = _attn_bwd(sv["h3"], sv["mix3"], sv["lse3"], dcat3, f"dil_attn_bwd_{i}")
      parts = [dq3.reshape(t, -1), dk3.reshape(t, -1), dv3.reshape(t, -1), dqm]
    else:
      du, dv, dws, dbs, dlg, dlb = _sgu_bwd(sv["h"], dcat, w["sgu_ln_g"][j], w["sgu_ln_b"][j], w["sgu_w_s"][j],
                                             w["sgu_b_s"][j], f"sgu_bwd_{i}")
      grads["sgu_w_s"][j], grads["sgu_b_s"][j], grads["sgu_ln_g"][j], grads["sgu_ln_b"][j] = dws, dbs, dlg, dlb
      parts = [du, dv, dqm]
    dh = jnp.concatenate(parts, axis=1)
    w_in = w["a_w_in"][j] if attn else w["b_w_in"][j]
    grads["a_w_in" if attn else "b_w_in"][j] = _mm(dh, sv["xb"], "tn", F32, f"dw_in_{i}", tm=1280 if attn else 896, tn=1024, tk=1024)
    if i > 0:
      dz2, dz2b, grads["ln_ffn_g"][i - 1], grads["ln_ffn_b"][i - 1] = _in_proj_bwd_ln(
          dh, w_in, dz1, saved[i - 1]["z2"], w["ln_ffn_g"][i - 1], f"in_proj_bwd_ln_{i}")
    else:
      dx = _mm(dh, w_in, "nn", F32, f"in_proj_bwd_{i}", add=dz1, add_scale=DN_ALPHA, tm=512, tn=d, tk=w_in.shape[0])
  return loss, dx.reshape(bl, s, d), grads


def _my_place():
  return lax.axis_index("x"), lax.axis_index("y"), lax.axis_index("c")


def _other_chips(x, y):
  return [(1 - x, y), (x, 1 - y), (1 - x, 1 - y)]


ANY = pl.BlockSpec(memory_space=pl.ANY)


def _all_gather_halves(wl, name):
  _, r, c_ = wl.shape

  def body(w_ref, g_ref, send_sems, recv_sems):
    x, y, c = _my_place()
    me = 2 * x + y
    sibling = (x, y, 1 - c)
    chips = _other_chips(x, y)

    def copy(k, src, dst, to):
      return pltpu.make_async_remote_copy(src_ref=src, dst_ref=dst, send_sem=send_sems.at[k], recv_sem=recv_sems.at[k],
                                          device_id=to, device_id_type=MESH_ID)

    first = [copy(k, w_ref.at[c], g_ref.at[me, c], (px, py, c)) for k, (px, py) in enumerate(chips)]
    for cp in first:
      cp.start()
    passed = []
    for k, (px, py) in enumerate(chips):
      landed = g_ref.at[2 * px + py, c]
      copy(k, landed, landed, (px, py, c)).wait_recv()
      fwd = copy(3 + k, landed, landed, sibling)
      fwd.start()
      passed.append(fwd)
    for k, (px, py) in enumerate(chips):
      theirs = g_ref.at[2 * px + py, 1 - c]
      copy(3 + k, theirs, theirs, sibling).wait_recv()
    for cp in first + passed:
      cp.wait_send()

  got = pl.pallas_call(
      body, name=name, in_specs=[ANY], out_specs=ANY,
      out_shape=jax.ShapeDtypeStruct((4, 2, r, c_), wl.dtype),
      scratch_shapes=[pltpu.SemaphoreType.DMA((6,)), pltpu.SemaphoreType.DMA((6,))],
  )(wl)
  chip = 2 * lax.axis_index("x") + lax.axis_index("y")
  return lax.dynamic_update_slice(got, wl[None], (chip, 0, 0, 0))


def _relayed_gather_phase(phase, w_ref, g_ref, send_sems, recv_sems):
  h = w_ref.shape[1] // 2
  x, y, c = _my_place()
  sibling = (x, y, 1 - c)
  xn, yn, dg = _other_chips(x, y)

  def copy(k, src, dst, to):
    return pltpu.make_async_remote_copy(src_ref=src, dst_ref=dst, send_sem=send_sems.at[k], recv_sem=recv_sems.at[k],
                                        device_id=to, device_id_type=MESH_ID)

  def block(chip, half):
    return g_ref.at[2 * chip[0] + chip[1], half]

  def same(k, ref, to):
    return copy(k, ref, ref, to)

  top, bottom = pl.ds(0, h), pl.ds(h, h)
  sends = [copy(0, w_ref.at[c], block((x, y), c), (*xn, c)), copy(1, w_ref.at[c], block((x, y), c), (*yn, c)),
           same(2, block(xn, c).at[top], (*yn, c)), same(3, block(yn, c).at[bottom], (*xn, c)),
           same(4, block(xn, c), sibling), same(5, block(yn, c), sibling), same(6, block(dg, c), sibling)]
  if phase == 0:
    sends[0].start()
    sends[1].start()
  elif phase == 1:
    same(0, block(xn, c), (*xn, c)).wait_recv()
    sends[2].start()
    sends[4].start()
    same(1, block(yn, c), (*yn, c)).wait_recv()
    sends[3].start()
    sends[5].start()
  else:
    same(2, block(dg, c).at[top], (*yn, c)).wait_recv()
    same(3, block(dg, c).at[bottom], (*xn, c)).wait_recv()
    sends[6].start()
    for k, chip in ((4, xn), (5, yn), (6, dg)):
      same(k, block(chip, 1 - c), sibling).wait_recv()
    for cp in sends:
      cp.wait_send()


N_RELAY_COPIES = 7


def _place_own_block(got, wl):
  chip = 2 * lax.axis_index("x") + lax.axis_index("y")
  return lax.dynamic_update_slice(got, wl[None], (chip, 0, 0, 0))


def _all_gather_relayed(wl, name):
  _, r, c_ = wl.shape
  assert (r // 2) % ROW_ALIGN == 0

  def body(w_ref, g_ref, send_sems, recv_sems):
    for phase in range(3):
      _relayed_gather_phase(phase, w_ref, g_ref, send_sems, recv_sems)

  got = pl.pallas_call(
      body, name=name, in_specs=[ANY], out_specs=ANY,
      out_shape=jax.ShapeDtypeStruct((4, 2, r, c_), wl.dtype),
      scratch_shapes=[pltpu.SemaphoreType.DMA((N_RELAY_COPIES,)), pltpu.SemaphoreType.DMA((N_RELAY_COPIES,))],
  )(wl)
  return _place_own_block(got, wl)


def _sibling_swap(v, name):
  def body(v_ref, o_ref, send_sem, recv_sem):
    x, y, c = _my_place()
    cp = pltpu.make_async_remote_copy(src_ref=v_ref, dst_ref=o_ref, send_sem=send_sem, recv_sem=recv_sem,
                                      device_id=(x, y, 1 - c), device_id_type=MESH_ID)
    cp.start()
    cp.wait()

  return pl.pallas_call(
      body, name=name, in_specs=[ANY], out_specs=ANY, out_shape=jax.ShapeDtypeStruct(v.shape, v.dtype),
      scratch_shapes=[pltpu.SemaphoreType.DMA, pltpu.SemaphoreType.DMA],
  )(v)


def _chip_exchange_copies(q_ref, o_ref, send_sems, recv_sems):
  x, y, c = _my_place()
  return [pltpu.make_async_remote_copy(src_ref=q_ref.at[2 * px + py], dst_ref=o_ref.at[k], send_sem=send_sems.at[k],
                                       recv_sem=recv_sems.at[k], device_id=(px, py, c), device_id_type=MESH_ID)
          for k, (px, py) in enumerate(_other_chips(x, y))]


def _chip_exchange(q, name):
  _, r, c_ = q.shape

  def body(q_ref, o_ref, send_sems, recv_sems):
    cps = _chip_exchange_copies(q_ref, o_ref, send_sems, recv_sems)
    for cp in cps:
      cp.start()
    for cp in cps:
      cp.wait()

  return pl.pallas_call(
      body, name=name, in_specs=[ANY], out_specs=ANY, out_shape=jax.ShapeDtypeStruct((3, r, c_), q.dtype),
      scratch_shapes=[pltpu.SemaphoreType.DMA((3,)), pltpu.SemaphoreType.DMA((3,))],
  )(q)


def _share_halves(both, name):
  _, r, c_ = both.shape

  def body(b_ref, o_ref, send_sem, recv_sem):
    x, y, c = _my_place()
    cp = pltpu.make_async_remote_copy(src_ref=b_ref.at[c], dst_ref=o_ref.at[c], send_sem=send_sem, recv_sem=recv_sem,
                                      device_id=(x, y, 1 - c), device_id_type=MESH_ID)
    cp.start()
    cp.wait()

  full = pl.pallas_call(
      body, name=name, in_specs=[ANY], out_specs=ANY, out_shape=jax.ShapeDtypeStruct(both.shape, both.dtype),
      input_output_aliases={0: 0},
      scratch_shapes=[pltpu.SemaphoreType.DMA, pltpu.SemaphoreType.DMA],
  )(both)
  return full.reshape(2 * r, c_)


def _half_spec(tr, c_, pick):
  return pl.BlockSpec((None, None, tr, c_), lambda s, r, place: (s, pick(place), r, 0))


def _cast_other_half(p, place, name, tr=1024):
  _, _, r, c_ = p.shape
  tr = _tile(r, tr, 16)

  def body(place_ref, p_ref, o_ref):
    o_ref[...] = _bf(p_ref[...])

  out_spec = pl.BlockSpec((None, tr, c_), lambda s, rr, place: (s, rr, 0))
  return pl.pallas_call(
      body, name=name, out_shape=jax.ShapeDtypeStruct((4, r, c_), BF16),
      grid_spec=pltpu.PrefetchScalarGridSpec(num_scalar_prefetch=1, grid=(4, r // tr),
                                             in_specs=[_half_spec(tr, c_, lambda place: 1 - place[1])], out_specs=out_spec),
      compiler_params=_params(("parallel", "parallel")),
  )(place, p)


def _add_sibling(p, x1, place, name, tr=1024):
  _, _, r, c_ = p.shape
  tr = _tile(r, tr, 16)

  def body(place_ref, p_ref, x_ref, o_ref):
    o_ref[...] = _bf(p_ref[...] + x_ref[...].astype(F32))

  row = pl.BlockSpec((None, tr, c_), lambda s, rr, place: (s, rr, 0))
  return pl.pallas_call(
      body, name=name, out_shape=jax.ShapeDtypeStruct((4, r, c_), BF16),
      grid_spec=pltpu.PrefetchScalarGridSpec(num_scalar_prefetch=1, grid=(4, r // tr),
                                             in_specs=[_half_spec(tr, c_, lambda place: place[1]), row], out_specs=row),
      compiler_params=_params(("parallel", "parallel")),
  )(place, p, x1)


def _sum_own(p, x1, x3, place, name, tr=1024):
  _, _, r, c_ = p.shape
  tr = _tile(r, tr, 16)

  def body(place_ref, p_ref, x1_ref, x3_ref, o_ref):
    acc = p_ref[...] + x1_ref[...].astype(F32)
    for k in range(3):
      acc = acc + x3_ref[k].astype(F32)
    o_ref[...] = acc

  return pl.pallas_call(
      body, name=name, out_shape=jax.ShapeDtypeStruct((2, r, c_), F32),
      grid_spec=pltpu.PrefetchScalarGridSpec(
          num_scalar_prefetch=1, grid=(r // tr,),
          in_specs=[pl.BlockSpec((None, None, tr, c_), lambda rr, place: (place[0], place[1], rr, 0)),
                    pl.BlockSpec((None, tr, c_), lambda rr, place: (place[0], rr, 0)),
                    pl.BlockSpec((3, tr, c_), lambda rr, place: (0, rr, 0))],
          out_specs=pl.BlockSpec((None, tr, c_), lambda rr, place: (place[1], rr, 0))),
      compiler_params=_params(("parallel",)),
  )(place, p, x1, x3)


def _reduce_scatter_begin(p, tag):
  x, y, c = _my_place()
  place = jnp.stack([2 * x + y, c]).astype(jnp.int32)
  x1 = _sibling_swap(_cast_other_half(p, place, f"rs_cast_other_half_{tag}"), f"rs_sibling_swap_{tag}")
  return _add_sibling(p, x1, place, f"rs_add_sibling_{tag}"), (p, x1, place)


def _reduce_scatter_end(state, x3, tag):
  p, x1, place = state
  return _share_halves(_sum_own(p, x1, x3, place, f"rs_sum_own_{tag}"), f"rs_share_halves_{tag}")


def _adamw(w, g, m, v, name):
  shape = w.shape
  cols = shape[-1]
  rows = w.size // cols
  tr = _tile(rows, max(8, (256 * 1024) // cols // 8 * 8), 8)

  def body(w_ref, g_ref, m_ref, v_ref, d_ref, nm_ref, nv_ref):
    gv = g_ref[...]
    nm = ADAM_B1 * m_ref[...] + (1.0 - ADAM_B1) * gv
    nv = ADAM_B2 * v_ref[...] + (1.0 - ADAM_B2) * (gv * gv)
    m_hat = nm / (1.0 - ADAM_B1 ** ADAM_STEP)
    v_hat = nv / (1.0 - ADAM_B2 ** ADAM_STEP)
    d_ref[...] = -ADAM_LR * (m_hat / (jnp.sqrt(v_hat) + ADAM_EPS) + ADAM_WD * w_ref[...])
    nm_ref[...] = nm
    nv_ref[...] = nv

  spec = pl.BlockSpec((tr, cols), lambda i: (i, 0))
  sds = jax.ShapeDtypeStruct((rows, cols), F32)
  outs = pl.pallas_call(
      body, name=name, grid=(rows // tr,), in_specs=[spec] * 4, out_specs=[spec] * 3, out_shape=[sds] * 3,
      compiler_params=_params(("parallel",)),
  )(*(t.reshape(rows, cols) for t in (w, g, m, v)))
  return tuple(o.reshape(shape) for o in outs)


SHARDED = (("a_w_in", True), ("b_w_in", True), ("w_mem_kv", False), ("w_out", False), ("w_gate", True),
           ("w_up", True), ("w_down", False))
SMALL_SHARDED = (("sgu_ln_g", 1), ("sgu_ln_b", 1))
REPLICATED = ("sgu_w_s", "sgu_b_s", "ln_mix_g", "ln_mix_b", "ln_ffn_g", "ln_ffn_b")
SMALL_ORDER = ("sgu_w_s", "sgu_b_s", "ln_mix_g", "ln_mix_b", "ln_ffn_g", "ln_ffn_b", "sgu_ln_g", "sgu_ln_b")
ROW_ALIGN = 16


def _pad_to(v, n):
  return jnp.pad(v, (0, n - v.shape[0]))


def _round_up(n, a):
  return -(-n // a) * a


def _exchange_form(t, transposed):
  return jnp.swapaxes(t, 1, 2) if transposed else t


def _to_shard_major(full, axis):
  shp = full.shape
  cut = shp[:axis] + (4, shp[axis] // 4) + shp[axis + 1:]
  return jnp.moveaxis(full.reshape(cut), axis, 0).reshape(4, -1, FLAT_COLS)


def _from_shard_major(rows, shard_shape, axis):
  full = jnp.moveaxis(rows.reshape((4,) + tuple(shard_shape)), 0, axis)
  shp = full.shape
  return full.reshape(shp[:axis] + (shp[axis] * shp[axis + 1],) + shp[axis + 2:])


class _WeightPack:
  def __init__(self, items, small=()):
    segs, self.rows, self.small, off = [], {}, [], 0
    for n, l, b in items:
      seg = b.reshape(-1, FLAT_COLS)
      self.rows[(n, l)] = (off, seg.shape[0], b.shape)
      segs.append(seg)
      off += seg.shape[0]
    if small:
      flat = jnp.concatenate([lax.bitcast_convert_type(v, BF16).reshape(-1) for _, v in small])
      rows = _round_up(flat.shape[0], ROW_ALIGN * FLAT_COLS) // FLAT_COLS
      self.small = [(n, v.shape) for n, v in small]
      self.small_rows = (off, rows)
      segs.append(_pad_to(flat, rows * FLAT_COLS).reshape(rows, FLAT_COLS))
      off += rows
    rows_pad = _round_up(off, 4 * ROW_ALIGN)
    if rows_pad > off:
      segs.append(jnp.zeros((rows_pad - off, FLAT_COLS), BF16))
    self.flat = jnp.concatenate(segs).reshape(2, rows_pad // 2, FLAT_COLS)

  def unpack(self, gathered):
    g = gathered.reshape(4, -1, FLAT_COLS)
    out = {}
    for (n, l), (off, nr, shape) in self.rows.items():
      out.setdefault(n, {})[l] = g[:, off:off + nr].reshape((4 * shape[0],) + shape[1:])
    if self.small:
      off, rows = self.small_rows
      flat = g[:, off:off + rows].reshape(4, rows * FLAT_COLS)
      pos = 0
      for n, shape in self.small:
        sz = 2 * math.prod(shape)
        vals = lax.bitcast_convert_type(flat[:, pos:pos + sz].reshape((4,) + shape + (2,)), F32)
        out[n] = _from_shard_major(vals, shape, len(shape) - 1)
        pos += sz
    return out


FIRST_WEIGHTS = (("a_w_in", 0), ("w_mem_kv", 0))


def _weight_packs(shards):
  blocks = {(n, l): _exchange_form(shards[n], tr)[l].astype(BF16)
            for n, tr in SHARDED for l in range(shards[n].shape[0])}
  first = _WeightPack([(n, l, blocks[(n, l)]) for n, l in FIRST_WEIGHTS],
                      small=[(n, shards[n]) for n, _ in SMALL_SHARDED])

  def model_layer(n, l):
    return {"a_w_in": 2 * l, "b_w_in": 2 * l + 1}.get(n, l)

  def in_second(n, l):
    return model_layer(n, l) == 3 or (model_layer(n, l) == 2 and n not in ("a_w_in", "w_mem_kv"))

  rest = [(n, l, b) for (n, l), b in blocks.items() if (n, l) not in FIRST_WEIGHTS]
  return first, {0: _WeightPack([it for it in rest if not in_second(it[0], it[1])]),
                 2: _WeightPack([it for it in rest if in_second(it[0], it[1])])}


def _reduce_grads(grads, shard_shapes):
  early, state, x3 = grads.pop("early_exchange")
  mine_early = _reduce_scatter_end(state, x3, "early")
  late = _GradPack([(n, l, g) for n, _ in SHARDED for l, g in enumerate(grads[n]) if (n, l) not in early.rows])
  q, state = _reduce_scatter_begin(late.p, "late")
  mine_late = _reduce_scatter_end(state, _chip_exchange(q, "rs_chip_exchange_late"), "late")
  out = {}
  for n, tr in SHARDED:
    layers, rows, cols = shard_shapes[n]
    blocks = []
    for l in range(layers):
      pack, mine = (early, mine_early) if (n, l) in early.rows else (late, mine_late)
      off, nr = pack.rows[(n, l)]
      block = mine[off:off + nr]
      blocks.append(block.reshape(cols, rows).T if tr else block.reshape(rows, cols))
    out[n] = jnp.stack(blocks)
  off, quarter_rows = early.rows["small"]
  piece = mine_early[off:off + quarter_rows].reshape(2, quarter_rows // 2, FLAT_COLS)
  small_sum = _all_gather_halves(piece, "gather_small_grads").reshape(-1)
  off = 0
  for n in SMALL_ORDER:
    shape = (len(grads[n]),) + grads[n][0].shape
    sz = math.prod(shape)
    out[n] = small_sum[off:off + sz].reshape(shape)
    off += sz
  return out


class _GradPack:
  def __init__(self, items, small=None):
    segs, self.rows, off = [], {}, 0
    for n, l, g in items:
      seg = _to_shard_major(g, 0)
      self.rows[(n, l)] = (off, seg.shape[1])
      segs.append(seg)
      off += seg.shape[1]
    if small is not None:
      flat = jnp.concatenate([jnp.stack(small[n]).reshape(-1) for n in SMALL_ORDER])
      n_small = _round_up(flat.shape[0], 4 * 2 * 8 * FLAT_COLS)
      quarter_rows = n_small // (4 * FLAT_COLS)
      self.rows["small"] = (off, quarter_rows)
      segs.append(_pad_to(flat, n_small).reshape(4, quarter_rows, FLAT_COLS))
      off += quarter_rows
    rows_pad = _round_up(off, 2 * ROW_ALIGN)
    if rows_pad > off:
      segs.append(jnp.zeros((4, rows_pad - off, FLAT_COLS), F32))
    self.p = jnp.concatenate(segs, axis=1).reshape(4, 2, rows_pad // 2, FLAT_COLS)


def _early_exchange_begin(grads):
  items = [(n, l, g) for n, _ in SHARDED for l, g in enumerate(grads[n]) if g is not None]
  pack = _GradPack(items, small={n: grads[n] for n in SMALL_ORDER})
  q, state = _reduce_scatter_begin(pack.p, "early")
  return q, (pack, state)


WEIGHT_NAMES = ("a_w_in", "b_w_in", "sgu_ln_g", "sgu_ln_b", "sgu_w_s", "sgu_b_s", "w_mem_kv", "w_out",
                "ln_mix_g", "ln_mix_b", "w_gate", "w_up", "w_down", "ln_ffn_g", "ln_ffn_b")


def kernel(x, mem, a_w_in, b_w_in, sgu_ln_g, sgu_ln_b, sgu_w_s, sgu_b_s, w_mem_kv, w_out, ln_mix_g, ln_mix_b, w_gate, w_up, w_down, ln_ffn_g, ln_ffn_b, loss_target, m_a_w_in, m_b_w_in, m_sgu_ln_g, m_sgu_ln_b, m_sgu_w_s, m_sgu_b_s, m_w_mem_kv, m_w_out, m_ln_mix_g, m_ln_mix_b, m_w_gate, m_w_up, m_w_down, m_ln_ffn_g, m_ln_ffn_b, v_a_w_in, v_b_w_in, v_sgu_ln_g, v_sgu_ln_b, v_sgu_w_s, v_sgu_b_s, v_w_mem_kv, v_w_out, v_ln_mix_g, v_ln_mix_b, v_w_gate, v_w_up, v_w_down, v_ln_ffn_g, v_ln_ffn_b):
  weights = dict(a_w_in=a_w_in, b_w_in=b_w_in, sgu_ln_g=sgu_ln_g, sgu_ln_b=sgu_ln_b, sgu_w_s=sgu_w_s, sgu_b_s=sgu_b_s,
                 w_mem_kv=w_mem_kv, w_out=w_out, ln_mix_g=ln_mix_g, ln_mix_b=ln_mix_b, w_gate=w_gate, w_up=w_up,
                 w_down=w_down, ln_ffn_g=ln_ffn_g, ln_ffn_b=ln_ffn_b)
  mom1 = dict(a_w_in=m_a_w_in, b_w_in=m_b_w_in, sgu_ln_g=m_sgu_ln_g, sgu_ln_b=m_sgu_ln_b, sgu_w_s=m_sgu_w_s,
              sgu_b_s=m_sgu_b_s, w_mem_kv=m_w_mem_kv, w_out=m_w_out, ln_mix_g=m_ln_mix_g, ln_mix_b=m_ln_mix_b,
              w_gate=m_w_gate, w_up=m_w_up, w_down=m_w_down, ln_ffn_g=m_ln_ffn_g, ln_ffn_b=m_ln_ffn_b)
  mom2 = dict(a_w_in=v_a_w_in, b_w_in=v_b_w_in, sgu_ln_g=v_sgu_ln_g, sgu_ln_b=v_sgu_ln_b, sgu_w_s=v_sgu_w_s,
              sgu_b_s=v_sgu_b_s, w_mem_kv=v_w_mem_kv, w_out=v_w_out, ln_mix_g=v_ln_mix_g, ln_mix_b=v_ln_mix_b,
              w_gate=v_w_gate, w_up=v_w_up, w_down=v_w_down, ln_ffn_g=v_ln_ffn_g, ln_ffn_b=v_ln_ffn_b)

  first, late = _weight_packs(weights)
  full = first.unpack(_all_gather_relayed(first.flat, "gather_first_weights"))
  for n in REPLICATED:
    full[n] = weights[n]
  loss_part, grad_x, grads = _local_step(x, mem, loss_target, full, late_weights=late,
                                         early_exchange=_early_exchange_begin)
  loss = lax.psum(loss_part, MESH_AXES)

  shard_shapes = {n: weights[n].shape for n, _ in SHARDED}
  red = _reduce_grads(grads, shard_shapes)
  chip = 2 * lax.axis_index("x") + lax.axis_index("y")
  for n, axis in SMALL_SHARDED:
    width = weights[n].shape[axis]
    red[n] = lax.dynamic_slice_in_dim(red[n], chip * width, width, axis)

  small_names = SMALL_ORDER
  def pack(d):
    flat = jnp.concatenate([d[n].reshape(-1) for n in small_names])
    return _pad_to(flat, _round_up(flat.shape[0], 8 * FLAT_COLS)).reshape(-1, FLAT_COLS)
  small_out = _adamw(pack(weights), pack(red), pack(mom1), pack(mom2), "adamw_small")
  delta, new_m, new_v = {}, {}, {}
  off = 0
  for n in small_names:
    sz = weights[n].size
    for dst, src in zip((delta, new_m, new_v), small_out):
      dst[n] = src.reshape(-1)[off:off + sz].reshape(weights[n].shape)
    off += sz
  for n, _ in SHARDED:
    delta[n], new_m[n], new_v[n] = _adamw(weights[n], red[n], mom1[n], mom2[n], f"adamw_{n}")

  return (loss, grad_x, *[red[n] for n in WEIGHT_NAMES], *[delta[n] for n in WEIGHT_NAMES],
          *[new_m[n] for n in WEIGHT_NAMES], *[new_v[n] for n in WEIGHT_NAMES])
```

```python
import math

import jax
import jax.numpy as jnp
from jax import lax
from jax.experimental import pallas as pl
from jax.experimental.pallas import tpu as pltpu

F32 = jnp.float32
BF16 = jnp.bfloat16

DEPTH = 4
HEAD_DIM = 64
N_DIL_HEADS = 12
DIL_WIDTH = N_DIL_HEADS * HEAD_DIM
DIL_PATTERNS = ((128, 1), (512, 4), (2048, 16))
BLOCK = 128
N_SGU_GROUPS = 12
SGU_WIDTH = N_SGU_GROUPS * 64
CHUNK = 128
N_MEM_HEADS = 4
MEM_WIDTH = N_MEM_HEADS * HEAD_DIM
DN_ALPHA = (2 * DEPTH) ** 0.25
LN_EPS = 1e-5
ATT_SCALE = HEAD_DIM ** -0.5
ADAM_LR = 0.001
ADAM_B1 = 0.9
ADAM_B2 = 0.999
ADAM_EPS = 1e-08
ADAM_WD = 0.01
ADAM_STEP = 10
NEG_BIG = -1e30

LANES = 128
FLAT_COLS = 1024
VMEM_LIMIT = 56 * 1024 * 1024
MESH_AXES = ("x", "y", "c")
MESH_ID = pl.DeviceIdType.MESH


def _tile(n, pref, align=LANES):
  if n <= pref:
    return n
  t = (pref // align) * align
  while t >= align:
    if n % t == 0:
      return t
    t -= align
  return n


def _params(sem):
  return pltpu.CompilerParams(dimension_semantics=sem, vmem_limit_bytes=VMEM_LIMIT)


def _dot(a, b):
  return jnp.dot(a, b, preferred_element_type=F32)


def _dot_nt(a, b):
  return lax.dot_general(a, b, (((1,), (1,)), ((), ())), preferred_element_type=F32)


def _dot_tn(a, b):
  return lax.dot_general(a, b, (((0,), (0,)), ((), ())), preferred_element_type=F32)


def _bf(v):
  return v.astype(BF16)


def _ln_stats(z):
  mu = jnp.mean(z, axis=-1, keepdims=True)
  zc = z - mu
  var = jnp.mean(zc * zc, axis=-1, keepdims=True)
  rstd = lax.rsqrt(var + LN_EPS)
  return zc * rstd, rstd


def _ln_bwd(dy, xhat, rstd, g):
  gdy = dy * g
  m1 = jnp.mean(gdy, axis=-1, keepdims=True)
  m2 = jnp.mean(gdy * xhat, axis=-1, keepdims=True)
  return rstd * (gdy - m1 - xhat * m2)


_GELU_C = math.sqrt(2.0 / math.pi)


def _gelu_parts(v):
  v2 = v * v
  t = jnp.tanh(_GELU_C * (v + 0.044715 * v * v2))
  val = 0.5 * v * (1.0 + t)
  der = 0.5 * (1.0 + t) + 0.5 * v * (1.0 - t * t) * (_GELU_C * (1.0 + 3.0 * 0.044715 * v2))
  return val, der


def _gelu(v):
  t = jnp.tanh(_GELU_C * (v + 0.044715 * v * v * v))
  return 0.5 * v * (1.0 + t)


def _sigmoid(v):
  return 1.0 / (1.0 + jnp.exp(-v))


def _mm(a, b, mode, out_dtype, name, add=None, add_scale=1.0, tm=512, tn=512, tk=512):
  if mode == "nn":
    (m, k), (k2, n) = a.shape, b.shape
  elif mode == "nt":
    (m, k), (n, k2) = a.shape, b.shape
  else:
    (k, m), (k2, n) = a.shape, b.shape
  assert k == k2, (a.shape, b.shape, mode)
  tm, tn, tk = _tile(m, tm), _tile(n, tn), _tile(k, tk)
  nk = k // tk
  if mode == "nn":
    a_spec = pl.BlockSpec((tm, tk), lambda i, j, kk: (i, kk))
    b_spec = pl.BlockSpec((tk, tn), lambda i, j, kk: (kk, j))
    dot = _dot
  elif mode == "nt":
    a_spec = pl.BlockSpec((tm, tk), lambda i, j, kk: (i, kk))
    b_spec = pl.BlockSpec((tn, tk), lambda i, j, kk: (j, kk))
    dot = _dot_nt
  else:
    a_spec = pl.BlockSpec((tk, tm), lambda i, j, kk: (kk, i))
    b_spec = pl.BlockSpec((tk, tn), lambda i, j, kk: (kk, j))
    dot = _dot_tn
  o_spec = pl.BlockSpec((tm, tn), lambda i, j, kk: (i, j))
  has_add = add is not None

  def body(*refs):
    if has_add:
      a_ref, b_ref, add_ref, o_ref, acc_ref = refs
    else:
      a_ref, b_ref, o_ref, acc_ref = refs
    kk = pl.program_id(2)

    @pl.when(kk == 0)
    def _():
      acc_ref[...] = jnp.zeros_like(acc_ref)

    acc_ref[...] += dot(_bf(a_ref[...]), _bf(b_ref[...]))

    @pl.when(kk == nk - 1)
    def _():
      r = acc_ref[...]
      if has_add:
        r = r + add_scale * add_ref[...].astype(F32)
      o_ref[...] = r.astype(out_dtype)

  in_specs = [a_spec, b_spec] + ([o_spec] if has_add else [])
  args = (a, b) + ((add,) if has_add else ())
  return pl.pallas_call(
      body, name=name, grid=(m // tm, n // tn, nk), in_specs=in_specs, out_specs=o_spec,
      out_shape=jax.ShapeDtypeStruct((m, n), out_dtype),
      scratch_shapes=[pltpu.VMEM((tm, tn), F32)],
      compiler_params=_params(("parallel", "parallel", "arbitrary")),
  )(*args)


def _mm_res_ln(a, w, res, g, b, name, tm=512, tk=512):
  m, k = a.shape
  d = w.shape[1]
  tm, tk = _tile(m, tm), _tile(k, tk)
  nk = k // tk

  def body(a_ref, w_ref, r_ref, g_ref, b_ref, z_ref, x_ref, xb_ref, acc_ref):
    kk = pl.program_id(1)

    @pl.when(kk == 0)
    def _():
      acc_ref[...] = jnp.zeros_like(acc_ref)

    acc_ref[...] += _dot(_bf(a_ref[...]), _bf(w_ref[...]))

    @pl.when(kk == nk - 1)
    def _():
      z = DN_ALPHA * r_ref[...] + acc_ref[...]
      xhat, _ = _ln_stats(z)
      xn = xhat * g_ref[...] + b_ref[...]
      z_ref[...] = z
      x_ref[...] = xn
      xb_ref[...] = _bf(xn)

  row = pl.BlockSpec((tm, d), lambda i, kk: (i, 0))
  vec = pl.BlockSpec((1, d), lambda i, kk: (0, 0))
  return pl.pallas_call(
      body, name=name, grid=(m // tm, nk),
      in_specs=[pl.BlockSpec((tm, tk), lambda i, kk: (i, kk)), pl.BlockSpec((tk, d), lambda i, kk: (kk, 0)), row, vec, vec],
      out_specs=[row, row, row],
      out_shape=[jax.ShapeDtypeStruct((m, d), F32), jax.ShapeDtypeStruct((m, d), F32), jax.ShapeDtypeStruct((m, d), BF16)],
      scratch_shapes=[pltpu.VMEM((tm, d), F32)],
      compiler_params=_params(("parallel", "arbitrary")),
  )(a, w, res, g.reshape(1, d), b.reshape(1, d))


def _ffn_up(xb, wg, wu, name, tm=512, tn=1408):
  m, d = xb.shape
  f = wg.shape[0]
  tm, tn = _tile(m, tm), _tile(f, tn)

  def body(x_ref, wg_ref, wu_ref, ga_ref, gb_ref, h_ref):
    xv = x_ref[...]
    a = _dot_nt(xv, wg_ref[...])
    b = _dot_nt(xv, wu_ref[...])
    sg = _sigmoid(a)
    silu = a * sg
    ga_ref[...] = _bf(b * (sg + silu * (1.0 - sg)))
    gb_ref[...] = _bf(silu)
    h_ref[...] = _bf(silu * b)

  wspec = pl.BlockSpec((tn, d), lambda j, i: (j, 0))
  ospec = pl.BlockSpec((tm, tn), lambda j, i: (i, j))
  sds = jax.ShapeDtypeStruct((m, f), BF16)
  return pl.pallas_call(
      body, name=name, grid=(f // tn, m // tm),
      in_specs=[pl.BlockSpec((tm, d), lambda j, i: (i, 0)), wspec, wspec],
      out_specs=[ospec, ospec, ospec], out_shape=[sds, sds, sds],
      compiler_params=_params(("parallel", "parallel")),
  )(xb, wg, wu)


def _ffn_bwd_hidden(dzb, wd, ga, gb, name, tm=512, tn=1408):
  m, d = dzb.shape
  f = wd.shape[0]
  tm, tn = _tile(m, tm), _tile(f, tn)

  def body(dz_ref, wd_ref, ga_ref, gb_ref, da_ref, db_ref):
    dh = _dot_nt(dz_ref[...], wd_ref[...])
    da_ref[...] = _bf(dh * ga_ref[...].astype(F32))
    db_ref[...] = _bf(dh * gb_ref[...].astype(F32))

  hspec = pl.BlockSpec((tm, tn), lambda j, i: (i, j))
  sds = jax.ShapeDtypeStruct((m, f), BF16)
  return pl.pallas_call(
      body, name=name, grid=(f // tn, m // tm),
      in_specs=[pl.BlockSpec((tm, d), lambda j, i: (i, 0)), pl.BlockSpec((tn, d), lambda j, i: (j, 0)), hspec, hspec],
      out_specs=[hspec, hspec], out_shape=[sds, sds],
      compiler_params=_params(("parallel", "parallel")),
  )(dzb, wd, ga, gb)


def _ln_bwd_tail(dy, z_ref, g_ref, dz_ref, dzb_ref, dg_ref, db_ref):
  @pl.when(pl.program_id(0) == 0)
  def _():
    dg_ref[...] = jnp.zeros_like(dg_ref)
    db_ref[...] = jnp.zeros_like(db_ref)

  xhat, rstd = _ln_stats(z_ref[...])
  dz = _ln_bwd(dy, xhat, rstd, g_ref[...])
  dz_ref[...] = dz
  dzb_ref[...] = _bf(dz)
  dg_ref[...] += jnp.sum(dy * xhat, axis=0, keepdims=True)
  db_ref[...] += jnp.sum(dy, axis=0, keepdims=True)


def _ln_bwd_outs(m, d, row, vec):
  return ([row, row, vec, vec],
          [jax.ShapeDtypeStruct((m, d), F32), jax.ShapeDtypeStruct((m, d), BF16),
           jax.ShapeDtypeStruct((1, d), F32), jax.ShapeDtypeStruct((1, d), F32)])


def _ffn_bwd_input_ln(da, db, wg, wu, dz2, z1, g, name, tm=512):
  m, f = da.shape
  d = wg.shape[1]
  tm = _tile(m, tm)

  def body(da_ref, db_ref, wg_ref, wu_ref, dz2_ref, z_ref, g_ref, dz_ref, dzb_ref, dg_ref, dbias_ref):
    dy = DN_ALPHA * dz2_ref[...] + _dot(da_ref[...], wg_ref[...]) + _dot(db_ref[...], wu_ref[...])
    _ln_bwd_tail(dy, z_ref, g_ref, dz_ref, dzb_ref, dg_ref, dbias_ref)

  hspec = pl.BlockSpec((tm, f), lambda i: (i, 0))
  wspec = pl.BlockSpec((f, d), lambda i: (0, 0), pipeline_mode=pl.Buffered(1))
  row = pl.BlockSpec((tm, d), lambda i: (i, 0))
  vec = pl.BlockSpec((1, d), lambda i: (0, 0))
  out_specs, out_shape = _ln_bwd_outs(m, d, row, vec)
  dz, dzb, dg, dbias = pl.pallas_call(
      body, name=name, grid=(m // tm,), in_specs=[hspec, hspec, wspec, wspec, row, row, vec],
      out_specs=out_specs, out_shape=out_shape, compiler_params=_params(("arbitrary",)),
  )(da, db, wg, wu, dz2, z1, g.reshape(1, d))
  return dz, dzb, dg[0], dbias[0]


def _in_proj_bwd_ln(dh, w_in, dz1, z2, g, name, tm=512):
  m, wd = dh.shape
  d = w_in.shape[1]
  tm = _tile(m, tm)

  def body(dh_ref, w_ref, dz1_ref, z_ref, g_ref, dz_ref, dzb_ref, dg_ref, dbias_ref):
    dy = DN_ALPHA * dz1_ref[...] + _dot(dh_ref[...], w_ref[...])
    _ln_bwd_tail(dy, z_ref, g_ref, dz_ref, dzb_ref, dg_ref, dbias_ref)

  row = pl.BlockSpec((tm, d), lambda i: (i, 0))
  vec = pl.BlockSpec((1, d), lambda i: (0, 0))
  out_specs, out_shape = _ln_bwd_outs(m, d, row, vec)
  dz, dzb, dg, dbias = pl.pallas_call(
      body, name=name, grid=(m // tm,),
      in_specs=[pl.BlockSpec((tm, wd), lambda i: (i, 0)),
                pl.BlockSpec((wd, d), lambda i: (0, 0), pipeline_mode=pl.Buffered(1)), row, row, vec],
      out_specs=out_specs, out_shape=out_shape, compiler_params=_params(("arbitrary",)),
  )(dh, w_in, dz1, z2, g.reshape(1, d))
  return dz, dzb, dg[0], dbias[0]


def _alibi_slopes():
  n = N_DIL_HEADS
  return jnp.exp2(-8.0 * (jnp.arange(n, dtype=F32) + 1.0) / n).reshape(1, n)


def _rows(start, d):
  if d == 1:
    return pl.ds(pl.multiple_of(start, BLOCK), BLOCK)
  return pl.ds(start, BLOCK, stride=d)


def _fill_bias_tables(bias_sc, slope0, slope1):
  row = lax.broadcasted_iota(jnp.int32, (2 * BLOCK, 2 * BLOCK), 0)
  col = lax.broadcasted_iota(jnp.int32, (2 * BLOCK, 2 * BLOCK), 1)
  qi = jnp.bitwise_and(row, BLOCK - 1)
  ki = jnp.bitwise_and(col, BLOCK - 1)
  is_cur = col >= BLOCK
  steps = jnp.where(is_cur, qi - ki, qi + BLOCK - ki)
  valid = jnp.logical_and(steps >= 0, steps <= BLOCK)
  slope = jnp.where(row >= BLOCK, slope1, slope0)
  dist = slope * steps.astype(F32)
  for p, (_, d) in enumerate(DIL_PATTERNS):
    base = jnp.where(valid, -d * dist, NEG_BIG)
    bias_sc[2 * p] = base
    bias_sc[2 * p + 1] = jnp.where(is_cur, base, NEG_BIG)


def _stack_heads(v2, head0):
  return jnp.concatenate([jnp.where(head0, v2, 0.0), jnp.where(head0, 0.0, v2)], axis=0)


def _unstack_heads(v, head0):
  return jnp.where(head0, v[:BLOCK], v[BLOCK:])


def _block_rows(idx, d, nblk):
  r = idx // nblk
  n = idx % nblk
  cur = _rows(r + n * (BLOCK * d), d)
  prev = _rows(r + jnp.maximum(n - 1, 0) * (BLOCK * d), d)
  return cur, prev, n


def pair_tile(dt):
  return pltpu.VMEM((2 * BLOCK, 2 * BLOCK), dt)


def _two_stage_loop(nb, first_stage, second_stage, bufs):
  a, b, c, d = bufs
  assert nb % 4 == 0 and nb >= 8

  def quad(u, carry):
    i = 4 * u
    first_stage(i + 2, c)
    first_stage(i + 3, d)
    second_stage(i, a)
    second_stage(i + 1, b)
    first_stage(i + 4, a)
    first_stage(i + 5, b)
    second_stage(i + 2, c)
    second_stage(i + 3, d)
    return carry

  first_stage(0, a)
  first_stage(1, b)
  lax.fori_loop(0, nb // 4 - 1, quad, 0)
  i = nb - 4
  first_stage(i + 2, c)
  first_stage(i + 3, d)
  for k, buf in enumerate(bufs):
    second_stage(i + k, buf)


def _attn_fwd(h3, name, gather=None):
  bl, s, _ = h3.shape
  npair = N_DIL_HEADS // 2
  nb = s // BLOCK
  hosted = gather is not None
  steps = bl * npair

  def body(*refs):
    if hosted:
      sl_ref, q_ref, k_ref, v_ref, w_ref, o_ref, lse_ref, g_ref, o_sc, l_sc, bias_sc, *s_bufs, send_sems, recv_sems = refs
      step = pl.program_id(0) * npair + pl.program_id(1)
      for phase, at in enumerate((0, (3 * steps) // 4)):
        @pl.when(step == at)
        def _(phase=phase):
          _relayed_gather_phase(phase, w_ref, g_ref, send_sems, recv_sems)
    else:
      sl_ref, q_ref, k_ref, v_ref, o_ref, lse_ref, o_sc, l_sc, bias_sc, *s_bufs = refs
    hp = pl.program_id(1)
    head0 = lax.broadcasted_iota(jnp.int32, (BLOCK, LANES), 1) < 64
    _fill_bias_tables(bias_sc, sl_ref[0, 2 * hp], sl_ref[0, 2 * hp + 1])

    for p, (_, d) in enumerate(DIL_PATTERNS):
      nblk = (s // d) // BLOCK
      two = nblk > 1
      ks = slice(0, 2 * BLOCK) if two else slice(BLOCK, 2 * BLOCK)

      def scores(idx, buf, p=p, d=d, nblk=nblk, two=two, ks=ks):
        cur, prev, n = _block_rows(idx, d, nblk)
        qs = _bf(_stack_heads(q_ref[cur, :], head0) * ATT_SCALE)
        kb = _bf(jnp.concatenate([k_ref[prev, :], k_ref[cur, :]], axis=0)) if two else _bf(k_ref[cur, :])
        first = jnp.where(n == 0, 1, 0) if two else 0
        buf[:, ks] = _dot_nt(qs, kb) + bias_sc[2 * p + first, :, ks]

      def values(idx, buf, p=p, d=d, nblk=nblk, two=two, ks=ks):
        cur, prev, _ = _block_rows(idx, d, nblk)
        sc = buf[:, ks]
        mx = jnp.max(sc, axis=1, keepdims=True)
        pe = jnp.exp(sc - mx)
        den = jnp.sum(pe, axis=1, keepdims=True)
        vb = _bf(jnp.concatenate([v_ref[prev, :], v_ref[cur, :]], axis=0)) if two else _bf(v_ref[cur, :])
        acc = _dot(_bf(pe), vb) / den
        o_sc[p, cur, :] = _unstack_heads(acc, head0)
        l_sc[p, cur, :] = _unstack_heads(jnp.broadcast_to(mx + jnp.log(den), (2 * BLOCK, LANES)), head0)

      _two_stage_loop(nb, scores, values, s_bufs)

    def merge(i, carry):
      rows = pl.ds(pl.multiple_of(i * BLOCK, BLOCK), BLOCK)
      l0, l1, l2 = l_sc[0, rows, :], l_sc[1, rows, :], l_sc[2, rows, :]
      mx = jnp.maximum(jnp.maximum(l0, l1), l2)
      e0, e1, e2 = jnp.exp(l0 - mx), jnp.exp(l1 - mx), jnp.exp(l2 - mx)
      tot = e0 + e1 + e2
      o_ref[rows, :] = _bf((e0 * o_sc[0, rows, :] + e1 * o_sc[1, rows, :] + e2 * o_sc[2, rows, :]) / tot)
      lse_ref[rows, :] = mx + jnp.log(tot)
      return carry

    lax.fori_loop(0, nb, merge, 0)

    if hosted:
      @pl.when(step == steps - 1)
      def _():
        _relayed_gather_phase(2, w_ref, g_ref, send_sems, recv_sems)

  def col(off):
    return pl.BlockSpec((None, s, LANES), lambda b, p: (b, 0, off + p))

  in_specs = [pl.BlockSpec(memory_space=pltpu.SMEM), col(0), col(npair), col(2 * npair)]
  out_specs = [col(0), col(0)]
  out_shape = [jax.ShapeDtypeStruct((bl, s, DIL_WIDTH), BF16), jax.ShapeDtypeStruct((bl, s, DIL_WIDTH), F32)]
  scratch = [pltpu.VMEM((3, s, LANES), F32), pltpu.VMEM((3, s, LANES), F32),
             pltpu.VMEM((6, 2 * BLOCK, 2 * BLOCK), F32)] + [pair_tile(F32)] * 4
  args = (_alibi_slopes(), h3, h3, h3)
  if hosted:
    assert (gather.shape[1] // 2) % ROW_ALIGN == 0 and steps >= 4
    in_specs.append(ANY)
    out_specs.append(ANY)
    out_shape.append(jax.ShapeDtypeStruct((4,) + gather.shape, gather.dtype))
    scratch += [pltpu.SemaphoreType.DMA((N_RELAY_COPIES,)), pltpu.SemaphoreType.DMA((N_RELAY_COPIES,))]
    args += (gather,)
  sem = ("arbitrary", "arbitrary") if hosted else ("parallel", "parallel")
  outs = list(pl.pallas_call(
      body, name=name, grid=(bl, npair), in_specs=in_specs, out_specs=out_specs, out_shape=out_shape,
      scratch_shapes=scratch, compiler_params=_params(sem),
  )(*args))
  if hosted:
    outs[2] = _place_own_block(outs[2], gather)
  return outs


def _attn_bwd(h3, out3, lse3, dcat3, name, exchange=None):
  bl, s, _ = h3.shape
  npair = N_DIL_HEADS // 2
  nb = s // BLOCK
  hosted = exchange is not None

  def body(*refs):
    if hosted:
      (sl_ref, q_ref, k_ref, v_ref, o_ref, l_ref, do_ref, ex_ref, dq_out, dk_out, dv_out, got_ref,
       bias_sc, *pd, prod_sc, dq_ref, dk_ref, dv_ref, send_sems, recv_sems) = refs
      step = pl.program_id(0) * npair + pl.program_id(1)

      @pl.when(step == 0)
      def _():
        for cp in _chip_exchange_copies(ex_ref, got_ref, send_sems, recv_sems):
          cp.start()
    else:
      (sl_ref, q_ref, k_ref, v_ref, o_ref, l_ref, do_ref, dq_out, dk_out, dv_out,
       bias_sc, *pd, prod_sc, dq_ref, dk_ref, dv_ref) = refs
    pd_bufs = list(zip(pd[0::2], pd[1::2]))
    hp = pl.program_id(1)
    lane = lax.broadcasted_iota(jnp.int32, (BLOCK, LANES), 1)
    head0 = lane < 64
    _fill_bias_tables(bias_sc, sl_ref[0, 2 * hp], sl_ref[0, 2 * hp + 1])
    dq_ref[...] = jnp.zeros_like(dq_ref)
    dk_ref[...] = jnp.zeros_like(dk_ref)
    dv_ref[...] = jnp.zeros_like(dv_ref)
    prod_sc[...] = do_ref[...] * o_ref[...].astype(F32)

    def per_row(v2, pick0, pick1):
      return jnp.concatenate([jnp.sum(jnp.where(pick0, v2, 0.0), axis=1, keepdims=True),
                              jnp.sum(jnp.where(pick1, v2, 0.0), axis=1, keepdims=True)], axis=0)

    for p, (_, d) in enumerate(DIL_PATTERNS):
      nblk = (s // d) // BLOCK
      two = nblk > 1
      ks = slice(0, 2 * BLOCK) if two else slice(BLOCK, 2 * BLOCK)

      def operands(idx, d=d, nblk=nblk, two=two):
        cur, prev, n = _block_rows(idx, d, nblk)
        qs = _bf(_stack_heads(q_ref[cur, :], head0) * ATT_SCALE)
        dos = _bf(_stack_heads(do_ref[cur, :], head0))
        kb = _bf(jnp.concatenate([k_ref[prev, :], k_ref[cur, :]], axis=0)) if two else _bf(k_ref[cur, :])
        return cur, prev, n, qs, dos, kb

      def probs(idx, bufs, p=p, two=two, ks=ks, operands=operands):
        cur, prev, n, qs, dos, kb = operands(idx)
        vb = _bf(jnp.concatenate([v_ref[prev, :], v_ref[cur, :]], axis=0)) if two else _bf(v_ref[cur, :])
        lse = per_row(l_ref[cur, :], lane == 0, lane == 64)
        delta = per_row(prod_sc[cur, :], head0, jnp.logical_not(head0))
        first = jnp.where(n == 0, 1, 0) if two else 0
        pr = jnp.exp(_dot_nt(qs, kb) + bias_sc[2 * p + first, :, ks] - lse)
        bufs[0][:, ks] = _bf(pr)
        bufs[1][:, ks] = _bf(pr * (_dot_nt(dos, vb) - delta))

      def products(idx, bufs, two=two, ks=ks, operands=operands):
        cur, prev, _, qs, dos, kb = operands(idx)
        pr = bufs[0][:, ks]
        ds = bufs[1][:, ks]
        dq_ref[cur, :] += _unstack_heads(_dot(ds, kb), head0) * ATT_SCALE
        dkb = _dot_tn(ds, qs)
        dvb = _dot_tn(pr, dos)
        if two:
          dk_ref[prev, :] += dkb[:BLOCK]
          dv_ref[prev, :] += dvb[:BLOCK]
          dk_ref[cur, :] += dkb[BLOCK:]
          dv_ref[cur, :] += dvb[BLOCK:]
        else:
          dk_ref[cur, :] += dkb
          dv_ref[cur, :] += dvb

      _two_stage_loop(nb, probs, products, pd_bufs)

    dq_out[...] = _bf(dq_ref[...])
    dk_out[...] = _bf(dk_ref[...])
    dv_out[...] = _bf(dv_ref[...])

    if hosted:
      @pl.when(step == bl * npair - 1)
      def _():
        for cp in _chip_exchange_copies(ex_ref, got_ref, send_sems, recv_sems):
          cp.wait()

  def col(off):
    return pl.BlockSpec((None, s, LANES), lambda b, p: (b, 0, off + p))

  sds = jax.ShapeDtypeStruct((bl, s, DIL_WIDTH), BF16)
  in_specs = [pl.BlockSpec(memory_space=pltpu.SMEM), col(0), col(npair), col(2 * npair), col(0), col(0), col(0)]
  out_specs, out_shape = [col(0), col(0), col(0)], [sds, sds, sds]
  scratch = [pltpu.VMEM((6, 2 * BLOCK, 2 * BLOCK), F32)] + [pair_tile(BF16)] * 8 + [pltpu.VMEM((s, LANES), F32)] * 4
  args = (_alibi_slopes(), h3, h3, h3, out3, lse3, dcat3)
  if hosted:
    in_specs.append(ANY)
    out_specs.append(ANY)
    out_shape.append(jax.ShapeDtypeStruct((3,) + exchange.shape[1:], exchange.dtype))
    scratch += [pltpu.SemaphoreType.DMA((3,)), pltpu.SemaphoreType.DMA((3,))]
    args += (exchange,)
  sem = ("arbitrary", "arbitrary") if hosted else ("parallel", "parallel")
  return pl.pallas_call(
      body, name=name, grid=(bl, npair), in_specs=in_specs, out_specs=out_specs, out_shape=out_shape,
      scratch_shapes=scratch, compiler_params=_params(sem),
  )(*args)


def _mem_heads(tq):
  lane = lax.broadcasted_iota(jnp.int32, (tq, LANES), 1)
  return lane < 64


def _mem_fwd(h3, qcol, mkv3, name, tq=512):
  bl, s, _ = h3.shape
  nm = mkv3.shape[1]
  tq = _tile(s, tq)

  def body(q_ref, kv_ref, o_ref):
    head0 = _mem_heads(tq)
    for lg in range(MEM_WIDTH // LANES):
      cs = slice(lg * LANES, (lg + 1) * LANES)
      q2 = q_ref[:, cs]
      mk = _bf(kv_ref[:, cs])
      mv = _bf(kv_ref[:, MEM_WIDTH + lg * LANES:MEM_WIDTH + (lg + 1) * LANES])
      outs = []
      for j in range(2):
        hm = head0 if j == 0 else jnp.logical_not(head0)
        qj = _bf(jnp.where(hm, q2, 0.0) * ATT_SCALE)
        sc = _dot_nt(qj, mk)
        mx = jnp.max(sc, axis=1, keepdims=True)
        pe = jnp.exp(sc - mx)
        den = jnp.sum(pe, axis=1, keepdims=True)
        outs.append(_dot(_bf(pe / den), mv))
      o_ref[:, cs] = _bf(jnp.where(head0, outs[0], outs[1]))

  return pl.pallas_call(
      body, name=name, grid=(bl, s // tq),
      in_specs=[pl.BlockSpec((None, tq, MEM_WIDTH), lambda b, i: (b, i, qcol)),
                pl.BlockSpec((None, nm, 2 * MEM_WIDTH), lambda b, i: (b, 0, 0))],
      out_specs=pl.BlockSpec((None, tq, MEM_WIDTH), lambda b, i: (b, i, 0)),
      out_shape=jax.ShapeDtypeStruct((bl, s, MEM_WIDTH), BF16),
      compiler_params=_params(("parallel", "parallel")),
  )(h3, mkv3)


def _mem_bwd(h3, qcol, mkv3, dcat3, name, tq=512):
  bl, s, _ = h3.shape
  nm = mkv3.shape[1]
  tq = _tile(s, tq)
  docol = dcat3.shape[2] // MEM_WIDTH - 1

  def body(q_ref, kv_ref, do_ref, dq_ref, dkv_ref):
    i = pl.program_id(1)

    @pl.when(i == 0)
    def _():
      dkv_ref[...] = jnp.zeros_like(dkv_ref)

    head0 = _mem_heads(tq)
    for lg in range(MEM_WIDTH // LANES):
      cs = slice(lg * LANES, (lg + 1) * LANES)
      vs = slice(MEM_WIDTH + lg * LANES, MEM_WIDTH + (lg + 1) * LANES)
      q2 = q_ref[:, cs]
      do2 = do_ref[:, cs]
      mk = _bf(kv_ref[:, cs])
      mv = _bf(kv_ref[:, vs])
      dq2 = jnp.zeros((tq, LANES), F32)
      dmk = jnp.zeros((nm, LANES), F32)
      dmv = jnp.zeros((nm, LANES), F32)
      for j in range(2):
        hm = head0 if j == 0 else jnp.logical_not(head0)
        qj = _bf(jnp.where(hm, q2, 0.0) * ATT_SCALE)
        doj = _bf(jnp.where(hm, do2, 0.0))
        sc = _dot_nt(qj, mk)
        mx = jnp.max(sc, axis=1, keepdims=True)
        pe = jnp.exp(sc - mx)
        pn = pe / jnp.sum(pe, axis=1, keepdims=True)
        pb = _bf(pn)
        dp = _dot_nt(doj, mv)
        dj = jnp.sum(pb.astype(F32) * dp, axis=1, keepdims=True)
        ds = _bf(pn * (dp - dj))
        dq2 = dq2 + jnp.where(hm, _dot(ds, mk), 0.0) * ATT_SCALE
        dmk = dmk + _dot_tn(ds, qj)
        dmv = dmv + _dot_tn(pb, doj)
      dq_ref[:, cs] = _bf(dq2)
      dkv_ref[:, cs] += dmk
      dkv_ref[:, vs] += dmv

  return pl.pallas_call(
      body, name=name, grid=(bl, s // tq),
      in_specs=[pl.BlockSpec((None, tq, MEM_WIDTH), lambda b, i: (b, i, qcol)),
                pl.BlockSpec((None, nm, 2 * MEM_WIDTH), lambda b, i: (b, 0, 0)),
                pl.BlockSpec((None, tq, MEM_WIDTH), lambda b, i: (b, i, docol))],
      out_specs=[pl.BlockSpec((None, tq, MEM_WIDTH), lambda b, i: (b, i, 0)),
                 pl.BlockSpec((None, nm, 2 * MEM_WIDTH), lambda b, i: (b, 0, 0))],
      out_shape=[jax.ShapeDtypeStruct((bl, s, MEM_WIDTH), BF16), jax.ShapeDtypeStruct((bl, nm, 2 * MEM_WIDTH), F32)],
      compiler_params=_params(("parallel", "arbitrary")),
  )(h3, mkv3, dcat3)


def _sgu_consts():
  ti = lax.broadcasted_iota(jnp.int32, (CHUNK, CHUNK), 0)
  si = lax.broadcasted_iota(jnp.int32, (CHUNK, CHUNK), 1)
  return si <= ti, si < 64


def _sgu_bias_lanes(b_s):
  return jnp.repeat(b_s.T, 64, axis=1)


def _sgu_fwd(h2, ln_g, ln_b, w_s, b_s, name, tr=512):
  t, _ = h2.shape
  tr = _tile(t, tr)
  nch = tr // CHUNK
  npair = N_SGU_GROUPS // 2

  def body(u_ref, v_ref, g_ref, b_ref, w_ref, bs_ref, o_ref, vn_sc):
    tril, head0 = _sgu_consts()
    xhat, _ = _ln_stats(_gelu(v_ref[...]))
    vn_sc[...] = _bf(xhat * g_ref[...] + b_ref[...])
    for jp in range(npair):
      cs = slice(jp * LANES, (jp + 1) * LANES)
      w0 = _bf(jnp.where(tril, w_ref[2 * jp], 0.0))
      w1 = _bf(jnp.where(tril, w_ref[2 * jp + 1], 0.0))
      bias = bs_ref[:, cs]
      for c in range(nch):
        rs = slice(c * CHUNK, (c + 1) * CHUNK)
        vb = vn_sc[rs, cs]
        mixed = jnp.where(head0, _dot(w0, vb), _dot(w1, vb)) + bias
        o_ref[rs, cs] = _bf(_gelu(u_ref[rs, cs]) * mixed)

  blk = lambda j: pl.BlockSpec((tr, SGU_WIDTH), lambda i: (i, j))
  vec = pl.BlockSpec((1, SGU_WIDTH), lambda i: (0, 0))
  return pl.pallas_call(
      body, name=name, grid=(t // tr,),
      in_specs=[blk(0), blk(1), vec, vec,
                pl.BlockSpec((N_SGU_GROUPS, CHUNK, CHUNK), lambda i: (0, 0, 0)),
                pl.BlockSpec((CHUNK, SGU_WIDTH), lambda i: (0, 0))],
      out_specs=blk(0), out_shape=jax.ShapeDtypeStruct((t, SGU_WIDTH), BF16),
      scratch_shapes=[pltpu.VMEM((tr, SGU_WIDTH), BF16)],
      compiler_params=_params(("parallel",)),
  )(h2, h2, ln_g.reshape(1, -1), ln_b.reshape(1, -1), w_s, _sgu_bias_lanes(b_s))


def _sgu_bwd(h2, dcat, ln_g, ln_b, w_s, b_s, name, tr=512):
  t, _ = h2.shape
  tr = _tile(t, tr)
  nch = tr // CHUNK
  npair = N_SGU_GROUPS // 2
  nsteps = t // tr

  def body(u_ref, v_ref, dm_ref, g_ref, b_ref, w_ref, bs_ref,
           du_ref, dv_ref, dw_ref, dbs_ref, dg_ref, db_ref, vn_sc, dmx_sc, dvn_sc, mix_sc, dbx_sc):
    i = pl.program_id(0)
    tril, head0 = _sgu_consts()

    @pl.when(i == 0)
    def _():
      dw_ref[...] = jnp.zeros_like(dw_ref)
      dg_ref[...] = jnp.zeros_like(dg_ref)
      db_ref[...] = jnp.zeros_like(db_ref)
      dbx_sc[...] = jnp.zeros_like(dbx_sc)

    gv, gv_der = _gelu_parts(v_ref[...])
    xhat, rstd = _ln_stats(gv)
    g = g_ref[...]
    vn_sc[...] = _bf(xhat * g + b_ref[...])
    gu, gu_der = _gelu_parts(u_ref[...])
    dmix = dm_ref[...]
    dmx_sc[...] = dmix * gu

    for jp in range(npair):
      cs = slice(jp * LANES, (jp + 1) * LANES)
      w0 = _bf(jnp.where(tril, w_ref[2 * jp], 0.0))
      w1 = _bf(jnp.where(tril, w_ref[2 * jp + 1], 0.0))
      bias = bs_ref[:, cs]
      dw0 = jnp.zeros((CHUNK, CHUNK), F32)
      dw1 = jnp.zeros((CHUNK, CHUNK), F32)
      dbx = jnp.zeros((CHUNK, LANES), F32)
      for c in range(nch):
        rs = slice(c * CHUNK, (c + 1) * CHUNK)
        vb = vn_sc[rs, cs]
        mix_sc[rs, cs] = jnp.where(head0, _dot(w0, vb), _dot(w1, vb)) + bias
        dmx = dmx_sc[rs, cs]
        d0 = _bf(jnp.where(head0, dmx, 0.0))
        d1 = _bf(jnp.where(head0, 0.0, dmx))
        dvn_sc[rs, cs] = _dot_tn(w0, d0) + _dot_tn(w1, d1)
        dw0 = dw0 + _dot_nt(d0, vb)
        dw1 = dw1 + _dot_nt(d1, vb)
        dbx = dbx + dmx
      dw_ref[2 * jp] += dw0
      dw_ref[2 * jp + 1] += dw1
      dbx_sc[:, cs] += dbx

    du_ref[...] = _bf(dmix * mix_sc[...] * gu_der)
    dvn = dvn_sc[...]
    dv_ref[...] = _bf(_ln_bwd(dvn, xhat, rstd, g) * gv_der)
    dg_ref[...] += jnp.sum(dvn * xhat, axis=0, keepdims=True)
    db_ref[...] += jnp.sum(dvn, axis=0, keepdims=True)

    @pl.when(i == nsteps - 1)
    def _():
      lane = lax.broadcasted_iota(jnp.int32, (CHUNK, LANES), 1)
      acc = jnp.zeros((CHUNK, LANES), F32)
      for gi in range(N_SGU_GROUPS):
        jp, j = gi // 2, gi % 2
        part = dbx_sc[:, jp * LANES:(jp + 1) * LANES]
        hm = (lane < 64) if j == 0 else (lane >= 64)
        colsum = jnp.sum(jnp.where(hm, part, 0.0), axis=1, keepdims=True)
        acc = jnp.where(lane == gi, colsum, acc)
        dw_ref[gi] = jnp.where(tril, dw_ref[gi], 0.0)
      dbs_ref[...] = acc

  blk = lambda j: pl.BlockSpec((tr, SGU_WIDTH), lambda i: (i, j))
  vec = pl.BlockSpec((1, SGU_WIDTH), lambda i: (0, 0))
  wspec = pl.BlockSpec((N_SGU_GROUPS, CHUNK, CHUNK), lambda i: (0, 0, 0))
  big = lambda dt: pltpu.VMEM((tr, SGU_WIDTH), dt)
  du, dv, dw, dbs, dg, db = pl.pallas_call(
      body, name=name, grid=(nsteps,),
      in_specs=[blk(0), blk(1), blk(0), vec, vec, wspec, pl.BlockSpec((CHUNK, SGU_WIDTH), lambda i: (0, 0))],
      out_specs=[blk(0), blk(0), wspec, pl.BlockSpec((CHUNK, LANES), lambda i: (0, 0)), vec, vec],
      out_shape=[jax.ShapeDtypeStruct((t, SGU_WIDTH), BF16), jax.ShapeDtypeStruct((t, SGU_WIDTH), BF16),
                 jax.ShapeDtypeStruct((N_SGU_GROUPS, CHUNK, CHUNK), F32), jax.ShapeDtypeStruct((CHUNK, LANES), F32),
                 jax.ShapeDtypeStruct((1, SGU_WIDTH), F32), jax.ShapeDtypeStruct((1, SGU_WIDTH), F32)],
      scratch_shapes=[big(BF16), big(F32), big(F32), big(F32), pltpu.VMEM((CHUNK, SGU_WIDTH), F32)],
      compiler_params=_params(("arbitrary",)),
  )(h2, h2, dcat, ln_g.reshape(1, -1), ln_b.reshape(1, -1), w_s, _sgu_bias_lanes(b_s))
  return du, dv, dw, dbs[:, :N_SGU_GROUPS].T, dg[0], db[0]


def _loss_head(xo, tgt, z, g, name, tm=512):
  m, d = xo.shape
  tm = _tile(m, tm)

  def body(x_ref, t_ref, z_ref, g_ref, l_ref, dz_ref, dzb_ref, dg_ref, dbias_ref):
    @pl.when(pl.program_id(0) == 0)
    def _():
      l_ref[...] = jnp.zeros_like(l_ref)

    diff = x_ref[...] - t_ref[...]
    rowsum = jnp.sum(diff * diff, axis=1, keepdims=True)
    tot = jnp.sum(rowsum, axis=0, keepdims=True) * (0.5 / d)
    l_ref[...] += jnp.broadcast_to(tot, l_ref.shape)
    _ln_bwd_tail(diff * (1.0 / d), z_ref, g_ref, dz_ref, dzb_ref, dg_ref, dbias_ref)

  row = pl.BlockSpec((tm, d), lambda i: (i, 0))
  vec = pl.BlockSpec((1, d), lambda i: (0, 0))
  out_specs, out_shape = _ln_bwd_outs(m, d, row, vec)
  l, dz, dzb, dg, dbias = pl.pallas_call(
      body, name=name, grid=(m // tm,), in_specs=[row, row, row, vec],
      out_specs=[pl.BlockSpec((8, LANES), lambda i: (0, 0))] + out_specs,
      out_shape=[jax.ShapeDtypeStruct((8, LANES), F32)] + out_shape,
      compiler_params=_params(("arbitrary",)),
  )(xo, tgt, z, g.reshape(1, d))
  return l[0, 0], dz, dzb, dg[0], dbias[0]


def _local_step(x3, mem3, tgt3, w, late_weights=None, early_exchange=None):
  w = dict(w)
  bl, s, d = x3.shape
  t = bl * s
  nm = mem3.shape[1]
  mem2 = mem3.reshape(bl * nm, d)
  x = x3.reshape(t, d)
  xb = x
  saved = []
  for i in range(DEPTH):
    j = i // 2
    attn = i % 2 == 0
    mkv = _mm(mem2, w["w_mem_kv"][i], "nn", F32, f"mkv_fwd_{i}", tm=1024, tn=512, tk=1024)
    mkv3 = mkv.reshape(bl, nm, 2 * MEM_WIDTH)
    w_in = w["a_w_in"][j] if attn else w["b_w_in"][j]
    h = _mm(xb, w_in, "nt", F32, f"in_proj_{i}", tm=512, tn=w_in.shape[0], tk=d)
    h3 = h.reshape(bl, s, -1)
    if attn and late_weights is not None and i in late_weights:
      mix3, lse3, gathered = _attn_fwd(h3, f"dil_attn_fwd_{i}", gather=late_weights[i].flat)
      for n, layers in late_weights[i].unpack(gathered).items():
        w[n] = {**w.get(n, {}), **layers}
    elif attn:
      mix3, lse3 = _attn_fwd(h3, f"dil_attn_fwd_{i}")
    if attn:
      mix = mix3.reshape(t, DIL_WIDTH)
      qcol = 3 * DIL_WIDTH // MEM_WIDTH
    else:
      mix = _sgu_fwd(h, w["sgu_ln_g"][j], w["sgu_ln_b"][j], w["sgu_w_s"][j], w["sgu_b_s"][j], f"sgu_fwd_{i}")
      lse3 = None
      qcol = 2 * SGU_WIDTH // MEM_WIDTH
    mo = _mem_fwd(h3, qcol, mkv3, f"mem_attn_fwd_{i}").reshape(t, MEM_WIDTH)
    cat = jnp.concatenate([mix, mo], axis=1)
    z1, xm, xmb = _mm_res_ln(cat, w["w_out"][i], x, w["ln_mix_g"][i], w["ln_mix_b"][i], f"out_proj_ln_{i}", tk=1024)
    ga, gb, hm = _ffn_up(xmb, w["w_gate"][i], w["w_up"][i], f"ffn_up_{i}")
    z2, xo, xob = _mm_res_ln(hm, w["w_down"][i], xm, w["ln_ffn_g"][i], w["ln_ffn_b"][i], f"ffn_down_ln_{i}", tk=hm.shape[1])
    saved.append(dict(xb=xb, h=h, h3=h3, mkv3=mkv3, mix3=(mix3 if attn else None), lse3=lse3, cat=cat, z1=z1,
                      xmb=xmb, ga=ga, gb=gb, hm=hm, z2=z2, qcol=qcol))
    x, xb = xo, xob

  names = ("a_w_in", "b_w_in", "sgu_ln_g", "sgu_ln_b", "sgu_w_s", "sgu_b_s", "w_mem_kv", "w_out",
           "ln_mix_g", "ln_mix_b", "w_gate", "w_up", "w_down", "ln_ffn_g", "ln_ffn_b")
  grads = {n: [None] * len(w[n]) for n in names}
  last = DEPTH - 1
  loss, dz2, dz2b, grads["ln_ffn_g"][last], grads["ln_ffn_b"][last] = _loss_head(
      x, tgt3.reshape(t, d), saved[last]["z2"], w["ln_ffn_g"][last], "loss_head")
  dx = None
  for i in reversed(range(DEPTH)):
    j = i // 2
    attn = i % 2 == 0
    sv = saved[i]
    da, db = _ffn_bwd_hidden(dz2b, w["w_down"][i], sv["ga"], sv["gb"], f"ffn_bwd_hidden_{i}")
    grads["w_down"][i] = _mm(sv["hm"], dz2b, "tn", F32, f"dw_down_{i}", tm=1408, tn=1024, tk=1024)
    grads["w_gate"][i] = _mm(da, sv["xmb"], "tn", F32, f"dw_gate_{i}", tm=1408, tn=1024, tk=1024)
    grads["w_up"][i] = _mm(db, sv["xmb"], "tn", F32, f"dw_up_{i}", tm=1408, tn=1024, tk=1024)
    dz1, dz1b, grads["ln_mix_g"][i], grads["ln_mix_b"][i] = _ffn_bwd_input_ln(
        da, db, w["w_gate"][i], w["w_up"][i], dz2, sv["z1"], w["ln_mix_g"][i], f"ffn_bwd_input_ln_{i}")
    grads["w_out"][i] = _mm(sv["cat"], dz1b, "tn", F32, f"dw_out_{i}", tm=1024, tn=1024, tk=1024)
    dcat = _mm(dz1b, w["w_out"][i], "nt", F32, f"out_proj_bwd_{i}", tm=1024, tn=1024, tk=1024)
    dcat3 = dcat.reshape(bl, s, -1)
    dqm3, dmkv3 = _mem_bwd(sv["h3"], sv["qcol"], sv["mkv3"], dcat3, f"mem_attn_bwd_{i}")
    grads["w_mem_kv"][i] = _mm(mem2, dmkv3.reshape(bl * nm, 2 * MEM_WIDTH), "tn", F32, f"dw_mem_kv_{i}", tm=1024, tn=512, tk=1024)
    dqm = dqm3.reshape(t, MEM_WIDTH)
    if attn and early_exchange is not None:
      begun = grads.setdefault("early_exchange", [])
      tag = f"early{i}"
      q, (pack, state) = early_exchange(grads, [e[0] for e in begun], i == 0, tag)
      dq3, dk3, dv3, x3 = _attn_bwd(sv["h3"], sv["mix3"], sv["lse3"], dcat3, f"dil_attn_bwd_{i}", exchange=q)
      begun.append((pack, state, x3, tag))
      parts = [dq3.reshape(t, -1), dk3.reshape(t, -1), dv3.reshape(t, -1), dqm]
    elif attn:
      dq3, dk3, dv3 = _attn_bwd(sv["h3"], sv["mix3"], sv["lse3"], dcat3, f"dil_attn_bwd_{i}")
      parts = [dq3.reshape(t, -1), dk3.reshape(t, -1), dv3.reshape(t, -1), dqm]
    else:
      du, dv, dws, dbs, dlg, dlb = _sgu_bwd(sv["h"], dcat, w["sgu_ln_g"][j], w["sgu_ln_b"][j], w["sgu_w_s"][j],
                                             w["sgu_b_s"][j], f"sgu_bwd_{i}")
      grads["sgu_w_s"][j], grads["sgu_b_s"][j], grads["sgu_ln_g"][j], grads["sgu_ln_b"][j] = dws, dbs, dlg, dlb
      parts = [du, dv, dqm]
    dh = jnp.concatenate(parts, axis=1)
    w_in = w["a_w_in"][j] if attn else w["b_w_in"][j]
    grads["a_w_in" if attn else "b_w_in"][j] = _mm(dh, sv["xb"], "tn", F32, f"dw_in_{i}", tm=1280 if attn else 896, tn=1024, tk=1024)
    if i > 0:
      dz2, dz2b, grads["ln_ffn_g"][i - 1], grads["ln_ffn_b"][i - 1] = _in_proj_bwd_ln(
          dh, w_in, dz1, saved[i - 1]["z2"], w["ln_ffn_g"][i - 1], f"in_proj_bwd_ln_{i}")
    else:
      dx = _mm(dh, w_in, "nn", F32, f"in_proj_bwd_{i}", add=dz1, add_scale=DN_ALPHA, tm=512, tn=d, tk=w_in.shape[0])
  return loss, dx.reshape(bl, s, d), grads


def _my_place():
  return lax.axis_index("x"), lax.axis_index("y"), lax.axis_index("c")


def _other_chips(x, y):
  return [(1 - x, y), (x, 1 - y), (1 - x, 1 - y)]


ANY = pl.BlockSpec(memory_space=pl.ANY)


def _all_gather_halves(wl, name):
  _, r, c_ = wl.shape

  def body(w_ref, g_ref, send_sems, recv_sems):
    x, y, c = _my_place()
    me = 2 * x + y
    sibling = (x, y, 1 - c)
    chips = _other_chips(x, y)

    def copy(k, src, dst, to):
      return pltpu.make_async_remote_copy(src_ref=src, dst_ref=dst, send_sem=send_sems.at[k], recv_sem=recv_sems.at[k],
                                          device_id=to, device_id_type=MESH_ID)

    first = [copy(k, w_ref.at[c], g_ref.at[me, c], (px, py, c)) for k, (px, py) in enumerate(chips)]
    for cp in first:
      cp.start()
    passed = []
    for k, (px, py) in enumerate(chips):
      landed = g_ref.at[2 * px + py, c]
      copy(k, landed, landed, (px, py, c)).wait_recv()
      fwd = copy(3 + k, landed, landed, sibling)
      fwd.start()
      passed.append(fwd)
    for k, (px, py) in enumerate(chips):
      theirs = g_ref.at[2 * px + py, 1 - c]
      copy(3 + k, theirs, theirs, sibling).wait_recv()
    for cp in first + passed:
      cp.wait_send()

  got = pl.pallas_call(
      body, name=name, in_specs=[ANY], out_specs=ANY,
      out_shape=jax.ShapeDtypeStruct((4, 2, r, c_), wl.dtype),
      scratch_shapes=[pltpu.SemaphoreType.DMA((6,)), pltpu.SemaphoreType.DMA((6,))],
  )(wl)
  chip = 2 * lax.axis_index("x") + lax.axis_index("y")
  return lax.dynamic_update_slice(got, wl[None], (chip, 0, 0, 0))


def _relayed_gather_phase(phase, w_ref, g_ref, send_sems, recv_sems):
  h = w_ref.shape[1] // 2
  x, y, c = _my_place()
  sibling = (x, y, 1 - c)
  xn, yn, dg = _other_chips(x, y)

  def copy(k, src, dst, to):
    return pltpu.make_async_remote_copy(src_ref=src, dst_ref=dst, send_sem=send_sems.at[k], recv_sem=recv_sems.at[k],
                                        device_id=to, device_id_type=MESH_ID)

  def block(chip, half):
    return g_ref.at[2 * chip[0] + chip[1], half]

  def same(k, ref, to):
    return copy(k, ref, ref, to)

  top, bottom = pl.ds(0, h), pl.ds(h, h)
  sends = [copy(0, w_ref.at[c], block((x, y), c), (*xn, c)), copy(1, w_ref.at[c], block((x, y), c), (*yn, c)),
           same(2, block(xn, c).at[top], (*yn, c)), same(3, block(yn, c).at[bottom], (*xn, c)),
           same(4, block(xn, c), sibling), same(5, block(yn, c), sibling), same(6, block(dg, c), sibling)]
  if phase == 0:
    sends[0].start()
    sends[1].start()
  elif phase == 1:
    same(0, block(xn, c), (*xn, c)).wait_recv()
    sends[2].start()
    sends[4].start()
    same(1, block(yn, c), (*yn, c)).wait_recv()
    sends[3].start()
    sends[5].start()
  else:
    same(2, block(dg, c).at[top], (*yn, c)).wait_recv()
    same(3, block(dg, c).at[bottom], (*xn, c)).wait_recv()
    sends[6].start()
    for k, chip in ((4, xn), (5, yn), (6, dg)):
      same(k, block(chip, 1 - c), sibling).wait_recv()
    for cp in sends:
      cp.wait_send()


N_RELAY_COPIES = 7


def _place_own_block(got, wl):
  chip = 2 * lax.axis_index("x") + lax.axis_index("y")
  return lax.dynamic_update_slice(got, wl[None], (chip, 0, 0, 0))


def _all_gather_relayed(wl, name):
  _, r, c_ = wl.shape
  assert (r // 2) % ROW_ALIGN == 0

  def body(w_ref, g_ref, send_sems, recv_sems):
    for phase in range(3):
      _relayed_gather_phase(phase, w_ref, g_ref, send_sems, recv_sems)

  got = pl.pallas_call(
      body, name=name, in_specs=[ANY], out_specs=ANY,
      out_shape=jax.ShapeDtypeStruct((4, 2, r, c_), wl.dtype),
      scratch_shapes=[pltpu.SemaphoreType.DMA((N_RELAY_COPIES,)), pltpu.SemaphoreType.DMA((N_RELAY_COPIES,))],
  )(wl)
  return _place_own_block(got, wl)


def _sibling_swap(v, name):
  def body(v_ref, o_ref, send_sem, recv_sem):
    x, y, c = _my_place()
    cp = pltpu.make_async_remote_copy(src_ref=v_ref, dst_ref=o_ref, send_sem=send_sem, recv_sem=recv_sem,
                                      device_id=(x, y, 1 - c), device_id_type=MESH_ID)
    cp.start()
    cp.wait()

  return pl.pallas_call(
      body, name=name, in_specs=[ANY], out_specs=ANY, out_shape=jax.ShapeDtypeStruct(v.shape, v.dtype),
      scratch_shapes=[pltpu.SemaphoreType.DMA, pltpu.SemaphoreType.DMA],
  )(v)


def _chip_exchange_copies(q_ref, o_ref, send_sems, recv_sems):
  x, y, c = _my_place()
  return [pltpu.make_async_remote_copy(src_ref=q_ref.at[2 * px + py], dst_ref=o_ref.at[k], send_sem=send_sems.at[k],
                                       recv_sem=recv_sems.at[k], device_id=(px, py, c), device_id_type=MESH_ID)
          for k, (px, py) in enumerate(_other_chips(x, y))]


def _chip_exchange(q, name):
  _, r, c_ = q.shape

  def body(q_ref, o_ref, send_sems, recv_sems):
    cps = _chip_exchange_copies(q_ref, o_ref, send_sems, recv_sems)
    for cp in cps:
      cp.start()
    for cp in cps:
      cp.wait()

  return pl.pallas_call(
      body, name=name, in_specs=[ANY], out_specs=ANY, out_shape=jax.ShapeDtypeStruct((3, r, c_), q.dtype),
      scratch_shapes=[pltpu.SemaphoreType.DMA((3,)), pltpu.SemaphoreType.DMA((3,))],
  )(q)


def _share_halves(both, name):
  _, r, c_ = both.shape

  def body(b_ref, o_ref, send_sem, recv_sem):
    x, y, c = _my_place()
    cp = pltpu.make_async_remote_copy(src_ref=b_ref.at[c], dst_ref=o_ref.at[c], send_sem=send_sem, recv_sem=recv_sem,
                                      device_id=(x, y, 1 - c), device_id_type=MESH_ID)
    cp.start()
    cp.wait()

  full = pl.pallas_call(
      body, name=name, in_specs=[ANY], out_specs=ANY, out_shape=jax.ShapeDtypeStruct(both.shape, both.dtype),
      input_output_aliases={0: 0},
      scratch_shapes=[pltpu.SemaphoreType.DMA, pltpu.SemaphoreType.DMA],
  )(both)
  return full.reshape(2 * r, c_)


def _half_spec(tr, c_, pick):
  return pl.BlockSpec((None, None, tr, c_), lambda s, r, place: (s, pick(place), r, 0))


def _cast_other_half(p, place, name, tr=512):
  _, _, r, c_ = p.shape
  tr = _tile(r, tr, 16)

  def body(place_ref, p_ref, o_ref):
    o_ref[...] = _bf(p_ref[...])

  out_spec = pl.BlockSpec((None, tr, c_), lambda s, rr, place: (s, rr, 0))
  return pl.pallas_call(
      body, name=name, out_shape=jax.ShapeDtypeStruct((4, r, c_), BF16),
      grid_spec=pltpu.PrefetchScalarGridSpec(num_scalar_prefetch=1, grid=(4, r // tr),
                                             in_specs=[_half_spec(tr, c_, lambda place: 1 - place[1])], out_specs=out_spec),
      compiler_params=_params(("parallel", "parallel")),
  )(place, p)


def _add_sibling(p, x1, place, name, tr=512):
  _, _, r, c_ = p.shape
  tr = _tile(r, tr, 16)

  def body(place_ref, p_ref, x_ref, o_ref):
    o_ref[...] = _bf(p_ref[...] + x_ref[...].astype(F32))

  row = pl.BlockSpec((None, tr, c_), lambda s, rr, place: (s, rr, 0))
  return pl.pallas_call(
      body, name=name, out_shape=jax.ShapeDtypeStruct((4, r, c_), BF16),
      grid_spec=pltpu.PrefetchScalarGridSpec(num_scalar_prefetch=1, grid=(4, r // tr),
                                             in_specs=[_half_spec(tr, c_, lambda place: place[1]), row], out_specs=row),
      compiler_params=_params(("parallel", "parallel")),
  )(place, p, x1)


def _sum_own(p, x1, x3, place, name, tr=512):
  _, _, r, c_ = p.shape
  tr = _tile(r, tr, 16)

  def body(place_ref, p_ref, x1_ref, x3_ref, o_ref):
    acc = p_ref[...] + x1_ref[...].astype(F32)
    for k in range(3):
      acc = acc + x3_ref[k].astype(F32)
    o_ref[...] = acc

  return pl.pallas_call(
      body, name=name, out_shape=jax.ShapeDtypeStruct((2, r, c_), F32),
      grid_spec=pltpu.PrefetchScalarGridSpec(
          num_scalar_prefetch=1, grid=(r // tr,),
          in_specs=[pl.BlockSpec((None, None, tr, c_), lambda rr, place: (place[0], place[1], rr, 0)),
                    pl.BlockSpec((None, tr, c_), lambda rr, place: (place[0], rr, 0)),
                    pl.BlockSpec((3, tr, c_), lambda rr, place: (0, rr, 0))],
          out_specs=pl.BlockSpec((None, tr, c_), lambda rr, place: (place[1], rr, 0))),
      compiler_params=_params(("parallel",)),
  )(place, p, x1, x3)


def _reduce_scatter_begin(p, tag):
  x, y, c = _my_place()
  place = jnp.stack([2 * x + y, c]).astype(jnp.int32)
  x1 = _sibling_swap(_cast_other_half(p, place, f"rs_cast_other_half_{tag}"), f"rs_sibling_swap_{tag}")
  return _add_sibling(p, x1, place, f"rs_add_sibling_{tag}"), (p, x1, place)


def _reduce_scatter_end(state, x3, tag):
  p, x1, place = state
  return _share_halves(_sum_own(p, x1, x3, place, f"rs_sum_own_{tag}"), f"rs_share_halves_{tag}")


def _adamw(w, g, m, v, name):
  shape = w.shape
  cols = shape[-1]
  rows = w.size // cols
  tr = _tile(rows, max(8, (256 * 1024) // cols // 8 * 8), 8)

  def body(w_ref, g_ref, m_ref, v_ref, d_ref, nm_ref, nv_ref):
    gv = g_ref[...]
    nm = ADAM_B1 * m_ref[...] + (1.0 - ADAM_B1) * gv
    nv = ADAM_B2 * v_ref[...] + (1.0 - ADAM_B2) * (gv * gv)
    m_hat = nm / (1.0 - ADAM_B1 ** ADAM_STEP)
    v_hat = nv / (1.0 - ADAM_B2 ** ADAM_STEP)
    d_ref[...] = -ADAM_LR * (m_hat / (jnp.sqrt(v_hat) + ADAM_EPS) + ADAM_WD * w_ref[...])
    nm_ref[...] = nm
    nv_ref[...] = nv

  spec = pl.BlockSpec((tr, cols), lambda i: (i, 0))
  sds = jax.ShapeDtypeStruct((rows, cols), F32)
  outs = pl.pallas_call(
      body, name=name, grid=(rows // tr,), in_specs=[spec] * 4, out_specs=[spec] * 3, out_shape=[sds] * 3,
      compiler_params=_params(("parallel",)),
  )(*(t.reshape(rows, cols) for t in (w, g, m, v)))
  return tuple(o.reshape(shape) for o in outs)


SHARDED = (("a_w_in", True), ("b_w_in", True), ("w_mem_kv", False), ("w_out", False), ("w_gate", True),
           ("w_up", True), ("w_down", False))
SMALL_SHARDED = (("sgu_ln_g", 1), ("sgu_ln_b", 1))
REPLICATED = ("sgu_w_s", "sgu_b_s", "ln_mix_g", "ln_mix_b", "ln_ffn_g", "ln_ffn_b")
SMALL_ORDER = ("sgu_w_s", "sgu_b_s", "ln_mix_g", "ln_mix_b", "ln_ffn_g", "ln_ffn_b", "sgu_ln_g", "sgu_ln_b")
ROW_ALIGN = 16


def _pad_to(v, n):
  return jnp.pad(v, (0, n - v.shape[0]))


def _round_up(n, a):
  return -(-n // a) * a


def _exchange_form(t, transposed):
  return jnp.swapaxes(t, 1, 2) if transposed else t


def _to_shard_major(full, axis):
  shp = full.shape
  cut = shp[:axis] + (4, shp[axis] // 4) + shp[axis + 1:]
  return jnp.moveaxis(full.reshape(cut), axis, 0).reshape(4, -1, FLAT_COLS)


def _from_shard_major(rows, shard_shape, axis):
  full = jnp.moveaxis(rows.reshape((4,) + tuple(shard_shape)), 0, axis)
  shp = full.shape
  return full.reshape(shp[:axis] + (shp[axis] * shp[axis + 1],) + shp[axis + 2:])


class _WeightPack:
  def __init__(self, items, small=()):
    segs, self.rows, self.small, off = [], {}, [], 0
    for n, l, b in items:
      seg = b.reshape(-1, FLAT_COLS)
      self.rows[(n, l)] = (off, seg.shape[0], b.shape)
      segs.append(seg)
      off += seg.shape[0]
    if small:
      flat = jnp.concatenate([lax.bitcast_convert_type(v, BF16).reshape(-1) for _, v in small])
      rows = _round_up(flat.shape[0], ROW_ALIGN * FLAT_COLS) // FLAT_COLS
      self.small = [(n, v.shape) for n, v in small]
      self.small_rows = (off, rows)
      segs.append(_pad_to(flat, rows * FLAT_COLS).reshape(rows, FLAT_COLS))
      off += rows
    rows_pad = _round_up(off, 4 * ROW_ALIGN)
    if rows_pad > off:
      segs.append(jnp.zeros((rows_pad - off, FLAT_COLS), BF16))
    self.flat = jnp.concatenate(segs).reshape(2, rows_pad // 2, FLAT_COLS)

  def unpack(self, gathered):
    g = gathered.reshape(4, -1, FLAT_COLS)
    out = {}
    for (n, l), (off, nr, shape) in self.rows.items():
      out.setdefault(n, {})[l] = g[:, off:off + nr].reshape((4 * shape[0],) + shape[1:])
    if self.small:
      off, rows = self.small_rows
      flat = g[:, off:off + rows].reshape(4, rows * FLAT_COLS)
      pos = 0
      for n, shape in self.small:
        sz = 2 * math.prod(shape)
        vals = lax.bitcast_convert_type(flat[:, pos:pos + sz].reshape((4,) + shape + (2,)), F32)
        out[n] = _from_shard_major(vals, shape, len(shape) - 1)
        pos += sz
    return out


FIRST_WEIGHTS = (("a_w_in", 0), ("w_mem_kv", 0))


def _weight_packs(shards):
  blocks = {(n, l): _exchange_form(shards[n], tr)[l].astype(BF16)
            for n, tr in SHARDED for l in range(shards[n].shape[0])}
  first = _WeightPack([(n, l, blocks[(n, l)]) for n, l in FIRST_WEIGHTS],
                      small=[(n, shards[n]) for n, _ in SMALL_SHARDED])

  def model_layer(n, l):
    return {"a_w_in": 2 * l, "b_w_in": 2 * l + 1}.get(n, l)

  def in_second(n, l):
    return model_layer(n, l) == 3 or (model_layer(n, l) == 2 and n not in ("a_w_in", "w_mem_kv"))

  rest = [(n, l, b) for (n, l), b in blocks.items() if (n, l) not in FIRST_WEIGHTS]
  return first, {0: _WeightPack([it for it in rest if not in_second(it[0], it[1])]),
                 2: _WeightPack([it for it in rest if in_second(it[0], it[1])])}


def _reduce_grads(grads, shard_shapes):
  done = [(pack, _reduce_scatter_end(state, x3, tag)) for pack, state, x3, tag in grads.pop("early_exchange")]
  sent = set().union(*[pack.rows.keys() for pack, _ in done])
  late = _GradPack([(n, l, g) for n, _ in SHARDED for l, g in enumerate(grads[n]) if (n, l) not in sent])
  q, state = _reduce_scatter_begin(late.p, "late")
  done.append((late, _reduce_scatter_end(state, _chip_exchange(q, "rs_chip_exchange_late"), "late")))

  def reduced(key):
    pack, mine = next((pack, mine) for pack, mine in done if key in pack.rows)
    off, nr = pack.rows[key]
    return mine[off:off + nr]

  out = {}
  for n, tr in SHARDED:
    layers, rows, cols = shard_shapes[n]
    blocks = [reduced((n, l)) for l in range(layers)]
    out[n] = jnp.stack([b.reshape(cols, rows).T if tr else b.reshape(rows, cols) for b in blocks])
  piece = reduced("small")
  quarter_rows = piece.shape[0]
  piece = piece.reshape(2, quarter_rows // 2, FLAT_COLS)
  small_sum = _all_gather_halves(piece, "gather_small_grads").reshape(-1)
  off = 0
  for n in SMALL_ORDER:
    shape = (len(grads[n]),) + grads[n][0].shape
    sz = math.prod(shape)
    out[n] = small_sum[off:off + sz].reshape(shape)
    off += sz
  return out


class _GradPack:
  def __init__(self, items, small=None):
    segs, self.rows, off = [], {}, 0
    for n, l, g in items:
      seg = _to_shard_major(g, 0)
      self.rows[(n, l)] = (off, seg.shape[1])
      segs.append(seg)
      off += seg.shape[1]
    if small is not None:
      flat = jnp.concatenate([jnp.stack(small[n]).reshape(-1) for n in SMALL_ORDER])
      n_small = _round_up(flat.shape[0], 4 * 2 * 8 * FLAT_COLS)
      quarter_rows = n_small // (4 * FLAT_COLS)
      self.rows["small"] = (off, quarter_rows)
      segs.append(_pad_to(flat, n_small).reshape(4, quarter_rows, FLAT_COLS))
      off += quarter_rows
    rows_pad = _round_up(off, 2 * ROW_ALIGN)
    if rows_pad > off:
      segs.append(jnp.zeros((4, rows_pad - off, FLAT_COLS), F32))
    self.p = jnp.concatenate(segs, axis=1).reshape(4, 2, rows_pad // 2, FLAT_COLS)


def _early_exchange_begin(grads, sent_packs, last, tag):
  sent = set().union(*[pack.rows.keys() for pack in sent_packs])
  items = [(n, l, g) for n, _ in SHARDED for l, g in enumerate(grads[n]) if g is not None and (n, l) not in sent]
  pack = _GradPack(items, small={n: grads[n] for n in SMALL_ORDER} if last else None)
  q, state = _reduce_scatter_begin(pack.p, tag)
  return q, (pack, state)


WEIGHT_NAMES = ("a_w_in", "b_w_in", "sgu_ln_g", "sgu_ln_b", "sgu_w_s", "sgu_b_s", "w_mem_kv", "w_out",
                "ln_mix_g", "ln_mix_b", "w_gate", "w_up", "w_down", "ln_ffn_g", "ln_ffn_b")


def kernel(x, mem, a_w_in, b_w_in, sgu_ln_g, sgu_ln_b, sgu_w_s, sgu_b_s, w_mem_kv, w_out, ln_mix_g, ln_mix_b, w_gate, w_up, w_down, ln_ffn_g, ln_ffn_b, loss_target, m_a_w_in, m_b_w_in, m_sgu_ln_g, m_sgu_ln_b, m_sgu_w_s, m_sgu_b_s, m_w_mem_kv, m_w_out, m_ln_mix_g, m_ln_mix_b, m_w_gate, m_w_up, m_w_down, m_ln_ffn_g, m_ln_ffn_b, v_a_w_in, v_b_w_in, v_sgu_ln_g, v_sgu_ln_b, v_sgu_w_s, v_sgu_b_s, v_w_mem_kv, v_w_out, v_ln_mix_g, v_ln_mix_b, v_w_gate, v_w_up, v_w_down, v_ln_ffn_g, v_ln_ffn_b):
  weights = dict(a_w_in=a_w_in, b_w_in=b_w_in, sgu_ln_g=sgu_ln_g, sgu_ln_b=sgu_ln_b, sgu_w_s=sgu_w_s, sgu_b_s=sgu_b_s,
                 w_mem_kv=w_mem_kv, w_out=w_out, ln_mix_g=ln_mix_g, ln_mix_b=ln_mix_b, w_gate=w_gate, w_up=w_up,
                 w_down=w_down, ln_ffn_g=ln_ffn_g, ln_ffn_b=ln_ffn_b)
  mom1 = dict(a_w_in=m_a_w_in, b_w_in=m_b_w_in, sgu_ln_g=m_sgu_ln_g, sgu_ln_b=m_sgu_ln_b, sgu_w_s=m_sgu_w_s,
              sgu_b_s=m_sgu_b_s, w_mem_kv=m_w_mem_kv, w_out=m_w_out, ln_mix_g=m_ln_mix_g, ln_mix_b=m_ln_mix_b,
              w_gate=m_w_gate, w_up=m_w_up, w_down=m_w_down, ln_ffn_g=m_ln_ffn_g, ln_ffn_b=m_ln_ffn_b)
  mom2 = dict(a_w_in=v_a_w_in, b_w_in=v_b_w_in, sgu_ln_g=v_sgu_ln_g, sgu_ln_b=v_sgu_ln_b, sgu_w_s=v_sgu_w_s,
              sgu_b_s=v_sgu_b_s, w_mem_kv=v_w_mem_kv, w_out=v_w_out, ln_mix_g=v_ln_mix_g, ln_mix_b=v_ln_mix_b,
              w_gate=v_w_gate, w_up=v_w_up, w_down=v_w_down, ln_ffn_g=v_ln_ffn_g, ln_ffn_b=v_ln_ffn_b)

  first, late = _weight_packs(weights)
  full = first.unpack(_all_gather_relayed(first.flat, "gather_first_weights"))
  for n in REPLICATED:
    full[n] = weights[n]
  loss_part, grad_x, grads = _local_step(x, mem, loss_target, full, late_weights=late,
                                         early_exchange=_early_exchange_begin)
  loss = lax.psum(loss_part, MESH_AXES)

  shard_shapes = {n: weights[n].shape for n, _ in SHARDED}
  red = _reduce_grads(grads, shard_shapes)
  chip = 2 * lax.axis_index("x") + lax.axis_index("y")
  for n, axis in SMALL_SHARDED:
    width = weights[n].shape[axis]
    red[n] = lax.dynamic_slice_in_dim(red[n], chip * width, width, axis)

  small_names = SMALL_ORDER
  def pack(d):
    flat = jnp.concatenate([d[n].reshape(-1) for n in small_names])
    return _pad_to(flat, _round_up(flat.shape[0], 8 * FLAT_COLS)).reshape(-1, FLAT_COLS)
  small_out = _adamw(pack(weights), pack(red), pack(mom1), pack(mom2), "adamw_small")
  delta, new_m, new_v = {}, {}, {}
  off = 0
  for n in small_names:
    sz = weights[n].size
    for dst, src in zip((delta, new_m, new_v), small_out):
      dst[n] = src.reshape(-1)[off:off + sz].reshape(weights[n].shape)
    off += sz
  for n, _ in SHARDED:
    delta[n], new_m[n], new_v[n] = _adamw(weights[n], red[n], mom1[n], mom2[n], f"adamw_{n}")

  return (loss, grad_x, *[red[n] for n in WEIGHT_NAMES], *[delta[n] for n in WEIGHT_NAMES],
          *[new_m[n] for n in WEIGHT_NAMES], *[new_v[n] for n in WEIGHT_NAMES])
```

```python
import math

import jax
import jax.numpy as jnp
from jax import lax
from jax.experimental import pallas as pl
from jax.experimental.pallas import tpu as pltpu

F32 = jnp.float32
BF16 = jnp.bfloat16

DEPTH = 4
HEAD_DIM = 64
N_DIL_HEADS = 12
DIL_WIDTH = N_DIL_HEADS * HEAD_DIM
DIL_PATTERNS = ((128, 1), (512, 4), (2048, 16))
BLOCK = 128
N_SGU_GROUPS = 12
SGU_WIDTH = N_SGU_GROUPS * 64
CHUNK = 128
N_MEM_HEADS = 4
MEM_WIDTH = N_MEM_HEADS * HEAD_DIM
DN_ALPHA = (2 * DEPTH) ** 0.25
LN_EPS = 1e-5
ATT_SCALE = HEAD_DIM ** -0.5
ADAM_LR = 0.001
ADAM_B1 = 0.9
ADAM_B2 = 0.999
ADAM_EPS = 1e-08
ADAM_WD = 0.01
ADAM_STEP = 10
NEG_BIG = -1e30

LANES = 128
FLAT_COLS = 1024
VMEM_LIMIT = 56 * 1024 * 1024
MESH_AXES = ("x", "y", "c")
MESH_ID = pl.DeviceIdType.MESH


def _tile(n, pref, align=LANES):
  if n <= pref:
    return n
  t = (pref // align) * align
  while t >= align:
    if n % t == 0:
      return t
    t -= align
  return n


def _params(sem):
  return pltpu.CompilerParams(dimension_semantics=sem, vmem_limit_bytes=VMEM_LIMIT)


def _dot(a, b):
  return jnp.dot(a, b, preferred_element_type=F32)


def _dot_nt(a, b):
  return lax.dot_general(a, b, (((1,), (1,)), ((), ())), preferred_element_type=F32)


def _dot_tn(a, b):
  return lax.dot_general(a, b, (((0,), (0,)), ((), ())), preferred_element_type=F32)


def _bf(v):
  return v.astype(BF16)


def _ln_stats(z):
  mu = jnp.mean(z, axis=-1, keepdims=True)
  zc = z - mu
  var = jnp.mean(zc * zc, axis=-1, keepdims=True)
  rstd = lax.rsqrt(var + LN_EPS)
  return zc * rstd, rstd


def _ln_bwd(dy, xhat, rstd, g):
  gdy = dy * g
  m1 = jnp.mean(gdy, axis=-1, keepdims=True)
  m2 = jnp.mean(gdy * xhat, axis=-1, keepdims=True)
  return rstd * (gdy - m1 - xhat * m2)


_GELU_C = math.sqrt(2.0 / math.pi)


def _gelu_parts(v):
  v2 = v * v
  t = jnp.tanh(_GELU_C * (v + 0.044715 * v * v2))
  val = 0.5 * v * (1.0 + t)
  der = 0.5 * (1.0 + t) + 0.5 * v * (1.0 - t * t) * (_GELU_C * (1.0 + 3.0 * 0.044715 * v2))
  return val, der


def _gelu(v):
  t = jnp.tanh(_GELU_C * (v + 0.044715 * v * v * v))
  return 0.5 * v * (1.0 + t)


def _sigmoid(v):
  return 1.0 / (1.0 + jnp.exp(-v))


def _mm(a, b, mode, out_dtype, name, add=None, add_scale=1.0, tm=512, tn=512, tk=512):
  if mode == "nn":
    (m, k), (k2, n) = a.shape, b.shape
  elif mode == "nt":
    (m, k), (n, k2) = a.shape, b.shape
  else:
    (k, m), (k2, n) = a.shape, b.shape
  assert k == k2, (a.shape, b.shape, mode)
  tm, tn, tk = _tile(m, tm), _tile(n, tn), _tile(k, tk)
  nk = k // tk
  if mode == "nn":
    a_spec = pl.BlockSpec((tm, tk), lambda i, j, kk: (i, kk))
    b_spec = pl.BlockSpec((tk, tn), lambda i, j, kk: (kk, j))
    dot = _dot
  elif mode == "nt":
    a_spec = pl.BlockSpec((tm, tk), lambda i, j, kk: (i, kk))
    b_spec = pl.BlockSpec((tn, tk), lambda i, j, kk: (j, kk))
    dot = _dot_nt
  else:
    a_spec = pl.BlockSpec((tk, tm), lambda i, j, kk: (kk, i))
    b_spec = pl.BlockSpec((tk, tn), lambda i, j, kk: (kk, j))
    dot = _dot_tn
  o_spec = pl.BlockSpec((tm, tn), lambda i, j, kk: (i, j))
  has_add = add is not None

  def body(*refs):
    if has_add:
      a_ref, b_ref, add_ref, o_ref, acc_ref = refs
    else:
      a_ref, b_ref, o_ref, acc_ref = refs
    kk = pl.program_id(2)

    @pl.when(kk == 0)
    def _():
      acc_ref[...] = jnp.zeros_like(acc_ref)

    acc_ref[...] += dot(_bf(a_ref[...]), _bf(b_ref[...]))

    @pl.when(kk == nk - 1)
    def _():
      r = acc_ref[...]
      if has_add:
        r = r + add_scale * add_ref[...].astype(F32)
      o_ref[...] = r.astype(out_dtype)

  in_specs = [a_spec, b_spec] + ([o_spec] if has_add else [])
  args = (a, b) + ((add,) if has_add else ())
  return pl.pallas_call(
      body, name=name, grid=(m // tm, n // tn, nk), in_specs=in_specs, out_specs=o_spec,
      out_shape=jax.ShapeDtypeStruct((m, n), out_dtype),
      scratch_shapes=[pltpu.VMEM((tm, tn), F32)],
      compiler_params=_params(("parallel", "parallel", "arbitrary")),
  )(*args)


def _mm_res_ln(a, w, res, g, b, name, tm=512, tk=512):
  m, k = a.shape
  d = w.shape[1]
  tm, tk = _tile(m, tm), _tile(k, tk)
  nk = k // tk

  def body(a_ref, w_ref, r_ref, g_ref, b_ref, z_ref, x_ref, xb_ref, acc_ref):
    kk = pl.program_id(1)

    @pl.when(kk == 0)
    def _():
      acc_ref[...] = jnp.zeros_like(acc_ref)

    acc_ref[...] += _dot(_bf(a_ref[...]), _bf(w_ref[...]))

    @pl.when(kk == nk - 1)
    def _():
      z = DN_ALPHA * r_ref[...] + acc_ref[...]
      xhat, _ = _ln_stats(z)
      xn = xhat * g_ref[...] + b_ref[...]
      z_ref[...] = z
      x_ref[...] = xn
      xb_ref[...] = _bf(xn)

  row = pl.BlockSpec((tm, d), lambda i, kk: (i, 0))
  vec = pl.BlockSpec((1, d), lambda i, kk: (0, 0))
  return pl.pallas_call(
      body, name=name, grid=(m // tm, nk),
      in_specs=[pl.BlockSpec((tm, tk), lambda i, kk: (i, kk)), pl.BlockSpec((tk, d), lambda i, kk: (kk, 0)), row, vec, vec],
      out_specs=[row, row, row],
      out_shape=[jax.ShapeDtypeStruct((m, d), F32), jax.ShapeDtypeStruct((m, d), F32), jax.ShapeDtypeStruct((m, d), BF16)],
      scratch_shapes=[pltpu.VMEM((tm, d), F32)],
      compiler_params=_params(("parallel", "arbitrary")),
  )(a, w, res, g.reshape(1, d), b.reshape(1, d))


def _ffn_up(xb, wg, wu, name, tm=512, tn=1408):
  m, d = xb.shape
  f = wg.shape[0]
  tm, tn = _tile(m, tm), _tile(f, tn)

  def body(x_ref, wg_ref, wu_ref, ga_ref, gb_ref, h_ref):
    xv = x_ref[...]
    a = _dot_nt(xv, wg_ref[...])
    b = _dot_nt(xv, wu_ref[...])
    sg = _sigmoid(a)
    silu = a * sg
    ga_ref[...] = _bf(b * (sg + silu * (1.0 - sg)))
    gb_ref[...] = _bf(silu)
    h_ref[...] = _bf(silu * b)

  wspec = pl.BlockSpec((tn, d), lambda j, i: (j, 0))
  ospec = pl.BlockSpec((tm, tn), lambda j, i: (i, j))
  sds = jax.ShapeDtypeStruct((m, f), BF16)
  return pl.pallas_call(
      body, name=name, grid=(f // tn, m // tm),
      in_specs=[pl.BlockSpec((tm, d), lambda j, i: (i, 0)), wspec, wspec],
      out_specs=[ospec, ospec, ospec], out_shape=[sds, sds, sds],
      compiler_params=_params(("parallel", "parallel")),
  )(xb, wg, wu)


def _ffn_bwd_hidden(dzb, wd, ga, gb, name, tm=512, tn=1408):
  m, d = dzb.shape
  f = wd.shape[0]
  tm, tn = _tile(m, tm), _tile(f, tn)

  def body(dz_ref, wd_ref, ga_ref, gb_ref, da_ref, db_ref):
    dh = _dot_nt(dz_ref[...], wd_ref[...])
    da_ref[...] = _bf(dh * ga_ref[...].astype(F32))
    db_ref[...] = _bf(dh * gb_ref[...].astype(F32))

  hspec = pl.BlockSpec((tm, tn), lambda j, i: (i, j))
  sds = jax.ShapeDtypeStruct((m, f), BF16)
  return pl.pallas_call(
      body, name=name, grid=(f // tn, m // tm),
      in_specs=[pl.BlockSpec((tm, d), lambda j, i: (i, 0)), pl.BlockSpec((tn, d), lambda j, i: (j, 0)), hspec, hspec],
      out_specs=[hspec, hspec], out_shape=[sds, sds],
      compiler_params=_params(("parallel", "parallel")),
  )(dzb, wd, ga, gb)


def _ln_bwd_tail(dy, z_ref, g_ref, dz_ref, dzb_ref, dg_ref, db_ref):
  @pl.when(pl.program_id(0) == 0)
  def _():
    dg_ref[...] = jnp.zeros_like(dg_ref)
    db_ref[...] = jnp.zeros_like(db_ref)

  xhat, rstd = _ln_stats(z_ref[...])
  dz = _ln_bwd(dy, xhat, rstd, g_ref[...])
  dz_ref[...] = dz
  dzb_ref[...] = _bf(dz)
  dg_ref[...] += jnp.sum(dy * xhat, axis=0, keepdims=True)
  db_ref[...] += jnp.sum(dy, axis=0, keepdims=True)


def _ln_bwd_outs(m, d, row, vec):
  return ([row, row, vec, vec],
          [jax.ShapeDtypeStruct((m, d), F32), jax.ShapeDtypeStruct((m, d), BF16),
           jax.ShapeDtypeStruct((1, d), F32), jax.ShapeDtypeStruct((1, d), F32)])


def _ffn_bwd_input_ln(da, db, wg, wu, dz2, z1, g, name, tm=512):
  m, f = da.shape
  d = wg.shape[1]
  tm = _tile(m, tm)

  def body(da_ref, db_ref, wg_ref, wu_ref, dz2_ref, z_ref, g_ref, dz_ref, dzb_ref, dg_ref, dbias_ref):
    dy = DN_ALPHA * dz2_ref[...] + _dot(da_ref[...], wg_ref[...]) + _dot(db_ref[...], wu_ref[...])
    _ln_bwd_tail(dy, z_ref, g_ref, dz_ref, dzb_ref, dg_ref, dbias_ref)

  hspec = pl.BlockSpec((tm, f), lambda i: (i, 0))
  wspec = pl.BlockSpec((f, d), lambda i: (0, 0), pipeline_mode=pl.Buffered(1))
  row = pl.BlockSpec((tm, d), lambda i: (i, 0))
  vec = pl.BlockSpec((1, d), lambda i: (0, 0))
  out_specs, out_shape = _ln_bwd_outs(m, d, row, vec)
  dz, dzb, dg, dbias = pl.pallas_call(
      body, name=name, grid=(m // tm,), in_specs=[hspec, hspec, wspec, wspec, row, row, vec],
      out_specs=out_specs, out_shape=out_shape, compiler_params=_params(("arbitrary",)),
  )(da, db, wg, wu, dz2, z1, g.reshape(1, d))
  return dz, dzb, dg[0], dbias[0]


def _in_proj_bwd_ln(dh, w_in, dz1, z2, g, name, tm=512):
  m, wd = dh.shape
  d = w_in.shape[1]
  tm = _tile(m, tm)

  def body(dh_ref, w_ref, dz1_ref, z_ref, g_ref, dz_ref, dzb_ref, dg_ref, dbias_ref):
    dy = DN_ALPHA * dz1_ref[...] + _dot(dh_ref[...], w_ref[...])
    _ln_bwd_tail(dy, z_ref, g_ref, dz_ref, dzb_ref, dg_ref, dbias_ref)

  row = pl.BlockSpec((tm, d), lambda i: (i, 0))
  vec = pl.BlockSpec((1, d), lambda i: (0, 0))
  out_specs, out_shape = _ln_bwd_outs(m, d, row, vec)
  dz, dzb, dg, dbias = pl.pallas_call(
      body, name=name, grid=(m // tm,),
      in_specs=[pl.BlockSpec((tm, wd), lambda i: (i, 0)),
                pl.BlockSpec((wd, d), lambda i: (0, 0), pipeline_mode=pl.Buffered(1)), row, row, vec],
      out_specs=out_specs, out_shape=out_shape, compiler_params=_params(("arbitrary",)),
  )(dh, w_in, dz1, z2, g.reshape(1, d))
  return dz, dzb, dg[0], dbias[0]


def _alibi_slopes():
  n = N_DIL_HEADS
  return jnp.exp2(-8.0 * (jnp.arange(n, dtype=F32) + 1.0) / n).reshape(1, n)


def _rows(start, d):
  if d == 1:
    return pl.ds(pl.multiple_of(start, BLOCK), BLOCK)
  return pl.ds(start, BLOCK, stride=d)


def _fill_bias_tables(bias_sc, slope0, slope1):
  row = lax.broadcasted_iota(jnp.int32, (2 * BLOCK, 2 * BLOCK), 0)
  col = lax.broadcasted_iota(jnp.int32, (2 * BLOCK, 2 * BLOCK), 1)
  qi = jnp.bitwise_and(row, BLOCK - 1)
  ki = jnp.bitwise_and(col, BLOCK - 1)
  is_cur = col >= BLOCK
  steps = jnp.where(is_cur, qi - ki, qi + BLOCK - ki)
  valid = jnp.logical_and(steps >= 0, steps <= BLOCK)
  slope = jnp.where(row >= BLOCK, slope1, slope0)
  dist = slope * steps.astype(F32)
  for p, (_, d) in enumerate(DIL_PATTERNS):
    base = jnp.where(valid, -d * dist, NEG_BIG)
    bias_sc[2 * p] = base
    bias_sc[2 * p + 1] = jnp.where(is_cur, base, NEG_BIG)


def _stack_heads(v2, head0):
  return jnp.concatenate([jnp.where(head0, v2, 0.0), jnp.where(head0, 0.0, v2)], axis=0)


def _unstack_heads(v, head0):
  return jnp.where(head0, v[:BLOCK], v[BLOCK:])


def _block_rows(idx, d, nblk):
  r = idx // nblk
  n = idx % nblk
  cur = _rows(r + n * (BLOCK * d), d)
  prev = _rows(r + jnp.maximum(n - 1, 0) * (BLOCK * d), d)
  return cur, prev, n


def pair_tile(dt):
  return pltpu.VMEM((2 * BLOCK, 2 * BLOCK), dt)


def _two_stage_loop(nb, first_stage, second_stage, bufs):
  a, b, c, d = bufs
  assert nb % 4 == 0 and nb >= 8

  def quad(u, carry):
    i = 4 * u
    first_stage(i + 2, c)
    first_stage(i + 3, d)
    second_stage(i, a)
    second_stage(i + 1, b)
    first_stage(i + 4, a)
    first_stage(i + 5, b)
    second_stage(i + 2, c)
    second_stage(i + 3, d)
    return carry

  first_stage(0, a)
  first_stage(1, b)
  lax.fori_loop(0, nb // 4 - 1, quad, 0)
  i = nb - 4
  first_stage(i + 2, c)
  first_stage(i + 3, d)
  for k, buf in enumerate(bufs):
    second_stage(i + k, buf)


def _attn_fwd(h3, name, gather=None):
  bl, s, _ = h3.shape
  npair = N_DIL_HEADS // 2
  nb = s // BLOCK
  hosted = gather is not None
  steps = bl * npair

  def body(*refs):
    if hosted:
      sl_ref, q_ref, k_ref, v_ref, w_ref, o_ref, lse_ref, g_ref, o_sc, l_sc, bias_sc, *s_bufs, send_sems, recv_sems = refs
      step = pl.program_id(0) * npair + pl.program_id(1)
      for phase, at in enumerate((0, (3 * steps) // 4)):
        @pl.when(step == at)
        def _(phase=phase):
          _relayed_gather_phase(phase, w_ref, g_ref, send_sems, recv_sems)
    else:
      sl_ref, q_ref, k_ref, v_ref, o_ref, lse_ref, o_sc, l_sc, bias_sc, *s_bufs = refs
    hp = pl.program_id(1)
    head0 = lax.broadcasted_iota(jnp.int32, (BLOCK, LANES), 1) < 64
    _fill_bias_tables(bias_sc, sl_ref[0, 2 * hp], sl_ref[0, 2 * hp + 1])

    for p, (_, d) in enumerate(DIL_PATTERNS):
      nblk = (s // d) // BLOCK
      two = nblk > 1
      ks = slice(0, 2 * BLOCK) if two else slice(BLOCK, 2 * BLOCK)

      def scores(idx, buf, p=p, d=d, nblk=nblk, two=two, ks=ks):
        cur, prev, n = _block_rows(idx, d, nblk)
        qs = _bf(_stack_heads(q_ref[cur, :], head0) * ATT_SCALE)
        kb = _bf(jnp.concatenate([k_ref[prev, :], k_ref[cur, :]], axis=0)) if two else _bf(k_ref[cur, :])
        first = jnp.where(n == 0, 1, 0) if two else 0
        buf[:, ks] = _dot_nt(qs, kb) + bias_sc[2 * p + first, :, ks]

      def values(idx, buf, p=p, d=d, nblk=nblk, two=two, ks=ks):
        cur, prev, _ = _block_rows(idx, d, nblk)
        sc = buf[:, ks]
        mx = jnp.max(sc, axis=1, keepdims=True)
        pe = jnp.exp(sc - mx)
        den = jnp.sum(pe, axis=1, keepdims=True)
        vb = _bf(jnp.concatenate([v_ref[prev, :], v_ref[cur, :]], axis=0)) if two else _bf(v_ref[cur, :])
        acc = _dot(_bf(pe), vb) / den
        o_sc[p, cur, :] = _unstack_heads(acc, head0)
        l_sc[p, cur, :] = _unstack_heads(jnp.broadcast_to(mx + jnp.log(den), (2 * BLOCK, LANES)), head0)

      _two_stage_loop(nb, scores, values, s_bufs)

    def merge(i, carry):
      rows = pl.ds(pl.multiple_of(i * BLOCK, BLOCK), BLOCK)
      l0, l1, l2 = l_sc[0, rows, :], l_sc[1, rows, :], l_sc[2, rows, :]
      mx = jnp.maximum(jnp.maximum(l0, l1), l2)
      e0, e1, e2 = jnp.exp(l0 - mx), jnp.exp(l1 - mx), jnp.exp(l2 - mx)
      tot = e0 + e1 + e2
      o_ref[rows, :] = _bf((e0 * o_sc[0, rows, :] + e1 * o_sc[1, rows, :] + e2 * o_sc[2, rows, :]) / tot)
      lse_ref[rows, :] = mx + jnp.log(tot)
      return carry

    lax.fori_loop(0, nb, merge, 0)

    if hosted:
      @pl.when(step == steps - 1)
      def _():
        _relayed_gather_phase(2, w_ref, g_ref, send_sems, recv_sems)

  def col(off):
    return pl.BlockSpec((None, s, LANES), lambda b, p: (b, 0, off + p))

  in_specs = [pl.BlockSpec(memory_space=pltpu.SMEM), col(0), col(npair), col(2 * npair)]
  out_specs = [col(0), col(0)]
  out_shape = [jax.ShapeDtypeStruct((bl, s, DIL_WIDTH), BF16), jax.ShapeDtypeStruct((bl, s, DIL_WIDTH), F32)]
  scratch = [pltpu.VMEM((3, s, LANES), F32), pltpu.VMEM((3, s, LANES), F32),
             pltpu.VMEM((6, 2 * BLOCK, 2 * BLOCK), F32)] + [pair_tile(F32)] * 4
  args = (_alibi_slopes(), h3, h3, h3)
  if hosted:
    assert (gather.shape[1] // 2) % ROW_ALIGN == 0 and steps >= 4
    in_specs.append(ANY)
    out_specs.append(ANY)
    out_shape.append(jax.ShapeDtypeStruct((4,) + gather.shape, gather.dtype))
    scratch += [pltpu.SemaphoreType.DMA((N_RELAY_COPIES,)), pltpu.SemaphoreType.DMA((N_RELAY_COPIES,))]
    args += (gather,)
  sem = ("arbitrary", "arbitrary") if hosted else ("parallel", "parallel")
  outs = list(pl.pallas_call(
      body, name=name, grid=(bl, npair), in_specs=in_specs, out_specs=out_specs, out_shape=out_shape,
      scratch_shapes=scratch, compiler_params=_params(sem),
  )(*args))
  if hosted:
    outs[2] = _place_own_block(outs[2], gather)
  return outs


def _attn_bwd(h3, out3, lse3, dcat3, name, exchange=None):
  bl, s, _ = h3.shape
  npair = N_DIL_HEADS // 2
  nb = s // BLOCK
  hosted = exchange is not None

  def body(*refs):
    if hosted:
      (sl_ref, q_ref, k_ref, v_ref, o_ref, l_ref, do_ref, ex_ref, dq_out, dk_out, dv_out, got_ref,
       bias_sc, *pd, prod_sc, dq_ref, dk_ref, dv_ref, send_sems, recv_sems) = refs
      step = pl.program_id(0) * npair + pl.program_id(1)

      @pl.when(step == 0)
      def _():
        for cp in _chip_exchange_copies(ex_ref, got_ref, send_sems, recv_sems):
          cp.start()
    else:
      (sl_ref, q_ref, k_ref, v_ref, o_ref, l_ref, do_ref, dq_out, dk_out, dv_out,
       bias_sc, *pd, prod_sc, dq_ref, dk_ref, dv_ref) = refs
    pd_bufs = list(zip(pd[0::2], pd[1::2]))
    hp = pl.program_id(1)
    lane = lax.broadcasted_iota(jnp.int32, (BLOCK, LANES), 1)
    head0 = lane < 64
    _fill_bias_tables(bias_sc, sl_ref[0, 2 * hp], sl_ref[0, 2 * hp + 1])
    dq_ref[...] = jnp.zeros_like(dq_ref)
    dk_ref[...] = jnp.zeros_like(dk_ref)
    dv_ref[...] = jnp.zeros_like(dv_ref)
    prod_sc[...] = do_ref[...] * o_ref[...].astype(F32)

    def per_row(v2, pick0, pick1):
      return jnp.concatenate([jnp.sum(jnp.where(pick0, v2, 0.0), axis=1, keepdims=True),
                              jnp.sum(jnp.where(pick1, v2, 0.0), axis=1, keepdims=True)], axis=0)

    for p, (_, d) in enumerate(DIL_PATTERNS):
      nblk = (s // d) // BLOCK
      two = nblk > 1
      ks = slice(0, 2 * BLOCK) if two else slice(BLOCK, 2 * BLOCK)

      def operands(idx, d=d, nblk=nblk, two=two):
        cur, prev, n = _block_rows(idx, d, nblk)
        qs = _bf(_stack_heads(q_ref[cur, :], head0) * ATT_SCALE)
        dos = _bf(_stack_heads(do_ref[cur, :], head0))
        kb = _bf(jnp.concatenate([k_ref[prev, :], k_ref[cur, :]], axis=0)) if two else _bf(k_ref[cur, :])
        return cur, prev, n, qs, dos, kb

      def probs(idx, bufs, p=p, two=two, ks=ks, operands=operands):
        cur, prev, n, qs, dos, kb = operands(idx)
        vb = _bf(jnp.concatenate([v_ref[prev, :], v_ref[cur, :]], axis=0)) if two else _bf(v_ref[cur, :])
        lse = per_row(l_ref[cur, :], lane == 0, lane == 64)
        delta = per_row(prod_sc[cur, :], head0, jnp.logical_not(head0))
        first = jnp.where(n == 0, 1, 0) if two else 0
        pr = jnp.exp(_dot_nt(qs, kb) + bias_sc[2 * p + first, :, ks] - lse)
        bufs[0][:, ks] = _bf(pr)
        bufs[1][:, ks] = _bf(pr * (_dot_nt(dos, vb) - delta))

      def products(idx, bufs, two=two, ks=ks, operands=operands):
        cur, prev, _, qs, dos, kb = operands(idx)
        pr = bufs[0][:, ks]
        ds = bufs[1][:, ks]
        dq_ref[cur, :] += _unstack_heads(_dot(ds, kb), head0) * ATT_SCALE
        dkb = _dot_tn(ds, qs)
        dvb = _dot_tn(pr, dos)
        if two:
          dk_ref[prev, :] += dkb[:BLOCK]
          dv_ref[prev, :] += dvb[:BLOCK]
          dk_ref[cur, :] += dkb[BLOCK:]
          dv_ref[cur, :] += dvb[BLOCK:]
        else:
          dk_ref[cur, :] += dkb
          dv_ref[cur, :] += dvb

      _two_stage_loop(nb, probs, products, pd_bufs)

    dq_out[...] = _bf(dq_ref[...])
    dk_out[...] = _bf(dk_ref[...])
    dv_out[...] = _bf(dv_ref[...])

    if hosted:
      @pl.when(step == bl * npair - 1)
      def _():
        for cp in _chip_exchange_copies(ex_ref, got_ref, send_sems, recv_sems):
          cp.wait()

  def col(off):
    return pl.BlockSpec((None, s, LANES), lambda b, p: (b, 0, off + p))

  sds = jax.ShapeDtypeStruct((bl, s, DIL_WIDTH), BF16)
  in_specs = [pl.BlockSpec(memory_space=pltpu.SMEM), col(0), col(npair), col(2 * npair), col(0), col(0), col(0)]
  out_specs, out_shape = [col(0), col(0), col(0)], [sds, sds, sds]
  scratch = [pltpu.VMEM((6, 2 * BLOCK, 2 * BLOCK), F32)] + [pair_tile(BF16)] * 8 + [pltpu.VMEM((s, LANES), F32)] * 4
  args = (_alibi_slopes(), h3, h3, h3, out3, lse3, dcat3)
  if hosted:
    in_specs.append(ANY)
    out_specs.append(ANY)
    out_shape.append(jax.ShapeDtypeStruct((3,) + exchange.shape[1:], exchange.dtype))
    scratch += [pltpu.SemaphoreType.DMA((3,)), pltpu.SemaphoreType.DMA((3,))]
    args += (exchange,)
  sem = ("arbitrary", "arbitrary") if hosted else ("parallel", "parallel")
  return pl.pallas_call(
      body, name=name, grid=(bl, npair), in_specs=in_specs, out_specs=out_specs, out_shape=out_shape,
      scratch_shapes=scratch, compiler_params=_params(sem),
  )(*args)


def _mem_heads(tq):
  lane = lax.broadcasted_iota(jnp.int32, (tq, LANES), 1)
  return lane < 64


def _mem_fwd(h3, qcol, mkv3, name, tq=512):
  bl, s, _ = h3.shape
  nm = mkv3.shape[1]
  tq = _tile(s, tq)

  def body(q_ref, kv_ref, o_ref):
    head0 = _mem_heads(tq)
    for lg in range(MEM_WIDTH // LANES):
      cs = slice(lg * LANES, (lg + 1) * LANES)
      q2 = q_ref[:, cs]
      mk = _bf(kv_ref[:, cs])
      mv = _bf(kv_ref[:, MEM_WIDTH + lg * LANES:MEM_WIDTH + (lg + 1) * LANES])
      outs = []
      for j in range(2):
        hm = head0 if j == 0 else jnp.logical_not(head0)
        qj = _bf(jnp.where(hm, q2, 0.0) * ATT_SCALE)
        sc = _dot_nt(qj, mk)
        mx = jnp.max(sc, axis=1, keepdims=True)
        pe = jnp.exp(sc - mx)
        den = jnp.sum(pe, axis=1, keepdims=True)
        outs.append(_dot(_bf(pe / den), mv))
      o_ref[:, cs] = _bf(jnp.where(head0, outs[0], outs[1]))

  return pl.pallas_call(
      body, name=name, grid=(bl, s // tq),
      in_specs=[pl.BlockSpec((None, tq, MEM_WIDTH), lambda b, i: (b, i, qcol)),
                pl.BlockSpec((None, nm, 2 * MEM_WIDTH), lambda b, i: (b, 0, 0))],
      out_specs=pl.BlockSpec((None, tq, MEM_WIDTH), lambda b, i: (b, i, 0)),
      out_shape=jax.ShapeDtypeStruct((bl, s, MEM_WIDTH), BF16),
      compiler_params=_params(("parallel", "parallel")),
  )(h3, mkv3)


def _mem_bwd(h3, qcol, mkv3, dcat3, name, tq=512):
  bl, s, _ = h3.shape
  nm = mkv3.shape[1]
  tq = _tile(s, tq)
  docol = dcat3.shape[2] // MEM_WIDTH - 1

  def body(q_ref, kv_ref, do_ref, dq_ref, dkv_ref):
    i = pl.program_id(1)

    @pl.when(i == 0)
    def _():
      dkv_ref[...] = jnp.zeros_like(dkv_ref)

    head0 = _mem_heads(tq)
    for lg in range(MEM_WIDTH // LANES):
      cs = slice(lg * LANES, (lg + 1) * LANES)
      vs = slice(MEM_WIDTH + lg * LANES, MEM_WIDTH + (lg + 1) * LANES)
      q2 = q_ref[:, cs]
      do2 = do_ref[:, cs]
      mk = _bf(kv_ref[:, cs])
      mv = _bf(kv_ref[:, vs])
      dq2 = jnp.zeros((tq, LANES), F32)
      dmk = jnp.zeros((nm, LANES), F32)
      dmv = jnp.zeros((nm, LANES), F32)
      for j in range(2):
        hm = head0 if j == 0 else jnp.logical_not(head0)
        qj = _bf(jnp.where(hm, q2, 0.0) * ATT_SCALE)
        doj = _bf(jnp.where(hm, do2, 0.0))
        sc = _dot_nt(qj, mk)
        mx = jnp.max(sc, axis=1, keepdims=True)
        pe = jnp.exp(sc - mx)
        pn = pe / jnp.sum(pe, axis=1, keepdims=True)
        pb = _bf(pn)
        dp = _dot_nt(doj, mv)
        dj = jnp.sum(pb.astype(F32) * dp, axis=1, keepdims=True)
        ds = _bf(pn * (dp - dj))
        dq2 = dq2 + jnp.where(hm, _dot(ds, mk), 0.0) * ATT_SCALE
        dmk = dmk + _dot_tn(ds, qj)
        dmv = dmv + _dot_tn(pb, doj)
      dq_ref[:, cs] = _bf(dq2)
      dkv_ref[:, cs] += dmk
      dkv_ref[:, vs] += dmv

  return pl.pallas_call(
      body, name=name, grid=(bl, s // tq),
      in_specs=[pl.BlockSpec((None, tq, MEM_WIDTH), lambda b, i: (b, i, qcol)),
                pl.BlockSpec((None, nm, 2 * MEM_WIDTH), lambda b, i: (b, 0, 0)),
                pl.BlockSpec((None, tq, MEM_WIDTH), lambda b, i: (b, i, docol))],
      out_specs=[pl.BlockSpec((None, tq, MEM_WIDTH), lambda b, i: (b, i, 0)),
                 pl.BlockSpec((None, nm, 2 * MEM_WIDTH), lambda b, i: (b, 0, 0))],
      out_shape=[jax.ShapeDtypeStruct((bl, s, MEM_WIDTH), BF16), jax.ShapeDtypeStruct((bl, nm, 2 * MEM_WIDTH), F32)],
      compiler_params=_params(("parallel", "arbitrary")),
  )(h3, mkv3, dcat3)


def _sgu_consts():
  ti = lax.broadcasted_iota(jnp.int32, (CHUNK, CHUNK), 0)
  si = lax.broadcasted_iota(jnp.int32, (CHUNK, CHUNK), 1)
  return si <= ti, si < 64


def _sgu_bias_lanes(b_s):
  return jnp.repeat(b_s.T, 64, axis=1)


def _sgu_fwd(h2, ln_g, ln_b, w_s, b_s, name, tr=512):
  t, _ = h2.shape
  tr = _tile(t, tr)
  nch = tr // CHUNK
  npair = N_SGU_GROUPS // 2

  def body(u_ref, v_ref, g_ref, b_ref, w_ref, bs_ref, o_ref, vn_sc):
    tril, head0 = _sgu_consts()
    xhat, _ = _ln_stats(_gelu(v_ref[...]))
    vn_sc[...] = _bf(xhat * g_ref[...] + b_ref[...])
    for jp in range(npair):
      cs = slice(jp * LANES, (jp + 1) * LANES)
      w0 = _bf(jnp.where(tril, w_ref[2 * jp], 0.0))
      w1 = _bf(jnp.where(tril, w_ref[2 * jp + 1], 0.0))
      bias = bs_ref[:, cs]
      for c in range(nch):
        rs = slice(c * CHUNK, (c + 1) * CHUNK)
        vb = vn_sc[rs, cs]
        mixed = jnp.where(head0, _dot(w0, vb), _dot(w1, vb)) + bias
        o_ref[rs, cs] = _bf(_gelu(u_ref[rs, cs]) * mixed)

  blk = lambda j: pl.BlockSpec((tr, SGU_WIDTH), lambda i: (i, j))
  vec = pl.BlockSpec((1, SGU_WIDTH), lambda i: (0, 0))
  return pl.pallas_call(
      body, name=name, grid=(t // tr,),
      in_specs=[blk(0), blk(1), vec, vec,
                pl.BlockSpec((N_SGU_GROUPS, CHUNK, CHUNK), lambda i: (0, 0, 0)),
                pl.BlockSpec((CHUNK, SGU_WIDTH), lambda i: (0, 0))],
      out_specs=blk(0), out_shape=jax.ShapeDtypeStruct((t, SGU_WIDTH), BF16),
      scratch_shapes=[pltpu.VMEM((tr, SGU_WIDTH), BF16)],
      compiler_params=_params(("parallel",)),
  )(h2, h2, ln_g.reshape(1, -1), ln_b.reshape(1, -1), w_s, _sgu_bias_lanes(b_s))


def _sgu_bwd(h2, dcat, ln_g, ln_b, w_s, b_s, name, tr=512):
  t, _ = h2.shape
  tr = _tile(t, tr)
  nch = tr // CHUNK
  npair = N_SGU_GROUPS // 2
  nsteps = t // tr

  def body(u_ref, v_ref, dm_ref, g_ref, b_ref, w_ref, bs_ref,
           du_ref, dv_ref, dw_ref, dbs_ref, dg_ref, db_ref, vn_sc, dmx_sc, dvn_sc, mix_sc, dbx_sc):
    i = pl.program_id(0)
    tril, head0 = _sgu_consts()

    @pl.when(i == 0)
    def _():
      dw_ref[...] = jnp.zeros_like(dw_ref)
      dg_ref[...] = jnp.zeros_like(dg_ref)
      db_ref[...] = jnp.zeros_like(db_ref)
      dbx_sc[...] = jnp.zeros_like(dbx_sc)

    gv, gv_der = _gelu_parts(v_ref[...])
    xhat, rstd = _ln_stats(gv)
    g = g_ref[...]
    vn_sc[...] = _bf(xhat * g + b_ref[...])
    gu, gu_der = _gelu_parts(u_ref[...])
    dmix = dm_ref[...]
    dmx_sc[...] = dmix * gu

    for jp in range(npair):
      cs = slice(jp * LANES, (jp + 1) * LANES)
      w0 = _bf(jnp.where(tril, w_ref[2 * jp], 0.0))
      w1 = _bf(jnp.where(tril, w_ref[2 * jp + 1], 0.0))
      bias = bs_ref[:, cs]
      dw0 = jnp.zeros((CHUNK, CHUNK), F32)
      dw1 = jnp.zeros((CHUNK, CHUNK), F32)
      dbx = jnp.zeros((CHUNK, LANES), F32)
      for c in range(nch):
        rs = slice(c * CHUNK, (c + 1) * CHUNK)
        vb = vn_sc[rs, cs]
        mix_sc[rs, cs] = jnp.where(head0, _dot(w0, vb), _dot(w1, vb)) + bias
        dmx = dmx_sc[rs, cs]
        d0 = _bf(jnp.where(head0, dmx, 0.0))
        d1 = _bf(jnp.where(head0, 0.0, dmx))
        dvn_sc[rs, cs] = _dot_tn(w0, d0) + _dot_tn(w1, d1)
        dw0 = dw0 + _dot_nt(d0, vb)
        dw1 = dw1 + _dot_nt(d1, vb)
        dbx = dbx + dmx
      dw_ref[2 * jp] += dw0
      dw_ref[2 * jp + 1] += dw1
      dbx_sc[:, cs] += dbx

    du_ref[...] = _bf(dmix * mix_sc[...] * gu_der)
    dvn = dvn_sc[...]
    dv_ref[...] = _bf(_ln_bwd(dvn, xhat, rstd, g) * gv_der)
    dg_ref[...] += jnp.sum(dvn * xhat, axis=0, keepdims=True)
    db_ref[...] += jnp.sum(dvn, axis=0, keepdims=True)

    @pl.when(i == nsteps - 1)
    def _():
      lane = lax.broadcasted_iota(jnp.int32, (CHUNK, LANES), 1)
      acc = jnp.zeros((CHUNK, LANES), F32)
      for gi in range(N_SGU_GROUPS):
        jp, j = gi // 2, gi % 2
        part = dbx_sc[:, jp * LANES:(jp + 1) * LANES]
        hm = (lane < 64) if j == 0 else (lane >= 64)
        colsum = jnp.sum(jnp.where(hm, part, 0.0), axis=1, keepdims=True)
        acc = jnp.where(lane == gi, colsum, acc)
        dw_ref[gi] = jnp.where(tril, dw_ref[gi], 0.0)
      dbs_ref[...] = acc

  blk = lambda j: pl.BlockSpec((tr, SGU_WIDTH), lambda i: (i, j))
  vec = pl.BlockSpec((1, SGU_WIDTH), lambda i: (0, 0))
  wspec = pl.BlockSpec((N_SGU_GROUPS, CHUNK, CHUNK), lambda i: (0, 0, 0))
  big = lambda dt: pltpu.VMEM((tr, SGU_WIDTH), dt)
  du, dv, dw, dbs, dg, db = pl.pallas_call(
      body, name=name, grid=(nsteps,),
      in_specs=[blk(0), blk(1), blk(0), vec, vec, wspec, pl.BlockSpec((CHUNK, SGU_WIDTH), lambda i: (0, 0))],
      out_specs=[blk(0), blk(0), wspec, pl.BlockSpec((CHUNK, LANES), lambda i: (0, 0)), vec, vec],
      out_shape=[jax.ShapeDtypeStruct((t, SGU_WIDTH), BF16), jax.ShapeDtypeStruct((t, SGU_WIDTH), BF16),
                 jax.ShapeDtypeStruct((N_SGU_GROUPS, CHUNK, CHUNK), F32), jax.ShapeDtypeStruct((CHUNK, LANES), F32),
                 jax.ShapeDtypeStruct((1, SGU_WIDTH), F32), jax.ShapeDtypeStruct((1, SGU_WIDTH), F32)],
      scratch_shapes=[big(BF16), big(F32), big(F32), big(F32), pltpu.VMEM((CHUNK, SGU_WIDTH), F32)],
      compiler_params=_params(("arbitrary",)),
  )(h2, h2, dcat, ln_g.reshape(1, -1), ln_b.reshape(1, -1), w_s, _sgu_bias_lanes(b_s))
  return du, dv, dw, dbs[:, :N_SGU_GROUPS].T, dg[0], db[0]


def _loss_head(xo, tgt, z, g, name, tm=512):
  m, d = xo.shape
  tm = _tile(m, tm)

  def body(x_ref, t_ref, z_ref, g_ref, l_ref, dz_ref, dzb_ref, dg_ref, dbias_ref):
    @pl.when(pl.program_id(0) == 0)
    def _():
      l_ref[...] = jnp.zeros_like(l_ref)

    diff = x_ref[...] - t_ref[...]
    rowsum = jnp.sum(diff * diff, axis=1, keepdims=True)
    tot = jnp.sum(rowsum, axis=0, keepdims=True) * (0.5 / d)
    l_ref[...] += jnp.broadcast_to(tot, l_ref.shape)
    _ln_bwd_tail(diff * (1.0 / d), z_ref, g_ref, dz_ref, dzb_ref, dg_ref, dbias_ref)

  row = pl.BlockSpec((tm, d), lambda i: (i, 0))
  vec = pl.BlockSpec((1, d), lambda i: (0, 0))
  out_specs, out_shape = _ln_bwd_outs(m, d, row, vec)
  l, dz, dzb, dg, dbias = pl.pallas_call(
      body, name=name, grid=(m // tm,), in_specs=[row, row, row, vec],
      out_specs=[pl.BlockSpec((8, LANES), lambda i: (0, 0))] + out_specs,
      out_shape=[jax.ShapeDtypeStruct((8, LANES), F32)] + out_shape,
      compiler_params=_params(("arbitrary",)),
  )(xo, tgt, z, g.reshape(1, d))
  return l[0, 0], dz, dzb, dg[0], dbias[0]


def _local_step(x3, mem3, tgt3, w, late_weights=None, early_exchange=None):
  w = dict(w)
  bl, s, d = x3.shape
  t = bl * s
  nm = mem3.shape[1]
  mem2 = mem3.reshape(bl * nm, d)
  x = x3.reshape(t, d)
  xb = x
  saved = []
  for i in range(DEPTH):
    j = i // 2
    attn = i % 2 == 0
    mkv = _mm(mem2, w["w_mem_kv"][i], "nn", F32, f"mkv_fwd_{i}", tm=1024, tn=512, tk=1024)
    mkv3 = mkv.reshape(bl, nm, 2 * MEM_WIDTH)
    w_in = w["a_w_in"][j] if attn else w["b_w_in"][j]
    h = _mm(xb, w_in, "nt", F32, f"in_proj_{i}", tm=512, tn=w_in.shape[0], tk=d)
    h3 = h.reshape(bl, s, -1)
    if attn and late_weights is not None and i in late_weights:
      mix3, lse3, gathered = _attn_fwd(h3, f"dil_attn_fwd_{i}", gather=late_weights[i].flat)
      for n, layers in late_weights[i].unpack(gathered).items():
        w[n] = {**w.get(n, {}), **layers}
    elif attn:
      mix3, lse3 = _attn_fwd(h3, f"dil_attn_fwd_{i}")
    if attn:
      mix = mix3.reshape(t, DIL_WIDTH)
      qcol = 3 * DIL_WIDTH // MEM_WIDTH
    else:
      mix = _sgu_fwd(h, w["sgu_ln_g"][j], w["sgu_ln_b"][j], w["sgu_w_s"][j], w["sgu_b_s"][j], f"sgu_fwd_{i}")
      lse3 = None
      qcol = 2 * SGU_WIDTH // MEM_WIDTH
    mo = _mem_fwd(h3, qcol, mkv3, f"mem_attn_fwd_{i}").reshape(t, MEM_WIDTH)
    cat = jnp.concatenate([mix, mo], axis=1)
    z1, xm, xmb = _mm_res_ln(cat, w["w_out"][i], x, w["ln_mix_g"][i], w["ln_mix_b"][i], f"out_proj_ln_{i}", tk=1024)
    ga, gb, hm = _ffn_up(xmb, w["w_gate"][i], w["w_up"][i], f"ffn_up_{i}")
    z2, xo, xob = _mm_res_ln(hm, w["w_down"][i], xm, w["ln_ffn_g"][i], w["ln_ffn_b"][i], f"ffn_down_ln_{i}", tk=hm.shape[1])
    saved.append(dict(xb=xb, h=h, h3=h3, mkv3=mkv3, mix3=(mix3 if attn else None), lse3=lse3, cat=cat, z1=z1,
                      xmb=xmb, ga=ga, gb=gb, hm=hm, z2=z2, qcol=qcol))
    x, xb = xo, xob

  names = ("a_w_in", "b_w_in", "sgu_ln_g", "sgu_ln_b", "sgu_w_s", "sgu_b_s", "w_mem_kv", "w_out",
           "ln_mix_g", "ln_mix_b", "w_gate", "w_up", "w_down", "ln_ffn_g", "ln_ffn_b")
  grads = {n: [None] * len(w[n]) for n in names}
  last = DEPTH - 1
  loss, dz2, dz2b, grads["ln_ffn_g"][last], grads["ln_ffn_b"][last] = _loss_head(
      x, tgt3.reshape(t, d), saved[last]["z2"], w["ln_ffn_g"][last], "loss_head")
  dx = None
  for i in reversed(range(DEPTH)):
    j = i // 2
    attn = i % 2 == 0
    sv = saved[i]
    da, db = _ffn_bwd_hidden(dz2b, w["w_down"][i], sv["ga"], sv["gb"], f"ffn_bwd_hidden_{i}")
    grads["w_down"][i] = _mm(sv["hm"], dz2b, "tn", F32, f"dw_down_{i}", tm=1408, tn=1024, tk=1024)
    grads["w_gate"][i] = _mm(da, sv["xmb"], "tn", F32, f"dw_gate_{i}", tm=1408, tn=1024, tk=1024)
    grads["w_up"][i] = _mm(db, sv["xmb"], "tn", F32, f"dw_up_{i}", tm=1408, tn=1024, tk=1024)
    dz1, dz1b, grads["ln_mix_g"][i], grads["ln_mix_b"][i] = _ffn_bwd_input_ln(
        da, db, w["w_gate"][i], w["w_up"][i], dz2, sv["z1"], w["ln_mix_g"][i], f"ffn_bwd_input_ln_{i}")
    grads["w_out"][i] = _mm(sv["cat"], dz1b, "tn", F32, f"dw_out_{i}", tm=1024, tn=1024, tk=1024)
    dcat = _mm(dz1b, w["w_out"][i], "nt", F32, f"out_proj_bwd_{i}", tm=1024, tn=1024, tk=1024)
    dcat3 = dcat.reshape(bl, s, -1)
    dqm3, dmkv3 = _mem_bwd(sv["h3"], sv["qcol"], sv["mkv3"], dcat3, f"mem_attn_bwd_{i}")
    grads["w_mem_kv"][i] = _mm(mem2, dmkv3.reshape(bl * nm, 2 * MEM_WIDTH), "tn", F32, f"dw_mem_kv_{i}", tm=1024, tn=512, tk=1024)
    dqm = dqm3.reshape(t, MEM_WIDTH)
    if attn and early_exchange is not None:
      begun = grads.setdefault("early_exchange", [])
      tag = f"early{i}"
      q, (pack, state) = early_exchange(grads, [e[0] for e in begun], i == 0, tag)
      dq3, dk3, dv3, x3 = _attn_bwd(sv["h3"], sv["mix3"], sv["lse3"], dcat3, f"dil_attn_bwd_{i}", exchange=q)
      begun.append((pack, state, x3, tag))
      parts = [dq3.reshape(t, -1), dk3.reshape(t, -1), dv3.reshape(t, -1), dqm]
    elif attn:
      dq3, dk3, dv3 = _attn_bwd(sv["h3"], sv["mix3"], sv["lse3"], dcat3, f"dil_attn_bwd_{i}")
      parts = [dq3.reshape(t, -1), dk3.reshape(t, -1), dv3.reshape(t, -1), dqm]
    else:
      du, dv, dws, dbs, dlg, dlb = _sgu_bwd(sv["h"], dcat, w["sgu_ln_g"][j], w["sgu_ln_b"][j], w["sgu_w_s"][j],
                                             w["sgu_b_s"][j], f"sgu_bwd_{i}")
      grads["sgu_w_s"][j], grads["sgu_b_s"][j], grads["sgu_ln_g"][j], grads["sgu_ln_b"][j] = dws, dbs, dlg, dlb
      parts = [du, dv, dqm]
    dh = jnp.concatenate(parts, axis=1)
    w_in = w["a_w_in"][j] if attn else w["b_w_in"][j]
    grads["a_w_in" if attn else "b_w_in"][j] = _mm(dh, sv["xb"], "tn", F32, f"dw_in_{i}", tm=1280 if attn else 896, tn=1024, tk=1024)
    if i > 0:
      dz2, dz2b, grads["ln_ffn_g"][i - 1], grads["ln_ffn_b"][i - 1] = _in_proj_bwd_ln(
          dh, w_in, dz1, saved[i - 1]["z2"], w["ln_ffn_g"][i - 1], f"in_proj_bwd_ln_{i}")
    else:
      dx = _mm(dh, w_in, "nn", F32, f"in_proj_bwd_{i}", add=dz1, add_scale=DN_ALPHA, tm=512, tn=d, tk=w_in.shape[0])
  return loss, dx.reshape(bl, s, d), grads


def _my_place():
  return lax.axis_index("x"), lax.axis_index("y"), lax.axis_index("c")


def _other_chips(x, y):
  return [(1 - x, y), (x, 1 - y), (1 - x, 1 - y)]


ANY = pl.BlockSpec(memory_space=pl.ANY)


def _all_gather_halves(wl, name):
  _, r, c_ = wl.shape

  def body(w_ref, g_ref, send_sems, recv_sems):
    x, y, c = _my_place()
    me = 2 * x + y
    sibling = (x, y, 1 - c)
    chips = _other_chips(x, y)

    def copy(k, src, dst, to):
      return pltpu.make_async_remote_copy(src_ref=src, dst_ref=dst, send_sem=send_sems.at[k], recv_sem=recv_sems.at[k],
                                          device_id=to, device_id_type=MESH_ID)

    first = [copy(k, w_ref.at[c], g_ref.at[me, c], (px, py, c)) for k, (px, py) in enumerate(chips)]
    for cp in first:
      cp.start()
    passed = []
    for k, (px, py) in enumerate(chips):
      landed = g_ref.at[2 * px + py, c]
      copy(k, landed, landed, (px, py, c)).wait_recv()
      fwd = copy(3 + k, landed, landed, sibling)
      fwd.start()
      passed.append(fwd)
    for k, (px, py) in enumerate(chips):
      theirs = g_ref.at[2 * px + py, 1 - c]
      copy(3 + k, theirs, theirs, sibling).wait_recv()
    for cp in first + passed:
      cp.wait_send()

  got = pl.pallas_call(
      body, name=name, in_specs=[ANY], out_specs=ANY,
      out_shape=jax.ShapeDtypeStruct((4, 2, r, c_), wl.dtype),
      scratch_shapes=[pltpu.SemaphoreType.DMA((6,)), pltpu.SemaphoreType.DMA((6,))],
  )(wl)
  chip = 2 * lax.axis_index("x") + lax.axis_index("y")
  return lax.dynamic_update_slice(got, wl[None], (chip, 0, 0, 0))


def _relayed_gather_phase(phase, w_ref, g_ref, send_sems, recv_sems):
  h = w_ref.shape[1] // 2
  x, y, c = _my_place()
  sibling = (x, y, 1 - c)
  xn, yn, dg = _other_chips(x, y)

  def copy(k, src, dst, to):
    return pltpu.make_async_remote_copy(src_ref=src, dst_ref=dst, send_sem=send_sems.at[k], recv_sem=recv_sems.at[k],
                                        device_id=to, device_id_type=MESH_ID)

  def block(chip, half):
    return g_ref.at[2 * chip[0] + chip[1], half]

  def same(k, ref, to):
    return copy(k, ref, ref, to)

  top, bottom = pl.ds(0, h), pl.ds(h, h)
  sends = [copy(0, w_ref.at[c], block((x, y), c), (*xn, c)), copy(1, w_ref.at[c], block((x, y), c), (*yn, c)),
           same(2, block(xn, c).at[top], (*yn, c)), same(3, block(yn, c).at[bottom], (*xn, c)),
           same(4, block(xn, c), sibling), same(5, block(yn, c), sibling), same(6, block(dg, c), sibling)]
  if phase == 0:
    sends[0].start()
    sends[1].start()
  elif phase == 1:
    same(0, block(xn, c), (*xn, c)).wait_recv()
    sends[2].start()
    sends[4].start()
    same(1, block(yn, c), (*yn, c)).wait_recv()
    sends[3].start()
    sends[5].start()
  else:
    same(2, block(dg, c).at[top], (*yn, c)).wait_recv()
    same(3, block(dg, c).at[bottom], (*xn, c)).wait_recv()
    sends[6].start()
    for k, chip in ((4, xn), (5, yn), (6, dg)):
      same(k, block(chip, 1 - c), sibling).wait_recv()
    for cp in sends:
      cp.wait_send()


N_RELAY_COPIES = 7


def _place_own_block(got, wl):
  chip = 2 * lax.axis_index("x") + lax.axis_index("y")
  return lax.dynamic_update_slice(got, wl[None], (chip, 0, 0, 0))


def _all_gather_relayed(wl, name):
  _, r, c_ = wl.shape
  assert (r // 2) % ROW_ALIGN == 0

  def body(w_ref, g_ref, send_sems, recv_sems):
    for phase in range(3):
      _relayed_gather_phase(phase, w_ref, g_ref, send_sems, recv_sems)

  got = pl.pallas_call(
      body, name=name, in_specs=[ANY], out_specs=ANY,
      out_shape=jax.ShapeDtypeStruct((4, 2, r, c_), wl.dtype),
      scratch_shapes=[pltpu.SemaphoreType.DMA((N_RELAY_COPIES,)), pltpu.SemaphoreType.DMA((N_RELAY_COPIES,))],
  )(wl)
  return _place_own_block(got, wl)


def _sibling_swap(v, name):
  def body(v_ref, o_ref, send_sem, recv_sem):
    x, y, c = _my_place()
    cp = pltpu.make_async_remote_copy(src_ref=v_ref, dst_ref=o_ref, send_sem=send_sem, recv_sem=recv_sem,
                                      device_id=(x, y, 1 - c), device_id_type=MESH_ID)
    cp.start()
    cp.wait()

  return pl.pallas_call(
      body, name=name, in_specs=[ANY], out_specs=ANY, out_shape=jax.ShapeDtypeStruct(v.shape, v.dtype),
      scratch_shapes=[pltpu.SemaphoreType.DMA, pltpu.SemaphoreType.DMA],
  )(v)


def _chip_exchange_copies(q_ref, o_ref, send_sems, recv_sems):
  x, y, c = _my_place()
  return [pltpu.make_async_remote_copy(src_ref=q_ref.at[2 * px + py], dst_ref=o_ref.at[k], send_sem=send_sems.at[k],
                                       recv_sem=recv_sems.at[k], device_id=(px, py, c), device_id_type=MESH_ID)
          for k, (px, py) in enumerate(_other_chips(x, y))]


def _chip_exchange(q, name):
  _, r, c_ = q.shape

  def body(q_ref, o_ref, send_sems, recv_sems):
    cps = _chip_exchange_copies(q_ref, o_ref, send_sems, recv_sems)
    for cp in cps:
      cp.start()
    for cp in cps:
      cp.wait()

  return pl.pallas_call(
      body, name=name, in_specs=[ANY], out_specs=ANY, out_shape=jax.ShapeDtypeStruct((3, r, c_), q.dtype),
      scratch_shapes=[pltpu.SemaphoreType.DMA((3,)), pltpu.SemaphoreType.DMA((3,))],
  )(q)


def _share_halves(both, name):
  _, r, c_ = both.shape

  def body(b_ref, o_ref, send_sem, recv_sem):
    x, y, c = _my_place()
    cp = pltpu.make_async_remote_copy(src_ref=b_ref.at[c], dst_ref=o_ref.at[c], send_sem=send_sem, recv_sem=recv_sem,
                                      device_id=(x, y, 1 - c), device_id_type=MESH_ID)
    cp.start()
    cp.wait()

  full = pl.pallas_call(
      body, name=name, in_specs=[ANY], out_specs=ANY, out_shape=jax.ShapeDtypeStruct(both.shape, both.dtype),
      input_output_aliases={0: 0},
      scratch_shapes=[pltpu.SemaphoreType.DMA, pltpu.SemaphoreType.DMA],
  )(both)
  return full.reshape(2 * r, c_)


def _half_spec(tr, c_, pick):
  return pl.BlockSpec((None, None, tr, c_), lambda s, r, place: (s, pick(place), r, 0))


def _cast_other_half(p, place, name, tr=512):
  _, _, r, c_ = p.shape
  tr = _tile(r, tr, 16)

  def body(place_ref, p_ref, o_ref):
    o_ref[...] = _bf(p_ref[...])

  out_spec = pl.BlockSpec((None, tr, c_), lambda s, rr, place: (s, rr, 0))
  return pl.pallas_call(
      body, name=name, out_shape=jax.ShapeDtypeStruct((4, r, c_), BF16),
      grid_spec=pltpu.PrefetchScalarGridSpec(num_scalar_prefetch=1, grid=(4, r // tr),
                                             in_specs=[_half_spec(tr, c_, lambda place: 1 - place[1])], out_specs=out_spec),
      compiler_params=_params(("parallel", "parallel")),
  )(place, p)


def _add_sibling(p, x1, place, name, tr=512):
  _, _, r, c_ = p.shape
  tr = _tile(r, tr, 16)

  def body(place_ref, p_ref, x_ref, o_ref):
    o_ref[...] = _bf(p_ref[...] + x_ref[...].astype(F32))

  row = pl.BlockSpec((None, tr, c_), lambda s, rr, place: (s, rr, 0))
  return pl.pallas_call(
      body, name=name, out_shape=jax.ShapeDtypeStruct((4, r, c_), BF16),
      grid_spec=pltpu.PrefetchScalarGridSpec(num_scalar_prefetch=1, grid=(4, r // tr),
                                             in_specs=[_half_spec(tr, c_, lambda place: place[1]), row], out_specs=row),
      compiler_params=_params(("parallel", "parallel")),
  )(place, p, x1)


def _sum_own(p, x1, x3, place, name, tr=512):
  _, _, r, c_ = p.shape
  tr = _tile(r, tr, 16)

  def body(place_ref, p_ref, x1_ref, x3_ref, o_ref):
    acc = p_ref[...] + x1_ref[...].astype(F32)
    for k in range(3):
      acc = acc + x3_ref[k].astype(F32)
    o_ref[...] = acc

  return pl.pallas_call(
      body, name=name, out_shape=jax.ShapeDtypeStruct((2, r, c_), F32),
      grid_spec=pltpu.PrefetchScalarGridSpec(
          num_scalar_prefetch=1, grid=(r // tr,),
          in_specs=[pl.BlockSpec((None, None, tr, c_), lambda rr, place: (place[0], place[1], rr, 0)),
                    pl.BlockSpec((None, tr, c_), lambda rr, place: (place[0], rr, 0)),
                    pl.BlockSpec((3, tr, c_), lambda rr, place: (0, rr, 0))],
          out_specs=pl.BlockSpec((None, tr, c_), lambda rr, place: (place[1], rr, 0))),
      compiler_params=_params(("parallel",)),
  )(place, p, x1, x3)


def _reduce_scatter_begin(p, tag):
  x, y, c = _my_place()
  place = jnp.stack([2 * x + y, c]).astype(jnp.int32)
  x1 = _sibling_swap(_cast_other_half(p, place, f"rs_cast_other_half_{tag}"), f"rs_sibling_swap_{tag}")
  return _add_sibling(p, x1, place, f"rs_add_sibling_{tag}"), (p, x1, place)


def _reduce_scatter_end(state, x3, tag):
  p, x1, place = state
  return _share_halves(_sum_own(p, x1, x3, place, f"rs_sum_own_{tag}"), f"rs_share_halves_{tag}")


def _adamw(w, g, m, v, name):
  shape = w.shape
  cols = shape[-1]
  rows = w.size // cols
  tr = _tile(rows, max(8, (256 * 1024) // cols // 8 * 8), 8)

  def body(w_ref, g_ref, m_ref, v_ref, d_ref, nm_ref, nv_ref):
    gv = g_ref[...]
    nm = ADAM_B1 * m_ref[...] + (1.0 - ADAM_B1) * gv
    nv = ADAM_B2 * v_ref[...] + (1.0 - ADAM_B2) * (gv * gv)
    m_hat = nm / (1.0 - ADAM_B1 ** ADAM_STEP)
    v_hat = nv / (1.0 - ADAM_B2 ** ADAM_STEP)
    d_ref[...] = -ADAM_LR * (m_hat / (jnp.sqrt(v_hat) + ADAM_EPS) + ADAM_WD * w_ref[...])
    nm_ref[...] = nm
    nv_ref[...] = nv

  spec = pl.BlockSpec((tr, cols), lambda i: (i, 0))
  sds = jax.ShapeDtypeStruct((rows, cols), F32)
  outs = pl.pallas_call(
      body, name=name, grid=(rows // tr,), in_specs=[spec] * 4, out_specs=[spec] * 3, out_shape=[sds] * 3,
      compiler_params=_params(("parallel",)),
  )(*(t.reshape(rows, cols) for t in (w, g, m, v)))
  return tuple(o.reshape(shape) for o in outs)


SHARDED = (("a_w_in", True), ("b_w_in", True), ("w_mem_kv", False), ("w_out", False), ("w_gate", True),
           ("w_up", True), ("w_down", False))
SMALL_SHARDED = (("sgu_ln_g", 1), ("sgu_ln_b", 1))
REPLICATED = ("sgu_w_s", "sgu_b_s", "ln_mix_g", "ln_mix_b", "ln_ffn_g", "ln_ffn_b")
SMALL_ORDER = ("sgu_w_s", "sgu_b_s", "ln_mix_g", "ln_mix_b", "ln_ffn_g", "ln_ffn_b", "sgu_ln_g", "sgu_ln_b")
ROW_ALIGN = 16
RS_TILE_ROWS = 512


def _pad_to(v, n):
  return jnp.pad(v, (0, n - v.shape[0]))


def _round_up(n, a):
  return -(-n // a) * a


def _exchange_form(t, transposed):
  return jnp.swapaxes(t, 1, 2) if transposed else t


def _to_shard_major(full, axis):
  shp = full.shape
  cut = shp[:axis] + (4, shp[axis] // 4) + shp[axis + 1:]
  return jnp.moveaxis(full.reshape(cut), axis, 0).reshape(4, -1, FLAT_COLS)


def _from_shard_major(rows, shard_shape, axis):
  full = jnp.moveaxis(rows.reshape((4,) + tuple(shard_shape)), 0, axis)
  shp = full.shape
  return full.reshape(shp[:axis] + (shp[axis] * shp[axis + 1],) + shp[axis + 2:])


class _WeightPack:
  def __init__(self, items, small=()):
    segs, self.rows, self.small, off = [], {}, [], 0
    for n, l, b in items:
      seg = b.reshape(-1, FLAT_COLS)
      self.rows[(n, l)] = (off, seg.shape[0], b.shape)
      segs.append(seg)
      off += seg.shape[0]
    if small:
      flat = jnp.concatenate([lax.bitcast_convert_type(v, BF16).reshape(-1) for _, v in small])
      rows = _round_up(flat.shape[0], ROW_ALIGN * FLAT_COLS) // FLAT_COLS
      self.small = [(n, v.shape) for n, v in small]
      self.small_rows = (off, rows)
      segs.append(_pad_to(flat, rows * FLAT_COLS).reshape(rows, FLAT_COLS))
      off += rows
    rows_pad = _round_up(off, 4 * ROW_ALIGN)
    if rows_pad > off:
      segs.append(jnp.zeros((rows_pad - off, FLAT_COLS), BF16))
    self.flat = jnp.concatenate(segs).reshape(2, rows_pad // 2, FLAT_COLS)

  def unpack(self, gathered):
    g = gathered.reshape(4, -1, FLAT_COLS)
    out = {}
    for (n, l), (off, nr, shape) in self.rows.items():
      out.setdefault(n, {})[l] = g[:, off:off + nr].reshape((4 * shape[0],) + shape[1:])
    if self.small:
      off, rows = self.small_rows
      flat = g[:, off:off + rows].reshape(4, rows * FLAT_COLS)
      pos = 0
      for n, shape in self.small:
        sz = 2 * math.prod(shape)
        vals = lax.bitcast_convert_type(flat[:, pos:pos + sz].reshape((4,) + shape + (2,)), F32)
        out[n] = _from_shard_major(vals, shape, len(shape) - 1)
        pos += sz
    return out


FIRST_WEIGHTS = (("a_w_in", 0), ("w_mem_kv", 0))


def _weight_packs(shards):
  blocks = {(n, l): _exchange_form(shards[n], tr)[l].astype(BF16)
            for n, tr in SHARDED for l in range(shards[n].shape[0])}
  first = _WeightPack([(n, l, blocks[(n, l)]) for n, l in FIRST_WEIGHTS],
                      small=[(n, shards[n]) for n, _ in SMALL_SHARDED])

  def model_layer(n, l):
    return {"a_w_in": 2 * l, "b_w_in": 2 * l + 1}.get(n, l)

  def in_second(n, l):
    return model_layer(n, l) == 3 or (model_layer(n, l) == 2 and n not in ("a_w_in", "w_mem_kv"))

  rest = [(n, l, b) for (n, l), b in blocks.items() if (n, l) not in FIRST_WEIGHTS]
  return first, {0: _WeightPack([it for it in rest if not in_second(it[0], it[1])]),
                 2: _WeightPack([it for it in rest if in_second(it[0], it[1])])}


def _reduce_grads(grads, shard_shapes):
  done = [(pack, _reduce_scatter_end(state, x3, tag)) for pack, state, x3, tag in grads.pop("early_exchange")]
  sent = set().union(*[pack.rows.keys() for pack, _ in done])
  late = _GradPack([(n, l, g) for n, _ in SHARDED for l, g in enumerate(grads[n]) if (n, l) not in sent])
  q, state = _reduce_scatter_begin(late.p, "late")
  done.append((late, _reduce_scatter_end(state, _chip_exchange(q, "rs_chip_exchange_late"), "late")))

  def reduced(key):
    pack, mine = next((pack, mine) for pack, mine in done if key in pack.rows)
    off, nr = pack.rows[key]
    return mine[off:off + nr]

  out = {}
  for n, tr in SHARDED:
    layers, rows, cols = shard_shapes[n]
    blocks = [reduced((n, l)) for l in range(layers)]
    out[n] = jnp.stack([b.reshape(cols, rows).T if tr else b.reshape(rows, cols) for b in blocks])
  piece = reduced("small")
  quarter_rows = piece.shape[0]
  piece = piece.reshape(2, quarter_rows // 2, FLAT_COLS)
  small_sum = _all_gather_halves(piece, "gather_small_grads").reshape(-1)
  off = 0
  for n in SMALL_ORDER:
    shape = (len(grads[n]),) + grads[n][0].shape
    sz = math.prod(shape)
    out[n] = small_sum[off:off + sz].reshape(shape)
    off += sz
  return out


class _GradPack:
  def __init__(self, items, small=None):
    segs, self.rows, off = [], {}, 0
    for n, l, g in items:
      seg = _to_shard_major(g, 0)
      self.rows[(n, l)] = (off, seg.shape[1])
      segs.append(seg)
      off += seg.shape[1]
    if small is not None:
      flat = jnp.concatenate([jnp.stack(small[n]).reshape(-1) for n in SMALL_ORDER])
      n_small = _round_up(flat.shape[0], 4 * 2 * 8 * FLAT_COLS)
      quarter_rows = n_small // (4 * FLAT_COLS)
      self.rows["small"] = (off, quarter_rows)
      segs.append(_pad_to(flat, n_small).reshape(4, quarter_rows, FLAT_COLS))
      off += quarter_rows
    rows_pad = _round_up(off, 2 * RS_TILE_ROWS)
    if rows_pad > off:
      segs.append(jnp.zeros((4, rows_pad - off, FLAT_COLS), F32))
    self.p = jnp.concatenate(segs, axis=1).reshape(4, 2, rows_pad // 2, FLAT_COLS)


def _early_exchange_begin(grads, sent_packs, last, tag):
  sent = set().union(*[pack.rows.keys() for pack in sent_packs])
  items = [(n, l, g) for n, _ in SHARDED for l, g in enumerate(grads[n]) if g is not None and (n, l) not in sent]
  pack = _GradPack(items, small={n: grads[n] for n in SMALL_ORDER} if last else None)
  q, state = _reduce_scatter_begin(pack.p, tag)
  return q, (pack, state)


WEIGHT_NAMES = ("a_w_in", "b_w_in", "sgu_ln_g", "sgu_ln_b", "sgu_w_s", "sgu_b_s", "w_mem_kv", "w_out",
                "ln_mix_g", "ln_mix_b", "w_gate", "w_up", "w_down", "ln_ffn_g", "ln_ffn_b")


def kernel(x, mem, a_w_in, b_w_in, sgu_ln_g, sgu_ln_b, sgu_w_s, sgu_b_s, w_mem_kv, w_out, ln_mix_g, ln_mix_b, w_gate, w_up, w_down, ln_ffn_g, ln_ffn_b, loss_target, m_a_w_in, m_b_w_in, m_sgu_ln_g, m_sgu_ln_b, m_sgu_w_s, m_sgu_b_s, m_w_mem_kv, m_w_out, m_ln_mix_g, m_ln_mix_b, m_w_gate, m_w_up, m_w_down, m_ln_ffn_g, m_ln_ffn_b, v_a_w_in, v_b_w_in, v_sgu_ln_g, v_sgu_ln_b, v_sgu_w_s, v_sgu_b_s, v_w_mem_kv, v_w_out, v_ln_mix_g, v_ln_mix_b, v_w_gate, v_w_up, v_w_down, v_ln_ffn_g, v_ln_ffn_b):
  weights = dict(a_w_in=a_w_in, b_w_in=b_w_in, sgu_ln_g=sgu_ln_g, sgu_ln_b=sgu_ln_b, sgu_w_s=sgu_w_s, sgu_b_s=sgu_b_s,
                 w_mem_kv=w_mem_kv, w_out=w_out, ln_mix_g=ln_mix_g, ln_mix_b=ln_mix_b, w_gate=w_gate, w_up=w_up,
                 w_down=w_down, ln_ffn_g=ln_ffn_g, ln_ffn_b=ln_ffn_b)
  mom1 = dict(a_w_in=m_a_w_in, b_w_in=m_b_w_in, sgu_ln_g=m_sgu_ln_g, sgu_ln_b=m_sgu_ln_b, sgu_w_s=m_sgu_w_s,
              sgu_b_s=m_sgu_b_s, w_mem_kv=m_w_mem_kv, w_out=m_w_out, ln_mix_g=m_ln_mix_g, ln_mix_b=m_ln_mix_b,
              w_gate=m_w_gate, w_up=m_w_up, w_down=m_w_down, ln_ffn_g=m_ln_ffn_g, ln_ffn_b=m_ln_ffn_b)
  mom2 = dict(a_w_in=v_a_w_in, b_w_in=v_b_w_in, sgu_ln_g=v_sgu_ln_g, sgu_ln_b=v_sgu_ln_b, sgu_w_s=v_sgu_w_s,
              sgu_b_s=v_sgu_b_s, w_mem_kv=v_w_mem_kv, w_out=v_w_out, ln_mix_g=v_ln_mix_g, ln_mix_b=v_ln_mix_b,
              w_gate=v_w_gate, w_up=v_w_up, w_down=v_w_down, ln_ffn_g=v_ln_ffn_g, ln_ffn_b=v_ln_ffn_b)

  first, late = _weight_packs(weights)
  full = first.unpack(_all_gather_relayed(first.flat, "gather_first_weights"))
  for n in REPLICATED:
    full[n] = weights[n]
  loss_part, grad_x, grads = _local_step(x, mem, loss_target, full, late_weights=late,
                                         early_exchange=_early_exchange_begin)
  loss = lax.psum(loss_part, MESH_AXES)

  shard_shapes = {n: weights[n].shape for n, _ in SHARDED}
  red = _reduce_grads(grads, shard_shapes)
  chip = 2 * lax.axis_index("x") + lax.axis_index("y")
  for n, axis in SMALL_SHARDED:
    width = weights[n].shape[axis]
    red[n] = lax.dynamic_slice_in_dim(red[n], chip * width, width, axis)

  small_names = SMALL_ORDER
  def pack(d):
    flat = jnp.concatenate([d[n].reshape(-1) for n in small_names])
    return _pad_to(flat, _round_up(flat.shape[0], 8 * FLAT_COLS)).reshape(-1, FLAT_COLS)
  small_out = _adamw(pack(weights), pack(red), pack(mom1), pack(mom2), "adamw_small")
  delta, new_m, new_v = {}, {}, {}
  off = 0
  for n in small_names:
    sz = weights[n].size
    for dst, src in zip((delta, new_m, new_v), small_out):
      dst[n] = src.reshape(-1)[off:off + sz].reshape(weights[n].shape)
    off += sz
  for n, _ in SHARDED:
    delta[n], new_m[n], new_v[n] = _adamw(weights[n], red[n], mom1[n], mom2[n], f"adamw_{n}")

  return (loss, grad_x, *[red[n] for n in WEIGHT_NAMES], *[delta[n] for n in WEIGHT_NAMES],
          *[new_m[n] for n in WEIGHT_NAMES], *[new_v[n] for n in WEIGHT_NAMES])
```

```python
import math

import jax
import jax.numpy as jnp
from jax import lax
from jax.experimental import pallas as pl
from jax.experimental.pallas import tpu as pltpu

F32 = jnp.float32
BF16 = jnp.bfloat16

DEPTH = 4
HEAD_DIM = 64
N_DIL_HEADS = 12
DIL_WIDTH = N_DIL_HEADS * HEAD_DIM
DIL_PATTERNS = ((128, 1), (512, 4), (2048, 16))
BLOCK = 128
N_SGU_GROUPS = 12
SGU_WIDTH = N_SGU_GROUPS * 64
CHUNK = 128
N_MEM_HEADS = 4
MEM_WIDTH = N_MEM_HEADS * HEAD_DIM
DN_ALPHA = (2 * DEPTH) ** 0.25
LN_EPS = 1e-5
ATT_SCALE = HEAD_DIM ** -0.5
ADAM_LR = 0.001
ADAM_B1 = 0.9
ADAM_B2 = 0.999
ADAM_EPS = 1e-08
ADAM_WD = 0.01
ADAM_STEP = 10
NEG_BIG = -1e30

LANES = 128
FLAT_COLS = 1024
VMEM_LIMIT = 56 * 1024 * 1024
MESH_AXES = ("x", "y", "c")
MESH_ID = pl.DeviceIdType.MESH


def _tile(n, pref, align=LANES):
  if n <= pref:
    return n
  t = (pref // align) * align
  while t >= align:
    if n % t == 0:
      return t
    t -= align
  return n


def _params(sem):
  return pltpu.CompilerParams(dimension_semantics=sem, vmem_limit_bytes=VMEM_LIMIT)


def _dot(a, b):
  return jnp.dot(a, b, preferred_element_type=F32)


def _dot_nt(a, b):
  return lax.dot_general(a, b, (((1,), (1,)), ((), ())), preferred_element_type=F32)


def _dot_tn(a, b):
  return lax.dot_general(a, b, (((0,), (0,)), ((), ())), preferred_element_type=F32)


def _bf(v):
  return v.astype(BF16)


def _ln_stats(z):
  mu = jnp.mean(z, axis=-1, keepdims=True)
  zc = z - mu
  var = jnp.mean(zc * zc, axis=-1, keepdims=True)
  rstd = lax.rsqrt(var + LN_EPS)
  return zc * rstd, rstd


def _ln_bwd(dy, xhat, rstd, g):
  gdy = dy * g
  m1 = jnp.mean(gdy, axis=-1, keepdims=True)
  m2 = jnp.mean(gdy * xhat, axis=-1, keepdims=True)
  return rstd * (gdy - m1 - xhat * m2)


_GELU_C = math.sqrt(2.0 / math.pi)


def _gelu_parts(v):
  v2 = v * v
  t = jnp.tanh(_GELU_C * (v + 0.044715 * v * v2))
  val = 0.5 * v * (1.0 + t)
  der = 0.5 * (1.0 + t) + 0.5 * v * (1.0 - t * t) * (_GELU_C * (1.0 + 3.0 * 0.044715 * v2))
  return val, der


def _gelu(v):
  t = jnp.tanh(_GELU_C * (v + 0.044715 * v * v * v))
  return 0.5 * v * (1.0 + t)


def _sigmoid(v):
  return 1.0 / (1.0 + jnp.exp(-v))


def _mm(a, b, mode, out_dtype, name, add=None, add_scale=1.0, tm=512, tn=512, tk=512):
  if mode == "nn":
    (m, k), (k2, n) = a.shape, b.shape
  elif mode == "nt":
    (m, k), (n, k2) = a.shape, b.shape
  else:
    (k, m), (k2, n) = a.shape, b.shape
  assert k == k2, (a.shape, b.shape, mode)
  tm, tn, tk = _tile(m, tm), _tile(n, tn), _tile(k, tk)
  nk = k // tk
  if mode == "nn":
    a_spec = pl.BlockSpec((tm, tk), lambda i, j, kk: (i, kk))
    b_spec = pl.BlockSpec((tk, tn), lambda i, j, kk: (kk, j))
    dot = _dot
  elif mode == "nt":
    a_spec = pl.BlockSpec((tm, tk), lambda i, j, kk: (i, kk))
    b_spec = pl.BlockSpec((tn, tk), lambda i, j, kk: (j, kk))
    dot = _dot_nt
  else:
    a_spec = pl.BlockSpec((tk, tm), lambda i, j, kk: (kk, i))
    b_spec = pl.BlockSpec((tk, tn), lambda i, j, kk: (kk, j))
    dot = _dot_tn
  o_spec = pl.BlockSpec((tm, tn), lambda i, j, kk: (i, j))
  has_add = add is not None

  def body(*refs):
    if has_add:
      a_ref, b_ref, add_ref, o_ref, acc_ref = refs
    else:
      a_ref, b_ref, o_ref, acc_ref = refs
    kk = pl.program_id(2)

    @pl.when(kk == 0)
    def _():
      acc_ref[...] = jnp.zeros_like(acc_ref)

    acc_ref[...] += dot(_bf(a_ref[...]), _bf(b_ref[...]))

    @pl.when(kk == nk - 1)
    def _():
      r = acc_ref[...]
      if has_add:
        r = r + add_scale * add_ref[...].astype(F32)
      o_ref[...] = r.astype(out_dtype)

  in_specs = [a_spec, b_spec] + ([o_spec] if has_add else [])
  args = (a, b) + ((add,) if has_add else ())
  return pl.pallas_call(
      body, name=name, grid=(m // tm, n // tn, nk), in_specs=in_specs, out_specs=o_spec,
      out_shape=jax.ShapeDtypeStruct((m, n), out_dtype),
      scratch_shapes=[pltpu.VMEM((tm, tn), F32)],
      compiler_params=_params(("parallel", "parallel", "arbitrary")),
  )(*args)


def _mm_res_ln(a, w, res, g, b, name, tm=512, tk=512):
  m, k = a.shape
  d = w.shape[1]
  tm, tk = _tile(m, tm), _tile(k, tk)
  nk = k // tk

  def body(a_ref, w_ref, r_ref, g_ref, b_ref, z_ref, x_ref, xb_ref, acc_ref):
    kk = pl.program_id(1)

    @pl.when(kk == 0)
    def _():
      acc_ref[...] = jnp.zeros_like(acc_ref)

    acc_ref[...] += _dot(_bf(a_ref[...]), _bf(w_ref[...]))

    @pl.when(kk == nk - 1)
    def _():
      z = DN_ALPHA * r_ref[...] + acc_ref[...]
      xhat, _ = _ln_stats(z)
      xn = xhat * g_ref[...] + b_ref[...]
      z_ref[...] = z
      x_ref[...] = xn
      xb_ref[...] = _bf(xn)

  row = pl.BlockSpec((tm, d), lambda i, kk: (i, 0))
  vec = pl.BlockSpec((1, d), lambda i, kk: (0, 0))
  return pl.pallas_call(
      body, name=name, grid=(m // tm, nk),
      in_specs=[pl.BlockSpec((tm, tk), lambda i, kk: (i, kk)), pl.BlockSpec((tk, d), lambda i, kk: (kk, 0)), row, vec, vec],
      out_specs=[row, row, row],
      out_shape=[jax.ShapeDtypeStruct((m, d), F32), jax.ShapeDtypeStruct((m, d), F32), jax.ShapeDtypeStruct((m, d), BF16)],
      scratch_shapes=[pltpu.VMEM((tm, d), F32)],
      compiler_params=_params(("parallel", "arbitrary")),
  )(a, w, res, g.reshape(1, d), b.reshape(1, d))


def _ffn_up(xb, wg, wu, name, tm=512, tn=1408):
  m, d = xb.shape
  f = wg.shape[0]
  tm, tn = _tile(m, tm), _tile(f, tn)

  def body(x_ref, wg_ref, wu_ref, ga_ref, gb_ref, h_ref):
    xv = x_ref[...]
    a = _dot_nt(xv, wg_ref[...])
    b = _dot_nt(xv, wu_ref[...])
    sg = _sigmoid(a)
    silu = a * sg
    ga_ref[...] = _bf(b * (sg + silu * (1.0 - sg)))
    gb_ref[...] = _bf(silu)
    h_ref[...] = _bf(silu * b)

  wspec = pl.BlockSpec((tn, d), lambda j, i: (j, 0))
  ospec = pl.BlockSpec((tm, tn), lambda j, i: (i, j))
  sds = jax.ShapeDtypeStruct((m, f), BF16)
  return pl.pallas_call(
      body, name=name, grid=(f // tn, m // tm),
      in_specs=[pl.BlockSpec((tm, d), lambda j, i: (i, 0)), wspec, wspec],
      out_specs=[ospec, ospec, ospec], out_shape=[sds, sds, sds],
      compiler_params=_params(("parallel", "parallel")),
  )(xb, wg, wu)


def _ffn_bwd_hidden(dzb, wd, ga, gb, name, tm=512, tn=1408):
  m, d = dzb.shape
  f = wd.shape[0]
  tm, tn = _tile(m, tm), _tile(f, tn)

  def body(dz_ref, wd_ref, ga_ref, gb_ref, da_ref, db_ref):
    dh = _dot_nt(dz_ref[...], wd_ref[...])
    da_ref[...] = _bf(dh * ga_ref[...].astype(F32))
    db_ref[...] = _bf(dh * gb_ref[...].astype(F32))

  hspec = pl.BlockSpec((tm, tn), lambda j, i: (i, j))
  sds = jax.ShapeDtypeStruct((m, f), BF16)
  return pl.pallas_call(
      body, name=name, grid=(f // tn, m // tm),
      in_specs=[pl.BlockSpec((tm, d), lambda j, i: (i, 0)), pl.BlockSpec((tn, d), lambda j, i: (j, 0)), hspec, hspec],
      out_specs=[hspec, hspec], out_shape=[sds, sds],
      compiler_params=_params(("parallel", "parallel")),
  )(dzb, wd, ga, gb)


def _ln_bwd_tail(dy, z_ref, g_ref, dz_ref, dzb_ref, dg_ref, db_ref):
  @pl.when(pl.program_id(0) == 0)
  def _():
    dg_ref[...] = jnp.zeros_like(dg_ref)
    db_ref[...] = jnp.zeros_like(db_ref)

  xhat, rstd = _ln_stats(z_ref[...])
  dz = _ln_bwd(dy, xhat, rstd, g_ref[...])
  dz_ref[...] = dz
  dzb_ref[...] = _bf(dz)
  dg_ref[...] += jnp.sum(dy * xhat, axis=0, keepdims=True)
  db_ref[...] += jnp.sum(dy, axis=0, keepdims=True)


def _ln_bwd_outs(m, d, row, vec):
  return ([row, row, vec, vec],
          [jax.ShapeDtypeStruct((m, d), F32), jax.ShapeDtypeStruct((m, d), BF16),
           jax.ShapeDtypeStruct((1, d), F32), jax.ShapeDtypeStruct((1, d), F32)])


def _ffn_bwd_input_ln(da, db, wg, wu, dz2, z1, g, name, tm=512):
  m, f = da.shape
  d = wg.shape[1]
  tm = _tile(m, tm)

  def body(da_ref, db_ref, wg_ref, wu_ref, dz2_ref, z_ref, g_ref, dz_ref, dzb_ref, dg_ref, dbias_ref):
    dy = DN_ALPHA * dz2_ref[...] + _dot(da_ref[...], wg_ref[...]) + _dot(db_ref[...], wu_ref[...])
    _ln_bwd_tail(dy, z_ref, g_ref, dz_ref, dzb_ref, dg_ref, dbias_ref)

  hspec = pl.BlockSpec((tm, f), lambda i: (i, 0))
  wspec = pl.BlockSpec((f, d), lambda i: (0, 0), pipeline_mode=pl.Buffered(1))
  row = pl.BlockSpec((tm, d), lambda i: (i, 0))
  vec = pl.BlockSpec((1, d), lambda i: (0, 0))
  out_specs, out_shape = _ln_bwd_outs(m, d, row, vec)
  dz, dzb, dg, dbias = pl.pallas_call(
      body, name=name, grid=(m // tm,), in_specs=[hspec, hspec, wspec, wspec, row, row, vec],
      out_specs=out_specs, out_shape=out_shape, compiler_params=_params(("arbitrary",)),
  )(da, db, wg, wu, dz2, z1, g.reshape(1, d))
  return dz, dzb, dg[0], dbias[0]


def _in_proj_bwd_ln(dh, w_in, dz1, z2, g, name, tm=512):
  m, wd = dh.shape
  d = w_in.shape[1]
  tm = _tile(m, tm)

  def body(dh_ref, w_ref, dz1_ref, z_ref, g_ref, dz_ref, dzb_ref, dg_ref, dbias_ref):
    dy = DN_ALPHA * dz1_ref[...] + _dot(dh_ref[...], w_ref[...])
    _ln_bwd_tail(dy, z_ref, g_ref, dz_ref, dzb_ref, dg_ref, dbias_ref)

  row = pl.BlockSpec((tm, d), lambda i: (i, 0))
  vec = pl.BlockSpec((1, d), lambda i: (0, 0))
  out_specs, out_shape = _ln_bwd_outs(m, d, row, vec)
  dz, dzb, dg, dbias = pl.pallas_call(
      body, name=name, grid=(m // tm,),
      in_specs=[pl.BlockSpec((tm, wd), lambda i: (i, 0)),
                pl.BlockSpec((wd, d), lambda i: (0, 0), pipeline_mode=pl.Buffered(1)), row, row, vec],
      out_specs=out_specs, out_shape=out_shape, compiler_params=_params(("arbitrary",)),
  )(dh, w_in, dz1, z2, g.reshape(1, d))
  return dz, dzb, dg[0], dbias[0]


def _alibi_slopes():
  n = N_DIL_HEADS
  return jnp.exp2(-8.0 * (jnp.arange(n, dtype=F32) + 1.0) / n).reshape(1, n)


def _rows(start, d):
  if d == 1:
    return pl.ds(pl.multiple_of(start, BLOCK), BLOCK)
  return pl.ds(start, BLOCK, stride=d)


def _fill_bias_tables(bias_sc, slope0, slope1):
  row = lax.broadcasted_iota(jnp.int32, (2 * BLOCK, 2 * BLOCK), 0)
  col = lax.broadcasted_iota(jnp.int32, (2 * BLOCK, 2 * BLOCK), 1)
  qi = jnp.bitwise_and(row, BLOCK - 1)
  ki = jnp.bitwise_and(col, BLOCK - 1)
  is_cur = col >= BLOCK
  steps = jnp.where(is_cur, qi - ki, qi + BLOCK - ki)
  valid = jnp.logical_and(steps >= 0, steps <= BLOCK)
  slope = jnp.where(row >= BLOCK, slope1, slope0)
  dist = slope * steps.astype(F32)
  for p, (_, d) in enumerate(DIL_PATTERNS):
    base = jnp.where(valid, -d * dist, NEG_BIG)
    bias_sc[2 * p] = base
    bias_sc[2 * p + 1] = jnp.where(is_cur, base, NEG_BIG)


def _stack_heads(v2, head0):
  return jnp.concatenate([jnp.where(head0, v2, 0.0), jnp.where(head0, 0.0, v2)], axis=0)


def _unstack_heads(v, head0):
  return jnp.where(head0, v[:BLOCK], v[BLOCK:])


def _block_rows(idx, d, nblk):
  r = idx // nblk
  n = idx % nblk
  cur = _rows(r + n * (BLOCK * d), d)
  prev = _rows(r + jnp.maximum(n - 1, 0) * (BLOCK * d), d)
  return cur, prev, n


def pair_tile(dt):
  return pltpu.VMEM((2 * BLOCK, 2 * BLOCK), dt)


def _two_stage_loop(nb, first_stage, second_stage, bufs):
  a, b, c, d = bufs
  assert nb % 4 == 0 and nb >= 8

  def quad(u, carry):
    i = 4 * u
    first_stage(i + 2, c)
    first_stage(i + 3, d)
    second_stage(i, a)
    second_stage(i + 1, b)
    first_stage(i + 4, a)
    first_stage(i + 5, b)
    second_stage(i + 2, c)
    second_stage(i + 3, d)
    return carry

  first_stage(0, a)
  first_stage(1, b)
  lax.fori_loop(0, nb // 4 - 1, quad, 0)
  i = nb - 4
  first_stage(i + 2, c)
  first_stage(i + 3, d)
  for k, buf in enumerate(bufs):
    second_stage(i + k, buf)


def _attn_fwd(h3, name, gather=None):
  bl, s, _ = h3.shape
  npair = N_DIL_HEADS // 2
  nb = s // BLOCK
  hosted = gather is not None
  steps = bl * npair

  def body(*refs):
    if hosted:
      sl_ref, q_ref, k_ref, v_ref, w_ref, o_ref, lse_ref, g_ref, o_sc, l_sc, bias_sc, *s_bufs, send_sems, recv_sems = refs
      step = pl.program_id(0) * bl + pl.program_id(1)
      for phase, at in enumerate((0, (3 * steps) // 4)):
        @pl.when(step == at)
        def _(phase=phase):
          _relayed_gather_phase(phase, w_ref, g_ref, send_sems, recv_sems)
    else:
      sl_ref, q_ref, k_ref, v_ref, o_ref, lse_ref, o_sc, l_sc, bias_sc, *s_bufs = refs
    hp = pl.program_id(0)
    head0 = lax.broadcasted_iota(jnp.int32, (BLOCK, LANES), 1) < 64

    @pl.when(pl.program_id(1) == 0)
    def _():
      _fill_bias_tables(bias_sc, sl_ref[0, 2 * hp], sl_ref[0, 2 * hp + 1])

    for p, (_, d) in enumerate(DIL_PATTERNS):
      nblk = (s // d) // BLOCK
      two = nblk > 1
      ks = slice(0, 2 * BLOCK) if two else slice(BLOCK, 2 * BLOCK)

      def scores(idx, buf, p=p, d=d, nblk=nblk, two=two, ks=ks):
        cur, prev, n = _block_rows(idx, d, nblk)
        qs = _bf(_stack_heads(q_ref[cur, :], head0) * ATT_SCALE)
        kb = _bf(jnp.concatenate([k_ref[prev, :], k_ref[cur, :]], axis=0)) if two else _bf(k_ref[cur, :])
        first = jnp.where(n == 0, 1, 0) if two else 0
        buf[:, ks] = _dot_nt(qs, kb) + bias_sc[2 * p + first, :, ks]

      def values(idx, buf, p=p, d=d, nblk=nblk, two=two, ks=ks):
        cur, prev, _ = _block_rows(idx, d, nblk)
        sc = buf[:, ks]
        mx = jnp.max(sc, axis=1, keepdims=True)
        pe = jnp.exp(sc - mx)
        den = jnp.sum(pe, axis=1, keepdims=True)
        vb = _bf(jnp.concatenate([v_ref[prev, :], v_ref[cur, :]], axis=0)) if two else _bf(v_ref[cur, :])
        acc = _dot(_bf(pe), vb) / den
        o_sc[p, cur, :] = _unstack_heads(acc, head0)
        l_sc[p, cur, :] = _unstack_heads(jnp.broadcast_to(mx + jnp.log(den), (2 * BLOCK, LANES)), head0)

      _two_stage_loop(nb, scores, values, s_bufs)

    def merge(i, carry):
      rows = pl.ds(pl.multiple_of(i * BLOCK, BLOCK), BLOCK)
      l0, l1, l2 = l_sc[0, rows, :], l_sc[1, rows, :], l_sc[2, rows, :]
      mx = jnp.maximum(jnp.maximum(l0, l1), l2)
      e0, e1, e2 = jnp.exp(l0 - mx), jnp.exp(l1 - mx), jnp.exp(l2 - mx)
      tot = e0 + e1 + e2
      o_ref[rows, :] = _bf((e0 * o_sc[0, rows, :] + e1 * o_sc[1, rows, :] + e2 * o_sc[2, rows, :]) / tot)
      lse_ref[rows, :] = mx + jnp.log(tot)
      return carry

    lax.fori_loop(0, nb, merge, 0)

    if hosted:
      @pl.when(step == steps - 1)
      def _():
        _relayed_gather_phase(2, w_ref, g_ref, send_sems, recv_sems)

  def col(off):
    return pl.BlockSpec((None, s, LANES), lambda p, b: (b, 0, off + p))

  in_specs = [pl.BlockSpec(memory_space=pltpu.SMEM), col(0), col(npair), col(2 * npair)]
  out_specs = [col(0), col(0)]
  out_shape = [jax.ShapeDtypeStruct((bl, s, DIL_WIDTH), BF16), jax.ShapeDtypeStruct((bl, s, DIL_WIDTH), F32)]
  scratch = [pltpu.VMEM((3, s, LANES), F32), pltpu.VMEM((3, s, LANES), F32),
             pltpu.VMEM((6, 2 * BLOCK, 2 * BLOCK), F32)] + [pair_tile(F32)] * 4
  args = (_alibi_slopes(), h3, h3, h3)
  if hosted:
    assert (gather.shape[1] // 2) % ROW_ALIGN == 0 and steps >= 4
    in_specs.append(ANY)
    out_specs.append(ANY)
    out_shape.append(jax.ShapeDtypeStruct((4,) + gather.shape, gather.dtype))
    scratch += [pltpu.SemaphoreType.DMA((N_RELAY_COPIES,)), pltpu.SemaphoreType.DMA((N_RELAY_COPIES,))]
    args += (gather,)
  sem = ("arbitrary", "arbitrary") if hosted else ("parallel", "arbitrary")
  outs = list(pl.pallas_call(
      body, name=name, grid=(npair, bl), in_specs=in_specs, out_specs=out_specs, out_shape=out_shape,
      scratch_shapes=scratch, compiler_params=_params(sem),
  )(*args))
  if hosted:
    outs[2] = _place_own_block(outs[2], gather)
  return outs


def _attn_bwd(h3, out3, lse3, dcat3, name, exchange=None):
  bl, s, _ = h3.shape
  npair = N_DIL_HEADS // 2
  nb = s // BLOCK
  hosted = exchange is not None

  def body(*refs):
    if hosted:
      (sl_ref, q_ref, k_ref, v_ref, o_ref, l_ref, do_ref, ex_ref, dq_out, dk_out, dv_out, got_ref,
       bias_sc, *pd, prod_sc, dq_ref, dk_ref, dv_ref, send_sems, recv_sems) = refs
      step = pl.program_id(0) * bl + pl.program_id(1)

      @pl.when(step == 0)
      def _():
        for cp in _chip_exchange_copies(ex_ref, got_ref, send_sems, recv_sems):
          cp.start()
    else:
      (sl_ref, q_ref, k_ref, v_ref, o_ref, l_ref, do_ref, dq_out, dk_out, dv_out,
       bias_sc, *pd, prod_sc, dq_ref, dk_ref, dv_ref) = refs
    pd_bufs = list(zip(pd[0::2], pd[1::2]))
    hp = pl.program_id(0)
    lane = lax.broadcasted_iota(jnp.int32, (BLOCK, LANES), 1)
    head0 = lane < 64

    @pl.when(pl.program_id(1) == 0)
    def _():
      _fill_bias_tables(bias_sc, sl_ref[0, 2 * hp], sl_ref[0, 2 * hp + 1])
    dq_ref[...] = jnp.zeros_like(dq_ref)
    dk_ref[...] = jnp.zeros_like(dk_ref)
    dv_ref[...] = jnp.zeros_like(dv_ref)
    prod_sc[...] = do_ref[...] * o_ref[...].astype(F32)

    def per_row(v2, pick0, pick1):
      return jnp.concatenate([jnp.sum(jnp.where(pick0, v2, 0.0), axis=1, keepdims=True),
                              jnp.sum(jnp.where(pick1, v2, 0.0), axis=1, keepdims=True)], axis=0)

    for p, (_, d) in enumerate(DIL_PATTERNS):
      nblk = (s // d) // BLOCK
      two = nblk > 1
      ks = slice(0, 2 * BLOCK) if two else slice(BLOCK, 2 * BLOCK)

      def operands(idx, d=d, nblk=nblk, two=two):
        cur, prev, n = _block_rows(idx, d, nblk)
        qs = _bf(_stack_heads(q_ref[cur, :], head0) * ATT_SCALE)
        dos = _bf(_stack_heads(do_ref[cur, :], head0))
        kb = _bf(jnp.concatenate([k_ref[prev, :], k_ref[cur, :]], axis=0)) if two else _bf(k_ref[cur, :])
        return cur, prev, n, qs, dos, kb

      def probs(idx, bufs, p=p, two=two, ks=ks, operands=operands):
        cur, prev, n, qs, dos, kb = operands(idx)
        vb = _bf(jnp.concatenate([v_ref[prev, :], v_ref[cur, :]], axis=0)) if two else _bf(v_ref[cur, :])
        lse = per_row(l_ref[cur, :], lane == 0, lane == 64)
        delta = per_row(prod_sc[cur, :], head0, jnp.logical_not(head0))
        first = jnp.where(n == 0, 1, 0) if two else 0
        pr = jnp.exp(_dot_nt(qs, kb) + bias_sc[2 * p + first, :, ks] - lse)
        bufs[0][:, ks] = _bf(pr)
        bufs[1][:, ks] = _bf(pr * (_dot_nt(dos, vb) - delta))

      def products(idx, bufs, two=two, ks=ks, operands=operands):
        cur, prev, _, qs, dos, kb = operands(idx)
        pr = bufs[0][:, ks]
        ds = bufs[1][:, ks]
        dq_ref[cur, :] += _unstack_heads(_dot(ds, kb), head0) * ATT_SCALE
        dkb = _dot_tn(ds, qs)
        dvb = _dot_tn(pr, dos)
        if two:
          dk_ref[prev, :] += dkb[:BLOCK]
          dv_ref[prev, :] += dvb[:BLOCK]
          dk_ref[cur, :] += dkb[BLOCK:]
          dv_ref[cur, :] += dvb[BLOCK:]
        else:
          dk_ref[cur, :] += dkb
          dv_ref[cur, :] += dvb

      _two_stage_loop(nb, probs, products, pd_bufs)

    dq_out[...] = _bf(dq_ref[...])
    dk_out[...] = _bf(dk_ref[...])
    dv_out[...] = _bf(dv_ref[...])

    if hosted:
      @pl.when(step == bl * npair - 1)
      def _():
        for cp in _chip_exchange_copies(ex_ref, got_ref, send_sems, recv_sems):
          cp.wait()

  def col(off):
    return pl.BlockSpec((None, s, LANES), lambda p, b: (b, 0, off + p))

  sds = jax.ShapeDtypeStruct((bl, s, DIL_WIDTH), BF16)
  in_specs = [pl.BlockSpec(memory_space=pltpu.SMEM), col(0), col(npair), col(2 * npair), col(0), col(0), col(0)]
  out_specs, out_shape = [col(0), col(0), col(0)], [sds, sds, sds]
  scratch = [pltpu.VMEM((6, 2 * BLOCK, 2 * BLOCK), F32)] + [pair_tile(BF16)] * 8 + [pltpu.VMEM((s, LANES), F32)] * 4
  args = (_alibi_slopes(), h3, h3, h3, out3, lse3, dcat3)
  if hosted:
    in_specs.append(ANY)
    out_specs.append(ANY)
    out_shape.append(jax.ShapeDtypeStruct((3,) + exchange.shape[1:], exchange.dtype))
    scratch += [pltpu.SemaphoreType.DMA((3,)), pltpu.SemaphoreType.DMA((3,))]
    args += (exchange,)
  sem = ("arbitrary", "arbitrary") if hosted else ("parallel", "arbitrary")
  return pl.pallas_call(
      body, name=name, grid=(npair, bl), in_specs=in_specs, out_specs=out_specs, out_shape=out_shape,
      scratch_shapes=scratch, compiler_params=_params(sem),
  )(*args)


def _mem_heads(tq):
  lane = lax.broadcasted_iota(jnp.int32, (tq, LANES), 1)
  return lane < 64


def _mem_fwd(h3, qcol, mkv3, name, tq=512):
  bl, s, _ = h3.shape
  nm = mkv3.shape[1]
  tq = _tile(s, tq)

  def body(q_ref, kv_ref, o_ref):
    head0 = _mem_heads(tq)
    for lg in range(MEM_WIDTH // LANES):
      cs = slice(lg * LANES, (lg + 1) * LANES)
      q2 = q_ref[:, cs]
      mk = _bf(kv_ref[:, cs])
      mv = _bf(kv_ref[:, MEM_WIDTH + lg * LANES:MEM_WIDTH + (lg + 1) * LANES])
      outs = []
      for j in range(2):
        hm = head0 if j == 0 else jnp.logical_not(head0)
        qj = _bf(jnp.where(hm, q2, 0.0) * ATT_SCALE)
        sc = _dot_nt(qj, mk)
        mx = jnp.max(sc, axis=1, keepdims=True)
        pe = jnp.exp(sc - mx)
        den = jnp.sum(pe, axis=1, keepdims=True)
        outs.append(_dot(_bf(pe / den), mv))
      o_ref[:, cs] = _bf(jnp.where(head0, outs[0], outs[1]))

  return pl.pallas_call(
      body, name=name, grid=(bl, s // tq),
      in_specs=[pl.BlockSpec((None, tq, MEM_WIDTH), lambda b, i: (b, i, qcol)),
                pl.BlockSpec((None, nm, 2 * MEM_WIDTH), lambda b, i: (b, 0, 0))],
      out_specs=pl.BlockSpec((None, tq, MEM_WIDTH), lambda b, i: (b, i, 0)),
      out_shape=jax.ShapeDtypeStruct((bl, s, MEM_WIDTH), BF16),
      compiler_params=_params(("parallel", "parallel")),
  )(h3, mkv3)


def _mem_bwd(h3, qcol, mkv3, dcat3, name, tq=512):
  bl, s, _ = h3.shape
  nm = mkv3.shape[1]
  tq = _tile(s, tq)
  docol = dcat3.shape[2] // MEM_WIDTH - 1

  def body(q_ref, kv_ref, do_ref, dq_ref, dkv_ref):
    i = pl.program_id(1)

    @pl.when(i == 0)
    def _():
      dkv_ref[...] = jnp.zeros_like(dkv_ref)

    head0 = _mem_heads(tq)
    for lg in range(MEM_WIDTH // LANES):
      cs = slice(lg * LANES, (lg + 1) * LANES)
      vs = slice(MEM_WIDTH + lg * LANES, MEM_WIDTH + (lg + 1) * LANES)
      q2 = q_ref[:, cs]
      do2 = do_ref[:, cs]
      mk = _bf(kv_ref[:, cs])
      mv = _bf(kv_ref[:, vs])
      dq2 = jnp.zeros((tq, LANES), F32)
      dmk = jnp.zeros((nm, LANES), F32)
      dmv = jnp.zeros((nm, LANES), F32)
      for j in range(2):
        hm = head0 if j == 0 else jnp.logical_not(head0)
        qj = _bf(jnp.where(hm, q2, 0.0) * ATT_SCALE)
        doj = _bf(jnp.where(hm, do2, 0.0))
        sc = _dot_nt(qj, mk)
        mx = jnp.max(sc, axis=1, keepdims=True)
        pe = jnp.exp(sc - mx)
        pn = pe / jnp.sum(pe, axis=1, keepdims=True)
        pb = _bf(pn)
        dp = _dot_nt(doj, mv)
        dj = jnp.sum(pb.astype(F32) * dp, axis=1, keepdims=True)
        ds = _bf(pn * (dp - dj))
        dq2 = dq2 + jnp.where(hm, _dot(ds, mk), 0.0) * ATT_SCALE
        dmk = dmk + _dot_tn(ds, qj)
        dmv = dmv + _dot_tn(pb, doj)
      dq_ref[:, cs] = _bf(dq2)
      dkv_ref[:, cs] += dmk
      dkv_ref[:, vs] += dmv

  return pl.pallas_call(
      body, name=name, grid=(bl, s // tq),
      in_specs=[pl.BlockSpec((None, tq, MEM_WIDTH), lambda b, i: (b, i, qcol)),
                pl.BlockSpec((None, nm, 2 * MEM_WIDTH), lambda b, i: (b, 0, 0)),
                pl.BlockSpec((None, tq, MEM_WIDTH), lambda b, i: (b, i, docol))],
      out_specs=[pl.BlockSpec((None, tq, MEM_WIDTH), lambda b, i: (b, i, 0)),
                 pl.BlockSpec((None, nm, 2 * MEM_WIDTH), lambda b, i: (b, 0, 0))],
      out_shape=[jax.ShapeDtypeStruct((bl, s, MEM_WIDTH), BF16), jax.ShapeDtypeStruct((bl, nm, 2 * MEM_WIDTH), F32)],
      compiler_params=_params(("parallel", "arbitrary")),
  )(h3, mkv3, dcat3)


def _sgu_consts():
  ti = lax.broadcasted_iota(jnp.int32, (CHUNK, CHUNK), 0)
  si = lax.broadcasted_iota(jnp.int32, (CHUNK, CHUNK), 1)
  return si <= ti, si < 64


def _sgu_bias_lanes(b_s):
  return jnp.repeat(b_s.T, 64, axis=1)


def _sgu_fwd(h2, ln_g, ln_b, w_s, b_s, name, tr=512):
  t, _ = h2.shape
  tr = _tile(t, tr)
  nch = tr // CHUNK
  npair = N_SGU_GROUPS // 2

  def body(u_ref, v_ref, g_ref, b_ref, w_ref, bs_ref, o_ref, vn_sc):
    tril, head0 = _sgu_consts()
    xhat, _ = _ln_stats(_gelu(v_ref[...]))
    vn_sc[...] = _bf(xhat * g_ref[...] + b_ref[...])
    for jp in range(npair):
      cs = slice(jp * LANES, (jp + 1) * LANES)
      w0 = _bf(jnp.where(tril, w_ref[2 * jp], 0.0))
      w1 = _bf(jnp.where(tril, w_ref[2 * jp + 1], 0.0))
      bias = bs_ref[:, cs]
      for c in range(nch):
        rs = slice(c * CHUNK, (c + 1) * CHUNK)
        vb = vn_sc[rs, cs]
        mixed = jnp.where(head0, _dot(w0, vb), _dot(w1, vb)) + bias
        o_ref[rs, cs] = _bf(_gelu(u_ref[rs, cs]) * mixed)

  blk = lambda j: pl.BlockSpec((tr, SGU_WIDTH), lambda i: (i, j))
  vec = pl.BlockSpec((1, SGU_WIDTH), lambda i: (0, 0))
  return pl.pallas_call(
      body, name=name, grid=(t // tr,),
      in_specs=[blk(0), blk(1), vec, vec,
                pl.BlockSpec((N_SGU_GROUPS, CHUNK, CHUNK), lambda i: (0, 0, 0)),
                pl.BlockSpec((CHUNK, SGU_WIDTH), lambda i: (0, 0))],
      out_specs=blk(0), out_shape=jax.ShapeDtypeStruct((t, SGU_WIDTH), BF16),
      scratch_shapes=[pltpu.VMEM((tr, SGU_WIDTH), BF16)],
      compiler_params=_params(("parallel",)),
  )(h2, h2, ln_g.reshape(1, -1), ln_b.reshape(1, -1), w_s, _sgu_bias_lanes(b_s))


def _sgu_bwd(h2, dcat, ln_g, ln_b, w_s, b_s, name, tr=512):
  t, _ = h2.shape
  tr = _tile(t, tr)
  nch = tr // CHUNK
  npair = N_SGU_GROUPS // 2
  nsteps = t // tr

  def body(u_ref, v_ref, dm_ref, g_ref, b_ref, w_ref, bs_ref,
           du_ref, dv_ref, dw_ref, dbs_ref, dg_ref, db_ref, vn_sc, dmx_sc, dvn_sc, mix_sc, dbx_sc):
    i = pl.program_id(0)
    tril, head0 = _sgu_consts()

    @pl.when(i == 0)
    def _():
      dw_ref[...] = jnp.zeros_like(dw_ref)
      dg_ref[...] = jnp.zeros_like(dg_ref)
      db_ref[...] = jnp.zeros_like(db_ref)
      dbx_sc[...] = jnp.zeros_like(dbx_sc)

    gv, gv_der = _gelu_parts(v_ref[...])
    xhat, rstd = _ln_stats(gv)
    g = g_ref[...]
    vn_sc[...] = _bf(xhat * g + b_ref[...])
    gu, gu_der = _gelu_parts(u_ref[...])
    dmix = dm_ref[...]
    dmx_sc[...] = dmix * gu

    for jp in range(npair):
      cs = slice(jp * LANES, (jp + 1) * LANES)
      w0 = _bf(jnp.where(tril, w_ref[2 * jp], 0.0))
      w1 = _bf(jnp.where(tril, w_ref[2 * jp + 1], 0.0))
      bias = bs_ref[:, cs]
      dw0 = jnp.zeros((CHUNK, CHUNK), F32)
      dw1 = jnp.zeros((CHUNK, CHUNK), F32)
      dbx = jnp.zeros((CHUNK, LANES), F32)
      for c in range(nch):
        rs = slice(c * CHUNK, (c + 1) * CHUNK)
        vb = vn_sc[rs, cs]
        mix_sc[rs, cs] = jnp.where(head0, _dot(w0, vb), _dot(w1, vb)) + bias
        dmx = dmx_sc[rs, cs]
        d0 = _bf(jnp.where(head0, dmx, 0.0))
        d1 = _bf(jnp.where(head0, 0.0, dmx))
        dvn_sc[rs, cs] = _dot_tn(w0, d0) + _dot_tn(w1, d1)
        dw0 = dw0 + _dot_nt(d0, vb)
        dw1 = dw1 + _dot_nt(d1, vb)
        dbx = dbx + dmx
      dw_ref[2 * jp] += dw0
      dw_ref[2 * jp + 1] += dw1
      dbx_sc[:, cs] += dbx

    du_ref[...] = _bf(dmix * mix_sc[...] * gu_der)
    dvn = dvn_sc[...]
    dv_ref[...] = _bf(_ln_bwd(dvn, xhat, rstd, g) * gv_der)
    dg_ref[...] += jnp.sum(dvn * xhat, axis=0, keepdims=True)
    db_ref[...] += jnp.sum(dvn, axis=0, keepdims=True)

    @pl.when(i == nsteps - 1)
    def _():
      lane = lax.broadcasted_iota(jnp.int32, (CHUNK, LANES), 1)
      acc = jnp.zeros((CHUNK, LANES), F32)
      for gi in range(N_SGU_GROUPS):
        jp, j = gi // 2, gi % 2
        part = dbx_sc[:, jp * LANES:(jp + 1) * LANES]
        hm = (lane < 64) if j == 0 else (lane >= 64)
        colsum = jnp.sum(jnp.where(hm, part, 0.0), axis=1, keepdims=True)
        acc = jnp.where(lane == gi, colsum, acc)
        dw_ref[gi] = jnp.where(tril, dw_ref[gi], 0.0)
      dbs_ref[...] = acc

  blk = lambda j: pl.BlockSpec((tr, SGU_WIDTH), lambda i: (i, j))
  vec = pl.BlockSpec((1, SGU_WIDTH), lambda i: (0, 0))
  wspec = pl.BlockSpec((N_SGU_GROUPS, CHUNK, CHUNK), lambda i: (0, 0, 0))
  big = lambda dt: pltpu.VMEM((tr, SGU_WIDTH), dt)
  du, dv, dw, dbs, dg, db = pl.pallas_call(
      body, name=name, grid=(nsteps,),
      in_specs=[blk(0), blk(1), blk(0), vec, vec, wspec, pl.BlockSpec((CHUNK, SGU_WIDTH), lambda i: (0, 0))],
      out_specs=[blk(0), blk(0), wspec, pl.BlockSpec((CHUNK, LANES), lambda i: (0, 0)), vec, vec],
      out_shape=[jax.ShapeDtypeStruct((t, SGU_WIDTH), BF16), jax.ShapeDtypeStruct((t, SGU_WIDTH), BF16),
                 jax.ShapeDtypeStruct((N_SGU_GROUPS, CHUNK, CHUNK), F32), jax.ShapeDtypeStruct((CHUNK, LANES), F32),
                 jax.ShapeDtypeStruct((1, SGU_WIDTH), F32), jax.ShapeDtypeStruct((1, SGU_WIDTH), F32)],
      scratch_shapes=[big(BF16), big(F32), big(F32), big(F32), pltpu.VMEM((CHUNK, SGU_WIDTH), F32)],
      compiler_params=_params(("arbitrary",)),
  )(h2, h2, dcat, ln_g.reshape(1, -1), ln_b.reshape(1, -1), w_s, _sgu_bias_lanes(b_s))
  return du, dv, dw, dbs[:, :N_SGU_GROUPS].T, dg[0], db[0]


def _loss_head(xo, tgt, z, g, name, tm=512):
  m, d = xo.shape
  tm = _tile(m, tm)

  def body(x_ref, t_ref, z_ref, g_ref, l_ref, dz_ref, dzb_ref, dg_ref, dbias_ref):
    @pl.when(pl.program_id(0) == 0)
    def _():
      l_ref[...] = jnp.zeros_like(l_ref)

    diff = x_ref[...] - t_ref[...]
    rowsum = jnp.sum(diff * diff, axis=1, keepdims=True)
    tot = jnp.sum(rowsum, axis=0, keepdims=True) * (0.5 / d)
    l_ref[...] += jnp.broadcast_to(tot, l_ref.shape)
    _ln_bwd_tail(diff * (1.0 / d), z_ref, g_ref, dz_ref, dzb_ref, dg_ref, dbias_ref)

  row = pl.BlockSpec((tm, d), lambda i: (i, 0))
  vec = pl.BlockSpec((1, d), lambda i: (0, 0))
  out_specs, out_shape = _ln_bwd_outs(m, d, row, vec)
  l, dz, dzb, dg, dbias = pl.pallas_call(
      body, name=name, grid=(m // tm,), in_specs=[row, row, row, vec],
      out_specs=[pl.BlockSpec((8, LANES), lambda i: (0, 0))] + out_specs,
      out_shape=[jax.ShapeDtypeStruct((8, LANES), F32)] + out_shape,
      compiler_params=_params(("arbitrary",)),
  )(xo, tgt, z, g.reshape(1, d))
  return l[0, 0], dz, dzb, dg[0], dbias[0]


def _local_step(x3, mem3, tgt3, w, late_weights=None, early_exchange=None):
  w = dict(w)
  bl, s, d = x3.shape
  t = bl * s
  nm = mem3.shape[1]
  mem2 = mem3.reshape(bl * nm, d)
  x = x3.reshape(t, d)
  xb = x
  saved = []
  for i in range(DEPTH):
    j = i // 2
    attn = i % 2 == 0
    mkv = _mm(mem2, w["w_mem_kv"][i], "nn", F32, f"mkv_fwd_{i}", tm=1024, tn=512, tk=1024)
    mkv3 = mkv.reshape(bl, nm, 2 * MEM_WIDTH)
    w_in = w["a_w_in"][j] if attn else w["b_w_in"][j]
    h = _mm(xb, w_in, "nt", F32, f"in_proj_{i}", tm=512, tn=w_in.shape[0], tk=d)
    h3 = h.reshape(bl, s, -1)
    if attn and late_weights is not None and i in late_weights:
      mix3, lse3, gathered = _attn_fwd(h3, f"dil_attn_fwd_{i}", gather=late_weights[i].flat)
      for n, layers in late_weights[i].unpack(gathered).items():
        w[n] = {**w.get(n, {}), **layers}
    elif attn:
      mix3, lse3 = _attn_fwd(h3, f"dil_attn_fwd_{i}")
    if attn:
      mix = mix3.reshape(t, DIL_WIDTH)
      qcol = 3 * DIL_WIDTH // MEM_WIDTH
    else:
      mix = _sgu_fwd(h, w["sgu_ln_g"][j], w["sgu_ln_b"][j], w["sgu_w_s"][j], w["sgu_b_s"][j], f"sgu_fwd_{i}")
      lse3 = None
      qcol = 2 * SGU_WIDTH // MEM_WIDTH
    mo = _mem_fwd(h3, qcol, mkv3, f"mem_attn_fwd_{i}").reshape(t, MEM_WIDTH)
    cat = jnp.concatenate([mix, mo], axis=1)
    z1, xm, xmb = _mm_res_ln(cat, w["w_out"][i], x, w["ln_mix_g"][i], w["ln_mix_b"][i], f"out_proj_ln_{i}", tk=1024)
    ga, gb, hm = _ffn_up(xmb, w["w_gate"][i], w["w_up"][i], f"ffn_up_{i}")
    z2, xo, xob = _mm_res_ln(hm, w["w_down"][i], xm, w["ln_ffn_g"][i], w["ln_ffn_b"][i], f"ffn_down_ln_{i}", tk=hm.shape[1])
    saved.append(dict(xb=xb, h=h, h3=h3, mkv3=mkv3, mix3=(mix3 if attn else None), lse3=lse3, cat=cat, z1=z1,
                      xmb=xmb, ga=ga, gb=gb, hm=hm, z2=z2, qcol=qcol))
    x, xb = xo, xob

  names = ("a_w_in", "b_w_in", "sgu_ln_g", "sgu_ln_b", "sgu_w_s", "sgu_b_s", "w_mem_kv", "w_out",
           "ln_mix_g", "ln_mix_b", "w_gate", "w_up", "w_down", "ln_ffn_g", "ln_ffn_b")
  grads = {n: [None] * len(w[n]) for n in names}
  last = DEPTH - 1
  loss, dz2, dz2b, grads["ln_ffn_g"][last], grads["ln_ffn_b"][last] = _loss_head(
      x, tgt3.reshape(t, d), saved[last]["z2"], w["ln_ffn_g"][last], "loss_head")
  dx = None
  for i in reversed(range(DEPTH)):
    j = i // 2
    attn = i % 2 == 0
    sv = saved[i]
    da, db = _ffn_bwd_hidden(dz2b, w["w_down"][i], sv["ga"], sv["gb"], f"ffn_bwd_hidden_{i}")
    grads["w_down"][i] = _mm(sv["hm"], dz2b, "tn", F32, f"dw_down_{i}", tm=1408, tn=1024, tk=1024)
    grads["w_gate"][i] = _mm(da, sv["xmb"], "tn", F32, f"dw_gate_{i}", tm=1408, tn=1024, tk=1024)
    grads["w_up"][i] = _mm(db, sv["xmb"], "tn", F32, f"dw_up_{i}", tm=1408, tn=1024, tk=1024)
    dz1, dz1b, grads["ln_mix_g"][i], grads["ln_mix_b"][i] = _ffn_bwd_input_ln(
        da, db, w["w_gate"][i], w["w_up"][i], dz2, sv["z1"], w["ln_mix_g"][i], f"ffn_bwd_input_ln_{i}")
    grads["w_out"][i] = _mm(sv["cat"], dz1b, "tn", F32, f"dw_out_{i}", tm=1024, tn=1024, tk=1024)
    dcat = _mm(dz1b, w["w_out"][i], "nt", F32, f"out_proj_bwd_{i}", tm=1024, tn=1024, tk=1024)
    dcat3 = dcat.reshape(bl, s, -1)
    dqm3, dmkv3 = _mem_bwd(sv["h3"], sv["qcol"], sv["mkv3"], dcat3, f"mem_attn_bwd_{i}")
    grads["w_mem_kv"][i] = _mm(mem2, dmkv3.reshape(bl * nm, 2 * MEM_WIDTH), "tn", F32, f"dw_mem_kv_{i}", tm=1024, tn=512, tk=1024)
    dqm = dqm3.reshape(t, MEM_WIDTH)
    if attn and i == 0 and early_exchange is not None:
      q, (pack, state) = early_exchange(grads)
      dq3, dk3, dv3, x3 = _attn_bwd(sv["h3"], sv["mix3"], sv["lse3"], dcat3, f"dil_attn_bwd_{i}", exchange=q)
      grads["early_exchange"] = (pack, state, x3)
      parts = [dq3.reshape(t, -1), dk3.reshape(t, -1), dv3.reshape(t, -1), dqm]
    elif attn:
      dq3, dk3, dv3 = _attn_bwd(sv["h3"], sv["mix3"], sv["lse3"], dcat3, f"dil_attn_bwd_{i}")
      parts = [dq3.reshape(t, -1), dk3.reshape(t, -1), dv3.reshape(t, -1), dqm]
    else:
      du, dv, dws, dbs, dlg, dlb = _sgu_bwd(sv["h"], dcat, w["sgu_ln_g"][j], w["sgu_ln_b"][j], w["sgu_w_s"][j],
                                             w["sgu_b_s"][j], f"sgu_bwd_{i}")
      grads["sgu_w_s"][j], grads["sgu_b_s"][j], grads["sgu_ln_g"][j], grads["sgu_ln_b"][j] = dws, dbs, dlg, dlb
      parts = [du, dv, dqm]
    dh = jnp.concatenate(parts, axis=1)
    w_in = w["a_w_in"][j] if attn else w["b_w_in"][j]
    grads["a_w_in" if attn else "b_w_in"][j] = _mm(dh, sv["xb"], "tn", F32, f"dw_in_{i}", tm=1280 if attn else 896, tn=1024, tk=1024)
    if i > 0:
      dz2, dz2b, grads["ln_ffn_g"][i - 1], grads["ln_ffn_b"][i - 1] = _in_proj_bwd_ln(
          dh, w_in, dz1, saved[i - 1]["z2"], w["ln_ffn_g"][i - 1], f"in_proj_bwd_ln_{i}")
    else:
      dx = _mm(dh, w_in, "nn", F32, f"in_proj_bwd_{i}", add=dz1, add_scale=DN_ALPHA, tm=512, tn=d, tk=w_in.shape[0])
  return loss, dx.reshape(bl, s, d), grads


def _my_place():
  return lax.axis_index("x"), lax.axis_index("y"), lax.axis_index("c")


def _other_chips(x, y):
  return [(1 - x, y), (x, 1 - y), (1 - x, 1 - y)]


ANY = pl.BlockSpec(memory_space=pl.ANY)


def _all_gather_halves(wl, name):
  _, r, c_ = wl.shape

  def body(w_ref, g_ref, send_sems, recv_sems):
    x, y, c = _my_place()
    me = 2 * x + y
    sibling = (x, y, 1 - c)
    chips = _other_chips(x, y)

    def copy(k, src, dst, to):
      return pltpu.make_async_remote_copy(src_ref=src, dst_ref=dst, send_sem=send_sems.at[k], recv_sem=recv_sems.at[k],
                                          device_id=to, device_id_type=MESH_ID)

    first = [copy(k, w_ref.at[c], g_ref.at[me, c], (px, py, c)) for k, (px, py) in enumerate(chips)]
    for cp in first:
      cp.start()
    passed = []
    for k, (px, py) in enumerate(chips):
      landed = g_ref.at[2 * px + py, c]
      copy(k, landed, landed, (px, py, c)).wait_recv()
      fwd = copy(3 + k, landed, landed, sibling)
      fwd.start()
      passed.append(fwd)
    for k, (px, py) in enumerate(chips):
      theirs = g_ref.at[2 * px + py, 1 - c]
      copy(3 + k, theirs, theirs, sibling).wait_recv()
    for cp in first + passed:
      cp.wait_send()

  got = pl.pallas_call(
      body, name=name, in_specs=[ANY], out_specs=ANY,
      out_shape=jax.ShapeDtypeStruct((4, 2, r, c_), wl.dtype),
      scratch_shapes=[pltpu.SemaphoreType.DMA((6,)), pltpu.SemaphoreType.DMA((6,))],
  )(wl)
  chip = 2 * lax.axis_index("x") + lax.axis_index("y")
  return lax.dynamic_update_slice(got, wl[None], (chip, 0, 0, 0))


def _relayed_gather_phase(phase, w_ref, g_ref, send_sems, recv_sems):
  h = w_ref.shape[1] // 2
  x, y, c = _my_place()
  sibling = (x, y, 1 - c)
  xn, yn, dg = _other_chips(x, y)

  def copy(k, src, dst, to):
    return pltpu.make_async_remote_copy(src_ref=src, dst_ref=dst, send_sem=send_sems.at[k], recv_sem=recv_sems.at[k],
                                        device_id=to, device_id_type=MESH_ID)

  def block(chip, half):
    return g_ref.at[2 * chip[0] + chip[1], half]

  def same(k, ref, to):
    return copy(k, ref, ref, to)

  top, bottom = pl.ds(0, h), pl.ds(h, h)
  sends = [copy(0, w_ref.at[c], block((x, y), c), (*xn, c)), copy(1, w_ref.at[c], block((x, y), c), (*yn, c)),
           same(2, block(xn, c).at[top], (*yn, c)), same(3, block(yn, c).at[bottom], (*xn, c)),
           same(4, block(xn, c), sibling), same(5, block(yn, c), sibling), same(6, block(dg, c), sibling)]
  if phase == 0:
    sends[0].start()
    sends[1].start()
  elif phase == 1:
    same(0, block(xn, c), (*xn, c)).wait_recv()
    sends[2].start()
    sends[4].start()
    same(1, block(yn, c), (*yn, c)).wait_recv()
    sends[3].start()
    sends[5].start()
  else:
    same(2, block(dg, c).at[top], (*yn, c)).wait_recv()
    same(3, block(dg, c).at[bottom], (*xn, c)).wait_recv()
    sends[6].start()
    for k, chip in ((4, xn), (5, yn), (6, dg)):
      same(k, block(chip, 1 - c), sibling).wait_recv()
    for cp in sends:
      cp.wait_send()


N_RELAY_COPIES = 7


def _place_own_block(got, wl):
  chip = 2 * lax.axis_index("x") + lax.axis_index("y")
  return lax.dynamic_update_slice(got, wl[None], (chip, 0, 0, 0))


def _all_gather_relayed(wl, name):
  _, r, c_ = wl.shape
  assert (r // 2) % ROW_ALIGN == 0

  def body(w_ref, g_ref, send_sems, recv_sems):
    for phase in range(3):
      _relayed_gather_phase(phase, w_ref, g_ref, send_sems, recv_sems)

  got = pl.pallas_call(
      body, name=name, in_specs=[ANY], out_specs=ANY,
      out_shape=jax.ShapeDtypeStruct((4, 2, r, c_), wl.dtype),
      scratch_shapes=[pltpu.SemaphoreType.DMA((N_RELAY_COPIES,)), pltpu.SemaphoreType.DMA((N_RELAY_COPIES,))],
  )(wl)
  return _place_own_block(got, wl)


def _sibling_swap(v, name):
  def body(v_ref, o_ref, send_sem, recv_sem):
    x, y, c = _my_place()
    cp = pltpu.make_async_remote_copy(src_ref=v_ref, dst_ref=o_ref, send_sem=send_sem, recv_sem=recv_sem,
                                      device_id=(x, y, 1 - c), device_id_type=MESH_ID)
    cp.start()
    cp.wait()

  return pl.pallas_call(
      body, name=name, in_specs=[ANY], out_specs=ANY, out_shape=jax.ShapeDtypeStruct(v.shape, v.dtype),
      scratch_shapes=[pltpu.SemaphoreType.DMA, pltpu.SemaphoreType.DMA],
  )(v)


def _chip_exchange_copies(q_ref, o_ref, send_sems, recv_sems):
  x, y, c = _my_place()
  return [pltpu.make_async_remote_copy(src_ref=q_ref.at[2 * px + py], dst_ref=o_ref.at[k], send_sem=send_sems.at[k],
                                       recv_sem=recv_sems.at[k], device_id=(px, py, c), device_id_type=MESH_ID)
          for k, (px, py) in enumerate(_other_chips(x, y))]


def _chip_exchange(q, name):
  _, r, c_ = q.shape

  def body(q_ref, o_ref, send_sems, recv_sems):
    cps = _chip_exchange_copies(q_ref, o_ref, send_sems, recv_sems)
    for cp in cps:
      cp.start()
    for cp in cps:
      cp.wait()

  return pl.pallas_call(
      body, name=name, in_specs=[ANY], out_specs=ANY, out_shape=jax.ShapeDtypeStruct((3, r, c_), q.dtype),
      scratch_shapes=[pltpu.SemaphoreType.DMA((3,)), pltpu.SemaphoreType.DMA((3,))],
  )(q)


def _share_halves(both, name):
  _, r, c_ = both.shape

  def body(b_ref, o_ref, send_sem, recv_sem):
    x, y, c = _my_place()
    cp = pltpu.make_async_remote_copy(src_ref=b_ref.at[c], dst_ref=o_ref.at[c], send_sem=send_sem, recv_sem=recv_sem,
                                      device_id=(x, y, 1 - c), device_id_type=MESH_ID)
    cp.start()
    cp.wait()

  full = pl.pallas_call(
      body, name=name, in_specs=[ANY], out_specs=ANY, out_shape=jax.ShapeDtypeStruct(both.shape, both.dtype),
      input_output_aliases={0: 0},
      scratch_shapes=[pltpu.SemaphoreType.DMA, pltpu.SemaphoreType.DMA],
  )(both)
  return full.reshape(2 * r, c_)


def _half_spec(tr, c_, pick):
  return pl.BlockSpec((None, None, tr, c_), lambda s, r, place: (s, pick(place), r, 0))


def _cast_other_half(p, place, name, tr=512):
  _, _, r, c_ = p.shape
  tr = _tile(r, tr, 16)

  def body(place_ref, p_ref, o_ref):
    o_ref[...] = _bf(p_ref[...])

  out_spec = pl.BlockSpec((None, tr, c_), lambda s, rr, place: (s, rr, 0))
  return pl.pallas_call(
      body, name=name, out_shape=jax.ShapeDtypeStruct((4, r, c_), BF16),
      grid_spec=pltpu.PrefetchScalarGridSpec(num_scalar_prefetch=1, grid=(4, r // tr),
                                             in_specs=[_half_spec(tr, c_, lambda place: 1 - place[1])], out_specs=out_spec),
      compiler_params=_params(("parallel", "parallel")),
  )(place, p)


def _add_sibling(p, x1, place, name, tr=512):
  _, _, r, c_ = p.shape
  tr = _tile(r, tr, 16)

  def body(place_ref, p_ref, x_ref, o_ref):
    o_ref[...] = _bf(p_ref[...] + x_ref[...].astype(F32))

  row = pl.BlockSpec((None, tr, c_), lambda s, rr, place: (s, rr, 0))
  return pl.pallas_call(
      body, name=name, out_shape=jax.ShapeDtypeStruct((4, r, c_), BF16),
      grid_spec=pltpu.PrefetchScalarGridSpec(num_scalar_prefetch=1, grid=(4, r // tr),
                                             in_specs=[_half_spec(tr, c_, lambda place: place[1]), row], out_specs=row),
      compiler_params=_params(("parallel", "parallel")),
  )(place, p, x1)


def _sum_own(p, x1, x3, place, name, tr=512):
  _, _, r, c_ = p.shape
  tr = _tile(r, tr, 16)

  def body(place_ref, p_ref, x1_ref, x3_ref, o_ref):
    acc = p_ref[...] + x1_ref[...].astype(F32)
    for k in range(3):
      acc = acc + x3_ref[k].astype(F32)
    o_ref[...] = acc

  return pl.pallas_call(
      body, name=name, out_shape=jax.ShapeDtypeStruct((2, r, c_), F32),
      grid_spec=pltpu.PrefetchScalarGridSpec(
          num_scalar_prefetch=1, grid=(r // tr,),
          in_specs=[pl.BlockSpec((None, None, tr, c_), lambda rr, place: (place[0], place[1], rr, 0)),
                    pl.BlockSpec((None, tr, c_), lambda rr, place: (place[0], rr, 0)),
                    pl.BlockSpec((3, tr, c_), lambda rr, place: (0, rr, 0))],
          out_specs=pl.BlockSpec((None, tr, c_), lambda rr, place: (place[1], rr, 0))),
      compiler_params=_params(("parallel",)),
  )(place, p, x1, x3)


def _reduce_scatter_begin(p, tag):
  x, y, c = _my_place()
  place = jnp.stack([2 * x + y, c]).astype(jnp.int32)
  x1 = _sibling_swap(_cast_other_half(p, place, f"rs_cast_other_half_{tag}"), f"rs_sibling_swap_{tag}")
  return _add_sibling(p, x1, place, f"rs_add_sibling_{tag}"), (p, x1, place)


def _reduce_scatter_end(state, x3, tag):
  p, x1, place = state
  return _share_halves(_sum_own(p, x1, x3, place, f"rs_sum_own_{tag}"), f"rs_share_halves_{tag}")


def _adamw(w, g, m, v, name):
  shape = w.shape
  cols = shape[-1]
  rows = w.size // cols
  tr = _tile(rows, max(8, (256 * 1024) // cols // 8 * 8), 8)

  def body(w_ref, g_ref, m_ref, v_ref, d_ref, nm_ref, nv_ref):
    gv = g_ref[...]
    nm = ADAM_B1 * m_ref[...] + (1.0 - ADAM_B1) * gv
    nv = ADAM_B2 * v_ref[...] + (1.0 - ADAM_B2) * (gv * gv)
    m_hat = nm / (1.0 - ADAM_B1 ** ADAM_STEP)
    v_hat = nv / (1.0 - ADAM_B2 ** ADAM_STEP)
    d_ref[...] = -ADAM_LR * (m_hat / (jnp.sqrt(v_hat) + ADAM_EPS) + ADAM_WD * w_ref[...])
    nm_ref[...] = nm
    nv_ref[...] = nv

  spec = pl.BlockSpec((tr, cols), lambda i: (i, 0))
  sds = jax.ShapeDtypeStruct((rows, cols), F32)
  outs = pl.pallas_call(
      body, name=name, grid=(rows // tr,), in_specs=[spec] * 4, out_specs=[spec] * 3, out_shape=[sds] * 3,
      compiler_params=_params(("parallel",)),
  )(*(t.reshape(rows, cols) for t in (w, g, m, v)))
  return tuple(o.reshape(shape) for o in outs)


SHARDED = (("a_w_in", True), ("b_w_in", True), ("w_mem_kv", False), ("w_out", False), ("w_gate", True),
           ("w_up", True), ("w_down", False))
SMALL_SHARDED = (("sgu_ln_g", 1), ("sgu_ln_b", 1))
REPLICATED = ("sgu_w_s", "sgu_b_s", "ln_mix_g", "ln_mix_b", "ln_ffn_g", "ln_ffn_b")
SMALL_ORDER = ("sgu_w_s", "sgu_b_s", "ln_mix_g", "ln_mix_b", "ln_ffn_g", "ln_ffn_b", "sgu_ln_g", "sgu_ln_b")
ROW_ALIGN = 16


def _pad_to(v, n):
  return jnp.pad(v, (0, n - v.shape[0]))


def _round_up(n, a):
  return -(-n // a) * a


def _exchange_form(t, transposed):
  return jnp.swapaxes(t, 1, 2) if transposed else t


def _to_shard_major(full, axis):
  shp = full.shape
  cut = shp[:axis] + (4, shp[axis] // 4) + shp[axis + 1:]
  return jnp.moveaxis(full.reshape(cut), axis, 0).reshape(4, -1, FLAT_COLS)


def _from_shard_major(rows, shard_shape, axis):
  full = jnp.moveaxis(rows.reshape((4,) + tuple(shard_shape)), 0, axis)
  shp = full.shape
  return full.reshape(shp[:axis] + (shp[axis] * shp[axis + 1],) + shp[axis + 2:])


class _WeightPack:
  def __init__(self, items, small=()):
    segs, self.rows, self.small, off = [], {}, [], 0
    for n, l, b in items:
      seg = b.reshape(-1, FLAT_COLS)
      self.rows[(n, l)] = (off, seg.shape[0], b.shape)
      segs.append(seg)
      off += seg.shape[0]
    if small:
      flat = jnp.concatenate([lax.bitcast_convert_type(v, BF16).reshape(-1) for _, v in small])
      rows = _round_up(flat.shape[0], ROW_ALIGN * FLAT_COLS) // FLAT_COLS
      self.small = [(n, v.shape) for n, v in small]
      self.small_rows = (off, rows)
      segs.append(_pad_to(flat, rows * FLAT_COLS).reshape(rows, FLAT_COLS))
      off += rows
    rows_pad = _round_up(off, 4 * ROW_ALIGN)
    if rows_pad > off:
      segs.append(jnp.zeros((rows_pad - off, FLAT_COLS), BF16))
    self.flat = jnp.concatenate(segs).reshape(2, rows_pad // 2, FLAT_COLS)

  def unpack(self, gathered):
    g = gathered.reshape(4, -1, FLAT_COLS)
    out = {}
    for (n, l), (off, nr, shape) in self.rows.items():
      out.setdefault(n, {})[l] = g[:, off:off + nr].reshape((4 * shape[0],) + shape[1:])
    if self.small:
      off, rows = self.small_rows
      flat = g[:, off:off + rows].reshape(4, rows * FLAT_COLS)
      pos = 0
      for n, shape in self.small:
        sz = 2 * math.prod(shape)
        vals = lax.bitcast_convert_type(flat[:, pos:pos + sz].reshape((4,) + shape + (2,)), F32)
        out[n] = _from_shard_major(vals, shape, len(shape) - 1)
        pos += sz
    return out


FIRST_WEIGHTS = (("a_w_in", 0), ("w_mem_kv", 0))


def _weight_packs(shards):
  blocks = {(n, l): _exchange_form(shards[n], tr)[l].astype(BF16)
            for n, tr in SHARDED for l in range(shards[n].shape[0])}
  first = _WeightPack([(n, l, blocks[(n, l)]) for n, l in FIRST_WEIGHTS],
                      small=[(n, shards[n]) for n, _ in SMALL_SHARDED])

  def model_layer(n, l):
    return {"a_w_in": 2 * l, "b_w_in": 2 * l + 1}.get(n, l)

  def in_second(n, l):
    return model_layer(n, l) == 3 or (model_layer(n, l) == 2 and n not in ("a_w_in", "w_mem_kv"))

  rest = [(n, l, b) for (n, l), b in blocks.items() if (n, l) not in FIRST_WEIGHTS]
  return first, {0: _WeightPack([it for it in rest if not in_second(it[0], it[1])]),
                 2: _WeightPack([it for it in rest if in_second(it[0], it[1])])}


def _reduce_grads(grads, shard_shapes):
  early, state, x3 = grads.pop("early_exchange")
  mine_early = _reduce_scatter_end(state, x3, "early")
  late = _GradPack([(n, l, g) for n, _ in SHARDED for l, g in enumerate(grads[n]) if (n, l) not in early.rows])
  q, state = _reduce_scatter_begin(late.p, "late")
  mine_late = _reduce_scatter_end(state, _chip_exchange(q, "rs_chip_exchange_late"), "late")
  out = {}
  for n, tr in SHARDED:
    layers, rows, cols = shard_shapes[n]
    blocks = []
    for l in range(layers):
      pack, mine = (early, mine_early) if (n, l) in early.rows else (late, mine_late)
      off, nr = pack.rows[(n, l)]
      block = mine[off:off + nr]
      blocks.append(block.reshape(cols, rows).T if tr else block.reshape(rows, cols))
    out[n] = jnp.stack(blocks)
  off, quarter_rows = early.rows["small"]
  piece = mine_early[off:off + quarter_rows].reshape(2, quarter_rows // 2, FLAT_COLS)
  small_sum = _all_gather_halves(piece, "gather_small_grads").reshape(-1)
  off = 0
  for n in SMALL_ORDER:
    shape = (len(grads[n]),) + grads[n][0].shape
    sz = math.prod(shape)
    out[n] = small_sum[off:off + sz].reshape(shape)
    off += sz
  return out


class _GradPack:
  def __init__(self, items, small=None):
    segs, self.rows, off = [], {}, 0
    for n, l, g in items:
      seg = _to_shard_major(g, 0)
      self.rows[(n, l)] = (off, seg.shape[1])
      segs.append(seg)
      off += seg.shape[1]
    if small is not None:
      flat = jnp.concatenate([jnp.stack(small[n]).reshape(-1) for n in SMALL_ORDER])
      n_small = _round_up(flat.shape[0], 4 * 2 * 8 * FLAT_COLS)
      quarter_rows = n_small // (4 * FLAT_COLS)
      self.rows["small"] = (off, quarter_rows)
      segs.append(_pad_to(flat, n_small).reshape(4, quarter_rows, FLAT_COLS))
      off += quarter_rows
    rows_pad = _round_up(off, 2 * ROW_ALIGN)
    if rows_pad > off:
      segs.append(jnp.zeros((4, rows_pad - off, FLAT_COLS), F32))
    self.p = jnp.concatenate(segs, axis=1).reshape(4, 2, rows_pad // 2, FLAT_COLS)


def _early_exchange_begin(grads):
  items = [(n, l, g) for n, _ in SHARDED for l, g in enumerate(grads[n]) if g is not None]
  pack = _GradPack(items, small={n: grads[n] for n in SMALL_ORDER})
  q, state = _reduce_scatter_begin(pack.p, "early")
  return q, (pack, state)


WEIGHT_NAMES = ("a_w_in", "b_w_in", "sgu_ln_g", "sgu_ln_b", "sgu_w_s", "sgu_b_s", "w_mem_kv", "w_out",
                "ln_mix_g", "ln_mix_b", "w_gate", "w_up", "w_down", "ln_ffn_g", "ln_ffn_b")


def kernel(x, mem, a_w_in, b_w_in, sgu_ln_g, sgu_ln_b, sgu_w_s, sgu_b_s, w_mem_kv, w_out, ln_mix_g, ln_mix_b, w_gate, w_up, w_down, ln_ffn_g, ln_ffn_b, loss_target, m_a_w_in, m_b_w_in, m_sgu_ln_g, m_sgu_ln_b, m_sgu_w_s, m_sgu_b_s, m_w_mem_kv, m_w_out, m_ln_mix_g, m_ln_mix_b, m_w_gate, m_w_up, m_w_down, m_ln_ffn_g, m_ln_ffn_b, v_a_w_in, v_b_w_in, v_sgu_ln_g, v_sgu_ln_b, v_sgu_w_s, v_sgu_b_s, v_w_mem_kv, v_w_out, v_ln_mix_g, v_ln_mix_b, v_w_gate, v_w_up, v_w_down, v_ln_ffn_g, v_ln_ffn_b):
  weights = dict(a_w_in=a_w_in, b_w_in=b_w_in, sgu_ln_g=sgu_ln_g, sgu_ln_b=sgu_ln_b, sgu_w_s=sgu_w_s, sgu_b_s=sgu_b_s,
                 w_mem_kv=w_mem_kv, w_out=w_out, ln_mix_g=ln_mix_g, ln_mix_b=ln_mix_b, w_gate=w_gate, w_up=w_up,
                 w_down=w_down, ln_ffn_g=ln_ffn_g, ln_ffn_b=ln_ffn_b)
  mom1 = dict(a_w_in=m_a_w_in, b_w_in=m_b_w_in, sgu_ln_g=m_sgu_ln_g, sgu_ln_b=m_sgu_ln_b, sgu_w_s=m_sgu_w_s,
              sgu_b_s=m_sgu_b_s, w_mem_kv=m_w_mem_kv, w_out=m_w_out, ln_mix_g=m_ln_mix_g, ln_mix_b=m_ln_mix_b,
              w_gate=m_w_gate, w_up=m_w_up, w_down=m_w_down, ln_ffn_g=m_ln_ffn_g, ln_ffn_b=m_ln_ffn_b)
  mom2 = dict(a_w_in=v_a_w_in, b_w_in=v_b_w_in, sgu_ln_g=v_sgu_ln_g, sgu_ln_b=v_sgu_ln_b, sgu_w_s=v_sgu_w_s,
              sgu_b_s=v_sgu_b_s, w_mem_kv=v_w_mem_kv, w_out=v_w_out, ln_mix_g=v_ln_mix_g, ln_mix_b=v_ln_mix_b,
              w_gate=v_w_gate, w_up=v_w_up, w_down=v_w_down, ln_ffn_g=v_ln_ffn_g, ln_ffn_b=v_ln_ffn_b)

  first, late = _weight_packs(weights)
  full = first.unpack(_all_gather_relayed(first.flat, "gather_first_weights"))
  for n in REPLICATED:
    full[n] = weights[n]
  loss_part, grad_x, grads = _local_step(x, mem, loss_target, full, late_weights=late,
                                         early_exchange=_early_exchange_begin)
  loss = lax.psum(loss_part, MESH_AXES)

  shard_shapes = {n: weights[n].shape for n, _ in SHARDED}
  red = _reduce_grads(grads, shard_shapes)
  chip = 2 * lax.axis_index("x") + lax.axis_index("y")
  for n, axis in SMALL_SHARDED:
    width = weights[n].shape[axis]
    red[n] = lax.dynamic_slice_in_dim(red[n], chip * width, width, axis)

  small_names = SMALL_ORDER
  def pack(d):
    flat = jnp.concatenate([d[n].reshape(-1) for n in small_names])
    return _pad_to(flat, _round_up(flat.shape[0], 8 * FLAT_COLS)).reshape(-1, FLAT_COLS)
  small_out = _adamw(pack(weights), pack(red), pack(mom1), pack(mom2), "adamw_small")
  delta, new_m, new_v = {}, {}, {}
  off = 0
  for n in small_names:
    sz = weights[n].size
    for dst, src in zip((delta, new_m, new_v), small_out):
      dst[n] = src.reshape(-1)[off:off + sz].reshape(weights[n].shape)
    off += sz
  for n, _ in SHARDED:
    delta[n], new_m[n], new_v[n] = _adamw(weights[n], red[n], mom1[n], mom2[n], f"adamw_{n}")

  return (loss, grad_x, *[red[n] for n in WEIGHT_NAMES], *[delta[n] for n in WEIGHT_NAMES],
          *[new_m[n] for n in WEIGHT_NAMES], *[new_v[n] for n in WEIGHT_NAMES])
```
